```python
import math
import jax, jax.numpy as jnp
from jax import lax
import numpy as np

D_MODEL = 2048
BATCH = 8
SEQ = 2048
DEPTH = 1

GRID_W = 64
NA_HEADS = 8
NA_HEAD_DIM = 128
NA_WIN_ROWS = 8
NA_WIN_COLS = 16
NA_WIDTH = NA_HEADS * NA_HEAD_DIM
MLA_HEADS = 8
MLA_Q_RANK = 512
MLA_KV_RANK = 512
MLA_NOPE_DIM = 128
MLA_ROPE_DIM = 64
MLA_V_DIM = 128
MLA_QK_DIM = MLA_NOPE_DIM + MLA_ROPE_DIM
MLA_WIDTH = MLA_HEADS * MLA_V_DIM
ROPE_THETA = 10000.0
Q_BLOCK = 128
D_FF = 5632
PL_DIM = 256
NORM_EPS = 1e-6
NEG_INF = -1e30
IN_SIZES = (NA_WIDTH, NA_WIDTH, NA_WIDTH, MLA_Q_RANK, MLA_KV_RANK, MLA_ROPE_DIM, D_MODEL, D_MODEL)
N_IN = NA_WIDTH * 3 + MLA_Q_RANK + MLA_KV_RANK + MLA_ROPE_DIM + 2 * D_MODEL

kernel_name = "hybrid_na2d_mla_macaron_encoder"


def rmsnorm(x, g):
    xf = x.astype(jnp.float32)
    y = xf * lax.rsqrt(jnp.mean(xf * xf, axis=-1, keepdims=True) + NORM_EPS)
    return (y * g.astype(jnp.float32)).astype(x.dtype)


def swiglu(x, w_gate, w_up, w_down):
    return (jax.nn.silu(x @ w_gate) * (x @ w_up)) @ w_down


def split_points():
    pts, acc = [], 0
    for s in IN_SIZES[:-1]:
        acc += s
        pts.append(acc)
    return pts


def rope_tables(seq_len, dim):
    pos = jnp.arange(seq_len, dtype=jnp.float32)
    inv_freq = 1.0 / (ROPE_THETA ** (jnp.arange(0, dim, 2, dtype=jnp.float32) / dim))
    ang = pos[:, None] * inv_freq[None, :]
    return jnp.cos(ang), jnp.sin(ang)


def apply_rope(x, cos, sin):
    half = x.shape[-1] // 2
    x1, x2 = x[..., :half], x[..., half:]
    cos = cos.astype(x.dtype)
    sin = sin.astype(x.dtype)
    return jnp.concatenate([x1 * cos - x2 * sin, x2 * cos + x1 * sin], axis=-1)


def neighbourhood_attention(q, k, v, rpb):
    B, S, _ = q.shape
    rows = S // GRID_W
    kh = min(NA_WIN_ROWS, rows)
    kw = NA_WIN_COLS
    grid = lambda t: t.reshape(B, rows, GRID_W, NA_HEADS, NA_HEAD_DIM)
    qg, kg, vg = grid(q), grid(k), grid(v)
    cols = np.arange(GRID_W)
    col_start = np.clip(cols - kw // 2, 0, GRID_W - kw)
    col_mask = (cols[None, :] >= col_start[:, None]) & (cols[None, :] < col_start[:, None] + kw)
    dc_idx = np.clip(cols[None, :] - cols[:, None], -(kw - 1), kw - 1) + (kw - 1)
    col_mask = jnp.asarray(col_mask)[:, None, :]
    dc_idx = jnp.asarray(dc_idx)
    scale = NA_HEAD_DIM ** -0.5

    def row_block(args):
        q_row, r = args
        rs = jnp.clip(r - kh // 2, 0, rows - kh)
        k_rows = lax.dynamic_slice_in_dim(kg, rs, kh, axis=1)
        v_rows = lax.dynamic_slice_in_dim(vg, rs, kh, axis=1)
        dr = rs + jnp.arange(kh) - r
        bias = rpb[:, dr + (NA_WIN_ROWS - 1)][:, :, dc_idx]
        bias = bias.transpose(0, 2, 1, 3).astype(jnp.float32)
        s = jnp.einsum('bqhd,bikhd->bhqik', q_row, k_rows).astype(jnp.float32) * scale + bias
        s = jnp.where(col_mask, s, NEG_INF)
        pr = jax.nn.softmax(s.reshape(B, NA_HEADS, GRID_W, kh * GRID_W), axis=-1)
        pr = pr.reshape(s.shape).astype(v.dtype)
        return jnp.einsum('bhqik,bikhd->bqhd', pr, v_rows)

    o = lax.map(row_block, (qg.swapaxes(0, 1), jnp.arange(rows)))
    return o.swapaxes(0, 1).reshape(B, S, NA_WIDTH)


def mla_attention(q_lat, kv_lat, k_rope_in, q_a_norm, w_uq, kv_a_norm, w_ukv):
    B, S, _ = q_lat.shape
    cq = rmsnorm(q_lat, q_a_norm)
    q = (cq @ w_uq).reshape(B, S, MLA_HEADS, MLA_QK_DIM)
    q_nope, q_rope = q[..., :MLA_NOPE_DIM], q[..., MLA_NOPE_DIM:]
    ckv = rmsnorm(kv_lat, kv_a_norm)
    kv = (ckv @ w_ukv).reshape(B, S, MLA_HEADS, MLA_NOPE_DIM + MLA_V_DIM)
    k_nope, v = kv[..., :MLA_NOPE_DIM], kv[..., MLA_NOPE_DIM:]
    cos, sin = rope_tables(S, MLA_ROPE_DIM)
    q_rope = apply_rope(q_rope, cos[:, None, :], sin[:, None, :])
    k_rope = apply_rope(k_rope_in, cos, sin)
    scale = MLA_QK_DIM ** -0.5
    nb = S // Q_BLOCK
    to_blocks = lambda t: t.reshape(B, nb, Q_BLOCK, *t.shape[2:]).swapaxes(0, 1)

    def q_block(args):
        qn, qr = args
        s = jnp.einsum('bqhd,bkhd->bhqk', qn, k_nope) + jnp.einsum('bqhr,bkr->bhqk', qr, k_rope)
        pr = jax.nn.softmax(s.astype(jnp.float32) * scale, axis=-1).astype(v.dtype)
        return jnp.einsum('bhqk,bkhd->bqhd', pr, v)

    o = lax.map(q_block, (to_blocks(q_nope), to_blocks(q_rope)))
    return o.swapaxes(0, 1).reshape(B, S, MLA_WIDTH)


def _fwd_setup_inputs(seed: int = 0) -> dict:
    key = jax.random.key(seed)
    ks = iter(jax.random.split(key, 32))
    f32 = jnp.float32
    w = lambda shape, fan_in: jax.random.normal(next(ks), shape, f32) * (fan_in ** -0.5)
    gain = lambda shape: 1.0 + 0.01 * jax.random.normal(next(ks), shape, f32)
    L = DEPTH
    return {
        "x": jax.random.normal(next(ks), (BATCH, SEQ, D_MODEL), f32),
        "p": jax.random.normal(next(ks), (DEPTH, BATCH, SEQ, PL_DIM), f32),
        "ffn1_norm": gain((L, D_MODEL)),
        "ffn1_w_gate": w((L, D_MODEL, D_FF), D_MODEL),
        "ffn1_w_up": w((L, D_MODEL, D_FF), D_MODEL),
        "ffn1_w_down": w((L, D_FF, D_MODEL), D_FF),
        "mix_norm": gain((L, D_MODEL)),
        "w_in": w((L, D_MODEL, N_IN), D_MODEL),
        "q_a_norm": gain((L, MLA_Q_RANK)),
        "w_uq": w((L, MLA_Q_RANK, MLA_HEADS * MLA_QK_DIM), MLA_Q_RANK),
        "kv_a_norm": gain((L, MLA_KV_RANK)),
        "w_ukv": w((L, MLA_KV_RANK, MLA_HEADS * (MLA_NOPE_DIM + MLA_V_DIM)), MLA_KV_RANK),
        "na_rpb": 0.02 * jax.random.normal(next(ks), (L, NA_HEADS, 2 * NA_WIN_ROWS - 1, 2 * NA_WIN_COLS - 1), f32),
        "w_branch_a": w((L, NA_WIDTH, D_MODEL), NA_WIDTH),
        "w_branch_b": w((L, MLA_WIDTH, D_MODEL), MLA_WIDTH),
        "w_out": w((L, D_MODEL, D_MODEL), D_MODEL),
        "ffn2_norm": gain((L, D_MODEL)),
        "ffn2_w_gate": w((L, D_MODEL, D_FF), D_MODEL),
        "ffn2_w_up": w((L, D_MODEL, D_FF), D_MODEL),
        "ffn2_w_down": w((L, D_FF, D_MODEL), D_FF),
        "pl_norm": gain((L, D_MODEL)),
        "w_pl": w((L, PL_DIM, D_MODEL), PL_DIM),
        "w_pl_gate": w((L, D_MODEL, D_MODEL), D_MODEL),
        "final_norm": gain((D_MODEL,)),
    }


def _fwd_reference(x, p, ffn1_norm, ffn1_w_gate, ffn1_w_up, ffn1_w_down, mix_norm, w_in,
              q_a_norm, w_uq, kv_a_norm, w_ukv, na_rpb, w_branch_a, w_branch_b, w_out,
              ffn2_norm, ffn2_w_gate, ffn2_w_up, ffn2_w_down, pl_norm, w_pl, w_pl_gate,
              final_norm):
    pts = split_points()
    h = x
    for i in range(DEPTH):
        h = h + 0.5 * swiglu(rmsnorm(h, ffn1_norm[i]), ffn1_w_gate[i], ffn1_w_up[i], ffn1_w_down[i])
        u = rmsnorm(h, mix_norm[i])
        z = u @ w_in[i]
        na_q, na_k, na_v, q_lat, kv_lat, k_rope, gate_a, gate_b = jnp.split(z, pts, axis=-1)
        y_a = neighbourhood_attention(na_q, na_k, na_v, na_rpb[i]) @ w_branch_a[i]
        y_b = mla_attention(q_lat, kv_lat, k_rope, q_a_norm[i], w_uq[i], kv_a_norm[i], w_ukv[i]) @ w_branch_b[i]
        merged = jax.nn.sigmoid(gate_a) * y_a + jax.nn.sigmoid(gate_b) * y_b
        h = h + merged @ w_out[i]
        h = h + 0.5 * swiglu(rmsnorm(h, ffn2_norm[i]), ffn2_w_gate[i], ffn2_w_up[i], ffn2_w_down[i])
        pl_gate = jax.nn.sigmoid(rmsnorm(h, pl_norm[i]) @ w_pl_gate[i])
        h = h + pl_gate * (p[i] @ w_pl[i])
    return rmsnorm(h, final_norm)


import jax as _jax
import jax.numpy as _jnp

TWIN_FORMAT = 'train_step'
FWD_PARAMS = ['x', 'p', 'ffn1_norm', 'ffn1_w_gate', 'ffn1_w_up', 'ffn1_w_down', 'mix_norm', 'w_in', 'q_a_norm', 'w_uq', 'kv_a_norm', 'w_ukv', 'na_rpb', 'w_branch_a', 'w_branch_b', 'w_out', 'ffn2_norm', 'ffn2_w_gate', 'ffn2_w_up', 'ffn2_w_down', 'pl_norm', 'w_pl', 'w_pl_gate', 'final_norm']
TWIN_WEIGHTS = ['ffn1_norm', 'ffn1_w_gate', 'ffn1_w_up', 'ffn1_w_down', 'mix_norm', 'w_in', 'q_a_norm', 'w_uq', 'kv_a_norm', 'w_ukv', 'na_rpb', 'w_branch_a', 'w_branch_b', 'w_out', 'ffn2_norm', 'ffn2_w_gate', 'ffn2_w_up', 'ffn2_w_down', 'pl_norm', 'w_pl', 'w_pl_gate', 'final_norm']
TWIN_DIFF_INPUT = 'x'
TWIN_INPUTS = ['x', 'p', 'ffn1_norm', 'ffn1_w_gate', 'ffn1_w_up', 'ffn1_w_down', 'mix_norm', 'w_in', 'q_a_norm', 'w_uq', 'kv_a_norm', 'w_ukv', 'na_rpb', 'w_branch_a', 'w_branch_b', 'w_out', 'ffn2_norm', 'ffn2_w_gate', 'ffn2_w_up', 'ffn2_w_down', 'pl_norm', 'w_pl', 'w_pl_gate', 'final_norm', 'loss_target', 'm_ffn1_norm', 'm_ffn1_w_gate', 'm_ffn1_w_up', 'm_ffn1_w_down', 'm_mix_norm', 'm_w_in', 'm_q_a_norm', 'm_w_uq', 'm_kv_a_norm', 'm_w_ukv', 'm_na_rpb', 'm_w_branch_a', 'm_w_branch_b', 'm_w_out', 'm_ffn2_norm', 'm_ffn2_w_gate', 'm_ffn2_w_up', 'm_ffn2_w_down', 'm_pl_norm', 'm_w_pl', 'm_w_pl_gate', 'm_final_norm', 'v_ffn1_norm', 'v_ffn1_w_gate', 'v_ffn1_w_up', 'v_ffn1_w_down', 'v_mix_norm', 'v_w_in', 'v_q_a_norm', 'v_w_uq', 'v_kv_a_norm', 'v_w_ukv', 'v_na_rpb', 'v_w_branch_a', 'v_w_branch_b', 'v_w_out', 'v_ffn2_norm', 'v_ffn2_w_gate', 'v_ffn2_w_up', 'v_ffn2_w_down', 'v_pl_norm', 'v_w_pl', 'v_w_pl_gate', 'v_final_norm']
TWIN_OUTPUTS = ['loss', 'grad_x', 'grad_ffn1_norm', 'grad_ffn1_w_gate', 'grad_ffn1_w_up', 'grad_ffn1_w_down', 'grad_mix_norm', 'grad_w_in', 'grad_q_a_norm', 'grad_w_uq', 'grad_kv_a_norm', 'grad_w_ukv', 'grad_na_rpb', 'grad_w_branch_a', 'grad_w_branch_b', 'grad_w_out', 'grad_ffn2_norm', 'grad_ffn2_w_gate', 'grad_ffn2_w_up', 'grad_ffn2_w_down', 'grad_pl_norm', 'grad_w_pl', 'grad_w_pl_gate', 'grad_final_norm', 'delta_ffn1_norm', 'delta_ffn1_w_gate', 'delta_ffn1_w_up', 'delta_ffn1_w_down', 'delta_mix_norm', 'delta_w_in', 'delta_q_a_norm', 'delta_w_uq', 'delta_kv_a_norm', 'delta_w_ukv', 'delta_na_rpb', 'delta_w_branch_a', 'delta_w_branch_b', 'delta_w_out', 'delta_ffn2_norm', 'delta_ffn2_w_gate', 'delta_ffn2_w_up', 'delta_ffn2_w_down', 'delta_pl_norm', 'delta_w_pl', 'delta_w_pl_gate', 'delta_final_norm', 'new_m_ffn1_norm', 'new_m_ffn1_w_gate', 'new_m_ffn1_w_up', 'new_m_ffn1_w_down', 'new_m_mix_norm', 'new_m_w_in', 'new_m_q_a_norm', 'new_m_w_uq', 'new_m_kv_a_norm', 'new_m_w_ukv', 'new_m_na_rpb', 'new_m_w_branch_a', 'new_m_w_branch_b', 'new_m_w_out', 'new_m_ffn2_norm', 'new_m_ffn2_w_gate', 'new_m_ffn2_w_up', 'new_m_ffn2_w_down', 'new_m_pl_norm', 'new_m_w_pl', 'new_m_w_pl_gate', 'new_m_final_norm', 'new_v_ffn1_norm', 'new_v_ffn1_w_gate', 'new_v_ffn1_w_up', 'new_v_ffn1_w_down', 'new_v_mix_norm', 'new_v_w_in', 'new_v_q_a_norm', 'new_v_w_uq', 'new_v_kv_a_norm', 'new_v_w_ukv', 'new_v_na_rpb', 'new_v_w_branch_a', 'new_v_w_branch_b', 'new_v_w_out', 'new_v_ffn2_norm', 'new_v_ffn2_w_gate', 'new_v_ffn2_w_up', 'new_v_ffn2_w_down', 'new_v_pl_norm', 'new_v_w_pl', 'new_v_w_pl_gate', 'new_v_final_norm']
TWIN_LEAF_KINDS = {'loss': 'loss', 'grad_x': 'grad_x', 'grad_ffn1_norm': 'grad_w', 'grad_ffn1_w_gate': 'grad_w', 'grad_ffn1_w_up': 'grad_w', 'grad_ffn1_w_down': 'grad_w', 'grad_mix_norm': 'grad_w', 'grad_w_in': 'grad_w', 'grad_q_a_norm': 'grad_w', 'grad_w_uq': 'grad_w', 'grad_kv_a_norm': 'grad_w', 'grad_w_ukv': 'grad_w', 'grad_na_rpb': 'grad_w', 'grad_w_branch_a': 'grad_w', 'grad_w_branch_b': 'grad_w', 'grad_w_out': 'grad_w', 'grad_ffn2_norm': 'grad_w', 'grad_ffn2_w_gate': 'grad_w', 'grad_ffn2_w_up': 'grad_w', 'grad_ffn2_w_down': 'grad_w', 'grad_pl_norm': 'grad_w', 'grad_w_pl': 'grad_w', 'grad_w_pl_gate': 'grad_w', 'grad_final_norm': 'grad_w', 'delta_ffn1_norm': 'delta_w', 'delta_ffn1_w_gate': 'delta_w', 'delta_ffn1_w_up': 'delta_w', 'delta_ffn1_w_down': 'delta_w', 'delta_mix_norm': 'delta_w', 'delta_w_in': 'delta_w', 'delta_q_a_norm': 'delta_w', 'delta_w_uq': 'delta_w', 'delta_kv_a_norm': 'delta_w', 'delta_w_ukv': 'delta_w', 'delta_na_rpb': 'delta_w', 'delta_w_branch_a': 'delta_w', 'delta_w_branch_b': 'delta_w', 'delta_w_out': 'delta_w', 'delta_ffn2_norm': 'delta_w', 'delta_ffn2_w_gate': 'delta_w', 'delta_ffn2_w_up': 'delta_w', 'delta_ffn2_w_down': 'delta_w', 'delta_pl_norm': 'delta_w', 'delta_w_pl': 'delta_w', 'delta_w_pl_gate': 'delta_w', 'delta_final_norm': 'delta_w', 'new_m_ffn1_norm': 'new_m', 'new_m_ffn1_w_gate': 'new_m', 'new_m_ffn1_w_up': 'new_m', 'new_m_ffn1_w_down': 'new_m', 'new_m_mix_norm': 'new_m', 'new_m_w_in': 'new_m', 'new_m_q_a_norm': 'new_m', 'new_m_w_uq': 'new_m', 'new_m_kv_a_norm': 'new_m', 'new_m_w_ukv': 'new_m', 'new_m_na_rpb': 'new_m', 'new_m_w_branch_a': 'new_m', 'new_m_w_branch_b': 'new_m', 'new_m_w_out': 'new_m', 'new_m_ffn2_norm': 'new_m', 'new_m_ffn2_w_gate': 'new_m', 'new_m_ffn2_w_up': 'new_m', 'new_m_ffn2_w_down': 'new_m', 'new_m_pl_norm': 'new_m', 'new_m_w_pl': 'new_m', 'new_m_w_pl_gate': 'new_m', 'new_m_final_norm': 'new_m', 'new_v_ffn1_norm': 'new_v', 'new_v_ffn1_w_gate': 'new_v', 'new_v_ffn1_w_up': 'new_v', 'new_v_ffn1_w_down': 'new_v', 'new_v_mix_norm': 'new_v', 'new_v_w_in': 'new_v', 'new_v_q_a_norm': 'new_v', 'new_v_w_uq': 'new_v', 'new_v_kv_a_norm': 'new_v', 'new_v_w_ukv': 'new_v', 'new_v_na_rpb': 'new_v', 'new_v_w_branch_a': 'new_v', 'new_v_w_branch_b': 'new_v', 'new_v_w_out': 'new_v', 'new_v_ffn2_norm': 'new_v', 'new_v_ffn2_w_gate': 'new_v', 'new_v_ffn2_w_up': 'new_v', 'new_v_ffn2_w_down': 'new_v', 'new_v_pl_norm': 'new_v', 'new_v_w_pl': 'new_v', 'new_v_w_pl_gate': 'new_v', 'new_v_final_norm': 'new_v'}


def _forward(args):
    return _fwd_reference(*[args[k] for k in FWD_PARAMS])


def _output_shape():
    out = _jax.eval_shape(lambda: _forward(_fwd_setup_inputs(0)))
    return out.shape, out.dtype

N_MICROBATCH = 1
ADAM_LR = 0.001
ADAM_B1 = 0.9
ADAM_B2 = 0.999
ADAM_EPS = 1e-08
ADAM_WD = 0.01
ADAM_STEP = 10
PER_EXAMPLE_BATCH_AXIS = {'x': 0, 'p': 1, 'loss_target': 0}
SHARED_INPUTS = []
_WEIGHT_DTYPES = {'ffn1_norm': _jnp.float32, 'ffn1_w_gate': _jnp.float32, 'ffn1_w_up': _jnp.float32, 'ffn1_w_down': _jnp.float32, 'mix_norm': _jnp.float32, 'w_in': _jnp.float32, 'q_a_norm': _jnp.float32, 'w_uq': _jnp.float32, 'kv_a_norm': _jnp.float32, 'w_ukv': _jnp.float32, 'na_rpb': _jnp.float32, 'w_branch_a': _jnp.float32, 'w_branch_b': _jnp.float32, 'w_out': _jnp.float32, 'ffn2_norm': _jnp.float32, 'ffn2_w_gate': _jnp.float32, 'ffn2_w_up': _jnp.float32, 'ffn2_w_down': _jnp.float32, 'pl_norm': _jnp.float32, 'w_pl': _jnp.float32, 'w_pl_gate': _jnp.float32, 'final_norm': _jnp.float32}
MOMENT_SCALE = {'ffn1_norm': 2.473439e-02, 'ffn1_w_gate': 1.068546e-02, 'ffn1_w_up': 1.034019e-02, 'ffn1_w_down': 1.715501e-02, 'mix_norm': 1.199110e-02, 'w_in': 5.883776e-03, 'q_a_norm': 6.604991e-03, 'w_uq': 3.788579e-03, 'kv_a_norm': 8.935538e-03, 'w_ukv': 4.323934e-03, 'na_rpb': 4.324021e-03, 'w_branch_a': 6.047335e-03, 'w_branch_b': 3.318470e-03, 'w_out': 6.940047e-03, 'ffn2_norm': 2.292965e-02, 'ffn2_w_gate': 9.761860e-03, 'ffn2_w_up': 9.443878e-03, 'ffn2_w_down': 1.565634e-02, 'pl_norm': 1.107401e-02, 'w_pl': 2.792999e-02, 'w_pl_gate': 1.090416e-02, 'final_norm': 7.994831e+00}


def _to_microbatches(a, axis):
    t = _jnp.moveaxis(a, axis, 0)
    t = t.reshape((N_MICROBATCH, t.shape[0] // N_MICROBATCH) + t.shape[1:])
    return _jnp.moveaxis(t, 1, axis + 1)


def setup_inputs(seed: int = 0) -> dict:
    inp = _fwd_setup_inputs(seed)
    key = _jax.random.fold_in(_jax.random.key(seed), 7919)
    shape, _ = _output_shape()
    out = dict(inp)
    out["loss_target"] = _jax.random.normal(_jax.random.fold_in(key, 0), shape, _jnp.float32)
    for i, name in enumerate(TWIN_WEIGHTS):
        w = inp[name].astype(_jnp.float32)
        if MOMENT_SCALE is None:
            s = _jnp.sqrt(_jnp.mean(_jnp.square(w)) + 1e-30)
        else:
            s = MOMENT_SCALE[name]
        km, kv = _jax.random.split(_jax.random.fold_in(key, i + 1))
        out[name] = w
        out["m_" + name] = s * _jax.random.normal(km, w.shape, _jnp.float32)
        out["v_" + name] = (s * s) * _jax.random.uniform(kv, w.shape, _jnp.float32, 0.5, 1.5)
    if N_MICROBATCH > 1:
        for name, axis in PER_EXAMPLE_BATCH_AXIS.items():
            out[name] = _to_microbatches(out[name], axis)
    return {'x': out['x'], 'p': out['p'], 'ffn1_norm': out['ffn1_norm'], 'ffn1_w_gate': out['ffn1_w_gate'], 'ffn1_w_up': out['ffn1_w_up'], 'ffn1_w_down': out['ffn1_w_down'], 'mix_norm': out['mix_norm'], 'w_in': out['w_in'], 'q_a_norm': out['q_a_norm'], 'w_uq': out['w_uq'], 'kv_a_norm': out['kv_a_norm'], 'w_ukv': out['w_ukv'], 'na_rpb': out['na_rpb'], 'w_branch_a': out['w_branch_a'], 'w_branch_b': out['w_branch_b'], 'w_out': out['w_out'], 'ffn2_norm': out['ffn2_norm'], 'ffn2_w_gate': out['ffn2_w_gate'], 'ffn2_w_up': out['ffn2_w_up'], 'ffn2_w_down': out['ffn2_w_down'], 'pl_norm': out['pl_norm'], 'w_pl': out['w_pl'], 'w_pl_gate': out['w_pl_gate'], 'final_norm': out['final_norm'], 'loss_target': out['loss_target'], 'm_ffn1_norm': out['m_ffn1_norm'], 'm_ffn1_w_gate': out['m_ffn1_w_gate'], 'm_ffn1_w_up': out['m_ffn1_w_up'], 'm_ffn1_w_down': out['m_ffn1_w_down'], 'm_mix_norm': out['m_mix_norm'], 'm_w_in': out['m_w_in'], 'm_q_a_norm': out['m_q_a_norm'], 'm_w_uq': out['m_w_uq'], 'm_kv_a_norm': out['m_kv_a_norm'], 'm_w_ukv': out['m_w_ukv'], 'm_na_rpb': out['m_na_rpb'], 'm_w_branch_a': out['m_w_branch_a'], 'm_w_branch_b': out['m_w_branch_b'], 'm_w_out': out['m_w_out'], 'm_ffn2_norm': out['m_ffn2_norm'], 'm_ffn2_w_gate': out['m_ffn2_w_gate'], 'm_ffn2_w_up': out['m_ffn2_w_up'], 'm_ffn2_w_down': out['m_ffn2_w_down'], 'm_pl_norm': out['m_pl_norm'], 'm_w_pl': out['m_w_pl'], 'm_w_pl_gate': out['m_w_pl_gate'], 'm_final_norm': out['m_final_norm'], 'v_ffn1_norm': out['v_ffn1_norm'], 'v_ffn1_w_gate': out['v_ffn1_w_gate'], 'v_ffn1_w_up': out['v_ffn1_w_up'], 'v_ffn1_w_down': out['v_ffn1_w_down'], 'v_mix_norm': out['v_mix_norm'], 'v_w_in': out['v_w_in'], 'v_q_a_norm': out['v_q_a_norm'], 'v_w_uq': out['v_w_uq'], 'v_kv_a_norm': out['v_kv_a_norm'], 'v_w_ukv': out['v_w_ukv'], 'v_na_rpb': out['v_na_rpb'], 'v_w_branch_a': out['v_w_branch_a'], 'v_w_branch_b': out['v_w_branch_b'], 'v_w_out': out['v_w_out'], 'v_ffn2_norm': out['v_ffn2_norm'], 'v_ffn2_w_gate': out['v_ffn2_w_gate'], 'v_ffn2_w_up': out['v_ffn2_w_up'], 'v_ffn2_w_down': out['v_ffn2_w_down'], 'v_pl_norm': out['v_pl_norm'], 'v_w_pl': out['v_w_pl'], 'v_w_pl_gate': out['v_w_pl_gate'], 'v_final_norm': out['v_final_norm']}


def _loss(weights, diff, rest, loss_target):
    with _jax.named_scope("forward"):
        args = {**rest, TWIN_DIFF_INPUT: diff, **{k: w.astype(_WEIGHT_DTYPES[k]) for k, w in weights.items()}}
        y = _forward(args)
    with _jax.named_scope("loss_head"):
        err = _jnp.square(y.astype(_jnp.float32) - loss_target)
        return 0.5 * _jnp.sum(_jnp.mean(err, axis=-1)) if err.ndim else 0.5 * err


def _adamw(w, g, m, v):
    m = ADAM_B1 * m + (1.0 - ADAM_B1) * g
    v = ADAM_B2 * v + (1.0 - ADAM_B2) * _jnp.square(g)
    m_hat = m / (1.0 - ADAM_B1 ** ADAM_STEP)
    v_hat = v / (1.0 - ADAM_B2 ** ADAM_STEP)
    delta = -ADAM_LR * (m_hat / (_jnp.sqrt(v_hat) + ADAM_EPS) + ADAM_WD * w)
    return delta, m, v


def reference(x, p, ffn1_norm, ffn1_w_gate, ffn1_w_up, ffn1_w_down, mix_norm, w_in, q_a_norm, w_uq, kv_a_norm, w_ukv, na_rpb, w_branch_a, w_branch_b, w_out, ffn2_norm, ffn2_w_gate, ffn2_w_up, ffn2_w_down, pl_norm, w_pl, w_pl_gate, final_norm, loss_target, m_ffn1_norm, m_ffn1_w_gate, m_ffn1_w_up, m_ffn1_w_down, m_mix_norm, m_w_in, m_q_a_norm, m_w_uq, m_kv_a_norm, m_w_ukv, m_na_rpb, m_w_branch_a, m_w_branch_b, m_w_out, m_ffn2_norm, m_ffn2_w_gate, m_ffn2_w_up, m_ffn2_w_down, m_pl_norm, m_w_pl, m_w_pl_gate, m_final_norm, v_ffn1_norm, v_ffn1_w_gate, v_ffn1_w_up, v_ffn1_w_down, v_mix_norm, v_w_in, v_q_a_norm, v_w_uq, v_kv_a_norm, v_w_ukv, v_na_rpb, v_w_branch_a, v_w_branch_b, v_w_out, v_ffn2_norm, v_ffn2_w_gate, v_ffn2_w_up, v_ffn2_w_down, v_pl_norm, v_w_pl, v_w_pl_gate, v_final_norm):
    given = dict(x=x, p=p, ffn1_norm=ffn1_norm, ffn1_w_gate=ffn1_w_gate, ffn1_w_up=ffn1_w_up, ffn1_w_down=ffn1_w_down, mix_norm=mix_norm, w_in=w_in, q_a_norm=q_a_norm, w_uq=w_uq, kv_a_norm=kv_a_norm, w_ukv=w_ukv, na_rpb=na_rpb, w_branch_a=w_branch_a, w_branch_b=w_branch_b, w_out=w_out, ffn2_norm=ffn2_norm, ffn2_w_gate=ffn2_w_gate, ffn2_w_up=ffn2_w_up, ffn2_w_down=ffn2_w_down, pl_norm=pl_norm, w_pl=w_pl, w_pl_gate=w_pl_gate, final_norm=final_norm, loss_target=loss_target, m_ffn1_norm=m_ffn1_norm, m_ffn1_w_gate=m_ffn1_w_gate, m_ffn1_w_up=m_ffn1_w_up, m_ffn1_w_down=m_ffn1_w_down, m_mix_norm=m_mix_norm, m_w_in=m_w_in, m_q_a_norm=m_q_a_norm, m_w_uq=m_w_uq, m_kv_a_norm=m_kv_a_norm, m_w_ukv=m_w_ukv, m_na_rpb=m_na_rpb, m_w_branch_a=m_w_branch_a, m_w_branch_b=m_w_branch_b, m_w_out=m_w_out, m_ffn2_norm=m_ffn2_norm, m_ffn2_w_gate=m_ffn2_w_gate, m_ffn2_w_up=m_ffn2_w_up, m_ffn2_w_down=m_ffn2_w_down, m_pl_norm=m_pl_norm, m_w_pl=m_w_pl, m_w_pl_gate=m_w_pl_gate, m_final_norm=m_final_norm, v_ffn1_norm=v_ffn1_norm, v_ffn1_w_gate=v_ffn1_w_gate, v_ffn1_w_up=v_ffn1_w_up, v_ffn1_w_down=v_ffn1_w_down, v_mix_norm=v_mix_norm, v_w_in=v_w_in, v_q_a_norm=v_q_a_norm, v_w_uq=v_w_uq, v_kv_a_norm=v_kv_a_norm, v_w_ukv=v_w_ukv, v_na_rpb=v_na_rpb, v_w_branch_a=v_w_branch_a, v_w_branch_b=v_w_branch_b, v_w_out=v_w_out, v_ffn2_norm=v_ffn2_norm, v_ffn2_w_gate=v_ffn2_w_gate, v_ffn2_w_up=v_ffn2_w_up, v_ffn2_w_down=v_ffn2_w_down, v_pl_norm=v_pl_norm, v_w_pl=v_w_pl, v_w_pl_gate=v_w_pl_gate, v_final_norm=v_final_norm)
    weights = {n: given[n] for n in TWIN_WEIGHTS}
    shared = {n: given[n] for n in SHARED_INPUTS}
    per_example = {n: given[n] for n in ['x', 'p']}
    grad_fn = _jax.value_and_grad(_loss, argnums=(0, 1))

    def one_microbatch(ex, loss_target):
        ex = dict(ex)
        diff = ex.pop(TWIN_DIFF_INPUT)
        return grad_fn(weights, diff, {**shared, **ex}, loss_target)

    if N_MICROBATCH == 1:
        loss, (grad_w, grad_x) = one_microbatch(per_example, given["loss_target"])
    else:
        def body(carry, xs):
            loss_sum, grad_sum = carry
            l_k, (gw_k, gx_k) = one_microbatch(xs[0], xs[1])
            with _jax.named_scope("update"):
                return (loss_sum + l_k, _jax.tree.map(_jnp.add, grad_sum, gw_k)), gx_k

        init = (_jnp.zeros((), _jnp.float32), _jax.tree.map(_jnp.zeros_like, weights))
        (loss, grad_w), grad_x = _jax.lax.scan(body, init, (per_example, given["loss_target"]))
    with _jax.named_scope("update"):
        delta_w, new_m, new_v = {}, {}, {}
        for n in TWIN_WEIGHTS:
            delta_w[n], new_m[n], new_v[n] = _adamw(weights[n], grad_w[n], given["m_" + n], given["v_" + n])
    return (loss, grad_x, *[grad_w[n] for n in TWIN_WEIGHTS], *[delta_w[n] for n in TWIN_WEIGHTS],
            *[new_m[n] for n in TWIN_WEIGHTS], *[new_v[n] for n in TWIN_WEIGHTS])
```

```python
import functools

import numpy as np
import jax
import jax.numpy as jnp
from jax import lax
from jax.experimental import pallas as pl
from jax.experimental.pallas import tpu as pltpu

F32 = jnp.float32
BF = jnp.bfloat16
MESH = pl.DeviceIdType.MESH

NDEV = 8
NCHIP = 4
VMEM_LIMIT = 56 * 1024 * 1024
EPS = 1e-6
NEG = -1e30
GRID_W = 64
NA_HEADS, NA_DIM = 8, 128
NA_ROWS_WIN, NA_COLS_WIN = 8, 16
NA_HG = 4
ML_HEADS, ML_NOPE, ML_ROPE, ML_V = 8, 128, 64, 128
ML_QK = ML_NOPE + ML_ROPE
ML_RANK = 512
ROPE_THETA = 10000.0
LR, B1, B2, ADAM_EPS, WD, STEP = 0.001, 0.9, 0.999, 1e-08, 0.01, 10
HI = lax.Precision.HIGHEST

_DN = {"nn": (((1,), (0,)), ((), ())), "nt": (((1,), (1,)), ((), ())), "tn": (((0,), (0,)), ((), ()))}


def _params(n):
    return pltpu.CompilerParams(dimension_semantics=("arbitrary",) * n, vmem_limit_bytes=VMEM_LIMIT)


def _sig(v):
    return jax.nn.sigmoid(v)


def _mm(name, grid, prods, extras, outs, epi, nacc=1):
    n_p, n_e = len(prods), len(extras)

    def body(*refs):
        ab, ex, out = refs[:2 * n_p], refs[2 * n_p:2 * n_p + n_e], refs[2 * n_p + n_e:]
        accs = [None] * nacc
        for i, prod in enumerate(prods):
            dn, acc, loop = prod[6], prod[7], prod[8]
            a_ref, b_ref = ab[2 * i], ab[2 * i + 1]
            if loop:
                for g in range(loop):
                    t = lax.dot_general(a_ref[g], b_ref[g], _DN[dn], preferred_element_type=F32)
                    accs[acc] = t if accs[acc] is None else accs[acc] + t
            else:
                t = lax.dot_general(a_ref[...], b_ref[...], _DN[dn], preferred_element_type=F32)
                accs[acc] = t if accs[acc] is None else accs[acc] + t
        epi(accs, ex, out)

    in_specs, args = [], []
    for prod in prods:
        in_specs += [pl.BlockSpec(prod[1], prod[2]), pl.BlockSpec(prod[4], prod[5])]
        args += [prod[0], prod[3]]
    for e, e_blk, e_map in extras:
        in_specs.append(pl.BlockSpec(e_blk, e_map))
        args.append(e)
    res = pl.pallas_call(
        body, name=name, grid=grid, in_specs=in_specs,
        out_specs=[pl.BlockSpec(blk, mp) for _, _, blk, mp in outs],
        out_shape=[jax.ShapeDtypeStruct(s, d) for s, d, _, _ in outs],
        compiler_params=_params(len(grid)),
    )(*args)
    return res


def _store(accs, ex, out):
    out[0][...] = accs[0].astype(out[0].dtype)


def _row_tile(r, c, row_bytes=4, budget=1 << 20):
    best = None
    for t in range(16, r + 1, 16):
        if r % t == 0 and t * c * row_bytes <= budget:
            best = t
    return best or r


def _tile(n, want):
    t = min(n, want)
    assert n % t == 0, (n, want)
    return t


def _mm_nn(name, a, b, out_dtype, tm=512, tn=512):
    m, k = a.shape
    n = b.shape[1]
    tm, tn = _tile(m, tm), (tn if n % tn == 0 else n)
    return _mm(name, (n // tn, m // tm),
               [(a, (tm, k), lambda j, i: (i, 0), b, (k, tn), lambda j, i: (0, j), "nn", 0, 0)], [],
               [((m, n), out_dtype, (tm, tn), lambda j, i: (i, j))], _store)[0]


def _mm_tn(name, a, b, out_dtype, ta=512, tb=512, scale=None):
    t, ka = a.shape
    nb = b.shape[1]
    ta, tb = (ta if ka % ta == 0 else ka), (tb if nb % tb == 0 else nb)

    def epi(accs, ex, out):
        v = accs[0] if scale is None else accs[0] * scale
        out[0][...] = v.astype(out[0].dtype)

    return _mm(name, (ka // ta, nb // tb),
               [(a, (t, ta), lambda i, j: (0, i), b, (t, tb), lambda i, j: (0, j), "tn", 0, 0)], [],
               [((ka, nb), out_dtype, (ta, tb), lambda i, j: (i, j))], epi)[0]


def _rms_fwd(name, x, g, tm=256):
    s, d = x.shape
    tm = _tile(s, tm)

    def body(x_ref, g_ref, o_ref):
        v = x_ref[...]
        o_ref[...] = (v * lax.rsqrt(jnp.mean(v * v, axis=-1, keepdims=True) + EPS) * g_ref[...]).astype(o_ref.dtype)

    return pl.pallas_call(
        body, name=name, grid=(s // tm,),
        in_specs=[pl.BlockSpec((tm, d), lambda i: (i, 0)), pl.BlockSpec((1, d), lambda i: (0, 0))],
        out_specs=pl.BlockSpec((tm, d), lambda i: (i, 0)), out_shape=jax.ShapeDtypeStruct((s, d), BF),
        compiler_params=_params(1))(x, g)


def _acc_rows(ref, part, i):
    @pl.when(i == 0)
    def _():
        ref[...] = part

    @pl.when(i > 0)
    def _():
        ref[...] += part


def _rms_bwd_math(dn, v, g):
    rstd = lax.rsqrt(jnp.mean(v * v, axis=-1, keepdims=True) + EPS)
    xh = v * rstd
    dxh = dn * g
    dx = rstd * (dxh - xh * jnp.mean(dxh * xh, axis=-1, keepdims=True))
    return dx, jnp.sum(dn * xh, axis=0, keepdims=True)


def _rms_bwd(name, dn, x, g, resid, tm=256):
    s, d = x.shape
    tm = _tile(s, tm)

    def body(dn_ref, x_ref, g_ref, r_ref, dx_ref, dg_ref):
        dx, part = _rms_bwd_math(dn_ref[...].astype(F32), x_ref[...], g_ref[...])
        dx_ref[...] = r_ref[...] + dx
        _acc_rows(dg_ref, part, pl.program_id(0))

    row = pl.BlockSpec((tm, d), lambda i: (i, 0))
    one = pl.BlockSpec((1, d), lambda i: (0, 0))
    return pl.pallas_call(
        body, name=name, grid=(s // tm,), in_specs=[row, row, one, row], out_specs=[row, one],
        out_shape=[jax.ShapeDtypeStruct((s, d), F32), jax.ShapeDtypeStruct((1, d), F32)],
        compiler_params=_params(1))(dn, x, g, resid)


def _loss_head(h, target, g, tm=256):
    s, d = h.shape
    tm = _tile(s, tm)

    def body(h_ref, t_ref, g_ref, dh_ref, dg_ref, loss_ref):
        v, gv = h_ref[...], g_ref[...]
        rstd = lax.rsqrt(jnp.mean(v * v, axis=-1, keepdims=True) + EPS)
        xh = v * rstd
        err = xh * gv - t_ref[...]
        part_loss = 0.5 * jnp.sum(jnp.mean(err * err, axis=-1, keepdims=True), axis=0, keepdims=True)
        dy = err * (1.0 / d)
        dxh = dy * gv
        dh_ref[...] = rstd * (dxh - xh * jnp.mean(dxh * xh, axis=-1, keepdims=True))
        i = pl.program_id(0)
        _acc_rows(dg_ref, jnp.sum(dy * xh, axis=0, keepdims=True), i)
        _acc_rows(loss_ref, jnp.broadcast_to(part_loss, loss_ref.shape), i)

    row = pl.BlockSpec((tm, d), lambda i: (i, 0))
    one = pl.BlockSpec((1, d), lambda i: (0, 0))
    return pl.pallas_call(
        body, name="loss_head", grid=(s // tm,), in_specs=[row, row, one],
        out_specs=[row, one, pl.BlockSpec((1, 128), lambda i: (0, 0))],
        out_shape=[jax.ShapeDtypeStruct((s, d), F32), jax.ShapeDtypeStruct((1, d), F32),
                   jax.ShapeDtypeStruct((1, 128), F32)],
        compiler_params=_params(1))(h, target, g)


def _pl_bwd_elem(dh, pe, t, tm=256):
    s, d = dh.shape
    tm = _tile(s, tm)

    def body(dh_ref, pe_ref, t_ref, dt_ref, dpe_ref):
        dh_v, sg = dh_ref[...], _sig(t_ref[...])
        dt_ref[...] = (dh_v * pe_ref[...].astype(F32) * sg * (1.0 - sg)).astype(BF)
        dpe_ref[...] = (dh_v * sg).astype(BF)

    row = pl.BlockSpec((tm, d), lambda i: (i, 0))
    return pl.pallas_call(
        body, name="pl_bwd_elem", grid=(s // tm,), in_specs=[row, row, row], out_specs=[row, row],
        out_shape=[jax.ShapeDtypeStruct((s, d), BF)] * 2, compiler_params=_params(1))(dh, pe, t)


def _ffn_up(name, xn, wg, wu, tm=512):
    s, d = xn.shape
    g, _, fb = wg.shape
    tm = _tile(s, tm)

    def epi(accs, ex, out):
        hg, hu = accs
        out[0][...] = hg.astype(BF)
        out[1][...] = hu.astype(BF)
        out[2][...] = (hg * _sig(hg) * hu).astype(BF)

    a_map = lambda j, i: (i, 0)
    w_map = lambda j, i: (j, 0, 0)
    o = ((g, s, fb), BF, (None, tm, fb), lambda j, i: (j, i, 0))
    return _mm(name, (g, s // tm),
               [(xn, (tm, d), a_map, wg, (None, d, fb), w_map, "nn", 0, 0),
                (xn, (tm, d), a_map, wu, (None, d, fb), w_map, "nn", 1, 0)], [], [o, o, o], epi, nacc=2)


def _ffn_down(name, a, wd, resid, tm=512, tn=512):
    g, s, fb = a.shape
    d = wd.shape[2]
    tm, tn = _tile(s, tm), _tile(d, tn)

    def epi(accs, ex, out):
        out[0][...] = ex[0][...] + 0.5 * accs[0]

    return _mm(name, (d // tn, s // tm),
               [(a, (g, tm, fb), lambda j, i: (0, i, 0), wd, (g, fb, tn), lambda j, i: (0, 0, j), "nn", 0, g)],
               [(resid, (tm, tn), lambda j, i: (i, j))],
               [((s, d), F32, (tm, tn), lambda j, i: (i, j))], epi)[0]


def _ffn_bwd_act(name, dh, wd, hg, hu, tm=512):
    s, d = dh.shape
    g, fb, _ = wd.shape
    tm = _tile(s, tm)

    def epi(accs, ex, out):
        da = 0.5 * accs[0]
        hg_v, hu_v = ex[0][...].astype(F32), ex[1][...].astype(F32)
        sg = _sig(hg_v)
        out[0][...] = (da * hu_v * (sg * (1.0 + hg_v * (1.0 - sg)))).astype(BF)
        out[1][...] = (da * (hg_v * sg)).astype(BF)

    blk = (None, tm, fb)
    gmap = lambda j, i: (j, i, 0)
    return _mm(name, (g, s // tm),
               [(dh, (tm, d), lambda j, i: (i, 0), wd, (None, fb, d), lambda j, i: (j, 0, 0), "nt", 0, 0)],
               [(hg, blk, gmap), (hu, blk, gmap)],
               [((g, s, fb), BF, blk, gmap), ((g, s, fb), BF, blk, gmap)], epi)


def _ffn_bwd_wd(name, a, dh, tn=512):
    g, s, fb = a.shape
    d = dh.shape[1]
    tn = _tile(d, tn)

    def epi(accs, ex, out):
        out[0][...] = (0.5 * accs[0]).astype(BF)

    return _mm(name, (g, d // tn),
               [(a, (None, s, fb), lambda j, i: (j, 0, 0), dh, (s, tn), lambda j, i: (0, i), "tn", 0, 0)], [],
               [((g, fb, d), BF, (None, fb, tn), lambda j, i: (j, 0, i))], epi)[0]


def _ffn_bwd_wup(name, xn, dhg, dhu, tk=512):
    s, d = xn.shape
    g, _, fb = dhg.shape
    tk = _tile(d, tk)

    def epi(accs, ex, out):
        out[0][...] = accs[0].astype(BF)
        out[1][...] = accs[1].astype(BF)

    a_map = lambda j, i: (0, i)
    b_map = lambda j, i: (j, 0, 0)
    o = ((g, d, fb), BF, (None, tk, fb), lambda j, i: (j, i, 0))
    return _mm(name, (g, d // tk),
               [(xn, (s, tk), a_map, dhg, (None, s, fb), b_map, "tn", 0, 0),
                (xn, (s, tk), a_map, dhu, (None, s, fb), b_map, "tn", 1, 0)], [], [o, o], epi, nacc=2)


def _ffn_bwd_x(name, dhg, dhu, wg, wu, tm=256, tn=256):
    g, s, fb = dhg.shape
    d = wg.shape[1]
    tm, tn = _tile(s, tm), _tile(d, tn)
    a_blk, a_map = (g, tm, fb), lambda j, i: (0, i, 0)
    b_blk, b_map = (g, tn, fb), lambda j, i: (0, j, 0)
    return _mm(name, (d // tn, s // tm),
               [(dhg, a_blk, a_map, wg, b_blk, b_map, "nt", 0, g), (dhu, a_blk, a_map, wu, b_blk, b_map, "nt", 0, g)],
               [], [((s, d), F32, (tm, tn), lambda j, i: (i, j))], _store)[0]


def _ffn_forward(tag, h, gain, wg, wu, wd):
    xn = _rms_fwd(tag + "_norm", h, gain)
    hg, hu, a = _ffn_up(tag + "_up", xn, wg, wu)
    return _ffn_down(tag + "_down", a, wd, h), (xn, hg, hu, a)


def _ffn_backward(tag, dh, h, gain, wg, wu, wd, saved):
    xn, hg, hu, a = saved
    dhb = dh.astype(BF)
    dwd = _ffn_bwd_wd(tag + "_dwd", a, dhb)
    dhg, dhu = _ffn_bwd_act(tag + "_dact", dhb, wd, hg, hu)
    dwg, dwu = _ffn_bwd_wup(tag + "_dwup", xn, dhg, dhu)
    dxn = _ffn_bwd_x(tag + "_dx", dhg, dhu, wg, wu)
    dx, dgain = _rms_bwd(tag + "_dnorm", dxn, h, gain, dh)
    return dx, dgain, dwg, dwu, dwd


def _na_geometry(rows):
    kh = min(NA_ROWS_WIN, rows)
    cols = np.arange(GRID_W)
    col_start = np.clip(cols - NA_COLS_WIN // 2, 0, GRID_W - NA_COLS_WIN)
    mask = (cols[None, :] >= col_start[:, None]) & (cols[None, :] < col_start[:, None] + NA_COLS_WIN)
    dc = np.clip(cols[None, :] - cols[:, None], -(NA_COLS_WIN - 1), NA_COLS_WIN - 1) + (NA_COLS_WIN - 1)
    return kh, mask, dc


def _na_bias_tables(rpb, rows):
    kh, mask, dc = _na_geometry(rows)
    t = jnp.where(jnp.asarray(mask)[None, None], rpb[:, :, dc], NEG)
    tb = jnp.stack([t[:, d0:d0 + kh] for d0 in range(NA_ROWS_WIN)], 0)
    return tb.transpose(0, 1, 3, 2, 4).reshape(NA_ROWS_WIN, NA_HEADS, GRID_W, kh * GRID_W)


def _na_row_start(r, rows, kh):
    return jnp.clip(r - kh // 2, 0, rows - kh)


def _na_specs(s, rows, kh):
    hw = NA_HG * NA_DIM
    nq = NA_HEADS // NA_HG
    q_spec = pl.BlockSpec((GRID_W, hw), lambda j, r: (r, j))
    k_spec = pl.BlockSpec((s, hw), lambda j, r: (0, nq + j))
    v_spec = pl.BlockSpec((s, hw), lambda j, r: (0, 2 * nq + j))
    b_spec = pl.BlockSpec((None, NA_HG, GRID_W, kh * GRID_W),
                          lambda j, r: (_na_row_start(r, rows, kh) - r + NA_ROWS_WIN - 1, j, 0, 0))
    return q_spec, k_spec, v_spec, b_spec, hw, nq


def _na_probs(q, k, bias):
    sc = lax.dot_general(q, k, _DN["nt"], preferred_element_type=F32) * (NA_DIM ** -0.5) + bias
    e = jnp.exp(sc - jnp.max(sc, axis=-1, keepdims=True))
    return e / jnp.sum(e, axis=-1, keepdims=True)


def _na_fwd(qkv, tb):
    s = qkv.shape[0]
    rows = s // GRID_W
    kh = min(NA_ROWS_WIN, rows)
    q_spec, k_spec, v_spec, b_spec, hw, nq = _na_specs(s, rows, kh)

    def body(q_ref, k_ref, v_ref, b_ref, o_ref):
        r = pl.program_id(1)
        start = pl.multiple_of(_na_row_start(r, rows, kh) * GRID_W, GRID_W)
        for h in range(NA_HG):
            cs = slice(h * NA_DIM, (h + 1) * NA_DIM)
            p = _na_probs(q_ref[:, cs], k_ref[pl.ds(start, kh * GRID_W), cs], b_ref[h])
            o_ref[:, cs] = jnp.dot(p.astype(BF), v_ref[pl.ds(start, kh * GRID_W), cs],
                                   preferred_element_type=F32).astype(BF)

    return pl.pallas_call(
        body, name="na_fwd", grid=(nq, rows), in_specs=[q_spec, k_spec, v_spec, b_spec],
        out_specs=pl.BlockSpec((GRID_W, hw), lambda j, r: (r, j)),
        out_shape=jax.ShapeDtypeStruct((s, NA_HEADS * NA_DIM), BF), compiler_params=_params(2))(qkv, qkv, qkv, tb)


def _na_bwd(qkv, tb, do):
    s = qkv.shape[0]
    rows = s // GRID_W
    kh = min(NA_ROWS_WIN, rows)
    q_spec, k_spec, v_spec, b_spec, hw, nq = _na_specs(s, rows, kh)
    nd = 2 * NA_ROWS_WIN - 1

    def body(q_ref, k_ref, v_ref, b_ref, do_ref, dq_ref, dk_ref, dv_ref, dt_ref):
        r = pl.program_id(1)

        @pl.when(r == 0)
        def _():
            dk_ref[...] = jnp.zeros_like(dk_ref)
            dv_ref[...] = jnp.zeros_like(dv_ref)
            dt_ref[...] = jnp.zeros_like(dt_ref)

        rs = _na_row_start(r, rows, kh)
        d0 = rs - r + NA_ROWS_WIN - 1
        win = pl.ds(pl.multiple_of(rs * GRID_W, GRID_W), kh * GRID_W)
        for h in range(NA_HG):
            cs = slice(h * NA_DIM, (h + 1) * NA_DIM)
            q, k, v, do_h = q_ref[:, cs], k_ref[win, cs], v_ref[win, cs], do_ref[:, cs]
            p = _na_probs(q, k, b_ref[h])
            dp = lax.dot_general(do_h, v, _DN["nt"], preferred_element_type=F32)
            ds = p * (dp - jnp.sum(p * dp, axis=-1, keepdims=True))
            for i in range(kh):
                dt_ref[h, d0 + i] += ds[:, i * GRID_W:(i + 1) * GRID_W]
            dsb = (ds * (NA_DIM ** -0.5)).astype(BF)
            dq_ref[:, cs] = jnp.dot(dsb, k, preferred_element_type=F32).astype(BF)
            dk_ref[win, cs] += lax.dot_general(dsb, q, _DN["tn"], preferred_element_type=F32)
            dv_ref[win, cs] += lax.dot_general(p.astype(BF), do_h, _DN["tn"], preferred_element_type=F32)

    width = NA_HEADS * NA_DIM
    whole = pl.BlockSpec((s, hw), lambda j, r: (0, j))
    return pl.pallas_call(
        body, name="na_bwd", grid=(nq, rows),
        in_specs=[q_spec, k_spec, v_spec, b_spec, pl.BlockSpec((GRID_W, hw), lambda j, r: (r, j))],
        out_specs=[pl.BlockSpec((GRID_W, hw), lambda j, r: (r, j)), whole, whole,
                   pl.BlockSpec((NA_HG, nd, GRID_W, GRID_W), lambda j, r: (j, 0, 0, 0))],
        out_shape=[jax.ShapeDtypeStruct((s, width), BF), jax.ShapeDtypeStruct((s, width), F32),
                   jax.ShapeDtypeStruct((s, width), F32),
                   jax.ShapeDtypeStruct((NA_HEADS, nd, GRID_W, GRID_W), F32)],
        compiler_params=_params(2))(qkv, qkv, qkv, tb, do)


def _na_rpb_grad(dt, rows):
    _, mask, dc = _na_geometry(rows)
    nd, nc = 2 * NA_ROWS_WIN - 1, 2 * NA_COLS_WIN - 1
    onehot = np.zeros((GRID_W * GRID_W, 128), np.float32)
    onehot[np.arange(GRID_W * GRID_W), dc.reshape(-1)] = mask.reshape(-1).astype(np.float32)
    flat = dt.reshape(NA_HEADS * nd, GRID_W * GRID_W)

    def body(a_ref, e_ref, o_ref):
        o_ref[...] = jnp.dot(a_ref[...], e_ref[...], precision=HI, preferred_element_type=F32)

    out = pl.pallas_call(body, name="na_rpb_grad", out_shape=jax.ShapeDtypeStruct((NA_HEADS * nd, 128), F32),
                         compiler_params=_params(0))(flat, jnp.asarray(onehot))
    return out[:, :nc].reshape(NA_HEADS, nd, nc)


def _rope_consts(s):
    pos = np.arange(s, dtype=np.float32)
    inv = (1.0 / (ROPE_THETA ** (np.arange(0, ML_ROPE, 2, dtype=np.float32) / ML_ROPE))).astype(np.float32)
    ang = pos[:, None] * inv[None, :]
    cos, sin = np.cos(ang).astype(np.float32), np.sin(ang).astype(np.float32)
    half = ML_ROPE // 2
    rot = np.zeros((ML_ROPE, ML_ROPE), np.float32)
    rot[np.arange(half) + half, np.arange(half)] = -1.0
    rot[np.arange(half), np.arange(half) + half] = 1.0
    return (jnp.asarray(np.concatenate([cos, cos], 1)), jnp.asarray(np.concatenate([sin, sin], 1)),
            jnp.asarray(rot), jnp.asarray(rot.T.copy()))


def _rope(v, cos, sin, rot):
    return v * cos + jnp.dot(v, rot, precision=HI, preferred_element_type=F32) * sin


def _unrope(dv, cos, sin, rot_t):
    return dv * cos + jnp.dot(dv * sin, rot_t, precision=HI, preferred_element_type=F32)


def _rms(v, g):
    return v * lax.rsqrt(jnp.mean(v * v, axis=-1, keepdims=True) + EPS) * g


def _mla_prep(lat, gq, gkv, cos, sin, rot, tm=256):
    s, w = lat.shape
    tm = _tile(s, tm)

    def body(l_ref, gq_ref, gkv_ref, c_ref, s_ref, r_ref, cq_ref, ckv_ref, kr_ref):
        cq_ref[...] = _rms(l_ref[:, :ML_RANK], gq_ref[...]).astype(BF)
        ckv_ref[...] = _rms(l_ref[:, ML_RANK:2 * ML_RANK], gkv_ref[...]).astype(BF)
        kr_ref[...] = _rope(l_ref[:, 2 * ML_RANK:], c_ref[...], s_ref[...], r_ref[...]).astype(BF)

    row = lambda c: pl.BlockSpec((tm, c), lambda i: (i, 0))
    full = lambda a: pl.BlockSpec(a.shape, lambda i: (0, 0))
    return pl.pallas_call(
        body, name="mla_prep", grid=(s // tm,),
        in_specs=[row(w), full(gq), full(gkv), row(ML_ROPE), row(ML_ROPE), full(rot)],
        out_specs=[row(ML_RANK), row(ML_RANK), row(ML_ROPE)],
        out_shape=[jax.ShapeDtypeStruct((s, ML_RANK), BF), jax.ShapeDtypeStruct((s, ML_RANK), BF),
                   jax.ShapeDtypeStruct((s, ML_ROPE), BF)],
        compiler_params=_params(1))(lat, gq, gkv, cos, sin, rot)


def _mla_q_proj(cq, wuq, cos, sin, rot, tm=512):
    s, k = cq.shape
    tm = _tile(s, tm)

    def epi(accs, ex, out):
        acc = accs[0]
        out[0][:, :ML_NOPE] = acc[:, :ML_NOPE].astype(BF)
        out[0][:, ML_NOPE:] = _rope(acc[:, ML_NOPE:], ex[0][...], ex[1][...], ex[2][...]).astype(BF)

    rmap = lambda j, i: (i, 0)
    return _mm("mla_q_proj", (ML_HEADS, s // tm),
               [(cq, (tm, k), rmap, wuq, (None, k, ML_QK), lambda j, i: (j, 0, 0), "nn", 0, 0)],
               [(cos, (tm, ML_ROPE), rmap), (sin, (tm, ML_ROPE), rmap), (rot, rot.shape, lambda j, i: (0, 0))],
               [((ML_HEADS, s, ML_QK), BF, (None, tm, ML_QK), lambda j, i: (j, i, 0))], epi)[0]


def _mla_kv_proj(ckv, wukv, kr, tm=512):
    s, k = ckv.shape
    tm = _tile(s, tm)

    def epi(accs, ex, out):
        acc = accs[0]
        out[0][:, :ML_NOPE] = acc[:, :ML_NOPE].astype(BF)
        out[0][:, ML_NOPE:] = ex[0][...]
        out[1][...] = acc[:, ML_NOPE:].astype(BF)

    rmap = lambda j, i: (i, 0)
    gmap = lambda j, i: (j, i, 0)
    return _mm("mla_kv_proj", (ML_HEADS, s // tm),
               [(ckv, (tm, k), rmap, wukv, (None, k, ML_NOPE + ML_V), lambda j, i: (j, 0, 0), "nn", 0, 0)],
               [(kr, (tm, ML_ROPE), rmap)],
               [((ML_HEADS, s, ML_QK), BF, (None, tm, ML_QK), gmap), ((ML_HEADS, s, ML_V), BF, (None, tm, ML_V), gmap)],
               epi)


def _mla_probs(q, k):
    sc = lax.dot_general(q, k, _DN["nt"], preferred_element_type=F32) * (ML_QK ** -0.5)
    e = jnp.exp(sc - jnp.max(sc, axis=-1, keepdims=True))
    return e / jnp.sum(e, axis=-1, keepdims=True)


def _mla_fwd(q, k, v, tq=512):
    _, s, _ = q.shape
    tq = _tile(s, tq)

    def body(q_ref, k_ref, v_ref, o_ref):
        p = _mla_probs(q_ref[...], k_ref[...])
        o_ref[...] = jnp.dot(p.astype(BF), v_ref[...], preferred_element_type=F32).astype(BF)

    return pl.pallas_call(
        body, name="mla_fwd", grid=(ML_HEADS, s // tq),
        in_specs=[pl.BlockSpec((None, tq, ML_QK), lambda h, i: (h, i, 0)),
                  pl.BlockSpec((None, s, ML_QK), lambda h, i: (h, 0, 0)),
                  pl.BlockSpec((None, s, ML_V), lambda h, i: (h, 0, 0))],
        out_specs=pl.BlockSpec((tq, ML_V), lambda h, i: (i, h)),
        out_shape=jax.ShapeDtypeStruct((s, ML_HEADS * ML_V), BF), compiler_params=_params(2))(q, k, v)


def _mla_bwd(q, k, v, do, tq=256):
    _, s, _ = q.shape
    tq = _tile(s, tq)

    def body(q_ref, k_ref, v_ref, do_ref, dq_ref, dk_ref, dv_ref):
        i = pl.program_id(1)
        qv, kv, vv, dov = q_ref[...], k_ref[...], v_ref[...], do_ref[...]
        p = _mla_probs(qv, kv)
        dp = lax.dot_general(dov, vv, _DN["nt"], preferred_element_type=F32)
        ds = (p * (dp - jnp.sum(p * dp, axis=-1, keepdims=True)) * (ML_QK ** -0.5)).astype(BF)
        dq_ref[...] = jnp.dot(ds, kv, preferred_element_type=F32)
        _acc_rows(dk_ref, lax.dot_general(ds, qv, _DN["tn"], preferred_element_type=F32), i)
        _acc_rows(dv_ref, lax.dot_general(p.astype(BF), dov, _DN["tn"], preferred_element_type=F32), i)

    return pl.pallas_call(
        body, name="mla_bwd", grid=(ML_HEADS, s // tq),
        in_specs=[pl.BlockSpec((None, tq, ML_QK), lambda h, i: (h, i, 0)),
                  pl.BlockSpec((None, s, ML_QK), lambda h, i: (h, 0, 0)),
                  pl.BlockSpec((None, s, ML_V), lambda h, i: (h, 0, 0)),
                  pl.BlockSpec((tq, ML_V), lambda h, i: (i, h))],
        out_specs=[pl.BlockSpec((None, tq, ML_QK), lambda h, i: (h, i, 0)),
                   pl.BlockSpec((None, s, ML_QK), lambda h, i: (h, 0, 0)),
                   pl.BlockSpec((None, s, ML_V), lambda h, i: (h, 0, 0))],
        out_shape=[jax.ShapeDtypeStruct((ML_HEADS, s, ML_QK), F32), jax.ShapeDtypeStruct((ML_HEADS, s, ML_QK), F32),
                   jax.ShapeDtypeStruct((ML_HEADS, s, ML_V), F32)],
        compiler_params=_params(2))(q, k, v, do)


def _mla_post(dq, dk, dv, cos, sin, rot_t, tm=256):
    _, s, _ = dq.shape
    tm = _tile(s, tm)

    def body(dq_ref, dk_ref, dv_ref, c_ref, s_ref, r_ref, dqp_ref, dkv_ref, dkr_ref):
        h = pl.program_id(1)
        dqv, dkk = dq_ref[...], dk_ref[...]
        dqp_ref[:, :ML_NOPE] = dqv[:, :ML_NOPE].astype(BF)
        dqp_ref[:, ML_NOPE:] = _unrope(dqv[:, ML_NOPE:], c_ref[...], s_ref[...], r_ref[...]).astype(BF)
        dkv_ref[:, :ML_NOPE] = dkk[:, :ML_NOPE].astype(BF)
        dkv_ref[:, ML_NOPE:] = dv_ref[...].astype(BF)
        _acc_rows(dkr_ref, dkk[:, ML_NOPE:], h)

    gspec = lambda c: pl.BlockSpec((None, tm, c), lambda i, h: (h, i, 0))
    rspec = pl.BlockSpec((tm, ML_ROPE), lambda i, h: (i, 0))
    return pl.pallas_call(
        body, name="mla_post", grid=(s // tm, ML_HEADS),
        in_specs=[gspec(ML_QK), gspec(ML_QK), gspec(ML_V), rspec, rspec,
                  pl.BlockSpec(rot_t.shape, lambda i, h: (0, 0))],
        out_specs=[gspec(ML_QK), gspec(ML_NOPE + ML_V), rspec],
        out_shape=[jax.ShapeDtypeStruct((ML_HEADS, s, ML_QK), BF),
                   jax.ShapeDtypeStruct((ML_HEADS, s, ML_NOPE + ML_V), BF),
                   jax.ShapeDtypeStruct((s, ML_ROPE), F32)],
        compiler_params=_params(2))(dq, dk, dv, cos, sin, rot_t)


def _mla_lat_bwd(dcq, dckv, dkr, lat, gq, gkv, cos, sin, rot_t, tm=256):
    s, w = lat.shape
    tm = _tile(s, tm)

    def body(dcq_ref, dckv_ref, dkr_ref, l_ref, gq_ref, gkv_ref, c_ref, s_ref, r_ref, dl_ref, dgq_ref, dgkv_ref):
        i = pl.program_id(0)
        dql, pq = _rms_bwd_math(dcq_ref[...], l_ref[:, :ML_RANK], gq_ref[...])
        dkl, pkv = _rms_bwd_math(dckv_ref[...], l_ref[:, ML_RANK:2 * ML_RANK], gkv_ref[...])
        dl_ref[:, :ML_RANK] = dql.astype(BF)
        dl_ref[:, ML_RANK:2 * ML_RANK] = dkl.astype(BF)
        dl_ref[:, 2 * ML_RANK:] = _unrope(dkr_ref[...], c_ref[...], s_ref[...], r_ref[...]).astype(BF)
        _acc_rows(dgq_ref, pq, i)
        _acc_rows(dgkv_ref, pkv, i)

    row = lambda c: pl.BlockSpec((tm, c), lambda i: (i, 0))
    full = lambda a: pl.BlockSpec(a.shape, lambda i: (0, 0))
    return pl.pallas_call(
        body, name="mla_lat_bwd", grid=(s // tm,),
        in_specs=[row(ML_RANK), row(ML_RANK), row(ML_ROPE), row(w), full(gq), full(gkv), row(ML_ROPE), row(ML_ROPE),
                  full(rot_t)],
        out_specs=[row(w), full(gq), full(gkv)],
        out_shape=[jax.ShapeDtypeStruct((s, w), BF), jax.ShapeDtypeStruct(gq.shape, F32),
                   jax.ShapeDtypeStruct(gkv.shape, F32)],
        compiler_params=_params(1))(dcq, dckv, dkr, lat, gq, gkv, cos, sin, rot_t)


def _grp_dw(name, a, dout, ta=512):
    s, k = a.shape
    ta = _tile(k, ta)
    if dout.ndim == 3:
        g, _, nb = dout.shape
        b_blk, b_map = (None, s, nb), lambda j, i: (j, 0, 0)
    else:
        g, nb = NDEV, dout.shape[1] // NDEV
        b_blk, b_map = (s, nb), lambda j, i: (0, j)
    return _mm(name, (g, k // ta),
               [(a, (s, ta), lambda j, i: (0, i), dout, b_blk, b_map, "tn", 0, 0)], [],
               [((g, k, nb), BF, (None, ta, nb), lambda j, i: (j, i, 0))], _store)[0]


def _grp_dx(name, dout, w, tm=512, tn=512):
    g, s, nb = dout.shape
    k = w.shape[1]
    tm, tn = _tile(s, tm), _tile(k, tn)
    return _mm(name, (k // tn, s // tm),
               [(dout, (g, tm, nb), lambda j, i: (0, i, 0), w, (g, tn, nb), lambda j, i: (0, j, 0), "nt", 0, g)], [],
               [((s, k), F32, (tm, tn), lambda j, i: (i, j))], _store)[0]


def _row_dw(name, a, dout, tn=512):
    s, n = dout.shape
    tn = _tile(n, tn)
    if a.ndim == 3:
        kb = a.shape[2]
        a_blk, a_map = (None, s, kb), lambda j, i: (j, 0, 0)
    else:
        kb = a.shape[1] // NDEV
        a_blk, a_map = (s, kb), lambda j, i: (0, j)
    return _mm(name, (NDEV, n // tn),
               [(a, a_blk, a_map, dout, (s, tn), lambda j, i: (0, i), "tn", 0, 0)], [],
               [((NDEV, kb, n), BF, (None, kb, tn), lambda j, i: (j, 0, i))], _store)[0]


def _mix_merge(oa, ob, wa, wb, ga, gb, tm=512):
    s, k = oa.shape
    g, _, nb = wa.shape
    tm = _tile(s, tm)

    def epi(accs, ex, out):
        ya, yb = accs
        out[0][...] = ya.astype(BF)
        out[1][...] = yb.astype(BF)
        out[2][...] = (_sig(ex[0][...]) * ya + _sig(ex[1][...]) * yb).astype(BF)

    rmap = lambda j, i: (i, 0)
    wmap = lambda j, i: (j, 0, 0)
    o = ((g, s, nb), BF, (None, tm, nb), lambda j, i: (j, i, 0))
    cmap = lambda j, i: (i, j)
    return _mm("mix_merge", (g, s // tm),
               [(oa, (tm, k), rmap, wa, (None, k, nb), wmap, "nn", 0, 0),
                (ob, (tm, k), rmap, wb, (None, k, nb), wmap, "nn", 1, 0)],
               [(ga, (tm, nb), cmap), (gb, (tm, nb), cmap)], [o, o, o], epi, nacc=2)


def _mix_out(merged, wout, resid, tm=512, tn=512):
    g, s, kb = merged.shape
    d = wout.shape[2]
    tm, tn = _tile(s, tm), _tile(d, tn)

    def epi(accs, ex, out):
        out[0][...] = ex[0][...] + accs[0]

    return _mm("mix_out", (d // tn, s // tm),
               [(merged, (g, tm, kb), lambda j, i: (0, i, 0), wout, (g, kb, tn), lambda j, i: (0, 0, j), "nn", 0, g)],
               [(resid, (tm, tn), lambda j, i: (i, j))],
               [((s, d), F32, (tm, tn), lambda j, i: (i, j))], epi)[0]


def _mix_out_bwd(dh, wout, ga, gb, ya, yb, tm=512):
    s, d = dh.shape
    g, kb, _ = wout.shape
    tm = _tile(s, tm)

    def epi(accs, ex, out):
        dm = accs[0]
        sa, sb = _sig(ex[0][...]), _sig(ex[1][...])
        out[0][...] = (dm * sa).astype(BF)
        out[1][...] = (dm * sb).astype(BF)
        out[2][...] = (dm * ex[2][...].astype(F32) * sa * (1.0 - sa)).astype(BF)
        out[3][...] = (dm * ex[3][...].astype(F32) * sb * (1.0 - sb)).astype(BF)

    cmap = lambda j, i: (i, j)
    gmap = lambda j, i: (j, i, 0)
    og = ((g, s, kb), BF, (None, tm, kb), gmap)
    oc = ((s, g * kb), BF, (tm, kb), cmap)
    return _mm("mix_out_bwd", (g, s // tm),
               [(dh, (tm, d), lambda j, i: (i, 0), wout, (None, kb, d), lambda j, i: (j, 0, 0), "nt", 0, 0)],
               [(ga, (tm, kb), cmap), (gb, (tm, kb), cmap), (ya, (None, tm, kb), gmap), (yb, (None, tm, kb), gmap)],
               [og, og, oc, oc], epi)


def _pl_forward(n4, wplg, p, wpl, h3, tm=512):
    s, d = n4.shape
    g, kb, _ = wplg.shape
    kp, nb = wpl.shape[1], wpl.shape[2]
    tm = _tile(s, tm)
    wplg_nat = wplg.reshape(g * kb, d)

    def epi(accs, ex, out):
        t, pe = accs
        out[0][...] = ex[0][...] + _sig(t) * pe
        out[1][...] = t
        out[2][...] = pe.astype(BF)

    rmap = lambda j, i: (i, 0)
    cmap = lambda j, i: (i, j)
    return _mm("pl_forward", (g, s // tm),
               [(n4, (tm, d), rmap, wplg_nat, (g * kb, nb), lambda j, i: (0, j), "nn", 0, 0),
                (p, (tm, kp), rmap, wpl, (None, kp, nb), lambda j, i: (j, 0, 0), "nn", 1, 0)],
               [(h3, (tm, nb), cmap)],
               [((s, d), F32, (tm, nb), cmap), ((s, d), F32, (tm, nb), cmap), ((s, d), BF, (tm, nb), cmap)],
               epi, nacc=2)


def _row_dx(name, dout, w, tm=512):
    s, n = dout.shape
    g, kb, _ = w.shape
    tm = _tile(s, tm)
    return _mm(name, (g, s // tm),
               [(dout, (tm, n), lambda j, i: (i, 0), w, (None, kb, n), lambda j, i: (j, 0, 0), "nt", 0, 0)], [],
               [((s, g * kb), F32, (tm, kb), lambda j, i: (i, j))], _store)[0]


def _in_proj_bwd_x(pieces, weights, tm=256, tn=256):
    s = pieces[0].shape[0]
    d = weights[0].shape[0]
    tm, tn = _tile(s, tm), _tile(d, tn)
    prods = [(pc, (tm, pc.shape[1]), lambda j, i: (i, 0), w, (tn, w.shape[1]), lambda j, i: (j, 0), "nt", 0, 0)
             for pc, w in zip(pieces, weights)]
    return _mm("in_proj_dx", (d // tn, s // tm), prods, [],
               [((s, d), F32, (tm, tn), lambda j, i: (i, j))], _store)[0]


def _split_w_in(w_in_g):
    g, d, nb = w_in_g.shape
    nat = jnp.transpose(w_in_g, (1, 0, 2)).reshape(d, g * nb)
    na, lat = 3 * NA_HEADS * NA_DIM, 2 * ML_RANK + ML_ROPE
    return nat[:, :na], nat[:, na:na + lat], nat[:, na + lat:na + lat + d], nat[:, na + lat + d:]


def _device_step(x, p, target, sp, w):
    s, d = x.shape
    rows = s // GRID_W
    cos, sin, rot, rot_t = _rope_consts(s)
    wqkv, wlat, wga, wgb = _split_w_in(w["w_in"])

    h1, ffn1_saved = _ffn_forward("ffn1", x, sp["ffn1_norm"], w["ffn1_w_gate"], w["ffn1_w_up"], w["ffn1_w_down"])
    u = _rms_fwd("mix_norm", h1, sp["mix_norm"])
    qkv = _mm_nn("in_qkv", u, wqkv, BF, tn=1024)
    lat = _mm_nn("in_lat", u, wlat, F32)
    ga = _mm_nn("in_ga", u, wga, F32, tn=1024)
    gb = _mm_nn("in_gb", u, wgb, F32, tn=1024)
    tb = _na_bias_tables(sp["na_rpb"], rows)
    oa = _na_fwd(qkv, tb)
    cq, ckv, kr = _mla_prep(lat, sp["q_a_norm"], sp["kv_a_norm"], cos, sin, rot)
    qf = _mla_q_proj(cq, w["w_uq"], cos, sin, rot)
    kf, vf = _mla_kv_proj(ckv, w["w_ukv"], kr)
    ob = _mla_fwd(qf, kf, vf)
    ya, yb, merged = _mix_merge(oa, ob, w["w_branch_a"], w["w_branch_b"], ga, gb)
    h2 = _mix_out(merged, w["w_out"], h1)
    h3, ffn2_saved = _ffn_forward("ffn2", h2, sp["ffn2_norm"], w["ffn2_w_gate"], w["ffn2_w_up"], w["ffn2_w_down"])
    n4 = _rms_fwd("pl_norm", h3, sp["pl_norm"])
    pb = p.astype(BF)
    h4, t, pe = _pl_forward(n4, w["w_pl_gate"], pb, w["w_pl"], h3)

    dw, dsp = {}, {}
    dh4, dsp["final_norm"], loss = _loss_head(h4, target, sp["final_norm"])
    dt, dpe = _pl_bwd_elem(dh4, pe, t)
    dw["w_pl"] = _grp_dw("pl_dw", pb, dpe)
    dw["w_pl_gate"] = _row_dw("plg_dw", n4, dt)
    dn4 = _row_dx("plg_dx", dt, w["w_pl_gate"])
    dh3, dsp["pl_norm"] = _rms_bwd("pl_dnorm", dn4, h3, sp["pl_norm"], dh4)
    dh2, dsp["ffn2_norm"], dw["ffn2_w_gate"], dw["ffn2_w_up"], dw["ffn2_w_down"] = _ffn_backward(
        "ffn2", dh3, h2, sp["ffn2_norm"], w["ffn2_w_gate"], w["ffn2_w_up"], w["ffn2_w_down"], ffn2_saved)

    dh2b = dh2.astype(BF)
    dw["w_out"] = _row_dw("out_dw", merged, dh2b)
    dya, dyb, dga, dgb = _mix_out_bwd(dh2b, w["w_out"], ga, gb, ya, yb)
    dw["w_branch_a"] = _grp_dw("bra_dw", oa, dya)
    dw["w_branch_b"] = _grp_dw("brb_dw", ob, dyb)
    doa = _grp_dx("bra_dx", dya, w["w_branch_a"]).astype(BF)
    dob = _grp_dx("brb_dx", dyb, w["w_branch_b"]).astype(BF)

    dqf, dkf, dvf = _mla_bwd(qf, kf, vf, dob)
    dqp, dkv, dkr = _mla_post(dqf, dkf, dvf, cos, sin, rot_t)
    dw["w_uq"] = _grp_dw("uq_dw", cq, dqp)
    dw["w_ukv"] = _grp_dw("ukv_dw", ckv, dkv)
    dcq = _grp_dx("uq_dx", dqp, w["w_uq"])
    dckv = _grp_dx("ukv_dx", dkv, w["w_ukv"])
    dlat, dsp["q_a_norm"], dsp["kv_a_norm"] = _mla_lat_bwd(dcq, dckv, dkr, lat, sp["q_a_norm"], sp["kv_a_norm"],
                                                         cos, sin, rot_t)
    dq_na, dk_na, dv_na, dtab = _na_bwd(qkv, tb, doa)
    dsp["na_rpb"] = _na_rpb_grad(dtab, rows)
    dqkv = jnp.concatenate([dq_na, dk_na.astype(BF), dv_na.astype(BF)], axis=1)

    pieces = [dqkv, dlat, dga, dgb]
    dwin = jnp.concatenate([_mm_tn("in_dw%d" % i, u, pc, BF) for i, pc in enumerate(pieces)], axis=1)
    dw["w_in"] = dwin.reshape(d, NDEV, -1).transpose(1, 0, 2)
    du = _in_proj_bwd_x(pieces, [wqkv, wlat, wga, wgb])
    dh1, dsp["mix_norm"] = _rms_bwd("mix_dnorm", du, h1, sp["mix_norm"], dh2)
    dx, dsp["ffn1_norm"], dw["ffn1_w_gate"], dw["ffn1_w_up"], dw["ffn1_w_down"] = _ffn_backward(
        "ffn1", dh1, x, sp["ffn1_norm"], w["ffn1_w_gate"], w["ffn1_w_up"], w["ffn1_w_down"], ffn1_saved)
    return loss, dx, dw, dsp


ANY = pl.BlockSpec(memory_space=pl.ANY)


def _coords():
    return lax.axis_index("x"), lax.axis_index("y"), lax.axis_index("c")


def _all_gather(shards):
    n = len(shards)

    def body(*refs):
        ins, outs = refs[:n], refs[n:2 * n]
        send_sems, recv_sems, local_sems = refs[2 * n:]
        x, y, c = _coords()
        me, sibling = (x, y, c), (x, y, 1 - c)
        chips = [(1 - x, y), (x, 1 - y), (1 - x, 1 - y)]

        def copy(i, k, block, to, src=None):
            px, py, pc = block
            dst = outs[i].at[4 * px + 2 * py + pc]
            return pltpu.make_async_remote_copy(
                src_ref=dst if src is None else src, dst_ref=dst, send_sem=send_sems.at[i, k],
                recv_sem=recv_sems.at[i, k], device_id=to, device_id_type=MESH)

        mine = [pltpu.make_async_copy(ins[i], outs[i].at[4 * x + 2 * y + c], local_sems.at[i]) for i in range(n)]
        for cp in mine:
            cp.start()
        first = []
        for i in range(n):
            first.append(copy(i, 0, me, sibling, src=ins[i]))
            first += [copy(i, 1 + j, me, (*chip, c), src=ins[i]) for j, chip in enumerate(chips)]
        for cp in first:
            cp.start()
        passed = []
        for i in range(n):
            for j, chip in enumerate(chips):
                copy(i, 1 + j, (*chip, c), me).wait_recv()
                fw = copy(i, 4 + j, (*chip, c), sibling)
                fw.start()
                passed.append(fw)
        for i in range(n):
            copy(i, 0, sibling, me).wait_recv()
            for j, chip in enumerate(chips):
                copy(i, 4 + j, (*chip, 1 - c), me).wait_recv()
        for cp in first + passed:
            cp.wait_send()
        for cp in mine:
            cp.wait()

    return pl.pallas_call(
        body, name="all_gather", in_specs=[ANY] * n, out_specs=[ANY] * n,
        out_shape=[jax.ShapeDtypeStruct((NDEV,) + a.shape, a.dtype) for a in shards],
        scratch_shapes=[pltpu.SemaphoreType.DMA((n, 7)), pltpu.SemaphoreType.DMA((n, 7)),
                        pltpu.SemaphoreType.DMA((n,))],
    )(*shards)


def _rs_sibling(parts):
    n = len(parts)

    def body(*refs):
        ins, outs = refs[:n], refs[n:2 * n]
        send_sems, recv_sems = refs[2 * n:]
        x, y, c = _coords()
        cps = [pltpu.make_async_remote_copy(
            src_ref=ins[i].at[:, 1 - c], dst_ref=outs[i], send_sem=send_sems.at[i], recv_sem=recv_sems.at[i],
            device_id=(x, y, 1 - c), device_id_type=MESH) for i in range(n)]
        for cp in cps:
            cp.start()
        for cp in cps:
            cp.wait_recv()
        for cp in cps:
            cp.wait_send()

    return pl.pallas_call(
        body, name="rs_sibling", in_specs=[ANY] * n, out_specs=[ANY] * n,
        out_shape=[jax.ShapeDtypeStruct((NCHIP,) + a.shape[2:], a.dtype) for a in parts],
        scratch_shapes=[pltpu.SemaphoreType.DMA((n,)), pltpu.SemaphoreType.DMA((n,))],
    )(*parts)


def _pair_sum(name, part, landed, core):
    _, _, r, c = part.shape
    tr = _row_tile(r, c)

    def body(core_ref, a_ref, b_ref, o_ref):
        o_ref[...] = (a_ref[...].astype(F32) + b_ref[...].astype(F32)).astype(o_ref.dtype)

    return pl.pallas_call(
        body, name=name,
        grid_spec=pltpu.PrefetchScalarGridSpec(
            num_scalar_prefetch=1, grid=(NCHIP, r // tr),
            in_specs=[pl.BlockSpec((None, None, tr, c), lambda j, i, core_ref: (j, core_ref[0], i, 0)),
                      pl.BlockSpec((None, tr, c), lambda j, i, core_ref: (j, i, 0))],
            out_specs=pl.BlockSpec((None, tr, c), lambda j, i, core_ref: (j, i, 0))),
        out_shape=jax.ShapeDtypeStruct(landed.shape, landed.dtype), compiler_params=_params(2),
    )(core, part, landed)


def _rs_chips(sums):
    n = len(sums)

    def body(*refs):
        ins, outs = refs[:n], refs[n:2 * n]
        send_sems, recv_sems, local_sems = refs[2 * n:]
        x, y, c = _coords()
        my_chip = 2 * x + y
        chips = [(1 - x, y), (x, 1 - y), (1 - x, 1 - y)]
        mine = [pltpu.make_async_copy(ins[i].at[my_chip], outs[i].at[my_chip], local_sems.at[i]) for i in range(n)]
        for cp in mine:
            cp.start()
        cps = []
        for i in range(n):
            for k, (px, py) in enumerate(chips):
                cps.append(pltpu.make_async_remote_copy(
                    src_ref=ins[i].at[2 * px + py], dst_ref=outs[i].at[my_chip], send_sem=send_sems.at[i, k],
                    recv_sem=recv_sems.at[i, k], device_id=(px, py, c), device_id_type=MESH))
        for cp in cps:
            cp.start()
        for i in range(n):
            for k, (px, py) in enumerate(chips):
                pltpu.make_async_remote_copy(
                    src_ref=ins[i].at[my_chip], dst_ref=outs[i].at[2 * px + py], send_sem=send_sems.at[i, k],
                    recv_sem=recv_sems.at[i, k], device_id=(px, py, c), device_id_type=MESH).wait_recv()
        for cp in cps:
            cp.wait_send()
        for cp in mine:
            cp.wait()

    return pl.pallas_call(
        body, name="rs_chips", in_specs=[ANY] * n, out_specs=[ANY] * n,
        out_shape=[jax.ShapeDtypeStruct(a.shape, a.dtype) for a in sums],
        scratch_shapes=[pltpu.SemaphoreType.DMA((n, 3)), pltpu.SemaphoreType.DMA((n, 3)),
                        pltpu.SemaphoreType.DMA((n,))],
    )(*sums)


def _gather_small(buf):
    def body(in_ref, out_ref, send_sems, recv_sems, local_sem):
        x, y, c = _coords()
        mine = pltpu.make_async_copy(in_ref, out_ref.at[4 * x + 2 * y + c], local_sem)
        mine.start()
        cps = []
        for k in range(1, NDEV):
            fx, fy, fc = (k >> 2) & 1, (k >> 1) & 1, k & 1
            peer = (x ^ fx, y ^ fy, c ^ fc)
            cps.append(pltpu.make_async_remote_copy(
                src_ref=in_ref, dst_ref=out_ref.at[4 * x + 2 * y + c], send_sem=send_sems.at[k - 1],
                recv_sem=recv_sems.at[k - 1], device_id=peer, device_id_type=MESH))
        for cp in cps:
            cp.start()
        for k in range(1, NDEV):
            fx, fy, fc = (k >> 2) & 1, (k >> 1) & 1, k & 1
            px, py, pc = x ^ fx, y ^ fy, c ^ fc
            pltpu.make_async_remote_copy(
                src_ref=in_ref, dst_ref=out_ref.at[4 * px + 2 * py + pc], send_sem=send_sems.at[k - 1],
                recv_sem=recv_sems.at[k - 1], device_id=(px, py, pc), device_id_type=MESH).wait_recv()
        for cp in cps:
            cp.wait_send()
        mine.wait()

    return pl.pallas_call(
        body, name="gather_small", in_specs=[ANY], out_specs=ANY,
        out_shape=jax.ShapeDtypeStruct((NDEV,) + buf.shape, buf.dtype),
        scratch_shapes=[pltpu.SemaphoreType.DMA((NDEV - 1,)), pltpu.SemaphoreType.DMA((NDEV - 1,)),
                        pltpu.SemaphoreType.DMA],
    )(buf)


def _adam_math(wv, g, m, v):
    m_new = B1 * m + (1.0 - B1) * g
    v_new = B2 * v + (1.0 - B2) * (g * g)
    m_hat = m_new / (1.0 - B1 ** STEP)
    v_hat = v_new / (1.0 - B2 ** STEP)
    return -LR * (m_hat / (jnp.sqrt(v_hat) + ADAM_EPS) + WD * wv), m_new, v_new


def _adam(name, parts, wv, m, v):
    npart, r, c = parts.shape
    tr = _row_tile(r, c)

    def body(p_ref, w_ref, m_ref, v_ref, g_ref, d_ref, mo_ref, vo_ref):
        g = p_ref[0].astype(F32)
        for j in range(1, npart):
            g = g + p_ref[j].astype(F32)
        g_ref[...] = g
        d_ref[...], mo_ref[...], vo_ref[...] = _adam_math(w_ref[...], g, m_ref[...], v_ref[...])

    row = pl.BlockSpec((tr, c), lambda i: (i, 0))
    return pl.pallas_call(
        body, name=name, grid=(r // tr,), in_specs=[pl.BlockSpec((npart, tr, c), lambda i: (0, i, 0)), row, row, row],
        out_specs=[row] * 4, out_shape=[jax.ShapeDtypeStruct((r, c), F32)] * 4, compiler_params=_params(1),
    )(parts, wv, m, v)


SHARDED = ("ffn1_w_gate", "ffn1_w_up", "ffn1_w_down", "w_in", "w_uq", "w_ukv", "w_branch_a", "w_branch_b", "w_out",
           "ffn2_w_gate", "ffn2_w_up", "ffn2_w_down", "w_pl", "w_pl_gate")
REPLICATED = ("ffn1_norm", "mix_norm", "q_a_norm", "kv_a_norm", "na_rpb", "ffn2_norm", "pl_norm", "final_norm")
WEIGHTS = ("ffn1_norm", "ffn1_w_gate", "ffn1_w_up", "ffn1_w_down", "mix_norm", "w_in", "q_a_norm", "w_uq",
           "kv_a_norm", "w_ukv", "na_rpb", "w_branch_a", "w_branch_b", "w_out", "ffn2_norm", "ffn2_w_gate",
           "ffn2_w_up", "ffn2_w_down", "pl_norm", "w_pl", "w_pl_gate", "final_norm")
SMALL_W = 2048


def _pack_small(vals):
    rows = []
    for name in REPLICATED:
        flat = vals[name].reshape(-1).astype(F32)
        n = -(-flat.shape[0] // SMALL_W) * SMALL_W
        rows.append(jnp.pad(flat, (0, n - flat.shape[0])).reshape(-1, SMALL_W))
    return jnp.concatenate(rows, axis=0)


def _unpack_small(buf, shapes):
    out, r = {}, 0
    for name in REPLICATED:
        size = int(np.prod(shapes[name]))
        nrow = -(-size // SMALL_W)
        out[name] = buf[r:r + nrow].reshape(-1)[:size].reshape(shapes[name])
        r += nrow
    return out


def kernel(x, p, ffn1_norm, ffn1_w_gate, ffn1_w_up, ffn1_w_down, mix_norm, w_in, q_a_norm, w_uq, kv_a_norm, w_ukv, na_rpb, w_branch_a, w_branch_b, w_out, ffn2_norm, ffn2_w_gate, ffn2_w_up, ffn2_w_down, pl_norm, w_pl, w_pl_gate, final_norm, loss_target, m_ffn1_norm, m_ffn1_w_gate, m_ffn1_w_up, m_ffn1_w_down, m_mix_norm, m_w_in, m_q_a_norm, m_w_uq, m_kv_a_norm, m_w_ukv, m_na_rpb, m_w_branch_a, m_w_branch_b, m_w_out, m_ffn2_norm, m_ffn2_w_gate, m_ffn2_w_up, m_ffn2_w_down, m_pl_norm, m_w_pl, m_w_pl_gate, m_final_norm, v_ffn1_norm, v_ffn1_w_gate, v_ffn1_w_up, v_ffn1_w_down, v_mix_norm, v_w_in, v_q_a_norm, v_w_uq, v_kv_a_norm, v_w_ukv, v_na_rpb, v_w_branch_a, v_w_branch_b, v_w_out, v_ffn2_norm, v_ffn2_w_gate, v_ffn2_w_up, v_ffn2_w_down, v_pl_norm, v_w_pl, v_w_pl_gate, v_final_norm):
    args = dict(locals())
    wts = {n: args[n] for n in WEIGHTS}
    mom = {n: args["m_" + n] for n in WEIGHTS}
    var = {n: args["v_" + n] for n in WEIGHTS}
    shapes = {n: wts[n].shape for n in WEIGHTS}
    core = lax.axis_index("c").astype(jnp.int32).reshape(1)

    gathered = _all_gather([wts[n][0].astype(BF) for n in SHARDED])
    w = dict(zip(SHARDED, gathered))

    sp = {n: wts[n].reshape(1, -1) for n in REPLICATED if n != "na_rpb"}
    sp["na_rpb"] = wts["na_rpb"][0]
    loss_part, grad_x, dw, dsp = _device_step(x[0], p[0, 0], loss_target[0], sp, w)

    parts = [dw[n].reshape((NCHIP, 2) + dw[n].shape[1:]) for n in SHARDED]
    landed = _rs_sibling(parts)
    sums = [_pair_sum("pair_sum_" + n, a, b, core) for n, a, b in zip(SHARDED, parts, landed)]
    chip_parts = dict(zip(SHARDED, _rs_chips(sums)))
    out = {}
    for n in SHARDED:
        g, dlt, m_new, v_new = _adam("adam_" + n, chip_parts[n], wts[n][0], mom[n][0], var[n][0])
        out[n] = tuple(a[None] for a in (g, dlt, m_new, v_new))

    small = jnp.concatenate([_pack_small(dsp), jnp.pad(loss_part, ((0, 0), (0, SMALL_W - loss_part.shape[1])))], 0)
    pad_rows = -small.shape[0] % 8
    small = jnp.pad(small, ((0, pad_rows), (0, 0)))
    every = _gather_small(small)
    zeros = jnp.zeros((1 + pad_rows, SMALL_W), F32)
    pack = lambda d: jnp.concatenate([_pack_small(d), zeros], 0)
    g_s, d_s, m_s, v_s = _adam("adam_small", every, pack(wts), pack(mom), pack(var))
    n_rows = small.shape[0] - 1 - pad_rows
    loss = g_s[n_rows, 0]
    small_out = [_unpack_small(b, shapes) for b in (g_s, d_s, m_s, v_s)]
    for n in REPLICATED:
        out[n] = tuple(b[n] for b in small_out)

    res = [loss, grad_x[None]]
    for k in range(4):
        res += [out[n][k] for n in WEIGHTS]
    return tuple(res)
```

```python
import functools

import numpy as np
import jax
import jax.numpy as jnp
from jax import lax
from jax.experimental import pallas as pl
from jax.experimental.pallas import tpu as pltpu

F32 = jnp.float32
BF = jnp.bfloat16
MESH = pl.DeviceIdType.MESH

NDEV = 8
NCHIP = 4
VMEM_LIMIT = 56 * 1024 * 1024
EPS = 1e-6
NEG = -1e30
GRID_W = 64
NA_HEADS, NA_DIM = 8, 128
NA_ROWS_WIN, NA_COLS_WIN = 8, 16
NA_HG = 4
ML_HEADS, ML_NOPE, ML_ROPE, ML_V = 8, 128, 64, 128
ML_QK = ML_NOPE + ML_ROPE
ML_RANK = 512
ROPE_THETA = 10000.0
LR, B1, B2, ADAM_EPS, WD, STEP = 0.001, 0.9, 0.999, 1e-08, 0.01, 10
HI = lax.Precision.HIGHEST

_DN = {"nn": (((1,), (0,)), ((), ())), "nt": (((1,), (1,)), ((), ())), "tn": (((0,), (0,)), ((), ()))}


def _params(n):
    return pltpu.CompilerParams(dimension_semantics=("arbitrary",) * n, vmem_limit_bytes=VMEM_LIMIT)


def _sig(v):
    return jax.nn.sigmoid(v)


ANY = pl.BlockSpec(memory_space=pl.ANY)


def _coords():
    return lax.axis_index("x"), lax.axis_index("y"), lax.axis_index("c")


class _Part:
    inputs, out_shapes, sem_shapes, results = (), (), (), None


class _GatherPart(_Part):
    def __init__(self, names, shards):
        n = len(shards)
        self.names, self.inputs = list(names), list(shards)
        self.out_shapes = [jax.ShapeDtypeStruct((NDEV,) + a.shape, a.dtype) for a in shards]
        self.sem_shapes = [pltpu.SemaphoreType.DMA((n, 7)), pltpu.SemaphoreType.DMA((n, 7)),
                           pltpu.SemaphoreType.DMA((n,))]

    def _plan(self, ins, outs, sems):
        send_sems, recv_sems, local_sems = sems
        x, y, c = _coords()
        chips = [(1 - x, y), (x, 1 - y), (1 - x, 1 - y)]

        def copy(i, k, block, to, src=None):
            px, py, pc = block
            dst = outs[i].at[4 * px + 2 * py + pc]
            return pltpu.make_async_remote_copy(
                src_ref=dst if src is None else src, dst_ref=dst, send_sem=send_sems.at[i, k],
                recv_sem=recv_sems.at[i, k], device_id=to, device_id_type=MESH)

        n = len(ins)
        mine = [pltpu.make_async_copy(ins[i], outs[i].at[4 * x + 2 * y + c], local_sems.at[i]) for i in range(n)]
        first = []
        for i in range(n):
            first.append(copy(i, 0, (x, y, c), (x, y, 1 - c), src=ins[i]))
            first += [copy(i, 1 + j, (x, y, c), (*chip, c), src=ins[i]) for j, chip in enumerate(chips)]
        return copy, mine, first, chips, (x, y, c)

    def start(self, ins, outs, sems):
        _, mine, first, _, _ = self._plan(ins, outs, sems)
        for cp in mine + first:
            cp.start()

    def finish(self, ins, outs, sems):
        copy, mine, first, chips, (x, y, c) = self._plan(ins, outs, sems)
        n = len(ins)
        passed = []
        for i in range(n):
            for j, chip in enumerate(chips):
                copy(i, 1 + j, (*chip, c), (x, y, c)).wait_recv()
                fw = copy(i, 4 + j, (*chip, c), (x, y, 1 - c))
                fw.start()
                passed.append(fw)
        for i in range(n):
            copy(i, 0, (x, y, 1 - c), (x, y, c)).wait_recv()
            for j, chip in enumerate(chips):
                copy(i, 4 + j, (*chip, 1 - c), (x, y, c)).wait_recv()
        for cp in first + passed:
            cp.wait_send()
        for cp in mine:
            cp.wait()


class _SiblingPart(_Part):
    def __init__(self, names, parts):
        n = len(parts)
        self.names, self.inputs = list(names), list(parts)
        self.out_shapes = [jax.ShapeDtypeStruct((NCHIP,) + a.shape[2:], a.dtype) for a in parts]
        self.sem_shapes = [pltpu.SemaphoreType.DMA((n,)), pltpu.SemaphoreType.DMA((n,))]

    def _copies(self, ins, outs, sems):
        x, y, c = _coords()
        return [pltpu.make_async_remote_copy(
            src_ref=ins[i].at[:, 1 - c], dst_ref=outs[i], send_sem=sems[0].at[i], recv_sem=sems[1].at[i],
            device_id=(x, y, 1 - c), device_id_type=MESH) for i in range(len(ins))]

    def start(self, ins, outs, sems):
        for cp in self._copies(ins, outs, sems):
            cp.start()

    def finish(self, ins, outs, sems):
        cps = self._copies(ins, outs, sems)
        for cp in cps:
            cp.wait_recv()
        for cp in cps:
            cp.wait_send()


class _ChipsPart(_Part):
    def __init__(self, names, sums):
        n = len(sums)
        self.names, self.inputs = list(names), list(sums)
        self.out_shapes = [jax.ShapeDtypeStruct(a.shape, a.dtype) for a in sums]
        self.sem_shapes = [pltpu.SemaphoreType.DMA((n, 3)), pltpu.SemaphoreType.DMA((n, 3)),
                           pltpu.SemaphoreType.DMA((n,))]

    def _plan(self, ins, outs, sems):
        send_sems, recv_sems, local_sems = sems
        x, y, c = _coords()
        my_chip = 2 * x + y
        chips = [(1 - x, y), (x, 1 - y), (1 - x, 1 - y)]
        n = len(ins)
        mine = [pltpu.make_async_copy(ins[i].at[my_chip], outs[i].at[my_chip], local_sems.at[i]) for i in range(n)]
        sends, recvs = [], []
        for i in range(n):
            for k, (px, py) in enumerate(chips):
                sends.append(pltpu.make_async_remote_copy(
                    src_ref=ins[i].at[2 * px + py], dst_ref=outs[i].at[my_chip], send_sem=send_sems.at[i, k],
                    recv_sem=recv_sems.at[i, k], device_id=(px, py, c), device_id_type=MESH))
                recvs.append(pltpu.make_async_remote_copy(
                    src_ref=ins[i].at[my_chip], dst_ref=outs[i].at[2 * px + py], send_sem=send_sems.at[i, k],
                    recv_sem=recv_sems.at[i, k], device_id=(px, py, c), device_id_type=MESH))
        return mine, sends, recvs

    def start(self, ins, outs, sems):
        mine, sends, _ = self._plan(ins, outs, sems)
        for cp in mine + sends:
            cp.start()

    def finish(self, ins, outs, sems):
        mine, sends, recvs = self._plan(ins, outs, sems)
        for cp in recvs:
            cp.wait_recv()
        for cp in sends:
            cp.wait_send()
        for cp in mine:
            cp.wait()


def _call(name, body, grid, in_specs, out_specs, out_shape, args, comm=()):
    comm = [p for p in comm if p is not None]
    single = not isinstance(out_shape, (list, tuple))
    o_specs = [out_specs] if single else list(out_specs)
    o_shape = [out_shape] if single else list(out_shape)
    n_in, n_out = len(in_specs), len(o_specs)
    c_in = [a for p in comm for a in p.inputs]
    c_out = [s for p in comm for s in p.out_shapes]
    c_sem = [s for p in comm for s in p.sem_shapes]

    def wrapped(*refs):
        ins, outs = refs[:n_in], refs[n_in + len(c_in):n_in + len(c_in) + n_out]
        pos = [n_in, n_in + len(c_in) + n_out, n_in + len(c_in) + n_out + len(c_out)]
        split = []
        for p in comm:
            sizes = [len(p.inputs), len(p.out_shapes), len(p.sem_shapes)]
            split.append([refs[o:o + n] for o, n in zip(pos, sizes)])
            pos = [o + n for o, n in zip(pos, sizes)]
        ids = [pl.program_id(a) for a in range(len(grid))]

        def run(which, when):
            def go():
                for p, cut in zip(comm, split):
                    getattr(p, which)(*cut)
            if not comm:
                return
            if ids:
                pl.when(functools.reduce(jnp.logical_and, when))(go)
            else:
                go()

        run("start", [i == 0 for i in ids])
        body(*ins, *outs)
        run("finish", [i == g - 1 for i, g in zip(ids, grid)])

    res = pl.pallas_call(
        wrapped, name=name, grid=grid, in_specs=list(in_specs) + [ANY] * len(c_in),
        out_specs=o_specs + [ANY] * len(c_out), out_shape=o_shape + c_out, scratch_shapes=c_sem,
        compiler_params=_params(len(grid)),
    )(*args, *c_in)
    pos = n_out
    for p in comm:
        p.results = list(res[pos:pos + len(p.out_shapes)])
        pos += len(p.out_shapes)
    return res[0] if single else list(res[:n_out])


def _comm_only(name, comm):
    def body(o_ref):
        o_ref[...] = jnp.zeros_like(o_ref)

    _call(name, body, (), [], pl.BlockSpec(memory_space=pltpu.VMEM), jax.ShapeDtypeStruct((8, 128), F32), [], comm)


def _mm(name, grid, prods, extras, outs, epi, nacc=1, comm=()):
    n_p, n_e = len(prods), len(extras)

    def body(*refs):
        ab, ex, out = refs[:2 * n_p], refs[2 * n_p:2 * n_p + n_e], refs[2 * n_p + n_e:]
        accs = [None] * nacc
        for i, prod in enumerate(prods):
            dn, acc, loop = prod[6], prod[7], prod[8]
            a_ref, b_ref = ab[2 * i], ab[2 * i + 1]
            if loop:
                for g in range(loop):
                    t = lax.dot_general(a_ref[g], b_ref[g], _DN[dn], preferred_element_type=F32)
                    accs[acc] = t if accs[acc] is None else accs[acc] + t
            else:
                t = lax.dot_general(a_ref[...], b_ref[...], _DN[dn], preferred_element_type=F32)
                accs[acc] = t if accs[acc] is None else accs[acc] + t
        epi(accs, ex, out)

    in_specs, args = [], []
    for prod in prods:
        in_specs += [pl.BlockSpec(prod[1], prod[2]), pl.BlockSpec(prod[4], prod[5])]
        args += [prod[0], prod[3]]
    for e, e_blk, e_map in extras:
        in_specs.append(pl.BlockSpec(e_blk, e_map))
        args.append(e)
    return _call(name, body, grid, in_specs, [pl.BlockSpec(blk, mp) for _, _, blk, mp in outs],
                 [jax.ShapeDtypeStruct(s, d) for s, d, _, _ in outs], args, comm)


def _store(accs, ex, out):
    out[0][...] = accs[0].astype(out[0].dtype)


def _row_tile(r, c, row_bytes=4, budget=1 << 20):
    best = None
    for t in range(16, r + 1, 16):
        if r % t == 0 and t * c * row_bytes <= budget:
            best = t
    return best or r


def _tile(n, want):
    t = min(n, want)
    assert n % t == 0, (n, want)
    return t


def _mm_nn(name, a, b, out_dtype, tm=512, tn=512, comm=()):
    m, k = a.shape
    n = b.shape[1]
    tm, tn = _tile(m, tm), (tn if n % tn == 0 else n)
    return _mm(name, (n // tn, m // tm),
               [(a, (tm, k), lambda j, i: (i, 0), b, (k, tn), lambda j, i: (0, j), "nn", 0, 0)], [],
               [((m, n), out_dtype, (tm, tn), lambda j, i: (i, j))], _store, comm=comm)[0]


def _mm_tn(name, a, b, out_dtype, ta=512, tb=512, scale=None):
    t, ka = a.shape
    nb = b.shape[1]
    ta, tb = (ta if ka % ta == 0 else ka), (tb if nb % tb == 0 else nb)

    def epi(accs, ex, out):
        v = accs[0] if scale is None else accs[0] * scale
        out[0][...] = v.astype(out[0].dtype)

    return _mm(name, (ka // ta, nb // tb),
               [(a, (t, ta), lambda i, j: (0, i), b, (t, tb), lambda i, j: (0, j), "tn", 0, 0)], [],
               [((ka, nb), out_dtype, (ta, tb), lambda i, j: (i, j))], epi)[0]


def _rms_fwd(name, x, g, tm=256):
    s, d = x.shape
    tm = _tile(s, tm)

    def body(x_ref, g_ref, o_ref):
        v = x_ref[...]
        o_ref[...] = (v * lax.rsqrt(jnp.mean(v * v, axis=-1, keepdims=True) + EPS) * g_ref[...]).astype(o_ref.dtype)

    return pl.pallas_call(
        body, name=name, grid=(s // tm,),
        in_specs=[pl.BlockSpec((tm, d), lambda i: (i, 0)), pl.BlockSpec((1, d), lambda i: (0, 0))],
        out_specs=pl.BlockSpec((tm, d), lambda i: (i, 0)), out_shape=jax.ShapeDtypeStruct((s, d), BF),
        compiler_params=_params(1))(x, g)


def _acc_rows(ref, part, i):
    @pl.when(i == 0)
    def _():
        ref[...] = part

    @pl.when(i > 0)
    def _():
        ref[...] += part


def _rms_bwd_math(dn, v, g):
    rstd = lax.rsqrt(jnp.mean(v * v, axis=-1, keepdims=True) + EPS)
    xh = v * rstd
    dxh = dn * g
    dx = rstd * (dxh - xh * jnp.mean(dxh * xh, axis=-1, keepdims=True))
    return dx, jnp.sum(dn * xh, axis=0, keepdims=True)


def _rms_bwd(name, dn, x, g, resid, tm=256):
    s, d = x.shape
    tm = _tile(s, tm)

    def body(dn_ref, x_ref, g_ref, r_ref, dx_ref, dg_ref):
        dx, part = _rms_bwd_math(dn_ref[...].astype(F32), x_ref[...], g_ref[...])
        dx_ref[...] = r_ref[...] + dx
        _acc_rows(dg_ref, part, pl.program_id(0))

    row = pl.BlockSpec((tm, d), lambda i: (i, 0))
    one = pl.BlockSpec((1, d), lambda i: (0, 0))
    return pl.pallas_call(
        body, name=name, grid=(s // tm,), in_specs=[row, row, one, row], out_specs=[row, one],
        out_shape=[jax.ShapeDtypeStruct((s, d), F32), jax.ShapeDtypeStruct((1, d), F32)],
        compiler_params=_params(1))(dn, x, g, resid)


def _loss_head(h, target, g, tm=256):
    s, d = h.shape
    tm = _tile(s, tm)

    def body(h_ref, t_ref, g_ref, dh_ref, dg_ref, loss_ref):
        v, gv = h_ref[...], g_ref[...]
        rstd = lax.rsqrt(jnp.mean(v * v, axis=-1, keepdims=True) + EPS)
        xh = v * rstd
        err = xh * gv - t_ref[...]
        part_loss = 0.5 * jnp.sum(jnp.mean(err * err, axis=-1, keepdims=True), axis=0, keepdims=True)
        dy = err * (1.0 / d)
        dxh = dy * gv
        dh_ref[...] = rstd * (dxh - xh * jnp.mean(dxh * xh, axis=-1, keepdims=True))
        i = pl.program_id(0)
        _acc_rows(dg_ref, jnp.sum(dy * xh, axis=0, keepdims=True), i)
        _acc_rows(loss_ref, jnp.broadcast_to(part_loss, loss_ref.shape), i)

    row = pl.BlockSpec((tm, d), lambda i: (i, 0))
    one = pl.BlockSpec((1, d), lambda i: (0, 0))
    return pl.pallas_call(
        body, name="loss_head", grid=(s // tm,), in_specs=[row, row, one],
        out_specs=[row, one, pl.BlockSpec((1, 128), lambda i: (0, 0))],
        out_shape=[jax.ShapeDtypeStruct((s, d), F32), jax.ShapeDtypeStruct((1, d), F32),
                   jax.ShapeDtypeStruct((1, 128), F32)],
        compiler_params=_params(1))(h, target, g)


def _pl_bwd_elem(dh, pe, t, tm=256):
    s, d = dh.shape
    tm = _tile(s, tm)

    def body(dh_ref, pe_ref, t_ref, dt_ref, dpe_ref):
        dh_v, sg = dh_ref[...], _sig(t_ref[...])
        dt_ref[...] = (dh_v * pe_ref[...].astype(F32) * sg * (1.0 - sg)).astype(BF)
        dpe_ref[...] = (dh_v * sg).astype(BF)

    row = pl.BlockSpec((tm, d), lambda i: (i, 0))
    return pl.pallas_call(
        body, name="pl_bwd_elem", grid=(s // tm,), in_specs=[row, row, row], out_specs=[row, row],
        out_shape=[jax.ShapeDtypeStruct((s, d), BF)] * 2, compiler_params=_params(1))(dh, pe, t)


def _ffn_up(name, xn, wg, wu, tm=512, comm=()):
    s, d = xn.shape
    g, _, fb = wg.shape
    tm = _tile(s, tm)

    def epi(accs, ex, out):
        hg, hu = accs
        out[0][...] = hg.astype(BF)
        out[1][...] = hu.astype(BF)
        out[2][...] = (hg * _sig(hg) * hu).astype(BF)

    a_map = lambda j, i: (i, 0)
    w_map = lambda j, i: (j, 0, 0)
    o = ((g, s, fb), BF, (None, tm, fb), lambda j, i: (j, i, 0))
    return _mm(name, (g, s // tm),
               [(xn, (tm, d), a_map, wg, (None, d, fb), w_map, "nn", 0, 0),
                (xn, (tm, d), a_map, wu, (None, d, fb), w_map, "nn", 1, 0)], [], [o, o, o], epi, nacc=2, comm=comm)


def _ffn_down(name, a, wd, resid, tm=512, tn=512, comm=()):
    g, s, fb = a.shape
    d = wd.shape[2]
    tm, tn = _tile(s, tm), _tile(d, tn)

    def epi(accs, ex, out):
        out[0][...] = ex[0][...] + 0.5 * accs[0]

    return _mm(name, (d // tn, s // tm),
               [(a, (g, tm, fb), lambda j, i: (0, i, 0), wd, (g, fb, tn), lambda j, i: (0, 0, j), "nn", 0, g)],
               [(resid, (tm, tn), lambda j, i: (i, j))],
               [((s, d), F32, (tm, tn), lambda j, i: (i, j))], epi, comm=comm)[0]


def _ffn_bwd_act(name, dh, wd, hg, hu, tm=512, comm=()):
    s, d = dh.shape
    g, fb, _ = wd.shape
    tm = _tile(s, tm)

    def epi(accs, ex, out):
        da = 0.5 * accs[0]
        hg_v, hu_v = ex[0][...].astype(F32), ex[1][...].astype(F32)
        sg = _sig(hg_v)
        out[0][...] = (da * hu_v * (sg * (1.0 + hg_v * (1.0 - sg)))).astype(BF)
        out[1][...] = (da * (hg_v * sg)).astype(BF)

    blk = (None, tm, fb)
    gmap = lambda j, i: (j, i, 0)
    return _mm(name, (g, s // tm),
               [(dh, (tm, d), lambda j, i: (i, 0), wd, (None, fb, d), lambda j, i: (j, 0, 0), "nt", 0, 0)],
               [(hg, blk, gmap), (hu, blk, gmap)],
               [((g, s, fb), BF, blk, gmap), ((g, s, fb), BF, blk, gmap)], epi, comm=comm)


def _ffn_bwd_wd(name, a, dh, tn=512, comm=()):
    g, s, fb = a.shape
    d = dh.shape[1]
    tn = _tile(d, tn)

    def epi(accs, ex, out):
        out[0][...] = (0.5 * accs[0]).astype(BF)

    return _mm(name, (g, d // tn),
               [(a, (None, s, fb), lambda j, i: (j, 0, 0), dh, (s, tn), lambda j, i: (0, i), "tn", 0, 0)], [],
               [((g, fb, d), BF, (None, fb, tn), lambda j, i: (j, 0, i))], epi, comm=comm)[0]


def _ffn_bwd_wup(name, xn, dhg, dhu, tk=512, comm=()):
    s, d = xn.shape
    g, _, fb = dhg.shape
    tk = _tile(d, tk)

    def epi(accs, ex, out):
        out[0][...] = accs[0].astype(BF)
        out[1][...] = accs[1].astype(BF)

    a_map = lambda j, i: (0, i)
    b_map = lambda j, i: (j, 0, 0)
    o = ((g, d, fb), BF, (None, tk, fb), lambda j, i: (j, i, 0))
    return _mm(name, (g, d // tk),
               [(xn, (s, tk), a_map, dhg, (None, s, fb), b_map, "tn", 0, 0),
                (xn, (s, tk), a_map, dhu, (None, s, fb), b_map, "tn", 1, 0)], [], [o, o], epi, nacc=2, comm=comm)


def _ffn_bwd_x(name, dhg, dhu, wg, wu, tm=256, tn=256, comm=()):
    g, s, fb = dhg.shape
    d = wg.shape[1]
    tm, tn = _tile(s, tm), _tile(d, tn)
    a_blk, a_map = (g, tm, fb), lambda j, i: (0, i, 0)
    b_blk, b_map = (g, tn, fb), lambda j, i: (0, j, 0)
    return _mm(name, (d // tn, s // tm),
               [(dhg, a_blk, a_map, wg, b_blk, b_map, "nt", 0, g), (dhu, a_blk, a_map, wu, b_blk, b_map, "nt", 0, g)],
               [], [((s, d), F32, (tm, tn), lambda j, i: (i, j))], _store, comm=comm)[0]


def _ffn_forward(tag, h, gain, wg, wu, get_wd, up_comm=(), down_comm=()):
    xn = _rms_fwd(tag + "_norm", h, gain)
    hg, hu, a = _ffn_up(tag + "_up", xn, wg, wu, comm=up_comm)
    return _ffn_down(tag + "_down", a, get_wd(), h, comm=down_comm), (xn, hg, hu, a)


def _na_geometry(rows):
    kh = min(NA_ROWS_WIN, rows)
    cols = np.arange(GRID_W)
    col_start = np.clip(cols - NA_COLS_WIN // 2, 0, GRID_W - NA_COLS_WIN)
    mask = (cols[None, :] >= col_start[:, None]) & (cols[None, :] < col_start[:, None] + NA_COLS_WIN)
    dc = np.clip(cols[None, :] - cols[:, None], -(NA_COLS_WIN - 1), NA_COLS_WIN - 1) + (NA_COLS_WIN - 1)
    return kh, mask, dc


def _na_bias_tables(rpb, rows):
    kh, mask, dc = _na_geometry(rows)
    t = jnp.where(jnp.asarray(mask)[None, None], rpb[:, :, dc], NEG)
    tb = jnp.stack([t[:, d0:d0 + kh] for d0 in range(NA_ROWS_WIN)], 0)
    return tb.transpose(0, 1, 3, 2, 4).reshape(NA_ROWS_WIN, NA_HEADS, GRID_W, kh * GRID_W)


def _na_row_start(r, rows, kh):
    return jnp.clip(r - kh // 2, 0, rows - kh)


def _na_specs(s, rows, kh):
    hw = NA_HG * NA_DIM
    nq = NA_HEADS // NA_HG
    q_spec = pl.BlockSpec((GRID_W, hw), lambda j, r: (r, j))
    k_spec = pl.BlockSpec((s, hw), lambda j, r: (0, nq + j))
    v_spec = pl.BlockSpec((s, hw), lambda j, r: (0, 2 * nq + j))
    b_spec = pl.BlockSpec((None, NA_HG, GRID_W, kh * GRID_W),
                          lambda j, r: (_na_row_start(r, rows, kh) - r + NA_ROWS_WIN - 1, j, 0, 0))
    return q_spec, k_spec, v_spec, b_spec, hw, nq


def _na_probs(q, k, bias):
    sc = lax.dot_general(q, k, _DN["nt"], preferred_element_type=F32) * (NA_DIM ** -0.5) + bias
    e = jnp.exp(sc - jnp.max(sc, axis=-1, keepdims=True))
    return e / jnp.sum(e, axis=-1, keepdims=True)


def _na_fwd(qkv, tb, comm=()):
    s = qkv.shape[0]
    rows = s // GRID_W
    kh = min(NA_ROWS_WIN, rows)
    q_spec, k_spec, v_spec, b_spec, hw, nq = _na_specs(s, rows, kh)

    def body(q_ref, k_ref, v_ref, b_ref, o_ref):
        r = pl.program_id(1)
        start = pl.multiple_of(_na_row_start(r, rows, kh) * GRID_W, GRID_W)
        for h in range(NA_HG):
            cs = slice(h * NA_DIM, (h + 1) * NA_DIM)
            p = _na_probs(q_ref[:, cs], k_ref[pl.ds(start, kh * GRID_W), cs], b_ref[h])
            o_ref[:, cs] = jnp.dot(p.astype(BF), v_ref[pl.ds(start, kh * GRID_W), cs],
                                   preferred_element_type=F32).astype(BF)

    return _call("na_fwd", body, (nq, rows), [q_spec, k_spec, v_spec, b_spec],
                 pl.BlockSpec((GRID_W, hw), lambda j, r: (r, j)),
                 jax.ShapeDtypeStruct((s, NA_HEADS * NA_DIM), BF), [qkv, qkv, qkv, tb], comm)


def _na_bwd(qkv, tb, do, comm=()):
    s = qkv.shape[0]
    rows = s // GRID_W
    kh = min(NA_ROWS_WIN, rows)
    q_spec, k_spec, v_spec, b_spec, hw, nq = _na_specs(s, rows, kh)
    nd = 2 * NA_ROWS_WIN - 1

    def body(q_ref, k_ref, v_ref, b_ref, do_ref, dq_ref, dk_ref, dv_ref, dt_ref):
        r = pl.program_id(1)

        @pl.when(r == 0)
        def _():
            dk_ref[...] = jnp.zeros_like(dk_ref)
            dv_ref[...] = jnp.zeros_like(dv_ref)
            dt_ref[...] = jnp.zeros_like(dt_ref)

        rs = _na_row_start(r, rows, kh)
        d0 = rs - r + NA_ROWS_WIN - 1
        win = pl.ds(pl.multiple_of(rs * GRID_W, GRID_W), kh * GRID_W)
        for h in range(NA_HG):
            cs = slice(h * NA_DIM, (h + 1) * NA_DIM)
            q, k, v, do_h = q_ref[:, cs], k_ref[win, cs], v_ref[win, cs], do_ref[:, cs]
            p = _na_probs(q, k, b_ref[h])
            dp = lax.dot_general(do_h, v, _DN["nt"], preferred_element_type=F32)
            ds = p * (dp - jnp.sum(p * dp, axis=-1, keepdims=True))
            for i in range(kh):
                dt_ref[h, d0 + i] += ds[:, i * GRID_W:(i + 1) * GRID_W]
            dsb = (ds * (NA_DIM ** -0.5)).astype(BF)
            dq_ref[:, cs] = jnp.dot(dsb, k, preferred_element_type=F32).astype(BF)
            dk_ref[win, cs] += lax.dot_general(dsb, q, _DN["tn"], preferred_element_type=F32)
            dv_ref[win, cs] += lax.dot_general(p.astype(BF), do_h, _DN["tn"], preferred_element_type=F32)

    width = NA_HEADS * NA_DIM
    whole = pl.BlockSpec((s, hw), lambda j, r: (0, j))
    return _call(
        "na_bwd", body, (nq, rows),
        [q_spec, k_spec, v_spec, b_spec, pl.BlockSpec((GRID_W, hw), lambda j, r: (r, j))],
        [pl.BlockSpec((GRID_W, hw), lambda j, r: (r, j)), whole, whole,
         pl.BlockSpec((NA_HG, nd, GRID_W, GRID_W), lambda j, r: (j, 0, 0, 0))],
        [jax.ShapeDtypeStruct((s, width), BF), jax.ShapeDtypeStruct((s, width), F32),
         jax.ShapeDtypeStruct((s, width), F32), jax.ShapeDtypeStruct((NA_HEADS, nd, GRID_W, GRID_W), F32)],
        [qkv, qkv, qkv, tb, do], comm)


def _na_rpb_grad(dt, rows):
    _, mask, dc = _na_geometry(rows)
    nd, nc = 2 * NA_ROWS_WIN - 1, 2 * NA_COLS_WIN - 1
    onehot = np.zeros((GRID_W * GRID_W, 128), np.float32)
    onehot[np.arange(GRID_W * GRID_W), dc.reshape(-1)] = mask.reshape(-1).astype(np.float32)
    flat = dt.reshape(NA_HEADS * nd, GRID_W * GRID_W)

    def body(a_ref, e_ref, o_ref):
        o_ref[...] = jnp.dot(a_ref[...], e_ref[...], precision=HI, preferred_element_type=F32)

    out = pl.pallas_call(body, name="na_rpb_grad", out_shape=jax.ShapeDtypeStruct((NA_HEADS * nd, 128), F32),
                         compiler_params=_params(0))(flat, jnp.asarray(onehot))
    return out[:, :nc].reshape(NA_HEADS, nd, nc)


def _rope_consts(s):
    pos = np.arange(s, dtype=np.float32)
    inv = (1.0 / (ROPE_THETA ** (np.arange(0, ML_ROPE, 2, dtype=np.float32) / ML_ROPE))).astype(np.float32)
    ang = pos[:, None] * inv[None, :]
    cos, sin = np.cos(ang).astype(np.float32), np.sin(ang).astype(np.float32)
    half = ML_ROPE // 2
    rot = np.zeros((ML_ROPE, ML_ROPE), np.float32)
    rot[np.arange(half) + half, np.arange(half)] = -1.0
    rot[np.arange(half), np.arange(half) + half] = 1.0
    return (jnp.asarray(np.concatenate([cos, cos], 1)), jnp.asarray(np.concatenate([sin, sin], 1)),
            jnp.asarray(rot), jnp.asarray(rot.T.copy()))


def _rope(v, cos, sin, rot):
    return v * cos + jnp.dot(v, rot, precision=HI, preferred_element_type=F32) * sin


def _unrope(dv, cos, sin, rot_t):
    return dv * cos + jnp.dot(dv * sin, rot_t, precision=HI, preferred_element_type=F32)


def _rms(v, g):
    return v * lax.rsqrt(jnp.mean(v * v, axis=-1, keepdims=True) + EPS) * g


def _mla_prep(lat, gq, gkv, cos, sin, rot, tm=256):
    s, w = lat.shape
    tm = _tile(s, tm)

    def body(l_ref, gq_ref, gkv_ref, c_ref, s_ref, r_ref, cq_ref, ckv_ref, kr_ref):
        cq_ref[...] = _rms(l_ref[:, :ML_RANK], gq_ref[...]).astype(BF)
        ckv_ref[...] = _rms(l_ref[:, ML_RANK:2 * ML_RANK], gkv_ref[...]).astype(BF)
        kr_ref[...] = _rope(l_ref[:, 2 * ML_RANK:], c_ref[...], s_ref[...], r_ref[...]).astype(BF)

    row = lambda c: pl.BlockSpec((tm, c), lambda i: (i, 0))
    full = lambda a: pl.BlockSpec(a.shape, lambda i: (0, 0))
    return pl.pallas_call(
        body, name="mla_prep", grid=(s // tm,),
        in_specs=[row(w), full(gq), full(gkv), row(ML_ROPE), row(ML_ROPE), full(rot)],
        out_specs=[row(ML_RANK), row(ML_RANK), row(ML_ROPE)],
        out_shape=[jax.ShapeDtypeStruct((s, ML_RANK), BF), jax.ShapeDtypeStruct((s, ML_RANK), BF),
                   jax.ShapeDtypeStruct((s, ML_ROPE), BF)],
        compiler_params=_params(1))(lat, gq, gkv, cos, sin, rot)


def _mla_q_proj(cq, wuq, cos, sin, rot, tm=512):
    s, k = cq.shape
    tm = _tile(s, tm)

    def epi(accs, ex, out):
        acc = accs[0]
        out[0][:, :ML_NOPE] = acc[:, :ML_NOPE].astype(BF)
        out[0][:, ML_NOPE:] = _rope(acc[:, ML_NOPE:], ex[0][...], ex[1][...], ex[2][...]).astype(BF)

    rmap = lambda j, i: (i, 0)
    return _mm("mla_q_proj", (ML_HEADS, s // tm),
               [(cq, (tm, k), rmap, wuq, (None, k, ML_QK), lambda j, i: (j, 0, 0), "nn", 0, 0)],
               [(cos, (tm, ML_ROPE), rmap), (sin, (tm, ML_ROPE), rmap), (rot, rot.shape, lambda j, i: (0, 0))],
               [((ML_HEADS, s, ML_QK), BF, (None, tm, ML_QK), lambda j, i: (j, i, 0))], epi)[0]


def _mla_kv_proj(ckv, wukv, kr, tm=512):
    s, k = ckv.shape
    tm = _tile(s, tm)

    def epi(accs, ex, out):
        acc = accs[0]
        out[0][:, :ML_NOPE] = acc[:, :ML_NOPE].astype(BF)
        out[0][:, ML_NOPE:] = ex[0][...]
        out[1][...] = acc[:, ML_NOPE:].astype(BF)

    rmap = lambda j, i: (i, 0)
    gmap = lambda j, i: (j, i, 0)
    return _mm("mla_kv_proj", (ML_HEADS, s // tm),
               [(ckv, (tm, k), rmap, wukv, (None, k, ML_NOPE + ML_V), lambda j, i: (j, 0, 0), "nn", 0, 0)],
               [(kr, (tm, ML_ROPE), rmap)],
               [((ML_HEADS, s, ML_QK), BF, (None, tm, ML_QK), gmap), ((ML_HEADS, s, ML_V), BF, (None, tm, ML_V), gmap)],
               epi)


def _mla_probs(q, k):
    sc = lax.dot_general(q, k, _DN["nt"], preferred_element_type=F32) * (ML_QK ** -0.5)
    e = jnp.exp(sc - jnp.max(sc, axis=-1, keepdims=True))
    return e / jnp.sum(e, axis=-1, keepdims=True)


def _mla_fwd(q, k, v, tq=512, comm=()):
    _, s, _ = q.shape
    tq = _tile(s, tq)

    def body(q_ref, k_ref, v_ref, o_ref):
        p = _mla_probs(q_ref[...], k_ref[...])
        o_ref[...] = jnp.dot(p.astype(BF), v_ref[...], preferred_element_type=F32).astype(BF)

    return _call("mla_fwd", body, (ML_HEADS, s // tq),
                 [pl.BlockSpec((None, tq, ML_QK), lambda h, i: (h, i, 0)),
                  pl.BlockSpec((None, s, ML_QK), lambda h, i: (h, 0, 0)),
                  pl.BlockSpec((None, s, ML_V), lambda h, i: (h, 0, 0))],
                 pl.BlockSpec((tq, ML_V), lambda h, i: (i, h)),
                 jax.ShapeDtypeStruct((s, ML_HEADS * ML_V), BF), [q, k, v], comm)


def _mla_bwd(q, k, v, do, tq=256, comm=()):
    _, s, _ = q.shape
    tq = _tile(s, tq)

    def body(q_ref, k_ref, v_ref, do_ref, dq_ref, dk_ref, dv_ref):
        i = pl.program_id(1)
        qv, kv, vv, dov = q_ref[...], k_ref[...], v_ref[...], do_ref[...]
        p = _mla_probs(qv, kv)
        dp = lax.dot_general(dov, vv, _DN["nt"], preferred_element_type=F32)
        ds = (p * (dp - jnp.sum(p * dp, axis=-1, keepdims=True)) * (ML_QK ** -0.5)).astype(BF)
        dq_ref[...] = jnp.dot(ds, kv, preferred_element_type=F32)
        _acc_rows(dk_ref, lax.dot_general(ds, qv, _DN["tn"], preferred_element_type=F32), i)
        _acc_rows(dv_ref, lax.dot_general(p.astype(BF), dov, _DN["tn"], preferred_element_type=F32), i)

    return _call(
        "mla_bwd", body, (ML_HEADS, s // tq),
        [pl.BlockSpec((None, tq, ML_QK), lambda h, i: (h, i, 0)),
         pl.BlockSpec((None, s, ML_QK), lambda h, i: (h, 0, 0)),
         pl.BlockSpec((None, s, ML_V), lambda h, i: (h, 0, 0)),
         pl.BlockSpec((tq, ML_V), lambda h, i: (i, h))],
        [pl.BlockSpec((None, tq, ML_QK), lambda h, i: (h, i, 0)),
         pl.BlockSpec((None, s, ML_QK), lambda h, i: (h, 0, 0)),
         pl.BlockSpec((None, s, ML_V), lambda h, i: (h, 0, 0))],
        [jax.ShapeDtypeStruct((ML_HEADS, s, ML_QK), F32), jax.ShapeDtypeStruct((ML_HEADS, s, ML_QK), F32),
         jax.ShapeDtypeStruct((ML_HEADS, s, ML_V), F32)],
        [q, k, v, do], comm)


def _mla_post(dq, dk, dv, cos, sin, rot_t, tm=256):
    _, s, _ = dq.shape
    tm = _tile(s, tm)

    def body(dq_ref, dk_ref, dv_ref, c_ref, s_ref, r_ref, dqp_ref, dkv_ref, dkr_ref):
        h = pl.program_id(1)
        dqv, dkk = dq_ref[...], dk_ref[...]
        dqp_ref[:, :ML_NOPE] = dqv[:, :ML_NOPE].astype(BF)
        dqp_ref[:, ML_NOPE:] = _unrope(dqv[:, ML_NOPE:], c_ref[...], s_ref[...], r_ref[...]).astype(BF)
        dkv_ref[:, :ML_NOPE] = dkk[:, :ML_NOPE].astype(BF)
        dkv_ref[:, ML_NOPE:] = dv_ref[...].astype(BF)
        _acc_rows(dkr_ref, dkk[:, ML_NOPE:], h)

    gspec = lambda c: pl.BlockSpec((None, tm, c), lambda i, h: (h, i, 0))
    rspec = pl.BlockSpec((tm, ML_ROPE), lambda i, h: (i, 0))
    return pl.pallas_call(
        body, name="mla_post", grid=(s // tm, ML_HEADS),
        in_specs=[gspec(ML_QK), gspec(ML_QK), gspec(ML_V), rspec, rspec,
                  pl.BlockSpec(rot_t.shape, lambda i, h: (0, 0))],
        out_specs=[gspec(ML_QK), gspec(ML_NOPE + ML_V), rspec],
        out_shape=[jax.ShapeDtypeStruct((ML_HEADS, s, ML_QK), BF),
                   jax.ShapeDtypeStruct((ML_HEADS, s, ML_NOPE + ML_V), BF),
                   jax.ShapeDtypeStruct((s, ML_ROPE), F32)],
        compiler_params=_params(2))(dq, dk, dv, cos, sin, rot_t)


def _mla_lat_bwd(dcq, dckv, dkr, lat, gq, gkv, cos, sin, rot_t, tm=256):
    s, w = lat.shape
    tm = _tile(s, tm)

    def body(dcq_ref, dckv_ref, dkr_ref, l_ref, gq_ref, gkv_ref, c_ref, s_ref, r_ref, dl_ref, dgq_ref, dgkv_ref):
        i = pl.program_id(0)
        dql, pq = _rms_bwd_math(dcq_ref[...], l_ref[:, :ML_RANK], gq_ref[...])
        dkl, pkv = _rms_bwd_math(dckv_ref[...], l_ref[:, ML_RANK:2 * ML_RANK], gkv_ref[...])
        dl_ref[:, :ML_RANK] = dql.astype(BF)
        dl_ref[:, ML_RANK:2 * ML_RANK] = dkl.astype(BF)
        dl_ref[:, 2 * ML_RANK:] = _unrope(dkr_ref[...], c_ref[...], s_ref[...], r_ref[...]).astype(BF)
        _acc_rows(dgq_ref, pq, i)
        _acc_rows(dgkv_ref, pkv, i)

    row = lambda c: pl.BlockSpec((tm, c), lambda i: (i, 0))
    full = lambda a: pl.BlockSpec(a.shape, lambda i: (0, 0))
    return pl.pallas_call(
        body, name="mla_lat_bwd", grid=(s // tm,),
        in_specs=[row(ML_RANK), row(ML_RANK), row(ML_ROPE), row(w), full(gq), full(gkv), row(ML_ROPE), row(ML_ROPE),
                  full(rot_t)],
        out_specs=[row(w), full(gq), full(gkv)],
        out_shape=[jax.ShapeDtypeStruct((s, w), BF), jax.ShapeDtypeStruct(gq.shape, F32),
                   jax.ShapeDtypeStruct(gkv.shape, F32)],
        compiler_params=_params(1))(dcq, dckv, dkr, lat, gq, gkv, cos, sin, rot_t)


def _grp_dw(name, a, dout, ta=512):
    s, k = a.shape
    ta = _tile(k, ta)
    if dout.ndim == 3:
        g, _, nb = dout.shape
        b_blk, b_map = (None, s, nb), lambda j, i: (j, 0, 0)
    else:
        g, nb = NDEV, dout.shape[1] // NDEV
        b_blk, b_map = (s, nb), lambda j, i: (0, j)
    return _mm(name, (g, k // ta),
               [(a, (s, ta), lambda j, i: (0, i), dout, b_blk, b_map, "tn", 0, 0)], [],
               [((g, k, nb), BF, (None, ta, nb), lambda j, i: (j, i, 0))], _store)[0]


def _grp_dx(name, dout, w, tm=512, tn=512, comm=()):
    g, s, nb = dout.shape
    k = w.shape[1]
    tm, tn = _tile(s, tm), _tile(k, tn)
    return _mm(name, (k // tn, s // tm),
               [(dout, (g, tm, nb), lambda j, i: (0, i, 0), w, (g, tn, nb), lambda j, i: (0, j, 0), "nt", 0, g)], [],
               [((s, k), F32, (tm, tn), lambda j, i: (i, j))], _store, comm=comm)[0]


def _row_dw(name, a, dout, tn=512):
    s, n = dout.shape
    tn = _tile(n, tn)
    if a.ndim == 3:
        kb = a.shape[2]
        a_blk, a_map = (None, s, kb), lambda j, i: (j, 0, 0)
    else:
        kb = a.shape[1] // NDEV
        a_blk, a_map = (s, kb), lambda j, i: (0, j)
    return _mm(name, (NDEV, n // tn),
               [(a, a_blk, a_map, dout, (s, tn), lambda j, i: (0, i), "tn", 0, 0)], [],
               [((NDEV, kb, n), BF, (None, kb, tn), lambda j, i: (j, 0, i))], _store)[0]


def _mix_merge(oa, ob, wa, wb, ga, gb, tm=512, comm=()):
    s, k = oa.shape
    g, _, nb = wa.shape
    tm = _tile(s, tm)

    def epi(accs, ex, out):
        ya, yb = accs
        out[0][...] = ya.astype(BF)
        out[1][...] = yb.astype(BF)
        out[2][...] = (_sig(ex[0][...]) * ya + _sig(ex[1][...]) * yb).astype(BF)

    rmap = lambda j, i: (i, 0)
    wmap = lambda j, i: (j, 0, 0)
    o = ((g, s, nb), BF, (None, tm, nb), lambda j, i: (j, i, 0))
    cmap = lambda j, i: (i, j)
    return _mm("mix_merge", (g, s // tm),
               [(oa, (tm, k), rmap, wa, (None, k, nb), wmap, "nn", 0, 0),
                (ob, (tm, k), rmap, wb, (None, k, nb), wmap, "nn", 1, 0)],
               [(ga, (tm, nb), cmap), (gb, (tm, nb), cmap)], [o, o, o], epi, nacc=2, comm=comm)


def _mix_out(merged, wout, resid, tm=512, tn=512):
    g, s, kb = merged.shape
    d = wout.shape[2]
    tm, tn = _tile(s, tm), _tile(d, tn)

    def epi(accs, ex, out):
        out[0][...] = ex[0][...] + accs[0]

    return _mm("mix_out", (d // tn, s // tm),
               [(merged, (g, tm, kb), lambda j, i: (0, i, 0), wout, (g, kb, tn), lambda j, i: (0, 0, j), "nn", 0, g)],
               [(resid, (tm, tn), lambda j, i: (i, j))],
               [((s, d), F32, (tm, tn), lambda j, i: (i, j))], epi)[0]


def _mix_out_bwd(dh, wout, ga, gb, ya, yb, tm=512, comm=()):
    s, d = dh.shape
    g, kb, _ = wout.shape
    tm = _tile(s, tm)

    def epi(accs, ex, out):
        dm = accs[0]
        sa, sb = _sig(ex[0][...]), _sig(ex[1][...])
        out[0][...] = (dm * sa).astype(BF)
        out[1][...] = (dm * sb).astype(BF)
        out[2][...] = (dm * ex[2][...].astype(F32) * sa * (1.0 - sa)).astype(BF)
        out[3][...] = (dm * ex[3][...].astype(F32) * sb * (1.0 - sb)).astype(BF)

    cmap = lambda j, i: (i, j)
    gmap = lambda j, i: (j, i, 0)
    og = ((g, s, kb), BF, (None, tm, kb), gmap)
    oc = ((s, g * kb), BF, (tm, kb), cmap)
    return _mm("mix_out_bwd", (g, s // tm),
               [(dh, (tm, d), lambda j, i: (i, 0), wout, (None, kb, d), lambda j, i: (j, 0, 0), "nt", 0, 0)],
               [(ga, (tm, kb), cmap), (gb, (tm, kb), cmap), (ya, (None, tm, kb), gmap), (yb, (None, tm, kb), gmap)],
               [og, og, oc, oc], epi, comm=comm)


def _pl_forward(n4, wplg, p, wpl, h3, tm=512):
    s, d = n4.shape
    g, kb, _ = wplg.shape
    kp, nb = wpl.shape[1], wpl.shape[2]
    tm = _tile(s, tm)
    wplg_nat = wplg.reshape(g * kb, d)

    def epi(accs, ex, out):
        t, pe = accs
        out[0][...] = ex[0][...] + _sig(t) * pe
        out[1][...] = t
        out[2][...] = pe.astype(BF)

    rmap = lambda j, i: (i, 0)
    cmap = lambda j, i: (i, j)
    return _mm("pl_forward", (g, s // tm),
               [(n4, (tm, d), rmap, wplg_nat, (g * kb, nb), lambda j, i: (0, j), "nn", 0, 0),
                (p, (tm, kp), rmap, wpl, (None, kp, nb), lambda j, i: (j, 0, 0), "nn", 1, 0)],
               [(h3, (tm, nb), cmap)],
               [((s, d), F32, (tm, nb), cmap), ((s, d), F32, (tm, nb), cmap), ((s, d), BF, (tm, nb), cmap)],
               epi, nacc=2)


def _row_dx(name, dout, w, tm=512, comm=()):
    s, n = dout.shape
    g, kb, _ = w.shape
    tm = _tile(s, tm)
    return _mm(name, (g, s // tm),
               [(dout, (tm, n), lambda j, i: (i, 0), w, (None, kb, n), lambda j, i: (j, 0, 0), "nt", 0, 0)], [],
               [((s, g * kb), F32, (tm, kb), lambda j, i: (i, j))], _store, comm=comm)[0]


def _in_proj_bwd_x(pieces, weights, tm=256, tn=256, comm=()):
    s = pieces[0].shape[0]
    d = weights[0].shape[0]
    tm, tn = _tile(s, tm), _tile(d, tn)
    prods = [(pc, (tm, pc.shape[1]), lambda j, i: (i, 0), w, (tn, w.shape[1]), lambda j, i: (j, 0), "nt", 0, 0)
             for pc, w in zip(pieces, weights)]
    return _mm("in_proj_dx", (d // tn, s // tm), prods, [],
               [((s, d), F32, (tm, tn), lambda j, i: (i, j))], _store, comm=comm)[0]


def _split_w_in(w_in_g):
    g, d, nb = w_in_g.shape
    nat = jnp.transpose(w_in_g, (1, 0, 2)).reshape(d, g * nb)
    na, lat = 3 * NA_HEADS * NA_DIM, 2 * ML_RANK + ML_ROPE
    return nat[:, :na], nat[:, na:na + lat], nat[:, na + lat:na + lat + d], nat[:, na + lat + d:]


def _pair_sum(name, part, landed, core):
    _, _, r, c = part.shape
    tr = _row_tile(r, c)

    def body(core_ref, a_ref, b_ref, o_ref):
        o_ref[...] = (a_ref[...].astype(F32) + b_ref[...].astype(F32)).astype(o_ref.dtype)

    return pl.pallas_call(
        body, name=name,
        grid_spec=pltpu.PrefetchScalarGridSpec(
            num_scalar_prefetch=1, grid=(NCHIP, r // tr),
            in_specs=[pl.BlockSpec((None, None, tr, c), lambda j, i, core_ref: (j, core_ref[0], i, 0)),
                      pl.BlockSpec((None, tr, c), lambda j, i, core_ref: (j, i, 0))],
            out_specs=pl.BlockSpec((None, tr, c), lambda j, i, core_ref: (j, i, 0))),
        out_shape=jax.ShapeDtypeStruct(landed.shape, landed.dtype), compiler_params=_params(2),
    )(core, part, landed)


def _device_step(x, p, target, sp, own, core):
    s, d = x.shape
    rows = s // GRID_W
    cos, sin, rot, rot_t = _rope_consts(s)
    w, dw4, sums, chip_parts, dsp = {}, {}, {}, {}, {}

    def gather(*names):
        return _GatherPart(names, [own[n] for n in names])

    def got(part):
        w.update(zip(part.names, part.results))

    def grad(name, g):
        dw4[name] = g.reshape((NCHIP, 2) + g.shape[1:])

    def to_sibling(*names):
        return _SiblingPart(names, [dw4[n] for n in names])

    def add_pairs(part):
        for n, landed in zip(part.names, part.results):
            sums[n] = _pair_sum("pair_sum_" + n, dw4[n], landed, core)

    def to_chips(*names):
        return _ChipsPart(names, [sums[n] for n in names])

    def done(part):
        chip_parts.update(zip(part.names, part.results))

    c0 = gather("ffn1_w_gate", "ffn1_w_up")
    _comm_only("gather_ffn1", [c0])
    got(c0)
    c1 = gather("ffn1_w_down", "w_in")
    c2 = gather("w_uq", "w_ukv", "w_branch_a", "w_branch_b", "w_out")

    def ffn1_wd():
        got(c1)
        return w["ffn1_w_down"]

    h1, ffn1_saved = _ffn_forward("ffn1", x, sp["ffn1_norm"], w["ffn1_w_gate"], w["ffn1_w_up"], ffn1_wd,
                                  up_comm=[c1], down_comm=[c2])
    got(c2)
    wqkv, wlat, wga, wgb = _split_w_in(w["w_in"])
    u = _rms_fwd("mix_norm", h1, sp["mix_norm"])
    c3 = gather("ffn2_w_gate")
    qkv = _mm_nn("in_qkv", u, wqkv, BF, tn=1024, comm=[c3])
    got(c3)
    lat = _mm_nn("in_lat", u, wlat, F32)
    ga = _mm_nn("in_ga", u, wga, F32, tn=1024)
    gb = _mm_nn("in_gb", u, wgb, F32, tn=1024)
    tb = _na_bias_tables(sp["na_rpb"], rows)
    c4 = gather("ffn2_w_up")
    oa = _na_fwd(qkv, tb, comm=[c4])
    got(c4)
    cq, ckv, kr = _mla_prep(lat, sp["q_a_norm"], sp["kv_a_norm"], cos, sin, rot)
    qf = _mla_q_proj(cq, w["w_uq"], cos, sin, rot)
    kf, vf = _mla_kv_proj(ckv, w["w_ukv"], kr)
    c5 = gather("ffn2_w_down")
    ob = _mla_fwd(qf, kf, vf, comm=[c5])
    got(c5)
    c6 = gather("w_pl", "w_pl_gate")
    ya, yb, merged = _mix_merge(oa, ob, w["w_branch_a"], w["w_branch_b"], ga, gb, comm=[c6])
    got(c6)
    h2 = _mix_out(merged, w["w_out"], h1)
    h3, ffn2_saved = _ffn_forward("ffn2", h2, sp["ffn2_norm"], w["ffn2_w_gate"], w["ffn2_w_up"],
                                  lambda: w["ffn2_w_down"])
    n4 = _rms_fwd("pl_norm", h3, sp["pl_norm"])
    pb = p.astype(BF)
    h4, t, pe = _pl_forward(n4, w["w_pl_gate"], pb, w["w_pl"], h3)

    dh4, dsp["final_norm"], loss = _loss_head(h4, target, sp["final_norm"])
    dt, dpe = _pl_bwd_elem(dh4, pe, t)
    grad("w_pl", _grp_dw("pl_dw", pb, dpe))
    grad("w_pl_gate", _row_dw("plg_dw", n4, dt))
    s1 = to_sibling("w_pl", "w_pl_gate")
    dn4 = _row_dx("plg_dx", dt, w["w_pl_gate"], comm=[s1])
    add_pairs(s1)
    dh3, dsp["pl_norm"] = _rms_bwd("pl_dnorm", dn4, h3, sp["pl_norm"], dh4)

    xn, hg, hu, a = ffn2_saved
    dhb = dh3.astype(BF)
    k1 = to_chips("w_pl", "w_pl_gate")
    grad("ffn2_w_down", _ffn_bwd_wd("ffn2_dwd", a, dhb, comm=[k1]))
    done(k1)
    s2 = to_sibling("ffn2_w_down")
    dhg, dhu = _ffn_bwd_act("ffn2_dact", dhb, w["ffn2_w_down"], hg, hu, comm=[s2])
    add_pairs(s2)
    k2 = to_chips("ffn2_w_down")
    dwg, dwu = _ffn_bwd_wup("ffn2_dwup", xn, dhg, dhu, comm=[k2])
    done(k2)
    grad("ffn2_w_gate", dwg)
    grad("ffn2_w_up", dwu)
    s3 = to_sibling("ffn2_w_gate", "ffn2_w_up")
    dxn = _ffn_bwd_x("ffn2_dx", dhg, dhu, w["ffn2_w_gate"], w["ffn2_w_up"], comm=[s3])
    add_pairs(s3)
    dh2, dsp["ffn2_norm"] = _rms_bwd("ffn2_dnorm", dxn, h2, sp["ffn2_norm"], dh3)

    dh2b = dh2.astype(BF)
    grad("w_out", _row_dw("out_dw", merged, dh2b))
    s4 = to_sibling("w_out")
    dya, dyb, dga, dgb = _mix_out_bwd(dh2b, w["w_out"], ga, gb, ya, yb, comm=[s4])
    add_pairs(s4)
    grad("w_branch_a", _grp_dw("bra_dw", oa, dya))
    grad("w_branch_b", _grp_dw("brb_dw", ob, dyb))
    doa = _grp_dx("bra_dx", dya, w["w_branch_a"]).astype(BF)
    s5 = to_sibling("w_branch_a", "w_branch_b")
    dob = _grp_dx("brb_dx", dyb, w["w_branch_b"], comm=[s5]).astype(BF)
    add_pairs(s5)

    k3 = to_chips("ffn2_w_gate", "w_out")
    dqf, dkf, dvf = _mla_bwd(qf, kf, vf, dob, comm=[k3])
    done(k3)
    dqp, dkv, dkr = _mla_post(dqf, dkf, dvf, cos, sin, rot_t)
    grad("w_uq", _grp_dw("uq_dw", cq, dqp))
    grad("w_ukv", _grp_dw("ukv_dw", ckv, dkv))
    dcq = _grp_dx("uq_dx", dqp, w["w_uq"])
    s6 = to_sibling("w_uq", "w_ukv")
    dckv = _grp_dx("ukv_dx", dkv, w["w_ukv"], comm=[s6])
    add_pairs(s6)
    dlat, dsp["q_a_norm"], dsp["kv_a_norm"] = _mla_lat_bwd(dcq, dckv, dkr, lat, sp["q_a_norm"], sp["kv_a_norm"],
                                                         cos, sin, rot_t)
    k4 = to_chips("ffn2_w_up", "w_branch_a", "w_branch_b")
    dq_na, dk_na, dv_na, dtab = _na_bwd(qkv, tb, doa, comm=[k4])
    done(k4)
    dsp["na_rpb"] = _na_rpb_grad(dtab, rows)
    dqkv = jnp.concatenate([dq_na, dk_na.astype(BF), dv_na.astype(BF)], axis=1)

    pieces = [dqkv, dlat, dga, dgb]
    dwin = jnp.concatenate([_mm_tn("in_dw%d" % i, u, pc, BF) for i, pc in enumerate(pieces)], axis=1)
    grad("w_in", dwin.reshape(d, NDEV, -1).transpose(1, 0, 2))
    s7 = to_sibling("w_in")
    k5 = to_chips("w_uq", "w_ukv")
    du = _in_proj_bwd_x(pieces, [wqkv, wlat, wga, wgb], comm=[s7, k5])
    add_pairs(s7)
    done(k5)
    dh1, dsp["mix_norm"] = _rms_bwd("mix_dnorm", du, h1, sp["mix_norm"], dh2)

    xn, hg, hu, a = ffn1_saved
    dhb = dh1.astype(BF)
    k6 = to_chips("w_in")
    grad("ffn1_w_down", _ffn_bwd_wd("ffn1_dwd", a, dhb, comm=[k6]))
    done(k6)
    s8 = to_sibling("ffn1_w_down")
    dhg, dhu = _ffn_bwd_act("ffn1_dact", dhb, w["ffn1_w_down"], hg, hu, comm=[s8])
    add_pairs(s8)
    k7 = to_chips("ffn1_w_down")
    dwg, dwu = _ffn_bwd_wup("ffn1_dwup", xn, dhg, dhu, comm=[k7])
    done(k7)
    grad("ffn1_w_gate", dwg)
    grad("ffn1_w_up", dwu)
    s9 = to_sibling("ffn1_w_gate", "ffn1_w_up")
    _comm_only("rs_sibling_ffn1", [s9])
    add_pairs(s9)
    k8 = to_chips("ffn1_w_gate", "ffn1_w_up")
    dxn = _ffn_bwd_x("ffn1_dx", dhg, dhu, w["ffn1_w_gate"], w["ffn1_w_up"], comm=[k8])
    done(k8)
    dx, dsp["ffn1_norm"] = _rms_bwd("ffn1_dnorm", dxn, x, sp["ffn1_norm"], dh1)
    return loss, dx, chip_parts, dsp


def _gather_small(buf):
    def body(in_ref, out_ref, send_sems, recv_sems, local_sem):
        x, y, c = _coords()
        mine = pltpu.make_async_copy(in_ref, out_ref.at[4 * x + 2 * y + c], local_sem)
        mine.start()
        cps = []
        for k in range(1, NDEV):
            fx, fy, fc = (k >> 2) & 1, (k >> 1) & 1, k & 1
            peer = (x ^ fx, y ^ fy, c ^ fc)
            cps.append(pltpu.make_async_remote_copy(
                src_ref=in_ref, dst_ref=out_ref.at[4 * x + 2 * y + c], send_sem=send_sems.at[k - 1],
                recv_sem=recv_sems.at[k - 1], device_id=peer, device_id_type=MESH))
        for cp in cps:
            cp.start()
        for k in range(1, NDEV):
            fx, fy, fc = (k >> 2) & 1, (k >> 1) & 1, k & 1
            px, py, pc = x ^ fx, y ^ fy, c ^ fc
            pltpu.make_async_remote_copy(
                src_ref=in_ref, dst_ref=out_ref.at[4 * px + 2 * py + pc], send_sem=send_sems.at[k - 1],
                recv_sem=recv_sems.at[k - 1], device_id=(px, py, pc), device_id_type=MESH).wait_recv()
        for cp in cps:
            cp.wait_send()
        mine.wait()

    return pl.pallas_call(
        body, name="gather_small", in_specs=[ANY], out_specs=ANY,
        out_shape=jax.ShapeDtypeStruct((NDEV,) + buf.shape, buf.dtype),
        scratch_shapes=[pltpu.SemaphoreType.DMA((NDEV - 1,)), pltpu.SemaphoreType.DMA((NDEV - 1,)),
                        pltpu.SemaphoreType.DMA],
    )(buf)


def _adam_math(wv, g, m, v):
    m_new = B1 * m + (1.0 - B1) * g
    v_new = B2 * v + (1.0 - B2) * (g * g)
    m_hat = m_new / (1.0 - B1 ** STEP)
    v_hat = v_new / (1.0 - B2 ** STEP)
    return -LR * (m_hat / (jnp.sqrt(v_hat) + ADAM_EPS) + WD * wv), m_new, v_new


def _adam(name, parts, wv, m, v):
    npart, r, c = parts.shape
    tr = _row_tile(r, c)

    def body(p_ref, w_ref, m_ref, v_ref, g_ref, d_ref, mo_ref, vo_ref):
        g = p_ref[0].astype(F32)
        for j in range(1, npart):
            g = g + p_ref[j].astype(F32)
        g_ref[...] = g
        d_ref[...], mo_ref[...], vo_ref[...] = _adam_math(w_ref[...], g, m_ref[...], v_ref[...])

    row = pl.BlockSpec((tr, c), lambda i: (i, 0))
    return pl.pallas_call(
        body, name=name, grid=(r // tr,), in_specs=[pl.BlockSpec((npart, tr, c), lambda i: (0, i, 0)), row, row, row],
        out_specs=[row] * 4, out_shape=[jax.ShapeDtypeStruct((r, c), F32)] * 4, compiler_params=_params(1),
    )(parts, wv, m, v)


SHARDED = ("ffn1_w_gate", "ffn1_w_up", "ffn1_w_down", "w_in", "w_uq", "w_ukv", "w_branch_a", "w_branch_b", "w_out",
           "ffn2_w_gate", "ffn2_w_up", "ffn2_w_down", "w_pl", "w_pl_gate")
REPLICATED = ("ffn1_norm", "mix_norm", "q_a_norm", "kv_a_norm", "na_rpb", "ffn2_norm", "pl_norm", "final_norm")
WEIGHTS = ("ffn1_norm", "ffn1_w_gate", "ffn1_w_up", "ffn1_w_down", "mix_norm", "w_in", "q_a_norm", "w_uq",
           "kv_a_norm", "w_ukv", "na_rpb", "w_branch_a", "w_branch_b", "w_out", "ffn2_norm", "ffn2_w_gate",
           "ffn2_w_up", "ffn2_w_down", "pl_norm", "w_pl", "w_pl_gate", "final_norm")
SMALL_W = 2048


def _pack_small(vals):
    rows = []
    for name in REPLICATED:
        flat = vals[name].reshape(-1).astype(F32)
        n = -(-flat.shape[0] // SMALL_W) * SMALL_W
        rows.append(jnp.pad(flat, (0, n - flat.shape[0])).reshape(-1, SMALL_W))
    return jnp.concatenate(rows, axis=0)


def _unpack_small(buf, shapes):
    out, r = {}, 0
    for name in REPLICATED:
        size = int(np.prod(shapes[name]))
        nrow = -(-size // SMALL_W)
        out[name] = buf[r:r + nrow].reshape(-1)[:size].reshape(shapes[name])
        r += nrow
    return out


def kernel(x, p, ffn1_norm, ffn1_w_gate, ffn1_w_up, ffn1_w_down, mix_norm, w_in, q_a_norm, w_uq, kv_a_norm, w_ukv, na_rpb, w_branch_a, w_branch_b, w_out, ffn2_norm, ffn2_w_gate, ffn2_w_up, ffn2_w_down, pl_norm, w_pl, w_pl_gate, final_norm, loss_target, m_ffn1_norm, m_ffn1_w_gate, m_ffn1_w_up, m_ffn1_w_down, m_mix_norm, m_w_in, m_q_a_norm, m_w_uq, m_kv_a_norm, m_w_ukv, m_na_rpb, m_w_branch_a, m_w_branch_b, m_w_out, m_ffn2_norm, m_ffn2_w_gate, m_ffn2_w_up, m_ffn2_w_down, m_pl_norm, m_w_pl, m_w_pl_gate, m_final_norm, v_ffn1_norm, v_ffn1_w_gate, v_ffn1_w_up, v_ffn1_w_down, v_mix_norm, v_w_in, v_q_a_norm, v_w_uq, v_kv_a_norm, v_w_ukv, v_na_rpb, v_w_branch_a, v_w_branch_b, v_w_out, v_ffn2_norm, v_ffn2_w_gate, v_ffn2_w_up, v_ffn2_w_down, v_pl_norm, v_w_pl, v_w_pl_gate, v_final_norm):
    args = dict(locals())
    wts = {n: args[n] for n in WEIGHTS}
    mom = {n: args["m_" + n] for n in WEIGHTS}
    var = {n: args["v_" + n] for n in WEIGHTS}
    shapes = {n: wts[n].shape for n in WEIGHTS}
    core = lax.axis_index("c").astype(jnp.int32).reshape(1)

    own = {n: wts[n][0].astype(BF) for n in SHARDED}
    sp = {n: wts[n].reshape(1, -1) for n in REPLICATED if n != "na_rpb"}
    sp["na_rpb"] = wts["na_rpb"][0]
    loss_part, grad_x, chip_parts, dsp = _device_step(x[0], p[0, 0], loss_target[0], sp, own, core)

    out = {}
    for n in SHARDED:
        g, dlt, m_new, v_new = _adam("adam_" + n, chip_parts[n], wts[n][0], mom[n][0], var[n][0])
        out[n] = tuple(a[None] for a in (g, dlt, m_new, v_new))

    small = jnp.concatenate([_pack_small(dsp), jnp.pad(loss_part, ((0, 0), (0, SMALL_W - loss_part.shape[1])))], 0)
    pad_rows = -small.shape[0] % 8
    small = jnp.pad(small, ((0, pad_rows), (0, 0)))
    every = _gather_small(small)
    zeros = jnp.zeros((1 + pad_rows, SMALL_W), F32)
    pack = lambda d: jnp.concatenate([_pack_small(d), zeros], 0)
    g_s, d_s, m_s, v_s = _adam("adam_small", every, pack(wts), pack(mom), pack(var))
    n_rows = small.shape[0] - 1 - pad_rows
    loss = g_s[n_rows, 0]
    small_out = [_unpack_small(b, shapes) for b in (g_s, d_s, m_s, v_s)]
    for n in REPLICATED:
        out[n] = tuple(b[n] for b in small_out)

    res = [loss, grad_x[None]]
    for k in range(4):
        res += [out[n][k] for n in WEIGHTS]
    return tuple(res)
```

```python
import functools

import numpy as np
import jax
import jax.numpy as jnp
from jax import lax
from jax.experimental import pallas as pl
from jax.experimental.pallas import tpu as pltpu

F32 = jnp.float32
BF = jnp.bfloat16
MESH = pl.DeviceIdType.MESH

NDEV = 8
NCHIP = 4
VMEM_LIMIT = 56 * 1024 * 1024
EPS = 1e-6
NEG = -1e30
GRID_W = 64
NA_HEADS, NA_DIM = 8, 128
NA_ROWS_WIN, NA_COLS_WIN = 8, 16
NA_HG = 4
ML_HEADS, ML_NOPE, ML_ROPE, ML_V = 8, 128, 64, 128
ML_QK = ML_NOPE + ML_ROPE
ML_RANK = 512
ROPE_THETA = 10000.0
LR, B1, B2, ADAM_EPS, WD, STEP = 0.001, 0.9, 0.999, 1e-08, 0.01, 10
HI = lax.Precision.HIGHEST

_DN = {"nn": (((1,), (0,)), ((), ())), "nt": (((1,), (1,)), ((), ())), "tn": (((0,), (0,)), ((), ()))}


def _params(n):
    return pltpu.CompilerParams(dimension_semantics=("arbitrary",) * n, vmem_limit_bytes=VMEM_LIMIT)


def _sig(v):
    return jax.nn.sigmoid(v)


ANY = pl.BlockSpec(memory_space=pl.ANY)


def _coords():
    return lax.axis_index("x"), lax.axis_index("y"), lax.axis_index("c")


class _Part:
    inputs, out_shapes, sem_shapes, results = (), (), (), None


class _GatherPart(_Part):
    def __init__(self, names, shards):
        n = len(shards)
        self.names, self.inputs = list(names), list(shards)
        self.out_shapes = [jax.ShapeDtypeStruct((NDEV,) + a.shape, a.dtype) for a in shards]
        self.sem_shapes = [pltpu.SemaphoreType.DMA((n, 7)), pltpu.SemaphoreType.DMA((n, 7)),
                           pltpu.SemaphoreType.DMA((n,))]

    def _plan(self, ins, outs, sems):
        send_sems, recv_sems, local_sems = sems
        x, y, c = _coords()
        chips = [(1 - x, y), (x, 1 - y), (1 - x, 1 - y)]

        def copy(i, k, block, to, src=None):
            px, py, pc = block
            dst = outs[i].at[4 * px + 2 * py + pc]
            return pltpu.make_async_remote_copy(
                src_ref=dst if src is None else src, dst_ref=dst, send_sem=send_sems.at[i, k],
                recv_sem=recv_sems.at[i, k], device_id=to, device_id_type=MESH)

        n = len(ins)
        mine = [pltpu.make_async_copy(ins[i], outs[i].at[4 * x + 2 * y + c], local_sems.at[i]) for i in range(n)]
        first = []
        for i in range(n):
            first.append(copy(i, 0, (x, y, c), (x, y, 1 - c), src=ins[i]))
            first += [copy(i, 1 + j, (x, y, c), (*chip, c), src=ins[i]) for j, chip in enumerate(chips)]
        return copy, mine, first, chips, (x, y, c)

    def start(self, ins, outs, sems):
        _, mine, first, _, _ = self._plan(ins, outs, sems)
        for cp in mine + first:
            cp.start()

    def finish(self, ins, outs, sems):
        copy, mine, first, chips, (x, y, c) = self._plan(ins, outs, sems)
        n = len(ins)
        passed = []
        for i in range(n):
            for j, chip in enumerate(chips):
                copy(i, 1 + j, (*chip, c), (x, y, c)).wait_recv()
                fw = copy(i, 4 + j, (*chip, c), (x, y, 1 - c))
                fw.start()
                passed.append(fw)
        for i in range(n):
            copy(i, 0, (x, y, 1 - c), (x, y, c)).wait_recv()
            for j, chip in enumerate(chips):
                copy(i, 4 + j, (*chip, 1 - c), (x, y, c)).wait_recv()
        for cp in first + passed:
            cp.wait_send()
        for cp in mine:
            cp.wait()


class _SiblingPart(_Part):
    def __init__(self, names, parts):
        n = len(parts)
        self.names, self.inputs = list(names), list(parts)
        self.out_shapes = [jax.ShapeDtypeStruct((NCHIP,) + a.shape[2:], a.dtype) for a in parts]
        self.sem_shapes = [pltpu.SemaphoreType.DMA((n,)), pltpu.SemaphoreType.DMA((n,))]

    def _copies(self, ins, outs, sems):
        x, y, c = _coords()
        return [pltpu.make_async_remote_copy(
            src_ref=ins[i].at[:, 1 - c], dst_ref=outs[i], send_sem=sems[0].at[i], recv_sem=sems[1].at[i],
            device_id=(x, y, 1 - c), device_id_type=MESH) for i in range(len(ins))]

    def start(self, ins, outs, sems):
        for cp in self._copies(ins, outs, sems):
            cp.start()

    def finish(self, ins, outs, sems):
        cps = self._copies(ins, outs, sems)
        for cp in cps:
            cp.wait_recv()
        for cp in cps:
            cp.wait_send()


class _ChipsPart(_Part):
    def __init__(self, names, sums):
        n = len(sums)
        self.names, self.inputs = list(names), list(sums)
        self.out_shapes = [jax.ShapeDtypeStruct(a.shape, a.dtype) for a in sums]
        self.sem_shapes = [pltpu.SemaphoreType.DMA((n, 3)), pltpu.SemaphoreType.DMA((n, 3)),
                           pltpu.SemaphoreType.DMA((n,))]

    def _plan(self, ins, outs, sems):
        send_sems, recv_sems, local_sems = sems
        x, y, c = _coords()
        my_chip = 2 * x + y
        chips = [(1 - x, y), (x, 1 - y), (1 - x, 1 - y)]
        n = len(ins)
        mine = [pltpu.make_async_copy(ins[i].at[my_chip], outs[i].at[my_chip], local_sems.at[i]) for i in range(n)]
        sends, recvs = [], []
        for i in range(n):
            for k, (px, py) in enumerate(chips):
                sends.append(pltpu.make_async_remote_copy(
                    src_ref=ins[i].at[2 * px + py], dst_ref=outs[i].at[my_chip], send_sem=send_sems.at[i, k],
                    recv_sem=recv_sems.at[i, k], device_id=(px, py, c), device_id_type=MESH))
                recvs.append(pltpu.make_async_remote_copy(
                    src_ref=ins[i].at[my_chip], dst_ref=outs[i].at[2 * px + py], send_sem=send_sems.at[i, k],
                    recv_sem=recv_sems.at[i, k], device_id=(px, py, c), device_id_type=MESH))
        return mine, sends, recvs

    def start(self, ins, outs, sems):
        mine, sends, _ = self._plan(ins, outs, sems)
        for cp in mine + sends:
            cp.start()

    def finish(self, ins, outs, sems):
        mine, sends, recvs = self._plan(ins, outs, sems)
        for cp in recvs:
            cp.wait_recv()
        for cp in sends:
            cp.wait_send()
        for cp in mine:
            cp.wait()


def _call(name, body, grid, in_specs, out_specs, out_shape, args, comm=()):
    comm = [p for p in comm if p is not None]
    single = not isinstance(out_shape, (list, tuple))
    o_specs = [out_specs] if single else list(out_specs)
    o_shape = [out_shape] if single else list(out_shape)
    n_in, n_out = len(in_specs), len(o_specs)
    c_in = [a for p in comm for a in p.inputs]
    c_out = [s for p in comm for s in p.out_shapes]
    c_sem = [s for p in comm for s in p.sem_shapes]

    def wrapped(*refs):
        ins, outs = refs[:n_in], refs[n_in + len(c_in):n_in + len(c_in) + n_out]
        pos = [n_in, n_in + len(c_in) + n_out, n_in + len(c_in) + n_out + len(c_out)]
        split = []
        for p in comm:
            sizes = [len(p.inputs), len(p.out_shapes), len(p.sem_shapes)]
            split.append([refs[o:o + n] for o, n in zip(pos, sizes)])
            pos = [o + n for o, n in zip(pos, sizes)]
        ids = [pl.program_id(a) for a in range(len(grid))]

        def run(which, when):
            def go():
                for p, cut in zip(comm, split):
                    getattr(p, which)(*cut)
            if not comm:
                return
            if ids:
                pl.when(functools.reduce(jnp.logical_and, when))(go)
            else:
                go()

        run("start", [i == 0 for i in ids])
        body(*ins, *outs)
        run("finish", [i == g - 1 for i, g in zip(ids, grid)])

    res = pl.pallas_call(
        wrapped, name=name, grid=grid, in_specs=list(in_specs) + [ANY] * len(c_in),
        out_specs=o_specs + [ANY] * len(c_out), out_shape=o_shape + c_out, scratch_shapes=c_sem,
        compiler_params=_params(len(grid)),
    )(*args, *c_in)
    pos = n_out
    for p in comm:
        p.results = list(res[pos:pos + len(p.out_shapes)])
        pos += len(p.out_shapes)
    return res[0] if single else list(res[:n_out])


def _comm_only(name, comm):
    def body(o_ref):
        o_ref[...] = jnp.zeros_like(o_ref)

    _call(name, body, (), [], pl.BlockSpec(memory_space=pltpu.VMEM), jax.ShapeDtypeStruct((8, 128), F32), [], comm)


def _mm(name, grid, prods, extras, outs, epi, nacc=1, comm=()):
    n_p, n_e = len(prods), len(extras)

    def body(*refs):
        ab, ex, out = refs[:2 * n_p], refs[2 * n_p:2 * n_p + n_e], refs[2 * n_p + n_e:]
        accs = [None] * nacc
        for i, prod in enumerate(prods):
            dn, acc, loop = prod[6], prod[7], prod[8]
            a_ref, b_ref = ab[2 * i], ab[2 * i + 1]
            if loop:
                for g in range(loop):
                    t = lax.dot_general(a_ref[g], b_ref[g], _DN[dn], preferred_element_type=F32)
                    accs[acc] = t if accs[acc] is None else accs[acc] + t
            else:
                t = lax.dot_general(a_ref[...], b_ref[...], _DN[dn], preferred_element_type=F32)
                accs[acc] = t if accs[acc] is None else accs[acc] + t
        epi(accs, ex, out)

    in_specs, args = [], []
    for prod in prods:
        in_specs += [pl.BlockSpec(prod[1], prod[2]), pl.BlockSpec(prod[4], prod[5])]
        args += [prod[0], prod[3]]
    for e, e_blk, e_map in extras:
        in_specs.append(pl.BlockSpec(e_blk, e_map))
        args.append(e)
    return _call(name, body, grid, in_specs, [pl.BlockSpec(blk, mp) for _, _, blk, mp in outs],
                 [jax.ShapeDtypeStruct(s, d) for s, d, _, _ in outs], args, comm)


def _store(accs, ex, out):
    out[0][...] = accs[0].astype(out[0].dtype)


def _ew_tile(r, c, budget=3 << 19):
    for t in range(r - r % 16, 0, -16):
        if r % t == 0 and t * c * 4 <= budget:
            return t, c
    for t in range(c - c % 128, 0, -128):
        if c % t == 0 and r * t * 4 <= budget:
            return r, t
    return r, c


def _tile(n, want):
    t = min(n, want)
    assert n % t == 0, (n, want)
    return t


def _mm_nn(name, a, b, out_dtype, tm=512, tn=512, comm=()):
    m, k = a.shape
    n = b.shape[1]
    tm, tn = _tile(m, tm), (tn if n % tn == 0 else n)
    return _mm(name, (n // tn, m // tm),
               [(a, (tm, k), lambda j, i: (i, 0), b, (k, tn), lambda j, i: (0, j), "nn", 0, 0)], [],
               [((m, n), out_dtype, (tm, tn), lambda j, i: (i, j))], _store, comm=comm)[0]


def _mm_nt(name, a, bt, out_dtype, tm=512, tn=512, comm=()):
    m, k = a.shape
    n = bt.shape[0]
    tm, tn = _tile(m, tm), (tn if n % tn == 0 else n)
    return _mm(name, (n // tn, m // tm),
               [(a, (tm, k), lambda j, i: (i, 0), bt, (tn, k), lambda j, i: (j, 0), "nt", 0, 0)], [],
               [((m, n), out_dtype, (tm, tn), lambda j, i: (i, j))], _store, comm=comm)[0]


def _mm_tn(name, a, b, out_dtype, ta=512, tb=512, scale=None):
    t, ka = a.shape
    nb = b.shape[1]
    ta, tb = (ta if ka % ta == 0 else ka), (tb if nb % tb == 0 else nb)

    def epi(accs, ex, out):
        v = accs[0] if scale is None else accs[0] * scale
        out[0][...] = v.astype(out[0].dtype)

    return _mm(name, (ka // ta, nb // tb),
               [(a, (t, ta), lambda i, j: (0, i), b, (t, tb), lambda i, j: (0, j), "tn", 0, 0)], [],
               [((ka, nb), out_dtype, (ta, tb), lambda i, j: (i, j))], epi)[0]


def _rms_fwd(name, x, g, tm=256):
    s, d = x.shape
    tm = _tile(s, tm)

    def body(x_ref, g_ref, o_ref):
        v = x_ref[...]
        o_ref[...] = (v * lax.rsqrt(jnp.mean(v * v, axis=-1, keepdims=True) + EPS) * g_ref[...]).astype(o_ref.dtype)

    return pl.pallas_call(
        body, name=name, grid=(s // tm,),
        in_specs=[pl.BlockSpec((tm, d), lambda i: (i, 0)), pl.BlockSpec((1, d), lambda i: (0, 0))],
        out_specs=pl.BlockSpec((tm, d), lambda i: (i, 0)), out_shape=jax.ShapeDtypeStruct((s, d), BF),
        compiler_params=_params(1))(x, g)


def _acc_rows(ref, part, i):
    @pl.when(i == 0)
    def _():
        ref[...] = part

    @pl.when(i > 0)
    def _():
        ref[...] += part


def _rms_bwd_math(dn, v, g):
    rstd = lax.rsqrt(jnp.mean(v * v, axis=-1, keepdims=True) + EPS)
    xh = v * rstd
    dxh = dn * g
    dx = rstd * (dxh - xh * jnp.mean(dxh * xh, axis=-1, keepdims=True))
    return dx, jnp.sum(dn * xh, axis=0, keepdims=True)


def _rms_bwd(name, dn, x, g, resid, tm=256):
    s, d = x.shape
    tm = _tile(s, tm)

    def body(dn_ref, x_ref, g_ref, r_ref, dx_ref, dg_ref):
        dx, part = _rms_bwd_math(dn_ref[...].astype(F32), x_ref[...], g_ref[...])
        dx_ref[...] = r_ref[...] + dx
        _acc_rows(dg_ref, part, pl.program_id(0))

    row = pl.BlockSpec((tm, d), lambda i: (i, 0))
    one = pl.BlockSpec((1, d), lambda i: (0, 0))
    return pl.pallas_call(
        body, name=name, grid=(s // tm,), in_specs=[row, row, one, row], out_specs=[row, one],
        out_shape=[jax.ShapeDtypeStruct((s, d), F32), jax.ShapeDtypeStruct((1, d), F32)],
        compiler_params=_params(1))(dn, x, g, resid)


def _loss_head(h, target, g, tm=256):
    s, d = h.shape
    tm = _tile(s, tm)

    def body(h_ref, t_ref, g_ref, dh_ref, dg_ref, loss_ref):
        v, gv = h_ref[...], g_ref[...]
        rstd = lax.rsqrt(jnp.mean(v * v, axis=-1, keepdims=True) + EPS)
        xh = v * rstd
        err = xh * gv - t_ref[...]
        part_loss = 0.5 * jnp.sum(jnp.mean(err * err, axis=-1, keepdims=True), axis=0, keepdims=True)
        dy = err * (1.0 / d)
        dxh = dy * gv
        dh_ref[...] = rstd * (dxh - xh * jnp.mean(dxh * xh, axis=-1, keepdims=True))
        i = pl.program_id(0)
        _acc_rows(dg_ref, jnp.sum(dy * xh, axis=0, keepdims=True), i)
        _acc_rows(loss_ref, jnp.broadcast_to(part_loss, loss_ref.shape), i)

    row = pl.BlockSpec((tm, d), lambda i: (i, 0))
    one = pl.BlockSpec((1, d), lambda i: (0, 0))
    return pl.pallas_call(
        body, name="loss_head", grid=(s // tm,), in_specs=[row, row, one],
        out_specs=[row, one, pl.BlockSpec((1, 128), lambda i: (0, 0))],
        out_shape=[jax.ShapeDtypeStruct((s, d), F32), jax.ShapeDtypeStruct((1, d), F32),
                   jax.ShapeDtypeStruct((1, 128), F32)],
        compiler_params=_params(1))(h, target, g)


def _pl_bwd_elem(dh, pe, t, tm=256):
    s, d = dh.shape
    tm = _tile(s, tm)

    def body(dh_ref, pe_ref, t_ref, dt_ref, dpe_ref):
        dh_v, sg = dh_ref[...], _sig(t_ref[...])
        dt_ref[...] = (dh_v * pe_ref[...].astype(F32) * sg * (1.0 - sg)).astype(BF)
        dpe_ref[...] = (dh_v * sg).astype(BF)

    row = pl.BlockSpec((tm, d), lambda i: (i, 0))
    return pl.pallas_call(
        body, name="pl_bwd_elem", grid=(s // tm,), in_specs=[row, row, row], out_specs=[row, row],
        out_shape=[jax.ShapeDtypeStruct((s, d), BF)] * 2, compiler_params=_params(1))(dh, pe, t)


def _ffn_up(name, xn, wg, wu, tm=512, comm=()):
    s, d = xn.shape
    g, fb, _ = wg.shape
    tm = _tile(s, tm)

    def epi(accs, ex, out):
        hg, hu = accs
        out[0][...] = hg.astype(BF)
        out[1][...] = hu.astype(BF)
        out[2][...] = (hg * _sig(hg) * hu).astype(BF)

    a_map = lambda j, i: (i, 0)
    w_map = lambda j, i: (j, 0, 0)
    o = ((g, s, fb), BF, (None, tm, fb), lambda j, i: (j, i, 0))
    return _mm(name, (g, s // tm),
               [(xn, (tm, d), a_map, wg, (None, fb, d), w_map, "nt", 0, 0),
                (xn, (tm, d), a_map, wu, (None, fb, d), w_map, "nt", 1, 0)], [], [o, o, o], epi, nacc=2, comm=comm)


def _ffn_down(name, a, wd, resid, tm=512, tn=512, comm=()):
    g, s, fb = a.shape
    d = wd.shape[2]
    tm, tn = _tile(s, tm), _tile(d, tn)

    def epi(accs, ex, out):
        out[0][...] = ex[0][...] + 0.5 * accs[0]

    return _mm(name, (d // tn, s // tm),
               [(a, (g, tm, fb), lambda j, i: (0, i, 0), wd, (g, fb, tn), lambda j, i: (0, 0, j), "nn", 0, g)],
               [(resid, (tm, tn), lambda j, i: (i, j))],
               [((s, d), F32, (tm, tn), lambda j, i: (i, j))], epi, comm=comm)[0]


def _ffn_bwd_act(name, dh, wd, hg, hu, tm=512, comm=()):
    s, d = dh.shape
    g, fb, _ = wd.shape
    tm = _tile(s, tm)

    def epi(accs, ex, out):
        da = 0.5 * accs[0]
        hg_v, hu_v = ex[0][...].astype(F32), ex[1][...].astype(F32)
        sg = _sig(hg_v)
        out[0][...] = (da * hu_v * (sg * (1.0 + hg_v * (1.0 - sg)))).astype(BF)
        out[1][...] = (da * (hg_v * sg)).astype(BF)

    blk = (None, tm, fb)
    gmap = lambda j, i: (j, i, 0)
    return _mm(name, (g, s // tm),
               [(dh, (tm, d), lambda j, i: (i, 0), wd, (None, fb, d), lambda j, i: (j, 0, 0), "nt", 0, 0)],
               [(hg, blk, gmap), (hu, blk, gmap)],
               [((g, s, fb), BF, blk, gmap), ((g, s, fb), BF, blk, gmap)], epi, comm=comm)


def _ffn_bwd_wd(name, a, dh, tn=512, comm=()):
    g, s, fb = a.shape
    d = dh.shape[1]
    tn = _tile(d, tn)

    def epi(accs, ex, out):
        out[0][...] = (0.5 * accs[0]).astype(BF)

    return _mm(name, (g, d // tn),
               [(a, (None, s, fb), lambda j, i: (j, 0, 0), dh, (s, tn), lambda j, i: (0, i), "tn", 0, 0)], [],
               [((g, fb, d), BF, (None, fb, tn), lambda j, i: (j, 0, i))], epi, comm=comm)[0]


def _ffn_bwd_wup(name, xn, dhg, dhu, tk=512, comm=()):
    s, d = xn.shape
    g, _, fb = dhg.shape
    tk = _tile(d, tk)

    def epi(accs, ex, out):
        out[0][...] = accs[0].astype(BF)
        out[1][...] = accs[1].astype(BF)

    a_map = lambda j, i: (j, 0, 0)
    b_map = lambda j, i: (0, i)
    o = ((g, fb, d), BF, (None, fb, tk), lambda j, i: (j, 0, i))
    return _mm(name, (g, d // tk),
               [(dhg, (None, s, fb), a_map, xn, (s, tk), b_map, "tn", 0, 0),
                (dhu, (None, s, fb), a_map, xn, (s, tk), b_map, "tn", 1, 0)], [], [o, o], epi, nacc=2, comm=comm)


def _ffn_bwd_x(name, dhg, dhu, wg, wu, tm=256, tn=256, comm=()):
    g, s, fb = dhg.shape
    d = wg.shape[2]
    tm, tn = _tile(s, tm), _tile(d, tn)
    a_blk, a_map = (g, tm, fb), lambda j, i: (0, i, 0)
    b_blk, b_map = (g, fb, tn), lambda j, i: (0, 0, j)
    return _mm(name, (d // tn, s // tm),
               [(dhg, a_blk, a_map, wg, b_blk, b_map, "nn", 0, g), (dhu, a_blk, a_map, wu, b_blk, b_map, "nn", 0, g)],
               [], [((s, d), F32, (tm, tn), lambda j, i: (i, j))], _store, comm=comm)[0]


def _ffn_forward(tag, h, gain, wg, wu, get_wd, up_comm=(), down_comm=()):
    xn = _rms_fwd(tag + "_norm", h, gain)
    hg, hu, a = _ffn_up(tag + "_up", xn, wg, wu, comm=up_comm)
    return _ffn_down(tag + "_down", a, get_wd(), h, comm=down_comm), (xn, hg, hu, a)


def _na_geometry(rows):
    kh = min(NA_ROWS_WIN, rows)
    cols = np.arange(GRID_W)
    col_start = np.clip(cols - NA_COLS_WIN // 2, 0, GRID_W - NA_COLS_WIN)
    mask = (cols[None, :] >= col_start[:, None]) & (cols[None, :] < col_start[:, None] + NA_COLS_WIN)
    dc = np.clip(cols[None, :] - cols[:, None], -(NA_COLS_WIN - 1), NA_COLS_WIN - 1) + (NA_COLS_WIN - 1)
    return kh, mask, dc


def _na_bias_tables(rpb, rows):
    kh, mask, dc = _na_geometry(rows)
    t = jnp.where(jnp.asarray(mask)[None, None], rpb[:, :, dc], NEG)
    tb = jnp.stack([t[:, d0:d0 + kh] for d0 in range(NA_ROWS_WIN)], 0)
    return tb.transpose(0, 1, 3, 2, 4).reshape(NA_ROWS_WIN, NA_HEADS, GRID_W, kh * GRID_W)


def _na_row_start(r, rows, kh):
    return jnp.clip(r - kh // 2, 0, rows - kh)


def _na_specs(s, rows, kh):
    hw = NA_HG * NA_DIM
    nq = NA_HEADS // NA_HG
    q_spec = pl.BlockSpec((GRID_W, hw), lambda j, r: (r, j))
    k_spec = pl.BlockSpec((s, hw), lambda j, r: (0, nq + j))
    v_spec = pl.BlockSpec((s, hw), lambda j, r: (0, 2 * nq + j))
    b_spec = pl.BlockSpec((None, NA_HG, GRID_W, kh * GRID_W),
                          lambda j, r: (_na_row_start(r, rows, kh) - r + NA_ROWS_WIN - 1, j, 0, 0))
    return q_spec, k_spec, v_spec, b_spec, hw, nq


def _na_probs(q, k, bias):
    sc = lax.dot_general(q, k, _DN["nt"], preferred_element_type=F32) * (NA_DIM ** -0.5) + bias
    e = jnp.exp(sc - jnp.max(sc, axis=-1, keepdims=True))
    return e / jnp.sum(e, axis=-1, keepdims=True)


def _na_fwd(qkv, tb, comm=()):
    s = qkv.shape[0]
    rows = s // GRID_W
    kh = min(NA_ROWS_WIN, rows)
    q_spec, k_spec, v_spec, b_spec, hw, nq = _na_specs(s, rows, kh)

    def body(q_ref, k_ref, v_ref, b_ref, o_ref):
        r = pl.program_id(1)
        start = pl.multiple_of(_na_row_start(r, rows, kh) * GRID_W, GRID_W)
        for h in range(NA_HG):
            cs = slice(h * NA_DIM, (h + 1) * NA_DIM)
            p = _na_probs(q_ref[:, cs], k_ref[pl.ds(start, kh * GRID_W), cs], b_ref[h])
            o_ref[:, cs] = jnp.dot(p.astype(BF), v_ref[pl.ds(start, kh * GRID_W), cs],
                                   preferred_element_type=F32).astype(BF)

    return _call("na_fwd", body, (nq, rows), [q_spec, k_spec, v_spec, b_spec],
                 pl.BlockSpec((GRID_W, hw), lambda j, r: (r, j)),
                 jax.ShapeDtypeStruct((s, NA_HEADS * NA_DIM), BF), [qkv, qkv, qkv, tb], comm)


def _na_bwd(qkv, tb, do, comm=()):
    s = qkv.shape[0]
    rows = s // GRID_W
    kh = min(NA_ROWS_WIN, rows)
    q_spec, k_spec, v_spec, b_spec, hw, nq = _na_specs(s, rows, kh)
    nd = 2 * NA_ROWS_WIN - 1

    def body(q_ref, k_ref, v_ref, b_ref, do_ref, dq_ref, dk_ref, dv_ref, dt_ref):
        r = pl.program_id(1)

        @pl.when(r == 0)
        def _():
            dk_ref[...] = jnp.zeros_like(dk_ref)
            dv_ref[...] = jnp.zeros_like(dv_ref)
            dt_ref[...] = jnp.zeros_like(dt_ref)

        rs = _na_row_start(r, rows, kh)
        d0 = rs - r + NA_ROWS_WIN - 1
        win = pl.ds(pl.multiple_of(rs * GRID_W, GRID_W), kh * GRID_W)
        for h in range(NA_HG):
            cs = slice(h * NA_DIM, (h + 1) * NA_DIM)
            q, k, v, do_h = q_ref[:, cs], k_ref[win, cs], v_ref[win, cs], do_ref[:, cs]
            p = _na_probs(q, k, b_ref[h])
            dp = lax.dot_general(do_h, v, _DN["nt"], preferred_element_type=F32)
            ds = p * (dp - jnp.sum(p * dp, axis=-1, keepdims=True))
            for i in range(kh):
                dt_ref[h, d0 + i] += ds[:, i * GRID_W:(i + 1) * GRID_W]
            dsb = (ds * (NA_DIM ** -0.5)).astype(BF)
            dq_ref[:, cs] = jnp.dot(dsb, k, preferred_element_type=F32).astype(BF)
            dk_ref[win, cs] += lax.dot_general(dsb, q, _DN["tn"], preferred_element_type=F32)
            dv_ref[win, cs] += lax.dot_general(p.astype(BF), do_h, _DN["tn"], preferred_element_type=F32)

    width = NA_HEADS * NA_DIM
    whole = pl.BlockSpec((s, hw), lambda j, r: (0, j))
    return _call(
        "na_bwd", body, (nq, rows),
        [q_spec, k_spec, v_spec, b_spec, pl.BlockSpec((GRID_W, hw), lambda j, r: (r, j))],
        [pl.BlockSpec((GRID_W, hw), lambda j, r: (r, j)), whole, whole,
         pl.BlockSpec((NA_HG, nd, GRID_W, GRID_W), lambda j, r: (j, 0, 0, 0))],
        [jax.ShapeDtypeStruct((s, width), BF), jax.ShapeDtypeStruct((s, width), F32),
         jax.ShapeDtypeStruct((s, width), F32), jax.ShapeDtypeStruct((NA_HEADS, nd, GRID_W, GRID_W), F32)],
        [qkv, qkv, qkv, tb, do], comm)


def _na_rpb_grad(dt, rows):
    _, mask, dc = _na_geometry(rows)
    nd, nc = 2 * NA_ROWS_WIN - 1, 2 * NA_COLS_WIN - 1
    onehot = np.zeros((GRID_W * GRID_W, 128), np.float32)
    onehot[np.arange(GRID_W * GRID_W), dc.reshape(-1)] = mask.reshape(-1).astype(np.float32)
    flat = dt.reshape(NA_HEADS * nd, GRID_W * GRID_W)

    def body(a_ref, e_ref, o_ref):
        o_ref[...] = jnp.dot(a_ref[...], e_ref[...], precision=HI, preferred_element_type=F32)

    out = pl.pallas_call(body, name="na_rpb_grad", out_shape=jax.ShapeDtypeStruct((NA_HEADS * nd, 128), F32),
                         compiler_params=_params(0))(flat, jnp.asarray(onehot))
    return out[:, :nc].reshape(NA_HEADS, nd, nc)


def _rope_consts(s):
    pos = np.arange(s, dtype=np.float32)
    inv = (1.0 / (ROPE_THETA ** (np.arange(0, ML_ROPE, 2, dtype=np.float32) / ML_ROPE))).astype(np.float32)
    ang = pos[:, None] * inv[None, :]
    cos, sin = np.cos(ang).astype(np.float32), np.sin(ang).astype(np.float32)
    half = ML_ROPE // 2
    rot = np.zeros((ML_ROPE, ML_ROPE), np.float32)
    rot[np.arange(half) + half, np.arange(half)] = -1.0
    rot[np.arange(half), np.arange(half) + half] = 1.0
    return (jnp.asarray(np.concatenate([cos, cos], 1)), jnp.asarray(np.concatenate([sin, sin], 1)),
            jnp.asarray(rot), jnp.asarray(rot.T.copy()))


def _rope(v, cos, sin, rot):
    return v * cos + jnp.dot(v, rot, precision=HI, preferred_element_type=F32) * sin


def _unrope(dv, cos, sin, rot_t):
    return dv * cos + jnp.dot(dv * sin, rot_t, precision=HI, preferred_element_type=F32)


def _rms(v, g):
    return v * lax.rsqrt(jnp.mean(v * v, axis=-1, keepdims=True) + EPS) * g


def _mla_prep(lat, gq, gkv, cos, sin, rot, tm=256):
    s, w = lat.shape
    tm = _tile(s, tm)

    def body(l_ref, gq_ref, gkv_ref, c_ref, s_ref, r_ref, cq_ref, ckv_ref, kr_ref):
        cq_ref[...] = _rms(l_ref[:, :ML_RANK], gq_ref[...]).astype(BF)
        ckv_ref[...] = _rms(l_ref[:, ML_RANK:2 * ML_RANK], gkv_ref[...]).astype(BF)
        kr_ref[...] = _rope(l_ref[:, 2 * ML_RANK:], c_ref[...], s_ref[...], r_ref[...]).astype(BF)

    row = lambda c: pl.BlockSpec((tm, c), lambda i: (i, 0))
    full = lambda a: pl.BlockSpec(a.shape, lambda i: (0, 0))
    return pl.pallas_call(
        body, name="mla_prep", grid=(s // tm,),
        in_specs=[row(w), full(gq), full(gkv), row(ML_ROPE), row(ML_ROPE), full(rot)],
        out_specs=[row(ML_RANK), row(ML_RANK), row(ML_ROPE)],
        out_shape=[jax.ShapeDtypeStruct((s, ML_RANK), BF), jax.ShapeDtypeStruct((s, ML_RANK), BF),
                   jax.ShapeDtypeStruct((s, ML_ROPE), BF)],
        compiler_params=_params(1))(lat, gq, gkv, cos, sin, rot)


def _mla_q_proj(cq, wuq, cos, sin, rot, tm=512):
    s, k = cq.shape
    tm = _tile(s, tm)

    def epi(accs, ex, out):
        acc = accs[0]
        out[0][:, :ML_NOPE] = acc[:, :ML_NOPE].astype(BF)
        out[0][:, ML_NOPE:] = _rope(acc[:, ML_NOPE:], ex[0][...], ex[1][...], ex[2][...]).astype(BF)

    rmap = lambda j, i: (i, 0)
    return _mm("mla_q_proj", (ML_HEADS, s // tm),
               [(cq, (tm, k), rmap, wuq, (None, ML_QK, k), lambda j, i: (j, 0, 0), "nt", 0, 0)],
               [(cos, (tm, ML_ROPE), rmap), (sin, (tm, ML_ROPE), rmap), (rot, rot.shape, lambda j, i: (0, 0))],
               [((ML_HEADS, s, ML_QK), BF, (None, tm, ML_QK), lambda j, i: (j, i, 0))], epi)[0]


def _mla_kv_proj(ckv, wukv, kr, tm=512):
    s, k = ckv.shape
    tm = _tile(s, tm)

    def epi(accs, ex, out):
        acc = accs[0]
        out[0][:, :ML_NOPE] = acc[:, :ML_NOPE].astype(BF)
        out[0][:, ML_NOPE:] = ex[0][...]
        out[1][...] = acc[:, ML_NOPE:].astype(BF)

    rmap = lambda j, i: (i, 0)
    gmap = lambda j, i: (j, i, 0)
    return _mm("mla_kv_proj", (ML_HEADS, s // tm),
               [(ckv, (tm, k), rmap, wukv, (None, k, ML_NOPE + ML_V), lambda j, i: (j, 0, 0), "nn", 0, 0)],
               [(kr, (tm, ML_ROPE), rmap)],
               [((ML_HEADS, s, ML_QK), BF, (None, tm, ML_QK), gmap), ((ML_HEADS, s, ML_V), BF, (None, tm, ML_V), gmap)],
               epi)


def _mla_probs(q, k):
    sc = lax.dot_general(q, k, _DN["nt"], preferred_element_type=F32) * (ML_QK ** -0.5)
    e = jnp.exp(sc - jnp.max(sc, axis=-1, keepdims=True))
    return e / jnp.sum(e, axis=-1, keepdims=True)


def _mla_fwd(q, k, v, tq=512, comm=()):
    _, s, _ = q.shape
    tq = _tile(s, tq)

    def body(q_ref, k_ref, v_ref, o_ref):
        p = _mla_probs(q_ref[...], k_ref[...])
        o_ref[...] = jnp.dot(p.astype(BF), v_ref[...], preferred_element_type=F32).astype(BF)

    return _call("mla_fwd", body, (ML_HEADS, s // tq),
                 [pl.BlockSpec((None, tq, ML_QK), lambda h, i: (h, i, 0)),
                  pl.BlockSpec((None, s, ML_QK), lambda h, i: (h, 0, 0)),
                  pl.BlockSpec((None, s, ML_V), lambda h, i: (h, 0, 0))],
                 pl.BlockSpec((tq, ML_V), lambda h, i: (i, h)),
                 jax.ShapeDtypeStruct((s, ML_HEADS * ML_V), BF), [q, k, v], comm)


def _mla_bwd(q, k, v, do, tq=256, comm=()):
    _, s, _ = q.shape
    tq = _tile(s, tq)

    def body(q_ref, k_ref, v_ref, do_ref, dq_ref, dk_ref, dv_ref):
        i = pl.program_id(1)
        qv, kv, vv, dov = q_ref[...], k_ref[...], v_ref[...], do_ref[...]
        p = _mla_probs(qv, kv)
        dp = lax.dot_general(dov, vv, _DN["nt"], preferred_element_type=F32)
        ds = (p * (dp - jnp.sum(p * dp, axis=-1, keepdims=True)) * (ML_QK ** -0.5)).astype(BF)
        dq_ref[...] = jnp.dot(ds, kv, preferred_element_type=F32)
        _acc_rows(dk_ref, lax.dot_general(ds, qv, _DN["tn"], preferred_element_type=F32), i)
        _acc_rows(dv_ref, lax.dot_general(p.astype(BF), dov, _DN["tn"], preferred_element_type=F32), i)

    return _call(
        "mla_bwd", body, (ML_HEADS, s // tq),
        [pl.BlockSpec((None, tq, ML_QK), lambda h, i: (h, i, 0)),
         pl.BlockSpec((None, s, ML_QK), lambda h, i: (h, 0, 0)),
         pl.BlockSpec((None, s, ML_V), lambda h, i: (h, 0, 0)),
         pl.BlockSpec((tq, ML_V), lambda h, i: (i, h))],
        [pl.BlockSpec((None, tq, ML_QK), lambda h, i: (h, i, 0)),
         pl.BlockSpec((None, s, ML_QK), lambda h, i: (h, 0, 0)),
         pl.BlockSpec((None, s, ML_V), lambda h, i: (h, 0, 0))],
        [jax.ShapeDtypeStruct((ML_HEADS, s, ML_QK), F32), jax.ShapeDtypeStruct((ML_HEADS, s, ML_QK), F32),
         jax.ShapeDtypeStruct((ML_HEADS, s, ML_V), F32)],
        [q, k, v, do], comm)


def _mla_post(dq, dk, dv, cos, sin, rot_t, tm=256):
    _, s, _ = dq.shape
    tm = _tile(s, tm)

    def body(dq_ref, dk_ref, dv_ref, c_ref, s_ref, r_ref, dqp_ref, dkv_ref, dkr_ref):
        h = pl.program_id(1)
        dqv, dkk = dq_ref[...], dk_ref[...]
        dqp_ref[:, :ML_NOPE] = dqv[:, :ML_NOPE].astype(BF)
        dqp_ref[:, ML_NOPE:] = _unrope(dqv[:, ML_NOPE:], c_ref[...], s_ref[...], r_ref[...]).astype(BF)
        dkv_ref[:, :ML_NOPE] = dkk[:, :ML_NOPE].astype(BF)
        dkv_ref[:, ML_NOPE:] = dv_ref[...].astype(BF)
        _acc_rows(dkr_ref, dkk[:, ML_NOPE:], h)

    gspec = lambda c: pl.BlockSpec((None, tm, c), lambda i, h: (h, i, 0))
    rspec = pl.BlockSpec((tm, ML_ROPE), lambda i, h: (i, 0))
    return pl.pallas_call(
        body, name="mla_post", grid=(s // tm, ML_HEADS),
        in_specs=[gspec(ML_QK), gspec(ML_QK), gspec(ML_V), rspec, rspec,
                  pl.BlockSpec(rot_t.shape, lambda i, h: (0, 0))],
        out_specs=[gspec(ML_QK), gspec(ML_NOPE + ML_V), rspec],
        out_shape=[jax.ShapeDtypeStruct((ML_HEADS, s, ML_QK), BF),
                   jax.ShapeDtypeStruct((ML_HEADS, s, ML_NOPE + ML_V), BF),
                   jax.ShapeDtypeStruct((s, ML_ROPE), F32)],
        compiler_params=_params(2))(dq, dk, dv, cos, sin, rot_t)


def _mla_lat_bwd(dcq, dckv, dkr, lat, gq, gkv, cos, sin, rot_t, tm=256):
    s, w = lat.shape
    tm = _tile(s, tm)

    def body(dcq_ref, dckv_ref, dkr_ref, l_ref, gq_ref, gkv_ref, c_ref, s_ref, r_ref, dl_ref, dgq_ref, dgkv_ref):
        i = pl.program_id(0)
        dql, pq = _rms_bwd_math(dcq_ref[...], l_ref[:, :ML_RANK], gq_ref[...])
        dkl, pkv = _rms_bwd_math(dckv_ref[...], l_ref[:, ML_RANK:2 * ML_RANK], gkv_ref[...])
        dl_ref[:, :ML_RANK] = dql.astype(BF)
        dl_ref[:, ML_RANK:2 * ML_RANK] = dkl.astype(BF)
        dl_ref[:, 2 * ML_RANK:] = _unrope(dkr_ref[...], c_ref[...], s_ref[...], r_ref[...]).astype(BF)
        _acc_rows(dgq_ref, pq, i)
        _acc_rows(dgkv_ref, pkv, i)

    row = lambda c: pl.BlockSpec((tm, c), lambda i: (i, 0))
    full = lambda a: pl.BlockSpec(a.shape, lambda i: (0, 0))
    return pl.pallas_call(
        body, name="mla_lat_bwd", grid=(s // tm,),
        in_specs=[row(ML_RANK), row(ML_RANK), row(ML_ROPE), row(w), full(gq), full(gkv), row(ML_ROPE), row(ML_ROPE),
                  full(rot_t)],
        out_specs=[row(w), full(gq), full(gkv)],
        out_shape=[jax.ShapeDtypeStruct((s, w), BF), jax.ShapeDtypeStruct(gq.shape, F32),
                   jax.ShapeDtypeStruct(gkv.shape, F32)],
        compiler_params=_params(1))(dcq, dckv, dkr, lat, gq, gkv, cos, sin, rot_t)


def _grp_dw(name, a, dout, ta=512):
    s, k = a.shape
    ta = _tile(k, ta)
    if dout.ndim == 3:
        g, _, nb = dout.shape
        b_blk, b_map = (None, s, nb), lambda j, i: (j, 0, 0)
    else:
        g, nb = NDEV, dout.shape[1] // NDEV
        b_blk, b_map = (s, nb), lambda j, i: (0, j)
    return _mm(name, (g, k // ta),
               [(a, (s, ta), lambda j, i: (0, i), dout, b_blk, b_map, "tn", 0, 0)], [],
               [((g, k, nb), BF, (None, ta, nb), lambda j, i: (j, i, 0))], _store)[0]


def _grp_dw_t(name, dout, a, ta=512):
    g, s, nb = dout.shape
    k = a.shape[1]
    ta = _tile(k, ta)
    return _mm(name, (g, k // ta),
               [(dout, (None, s, nb), lambda j, i: (j, 0, 0), a, (s, ta), lambda j, i: (0, i), "tn", 0, 0)], [],
               [((g, nb, k), BF, (None, nb, ta), lambda j, i: (j, 0, i))], _store)[0]


def _grp_dx_t(name, dout, wt, tm=512, tn=512, comm=()):
    g, s, nb = dout.shape
    k = wt.shape[2]
    tm, tn = _tile(s, tm), _tile(k, tn)
    return _mm(name, (k // tn, s // tm),
               [(dout, (g, tm, nb), lambda j, i: (0, i, 0), wt, (g, nb, tn), lambda j, i: (0, 0, j), "nn", 0, g)], [],
               [((s, k), F32, (tm, tn), lambda j, i: (i, j))], _store, comm=comm)[0]


def _grp_dx(name, dout, w, tm=512, tn=512, comm=()):
    g, s, nb = dout.shape
    k = w.shape[1]
    tm, tn = _tile(s, tm), _tile(k, tn)
    return _mm(name, (k // tn, s // tm),
               [(dout, (g, tm, nb), lambda j, i: (0, i, 0), w, (g, tn, nb), lambda j, i: (0, j, 0), "nt", 0, g)], [],
               [((s, k), F32, (tm, tn), lambda j, i: (i, j))], _store, comm=comm)[0]


def _row_dw(name, a, dout, tn=512):
    s, n = dout.shape
    tn = _tile(n, tn)
    if a.ndim == 3:
        kb = a.shape[2]
        a_blk, a_map = (None, s, kb), lambda j, i: (j, 0, 0)
    else:
        kb = a.shape[1] // NDEV
        a_blk, a_map = (s, kb), lambda j, i: (0, j)
    return _mm(name, (NDEV, n // tn),
               [(a, a_blk, a_map, dout, (s, tn), lambda j, i: (0, i), "tn", 0, 0)], [],
               [((NDEV, kb, n), BF, (None, kb, tn), lambda j, i: (j, 0, i))], _store)[0]


def _mix_merge(oa, ob, wa, wb, ga, gb, tm=512, comm=()):
    s, k = oa.shape
    g, _, nb = wa.shape
    tm = _tile(s, tm)

    def epi(accs, ex, out):
        ya, yb = accs
        out[0][...] = ya.astype(BF)
        out[1][...] = yb.astype(BF)
        out[2][...] = (_sig(ex[0][...]) * ya + _sig(ex[1][...]) * yb).astype(BF)

    rmap = lambda j, i: (i, 0)
    wmap = lambda j, i: (j, 0, 0)
    o = ((g, s, nb), BF, (None, tm, nb), lambda j, i: (j, i, 0))
    cmap = lambda j, i: (i, j)
    return _mm("mix_merge", (g, s // tm),
               [(oa, (tm, k), rmap, wa, (None, k, nb), wmap, "nn", 0, 0),
                (ob, (tm, k), rmap, wb, (None, k, nb), wmap, "nn", 1, 0)],
               [(ga, (tm, nb), cmap), (gb, (tm, nb), cmap)], [o, o, o], epi, nacc=2, comm=comm)


def _mix_out(merged, wout, resid, tm=512, tn=512):
    g, s, kb = merged.shape
    d = wout.shape[2]
    tm, tn = _tile(s, tm), _tile(d, tn)

    def epi(accs, ex, out):
        out[0][...] = ex[0][...] + accs[0]

    return _mm("mix_out", (d // tn, s // tm),
               [(merged, (g, tm, kb), lambda j, i: (0, i, 0), wout, (g, kb, tn), lambda j, i: (0, 0, j), "nn", 0, g)],
               [(resid, (tm, tn), lambda j, i: (i, j))],
               [((s, d), F32, (tm, tn), lambda j, i: (i, j))], epi)[0]


def _mix_out_bwd(dh, wout, ga, gb, ya, yb, tm=512, comm=()):
    s, d = dh.shape
    g, kb, _ = wout.shape
    tm = _tile(s, tm)

    def epi(accs, ex, out):
        dm = accs[0]
        sa, sb = _sig(ex[0][...]), _sig(ex[1][...])
        out[0][...] = (dm * sa).astype(BF)
        out[1][...] = (dm * sb).astype(BF)
        out[2][...] = (dm * ex[2][...].astype(F32) * sa * (1.0 - sa)).astype(BF)
        out[3][...] = (dm * ex[3][...].astype(F32) * sb * (1.0 - sb)).astype(BF)

    cmap = lambda j, i: (i, j)
    gmap = lambda j, i: (j, i, 0)
    og = ((g, s, kb), BF, (None, tm, kb), gmap)
    oc = ((s, g * kb), BF, (tm, kb), cmap)
    return _mm("mix_out_bwd", (g, s // tm),
               [(dh, (tm, d), lambda j, i: (i, 0), wout, (None, kb, d), lambda j, i: (j, 0, 0), "nt", 0, 0)],
               [(ga, (tm, kb), cmap), (gb, (tm, kb), cmap), (ya, (None, tm, kb), gmap), (yb, (None, tm, kb), gmap)],
               [og, og, oc, oc], epi, comm=comm)


def _pl_forward(n4, wplg, p, wpl, h3, tm=512):
    s, d = n4.shape
    g, kb, _ = wplg.shape
    kp, nb = wpl.shape[1], wpl.shape[2]
    tm = _tile(s, tm)
    wplg_nat = wplg.reshape(g * kb, d)

    def epi(accs, ex, out):
        t, pe = accs
        out[0][...] = ex[0][...] + _sig(t) * pe
        out[1][...] = t
        out[2][...] = pe.astype(BF)

    rmap = lambda j, i: (i, 0)
    cmap = lambda j, i: (i, j)
    return _mm("pl_forward", (g, s // tm),
               [(n4, (tm, d), rmap, wplg_nat, (g * kb, nb), lambda j, i: (0, j), "nn", 0, 0),
                (p, (tm, kp), rmap, wpl, (None, kp, nb), lambda j, i: (j, 0, 0), "nn", 1, 0)],
               [(h3, (tm, nb), cmap)],
               [((s, d), F32, (tm, nb), cmap), ((s, d), F32, (tm, nb), cmap), ((s, d), BF, (tm, nb), cmap)],
               epi, nacc=2)


def _row_dx(name, dout, w, tm=512, comm=()):
    s, n = dout.shape
    g, kb, _ = w.shape
    tm = _tile(s, tm)
    return _mm(name, (g, s // tm),
               [(dout, (tm, n), lambda j, i: (i, 0), w, (None, kb, n), lambda j, i: (j, 0, 0), "nt", 0, 0)], [],
               [((s, g * kb), F32, (tm, kb), lambda j, i: (i, j))], _store, comm=comm)[0]


def _in_proj_bwd_x(pieces, weights, tm=256, tn=256, comm=()):
    s = pieces[0].shape[0]
    d = weights[0].shape[1]
    tm, tn = _tile(s, tm), _tile(d, tn)
    prods = [(pc, (tm, pc.shape[1]), lambda j, i: (i, 0), w, (w.shape[0], tn), lambda j, i: (0, j), "nn", 0, 0)
             for pc, w in zip(pieces, weights)]
    return _mm("in_proj_dx", (d // tn, s // tm), prods, [],
               [((s, d), F32, (tm, tn), lambda j, i: (i, j))], _store, comm=comm)[0]


def _split_w_in(w_in_t):
    g, nb, d = w_in_t.shape
    nat = w_in_t.reshape(g * nb, d)
    na, lat = 3 * NA_HEADS * NA_DIM, 2 * ML_RANK + ML_ROPE
    return nat[:na], nat[na:na + lat], nat[na + lat:na + lat + d], nat[na + lat + d:]


def _pair_sum(name, part, landed, core):
    _, _, r, c = part.shape
    tr, tc = _ew_tile(r, c)

    def body(core_ref, a_ref, b_ref, o_ref):
        o_ref[...] = (a_ref[...].astype(F32) + b_ref[...].astype(F32)).astype(o_ref.dtype)

    return pl.pallas_call(
        body, name=name,
        grid_spec=pltpu.PrefetchScalarGridSpec(
            num_scalar_prefetch=1, grid=(NCHIP, r // tr, c // tc),
            in_specs=[pl.BlockSpec((None, None, tr, tc), lambda j, i, k, core_ref: (j, core_ref[0], i, k)),
                      pl.BlockSpec((None, tr, tc), lambda j, i, k, core_ref: (j, i, k))],
            out_specs=pl.BlockSpec((None, tr, tc), lambda j, i, k, core_ref: (j, i, k))),
        out_shape=jax.ShapeDtypeStruct(landed.shape, landed.dtype), compiler_params=_params(3),
    )(core, part, landed)


def _device_step(x, p, target, sp, own, core):
    s, d = x.shape
    rows = s // GRID_W
    cos, sin, rot, rot_t = _rope_consts(s)
    w, dw4, sums, chip_parts, dsp = {}, {}, {}, {}, {}

    def gather(*names):
        return _GatherPart(names, [own[n] for n in names])

    def got(part):
        w.update(zip(part.names, part.results))

    def grad(name, g):
        dw4[name] = g.reshape((NCHIP, 2) + g.shape[1:])

    def to_sibling(*names):
        return _SiblingPart(names, [dw4[n] for n in names])

    def add_pairs(part):
        for n, landed in zip(part.names, part.results):
            sums[n] = _pair_sum("pair_sum_" + n, dw4[n], landed, core)

    def to_chips(*names):
        return _ChipsPart(names, [sums[n] for n in names])

    def done(part):
        chip_parts.update(zip(part.names, part.results))

    c0 = gather("ffn1_w_gate", "ffn1_w_up")
    _comm_only("gather_ffn1", [c0])
    got(c0)
    c1 = gather("ffn1_w_down")
    c2 = gather("w_in")

    def ffn1_wd():
        got(c1)
        return w["ffn1_w_down"]

    h1, ffn1_saved = _ffn_forward("ffn1", x, sp["ffn1_norm"], w["ffn1_w_gate"], w["ffn1_w_up"], ffn1_wd,
                                  up_comm=[c1], down_comm=[c2])
    got(c2)
    wqkv, wlat, wga, wgb = _split_w_in(w["w_in"])
    u = _rms_fwd("mix_norm", h1, sp["mix_norm"])
    c3 = gather("w_uq", "w_ukv", "w_out")
    qkv = _mm_nt("in_qkv", u, wqkv, BF, tn=1024, comm=[c3])
    got(c3)
    lat = _mm_nt("in_lat", u, wlat, F32)
    c3a = gather("w_branch_a")
    ga = _mm_nt("in_ga", u, wga, F32, tn=1024, comm=[c3a])
    got(c3a)
    c3b = gather("w_branch_b")
    gb = _mm_nt("in_gb", u, wgb, F32, tn=1024, comm=[c3b])
    got(c3b)
    tb = _na_bias_tables(sp["na_rpb"], rows)
    c4 = gather("ffn2_w_gate")
    oa = _na_fwd(qkv, tb, comm=[c4])
    got(c4)
    cq, ckv, kr = _mla_prep(lat, sp["q_a_norm"], sp["kv_a_norm"], cos, sin, rot)
    qf = _mla_q_proj(cq, w["w_uq"], cos, sin, rot)
    kf, vf = _mla_kv_proj(ckv, w["w_ukv"], kr)
    c5 = gather("ffn2_w_up")
    ob = _mla_fwd(qf, kf, vf, comm=[c5])
    got(c5)
    ya, yb, merged = _mix_merge(oa, ob, w["w_branch_a"], w["w_branch_b"], ga, gb)
    h2 = _mix_out(merged, w["w_out"], h1)
    c6 = gather("ffn2_w_down")
    c7 = gather("w_pl", "w_pl_gate")

    def ffn2_wd():
        got(c6)
        return w["ffn2_w_down"]

    h3, ffn2_saved = _ffn_forward("ffn2", h2, sp["ffn2_norm"], w["ffn2_w_gate"], w["ffn2_w_up"], ffn2_wd,
                                  up_comm=[c6], down_comm=[c7])
    got(c7)
    n4 = _rms_fwd("pl_norm", h3, sp["pl_norm"])
    pb = p.astype(BF)
    h4, t, pe = _pl_forward(n4, w["w_pl_gate"], pb, w["w_pl"], h3)

    dh4, dsp["final_norm"], loss = _loss_head(h4, target, sp["final_norm"])
    dt, dpe = _pl_bwd_elem(dh4, pe, t)
    grad("w_pl", _grp_dw("pl_dw", pb, dpe))
    grad("w_pl_gate", _row_dw("plg_dw", n4, dt))
    s1 = to_sibling("w_pl", "w_pl_gate")
    dn4 = _row_dx("plg_dx", dt, w["w_pl_gate"], comm=[s1])
    add_pairs(s1)
    dh3, dsp["pl_norm"] = _rms_bwd("pl_dnorm", dn4, h3, sp["pl_norm"], dh4)

    xn, hg, hu, a = ffn2_saved
    dhb = dh3.astype(BF)
    k1 = to_chips("w_pl", "w_pl_gate")
    grad("ffn2_w_down", _ffn_bwd_wd("ffn2_dwd", a, dhb, comm=[k1]))
    done(k1)
    s2 = to_sibling("ffn2_w_down")
    dhg, dhu = _ffn_bwd_act("ffn2_dact", dhb, w["ffn2_w_down"], hg, hu, comm=[s2])
    add_pairs(s2)
    k2 = to_chips("ffn2_w_down")
    dwg, dwu = _ffn_bwd_wup("ffn2_dwup", xn, dhg, dhu, comm=[k2])
    done(k2)
    grad("ffn2_w_gate", dwg)
    grad("ffn2_w_up", dwu)
    s3 = to_sibling("ffn2_w_gate", "ffn2_w_up")
    dxn = _ffn_bwd_x("ffn2_dx", dhg, dhu, w["ffn2_w_gate"], w["ffn2_w_up"], comm=[s3])
    add_pairs(s3)
    dh2, dsp["ffn2_norm"] = _rms_bwd("ffn2_dnorm", dxn, h2, sp["ffn2_norm"], dh3)

    dh2b = dh2.astype(BF)
    grad("w_out", _row_dw("out_dw", merged, dh2b))
    s4 = to_sibling("w_out")
    dya, dyb, dga, dgb = _mix_out_bwd(dh2b, w["w_out"], ga, gb, ya, yb, comm=[s4])
    add_pairs(s4)
    grad("w_branch_a", _grp_dw("bra_dw", oa, dya))
    grad("w_branch_b", _grp_dw("brb_dw", ob, dyb))
    doa = _grp_dx("bra_dx", dya, w["w_branch_a"]).astype(BF)
    s5 = to_sibling("w_branch_a", "w_branch_b")
    dob = _grp_dx("brb_dx", dyb, w["w_branch_b"], comm=[s5]).astype(BF)
    add_pairs(s5)

    k3 = to_chips("ffn2_w_gate", "w_out")
    dqf, dkf, dvf = _mla_bwd(qf, kf, vf, dob, comm=[k3])
    done(k3)
    dqp, dkv, dkr = _mla_post(dqf, dkf, dvf, cos, sin, rot_t)
    grad("w_uq", _grp_dw_t("uq_dw", dqp, cq))
    grad("w_ukv", _grp_dw("ukv_dw", ckv, dkv))
    dcq = _grp_dx_t("uq_dx", dqp, w["w_uq"])
    s6 = to_sibling("w_uq", "w_ukv")
    dckv = _grp_dx("ukv_dx", dkv, w["w_ukv"], comm=[s6])
    add_pairs(s6)
    dlat, dsp["q_a_norm"], dsp["kv_a_norm"] = _mla_lat_bwd(dcq, dckv, dkr, lat, sp["q_a_norm"], sp["kv_a_norm"],
                                                         cos, sin, rot_t)
    k4 = to_chips("ffn2_w_up", "w_branch_a", "w_branch_b")
    dq_na, dk_na, dv_na, dtab = _na_bwd(qkv, tb, doa, comm=[k4])
    done(k4)
    dsp["na_rpb"] = _na_rpb_grad(dtab, rows)
    dqkv = jnp.concatenate([dq_na, dk_na.astype(BF), dv_na.astype(BF)], axis=1)

    pieces = [dqkv, dlat, dga, dgb]
    dwin = jnp.concatenate([_mm_tn("in_dw%d" % i, pc, u, BF) for i, pc in enumerate(pieces)], axis=0)
    grad("w_in", dwin.reshape(NDEV, -1, d))
    s7 = to_sibling("w_in")
    k5 = to_chips("w_uq", "w_ukv")
    du = _in_proj_bwd_x(pieces, [wqkv, wlat, wga, wgb], comm=[s7, k5])
    add_pairs(s7)
    done(k5)
    dh1, dsp["mix_norm"] = _rms_bwd("mix_dnorm", du, h1, sp["mix_norm"], dh2)

    xn, hg, hu, a = ffn1_saved
    dhb = dh1.astype(BF)
    k6 = to_chips("w_in")
    grad("ffn1_w_down", _ffn_bwd_wd("ffn1_dwd", a, dhb, comm=[k6]))
    done(k6)
    s8 = to_sibling("ffn1_w_down")
    dhg, dhu = _ffn_bwd_act("ffn1_dact", dhb, w["ffn1_w_down"], hg, hu, comm=[s8])
    add_pairs(s8)
    k7 = to_chips("ffn1_w_down")
    dwg, dwu = _ffn_bwd_wup("ffn1_dwup", xn, dhg, dhu, comm=[k7])
    done(k7)
    grad("ffn1_w_gate", dwg)
    grad("ffn1_w_up", dwu)
    s9 = to_sibling("ffn1_w_gate", "ffn1_w_up")
    _comm_only("rs_sibling_ffn1", [s9])
    add_pairs(s9)
    k8 = to_chips("ffn1_w_gate", "ffn1_w_up")
    dxn = _ffn_bwd_x("ffn1_dx", dhg, dhu, w["ffn1_w_gate"], w["ffn1_w_up"], comm=[k8])
    done(k8)
    dx, dsp["ffn1_norm"] = _rms_bwd("ffn1_dnorm", dxn, x, sp["ffn1_norm"], dh1)
    return loss, dx, chip_parts, dsp


def _gather_small(buf):
    def body(in_ref, out_ref, send_sems, recv_sems, local_sem):
        x, y, c = _coords()
        mine = pltpu.make_async_copy(in_ref, out_ref.at[4 * x + 2 * y + c], local_sem)
        mine.start()
        cps = []
        for k in range(1, NDEV):
            fx, fy, fc = (k >> 2) & 1, (k >> 1) & 1, k & 1
            peer = (x ^ fx, y ^ fy, c ^ fc)
            cps.append(pltpu.make_async_remote_copy(
                src_ref=in_ref, dst_ref=out_ref.at[4 * x + 2 * y + c], send_sem=send_sems.at[k - 1],
                recv_sem=recv_sems.at[k - 1], device_id=peer, device_id_type=MESH))
        for cp in cps:
            cp.start()
        for k in range(1, NDEV):
            fx, fy, fc = (k >> 2) & 1, (k >> 1) & 1, k & 1
            px, py, pc = x ^ fx, y ^ fy, c ^ fc
            pltpu.make_async_remote_copy(
                src_ref=in_ref, dst_ref=out_ref.at[4 * px + 2 * py + pc], send_sem=send_sems.at[k - 1],
                recv_sem=recv_sems.at[k - 1], device_id=(px, py, pc), device_id_type=MESH).wait_recv()
        for cp in cps:
            cp.wait_send()
        mine.wait()

    return pl.pallas_call(
        body, name="gather_small", in_specs=[ANY], out_specs=ANY,
        out_shape=jax.ShapeDtypeStruct((NDEV,) + buf.shape, buf.dtype),
        scratch_shapes=[pltpu.SemaphoreType.DMA((NDEV - 1,)), pltpu.SemaphoreType.DMA((NDEV - 1,)),
                        pltpu.SemaphoreType.DMA],
    )(buf)


def _adam_math(wv, g, m, v):
    m_new = B1 * m + (1.0 - B1) * g
    v_new = B2 * v + (1.0 - B2) * (g * g)
    m_hat = m_new / (1.0 - B1 ** STEP)
    v_hat = v_new / (1.0 - B2 ** STEP)
    return -LR * (m_hat / (jnp.sqrt(v_hat) + ADAM_EPS) + WD * wv), m_new, v_new


def _adam(name, parts, wv, m, v):
    npart, r, c = parts.shape
    tr, tc = _ew_tile(r, c)

    def body(p_ref, w_ref, m_ref, v_ref, g_ref, d_ref, mo_ref, vo_ref):
        g = p_ref[0].astype(F32)
        for j in range(1, npart):
            g = g + p_ref[j].astype(F32)
        g_ref[...] = g
        d_ref[...], mo_ref[...], vo_ref[...] = _adam_math(w_ref[...], g, m_ref[...], v_ref[...])

    blk = pl.BlockSpec((tr, tc), lambda i, k: (i, k))
    return pl.pallas_call(
        body, name=name, grid=(r // tr, c // tc),
        in_specs=[pl.BlockSpec((npart, tr, tc), lambda i, k: (0, i, k)), blk, blk, blk],
        out_specs=[blk] * 4, out_shape=[jax.ShapeDtypeStruct((r, c), F32)] * 4, compiler_params=_params(2),
    )(parts, wv, m, v)


SHARDED = ("ffn1_w_gate", "ffn1_w_up", "ffn1_w_down", "w_in", "w_uq", "w_ukv", "w_branch_a", "w_branch_b", "w_out",
           "ffn2_w_gate", "ffn2_w_up", "ffn2_w_down", "w_pl", "w_pl_gate")
TRANSPOSED = ("ffn1_w_gate", "ffn1_w_up", "ffn2_w_gate", "ffn2_w_up", "w_in", "w_uq")
REPLICATED = ("ffn1_norm", "mix_norm", "q_a_norm", "kv_a_norm", "na_rpb", "ffn2_norm", "pl_norm", "final_norm")
WEIGHTS = ("ffn1_norm", "ffn1_w_gate", "ffn1_w_up", "ffn1_w_down", "mix_norm", "w_in", "q_a_norm", "w_uq",
           "kv_a_norm", "w_ukv", "na_rpb", "w_branch_a", "w_branch_b", "w_out", "ffn2_norm", "ffn2_w_gate",
           "ffn2_w_up", "ffn2_w_down", "pl_norm", "w_pl", "w_pl_gate", "final_norm")
SMALL_W = 2048


def _pack_small(vals):
    rows = []
    for name in REPLICATED:
        flat = vals[name].reshape(-1).astype(F32)
        n = -(-flat.shape[0] // SMALL_W) * SMALL_W
        rows.append(jnp.pad(flat, (0, n - flat.shape[0])).reshape(-1, SMALL_W))
    return jnp.concatenate(rows, axis=0)


def _unpack_small(buf, shapes):
    out, r = {}, 0
    for name in REPLICATED:
        size = int(np.prod(shapes[name]))
        nrow = -(-size // SMALL_W)
        out[name] = buf[r:r + nrow].reshape(-1)[:size].reshape(shapes[name])
        r += nrow
    return out


def kernel(x, p, ffn1_norm, ffn1_w_gate, ffn1_w_up, ffn1_w_down, mix_norm, w_in, q_a_norm, w_uq, kv_a_norm, w_ukv, na_rpb, w_branch_a, w_branch_b, w_out, ffn2_norm, ffn2_w_gate, ffn2_w_up, ffn2_w_down, pl_norm, w_pl, w_pl_gate, final_norm, loss_target, m_ffn1_norm, m_ffn1_w_gate, m_ffn1_w_up, m_ffn1_w_down, m_mix_norm, m_w_in, m_q_a_norm, m_w_uq, m_kv_a_norm, m_w_ukv, m_na_rpb, m_w_branch_a, m_w_branch_b, m_w_out, m_ffn2_norm, m_ffn2_w_gate, m_ffn2_w_up, m_ffn2_w_down, m_pl_norm, m_w_pl, m_w_pl_gate, m_final_norm, v_ffn1_norm, v_ffn1_w_gate, v_ffn1_w_up, v_ffn1_w_down, v_mix_norm, v_w_in, v_q_a_norm, v_w_uq, v_kv_a_norm, v_w_ukv, v_na_rpb, v_w_branch_a, v_w_branch_b, v_w_out, v_ffn2_norm, v_ffn2_w_gate, v_ffn2_w_up, v_ffn2_w_down, v_pl_norm, v_w_pl, v_w_pl_gate, v_final_norm):
    args = dict(locals())
    wts = {n: args[n] for n in WEIGHTS}
    mom = {n: args["m_" + n] for n in WEIGHTS}
    var = {n: args["v_" + n] for n in WEIGHTS}
    shapes = {n: wts[n].shape for n in WEIGHTS}
    core = lax.axis_index("c").astype(jnp.int32).reshape(1)

    local = lambda n, a: a[0].T if n in TRANSPOSED else a[0]
    own = {n: local(n, wts[n]).astype(BF) for n in SHARDED}
    sp = {n: wts[n].reshape(1, -1) for n in REPLICATED if n != "na_rpb"}
    sp["na_rpb"] = wts["na_rpb"][0]
    loss_part, grad_x, chip_parts, dsp = _device_step(x[0], p[0, 0], loss_target[0], sp, own, core)

    out = {}
    for n in SHARDED:
        res4 = _adam("adam_" + n, chip_parts[n], local(n, wts[n]), local(n, mom[n]), local(n, var[n]))
        out[n] = tuple((a.T if n in TRANSPOSED else a)[None] for a in res4)

    small = jnp.concatenate([_pack_small(dsp), jnp.pad(loss_part, ((0, 0), (0, SMALL_W - loss_part.shape[1])))], 0)
    pad_rows = -small.shape[0] % 8
    small = jnp.pad(small, ((0, pad_rows), (0, 0)))
    every = _gather_small(small)
    zeros = jnp.zeros((1 + pad_rows, SMALL_W), F32)
    pack = lambda d: jnp.concatenate([_pack_small(d), zeros], 0)
    g_s, d_s, m_s, v_s = _adam("adam_small", every, pack(wts), pack(mom), pack(var))
    n_rows = small.shape[0] - 1 - pad_rows
    loss = g_s[n_rows, 0]
    small_out = [_unpack_small(b, shapes) for b in (g_s, d_s, m_s, v_s)]
    for n in REPLICATED:
        out[n] = tuple(b[n] for b in small_out)

    res = [loss, grad_x[None]]
    for k in range(4):
        res += [out[n][k] for n in WEIGHTS]
    return tuple(res)
```

```python
import functools

import numpy as np
import jax
import jax.numpy as jnp
from jax import lax
from jax.experimental import pallas as pl
from jax.experimental.pallas import tpu as pltpu

F32 = jnp.float32
BF = jnp.bfloat16
MESH = pl.DeviceIdType.MESH

NDEV = 8
NCHIP = 4
VMEM_LIMIT = 56 * 1024 * 1024
EPS = 1e-6
NEG = -1e30
GRID_W = 64
NA_HEADS, NA_DIM = 8, 128
NA_ROWS_WIN, NA_COLS_WIN = 8, 16
NA_HG = 4
ML_HEADS, ML_NOPE, ML_ROPE, ML_V = 8, 128, 64, 128
ML_QK = ML_NOPE + ML_ROPE
ML_RANK = 512
ROPE_THETA = 10000.0
LR, B1, B2, ADAM_EPS, WD, STEP = 0.001, 0.9, 0.999, 1e-08, 0.01, 10
HI = lax.Precision.HIGHEST

_DN = {"nn": (((1,), (0,)), ((), ())), "nt": (((1,), (1,)), ((), ())), "tn": (((0,), (0,)), ((), ()))}


def _params(n):
    return pltpu.CompilerParams(dimension_semantics=("arbitrary",) * n, vmem_limit_bytes=VMEM_LIMIT)


def _sig(v):
    return jax.nn.sigmoid(v)


ANY = pl.BlockSpec(memory_space=pl.ANY)


def _coords():
    return lax.axis_index("x"), lax.axis_index("y"), lax.axis_index("c")


class _Part:
    inputs, out_shapes, sem_shapes, results = (), (), (), None


class _GatherPart(_Part):
    def __init__(self, names, shards):
        n = len(shards)
        self.names, self.inputs = list(names), list(shards)
        self.out_shapes = [jax.ShapeDtypeStruct((NDEV,) + a.shape, a.dtype) for a in shards]
        self.sem_shapes = [pltpu.SemaphoreType.DMA((n, 7)), pltpu.SemaphoreType.DMA((n, 7)),
                           pltpu.SemaphoreType.DMA((n,))]

    def _plan(self, ins, outs, sems):
        send_sems, recv_sems, local_sems = sems
        x, y, c = _coords()
        chips = [(1 - x, y), (x, 1 - y), (1 - x, 1 - y)]

        def copy(i, k, block, to, src=None):
            px, py, pc = block
            dst = outs[i].at[4 * px + 2 * py + pc]
            return pltpu.make_async_remote_copy(
                src_ref=dst if src is None else src, dst_ref=dst, send_sem=send_sems.at[i, k],
                recv_sem=recv_sems.at[i, k], device_id=to, device_id_type=MESH)

        n = len(ins)
        mine = [pltpu.make_async_copy(ins[i], outs[i].at[4 * x + 2 * y + c], local_sems.at[i]) for i in range(n)]
        first = []
        for i in range(n):
            first.append(copy(i, 0, (x, y, c), (x, y, 1 - c), src=ins[i]))
            first += [copy(i, 1 + j, (x, y, c), (*chip, c), src=ins[i]) for j, chip in enumerate(chips)]
        return copy, mine, first, chips, (x, y, c)

    def start(self, ins, outs, sems):
        _, mine, first, _, _ = self._plan(ins, outs, sems)
        for cp in mine + first:
            cp.start()

    def finish(self, ins, outs, sems):
        copy, mine, first, chips, (x, y, c) = self._plan(ins, outs, sems)
        n = len(ins)
        passed = []
        for i in range(n):
            for j, chip in enumerate(chips):
                copy(i, 1 + j, (*chip, c), (x, y, c)).wait_recv()
                fw = copy(i, 4 + j, (*chip, c), (x, y, 1 - c))
                fw.start()
                passed.append(fw)
        for i in range(n):
            copy(i, 0, (x, y, 1 - c), (x, y, c)).wait_recv()
            for j, chip in enumerate(chips):
                copy(i, 4 + j, (*chip, 1 - c), (x, y, c)).wait_recv()
        for cp in first + passed:
            cp.wait_send()
        for cp in mine:
            cp.wait()


class _SiblingPart(_Part):
    def __init__(self, names, parts):
        n = len(parts)
        self.names, self.inputs = list(names), list(parts)
        self.out_shapes = [jax.ShapeDtypeStruct((NCHIP,) + a.shape[2:], a.dtype) for a in parts]
        self.sem_shapes = [pltpu.SemaphoreType.DMA((n,)), pltpu.SemaphoreType.DMA((n,))]

    def _copies(self, ins, outs, sems):
        x, y, c = _coords()
        return [pltpu.make_async_remote_copy(
            src_ref=ins[i].at[:, 1 - c], dst_ref=outs[i], send_sem=sems[0].at[i], recv_sem=sems[1].at[i],
            device_id=(x, y, 1 - c), device_id_type=MESH) for i in range(len(ins))]

    def start(self, ins, outs, sems):
        for cp in self._copies(ins, outs, sems):
            cp.start()

    def finish(self, ins, outs, sems):
        cps = self._copies(ins, outs, sems)
        for cp in cps:
            cp.wait_recv()
        for cp in cps:
            cp.wait_send()


class _ChipsPart(_Part):
    def __init__(self, names, sums):
        n = len(sums)
        self.names, self.inputs = list(names), list(sums)
        self.out_shapes = [jax.ShapeDtypeStruct(a.shape, a.dtype) for a in sums]
        self.sem_shapes = [pltpu.SemaphoreType.DMA((n, 3)), pltpu.SemaphoreType.DMA((n, 3)),
                           pltpu.SemaphoreType.DMA((n,))]

    def _plan(self, ins, outs, sems):
        send_sems, recv_sems, local_sems = sems
        x, y, c = _coords()
        my_chip = 2 * x + y
        chips = [(1 - x, y), (x, 1 - y), (1 - x, 1 - y)]
        n = len(ins)
        mine = [pltpu.make_async_copy(ins[i].at[my_chip], outs[i].at[my_chip], local_sems.at[i]) for i in range(n)]
        sends, recvs = [], []
        for i in range(n):
            for k, (px, py) in enumerate(chips):
                sends.append(pltpu.make_async_remote_copy(
                    src_ref=ins[i].at[2 * px + py], dst_ref=outs[i].at[my_chip], send_sem=send_sems.at[i, k],
                    recv_sem=recv_sems.at[i, k], device_id=(px, py, c), device_id_type=MESH))
                recvs.append(pltpu.make_async_remote_copy(
                    src_ref=ins[i].at[my_chip], dst_ref=outs[i].at[2 * px + py], send_sem=send_sems.at[i, k],
                    recv_sem=recv_sems.at[i, k], device_id=(px, py, c), device_id_type=MESH))
        return mine, sends, recvs

    def start(self, ins, outs, sems):
        mine, sends, _ = self._plan(ins, outs, sems)
        for cp in mine + sends:
            cp.start()

    def finish(self, ins, outs, sems):
        mine, sends, recvs = self._plan(ins, outs, sems)
        for cp in recvs:
            cp.wait_recv()
        for cp in sends:
            cp.wait_send()
        for cp in mine:
            cp.wait()


def _call(name, body, grid, in_specs, out_specs, out_shape, args, comm=()):
    comm = [p for p in comm if p is not None]
    single = not isinstance(out_shape, (list, tuple))
    o_specs = [out_specs] if single else list(out_specs)
    o_shape = [out_shape] if single else list(out_shape)
    n_in, n_out = len(in_specs), len(o_specs)
    c_in = [a for p in comm for a in p.inputs]
    c_out = [s for p in comm for s in p.out_shapes]
    c_sem = [s for p in comm for s in p.sem_shapes]

    def wrapped(*refs):
        ins, outs = refs[:n_in], refs[n_in + len(c_in):n_in + len(c_in) + n_out]
        pos = [n_in, n_in + len(c_in) + n_out, n_in + len(c_in) + n_out + len(c_out)]
        split = []
        for p in comm:
            sizes = [len(p.inputs), len(p.out_shapes), len(p.sem_shapes)]
            split.append([refs[o:o + n] for o, n in zip(pos, sizes)])
            pos = [o + n for o, n in zip(pos, sizes)]
        ids = [pl.program_id(a) for a in range(len(grid))]

        def run(which, when):
            def go():
                for p, cut in zip(comm, split):
                    getattr(p, which)(*cut)
            if not comm:
                return
            if ids:
                pl.when(functools.reduce(jnp.logical_and, when))(go)
            else:
                go()

        run("start", [i == 0 for i in ids])
        body(*ins, *outs)
        run("finish", [i == g - 1 for i, g in zip(ids, grid)])

    res = pl.pallas_call(
        wrapped, name=name, grid=grid, in_specs=list(in_specs) + [ANY] * len(c_in),
        out_specs=o_specs + [ANY] * len(c_out), out_shape=o_shape + c_out, scratch_shapes=c_sem,
        compiler_params=_params(len(grid)),
    )(*args, *c_in)
    pos = n_out
    for p in comm:
        p.results = list(res[pos:pos + len(p.out_shapes)])
        pos += len(p.out_shapes)
    return res[0] if single else list(res[:n_out])


def _comm_only(name, comm):
    def body(o_ref):
        o_ref[...] = jnp.zeros_like(o_ref)

    _call(name, body, (), [], pl.BlockSpec(memory_space=pltpu.VMEM), jax.ShapeDtypeStruct((8, 128), F32), [], comm)


def _mm(name, grid, prods, extras, outs, epi, nacc=1, comm=()):
    n_p, n_e = len(prods), len(extras)

    def body(*refs):
        ab, ex, out = refs[:2 * n_p], refs[2 * n_p:2 * n_p + n_e], refs[2 * n_p + n_e:]
        accs = [None] * nacc
        for i, prod in enumerate(prods):
            dn, acc, loop = prod[6], prod[7], prod[8]
            a_ref, b_ref = ab[2 * i], ab[2 * i + 1]
            if loop:
                for g in range(loop):
                    t = lax.dot_general(a_ref[g], b_ref[g], _DN[dn], preferred_element_type=F32)
                    accs[acc] = t if accs[acc] is None else accs[acc] + t
            else:
                t = lax.dot_general(a_ref[...], b_ref[...], _DN[dn], preferred_element_type=F32)
                accs[acc] = t if accs[acc] is None else accs[acc] + t
        epi(accs, ex, out)

    in_specs, args = [], []
    for prod in prods:
        in_specs += [pl.BlockSpec(prod[1], prod[2]), pl.BlockSpec(prod[4], prod[5])]
        args += [prod[0], prod[3]]
    for e, e_blk, e_map in extras:
        in_specs.append(pl.BlockSpec(e_blk, e_map))
        args.append(e)
    return _call(name, body, grid, in_specs, [pl.BlockSpec(blk, mp) for _, _, blk, mp in outs],
                 [jax.ShapeDtypeStruct(s, d) for s, d, _, _ in outs], args, comm)


def _store(accs, ex, out):
    out[0][...] = accs[0].astype(out[0].dtype)


def _ew_tile(r, c, budget=3 << 19):
    for t in range(r - r % 16, 0, -16):
        if r % t == 0 and t * c * 4 <= budget:
            return t, c
    for t in range(c - c % 128, 0, -128):
        if c % t == 0 and r * t * 4 <= budget:
            return r, t
    return r, c


def _tile(n, want):
    t = min(n, want)
    assert n % t == 0, (n, want)
    return t


def _mm_nn(name, a, b, out_dtype, tm=512, tn=512, comm=()):
    m, k = a.shape
    n = b.shape[1]
    tm, tn = _tile(m, tm), (tn if n % tn == 0 else n)
    return _mm(name, (n // tn, m // tm),
               [(a, (tm, k), lambda j, i: (i, 0), b, (k, tn), lambda j, i: (0, j), "nn", 0, 0)], [],
               [((m, n), out_dtype, (tm, tn), lambda j, i: (i, j))], _store, comm=comm)[0]


def _mm_nt(name, a, bt, out_dtype, tm=512, tn=512, comm=()):
    m, k = a.shape
    n = bt.shape[0]
    tm, tn = _tile(m, tm), (tn if n % tn == 0 else n)
    return _mm(name, (n // tn, m // tm),
               [(a, (tm, k), lambda j, i: (i, 0), bt, (tn, k), lambda j, i: (j, 0), "nt", 0, 0)], [],
               [((m, n), out_dtype, (tm, tn), lambda j, i: (i, j))], _store, comm=comm)[0]


def _concat_rows(name, pieces):
    n = len(pieces)
    c = pieces[0].shape[1]
    offs = [sum(p.shape[0] for p in pieces[:i]) for i in range(n + 1)]

    def body(*refs):
        out, sems = refs[n], refs[n + 1]
        cps = [pltpu.make_async_copy(refs[i], out.at[pl.ds(offs[i], pieces[i].shape[0])], sems.at[i]) for i in range(n)]
        for cp in cps:
            cp.start()
        for cp in cps:
            cp.wait()

    return pl.pallas_call(
        body, name=name, in_specs=[ANY] * n, out_specs=ANY,
        out_shape=jax.ShapeDtypeStruct((offs[n], c), pieces[0].dtype),
        scratch_shapes=[pltpu.SemaphoreType.DMA((n,))])(*pieces)


def _mm_tn(name, a, b, out_dtype, ta=512, tb=512, scale=None):
    t, ka = a.shape
    nb = b.shape[1]
    ta, tb = (ta if ka % ta == 0 else ka), (tb if nb % tb == 0 else nb)

    def epi(accs, ex, out):
        v = accs[0] if scale is None else accs[0] * scale
        out[0][...] = v.astype(out[0].dtype)

    return _mm(name, (ka // ta, nb // tb),
               [(a, (t, ta), lambda i, j: (0, i), b, (t, tb), lambda i, j: (0, j), "tn", 0, 0)], [],
               [((ka, nb), out_dtype, (ta, tb), lambda i, j: (i, j))], epi)[0]


def _rms_fwd(name, x, g, tm=256):
    s, d = x.shape
    tm = _tile(s, tm)

    def body(x_ref, g_ref, o_ref):
        v = x_ref[...]
        o_ref[...] = (v * lax.rsqrt(jnp.mean(v * v, axis=-1, keepdims=True) + EPS) * g_ref[...]).astype(o_ref.dtype)

    return pl.pallas_call(
        body, name=name, grid=(s // tm,),
        in_specs=[pl.BlockSpec((tm, d), lambda i: (i, 0)), pl.BlockSpec((1, d), lambda i: (0, 0))],
        out_specs=pl.BlockSpec((tm, d), lambda i: (i, 0)), out_shape=jax.ShapeDtypeStruct((s, d), BF),
        compiler_params=_params(1))(x, g)


def _acc_rows(ref, part, i):
    @pl.when(i == 0)
    def _():
        ref[...] = part

    @pl.when(i > 0)
    def _():
        ref[...] += part


def _rms_bwd_math(dn, v, g):
    rstd = lax.rsqrt(jnp.mean(v * v, axis=-1, keepdims=True) + EPS)
    xh = v * rstd
    dxh = dn * g
    dx = rstd * (dxh - xh * jnp.mean(dxh * xh, axis=-1, keepdims=True))
    return dx, jnp.sum(dn * xh, axis=0, keepdims=True)


def _rms_bwd(name, dn, x, g, resid, tm=256):
    s, d = x.shape
    tm = _tile(s, tm)

    def body(dn_ref, x_ref, g_ref, r_ref, dx_ref, dg_ref):
        dx, part = _rms_bwd_math(dn_ref[...].astype(F32), x_ref[...], g_ref[...])
        dx_ref[...] = r_ref[...] + dx
        _acc_rows(dg_ref, part, pl.program_id(0))

    row = pl.BlockSpec((tm, d), lambda i: (i, 0))
    one = pl.BlockSpec((1, d), lambda i: (0, 0))
    return pl.pallas_call(
        body, name=name, grid=(s // tm,), in_specs=[row, row, one, row], out_specs=[row, one],
        out_shape=[jax.ShapeDtypeStruct((s, d), F32), jax.ShapeDtypeStruct((1, d), F32)],
        compiler_params=_params(1))(dn, x, g, resid)


def _loss_head(h, target, g, tm=256):
    s, d = h.shape
    tm = _tile(s, tm)

    def body(h_ref, t_ref, g_ref, dh_ref, dg_ref, loss_ref):
        v, gv = h_ref[...], g_ref[...]
        rstd = lax.rsqrt(jnp.mean(v * v, axis=-1, keepdims=True) + EPS)
        xh = v * rstd
        err = xh * gv - t_ref[...]
        part_loss = 0.5 * jnp.sum(jnp.mean(err * err, axis=-1, keepdims=True), axis=0, keepdims=True)
        dy = err * (1.0 / d)
        dxh = dy * gv
        dh_ref[...] = rstd * (dxh - xh * jnp.mean(dxh * xh, axis=-1, keepdims=True))
        i = pl.program_id(0)
        _acc_rows(dg_ref, jnp.sum(dy * xh, axis=0, keepdims=True), i)
        _acc_rows(loss_ref, jnp.broadcast_to(part_loss, loss_ref.shape), i)

    row = pl.BlockSpec((tm, d), lambda i: (i, 0))
    one = pl.BlockSpec((1, d), lambda i: (0, 0))
    return pl.pallas_call(
        body, name="loss_head", grid=(s // tm,), in_specs=[row, row, one],
        out_specs=[row, one, pl.BlockSpec((1, 128), lambda i: (0, 0))],
        out_shape=[jax.ShapeDtypeStruct((s, d), F32), jax.ShapeDtypeStruct((1, d), F32),
                   jax.ShapeDtypeStruct((1, 128), F32)],
        compiler_params=_params(1))(h, target, g)


def _pl_bwd_elem(dh, pe, t, tm=256):
    s, d = dh.shape
    tm = _tile(s, tm)

    def body(dh_ref, pe_ref, t_ref, dt_ref, dpe_ref):
        dh_v, sg = dh_ref[...], _sig(t_ref[...])
        dt_ref[...] = (dh_v * pe_ref[...].astype(F32) * sg * (1.0 - sg)).astype(BF)
        dpe_ref[...] = (dh_v * sg).astype(BF)

    row = pl.BlockSpec((tm, d), lambda i: (i, 0))
    return pl.pallas_call(
        body, name="pl_bwd_elem", grid=(s // tm,), in_specs=[row, row, row], out_specs=[row, row],
        out_shape=[jax.ShapeDtypeStruct((s, d), BF)] * 2, compiler_params=_params(1))(dh, pe, t)


def _ffn_up(name, xn, wg, wu, tm=512, comm=()):
    s, d = xn.shape
    g, fb, _ = wg.shape
    tm = _tile(s, tm)

    def epi(accs, ex, out):
        hg, hu = accs
        out[0][...] = hg.astype(BF)
        out[1][...] = hu.astype(BF)
        out[2][...] = (hg * _sig(hg) * hu).astype(BF)

    a_map = lambda j, i: (i, 0)
    w_map = lambda j, i: (j, 0, 0)
    o = ((g, s, fb), BF, (None, tm, fb), lambda j, i: (j, i, 0))
    return _mm(name, (g, s // tm),
               [(xn, (tm, d), a_map, wg, (None, fb, d), w_map, "nt", 0, 0),
                (xn, (tm, d), a_map, wu, (None, fb, d), w_map, "nt", 1, 0)], [], [o, o, o], epi, nacc=2, comm=comm)


def _ffn_down(name, a, wd, resid, tm=512, tn=512, comm=()):
    g, s, fb = a.shape
    d = wd.shape[2]
    tm, tn = _tile(s, tm), _tile(d, tn)

    def epi(accs, ex, out):
        out[0][...] = ex[0][...] + 0.5 * accs[0]

    return _mm(name, (d // tn, s // tm),
               [(a, (g, tm, fb), lambda j, i: (0, i, 0), wd, (g, fb, tn), lambda j, i: (0, 0, j), "nn", 0, g)],
               [(resid, (tm, tn), lambda j, i: (i, j))],
               [((s, d), F32, (tm, tn), lambda j, i: (i, j))], epi, comm=comm)[0]


def _ffn_bwd_act(name, dh, wd, hg, hu, tm=512, comm=()):
    s, d = dh.shape
    g, fb, _ = wd.shape
    tm = _tile(s, tm)

    def epi(accs, ex, out):
        da = 0.5 * accs[0]
        hg_v, hu_v = ex[0][...].astype(F32), ex[1][...].astype(F32)
        sg = _sig(hg_v)
        out[0][...] = (da * hu_v * (sg * (1.0 + hg_v * (1.0 - sg)))).astype(BF)
        out[1][...] = (da * (hg_v * sg)).astype(BF)

    blk = (None, tm, fb)
    gmap = lambda j, i: (j, i, 0)
    return _mm(name, (g, s // tm),
               [(dh, (tm, d), lambda j, i: (i, 0), wd, (None, fb, d), lambda j, i: (j, 0, 0), "nt", 0, 0)],
               [(hg, blk, gmap), (hu, blk, gmap)],
               [((g, s, fb), BF, blk, gmap), ((g, s, fb), BF, blk, gmap)], epi, comm=comm)


def _ffn_bwd_wd(name, a, dh, tn=512, comm=()):
    g, s, fb = a.shape
    d = dh.shape[1]
    tn = _tile(d, tn)

    def epi(accs, ex, out):
        out[0][...] = (0.5 * accs[0]).astype(BF)

    return _mm(name, (g, d // tn),
               [(a, (None, s, fb), lambda j, i: (j, 0, 0), dh, (s, tn), lambda j, i: (0, i), "tn", 0, 0)], [],
               [((g, fb, d), BF, (None, fb, tn), lambda j, i: (j, 0, i))], epi, comm=comm)[0]


def _ffn_bwd_wup(name, xn, dhg, dhu, tk=512, comm=()):
    s, d = xn.shape
    g, _, fb = dhg.shape
    tk = _tile(d, tk)

    def epi(accs, ex, out):
        out[0][...] = accs[0].astype(BF)
        out[1][...] = accs[1].astype(BF)

    a_map = lambda j, i: (j, 0, 0)
    b_map = lambda j, i: (0, i)
    o = ((g, fb, d), BF, (None, fb, tk), lambda j, i: (j, 0, i))
    return _mm(name, (g, d // tk),
               [(dhg, (None, s, fb), a_map, xn, (s, tk), b_map, "tn", 0, 0),
                (dhu, (None, s, fb), a_map, xn, (s, tk), b_map, "tn", 1, 0)], [], [o, o], epi, nacc=2, comm=comm)


def _ffn_bwd_x(name, dhg, dhu, wg, wu, tm=512, tn=512, comm=()):
    g, s, fb = dhg.shape
    d = wg.shape[2]
    tm, tn = _tile(s, tm), _tile(d, tn)
    a_blk, a_map = (g, tm, fb), lambda j, i: (0, i, 0)
    b_blk, b_map = (g, fb, tn), lambda j, i: (0, 0, j)
    return _mm(name, (d // tn, s // tm),
               [(dhg, a_blk, a_map, wg, b_blk, b_map, "nn", 0, g), (dhu, a_blk, a_map, wu, b_blk, b_map, "nn", 0, g)],
               [], [((s, d), F32, (tm, tn), lambda j, i: (i, j))], _store, comm=comm)[0]


def _ffn_forward(tag, h, gain, wg, wu, get_wd, up_comm=(), down_comm=()):
    xn = _rms_fwd(tag + "_norm", h, gain)
    hg, hu, a = _ffn_up(tag + "_up", xn, wg, wu, comm=up_comm)
    return _ffn_down(tag + "_down", a, get_wd(), h, comm=down_comm), (xn, hg, hu, a)


def _na_geometry(rows):
    kh = min(NA_ROWS_WIN, rows)
    cols = np.arange(GRID_W)
    col_start = np.clip(cols - NA_COLS_WIN // 2, 0, GRID_W - NA_COLS_WIN)
    mask = (cols[None, :] >= col_start[:, None]) & (cols[None, :] < col_start[:, None] + NA_COLS_WIN)
    dc = np.clip(cols[None, :] - cols[:, None], -(NA_COLS_WIN - 1), NA_COLS_WIN - 1) + (NA_COLS_WIN - 1)
    return kh, mask, dc


def _na_bias_tables(rpb, rows):
    kh, mask, dc = _na_geometry(rows)
    t = jnp.where(jnp.asarray(mask)[None, None], rpb[:, :, dc], NEG)
    tb = jnp.stack([t[:, d0:d0 + kh] for d0 in range(NA_ROWS_WIN)], 0)
    return tb.transpose(0, 1, 3, 2, 4).reshape(NA_ROWS_WIN, NA_HEADS, GRID_W, kh * GRID_W)


def _na_row_start(r, rows, kh):
    return jnp.clip(r - kh // 2, 0, rows - kh)


def _na_specs(s, rows, kh):
    hw = NA_HG * NA_DIM
    nq = NA_HEADS // NA_HG
    q_spec = pl.BlockSpec((GRID_W, hw), lambda j, r: (r, j))
    k_spec = pl.BlockSpec((s, hw), lambda j, r: (0, nq + j))
    v_spec = pl.BlockSpec((s, hw), lambda j, r: (0, 2 * nq + j))
    b_spec = pl.BlockSpec((None, NA_HG, GRID_W, kh * GRID_W),
                          lambda j, r: (_na_row_start(r, rows, kh) - r + NA_ROWS_WIN - 1, j, 0, 0))
    return q_spec, k_spec, v_spec, b_spec, hw, nq


def _na_probs(q, k, bias):
    sc = lax.dot_general(q, k, _DN["nt"], preferred_element_type=F32) * (NA_DIM ** -0.5) + bias
    e = jnp.exp(sc - jnp.max(sc, axis=-1, keepdims=True))
    return e / jnp.sum(e, axis=-1, keepdims=True)


def _na_fwd(qkv, tb, comm=()):
    s = qkv.shape[0]
    rows = s // GRID_W
    kh = min(NA_ROWS_WIN, rows)
    q_spec, k_spec, v_spec, b_spec, hw, nq = _na_specs(s, rows, kh)

    def body(q_ref, k_ref, v_ref, b_ref, o_ref):
        r = pl.program_id(1)
        start = pl.multiple_of(_na_row_start(r, rows, kh) * GRID_W, GRID_W)
        for h in range(NA_HG):
            cs = slice(h * NA_DIM, (h + 1) * NA_DIM)
            p = _na_probs(q_ref[:, cs], k_ref[pl.ds(start, kh * GRID_W), cs], b_ref[h])
            o_ref[:, cs] = jnp.dot(p.astype(BF), v_ref[pl.ds(start, kh * GRID_W), cs],
                                   preferred_element_type=F32).astype(BF)

    return _call("na_fwd", body, (nq, rows), [q_spec, k_spec, v_spec, b_spec],
                 pl.BlockSpec((GRID_W, hw), lambda j, r: (r, j)),
                 jax.ShapeDtypeStruct((s, NA_HEADS * NA_DIM), BF), [qkv, qkv, qkv, tb], comm)


def _na_bwd(qkv, tb, do, comm=()):
    s = qkv.shape[0]
    rows = s // GRID_W
    kh = min(NA_ROWS_WIN, rows)
    q_spec, k_spec, v_spec, b_spec, hw, nq = _na_specs(s, rows, kh)
    nd = 2 * NA_ROWS_WIN - 1

    def body(q_ref, k_ref, v_ref, b_ref, do_ref, dq_ref, dk_ref, dv_ref, dt_ref):
        r = pl.program_id(1)

        @pl.when(r == 0)
        def _():
            dk_ref[...] = jnp.zeros_like(dk_ref)
            dv_ref[...] = jnp.zeros_like(dv_ref)
            dt_ref[...] = jnp.zeros_like(dt_ref)

        rs = _na_row_start(r, rows, kh)
        d0 = rs - r + NA_ROWS_WIN - 1
        win = pl.ds(pl.multiple_of(rs * GRID_W, GRID_W), kh * GRID_W)
        for h in range(NA_HG):
            cs = slice(h * NA_DIM, (h + 1) * NA_DIM)
            q, k, v, do_h = q_ref[:, cs], k_ref[win, cs], v_ref[win, cs], do_ref[:, cs]
            p = _na_probs(q, k, b_ref[h])
            dp = lax.dot_general(do_h, v, _DN["nt"], preferred_element_type=F32)
            ds = p * (dp - jnp.sum(p * dp, axis=-1, keepdims=True))
            for i in range(kh):
                dt_ref[h, d0 + i] += ds[:, i * GRID_W:(i + 1) * GRID_W]
            dsb = (ds * (NA_DIM ** -0.5)).astype(BF)
            dq_ref[:, cs] = jnp.dot(dsb, k, preferred_element_type=F32).astype(BF)
            dk_ref[win, cs] += lax.dot_general(dsb, q, _DN["tn"], preferred_element_type=F32)
            dv_ref[win, cs] += lax.dot_general(p.astype(BF), do_h, _DN["tn"], preferred_element_type=F32)

    width = NA_HEADS * NA_DIM
    whole = pl.BlockSpec((s, hw), lambda j, r: (0, j))
    return _call(
        "na_bwd", body, (nq, rows),
        [q_spec, k_spec, v_spec, b_spec, pl.BlockSpec((GRID_W, hw), lambda j, r: (r, j))],
        [pl.BlockSpec((GRID_W, hw), lambda j, r: (r, j)), whole, whole,
         pl.BlockSpec((NA_HG, nd, GRID_W, GRID_W), lambda j, r: (j, 0, 0, 0))],
        [jax.ShapeDtypeStruct((s, width), BF), jax.ShapeDtypeStruct((s, width), F32),
         jax.ShapeDtypeStruct((s, width), F32), jax.ShapeDtypeStruct((NA_HEADS, nd, GRID_W, GRID_W), F32)],
        [qkv, qkv, qkv, tb, do], comm)


def _na_rpb_grad(dt, rows):
    _, mask, dc = _na_geometry(rows)
    nd, nc = 2 * NA_ROWS_WIN - 1, 2 * NA_COLS_WIN - 1
    onehot = np.zeros((GRID_W * GRID_W, 128), np.float32)
    onehot[np.arange(GRID_W * GRID_W), dc.reshape(-1)] = mask.reshape(-1).astype(np.float32)
    flat = dt.reshape(NA_HEADS * nd, GRID_W * GRID_W)

    def body(a_ref, e_ref, o_ref):
        o_ref[...] = jnp.dot(a_ref[...], e_ref[...], precision=HI, preferred_element_type=F32)

    out = pl.pallas_call(body, name="na_rpb_grad", out_shape=jax.ShapeDtypeStruct((NA_HEADS * nd, 128), F32),
                         compiler_params=_params(0))(flat, jnp.asarray(onehot))
    return out[:, :nc].reshape(NA_HEADS, nd, nc)


def _rope_consts(s):
    pos = np.arange(s, dtype=np.float32)
    inv = (1.0 / (ROPE_THETA ** (np.arange(0, ML_ROPE, 2, dtype=np.float32) / ML_ROPE))).astype(np.float32)
    ang = pos[:, None] * inv[None, :]
    cos, sin = np.cos(ang).astype(np.float32), np.sin(ang).astype(np.float32)
    half = ML_ROPE // 2
    rot = np.zeros((ML_ROPE, ML_ROPE), np.float32)
    rot[np.arange(half) + half, np.arange(half)] = -1.0
    rot[np.arange(half), np.arange(half) + half] = 1.0
    return (jnp.asarray(np.concatenate([cos, cos], 1)), jnp.asarray(np.concatenate([sin, sin], 1)),
            jnp.asarray(rot), jnp.asarray(rot.T.copy()))


def _rope(v, cos, sin, rot):
    return v * cos + jnp.dot(v, rot, precision=HI, preferred_element_type=F32) * sin


def _unrope(dv, cos, sin, rot_t):
    return dv * cos + jnp.dot(dv * sin, rot_t, precision=HI, preferred_element_type=F32)


def _rms(v, g):
    return v * lax.rsqrt(jnp.mean(v * v, axis=-1, keepdims=True) + EPS) * g


def _mla_prep(lat, gq, gkv, cos, sin, rot, tm=256):
    s, w = lat.shape
    tm = _tile(s, tm)

    def body(l_ref, gq_ref, gkv_ref, c_ref, s_ref, r_ref, cq_ref, ckv_ref, kr_ref):
        cq_ref[...] = _rms(l_ref[:, :ML_RANK], gq_ref[...]).astype(BF)
        ckv_ref[...] = _rms(l_ref[:, ML_RANK:2 * ML_RANK], gkv_ref[...]).astype(BF)
        kr_ref[...] = _rope(l_ref[:, 2 * ML_RANK:], c_ref[...], s_ref[...], r_ref[...]).astype(BF)

    row = lambda c: pl.BlockSpec((tm, c), lambda i: (i, 0))
    full = lambda a: pl.BlockSpec(a.shape, lambda i: (0, 0))
    return pl.pallas_call(
        body, name="mla_prep", grid=(s // tm,),
        in_specs=[row(w), full(gq), full(gkv), row(ML_ROPE), row(ML_ROPE), full(rot)],
        out_specs=[row(ML_RANK), row(ML_RANK), row(ML_ROPE)],
        out_shape=[jax.ShapeDtypeStruct((s, ML_RANK), BF), jax.ShapeDtypeStruct((s, ML_RANK), BF),
                   jax.ShapeDtypeStruct((s, ML_ROPE), BF)],
        compiler_params=_params(1))(lat, gq, gkv, cos, sin, rot)


def _mla_q_proj(cq, wuq, cos, sin, rot, tm=512, comm=()):
    s, k = cq.shape
    tm = _tile(s, tm)

    def epi(accs, ex, out):
        acc = accs[0]
        out[0][:, :ML_NOPE] = acc[:, :ML_NOPE].astype(BF)
        out[0][:, ML_NOPE:] = _rope(acc[:, ML_NOPE:], ex[0][...], ex[1][...], ex[2][...]).astype(BF)

    rmap = lambda j, i: (i, 0)
    return _mm("mla_q_proj", (ML_HEADS, s // tm),
               [(cq, (tm, k), rmap, wuq, (None, ML_QK, k), lambda j, i: (j, 0, 0), "nt", 0, 0)],
               [(cos, (tm, ML_ROPE), rmap), (sin, (tm, ML_ROPE), rmap), (rot, rot.shape, lambda j, i: (0, 0))],
               [((ML_HEADS, s, ML_QK), BF, (None, tm, ML_QK), lambda j, i: (j, i, 0))], epi, comm=comm)[0]


def _mla_kv_proj(ckv, wukv, kr, tm=512, comm=()):
    s, k = ckv.shape
    tm = _tile(s, tm)

    def epi(accs, ex, out):
        acc = accs[0]
        out[0][:, :ML_NOPE] = acc[:, :ML_NOPE].astype(BF)
        out[0][:, ML_NOPE:] = ex[0][...]
        out[1][...] = acc[:, ML_NOPE:].astype(BF)

    rmap = lambda j, i: (i, 0)
    gmap = lambda j, i: (j, i, 0)
    return _mm("mla_kv_proj", (ML_HEADS, s // tm),
               [(ckv, (tm, k), rmap, wukv, (None, k, ML_NOPE + ML_V), lambda j, i: (j, 0, 0), "nn", 0, 0)],
               [(kr, (tm, ML_ROPE), rmap)],
               [((ML_HEADS, s, ML_QK), BF, (None, tm, ML_QK), gmap), ((ML_HEADS, s, ML_V), BF, (None, tm, ML_V), gmap)],
               epi, comm=comm)


def _mla_probs(q, k):
    sc = lax.dot_general(q, k, _DN["nt"], preferred_element_type=F32) * (ML_QK ** -0.5)
    e = jnp.exp(sc - jnp.max(sc, axis=-1, keepdims=True))
    return e / jnp.sum(e, axis=-1, keepdims=True)


def _mla_fwd(q, k, v, tq=512, comm=()):
    _, s, _ = q.shape
    tq = _tile(s, tq)

    def body(q_ref, k_ref, v_ref, o_ref):
        p = _mla_probs(q_ref[...], k_ref[...])
        o_ref[...] = jnp.dot(p.astype(BF), v_ref[...], preferred_element_type=F32).astype(BF)

    return _call("mla_fwd", body, (ML_HEADS, s // tq),
                 [pl.BlockSpec((None, tq, ML_QK), lambda h, i: (h, i, 0)),
                  pl.BlockSpec((None, s, ML_QK), lambda h, i: (h, 0, 0)),
                  pl.BlockSpec((None, s, ML_V), lambda h, i: (h, 0, 0))],
                 pl.BlockSpec((tq, ML_V), lambda h, i: (i, h)),
                 jax.ShapeDtypeStruct((s, ML_HEADS * ML_V), BF), [q, k, v], comm)


def _mla_bwd(q, k, v, do, tq=256, comm=()):
    _, s, _ = q.shape
    tq = _tile(s, tq)

    def body(q_ref, k_ref, v_ref, do_ref, dq_ref, dk_ref, dv_ref):
        i = pl.program_id(1)
        qv, kv, vv, dov = q_ref[...], k_ref[...], v_ref[...], do_ref[...]
        p = _mla_probs(qv, kv)
        dp = lax.dot_general(dov, vv, _DN["nt"], preferred_element_type=F32)
        ds = (p * (dp - jnp.sum(p * dp, axis=-1, keepdims=True)) * (ML_QK ** -0.5)).astype(BF)
        dq_ref[...] = jnp.dot(ds, kv, preferred_element_type=F32)
        _acc_rows(dk_ref, lax.dot_general(ds, qv, _DN["tn"], preferred_element_type=F32), i)
        _acc_rows(dv_ref, lax.dot_general(p.astype(BF), dov, _DN["tn"], preferred_element_type=F32), i)

    return _call(
        "mla_bwd", body, (ML_HEADS, s // tq),
        [pl.BlockSpec((None, tq, ML_QK), lambda h, i: (h, i, 0)),
         pl.BlockSpec((None, s, ML_QK), lambda h, i: (h, 0, 0)),
         pl.BlockSpec((None, s, ML_V), lambda h, i: (h, 0, 0)),
         pl.BlockSpec((tq, ML_V), lambda h, i: (i, h))],
        [pl.BlockSpec((None, tq, ML_QK), lambda h, i: (h, i, 0)),
         pl.BlockSpec((None, s, ML_QK), lambda h, i: (h, 0, 0)),
         pl.BlockSpec((None, s, ML_V), lambda h, i: (h, 0, 0))],
        [jax.ShapeDtypeStruct((ML_HEADS, s, ML_QK), F32), jax.ShapeDtypeStruct((ML_HEADS, s, ML_QK), F32),
         jax.ShapeDtypeStruct((ML_HEADS, s, ML_V), F32)],
        [q, k, v, do], comm)


def _mla_post(dq, dk, dv, cos, sin, rot_t, tm=256):
    _, s, _ = dq.shape
    tm = _tile(s, tm)

    def body(dq_ref, dk_ref, dv_ref, c_ref, s_ref, r_ref, dqp_ref, dkv_ref, dkr_ref):
        h = pl.program_id(1)
        dqv, dkk = dq_ref[...], dk_ref[...]
        dqp_ref[:, :ML_NOPE] = dqv[:, :ML_NOPE].astype(BF)
        dqp_ref[:, ML_NOPE:] = _unrope(dqv[:, ML_NOPE:], c_ref[...], s_ref[...], r_ref[...]).astype(BF)
        dkv_ref[:, :ML_NOPE] = dkk[:, :ML_NOPE].astype(BF)
        dkv_ref[:, ML_NOPE:] = dv_ref[...].astype(BF)
        _acc_rows(dkr_ref, dkk[:, ML_NOPE:], h)

    gspec = lambda c: pl.BlockSpec((None, tm, c), lambda i, h: (h, i, 0))
    rspec = pl.BlockSpec((tm, ML_ROPE), lambda i, h: (i, 0))
    return pl.pallas_call(
        body, name="mla_post", grid=(s // tm, ML_HEADS),
        in_specs=[gspec(ML_QK), gspec(ML_QK), gspec(ML_V), rspec, rspec,
                  pl.BlockSpec(rot_t.shape, lambda i, h: (0, 0))],
        out_specs=[gspec(ML_QK), gspec(ML_NOPE + ML_V), rspec],
        out_shape=[jax.ShapeDtypeStruct((ML_HEADS, s, ML_QK), BF),
                   jax.ShapeDtypeStruct((ML_HEADS, s, ML_NOPE + ML_V), BF),
                   jax.ShapeDtypeStruct((s, ML_ROPE), F32)],
        compiler_params=_params(2))(dq, dk, dv, cos, sin, rot_t)


def _mla_lat_bwd(dcq, dckv, dkr, lat, gq, gkv, cos, sin, rot_t, tm=256):
    s, w = lat.shape
    tm = _tile(s, tm)

    def body(dcq_ref, dckv_ref, dkr_ref, l_ref, gq_ref, gkv_ref, c_ref, s_ref, r_ref, dl_ref, dgq_ref, dgkv_ref):
        i = pl.program_id(0)
        dql, pq = _rms_bwd_math(dcq_ref[...], l_ref[:, :ML_RANK], gq_ref[...])
        dkl, pkv = _rms_bwd_math(dckv_ref[...], l_ref[:, ML_RANK:2 * ML_RANK], gkv_ref[...])
        dl_ref[:, :ML_RANK] = dql.astype(BF)
        dl_ref[:, ML_RANK:2 * ML_RANK] = dkl.astype(BF)
        dl_ref[:, 2 * ML_RANK:] = _unrope(dkr_ref[...], c_ref[...], s_ref[...], r_ref[...]).astype(BF)
        _acc_rows(dgq_ref, pq, i)
        _acc_rows(dgkv_ref, pkv, i)

    row = lambda c: pl.BlockSpec((tm, c), lambda i: (i, 0))
    full = lambda a: pl.BlockSpec(a.shape, lambda i: (0, 0))
    return pl.pallas_call(
        body, name="mla_lat_bwd", grid=(s // tm,),
        in_specs=[row(ML_RANK), row(ML_RANK), row(ML_ROPE), row(w), full(gq), full(gkv), row(ML_ROPE), row(ML_ROPE),
                  full(rot_t)],
        out_specs=[row(w), full(gq), full(gkv)],
        out_shape=[jax.ShapeDtypeStruct((s, w), BF), jax.ShapeDtypeStruct(gq.shape, F32),
                   jax.ShapeDtypeStruct(gkv.shape, F32)],
        compiler_params=_params(1))(dcq, dckv, dkr, lat, gq, gkv, cos, sin, rot_t)


def _grp_dw(name, a, dout, ta=512):
    s, k = a.shape
    ta = _tile(k, ta)
    if dout.ndim == 3:
        g, _, nb = dout.shape
        b_blk, b_map = (None, s, nb), lambda j, i: (j, 0, 0)
    else:
        g, nb = NDEV, dout.shape[1] // NDEV
        b_blk, b_map = (s, nb), lambda j, i: (0, j)
    return _mm(name, (g, k // ta),
               [(a, (s, ta), lambda j, i: (0, i), dout, b_blk, b_map, "tn", 0, 0)], [],
               [((g, k, nb), BF, (None, ta, nb), lambda j, i: (j, i, 0))], _store)[0]


def _grp_dw_t(name, dout, a, ta=512):
    g, s, nb = dout.shape
    k = a.shape[1]
    ta = _tile(k, ta)
    return _mm(name, (g, k // ta),
               [(dout, (None, s, nb), lambda j, i: (j, 0, 0), a, (s, ta), lambda j, i: (0, i), "tn", 0, 0)], [],
               [((g, nb, k), BF, (None, nb, ta), lambda j, i: (j, 0, i))], _store)[0]


def _grp_dx_t(name, dout, wt, tm=512, tn=512, comm=()):
    g, s, nb = dout.shape
    k = wt.shape[2]
    tm, tn = _tile(s, tm), _tile(k, tn)
    return _mm(name, (k // tn, s // tm),
               [(dout, (g, tm, nb), lambda j, i: (0, i, 0), wt, (g, nb, tn), lambda j, i: (0, 0, j), "nn", 0, g)], [],
               [((s, k), F32, (tm, tn), lambda j, i: (i, j))], _store, comm=comm)[0]


def _grp_dx(name, dout, w, tm=512, tn=512, comm=()):
    g, s, nb = dout.shape
    k = w.shape[1]
    tm, tn = _tile(s, tm), _tile(k, tn)
    return _mm(name, (k // tn, s // tm),
               [(dout, (g, tm, nb), lambda j, i: (0, i, 0), w, (g, tn, nb), lambda j, i: (0, j, 0), "nt", 0, g)], [],
               [((s, k), F32, (tm, tn), lambda j, i: (i, j))], _store, comm=comm)[0]


def _row_dw(name, a, dout, tn=512):
    s, n = dout.shape
    tn = _tile(n, tn)
    if a.ndim == 3:
        kb = a.shape[2]
        a_blk, a_map = (None, s, kb), lambda j, i: (j, 0, 0)
    else:
        kb = a.shape[1] // NDEV
        a_blk, a_map = (s, kb), lambda j, i: (0, j)
    return _mm(name, (NDEV, n // tn),
               [(a, a_blk, a_map, dout, (s, tn), lambda j, i: (0, i), "tn", 0, 0)], [],
               [((NDEV, kb, n), BF, (None, kb, tn), lambda j, i: (j, 0, i))], _store)[0]


def _mix_merge(oa, ob, wa, wb, ga, gb, tm=512, comm=()):
    s, k = oa.shape
    g, _, nb = wa.shape
    tm = _tile(s, tm)

    def epi(accs, ex, out):
        ya, yb = accs
        out[0][...] = ya.astype(BF)
        out[1][...] = yb.astype(BF)
        out[2][...] = (_sig(ex[0][...]) * ya + _sig(ex[1][...]) * yb).astype(BF)

    rmap = lambda j, i: (i, 0)
    wmap = lambda j, i: (j, 0, 0)
    o = ((g, s, nb), BF, (None, tm, nb), lambda j, i: (j, i, 0))
    cmap = lambda j, i: (i, j)
    return _mm("mix_merge", (g, s // tm),
               [(oa, (tm, k), rmap, wa, (None, k, nb), wmap, "nn", 0, 0),
                (ob, (tm, k), rmap, wb, (None, k, nb), wmap, "nn", 1, 0)],
               [(ga, (tm, nb), cmap), (gb, (tm, nb), cmap)], [o, o, o], epi, nacc=2, comm=comm)


def _mix_out(merged, wout, resid, tm=512, tn=512):
    g, s, kb = merged.shape
    d = wout.shape[2]
    tm, tn = _tile(s, tm), _tile(d, tn)

    def epi(accs, ex, out):
        out[0][...] = ex[0][...] + accs[0]

    return _mm("mix_out", (d // tn, s // tm),
               [(merged, (g, tm, kb), lambda j, i: (0, i, 0), wout, (g, kb, tn), lambda j, i: (0, 0, j), "nn", 0, g)],
               [(resid, (tm, tn), lambda j, i: (i, j))],
               [((s, d), F32, (tm, tn), lambda j, i: (i, j))], epi)[0]


def _mix_out_bwd(dh, wout, ga, gb, ya, yb, tm=512, comm=()):
    s, d = dh.shape
    g, kb, _ = wout.shape
    tm = _tile(s, tm)

    def epi(accs, ex, out):
        dm = accs[0]
        sa, sb = _sig(ex[0][...]), _sig(ex[1][...])
        out[0][...] = (dm * sa).astype(BF)
        out[1][...] = (dm * sb).astype(BF)
        out[2][...] = (dm * ex[2][...].astype(F32) * sa * (1.0 - sa)).astype(BF)
        out[3][...] = (dm * ex[3][...].astype(F32) * sb * (1.0 - sb)).astype(BF)

    cmap = lambda j, i: (i, j)
    gmap = lambda j, i: (j, i, 0)
    og = ((g, s, kb), BF, (None, tm, kb), gmap)
    oc = ((s, g * kb), BF, (tm, kb), cmap)
    return _mm("mix_out_bwd", (g, s // tm),
               [(dh, (tm, d), lambda j, i: (i, 0), wout, (None, kb, d), lambda j, i: (j, 0, 0), "nt", 0, 0)],
               [(ga, (tm, kb), cmap), (gb, (tm, kb), cmap), (ya, (None, tm, kb), gmap), (yb, (None, tm, kb), gmap)],
               [og, og, oc, oc], epi, comm=comm)


def _pl_forward(n4, wplg, p, wpl, h3, tm=512):
    s, d = n4.shape
    g, kb, _ = wplg.shape
    kp, nb = wpl.shape[1], wpl.shape[2]
    tm = _tile(s, tm)
    wplg_nat = wplg.reshape(g * kb, d)

    def epi(accs, ex, out):
        t, pe = accs
        out[0][...] = ex[0][...] + _sig(t) * pe
        out[1][...] = t
        out[2][...] = pe.astype(BF)

    rmap = lambda j, i: (i, 0)
    cmap = lambda j, i: (i, j)
    return _mm("pl_forward", (g, s // tm),
               [(n4, (tm, d), rmap, wplg_nat, (g * kb, nb), lambda j, i: (0, j), "nn", 0, 0),
                (p, (tm, kp), rmap, wpl, (None, kp, nb), lambda j, i: (j, 0, 0), "nn", 1, 0)],
               [(h3, (tm, nb), cmap)],
               [((s, d), F32, (tm, nb), cmap), ((s, d), F32, (tm, nb), cmap), ((s, d), BF, (tm, nb), cmap)],
               epi, nacc=2)


def _row_dx(name, dout, w, tm=512, comm=()):
    s, n = dout.shape
    g, kb, _ = w.shape
    tm = _tile(s, tm)
    return _mm(name, (g, s // tm),
               [(dout, (tm, n), lambda j, i: (i, 0), w, (None, kb, n), lambda j, i: (j, 0, 0), "nt", 0, 0)], [],
               [((s, g * kb), F32, (tm, kb), lambda j, i: (i, j))], _store, comm=comm)[0]


def _in_proj_bwd_x(pieces, weights, tm=256, tn=256, comm=()):
    s = pieces[0].shape[0]
    d = weights[0].shape[1]
    tm, tn = _tile(s, tm), _tile(d, tn)
    prods = [(pc, (tm, pc.shape[1]), lambda j, i: (i, 0), w, (w.shape[0], tn), lambda j, i: (0, j), "nn", 0, 0)
             for pc, w in zip(pieces, weights)]
    return _mm("in_proj_dx", (d // tn, s // tm), prods, [],
               [((s, d), F32, (tm, tn), lambda j, i: (i, j))], _store, comm=comm)[0]


def _split_w_in(w_in_t):
    g, nb, d = w_in_t.shape
    nat = w_in_t.reshape(g * nb, d)
    na, lat = 3 * NA_HEADS * NA_DIM, 2 * ML_RANK + ML_ROPE
    return nat[:na], nat[na:na + lat], nat[na + lat:na + lat + d], nat[na + lat + d:]


def _pair_sum(name, part, landed, core):
    _, _, r, c = part.shape
    tr, tc = _ew_tile(r, c)

    def body(core_ref, a_ref, b_ref, o_ref):
        o_ref[...] = (a_ref[...].astype(F32) + b_ref[...].astype(F32)).astype(o_ref.dtype)

    return pl.pallas_call(
        body, name=name,
        grid_spec=pltpu.PrefetchScalarGridSpec(
            num_scalar_prefetch=1, grid=(NCHIP, r // tr, c // tc),
            in_specs=[pl.BlockSpec((None, None, tr, tc), lambda j, i, k, core_ref: (j, core_ref[0], i, k)),
                      pl.BlockSpec((None, tr, tc), lambda j, i, k, core_ref: (j, i, k))],
            out_specs=pl.BlockSpec((None, tr, tc), lambda j, i, k, core_ref: (j, i, k))),
        out_shape=jax.ShapeDtypeStruct(landed.shape, landed.dtype), compiler_params=_params(3),
    )(core, part, landed)


def _device_step(x, p, target, sp, own, core):
    s, d = x.shape
    rows = s // GRID_W
    cos, sin, rot, rot_t = _rope_consts(s)
    w, dw4, sums, chip_parts, dsp = {}, {}, {}, {}, {}

    def gather(*names):
        return _GatherPart(names, [own[n] for n in names])

    def got(part):
        w.update(zip(part.names, part.results))

    def grad(name, g):
        dw4[name] = g.reshape((NCHIP, 2) + g.shape[1:])

    def to_sibling(*names):
        return _SiblingPart(names, [dw4[n] for n in names])

    def add_pairs(part):
        for n, landed in zip(part.names, part.results):
            sums[n] = _pair_sum("pair_sum_" + n, dw4[n], landed, core)

    def to_chips(*names):
        return _ChipsPart(names, [sums[n] for n in names])

    def done(part):
        chip_parts.update(zip(part.names, part.results))

    c0 = gather("ffn1_w_gate", "ffn1_w_up")
    _comm_only("gather_ffn1", [c0])
    got(c0)
    c1 = gather("ffn1_w_down")
    c2 = gather("w_in")

    def ffn1_wd():
        got(c1)
        return w["ffn1_w_down"]

    h1, ffn1_saved = _ffn_forward("ffn1", x, sp["ffn1_norm"], w["ffn1_w_gate"], w["ffn1_w_up"], ffn1_wd,
                                  up_comm=[c1], down_comm=[c2])
    got(c2)
    wqkv, wlat, wga, wgb = _split_w_in(w["w_in"])
    u = _rms_fwd("mix_norm", h1, sp["mix_norm"])
    c3 = gather("w_uq", "w_ukv")
    qkv = _mm_nt("in_qkv", u, wqkv, BF, tn=1024, comm=[c3])
    got(c3)
    lat = _mm_nt("in_lat", u, wlat, F32)
    c3a = gather("w_branch_a")
    ga = _mm_nt("in_ga", u, wga, F32, tn=1024, comm=[c3a])
    got(c3a)
    c3b = gather("w_branch_b")
    gb = _mm_nt("in_gb", u, wgb, F32, tn=1024, comm=[c3b])
    got(c3b)
    tb = _na_bias_tables(sp["na_rpb"], rows)
    c4 = gather("ffn2_w_gate")
    oa = _na_fwd(qkv, tb, comm=[c4])
    got(c4)
    cq, ckv, kr = _mla_prep(lat, sp["q_a_norm"], sp["kv_a_norm"], cos, sin, rot)
    c4a = gather("w_out")
    qf = _mla_q_proj(cq, w["w_uq"], cos, sin, rot, comm=[c4a])
    got(c4a)
    c4b = gather("w_pl_gate")
    kf, vf = _mla_kv_proj(ckv, w["w_ukv"], kr, comm=[c4b])
    got(c4b)
    c5 = gather("ffn2_w_up")
    ob = _mla_fwd(qf, kf, vf, comm=[c5])
    got(c5)
    c5a = gather("w_pl")
    ya, yb, merged = _mix_merge(oa, ob, w["w_branch_a"], w["w_branch_b"], ga, gb, comm=[c5a])
    got(c5a)
    h2 = _mix_out(merged, w["w_out"], h1)
    c6 = gather("ffn2_w_down")

    def ffn2_wd():
        got(c6)
        return w["ffn2_w_down"]

    h3, ffn2_saved = _ffn_forward("ffn2", h2, sp["ffn2_norm"], w["ffn2_w_gate"], w["ffn2_w_up"], ffn2_wd,
                                  up_comm=[c6])
    n4 = _rms_fwd("pl_norm", h3, sp["pl_norm"])
    pb = p.astype(BF)
    h4, t, pe = _pl_forward(n4, w["w_pl_gate"], pb, w["w_pl"], h3)

    dh4, dsp["final_norm"], loss = _loss_head(h4, target, sp["final_norm"])
    dt, dpe = _pl_bwd_elem(dh4, pe, t)
    grad("w_pl", _grp_dw("pl_dw", pb, dpe))
    grad("w_pl_gate", _row_dw("plg_dw", n4, dt))
    s1 = to_sibling("w_pl", "w_pl_gate")
    dn4 = _row_dx("plg_dx", dt, w["w_pl_gate"], comm=[s1])
    add_pairs(s1)
    dh3, dsp["pl_norm"] = _rms_bwd("pl_dnorm", dn4, h3, sp["pl_norm"], dh4)

    xn, hg, hu, a = ffn2_saved
    dhb = dh3.astype(BF)
    k1 = to_chips("w_pl", "w_pl_gate")
    grad("ffn2_w_down", _ffn_bwd_wd("ffn2_dwd", a, dhb, comm=[k1]))
    done(k1)
    s2 = to_sibling("ffn2_w_down")
    dhg, dhu = _ffn_bwd_act("ffn2_dact", dhb, w["ffn2_w_down"], hg, hu, comm=[s2])
    add_pairs(s2)
    k2 = to_chips("ffn2_w_down")
    dwg, dwu = _ffn_bwd_wup("ffn2_dwup", xn, dhg, dhu, comm=[k2])
    done(k2)
    grad("ffn2_w_gate", dwg)
    grad("ffn2_w_up", dwu)
    s3 = to_sibling("ffn2_w_gate", "ffn2_w_up")
    dxn = _ffn_bwd_x("ffn2_dx", dhg, dhu, w["ffn2_w_gate"], w["ffn2_w_up"], comm=[s3])
    add_pairs(s3)
    dh2, dsp["ffn2_norm"] = _rms_bwd("ffn2_dnorm", dxn, h2, sp["ffn2_norm"], dh3)

    dh2b = dh2.astype(BF)
    grad("w_out", _row_dw("out_dw", merged, dh2b))
    s4 = to_sibling("w_out")
    dya, dyb, dga, dgb = _mix_out_bwd(dh2b, w["w_out"], ga, gb, ya, yb, comm=[s4])
    add_pairs(s4)
    grad("w_branch_a", _grp_dw("bra_dw", oa, dya))
    grad("w_branch_b", _grp_dw("brb_dw", ob, dyb))
    doa = _grp_dx("bra_dx", dya, w["w_branch_a"]).astype(BF)
    s5 = to_sibling("w_branch_a", "w_branch_b")
    dob = _grp_dx("brb_dx", dyb, w["w_branch_b"], comm=[s5]).astype(BF)
    add_pairs(s5)

    k3 = to_chips("ffn2_w_gate", "w_out")
    dqf, dkf, dvf = _mla_bwd(qf, kf, vf, dob, comm=[k3])
    done(k3)
    dqp, dkv, dkr = _mla_post(dqf, dkf, dvf, cos, sin, rot_t)
    grad("w_uq", _grp_dw_t("uq_dw", dqp, cq))
    grad("w_ukv", _grp_dw("ukv_dw", ckv, dkv))
    dcq = _grp_dx_t("uq_dx", dqp, w["w_uq"])
    s6 = to_sibling("w_uq", "w_ukv")
    dckv = _grp_dx("ukv_dx", dkv, w["w_ukv"], comm=[s6])
    add_pairs(s6)
    dlat, dsp["q_a_norm"], dsp["kv_a_norm"] = _mla_lat_bwd(dcq, dckv, dkr, lat, sp["q_a_norm"], sp["kv_a_norm"],
                                                         cos, sin, rot_t)
    k4 = to_chips("ffn2_w_up", "w_branch_a", "w_branch_b")
    dq_na, dk_na, dv_na, dtab = _na_bwd(qkv, tb, doa, comm=[k4])
    done(k4)
    dsp["na_rpb"] = _na_rpb_grad(dtab, rows)
    dqkv = jnp.concatenate([dq_na, dk_na.astype(BF), dv_na.astype(BF)], axis=1)

    pieces = [dqkv, dlat, dga, dgb]
    dwin = _concat_rows("in_dw_rows", [_mm_tn("in_dw%d" % i, pc, u, BF) for i, pc in enumerate(pieces)])
    grad("w_in", dwin.reshape(NDEV, -1, d))
    s7 = to_sibling("w_in")
    k5 = to_chips("w_uq", "w_ukv")
    du = _in_proj_bwd_x(pieces, [wqkv, wlat, wga, wgb], comm=[s7, k5])
    add_pairs(s7)
    done(k5)
    dh1, dsp["mix_norm"] = _rms_bwd("mix_dnorm", du, h1, sp["mix_norm"], dh2)

    xn, hg, hu, a = ffn1_saved
    dhb = dh1.astype(BF)
    k6 = to_chips("w_in")
    grad("ffn1_w_down", _ffn_bwd_wd("ffn1_dwd", a, dhb, comm=[k6]))
    done(k6)
    s8 = to_sibling("ffn1_w_down")
    dhg, dhu = _ffn_bwd_act("ffn1_dact", dhb, w["ffn1_w_down"], hg, hu, comm=[s8])
    add_pairs(s8)
    k7 = to_chips("ffn1_w_down")
    dwg, dwu = _ffn_bwd_wup("ffn1_dwup", xn, dhg, dhu, comm=[k7])
    done(k7)
    grad("ffn1_w_gate", dwg)
    grad("ffn1_w_up", dwu)
    s9 = to_sibling("ffn1_w_gate", "ffn1_w_up")
    _comm_only("rs_sibling_ffn1", [s9])
    add_pairs(s9)
    k8 = to_chips("ffn1_w_gate", "ffn1_w_up")
    dxn = _ffn_bwd_x("ffn1_dx", dhg, dhu, w["ffn1_w_gate"], w["ffn1_w_up"], comm=[k8])
    done(k8)
    dx, dsp["ffn1_norm"] = _rms_bwd("ffn1_dnorm", dxn, x, sp["ffn1_norm"], dh1)
    return loss, dx, chip_parts, dsp


def _gather_small(buf):
    def body(in_ref, out_ref, send_sems, recv_sems, local_sem):
        x, y, c = _coords()
        mine = pltpu.make_async_copy(in_ref, out_ref.at[4 * x + 2 * y + c], local_sem)
        mine.start()
        cps = []
        for k in range(1, NDEV):
            fx, fy, fc = (k >> 2) & 1, (k >> 1) & 1, k & 1
            peer = (x ^ fx, y ^ fy, c ^ fc)
            cps.append(pltpu.make_async_remote_copy(
                src_ref=in_ref, dst_ref=out_ref.at[4 * x + 2 * y + c], send_sem=send_sems.at[k - 1],
                recv_sem=recv_sems.at[k - 1], device_id=peer, device_id_type=MESH))
        for cp in cps:
            cp.start()
        for k in range(1, NDEV):
            fx, fy, fc = (k >> 2) & 1, (k >> 1) & 1, k & 1
            px, py, pc = x ^ fx, y ^ fy, c ^ fc
            pltpu.make_async_remote_copy(
                src_ref=in_ref, dst_ref=out_ref.at[4 * px + 2 * py + pc], send_sem=send_sems.at[k - 1],
                recv_sem=recv_sems.at[k - 1], device_id=(px, py, pc), device_id_type=MESH).wait_recv()
        for cp in cps:
            cp.wait_send()
        mine.wait()

    return pl.pallas_call(
        body, name="gather_small", in_specs=[ANY], out_specs=ANY,
        out_shape=jax.ShapeDtypeStruct((NDEV,) + buf.shape, buf.dtype),
        scratch_shapes=[pltpu.SemaphoreType.DMA((NDEV - 1,)), pltpu.SemaphoreType.DMA((NDEV - 1,)),
                        pltpu.SemaphoreType.DMA],
    )(buf)


def _adam_math(wv, g, m, v):
    m_new = B1 * m + (1.0 - B1) * g
    v_new = B2 * v + (1.0 - B2) * (g * g)
    m_hat = m_new / (1.0 - B1 ** STEP)
    v_hat = v_new / (1.0 - B2 ** STEP)
    return -LR * (m_hat / (jnp.sqrt(v_hat) + ADAM_EPS) + WD * wv), m_new, v_new


def _adam(name, parts, wv, m, v):
    npart, r, c = parts.shape
    tr, tc = _ew_tile(r, c)

    def body(p_ref, w_ref, m_ref, v_ref, g_ref, d_ref, mo_ref, vo_ref):
        g = p_ref[0].astype(F32)
        for j in range(1, npart):
            g = g + p_ref[j].astype(F32)
        g_ref[...] = g
        d_ref[...], mo_ref[...], vo_ref[...] = _adam_math(w_ref[...], g, m_ref[...], v_ref[...])

    blk = pl.BlockSpec((tr, tc), lambda i, k: (i, k))
    return pl.pallas_call(
        body, name=name, grid=(r // tr, c // tc),
        in_specs=[pl.BlockSpec((npart, tr, tc), lambda i, k: (0, i, k)), blk, blk, blk],
        out_specs=[blk] * 4, out_shape=[jax.ShapeDtypeStruct((r, c), F32)] * 4, compiler_params=_params(2),
    )(parts, wv, m, v)


SHARDED = ("ffn1_w_gate", "ffn1_w_up", "ffn1_w_down", "w_in", "w_uq", "w_ukv", "w_branch_a", "w_branch_b", "w_out",
           "ffn2_w_gate", "ffn2_w_up", "ffn2_w_down", "w_pl", "w_pl_gate")
TRANSPOSED = ("ffn1_w_gate", "ffn1_w_up", "ffn2_w_gate", "ffn2_w_up", "w_in", "w_uq")
REPLICATED = ("ffn1_norm", "mix_norm", "q_a_norm", "kv_a_norm", "na_rpb", "ffn2_norm", "pl_norm", "final_norm")
WEIGHTS = ("ffn1_norm", "ffn1_w_gate", "ffn1_w_up", "ffn1_w_down", "mix_norm", "w_in", "q_a_norm", "w_uq",
           "kv_a_norm", "w_ukv", "na_rpb", "w_branch_a", "w_branch_b", "w_out", "ffn2_norm", "ffn2_w_gate",
           "ffn2_w_up", "ffn2_w_down", "pl_norm", "w_pl", "w_pl_gate", "final_norm")
SMALL_W = 2048


def _pack_small(vals):
    rows = []
    for name in REPLICATED:
        flat = vals[name].reshape(-1).astype(F32)
        n = -(-flat.shape[0] // SMALL_W) * SMALL_W
        rows.append(jnp.pad(flat, (0, n - flat.shape[0])).reshape(-1, SMALL_W))
    return jnp.concatenate(rows, axis=0)


def _unpack_small(buf, shapes):
    out, r = {}, 0
    for name in REPLICATED:
        size = int(np.prod(shapes[name]))
        nrow = -(-size // SMALL_W)
        out[name] = buf[r:r + nrow].reshape(-1)[:size].reshape(shapes[name])
        r += nrow
    return out


def kernel(x, p, ffn1_norm, ffn1_w_gate, ffn1_w_up, ffn1_w_down, mix_norm, w_in, q_a_norm, w_uq, kv_a_norm, w_ukv, na_rpb, w_branch_a, w_branch_b, w_out, ffn2_norm, ffn2_w_gate, ffn2_w_up, ffn2_w_down, pl_norm, w_pl, w_pl_gate, final_norm, loss_target, m_ffn1_norm, m_ffn1_w_gate, m_ffn1_w_up, m_ffn1_w_down, m_mix_norm, m_w_in, m_q_a_norm, m_w_uq, m_kv_a_norm, m_w_ukv, m_na_rpb, m_w_branch_a, m_w_branch_b, m_w_out, m_ffn2_norm, m_ffn2_w_gate, m_ffn2_w_up, m_ffn2_w_down, m_pl_norm, m_w_pl, m_w_pl_gate, m_final_norm, v_ffn1_norm, v_ffn1_w_gate, v_ffn1_w_up, v_ffn1_w_down, v_mix_norm, v_w_in, v_q_a_norm, v_w_uq, v_kv_a_norm, v_w_ukv, v_na_rpb, v_w_branch_a, v_w_branch_b, v_w_out, v_ffn2_norm, v_ffn2_w_gate, v_ffn2_w_up, v_ffn2_w_down, v_pl_norm, v_w_pl, v_w_pl_gate, v_final_norm):
    args = dict(locals())
    wts = {n: args[n] for n in WEIGHTS}
    mom = {n: args["m_" + n] for n in WEIGHTS}
    var = {n: args["v_" + n] for n in WEIGHTS}
    shapes = {n: wts[n].shape for n in WEIGHTS}
    core = lax.axis_index("c").astype(jnp.int32).reshape(1)

    local = lambda n, a: a[0].T if n in TRANSPOSED else a[0]
    own = {n: local(n, wts[n]).astype(BF) for n in SHARDED}
    sp = {n: wts[n].reshape(1, -1) for n in REPLICATED if n != "na_rpb"}
    sp["na_rpb"] = wts["na_rpb"][0]
    loss_part, grad_x, chip_parts, dsp = _device_step(x[0], p[0, 0], loss_target[0], sp, own, core)

    out = {}
    for n in SHARDED:
        res4 = _adam("adam_" + n, chip_parts[n], local(n, wts[n]), local(n, mom[n]), local(n, var[n]))
        out[n] = tuple((a.T if n in TRANSPOSED else a)[None] for a in res4)

    small = jnp.concatenate([_pack_small(dsp), jnp.pad(loss_part, ((0, 0), (0, SMALL_W - loss_part.shape[1])))], 0)
    pad_rows = -small.shape[0] % 8
    small = jnp.pad(small, ((0, pad_rows), (0, 0)))
    every = _gather_small(small)
    zeros = jnp.zeros((1 + pad_rows, SMALL_W), F32)
    pack = lambda d: jnp.concatenate([_pack_small(d), zeros], 0)
    g_s, d_s, m_s, v_s = _adam("adam_small", every, pack(wts), pack(mom), pack(var))
    n_rows = small.shape[0] - 1 - pad_rows
    loss = g_s[n_rows, 0]
    small_out = [_unpack_small(b, shapes) for b in (g_s, d_s, m_s, v_s)]
    for n in REPLICATED:
        out[n] = tuple(b[n] for b in small_out)

    res = [loss, grad_x[None]]
    for k in range(4):
        res += [out[n][k] for n in WEIGHTS]
    return tuple(res)
```

```python
import functools

import numpy as np
import jax
import jax.numpy as jnp
from jax import lax
from jax.experimental import pallas as pl
from jax.experimental.pallas import tpu as pltpu

F32 = jnp.float32
BF = jnp.bfloat16
MESH = pl.DeviceIdType.MESH

NDEV = 8
NCHIP = 4
VMEM_LIMIT = 56 * 1024 * 1024
EPS = 1e-6
NEG = -1e30
GRID_W = 64
NA_HEADS, NA_DIM = 8, 128
NA_ROWS_WIN, NA_COLS_WIN = 8, 16
NA_HG = 4
ML_HEADS, ML_NOPE, ML_ROPE, ML_V = 8, 128, 64, 128
ML_QK = ML_NOPE + ML_ROPE
ML_RANK = 512
ROPE_THETA = 10000.0
LR, B1, B2, ADAM_EPS, WD, STEP = 0.001, 0.9, 0.999, 1e-08, 0.01, 10
HI = lax.Precision.HIGHEST

_DN = {"nn": (((1,), (0,)), ((), ())), "nt": (((1,), (1,)), ((), ())), "tn": (((0,), (0,)), ((), ()))}


def _params(n):
    return pltpu.CompilerParams(dimension_semantics=("arbitrary",) * n, vmem_limit_bytes=VMEM_LIMIT)


def _sig(v):
    return jax.nn.sigmoid(v)


ANY = pl.BlockSpec(memory_space=pl.ANY)


def _coords():
    return lax.axis_index("x"), lax.axis_index("y"), lax.axis_index("c")


class _Part:
    inputs, out_shapes, sem_shapes, results = (), (), (), None


class _GatherPart(_Part):
    def __init__(self, names, shards):
        n = len(shards)
        self.names, self.inputs = list(names), list(shards)
        self.out_shapes = [jax.ShapeDtypeStruct((NDEV,) + a.shape, a.dtype) for a in shards]
        self.sem_shapes = [pltpu.SemaphoreType.DMA((n, 7)), pltpu.SemaphoreType.DMA((n, 7)),
                           pltpu.SemaphoreType.DMA((n,))]

    def _plan(self, ins, outs, sems):
        send_sems, recv_sems, local_sems = sems
        x, y, c = _coords()
        chips = [(1 - x, y), (x, 1 - y), (1 - x, 1 - y)]

        def copy(i, k, block, to, src=None):
            px, py, pc = block
            dst = outs[i].at[4 * px + 2 * py + pc]
            return pltpu.make_async_remote_copy(
                src_ref=dst if src is None else src, dst_ref=dst, send_sem=send_sems.at[i, k],
                recv_sem=recv_sems.at[i, k], device_id=to, device_id_type=MESH)

        n = len(ins)
        mine = [pltpu.make_async_copy(ins[i], outs[i].at[4 * x + 2 * y + c], local_sems.at[i]) for i in range(n)]
        first = []
        for i in range(n):
            first.append(copy(i, 0, (x, y, c), (x, y, 1 - c), src=ins[i]))
            first += [copy(i, 1 + j, (x, y, c), (*chip, c), src=ins[i]) for j, chip in enumerate(chips)]
        return copy, mine, first, chips, (x, y, c)

    def start(self, ins, outs, sems):
        _, mine, first, _, _ = self._plan(ins, outs, sems)
        for cp in mine + first:
            cp.start()

    def finish(self, ins, outs, sems):
        copy, mine, first, chips, (x, y, c) = self._plan(ins, outs, sems)
        n = len(ins)
        passed = []
        for i in range(n):
            for j, chip in enumerate(chips):
                copy(i, 1 + j, (*chip, c), (x, y, c)).wait_recv()
                fw = copy(i, 4 + j, (*chip, c), (x, y, 1 - c))
                fw.start()
                passed.append(fw)
        for i in range(n):
            copy(i, 0, (x, y, 1 - c), (x, y, c)).wait_recv()
            for j, chip in enumerate(chips):
                copy(i, 4 + j, (*chip, 1 - c), (x, y, c)).wait_recv()
        for cp in first + passed:
            cp.wait_send()
        for cp in mine:
            cp.wait()


class _SiblingPart(_Part):
    def __init__(self, names, parts):
        n = len(parts)
        self.names, self.inputs = list(names), list(parts)
        self.out_shapes = [jax.ShapeDtypeStruct((NCHIP,) + a.shape[2:], a.dtype) for a in parts]
        self.sem_shapes = [pltpu.SemaphoreType.DMA((n,)), pltpu.SemaphoreType.DMA((n,))]

    def _copies(self, ins, outs, sems):
        x, y, c = _coords()
        return [pltpu.make_async_remote_copy(
            src_ref=ins[i].at[:, 1 - c], dst_ref=outs[i], send_sem=sems[0].at[i], recv_sem=sems[1].at[i],
            device_id=(x, y, 1 - c), device_id_type=MESH) for i in range(len(ins))]

    def start(self, ins, outs, sems):
        for cp in self._copies(ins, outs, sems):
            cp.start()

    def finish(self, ins, outs, sems):
        cps = self._copies(ins, outs, sems)
        for cp in cps:
            cp.wait_recv()
        for cp in cps:
            cp.wait_send()


class _ChipsPart(_Part):
    def __init__(self, names, sums):
        n = len(sums)
        self.names, self.inputs = list(names), list(sums)
        self.out_shapes = [jax.ShapeDtypeStruct(a.shape, a.dtype) for a in sums]
        self.sem_shapes = [pltpu.SemaphoreType.DMA((n, 3)), pltpu.SemaphoreType.DMA((n, 3)),
                           pltpu.SemaphoreType.DMA((n,))]

    def _plan(self, ins, outs, sems):
        send_sems, recv_sems, local_sems = sems
        x, y, c = _coords()
        my_chip = 2 * x + y
        chips = [(1 - x, y), (x, 1 - y), (1 - x, 1 - y)]
        n = len(ins)
        mine = [pltpu.make_async_copy(ins[i].at[my_chip], outs[i].at[my_chip], local_sems.at[i]) for i in range(n)]
        sends, recvs = [], []
        for i in range(n):
            for k, (px, py) in enumerate(chips):
                sends.append(pltpu.make_async_remote_copy(
                    src_ref=ins[i].at[2 * px + py], dst_ref=outs[i].at[my_chip], send_sem=send_sems.at[i, k],
                    recv_sem=recv_sems.at[i, k], device_id=(px, py, c), device_id_type=MESH))
                recvs.append(pltpu.make_async_remote_copy(
                    src_ref=ins[i].at[my_chip], dst_ref=outs[i].at[2 * px + py], send_sem=send_sems.at[i, k],
                    recv_sem=recv_sems.at[i, k], device_id=(px, py, c), device_id_type=MESH))
        return mine, sends, recvs

    def start(self, ins, outs, sems):
        mine, sends, _ = self._plan(ins, outs, sems)
        for cp in mine + sends:
            cp.start()

    def finish(self, ins, outs, sems):
        mine, sends, recvs = self._plan(ins, outs, sems)
        for cp in recvs:
            cp.wait_recv()
        for cp in sends:
            cp.wait_send()
        for cp in mine:
            cp.wait()


def _call(name, body, grid, in_specs, out_specs, out_shape, args, comm=()):
    comm = [p for p in comm if p is not None]
    single = not isinstance(out_shape, (list, tuple))
    o_specs = [out_specs] if single else list(out_specs)
    o_shape = [out_shape] if single else list(out_shape)
    n_in, n_out = len(in_specs), len(o_specs)
    c_in = [a for p in comm for a in p.inputs]
    c_out = [s for p in comm for s in p.out_shapes]
    c_sem = [s for p in comm for s in p.sem_shapes]

    def wrapped(*refs):
        ins, outs = refs[:n_in], refs[n_in + len(c_in):n_in + len(c_in) + n_out]
        pos = [n_in, n_in + len(c_in) + n_out, n_in + len(c_in) + n_out + len(c_out)]
        split = []
        for p in comm:
            sizes = [len(p.inputs), len(p.out_shapes), len(p.sem_shapes)]
            split.append([refs[o:o + n] for o, n in zip(pos, sizes)])
            pos = [o + n for o, n in zip(pos, sizes)]
        ids = [pl.program_id(a) for a in range(len(grid))]

        def run(which, when):
            def go():
                for p, cut in zip(comm, split):
                    getattr(p, which)(*cut)
            if not comm:
                return
            if ids:
                pl.when(functools.reduce(jnp.logical_and, when))(go)
            else:
                go()

        run("start", [i == 0 for i in ids])
        body(*ins, *outs)
        run("finish", [i == g - 1 for i, g in zip(ids, grid)])

    res = pl.pallas_call(
        wrapped, name=name, grid=grid, in_specs=list(in_specs) + [ANY] * len(c_in),
        out_specs=o_specs + [ANY] * len(c_out), out_shape=o_shape + c_out, scratch_shapes=c_sem,
        compiler_params=_params(len(grid)),
    )(*args, *c_in)
    pos = n_out
    for p in comm:
        p.results = list(res[pos:pos + len(p.out_shapes)])
        pos += len(p.out_shapes)
    return res[0] if single else list(res[:n_out])


def _comm_only(name, comm):
    def body(o_ref):
        o_ref[...] = jnp.zeros_like(o_ref)

    _call(name, body, (), [], pl.BlockSpec(memory_space=pltpu.VMEM), jax.ShapeDtypeStruct((8, 128), F32), [], comm)


def _mm(name, grid, prods, extras, outs, epi, nacc=1, comm=()):
    n_p, n_e = len(prods), len(extras)

    def body(*refs):
        ab, ex, out = refs[:2 * n_p], refs[2 * n_p:2 * n_p + n_e], refs[2 * n_p + n_e:]
        accs = [None] * nacc
        for i, prod in enumerate(prods):
            dn, acc, loop = prod[6], prod[7], prod[8]
            a_ref, b_ref = ab[2 * i], ab[2 * i + 1]
            if loop:
                for g in range(loop):
                    t = lax.dot_general(a_ref[g], b_ref[g], _DN[dn], preferred_element_type=F32)
                    accs[acc] = t if accs[acc] is None else accs[acc] + t
            else:
                t = lax.dot_general(a_ref[...], b_ref[...], _DN[dn], preferred_element_type=F32)
                accs[acc] = t if accs[acc] is None else accs[acc] + t
        epi(accs, ex, out)

    in_specs, args = [], []
    for prod in prods:
        in_specs += [pl.BlockSpec(prod[1], prod[2]), pl.BlockSpec(prod[4], prod[5])]
        args += [prod[0], prod[3]]
    for e, e_blk, e_map in extras:
        in_specs.append(pl.BlockSpec(e_blk, e_map))
        args.append(e)
    return _call(name, body, grid, in_specs, [pl.BlockSpec(blk, mp) for _, _, blk, mp in outs],
                 [jax.ShapeDtypeStruct(s, d) for s, d, _, _ in outs], args, comm)


def _store(accs, ex, out):
    out[0][...] = accs[0].astype(out[0].dtype)


def _ew_tile(r, c, budget=3 << 19):
    for t in range(r - r % 16, 0, -16):
        if r % t == 0 and t * c * 4 <= budget:
            return t, c
    for t in range(c - c % 128, 0, -128):
        if c % t == 0 and r * t * 4 <= budget:
            return r, t
    return r, c


def _tile(n, want):
    t = min(n, want)
    assert n % t == 0, (n, want)
    return t


def _mm_nn(name, a, b, out_dtype, tm=512, tn=512, comm=()):
    m, k = a.shape
    n = b.shape[1]
    tm, tn = _tile(m, tm), (tn if n % tn == 0 else n)
    return _mm(name, (n // tn, m // tm),
               [(a, (tm, k), lambda j, i: (i, 0), b, (k, tn), lambda j, i: (0, j), "nn", 0, 0)], [],
               [((m, n), out_dtype, (tm, tn), lambda j, i: (i, j))], _store, comm=comm)[0]


def _mm_nt(name, a, bt, out_dtype, tm=512, tn=512, comm=()):
    m, k = a.shape
    n = bt.shape[0]
    tm, tn = _tile(m, tm), (tn if n % tn == 0 else n)
    return _mm(name, (n // tn, m // tm),
               [(a, (tm, k), lambda j, i: (i, 0), bt, (tn, k), lambda j, i: (j, 0), "nt", 0, 0)], [],
               [((m, n), out_dtype, (tm, tn), lambda j, i: (i, j))], _store, comm=comm)[0]


def _mm_tn_into(name, a, b, buf, row0, ta=512, tb=512):
    t, ka = a.shape
    nb = b.shape[1]
    ta, tb = (ta if ka % ta == 0 else ka), (tb if nb % tb == 0 else nb)

    def body(a_ref, b_ref, buf_in, buf_out, tile, sem):
        i, j = pl.program_id(0), pl.program_id(1)
        tile[...] = lax.dot_general(a_ref[...], b_ref[...], _DN["tn"], preferred_element_type=F32).astype(tile.dtype)
        rows = pl.ds(pl.multiple_of(row0 + i * ta, 16), ta)
        cp = pltpu.make_async_copy(tile, buf_out.at[rows, pl.ds(pl.multiple_of(j * tb, 128), tb)], sem)
        cp.start()
        cp.wait()

    return pl.pallas_call(
        body, name=name, grid=(ka // ta, nb // tb),
        in_specs=[pl.BlockSpec((t, ta), lambda i, j: (0, i)), pl.BlockSpec((t, tb), lambda i, j: (0, j)), ANY],
        out_specs=ANY, out_shape=jax.ShapeDtypeStruct(buf.shape, buf.dtype), input_output_aliases={2: 0},
        scratch_shapes=[pltpu.VMEM((ta, tb), buf.dtype), pltpu.SemaphoreType.DMA],
        compiler_params=_params(2))(a, b, buf)


def _mm_tn(name, a, b, out_dtype, ta=512, tb=512, scale=None):
    t, ka = a.shape
    nb = b.shape[1]
    ta, tb = (ta if ka % ta == 0 else ka), (tb if nb % tb == 0 else nb)

    def epi(accs, ex, out):
        v = accs[0] if scale is None else accs[0] * scale
        out[0][...] = v.astype(out[0].dtype)

    return _mm(name, (ka // ta, nb // tb),
               [(a, (t, ta), lambda i, j: (0, i), b, (t, tb), lambda i, j: (0, j), "tn", 0, 0)], [],
               [((ka, nb), out_dtype, (ta, tb), lambda i, j: (i, j))], epi)[0]


def _rms_fwd(name, x, g, tm=256):
    s, d = x.shape
    tm = _tile(s, tm)

    def body(x_ref, g_ref, o_ref):
        v = x_ref[...]
        o_ref[...] = (v * lax.rsqrt(jnp.mean(v * v, axis=-1, keepdims=True) + EPS) * g_ref[...]).astype(o_ref.dtype)

    return pl.pallas_call(
        body, name=name, grid=(s // tm,),
        in_specs=[pl.BlockSpec((tm, d), lambda i: (i, 0)), pl.BlockSpec((1, d), lambda i: (0, 0))],
        out_specs=pl.BlockSpec((tm, d), lambda i: (i, 0)), out_shape=jax.ShapeDtypeStruct((s, d), BF),
        compiler_params=_params(1))(x, g)


def _acc_rows(ref, part, i):
    @pl.when(i == 0)
    def _():
        ref[...] = part

    @pl.when(i > 0)
    def _():
        ref[...] += part


def _rms_bwd_math(dn, v, g):
    rstd = lax.rsqrt(jnp.mean(v * v, axis=-1, keepdims=True) + EPS)
    xh = v * rstd
    dxh = dn * g
    dx = rstd * (dxh - xh * jnp.mean(dxh * xh, axis=-1, keepdims=True))
    return dx, jnp.sum(dn * xh, axis=0, keepdims=True)


def _rms_bwd(name, dn, x, g, resid, tm=256):
    s, d = x.shape
    tm = _tile(s, tm)

    def body(dn_ref, x_ref, g_ref, r_ref, dx_ref, dg_ref):
        dx, part = _rms_bwd_math(dn_ref[...].astype(F32), x_ref[...], g_ref[...])
        dx_ref[...] = r_ref[...] + dx
        _acc_rows(dg_ref, part, pl.program_id(0))

    row = pl.BlockSpec((tm, d), lambda i: (i, 0))
    one = pl.BlockSpec((1, d), lambda i: (0, 0))
    return pl.pallas_call(
        body, name=name, grid=(s // tm,), in_specs=[row, row, one, row], out_specs=[row, one],
        out_shape=[jax.ShapeDtypeStruct((s, d), F32), jax.ShapeDtypeStruct((1, d), F32)],
        compiler_params=_params(1))(dn, x, g, resid)


def _loss_head(h, target, g, tm=256):
    s, d = h.shape
    tm = _tile(s, tm)

    def body(h_ref, t_ref, g_ref, dh_ref, dg_ref, loss_ref):
        v, gv = h_ref[...], g_ref[...]
        rstd = lax.rsqrt(jnp.mean(v * v, axis=-1, keepdims=True) + EPS)
        xh = v * rstd
        err = xh * gv - t_ref[...]
        part_loss = 0.5 * jnp.sum(jnp.mean(err * err, axis=-1, keepdims=True), axis=0, keepdims=True)
        dy = err * (1.0 / d)
        dxh = dy * gv
        dh_ref[...] = rstd * (dxh - xh * jnp.mean(dxh * xh, axis=-1, keepdims=True))
        i = pl.program_id(0)
        _acc_rows(dg_ref, jnp.sum(dy * xh, axis=0, keepdims=True), i)
        _acc_rows(loss_ref, jnp.broadcast_to(part_loss, loss_ref.shape), i)

    row = pl.BlockSpec((tm, d), lambda i: (i, 0))
    one = pl.BlockSpec((1, d), lambda i: (0, 0))
    return pl.pallas_call(
        body, name="loss_head", grid=(s // tm,), in_specs=[row, row, one],
        out_specs=[row, one, pl.BlockSpec((1, 128), lambda i: (0, 0))],
        out_shape=[jax.ShapeDtypeStruct((s, d), F32), jax.ShapeDtypeStruct((1, d), F32),
                   jax.ShapeDtypeStruct((1, 128), F32)],
        compiler_params=_params(1))(h, target, g)


def _pl_bwd_elem(dh, pe, t, tm=256):
    s, d = dh.shape
    tm = _tile(s, tm)

    def body(dh_ref, pe_ref, t_ref, dt_ref, dpe_ref):
        dh_v, sg = dh_ref[...], _sig(t_ref[...])
        dt_ref[...] = (dh_v * pe_ref[...].astype(F32) * sg * (1.0 - sg)).astype(BF)
        dpe_ref[...] = (dh_v * sg).astype(BF)

    row = pl.BlockSpec((tm, d), lambda i: (i, 0))
    return pl.pallas_call(
        body, name="pl_bwd_elem", grid=(s // tm,), in_specs=[row, row, row], out_specs=[row, row],
        out_shape=[jax.ShapeDtypeStruct((s, d), BF)] * 2, compiler_params=_params(1))(dh, pe, t)


def _ffn_up(name, xn, wg, wu, tm=512, comm=()):
    s, d = xn.shape
    g, fb, _ = wg.shape
    tm = _tile(s, tm)

    def epi(accs, ex, out):
        hg, hu = accs
        out[0][...] = hg.astype(BF)
        out[1][...] = hu.astype(BF)
        out[2][...] = (hg * _sig(hg) * hu).astype(BF)

    a_map = lambda j, i: (i, 0)
    w_map = lambda j, i: (j, 0, 0)
    o = ((g, s, fb), BF, (None, tm, fb), lambda j, i: (j, i, 0))
    return _mm(name, (g, s // tm),
               [(xn, (tm, d), a_map, wg, (None, fb, d), w_map, "nt", 0, 0),
                (xn, (tm, d), a_map, wu, (None, fb, d), w_map, "nt", 1, 0)], [], [o, o, o], epi, nacc=2, comm=comm)


def _ffn_down(name, a, wd, resid, tm=512, tn=512, comm=()):
    g, s, fb = a.shape
    d = wd.shape[2]
    tm, tn = _tile(s, tm), _tile(d, tn)

    def epi(accs, ex, out):
        out[0][...] = ex[0][...] + 0.5 * accs[0]

    return _mm(name, (d // tn, s // tm),
               [(a, (g, tm, fb), lambda j, i: (0, i, 0), wd, (g, fb, tn), lambda j, i: (0, 0, j), "nn", 0, g)],
               [(resid, (tm, tn), lambda j, i: (i, j))],
               [((s, d), F32, (tm, tn), lambda j, i: (i, j))], epi, comm=comm)[0]


def _ffn_bwd_act(name, dh, wd, hg, hu, tm=512, comm=()):
    s, d = dh.shape
    g, fb, _ = wd.shape
    tm = _tile(s, tm)

    def epi(accs, ex, out):
        da = 0.5 * accs[0]
        hg_v, hu_v = ex[0][...].astype(F32), ex[1][...].astype(F32)
        sg = _sig(hg_v)
        out[0][...] = (da * hu_v * (sg * (1.0 + hg_v * (1.0 - sg)))).astype(BF)
        out[1][...] = (da * (hg_v * sg)).astype(BF)

    blk = (None, tm, fb)
    gmap = lambda j, i: (j, i, 0)
    return _mm(name, (g, s // tm),
               [(dh, (tm, d), lambda j, i: (i, 0), wd, (None, fb, d), lambda j, i: (j, 0, 0), "nt", 0, 0)],
               [(hg, blk, gmap), (hu, blk, gmap)],
               [((g, s, fb), BF, blk, gmap), ((g, s, fb), BF, blk, gmap)], epi, comm=comm)


def _ffn_bwd_wd(name, a, dh, tn=512, comm=()):
    g, s, fb = a.shape
    d = dh.shape[1]
    tn = _tile(d, tn)

    def epi(accs, ex, out):
        out[0][...] = (0.5 * accs[0]).astype(BF)

    return _mm(name, (g, d // tn),
               [(a, (None, s, fb), lambda j, i: (j, 0, 0), dh, (s, tn), lambda j, i: (0, i), "tn", 0, 0)], [],
               [((g, fb, d), BF, (None, fb, tn), lambda j, i: (j, 0, i))], epi, comm=comm)[0]


def _ffn_bwd_wup(name, xn, dhg, dhu, tk=512, comm=()):
    s, d = xn.shape
    g, _, fb = dhg.shape
    tk = _tile(d, tk)

    def epi(accs, ex, out):
        out[0][...] = accs[0].astype(BF)
        out[1][...] = accs[1].astype(BF)

    a_map = lambda j, i: (j, 0, 0)
    b_map = lambda j, i: (0, i)
    o = ((g, fb, d), BF, (None, fb, tk), lambda j, i: (j, 0, i))
    return _mm(name, (g, d // tk),
               [(dhg, (None, s, fb), a_map, xn, (s, tk), b_map, "tn", 0, 0),
                (dhu, (None, s, fb), a_map, xn, (s, tk), b_map, "tn", 1, 0)], [], [o, o], epi, nacc=2, comm=comm)


def _ffn_bwd_x(name, dhg, dhu, wg, wu, tm=512, tn=512, comm=()):
    g, s, fb = dhg.shape
    d = wg.shape[2]
    tm, tn = _tile(s, tm), _tile(d, tn)
    a_blk, a_map = (g, tm, fb), lambda j, i: (0, i, 0)
    b_blk, b_map = (g, fb, tn), lambda j, i: (0, 0, j)
    return _mm(name, (d // tn, s // tm),
               [(dhg, a_blk, a_map, wg, b_blk, b_map, "nn", 0, g), (dhu, a_blk, a_map, wu, b_blk, b_map, "nn", 0, g)],
               [], [((s, d), F32, (tm, tn), lambda j, i: (i, j))], _store, comm=comm)[0]


def _ffn_forward(tag, h, gain, wg, wu, get_wd, up_comm=(), down_comm=()):
    xn = _rms_fwd(tag + "_norm", h, gain)
    hg, hu, a = _ffn_up(tag + "_up", xn, wg, wu, comm=up_comm)
    return _ffn_down(tag + "_down", a, get_wd(), h, comm=down_comm), (xn, hg, hu, a)


def _na_geometry(rows):
    kh = min(NA_ROWS_WIN, rows)
    cols = np.arange(GRID_W)
    col_start = np.clip(cols - NA_COLS_WIN // 2, 0, GRID_W - NA_COLS_WIN)
    mask = (cols[None, :] >= col_start[:, None]) & (cols[None, :] < col_start[:, None] + NA_COLS_WIN)
    dc = np.clip(cols[None, :] - cols[:, None], -(NA_COLS_WIN - 1), NA_COLS_WIN - 1) + (NA_COLS_WIN - 1)
    return kh, mask, dc


def _na_bias_tables(rpb, rows):
    kh, mask, dc = _na_geometry(rows)
    t = jnp.where(jnp.asarray(mask)[None, None], rpb[:, :, dc], NEG)
    tb = jnp.stack([t[:, d0:d0 + kh] for d0 in range(NA_ROWS_WIN)], 0)
    return tb.transpose(0, 1, 3, 2, 4).reshape(NA_ROWS_WIN, NA_HEADS, GRID_W, kh * GRID_W)


def _na_row_start(r, rows, kh):
    return jnp.clip(r - kh // 2, 0, rows - kh)


def _na_specs(s, rows, kh):
    hw = NA_HG * NA_DIM
    nq = NA_HEADS // NA_HG
    q_spec = pl.BlockSpec((GRID_W, hw), lambda j, r: (r, j))
    k_spec = pl.BlockSpec((s, hw), lambda j, r: (0, nq + j))
    v_spec = pl.BlockSpec((s, hw), lambda j, r: (0, 2 * nq + j))
    b_spec = pl.BlockSpec((None, NA_HG, GRID_W, kh * GRID_W),
                          lambda j, r: (_na_row_start(r, rows, kh) - r + NA_ROWS_WIN - 1, j, 0, 0))
    return q_spec, k_spec, v_spec, b_spec, hw, nq


def _na_probs(q, k, bias):
    sc = lax.dot_general(q, k, _DN["nt"], preferred_element_type=F32) * (NA_DIM ** -0.5) + bias
    e = jnp.exp(sc - jnp.max(sc, axis=-1, keepdims=True))
    return e / jnp.sum(e, axis=-1, keepdims=True)


def _na_fwd(qkv, tb, comm=()):
    s = qkv.shape[0]
    rows = s // GRID_W
    kh = min(NA_ROWS_WIN, rows)
    q_spec, k_spec, v_spec, b_spec, hw, nq = _na_specs(s, rows, kh)

    def body(q_ref, k_ref, v_ref, b_ref, o_ref):
        r = pl.program_id(1)
        start = pl.multiple_of(_na_row_start(r, rows, kh) * GRID_W, GRID_W)
        for h in range(NA_HG):
            cs = slice(h * NA_DIM, (h + 1) * NA_DIM)
            p = _na_probs(q_ref[:, cs], k_ref[pl.ds(start, kh * GRID_W), cs], b_ref[h])
            o_ref[:, cs] = jnp.dot(p.astype(BF), v_ref[pl.ds(start, kh * GRID_W), cs],
                                   preferred_element_type=F32).astype(BF)

    return _call("na_fwd", body, (nq, rows), [q_spec, k_spec, v_spec, b_spec],
                 pl.BlockSpec((GRID_W, hw), lambda j, r: (r, j)),
                 jax.ShapeDtypeStruct((s, NA_HEADS * NA_DIM), BF), [qkv, qkv, qkv, tb], comm)


def _na_bwd(qkv, tb, do, comm=()):
    s = qkv.shape[0]
    rows = s // GRID_W
    kh = min(NA_ROWS_WIN, rows)
    q_spec, k_spec, v_spec, b_spec, hw, nq = _na_specs(s, rows, kh)
    nd = 2 * NA_ROWS_WIN - 1

    def body(q_ref, k_ref, v_ref, b_ref, do_ref, dq_ref, dk_ref, dv_ref, dt_ref):
        r = pl.program_id(1)

        @pl.when(r == 0)
        def _():
            dk_ref[...] = jnp.zeros_like(dk_ref)
            dv_ref[...] = jnp.zeros_like(dv_ref)
            dt_ref[...] = jnp.zeros_like(dt_ref)

        rs = _na_row_start(r, rows, kh)
        d0 = rs - r + NA_ROWS_WIN - 1
        win = pl.ds(pl.multiple_of(rs * GRID_W, GRID_W), kh * GRID_W)
        for h in range(NA_HG):
            cs = slice(h * NA_DIM, (h + 1) * NA_DIM)
            q, k, v, do_h = q_ref[:, cs], k_ref[win, cs], v_ref[win, cs], do_ref[:, cs]
            p = _na_probs(q, k, b_ref[h])
            dp = lax.dot_general(do_h, v, _DN["nt"], preferred_element_type=F32)
            ds = p * (dp - jnp.sum(p * dp, axis=-1, keepdims=True))
            for i in range(kh):
                dt_ref[h, d0 + i] += ds[:, i * GRID_W:(i + 1) * GRID_W]
            dsb = (ds * (NA_DIM ** -0.5)).astype(BF)
            dq_ref[:, cs] = jnp.dot(dsb, k, preferred_element_type=F32).astype(BF)
            dk_ref[win, cs] += lax.dot_general(dsb, q, _DN["tn"], preferred_element_type=F32)
            dv_ref[win, cs] += lax.dot_general(p.astype(BF), do_h, _DN["tn"], preferred_element_type=F32)

    width = NA_HEADS * NA_DIM
    whole = pl.BlockSpec((s, hw), lambda j, r: (0, j))
    return _call(
        "na_bwd", body, (nq, rows),
        [q_spec, k_spec, v_spec, b_spec, pl.BlockSpec((GRID_W, hw), lambda j, r: (r, j))],
        [pl.BlockSpec((GRID_W, hw), lambda j, r: (r, j)), whole, whole,
         pl.BlockSpec((NA_HG, nd, GRID_W, GRID_W), lambda j, r: (j, 0, 0, 0))],
        [jax.ShapeDtypeStruct((s, width), BF), jax.ShapeDtypeStruct((s, width), F32),
         jax.ShapeDtypeStruct((s, width), F32), jax.ShapeDtypeStruct((NA_HEADS, nd, GRID_W, GRID_W), F32)],
        [qkv, qkv, qkv, tb, do], comm)


def _na_rpb_grad(dt, rows):
    _, mask, dc = _na_geometry(rows)
    nd, nc = 2 * NA_ROWS_WIN - 1, 2 * NA_COLS_WIN - 1
    onehot = np.zeros((GRID_W * GRID_W, 128), np.float32)
    onehot[np.arange(GRID_W * GRID_W), dc.reshape(-1)] = mask.reshape(-1).astype(np.float32)
    flat = dt.reshape(NA_HEADS * nd, GRID_W * GRID_W)

    def body(a_ref, e_ref, o_ref):
        o_ref[...] = jnp.dot(a_ref[...], e_ref[...], precision=HI, preferred_element_type=F32)

    out = pl.pallas_call(body, name="na_rpb_grad", out_shape=jax.ShapeDtypeStruct((NA_HEADS * nd, 128), F32),
                         compiler_params=_params(0))(flat, jnp.asarray(onehot))
    return out[:, :nc].reshape(NA_HEADS, nd, nc)


def _rope_consts(s):
    pos = np.arange(s, dtype=np.float32)
    inv = (1.0 / (ROPE_THETA ** (np.arange(0, ML_ROPE, 2, dtype=np.float32) / ML_ROPE))).astype(np.float32)
    ang = pos[:, None] * inv[None, :]
    cos, sin = np.cos(ang).astype(np.float32), np.sin(ang).astype(np.float32)
    half = ML_ROPE // 2
    rot = np.zeros((ML_ROPE, ML_ROPE), np.float32)
    rot[np.arange(half) + half, np.arange(half)] = -1.0
    rot[np.arange(half), np.arange(half) + half] = 1.0
    return (jnp.asarray(np.concatenate([cos, cos], 1)), jnp.asarray(np.concatenate([sin, sin], 1)),
            jnp.asarray(rot), jnp.asarray(rot.T.copy()))


def _rope(v, cos, sin, rot):
    return v * cos + jnp.dot(v, rot, precision=HI, preferred_element_type=F32) * sin


def _unrope(dv, cos, sin, rot_t):
    return dv * cos + jnp.dot(dv * sin, rot_t, precision=HI, preferred_element_type=F32)


def _rms(v, g):
    return v * lax.rsqrt(jnp.mean(v * v, axis=-1, keepdims=True) + EPS) * g


def _mla_prep(lat, gq, gkv, cos, sin, rot, tm=256):
    s, w = lat.shape
    tm = _tile(s, tm)

    def body(l_ref, gq_ref, gkv_ref, c_ref, s_ref, r_ref, cq_ref, ckv_ref, kr_ref):
        cq_ref[...] = _rms(l_ref[:, :ML_RANK], gq_ref[...]).astype(BF)
        ckv_ref[...] = _rms(l_ref[:, ML_RANK:2 * ML_RANK], gkv_ref[...]).astype(BF)
        kr_ref[...] = _rope(l_ref[:, 2 * ML_RANK:], c_ref[...], s_ref[...], r_ref[...]).astype(BF)

    row = lambda c: pl.BlockSpec((tm, c), lambda i: (i, 0))
    full = lambda a: pl.BlockSpec(a.shape, lambda i: (0, 0))
    return pl.pallas_call(
        body, name="mla_prep", grid=(s // tm,),
        in_specs=[row(w), full(gq), full(gkv), row(ML_ROPE), row(ML_ROPE), full(rot)],
        out_specs=[row(ML_RANK), row(ML_RANK), row(ML_ROPE)],
        out_shape=[jax.ShapeDtypeStruct((s, ML_RANK), BF), jax.ShapeDtypeStruct((s, ML_RANK), BF),
                   jax.ShapeDtypeStruct((s, ML_ROPE), BF)],
        compiler_params=_params(1))(lat, gq, gkv, cos, sin, rot)


def _mla_q_proj(cq, wuq, cos, sin, rot, tm=512, comm=()):
    s, k = cq.shape
    tm = _tile(s, tm)

    def epi(accs, ex, out):
        acc = accs[0]
        out[0][:, :ML_NOPE] = acc[:, :ML_NOPE].astype(BF)
        out[0][:, ML_NOPE:] = _rope(acc[:, ML_NOPE:], ex[0][...], ex[1][...], ex[2][...]).astype(BF)

    rmap = lambda j, i: (i, 0)
    return _mm("mla_q_proj", (ML_HEADS, s // tm),
               [(cq, (tm, k), rmap, wuq, (None, ML_QK, k), lambda j, i: (j, 0, 0), "nt", 0, 0)],
               [(cos, (tm, ML_ROPE), rmap), (sin, (tm, ML_ROPE), rmap), (rot, rot.shape, lambda j, i: (0, 0))],
               [((ML_HEADS, s, ML_QK), BF, (None, tm, ML_QK), lambda j, i: (j, i, 0))], epi, comm=comm)[0]


def _mla_kv_proj(ckv, wukv, kr, tm=512, comm=()):
    s, k = ckv.shape
    tm = _tile(s, tm)

    def epi(accs, ex, out):
        acc = accs[0]
        out[0][:, :ML_NOPE] = acc[:, :ML_NOPE].astype(BF)
        out[0][:, ML_NOPE:] = ex[0][...]
        out[1][...] = acc[:, ML_NOPE:].astype(BF)

    rmap = lambda j, i: (i, 0)
    gmap = lambda j, i: (j, i, 0)
    return _mm("mla_kv_proj", (ML_HEADS, s // tm),
               [(ckv, (tm, k), rmap, wukv, (None, k, ML_NOPE + ML_V), lambda j, i: (j, 0, 0), "nn", 0, 0)],
               [(kr, (tm, ML_ROPE), rmap)],
               [((ML_HEADS, s, ML_QK), BF, (None, tm, ML_QK), gmap), ((ML_HEADS, s, ML_V), BF, (None, tm, ML_V), gmap)],
               epi, comm=comm)


def _mla_probs(q, k):
    sc = lax.dot_general(q, k, _DN["nt"], preferred_element_type=F32) * (ML_QK ** -0.5)
    e = jnp.exp(sc - jnp.max(sc, axis=-1, keepdims=True))
    return e / jnp.sum(e, axis=-1, keepdims=True)


def _mla_fwd(q, k, v, tq=512, comm=()):
    _, s, _ = q.shape
    tq = _tile(s, tq)

    def body(q_ref, k_ref, v_ref, o_ref):
        p = _mla_probs(q_ref[...], k_ref[...])
        o_ref[...] = jnp.dot(p.astype(BF), v_ref[...], preferred_element_type=F32).astype(BF)

    return _call("mla_fwd", body, (ML_HEADS, s // tq),
                 [pl.BlockSpec((None, tq, ML_QK), lambda h, i: (h, i, 0)),
                  pl.BlockSpec((None, s, ML_QK), lambda h, i: (h, 0, 0)),
                  pl.BlockSpec((None, s, ML_V), lambda h, i: (h, 0, 0))],
                 pl.BlockSpec((tq, ML_V), lambda h, i: (i, h)),
                 jax.ShapeDtypeStruct((s, ML_HEADS * ML_V), BF), [q, k, v], comm)


def _mla_bwd(q, k, v, do, tq=256, comm=()):
    _, s, _ = q.shape
    tq = _tile(s, tq)

    def body(q_ref, k_ref, v_ref, do_ref, dq_ref, dk_ref, dv_ref):
        i = pl.program_id(1)
        qv, kv, vv, dov = q_ref[...], k_ref[...], v_ref[...], do_ref[...]
        p = _mla_probs(qv, kv)
        dp = lax.dot_general(dov, vv, _DN["nt"], preferred_element_type=F32)
        ds = (p * (dp - jnp.sum(p * dp, axis=-1, keepdims=True)) * (ML_QK ** -0.5)).astype(BF)
        dq_ref[...] = jnp.dot(ds, kv, preferred_element_type=F32)
        _acc_rows(dk_ref, lax.dot_general(ds, qv, _DN["tn"], preferred_element_type=F32), i)
        _acc_rows(dv_ref, lax.dot_general(p.astype(BF), dov, _DN["tn"], preferred_element_type=F32), i)

    return _call(
        "mla_bwd", body, (ML_HEADS, s // tq),
        [pl.BlockSpec((None, tq, ML_QK), lambda h, i: (h, i, 0)),
         pl.BlockSpec((None, s, ML_QK), lambda h, i: (h, 0, 0)),
         pl.BlockSpec((None, s, ML_V), lambda h, i: (h, 0, 0)),
         pl.BlockSpec((tq, ML_V), lambda h, i: (i, h))],
        [pl.BlockSpec((None, tq, ML_QK), lambda h, i: (h, i, 0)),
         pl.BlockSpec((None, s, ML_QK), lambda h, i: (h, 0, 0)),
         pl.BlockSpec((None, s, ML_V), lambda h, i: (h, 0, 0))],
        [jax.ShapeDtypeStruct((ML_HEADS, s, ML_QK), F32), jax.ShapeDtypeStruct((ML_HEADS, s, ML_QK), F32),
         jax.ShapeDtypeStruct((ML_HEADS, s, ML_V), F32)],
        [q, k, v, do], comm)


def _mla_post(dq, dk, dv, cos, sin, rot_t, tm=256):
    _, s, _ = dq.shape
    tm = _tile(s, tm)

    def body(dq_ref, dk_ref, dv_ref, c_ref, s_ref, r_ref, dqp_ref, dkv_ref, dkr_ref):
        h = pl.program_id(1)
        dqv, dkk = dq_ref[...], dk_ref[...]
        dqp_ref[:, :ML_NOPE] = dqv[:, :ML_NOPE].astype(BF)
        dqp_ref[:, ML_NOPE:] = _unrope(dqv[:, ML_NOPE:], c_ref[...], s_ref[...], r_ref[...]).astype(BF)
        dkv_ref[:, :ML_NOPE] = dkk[:, :ML_NOPE].astype(BF)
        dkv_ref[:, ML_NOPE:] = dv_ref[...].astype(BF)
        _acc_rows(dkr_ref, dkk[:, ML_NOPE:], h)

    gspec = lambda c: pl.BlockSpec((None, tm, c), lambda i, h: (h, i, 0))
    rspec = pl.BlockSpec((tm, ML_ROPE), lambda i, h: (i, 0))
    return pl.pallas_call(
        body, name="mla_post", grid=(s // tm, ML_HEADS),
        in_specs=[gspec(ML_QK), gspec(ML_QK), gspec(ML_V), rspec, rspec,
                  pl.BlockSpec(rot_t.shape, lambda i, h: (0, 0))],
        out_specs=[gspec(ML_QK), gspec(ML_NOPE + ML_V), rspec],
        out_shape=[jax.ShapeDtypeStruct((ML_HEADS, s, ML_QK), BF),
                   jax.ShapeDtypeStruct((ML_HEADS, s, ML_NOPE + ML_V), BF),
                   jax.ShapeDtypeStruct((s, ML_ROPE), F32)],
        compiler_params=_params(2))(dq, dk, dv, cos, sin, rot_t)


def _mla_lat_bwd(dcq, dckv, dkr, lat, gq, gkv, cos, sin, rot_t, tm=256):
    s, w = lat.shape
    tm = _tile(s, tm)

    def body(dcq_ref, dckv_ref, dkr_ref, l_ref, gq_ref, gkv_ref, c_ref, s_ref, r_ref, dl_ref, dgq_ref, dgkv_ref):
        i = pl.program_id(0)
        dql, pq = _rms_bwd_math(dcq_ref[...], l_ref[:, :ML_RANK], gq_ref[...])
        dkl, pkv = _rms_bwd_math(dckv_ref[...], l_ref[:, ML_RANK:2 * ML_RANK], gkv_ref[...])
        dl_ref[:, :ML_RANK] = dql.astype(BF)
        dl_ref[:, ML_RANK:2 * ML_RANK] = dkl.astype(BF)
        dl_ref[:, 2 * ML_RANK:] = _unrope(dkr_ref[...], c_ref[...], s_ref[...], r_ref[...]).astype(BF)
        _acc_rows(dgq_ref, pq, i)
        _acc_rows(dgkv_ref, pkv, i)

    row = lambda c: pl.BlockSpec((tm, c), lambda i: (i, 0))
    full = lambda a: pl.BlockSpec(a.shape, lambda i: (0, 0))
    return pl.pallas_call(
        body, name="mla_lat_bwd", grid=(s // tm,),
        in_specs=[row(ML_RANK), row(ML_RANK), row(ML_ROPE), row(w), full(gq), full(gkv), row(ML_ROPE), row(ML_ROPE),
                  full(rot_t)],
        out_specs=[row(w), full(gq), full(gkv)],
        out_shape=[jax.ShapeDtypeStruct((s, w), BF), jax.ShapeDtypeStruct(gq.shape, F32),
                   jax.ShapeDtypeStruct(gkv.shape, F32)],
        compiler_params=_params(1))(dcq, dckv, dkr, lat, gq, gkv, cos, sin, rot_t)


def _grp_dw(name, a, dout, ta=512):
    s, k = a.shape
    ta = _tile(k, ta)
    if dout.ndim == 3:
        g, _, nb = dout.shape
        b_blk, b_map = (None, s, nb), lambda j, i: (j, 0, 0)
    else:
        g, nb = NDEV, dout.shape[1] // NDEV
        b_blk, b_map = (s, nb), lambda j, i: (0, j)
    return _mm(name, (g, k // ta),
               [(a, (s, ta), lambda j, i: (0, i), dout, b_blk, b_map, "tn", 0, 0)], [],
               [((g, k, nb), BF, (None, ta, nb), lambda j, i: (j, i, 0))], _store)[0]


def _grp_dw_t(name, dout, a, ta=512):
    g, s, nb = dout.shape
    k = a.shape[1]
    ta = _tile(k, ta)
    return _mm(name, (g, k // ta),
               [(dout, (None, s, nb), lambda j, i: (j, 0, 0), a, (s, ta), lambda j, i: (0, i), "tn", 0, 0)], [],
               [((g, nb, k), BF, (None, nb, ta), lambda j, i: (j, 0, i))], _store)[0]


def _grp_dx_t(name, dout, wt, tm=512, tn=512, comm=()):
    g, s, nb = dout.shape
    k = wt.shape[2]
    tm, tn = _tile(s, tm), _tile(k, tn)
    return _mm(name, (k // tn, s // tm),
               [(dout, (g, tm, nb), lambda j, i: (0, i, 0), wt, (g, nb, tn), lambda j, i: (0, 0, j), "nn", 0, g)], [],
               [((s, k), F32, (tm, tn), lambda j, i: (i, j))], _store, comm=comm)[0]


def _grp_dx(name, dout, w, tm=512, tn=512, comm=()):
    g, s, nb = dout.shape
    k = w.shape[1]
    tm, tn = _tile(s, tm), _tile(k, tn)
    return _mm(name, (k // tn, s // tm),
               [(dout, (g, tm, nb), lambda j, i: (0, i, 0), w, (g, tn, nb), lambda j, i: (0, j, 0), "nt", 0, g)], [],
               [((s, k), F32, (tm, tn), lambda j, i: (i, j))], _store, comm=comm)[0]


def _row_dw(name, a, dout, tn=512):
    s, n = dout.shape
    tn = _tile(n, tn)
    if a.ndim == 3:
        kb = a.shape[2]
        a_blk, a_map = (None, s, kb), lambda j, i: (j, 0, 0)
    else:
        kb = a.shape[1] // NDEV
        a_blk, a_map = (s, kb), lambda j, i: (0, j)
    return _mm(name, (NDEV, n // tn),
               [(a, a_blk, a_map, dout, (s, tn), lambda j, i: (0, i), "tn", 0, 0)], [],
               [((NDEV, kb, n), BF, (None, kb, tn), lambda j, i: (j, 0, i))], _store)[0]


def _mix_merge(oa, ob, wa, wb, ga, gb, tm=512, comm=()):
    s, k = oa.shape
    g, _, nb = wa.shape
    tm = _tile(s, tm)

    def epi(accs, ex, out):
        ya, yb = accs
        out[0][...] = ya.astype(BF)
        out[1][...] = yb.astype(BF)
        out[2][...] = (_sig(ex[0][...]) * ya + _sig(ex[1][...]) * yb).astype(BF)

    rmap = lambda j, i: (i, 0)
    wmap = lambda j, i: (j, 0, 0)
    o = ((g, s, nb), BF, (None, tm, nb), lambda j, i: (j, i, 0))
    cmap = lambda j, i: (i, j)
    return _mm("mix_merge", (g, s // tm),
               [(oa, (tm, k), rmap, wa, (None, k, nb), wmap, "nn", 0, 0),
                (ob, (tm, k), rmap, wb, (None, k, nb), wmap, "nn", 1, 0)],
               [(ga, (tm, nb), cmap), (gb, (tm, nb), cmap)], [o, o, o], epi, nacc=2, comm=comm)


def _mix_out(merged, wout, resid, tm=512, tn=512):
    g, s, kb = merged.shape
    d = wout.shape[2]
    tm, tn = _tile(s, tm), _tile(d, tn)

    def epi(accs, ex, out):
        out[0][...] = ex[0][...] + accs[0]

    return _mm("mix_out", (d // tn, s // tm),
               [(merged, (g, tm, kb), lambda j, i: (0, i, 0), wout, (g, kb, tn), lambda j, i: (0, 0, j), "nn", 0, g)],
               [(resid, (tm, tn), lambda j, i: (i, j))],
               [((s, d), F32, (tm, tn), lambda j, i: (i, j))], epi)[0]


def _mix_out_bwd(dh, wout, ga, gb, ya, yb, tm=512, comm=()):
    s, d = dh.shape
    g, kb, _ = wout.shape
    tm = _tile(s, tm)

    def epi(accs, ex, out):
        dm = accs[0]
        sa, sb = _sig(ex[0][...]), _sig(ex[1][...])
        out[0][...] = (dm * sa).astype(BF)
        out[1][...] = (dm * sb).astype(BF)
        out[2][...] = (dm * ex[2][...].astype(F32) * sa * (1.0 - sa)).astype(BF)
        out[3][...] = (dm * ex[3][...].astype(F32) * sb * (1.0 - sb)).astype(BF)

    cmap = lambda j, i: (i, j)
    gmap = lambda j, i: (j, i, 0)
    og = ((g, s, kb), BF, (None, tm, kb), gmap)
    oc = ((s, g * kb), BF, (tm, kb), cmap)
    return _mm("mix_out_bwd", (g, s // tm),
               [(dh, (tm, d), lambda j, i: (i, 0), wout, (None, kb, d), lambda j, i: (j, 0, 0), "nt", 0, 0)],
               [(ga, (tm, kb), cmap), (gb, (tm, kb), cmap), (ya, (None, tm, kb), gmap), (yb, (None, tm, kb), gmap)],
               [og, og, oc, oc], epi, comm=comm)


def _pl_forward(n4, wplg, p, wpl, h3, tm=512):
    s, d = n4.shape
    g, kb, _ = wplg.shape
    kp, nb = wpl.shape[1], wpl.shape[2]
    tm = _tile(s, tm)
    wplg_nat = wplg.reshape(g * kb, d)

    def epi(accs, ex, out):
        t, pe = accs
        out[0][...] = ex[0][...] + _sig(t) * pe
        out[1][...] = t
        out[2][...] = pe.astype(BF)

    rmap = lambda j, i: (i, 0)
    cmap = lambda j, i: (i, j)
    return _mm("pl_forward", (g, s // tm),
               [(n4, (tm, d), rmap, wplg_nat, (g * kb, nb), lambda j, i: (0, j), "nn", 0, 0),
                (p, (tm, kp), rmap, wpl, (None, kp, nb), lambda j, i: (j, 0, 0), "nn", 1, 0)],
               [(h3, (tm, nb), cmap)],
               [((s, d), F32, (tm, nb), cmap), ((s, d), F32, (tm, nb), cmap), ((s, d), BF, (tm, nb), cmap)],
               epi, nacc=2)


def _row_dx(name, dout, w, tm=512, comm=()):
    s, n = dout.shape
    g, kb, _ = w.shape
    tm = _tile(s, tm)
    return _mm(name, (g, s // tm),
               [(dout, (tm, n), lambda j, i: (i, 0), w, (None, kb, n), lambda j, i: (j, 0, 0), "nt", 0, 0)], [],
               [((s, g * kb), F32, (tm, kb), lambda j, i: (i, j))], _store, comm=comm)[0]


def _in_proj_bwd_x(pieces, weights, tm=512, tn=512, comm=()):
    s = pieces[0].shape[0]
    d = weights[0].shape[1]
    tm, tn = _tile(s, tm), _tile(d, tn)
    prods = [(pc, (tm, pc.shape[1]), lambda j, i: (i, 0), w, (w.shape[0], tn), lambda j, i: (0, j), "nn", 0, 0)
             for pc, w in zip(pieces, weights)]
    return _mm("in_proj_dx", (d // tn, s // tm), prods, [],
               [((s, d), F32, (tm, tn), lambda j, i: (i, j))], _store, comm=comm)[0]


def _split_w_in(w_in_t):
    g, nb, d = w_in_t.shape
    nat = w_in_t.reshape(g * nb, d)
    na, lat = 3 * NA_HEADS * NA_DIM, 2 * ML_RANK + ML_ROPE
    return nat[:na], nat[na:na + lat], nat[na + lat:na + lat + d], nat[na + lat + d:]


def _pair_sum(name, part, landed, core):
    _, _, r, c = part.shape
    tr, tc = _ew_tile(r, c)

    def body(core_ref, a_ref, b_ref, o_ref):
        o_ref[...] = (a_ref[...].astype(F32) + b_ref[...].astype(F32)).astype(o_ref.dtype)

    return pl.pallas_call(
        body, name=name,
        grid_spec=pltpu.PrefetchScalarGridSpec(
            num_scalar_prefetch=1, grid=(NCHIP, r // tr, c // tc),
            in_specs=[pl.BlockSpec((None, None, tr, tc), lambda j, i, k, core_ref: (j, core_ref[0], i, k)),
                      pl.BlockSpec((None, tr, tc), lambda j, i, k, core_ref: (j, i, k))],
            out_specs=pl.BlockSpec((None, tr, tc), lambda j, i, k, core_ref: (j, i, k))),
        out_shape=jax.ShapeDtypeStruct(landed.shape, landed.dtype), compiler_params=_params(3),
    )(core, part, landed)


def _device_step(x, p, target, sp, own, core):
    s, d = x.shape
    rows = s // GRID_W
    cos, sin, rot, rot_t = _rope_consts(s)
    w, dw4, sums, chip_parts, dsp = {}, {}, {}, {}, {}

    def gather(*names):
        return _GatherPart(names, [own[n] for n in names])

    def got(part):
        w.update(zip(part.names, part.results))

    def grad(name, g):
        dw4[name] = g.reshape((NCHIP, 2) + g.shape[1:])

    def to_sibling(*names):
        return _SiblingPart(names, [dw4[n] for n in names])

    def add_pairs(part):
        for n, landed in zip(part.names, part.results):
            sums[n] = _pair_sum("pair_sum_" + n, dw4[n], landed, core)

    def to_chips(*names):
        return _ChipsPart(names, [sums[n] for n in names])

    def done(part):
        chip_parts.update(zip(part.names, part.results))

    c0 = gather("ffn1_w_gate", "ffn1_w_up")
    _comm_only("gather_ffn1", [c0])
    got(c0)
    c1 = gather("ffn1_w_down")
    c2 = gather("w_in")

    def ffn1_wd():
        got(c1)
        return w["ffn1_w_down"]

    h1, ffn1_saved = _ffn_forward("ffn1", x, sp["ffn1_norm"], w["ffn1_w_gate"], w["ffn1_w_up"], ffn1_wd,
                                  up_comm=[c1], down_comm=[c2])
    got(c2)
    wqkv, wlat, wga, wgb = _split_w_in(w["w_in"])
    u = _rms_fwd("mix_norm", h1, sp["mix_norm"])
    c3 = gather("w_uq", "w_ukv")
    qkv = _mm_nt("in_qkv", u, wqkv, BF, tn=1024, comm=[c3])
    got(c3)
    lat = _mm_nt("in_lat", u, wlat, F32)
    c3a = gather("w_branch_a")
    ga = _mm_nt("in_ga", u, wga, F32, tn=1024, comm=[c3a])
    got(c3a)
    c3b = gather("w_branch_b")
    gb = _mm_nt("in_gb", u, wgb, F32, tn=1024, comm=[c3b])
    got(c3b)
    tb = _na_bias_tables(sp["na_rpb"], rows)
    c4 = gather("ffn2_w_gate")
    oa = _na_fwd(qkv, tb, comm=[c4])
    got(c4)
    cq, ckv, kr = _mla_prep(lat, sp["q_a_norm"], sp["kv_a_norm"], cos, sin, rot)
    c4a = gather("w_out")
    qf = _mla_q_proj(cq, w["w_uq"], cos, sin, rot, comm=[c4a])
    got(c4a)
    c4b = gather("w_pl_gate")
    kf, vf = _mla_kv_proj(ckv, w["w_ukv"], kr, comm=[c4b])
    got(c4b)
    c5 = gather("ffn2_w_up")
    ob = _mla_fwd(qf, kf, vf, comm=[c5])
    got(c5)
    c5a = gather("w_pl")
    ya, yb, merged = _mix_merge(oa, ob, w["w_branch_a"], w["w_branch_b"], ga, gb, comm=[c5a])
    got(c5a)
    h2 = _mix_out(merged, w["w_out"], h1)
    c6 = gather("ffn2_w_down")

    def ffn2_wd():
        got(c6)
        return w["ffn2_w_down"]

    h3, ffn2_saved = _ffn_forward("ffn2", h2, sp["ffn2_norm"], w["ffn2_w_gate"], w["ffn2_w_up"], ffn2_wd,
                                  up_comm=[c6])
    n4 = _rms_fwd("pl_norm", h3, sp["pl_norm"])
    pb = p.astype(BF)
    h4, t, pe = _pl_forward(n4, w["w_pl_gate"], pb, w["w_pl"], h3)

    dh4, dsp["final_norm"], loss = _loss_head(h4, target, sp["final_norm"])
    dt, dpe = _pl_bwd_elem(dh4, pe, t)
    grad("w_pl", _grp_dw("pl_dw", pb, dpe))
    grad("w_pl_gate", _row_dw("plg_dw", n4, dt))
    s1 = to_sibling("w_pl", "w_pl_gate")
    dn4 = _row_dx("plg_dx", dt, w["w_pl_gate"], comm=[s1])
    add_pairs(s1)
    dh3, dsp["pl_norm"] = _rms_bwd("pl_dnorm", dn4, h3, sp["pl_norm"], dh4)

    xn, hg, hu, a = ffn2_saved
    dhb = dh3.astype(BF)
    k1 = to_chips("w_pl", "w_pl_gate")
    grad("ffn2_w_down", _ffn_bwd_wd("ffn2_dwd", a, dhb, comm=[k1]))
    done(k1)
    s2 = to_sibling("ffn2_w_down")
    dhg, dhu = _ffn_bwd_act("ffn2_dact", dhb, w["ffn2_w_down"], hg, hu, comm=[s2])
    add_pairs(s2)
    k2 = to_chips("ffn2_w_down")
    dwg, dwu = _ffn_bwd_wup("ffn2_dwup", xn, dhg, dhu, comm=[k2])
    done(k2)
    grad("ffn2_w_gate", dwg)
    grad("ffn2_w_up", dwu)
    s3 = to_sibling("ffn2_w_gate", "ffn2_w_up")
    dxn = _ffn_bwd_x("ffn2_dx", dhg, dhu, w["ffn2_w_gate"], w["ffn2_w_up"], comm=[s3])
    add_pairs(s3)
    dh2, dsp["ffn2_norm"] = _rms_bwd("ffn2_dnorm", dxn, h2, sp["ffn2_norm"], dh3)

    dh2b = dh2.astype(BF)
    grad("w_out", _row_dw("out_dw", merged, dh2b))
    s4 = to_sibling("w_out")
    dya, dyb, dga, dgb = _mix_out_bwd(dh2b, w["w_out"], ga, gb, ya, yb, comm=[s4])
    add_pairs(s4)
    grad("w_branch_a", _grp_dw("bra_dw", oa, dya))
    grad("w_branch_b", _grp_dw("brb_dw", ob, dyb))
    doa = _grp_dx("bra_dx", dya, w["w_branch_a"]).astype(BF)
    s5 = to_sibling("w_branch_a", "w_branch_b")
    dob = _grp_dx("brb_dx", dyb, w["w_branch_b"], comm=[s5]).astype(BF)
    add_pairs(s5)

    k3 = to_chips("ffn2_w_gate", "w_out")
    dqf, dkf, dvf = _mla_bwd(qf, kf, vf, dob, comm=[k3])
    done(k3)
    dqp, dkv, dkr = _mla_post(dqf, dkf, dvf, cos, sin, rot_t)
    grad("w_uq", _grp_dw_t("uq_dw", dqp, cq))
    grad("w_ukv", _grp_dw("ukv_dw", ckv, dkv))
    dcq = _grp_dx_t("uq_dx", dqp, w["w_uq"])
    s6 = to_sibling("w_uq", "w_ukv")
    dckv = _grp_dx("ukv_dx", dkv, w["w_ukv"], comm=[s6])
    add_pairs(s6)
    dlat, dsp["q_a_norm"], dsp["kv_a_norm"] = _mla_lat_bwd(dcq, dckv, dkr, lat, sp["q_a_norm"], sp["kv_a_norm"],
                                                         cos, sin, rot_t)
    k4 = to_chips("ffn2_w_up", "w_branch_a", "w_branch_b")
    dq_na, dk_na, dv_na, dtab = _na_bwd(qkv, tb, doa, comm=[k4])
    done(k4)
    dsp["na_rpb"] = _na_rpb_grad(dtab, rows)
    dqkv = jnp.concatenate([dq_na, dk_na.astype(BF), dv_na.astype(BF)], axis=1)

    pieces = [dqkv, dlat, dga, dgb]
    dwin = jnp.zeros((sum(pc.shape[1] for pc in pieces), d), BF)
    row0 = 0
    for i, pc in enumerate(pieces):
        dwin = _mm_tn_into("in_dw%d" % i, pc, u, dwin, row0)
        row0 += pc.shape[1]
    grad("w_in", dwin.reshape(NDEV, -1, d))
    s7 = to_sibling("w_in")
    k5 = to_chips("w_uq", "w_ukv")
    du = _in_proj_bwd_x(pieces, [wqkv, wlat, wga, wgb], comm=[s7, k5])
    add_pairs(s7)
    done(k5)
    dh1, dsp["mix_norm"] = _rms_bwd("mix_dnorm", du, h1, sp["mix_norm"], dh2)

    xn, hg, hu, a = ffn1_saved
    dhb = dh1.astype(BF)
    k6 = to_chips("w_in")
    grad("ffn1_w_down", _ffn_bwd_wd("ffn1_dwd", a, dhb, comm=[k6]))
    done(k6)
    s8 = to_sibling("ffn1_w_down")
    dhg, dhu = _ffn_bwd_act("ffn1_dact", dhb, w["ffn1_w_down"], hg, hu, comm=[s8])
    add_pairs(s8)
    k7 = to_chips("ffn1_w_down")
    dwg, dwu = _ffn_bwd_wup("ffn1_dwup", xn, dhg, dhu, comm=[k7])
    done(k7)
    grad("ffn1_w_gate", dwg)
    grad("ffn1_w_up", dwu)
    s9 = to_sibling("ffn1_w_gate", "ffn1_w_up")
    _comm_only("rs_sibling_ffn1", [s9])
    add_pairs(s9)
    k8 = to_chips("ffn1_w_gate", "ffn1_w_up")
    dxn = _ffn_bwd_x("ffn1_dx", dhg, dhu, w["ffn1_w_gate"], w["ffn1_w_up"], comm=[k8])
    done(k8)
    dx, dsp["ffn1_norm"] = _rms_bwd("ffn1_dnorm", dxn, x, sp["ffn1_norm"], dh1)
    return loss, dx, chip_parts, dsp


def _gather_small(buf):
    def body(in_ref, out_ref, send_sems, recv_sems, local_sem):
        x, y, c = _coords()
        mine = pltpu.make_async_copy(in_ref, out_ref.at[4 * x + 2 * y + c], local_sem)
        mine.start()
        cps = []
        for k in range(1, NDEV):
            fx, fy, fc = (k >> 2) & 1, (k >> 1) & 1, k & 1
            peer = (x ^ fx, y ^ fy, c ^ fc)
            cps.append(pltpu.make_async_remote_copy(
                src_ref=in_ref, dst_ref=out_ref.at[4 * x + 2 * y + c], send_sem=send_sems.at[k - 1],
                recv_sem=recv_sems.at[k - 1], device_id=peer, device_id_type=MESH))
        for cp in cps:
            cp.start()
        for k in range(1, NDEV):
            fx, fy, fc = (k >> 2) & 1, (k >> 1) & 1, k & 1
            px, py, pc = x ^ fx, y ^ fy, c ^ fc
            pltpu.make_async_remote_copy(
                src_ref=in_ref, dst_ref=out_ref.at[4 * px + 2 * py + pc], send_sem=send_sems.at[k - 1],
                recv_sem=recv_sems.at[k - 1], device_id=(px, py, pc), device_id_type=MESH).wait_recv()
        for cp in cps:
            cp.wait_send()
        mine.wait()

    return pl.pallas_call(
        body, name="gather_small", in_specs=[ANY], out_specs=ANY,
        out_shape=jax.ShapeDtypeStruct((NDEV,) + buf.shape, buf.dtype),
        scratch_shapes=[pltpu.SemaphoreType.DMA((NDEV - 1,)), pltpu.SemaphoreType.DMA((NDEV - 1,)),
                        pltpu.SemaphoreType.DMA],
    )(buf)


def _adam_math(wv, g, m, v):
    m_new = B1 * m + (1.0 - B1) * g
    v_new = B2 * v + (1.0 - B2) * (g * g)
    m_hat = m_new / (1.0 - B1 ** STEP)
    v_hat = v_new / (1.0 - B2 ** STEP)
    return -LR * (m_hat / (jnp.sqrt(v_hat) + ADAM_EPS) + WD * wv), m_new, v_new


def _adam(name, parts, wv, m, v):
    npart, r, c = parts.shape
    tr, tc = _ew_tile(r, c)

    def body(p_ref, w_ref, m_ref, v_ref, g_ref, d_ref, mo_ref, vo_ref):
        g = p_ref[0].astype(F32)
        for j in range(1, npart):
            g = g + p_ref[j].astype(F32)
        g_ref[...] = g
        d_ref[...], mo_ref[...], vo_ref[...] = _adam_math(w_ref[...], g, m_ref[...], v_ref[...])

    blk = pl.BlockSpec((tr, tc), lambda i, k: (i, k))
    return pl.pallas_call(
        body, name=name, grid=(r // tr, c // tc),
        in_specs=[pl.BlockSpec((npart, tr, tc), lambda i, k: (0, i, k)), blk, blk, blk],
        out_specs=[blk] * 4, out_shape=[jax.ShapeDtypeStruct((r, c), F32)] * 4, compiler_params=_params(2),
    )(parts, wv, m, v)


SHARDED = ("ffn1_w_gate", "ffn1_w_up", "ffn1_w_down", "w_in", "w_uq", "w_ukv", "w_branch_a", "w_branch_b", "w_out",
           "ffn2_w_gate", "ffn2_w_up", "ffn2_w_down", "w_pl", "w_pl_gate")
TRANSPOSED = ("ffn1_w_gate", "ffn1_w_up", "ffn2_w_gate", "ffn2_w_up", "w_in", "w_uq")
REPLICATED = ("ffn1_norm", "mix_norm", "q_a_norm", "kv_a_norm", "na_rpb", "ffn2_norm", "pl_norm", "final_norm")
WEIGHTS = ("ffn1_norm", "ffn1_w_gate", "ffn1_w_up", "ffn1_w_down", "mix_norm", "w_in", "q_a_norm", "w_uq",
           "kv_a_norm", "w_ukv", "na_rpb", "w_branch_a", "w_branch_b", "w_out", "ffn2_norm", "ffn2_w_gate",
           "ffn2_w_up", "ffn2_w_down", "pl_norm", "w_pl", "w_pl_gate", "final_norm")
SMALL_W = 2048


def _pack_small(vals):
    rows = []
    for name in REPLICATED:
        flat = vals[name].reshape(-1).astype(F32)
        n = -(-flat.shape[0] // SMALL_W) * SMALL_W
        rows.append(jnp.pad(flat, (0, n - flat.shape[0])).reshape(-1, SMALL_W))
    return jnp.concatenate(rows, axis=0)


def _unpack_small(buf, shapes):
    out, r = {}, 0
    for name in REPLICATED:
        size = int(np.prod(shapes[name]))
        nrow = -(-size // SMALL_W)
        out[name] = buf[r:r + nrow].reshape(-1)[:size].reshape(shapes[name])
        r += nrow
    return out


def kernel(x, p, ffn1_norm, ffn1_w_gate, ffn1_w_up, ffn1_w_down, mix_norm, w_in, q_a_norm, w_uq, kv_a_norm, w_ukv, na_rpb, w_branch_a, w_branch_b, w_out, ffn2_norm, ffn2_w_gate, ffn2_w_up, ffn2_w_down, pl_norm, w_pl, w_pl_gate, final_norm, loss_target, m_ffn1_norm, m_ffn1_w_gate, m_ffn1_w_up, m_ffn1_w_down, m_mix_norm, m_w_in, m_q_a_norm, m_w_uq, m_kv_a_norm, m_w_ukv, m_na_rpb, m_w_branch_a, m_w_branch_b, m_w_out, m_ffn2_norm, m_ffn2_w_gate, m_ffn2_w_up, m_ffn2_w_down, m_pl_norm, m_w_pl, m_w_pl_gate, m_final_norm, v_ffn1_norm, v_ffn1_w_gate, v_ffn1_w_up, v_ffn1_w_down, v_mix_norm, v_w_in, v_q_a_norm, v_w_uq, v_kv_a_norm, v_w_ukv, v_na_rpb, v_w_branch_a, v_w_branch_b, v_w_out, v_ffn2_norm, v_ffn2_w_gate, v_ffn2_w_up, v_ffn2_w_down, v_pl_norm, v_w_pl, v_w_pl_gate, v_final_norm):
    args = dict(locals())
    wts = {n: args[n] for n in WEIGHTS}
    mom = {n: args["m_" + n] for n in WEIGHTS}
    var = {n: args["v_" + n] for n in WEIGHTS}
    shapes = {n: wts[n].shape for n in WEIGHTS}
    core = lax.axis_index("c").astype(jnp.int32).reshape(1)

    local = lambda n, a: a[0].T if n in TRANSPOSED else a[0]
    own = {n: local(n, wts[n]).astype(BF) for n in SHARDED}
    sp = {n: wts[n].reshape(1, -1) for n in REPLICATED if n != "na_rpb"}
    sp["na_rpb"] = wts["na_rpb"][0]
    loss_part, grad_x, chip_parts, dsp = _device_step(x[0], p[0, 0], loss_target[0], sp, own, core)

    out = {}
    for n in SHARDED:
        res4 = _adam("adam_" + n, chip_parts[n], local(n, wts[n]), local(n, mom[n]), local(n, var[n]))
        out[n] = tuple((a.T if n in TRANSPOSED else a)[None] for a in res4)

    small = jnp.concatenate([_pack_small(dsp), jnp.pad(loss_part, ((0, 0), (0, SMALL_W - loss_part.shape[1])))], 0)
    pad_rows = -small.shape[0] % 8
    small = jnp.pad(small, ((0, pad_rows), (0, 0)))
    every = _gather_small(small)
    zeros = jnp.zeros((1 + pad_rows, SMALL_W), F32)
    pack = lambda d: jnp.concatenate([_pack_small(d), zeros], 0)
    g_s, d_s, m_s, v_s = _adam("adam_small", every, pack(wts), pack(mom), pack(var))
    n_rows = small.shape[0] - 1 - pad_rows
    loss = g_s[n_rows, 0]
    small_out = [_unpack_small(b, shapes) for b in (g_s, d_s, m_s, v_s)]
    for n in REPLICATED:
        out[n] = tuple(b[n] for b in small_out)

    res = [loss, grad_x[None]]
    for k in range(4):
        res += [out[n][k] for n in WEIGHTS]
    return tuple(res)
```

```python
import functools

import numpy as np
import jax
import jax.numpy as jnp
from jax import lax
from jax.experimental import pallas as pl
from jax.experimental.pallas import tpu as pltpu

F32 = jnp.float32
BF = jnp.bfloat16
MESH = pl.DeviceIdType.MESH

NDEV = 8
NCHIP = 4
VMEM_LIMIT = 56 * 1024 * 1024
EPS = 1e-6
NEG = -1e30
GRID_W = 64
NA_HEADS, NA_DIM = 8, 128
NA_ROWS_WIN, NA_COLS_WIN = 8, 16
NA_HG = 4
ML_HEADS, ML_NOPE, ML_ROPE, ML_V = 8, 128, 64, 128
ML_QK = ML_NOPE + ML_ROPE
ML_RANK = 512
ROPE_THETA = 10000.0
LR, B1, B2, ADAM_EPS, WD, STEP = 0.001, 0.9, 0.999, 1e-08, 0.01, 10
HI = lax.Precision.HIGHEST

_DN = {"nn": (((1,), (0,)), ((), ())), "nt": (((1,), (1,)), ((), ())), "tn": (((0,), (0,)), ((), ()))}


def _params(n):
    return pltpu.CompilerParams(dimension_semantics=("arbitrary",) * n, vmem_limit_bytes=VMEM_LIMIT)


def _sig(v):
    return jax.nn.sigmoid(v)


ANY = pl.BlockSpec(memory_space=pl.ANY)


def _coords():
    return lax.axis_index("x"), lax.axis_index("y"), lax.axis_index("c")


class _Part:
    inputs, out_shapes, sem_shapes, results = (), (), (), None

    def mid(self, ins, outs, sems):
        pass


class _GatherPart(_Part):
    def __init__(self, names, shards):
        n = len(shards)
        self.names, self.inputs = list(names), list(shards)
        self.out_shapes = [jax.ShapeDtypeStruct((NDEV,) + a.shape, a.dtype) for a in shards]
        self.sem_shapes = [pltpu.SemaphoreType.DMA((n, 7)), pltpu.SemaphoreType.DMA((n, 7)),
                           pltpu.SemaphoreType.DMA((n,))]

    def _plan(self, ins, outs, sems):
        send_sems, recv_sems, local_sems = sems
        x, y, c = _coords()
        me, sib, diag = (x, y, c), (x, y, 1 - c), (1 - x, 1 - y, c)
        n1, n2 = (x ^ (1 - c), y ^ c, c), (x ^ c, y ^ (1 - c), c)

        def copy(i, k, block, to, src=None):
            px, py, pc = block
            dst = outs[i].at[4 * px + 2 * py + pc]
            return pltpu.make_async_remote_copy(
                src_ref=dst if src is None else src, dst_ref=dst, send_sem=send_sems.at[i, k],
                recv_sem=recv_sems.at[i, k], device_id=to, device_id_type=MESH)

        mine = [pltpu.make_async_copy(ins[i], outs[i].at[4 * x + 2 * y + c], local_sems.at[i])
                for i in range(len(ins))]
        return copy, mine, me, sib, n1, n2, diag

    def _own_sends(self, ins, copy, me, sib, n1, n2):
        return [copy(i, k, me, to, src=ins[i]) for i in range(len(ins)) for k, to in enumerate((sib, n1, n2))]

    def start(self, ins, outs, sems):
        copy, mine, me, sib, n1, n2, _ = self._plan(ins, outs, sems)
        for cp in mine + self._own_sends(ins, copy, me, sib, n1, n2):
            cp.start()

    def mid(self, ins, outs, sems):
        copy, _, me, sib, n1, n2, _ = self._plan(ins, outs, sems)
        for i in range(len(ins)):
            copy(i, 1, n1, me).wait_recv()
            copy(i, 3, n1, n2).start()
            copy(i, 4, n1, sib).start()

    def finish(self, ins, outs, sems):
        copy, mine, me, sib, n1, n2, diag = self._plan(ins, outs, sems)
        other = lambda dev: (dev[0], dev[1], sib[2])
        n = len(ins)
        for i in range(n):
            copy(i, 2, n2, me).wait_recv()
            copy(i, 5, n2, sib).start()
        for i in range(n):
            copy(i, 3, diag, me).wait_recv()
            copy(i, 6, diag, sib).start()
        for i in range(n):
            copy(i, 0, sib, me).wait_recv()
            for k, block in ((4, other(n2)), (5, other(n1)), (6, other(diag))):
                copy(i, k, block, me).wait_recv()
        for cp in self._own_sends(ins, copy, me, sib, n1, n2):
            cp.wait_send()
        for i in range(n):
            for k, block in ((3, n1), (4, n1), (5, n2), (6, diag)):
                copy(i, k, block, sib).wait_send()
        for cp in mine:
            cp.wait()


class _SiblingPart(_Part):
    def __init__(self, names, parts):
        n = len(parts)
        self.names, self.inputs = list(names), list(parts)
        self.out_shapes = [jax.ShapeDtypeStruct((NCHIP,) + a.shape[2:], a.dtype) for a in parts]
        self.sem_shapes = [pltpu.SemaphoreType.DMA((n,)), pltpu.SemaphoreType.DMA((n,))]

    def _copies(self, ins, outs, sems):
        x, y, c = _coords()
        return [pltpu.make_async_remote_copy(
            src_ref=ins[i].at[:, 1 - c], dst_ref=outs[i], send_sem=sems[0].at[i], recv_sem=sems[1].at[i],
            device_id=(x, y, 1 - c), device_id_type=MESH) for i in range(len(ins))]

    def start(self, ins, outs, sems):
        for cp in self._copies(ins, outs, sems):
            cp.start()

    def finish(self, ins, outs, sems):
        cps = self._copies(ins, outs, sems)
        for cp in cps:
            cp.wait_recv()
        for cp in cps:
            cp.wait_send()


class _ChipsPart(_Part):
    def __init__(self, names, sums):
        n = len(sums)
        self.names, self.inputs = list(names), list(sums)
        self.out_shapes = [jax.ShapeDtypeStruct(a.shape, a.dtype) for a in sums]
        self.sem_shapes = [pltpu.SemaphoreType.DMA((n, 3)), pltpu.SemaphoreType.DMA((n, 3)),
                           pltpu.SemaphoreType.DMA((n,))]

    def _plan(self, ins, outs, sems):
        send_sems, recv_sems, local_sems = sems
        x, y, c = _coords()
        my_chip = 2 * x + y
        chips = [(1 - x, y), (x, 1 - y), (1 - x, 1 - y)]
        n = len(ins)
        mine = [pltpu.make_async_copy(ins[i].at[my_chip], outs[i].at[my_chip], local_sems.at[i]) for i in range(n)]
        sends, recvs = [], []
        for i in range(n):
            for k, (px, py) in enumerate(chips):
                sends.append(pltpu.make_async_remote_copy(
                    src_ref=ins[i].at[2 * px + py], dst_ref=outs[i].at[my_chip], send_sem=send_sems.at[i, k],
                    recv_sem=recv_sems.at[i, k], device_id=(px, py, c), device_id_type=MESH))
                recvs.append(pltpu.make_async_remote_copy(
                    src_ref=ins[i].at[my_chip], dst_ref=outs[i].at[2 * px + py], send_sem=send_sems.at[i, k],
                    recv_sem=recv_sems.at[i, k], device_id=(px, py, c), device_id_type=MESH))
        return mine, sends, recvs

    def start(self, ins, outs, sems):
        mine, sends, _ = self._plan(ins, outs, sems)
        for cp in mine + sends:
            cp.start()

    def finish(self, ins, outs, sems):
        mine, sends, recvs = self._plan(ins, outs, sems)
        for cp in recvs:
            cp.wait_recv()
        for cp in sends:
            cp.wait_send()
        for cp in mine:
            cp.wait()


def _call(name, body, grid, in_specs, out_specs, out_shape, args, comm=()):
    comm = [p for p in comm if p is not None]
    single = not isinstance(out_shape, (list, tuple))
    o_specs = [out_specs] if single else list(out_specs)
    o_shape = [out_shape] if single else list(out_shape)
    n_in, n_out = len(in_specs), len(o_specs)
    c_in = [a for p in comm for a in p.inputs]
    c_out = [s for p in comm for s in p.out_shapes]
    c_sem = [s for p in comm for s in p.sem_shapes]

    def wrapped(*refs):
        ins, outs = refs[:n_in], refs[n_in + len(c_in):n_in + len(c_in) + n_out]
        pos = [n_in, n_in + len(c_in) + n_out, n_in + len(c_in) + n_out + len(c_out)]
        split = []
        for p in comm:
            sizes = [len(p.inputs), len(p.out_shapes), len(p.sem_shapes)]
            split.append([refs[o:o + n] for o, n in zip(pos, sizes)])
            pos = [o + n for o, n in zip(pos, sizes)]
        step, steps = 0, 1
        for a, g in enumerate(grid):
            step, steps = step * g + pl.program_id(a), steps * g

        def run(which, at):
            def go():
                for p, cut in zip(comm, split):
                    getattr(p, which)(*cut)
            if not comm:
                return
            if grid:
                pl.when(step == at)(go)
            else:
                go()

        run("start", 0)
        if steps == 1:
            run("mid", 0)
        body(*ins, *outs)
        if steps > 1:
            run("mid", steps // 2)
        run("finish", steps - 1)

    res = pl.pallas_call(
        wrapped, name=name, grid=grid, in_specs=list(in_specs) + [ANY] * len(c_in),
        out_specs=o_specs + [ANY] * len(c_out), out_shape=o_shape + c_out, scratch_shapes=c_sem,
        compiler_params=_params(len(grid)),
    )(*args, *c_in)
    pos = n_out
    for p in comm:
        p.results = list(res[pos:pos + len(p.out_shapes)])
        pos += len(p.out_shapes)
    return res[0] if single else list(res[:n_out])


def _comm_only(name, comm):
    def body(o_ref):
        o_ref[...] = jnp.zeros_like(o_ref)

    _call(name, body, (), [], pl.BlockSpec(memory_space=pltpu.VMEM), jax.ShapeDtypeStruct((8, 128), F32), [], comm)


def _mm(name, grid, prods, extras, outs, epi, nacc=1, comm=()):
    n_p, n_e = len(prods), len(extras)

    def body(*refs):
        ab, ex, out = refs[:2 * n_p], refs[2 * n_p:2 * n_p + n_e], refs[2 * n_p + n_e:]
        accs = [None] * nacc
        for i, prod in enumerate(prods):
            dn, acc, loop = prod[6], prod[7], prod[8]
            a_ref, b_ref = ab[2 * i], ab[2 * i + 1]
            if loop:
                for g in range(loop):
                    t = lax.dot_general(a_ref[g], b_ref[g], _DN[dn], preferred_element_type=F32)
                    accs[acc] = t if accs[acc] is None else accs[acc] + t
            else:
                t = lax.dot_general(a_ref[...], b_ref[...], _DN[dn], preferred_element_type=F32)
                accs[acc] = t if accs[acc] is None else accs[acc] + t
        epi(accs, ex, out)

    in_specs, args = [], []
    for prod in prods:
        in_specs += [pl.BlockSpec(prod[1], prod[2]), pl.BlockSpec(prod[4], prod[5])]
        args += [prod[0], prod[3]]
    for e, e_blk, e_map in extras:
        in_specs.append(pl.BlockSpec(e_blk, e_map))
        args.append(e)
    return _call(name, body, grid, in_specs, [pl.BlockSpec(blk, mp) for _, _, blk, mp in outs],
                 [jax.ShapeDtypeStruct(s, d) for s, d, _, _ in outs], args, comm)


def _store(accs, ex, out):
    out[0][...] = accs[0].astype(out[0].dtype)


def _ew_tile(r, c, budget=3 << 19):
    for t in range(r - r % 16, 0, -16):
        if r % t == 0 and t * c * 4 <= budget:
            return t, c
    for t in range(c - c % 128, 0, -128):
        if c % t == 0 and r * t * 4 <= budget:
            return r, t
    return r, c


def _tile(n, want):
    t = min(n, want)
    assert n % t == 0, (n, want)
    return t


def _mm_nn(name, a, b, out_dtype, tm=512, tn=512, comm=()):
    m, k = a.shape
    n = b.shape[1]
    tm, tn = _tile(m, tm), (tn if n % tn == 0 else n)
    return _mm(name, (n // tn, m // tm),
               [(a, (tm, k), lambda j, i: (i, 0), b, (k, tn), lambda j, i: (0, j), "nn", 0, 0)], [],
               [((m, n), out_dtype, (tm, tn), lambda j, i: (i, j))], _store, comm=comm)[0]


def _mm_nt(name, a, bt, out_dtype, tm=512, tn=512, comm=()):
    m, k = a.shape
    n = bt.shape[0]
    tm, tn = _tile(m, tm), (tn if n % tn == 0 else n)
    return _mm(name, (n // tn, m // tm),
               [(a, (tm, k), lambda j, i: (i, 0), bt, (tn, k), lambda j, i: (j, 0), "nt", 0, 0)], [],
               [((m, n), out_dtype, (tm, tn), lambda j, i: (i, j))], _store, comm=comm)[0]


def _mm_tn_into(name, a, b, buf, row0, ta=512, tb=512):
    t, ka = a.shape
    nb = b.shape[1]
    ta, tb = (ta if ka % ta == 0 else ka), (tb if nb % tb == 0 else nb)

    def body(a_ref, b_ref, buf_in, buf_out, tile, sem):
        i, j = pl.program_id(0), pl.program_id(1)
        tile[...] = lax.dot_general(a_ref[...], b_ref[...], _DN["tn"], preferred_element_type=F32).astype(tile.dtype)
        rows = pl.ds(pl.multiple_of(row0 + i * ta, 16), ta)
        cp = pltpu.make_async_copy(tile, buf_out.at[rows, pl.ds(pl.multiple_of(j * tb, 128), tb)], sem)
        cp.start()
        cp.wait()

    return pl.pallas_call(
        body, name=name, grid=(ka // ta, nb // tb),
        in_specs=[pl.BlockSpec((t, ta), lambda i, j: (0, i)), pl.BlockSpec((t, tb), lambda i, j: (0, j)), ANY],
        out_specs=ANY, out_shape=jax.ShapeDtypeStruct(buf.shape, buf.dtype), input_output_aliases={2: 0},
        scratch_shapes=[pltpu.VMEM((ta, tb), buf.dtype), pltpu.SemaphoreType.DMA],
        compiler_params=_params(2))(a, b, buf)


def _mm_tn(name, a, b, out_dtype, ta=512, tb=512, scale=None):
    t, ka = a.shape
    nb = b.shape[1]
    ta, tb = (ta if ka % ta == 0 else ka), (tb if nb % tb == 0 else nb)

    def epi(accs, ex, out):
        v = accs[0] if scale is None else accs[0] * scale
        out[0][...] = v.astype(out[0].dtype)

    return _mm(name, (ka // ta, nb // tb),
               [(a, (t, ta), lambda i, j: (0, i), b, (t, tb), lambda i, j: (0, j), "tn", 0, 0)], [],
               [((ka, nb), out_dtype, (ta, tb), lambda i, j: (i, j))], epi)[0]


def _rms_fwd(name, x, g, tm=256):
    s, d = x.shape
    tm = _tile(s, tm)

    def body(x_ref, g_ref, o_ref):
        v = x_ref[...]
        o_ref[...] = (v * lax.rsqrt(jnp.mean(v * v, axis=-1, keepdims=True) + EPS) * g_ref[...]).astype(o_ref.dtype)

    return pl.pallas_call(
        body, name=name, grid=(s // tm,),
        in_specs=[pl.BlockSpec((tm, d), lambda i: (i, 0)), pl.BlockSpec((1, d), lambda i: (0, 0))],
        out_specs=pl.BlockSpec((tm, d), lambda i: (i, 0)), out_shape=jax.ShapeDtypeStruct((s, d), BF),
        compiler_params=_params(1))(x, g)


def _acc_rows(ref, part, i):
    @pl.when(i == 0)
    def _():
        ref[...] = part

    @pl.when(i > 0)
    def _():
        ref[...] += part


def _rms_bwd_math(dn, v, g):
    rstd = lax.rsqrt(jnp.mean(v * v, axis=-1, keepdims=True) + EPS)
    xh = v * rstd
    dxh = dn * g
    dx = rstd * (dxh - xh * jnp.mean(dxh * xh, axis=-1, keepdims=True))
    return dx, jnp.sum(dn * xh, axis=0, keepdims=True)


def _rms_bwd(name, dn, x, g, resid, tm=256):
    s, d = x.shape
    tm = _tile(s, tm)

    def body(dn_ref, x_ref, g_ref, r_ref, dx_ref, dg_ref):
        dx, part = _rms_bwd_math(dn_ref[...].astype(F32), x_ref[...], g_ref[...])
        dx_ref[...] = r_ref[...] + dx
        _acc_rows(dg_ref, part, pl.program_id(0))

    row = pl.BlockSpec((tm, d), lambda i: (i, 0))
    one = pl.BlockSpec((1, d), lambda i: (0, 0))
    return pl.pallas_call(
        body, name=name, grid=(s // tm,), in_specs=[row, row, one, row], out_specs=[row, one],
        out_shape=[jax.ShapeDtypeStruct((s, d), F32), jax.ShapeDtypeStruct((1, d), F32)],
        compiler_params=_params(1))(dn, x, g, resid)


def _loss_head(h, target, g, tm=256):
    s, d = h.shape
    tm = _tile(s, tm)

    def body(h_ref, t_ref, g_ref, dh_ref, dg_ref, loss_ref):
        v, gv = h_ref[...], g_ref[...]
        rstd = lax.rsqrt(jnp.mean(v * v, axis=-1, keepdims=True) + EPS)
        xh = v * rstd
        err = xh * gv - t_ref[...]
        part_loss = 0.5 * jnp.sum(jnp.mean(err * err, axis=-1, keepdims=True), axis=0, keepdims=True)
        dy = err * (1.0 / d)
        dxh = dy * gv
        dh_ref[...] = rstd * (dxh - xh * jnp.mean(dxh * xh, axis=-1, keepdims=True))
        i = pl.program_id(0)
        _acc_rows(dg_ref, jnp.sum(dy * xh, axis=0, keepdims=True), i)
        _acc_rows(loss_ref, jnp.broadcast_to(part_loss, loss_ref.shape), i)

    row = pl.BlockSpec((tm, d), lambda i: (i, 0))
    one = pl.BlockSpec((1, d), lambda i: (0, 0))
    return pl.pallas_call(
        body, name="loss_head", grid=(s // tm,), in_specs=[row, row, one],
        out_specs=[row, one, pl.BlockSpec((1, 128), lambda i: (0, 0))],
        out_shape=[jax.ShapeDtypeStruct((s, d), F32), jax.ShapeDtypeStruct((1, d), F32),
                   jax.ShapeDtypeStruct((1, 128), F32)],
        compiler_params=_params(1))(h, target, g)


def _pl_bwd_elem(dh, pe, t, tm=256):
    s, d = dh.shape
    tm = _tile(s, tm)

    def body(dh_ref, pe_ref, t_ref, dt_ref, dpe_ref):
        dh_v, sg = dh_ref[...], _sig(t_ref[...])
        dt_ref[...] = (dh_v * pe_ref[...].astype(F32) * sg * (1.0 - sg)).astype(BF)
        dpe_ref[...] = (dh_v * sg).astype(BF)

    row = pl.BlockSpec((tm, d), lambda i: (i, 0))
    return pl.pallas_call(
        body, name="pl_bwd_elem", grid=(s // tm,), in_specs=[row, row, row], out_specs=[row, row],
        out_shape=[jax.ShapeDtypeStruct((s, d), BF)] * 2, compiler_params=_params(1))(dh, pe, t)


def _ffn_up(name, xn, wg, wu, tm=512, comm=()):
    s, d = xn.shape
    g, fb, _ = wg.shape
    tm = _tile(s, tm)

    def epi(accs, ex, out):
        hg, hu = accs
        out[0][...] = hg.astype(BF)
        out[1][...] = hu.astype(BF)
        out[2][...] = (hg * _sig(hg) * hu).astype(BF)

    a_map = lambda j, i: (i, 0)
    w_map = lambda j, i: (j, 0, 0)
    o = ((g, s, fb), BF, (None, tm, fb), lambda j, i: (j, i, 0))
    return _mm(name, (g, s // tm),
               [(xn, (tm, d), a_map, wg, (None, fb, d), w_map, "nt", 0, 0),
                (xn, (tm, d), a_map, wu, (None, fb, d), w_map, "nt", 1, 0)], [], [o, o, o], epi, nacc=2, comm=comm)


def _ffn_down(name, a, wd, resid, tm=512, tn=512, comm=()):
    g, s, fb = a.shape
    d = wd.shape[2]
    tm, tn = _tile(s, tm), _tile(d, tn)

    def epi(accs, ex, out):
        out[0][...] = ex[0][...] + 0.5 * accs[0]

    return _mm(name, (d // tn, s // tm),
               [(a, (g, tm, fb), lambda j, i: (0, i, 0), wd, (g, fb, tn), lambda j, i: (0, 0, j), "nn", 0, g)],
               [(resid, (tm, tn), lambda j, i: (i, j))],
               [((s, d), F32, (tm, tn), lambda j, i: (i, j))], epi, comm=comm)[0]


def _ffn_bwd_act(name, dh, wd, hg, hu, tm=512, comm=()):
    s, d = dh.shape
    g, fb, _ = wd.shape
    tm = _tile(s, tm)

    def epi(accs, ex, out):
        da = 0.5 * accs[0]
        hg_v, hu_v = ex[0][...].astype(F32), ex[1][...].astype(F32)
        sg = _sig(hg_v)
        out[0][...] = (da * hu_v * (sg * (1.0 + hg_v * (1.0 - sg)))).astype(BF)
        out[1][...] = (da * (hg_v * sg)).astype(BF)

    blk = (None, tm, fb)
    gmap = lambda j, i: (j, i, 0)
    return _mm(name, (g, s // tm),
               [(dh, (tm, d), lambda j, i: (i, 0), wd, (None, fb, d), lambda j, i: (j, 0, 0), "nt", 0, 0)],
               [(hg, blk, gmap), (hu, blk, gmap)],
               [((g, s, fb), BF, blk, gmap), ((g, s, fb), BF, blk, gmap)], epi, comm=comm)


def _ffn_bwd_wd(name, a, dh, tn=512, comm=()):
    g, s, fb = a.shape
    d = dh.shape[1]
    tn = _tile(d, tn)

    def epi(accs, ex, out):
        out[0][...] = (0.5 * accs[0]).astype(BF)

    return _mm(name, (g, d // tn),
               [(a, (None, s, fb), lambda j, i: (j, 0, 0), dh, (s, tn), lambda j, i: (0, i), "tn", 0, 0)], [],
               [((g, fb, d), BF, (None, fb, tn), lambda j, i: (j, 0, i))], epi, comm=comm)[0]


def _ffn_bwd_wup(name, xn, dhg, dhu, tk=512, comm=()):
    s, d = xn.shape
    g, _, fb = dhg.shape
    tk = _tile(d, tk)

    def epi(accs, ex, out):
        out[0][...] = accs[0].astype(BF)
        out[1][...] = accs[1].astype(BF)

    a_map = lambda j, i: (j, 0, 0)
    b_map = lambda j, i: (0, i)
    o = ((g, fb, d), BF, (None, fb, tk), lambda j, i: (j, 0, i))
    return _mm(name, (g, d // tk),
               [(dhg, (None, s, fb), a_map, xn, (s, tk), b_map, "tn", 0, 0),
                (dhu, (None, s, fb), a_map, xn, (s, tk), b_map, "tn", 1, 0)], [], [o, o], epi, nacc=2, comm=comm)


def _ffn_bwd_x(name, dhg, dhu, wg, wu, tm=512, tn=512, comm=()):
    g, s, fb = dhg.shape
    d = wg.shape[2]
    tm, tn = _tile(s, tm), _tile(d, tn)
    a_blk, a_map = (g, tm, fb), lambda j, i: (0, i, 0)
    b_blk, b_map = (g, fb, tn), lambda j, i: (0, 0, j)
    return _mm(name, (d // tn, s // tm),
               [(dhg, a_blk, a_map, wg, b_blk, b_map, "nn", 0, g), (dhu, a_blk, a_map, wu, b_blk, b_map, "nn", 0, g)],
               [], [((s, d), F32, (tm, tn), lambda j, i: (i, j))], _store, comm=comm)[0]


def _ffn_forward(tag, h, gain, wg, wu, get_wd, up_comm=(), down_comm=()):
    xn = _rms_fwd(tag + "_norm", h, gain)
    hg, hu, a = _ffn_up(tag + "_up", xn, wg, wu, comm=up_comm)
    return _ffn_down(tag + "_down", a, get_wd(), h, comm=down_comm), (xn, hg, hu, a)


def _na_geometry(rows):
    kh = min(NA_ROWS_WIN, rows)
    cols = np.arange(GRID_W)
    col_start = np.clip(cols - NA_COLS_WIN // 2, 0, GRID_W - NA_COLS_WIN)
    mask = (cols[None, :] >= col_start[:, None]) & (cols[None, :] < col_start[:, None] + NA_COLS_WIN)
    dc = np.clip(cols[None, :] - cols[:, None], -(NA_COLS_WIN - 1), NA_COLS_WIN - 1) + (NA_COLS_WIN - 1)
    return kh, mask, dc


def _na_bias_tables(rpb, rows):
    kh, mask, dc = _na_geometry(rows)
    t = jnp.where(jnp.asarray(mask)[None, None], rpb[:, :, dc], NEG)
    tb = jnp.stack([t[:, d0:d0 + kh] for d0 in range(NA_ROWS_WIN)], 0)
    return tb.transpose(0, 1, 3, 2, 4).reshape(NA_ROWS_WIN, NA_HEADS, GRID_W, kh * GRID_W)


def _na_row_start(r, rows, kh):
    return jnp.clip(r - kh // 2, 0, rows - kh)


def _na_specs(s, rows, kh):
    hw = NA_HG * NA_DIM
    nq = NA_HEADS // NA_HG
    q_spec = pl.BlockSpec((GRID_W, hw), lambda j, r: (r, j))
    k_spec = pl.BlockSpec((s, hw), lambda j, r: (0, nq + j))
    v_spec = pl.BlockSpec((s, hw), lambda j, r: (0, 2 * nq + j))
    b_spec = pl.BlockSpec((None, NA_HG, GRID_W, kh * GRID_W),
                          lambda j, r: (_na_row_start(r, rows, kh) - r + NA_ROWS_WIN - 1, j, 0, 0))
    return q_spec, k_spec, v_spec, b_spec, hw, nq


def _na_probs(q, k, bias):
    sc = lax.dot_general(q, k, _DN["nt"], preferred_element_type=F32) * (NA_DIM ** -0.5) + bias
    e = jnp.exp(sc - jnp.max(sc, axis=-1, keepdims=True))
    return e / jnp.sum(e, axis=-1, keepdims=True)


def _na_fwd(qkv, tb, comm=()):
    s = qkv.shape[0]
    rows = s // GRID_W
    kh = min(NA_ROWS_WIN, rows)
    q_spec, k_spec, v_spec, b_spec, hw, nq = _na_specs(s, rows, kh)

    def body(q_ref, k_ref, v_ref, b_ref, o_ref):
        r = pl.program_id(1)
        start = pl.multiple_of(_na_row_start(r, rows, kh) * GRID_W, GRID_W)
        for h in range(NA_HG):
            cs = slice(h * NA_DIM, (h + 1) * NA_DIM)
            p = _na_probs(q_ref[:, cs], k_ref[pl.ds(start, kh * GRID_W), cs], b_ref[h])
            o_ref[:, cs] = jnp.dot(p.astype(BF), v_ref[pl.ds(start, kh * GRID_W), cs],
                                   preferred_element_type=F32).astype(BF)

    return _call("na_fwd", body, (nq, rows), [q_spec, k_spec, v_spec, b_spec],
                 pl.BlockSpec((GRID_W, hw), lambda j, r: (r, j)),
                 jax.ShapeDtypeStruct((s, NA_HEADS * NA_DIM), BF), [qkv, qkv, qkv, tb], comm)


def _na_bwd(qkv, tb, do, comm=()):
    s = qkv.shape[0]
    rows = s // GRID_W
    kh = min(NA_ROWS_WIN, rows)
    q_spec, k_spec, v_spec, b_spec, hw, nq = _na_specs(s, rows, kh)
    nd = 2 * NA_ROWS_WIN - 1

    def body(q_ref, k_ref, v_ref, b_ref, do_ref, dq_ref, dk_ref, dv_ref, dt_ref):
        r = pl.program_id(1)

        @pl.when(r == 0)
        def _():
            dk_ref[...] = jnp.zeros_like(dk_ref)
            dv_ref[...] = jnp.zeros_like(dv_ref)
            dt_ref[...] = jnp.zeros_like(dt_ref)

        rs = _na_row_start(r, rows, kh)
        d0 = rs - r + NA_ROWS_WIN - 1
        win = pl.ds(pl.multiple_of(rs * GRID_W, GRID_W), kh * GRID_W)
        for h in range(NA_HG):
            cs = slice(h * NA_DIM, (h + 1) * NA_DIM)
            q, k, v, do_h = q_ref[:, cs], k_ref[win, cs], v_ref[win, cs], do_ref[:, cs]
            p = _na_probs(q, k, b_ref[h])
            dp = lax.dot_general(do_h, v, _DN["nt"], preferred_element_type=F32)
            ds = p * (dp - jnp.sum(p * dp, axis=-1, keepdims=True))
            for i in range(kh):
                dt_ref[h, d0 + i] += ds[:, i * GRID_W:(i + 1) * GRID_W]
            dsb = (ds * (NA_DIM ** -0.5)).astype(BF)
            dq_ref[:, cs] = jnp.dot(dsb, k, preferred_element_type=F32).astype(BF)
            dk_ref[win, cs] += lax.dot_general(dsb, q, _DN["tn"], preferred_element_type=F32)
            dv_ref[win, cs] += lax.dot_general(p.astype(BF), do_h, _DN["tn"], preferred_element_type=F32)

    width = NA_HEADS * NA_DIM
    whole = pl.BlockSpec((s, hw), lambda j, r: (0, j))
    return _call(
        "na_bwd", body, (nq, rows),
        [q_spec, k_spec, v_spec, b_spec, pl.BlockSpec((GRID_W, hw), lambda j, r: (r, j))],
        [pl.BlockSpec((GRID_W, hw), lambda j, r: (r, j)), whole, whole,
         pl.BlockSpec((NA_HG, nd, GRID_W, GRID_W), lambda j, r: (j, 0, 0, 0))],
        [jax.ShapeDtypeStruct((s, width), BF), jax.ShapeDtypeStruct((s, width), F32),
         jax.ShapeDtypeStruct((s, width), F32), jax.ShapeDtypeStruct((NA_HEADS, nd, GRID_W, GRID_W), F32)],
        [qkv, qkv, qkv, tb, do], comm)


def _na_rpb_grad(dt, rows):
    _, mask, dc = _na_geometry(rows)
    nd, nc = 2 * NA_ROWS_WIN - 1, 2 * NA_COLS_WIN - 1
    onehot = np.zeros((GRID_W * GRID_W, 128), np.float32)
    onehot[np.arange(GRID_W * GRID_W), dc.reshape(-1)] = mask.reshape(-1).astype(np.float32)
    flat = dt.reshape(NA_HEADS * nd, GRID_W * GRID_W)

    def body(a_ref, e_ref, o_ref):
        o_ref[...] = jnp.dot(a_ref[...], e_ref[...], precision=HI, preferred_element_type=F32)

    out = pl.pallas_call(body, name="na_rpb_grad", out_shape=jax.ShapeDtypeStruct((NA_HEADS * nd, 128), F32),
                         compiler_params=_params(0))(flat, jnp.asarray(onehot))
    return out[:, :nc].reshape(NA_HEADS, nd, nc)


def _rope_consts(s):
    pos = np.arange(s, dtype=np.float32)
    inv = (1.0 / (ROPE_THETA ** (np.arange(0, ML_ROPE, 2, dtype=np.float32) / ML_ROPE))).astype(np.float32)
    ang = pos[:, None] * inv[None, :]
    cos, sin = np.cos(ang).astype(np.float32), np.sin(ang).astype(np.float32)
    half = ML_ROPE // 2
    rot = np.zeros((ML_ROPE, ML_ROPE), np.float32)
    rot[np.arange(half) + half, np.arange(half)] = -1.0
    rot[np.arange(half), np.arange(half) + half] = 1.0
    return (jnp.asarray(np.concatenate([cos, cos], 1)), jnp.asarray(np.concatenate([sin, sin], 1)),
            jnp.asarray(rot), jnp.asarray(rot.T.copy()))


def _rope(v, cos, sin, rot):
    return v * cos + jnp.dot(v, rot, precision=HI, preferred_element_type=F32) * sin


def _unrope(dv, cos, sin, rot_t):
    return dv * cos + jnp.dot(dv * sin, rot_t, precision=HI, preferred_element_type=F32)


def _rms(v, g):
    return v * lax.rsqrt(jnp.mean(v * v, axis=-1, keepdims=True) + EPS) * g


def _mla_prep(lat, gq, gkv, cos, sin, rot, tm=256):
    s, w = lat.shape
    tm = _tile(s, tm)

    def body(l_ref, gq_ref, gkv_ref, c_ref, s_ref, r_ref, cq_ref, ckv_ref, kr_ref):
        cq_ref[...] = _rms(l_ref[:, :ML_RANK], gq_ref[...]).astype(BF)
        ckv_ref[...] = _rms(l_ref[:, ML_RANK:2 * ML_RANK], gkv_ref[...]).astype(BF)
        kr_ref[...] = _rope(l_ref[:, 2 * ML_RANK:], c_ref[...], s_ref[...], r_ref[...]).astype(BF)

    row = lambda c: pl.BlockSpec((tm, c), lambda i: (i, 0))
    full = lambda a: pl.BlockSpec(a.shape, lambda i: (0, 0))
    return pl.pallas_call(
        body, name="mla_prep", grid=(s // tm,),
        in_specs=[row(w), full(gq), full(gkv), row(ML_ROPE), row(ML_ROPE), full(rot)],
        out_specs=[row(ML_RANK), row(ML_RANK), row(ML_ROPE)],
        out_shape=[jax.ShapeDtypeStruct((s, ML_RANK), BF), jax.ShapeDtypeStruct((s, ML_RANK), BF),
                   jax.ShapeDtypeStruct((s, ML_ROPE), BF)],
        compiler_params=_params(1))(lat, gq, gkv, cos, sin, rot)


def _mla_q_proj(cq, wuq, cos, sin, rot, tm=512, comm=()):
    s, k = cq.shape
    tm = _tile(s, tm)

    def epi(accs, ex, out):
        acc = accs[0]
        out[0][:, :ML_NOPE] = acc[:, :ML_NOPE].astype(BF)
        out[0][:, ML_NOPE:] = _rope(acc[:, ML_NOPE:], ex[0][...], ex[1][...], ex[2][...]).astype(BF)

    rmap = lambda j, i: (i, 0)
    return _mm("mla_q_proj", (ML_HEADS, s // tm),
               [(cq, (tm, k), rmap, wuq, (None, ML_QK, k), lambda j, i: (j, 0, 0), "nt", 0, 0)],
               [(cos, (tm, ML_ROPE), rmap), (sin, (tm, ML_ROPE), rmap), (rot, rot.shape, lambda j, i: (0, 0))],
               [((ML_HEADS, s, ML_QK), BF, (None, tm, ML_QK), lambda j, i: (j, i, 0))], epi, comm=comm)[0]


def _mla_kv_proj(ckv, wukv, kr, tm=512, comm=()):
    s, k = ckv.shape
    tm = _tile(s, tm)

    def epi(accs, ex, out):
        acc = accs[0]
        out[0][:, :ML_NOPE] = acc[:, :ML_NOPE].astype(BF)
        out[0][:, ML_NOPE:] = ex[0][...]
        out[1][...] = acc[:, ML_NOPE:].astype(BF)

    rmap = lambda j, i: (i, 0)
    gmap = lambda j, i: (j, i, 0)
    return _mm("mla_kv_proj", (ML_HEADS, s // tm),
               [(ckv, (tm, k), rmap, wukv, (None, k, ML_NOPE + ML_V), lambda j, i: (j, 0, 0), "nn", 0, 0)],
               [(kr, (tm, ML_ROPE), rmap)],
               [((ML_HEADS, s, ML_QK), BF, (None, tm, ML_QK), gmap), ((ML_HEADS, s, ML_V), BF, (None, tm, ML_V), gmap)],
               epi, comm=comm)


def _mla_probs(q, k):
    sc = lax.dot_general(q, k, _DN["nt"], preferred_element_type=F32) * (ML_QK ** -0.5)
    e = jnp.exp(sc - jnp.max(sc, axis=-1, keepdims=True))
    return e / jnp.sum(e, axis=-1, keepdims=True)


def _mla_fwd(q, k, v, tq=512, comm=()):
    _, s, _ = q.shape
    tq = _tile(s, tq)

    def body(q_ref, k_ref, v_ref, o_ref):
        p = _mla_probs(q_ref[...], k_ref[...])
        o_ref[...] = jnp.dot(p.astype(BF), v_ref[...], preferred_element_type=F32).astype(BF)

    return _call("mla_fwd", body, (ML_HEADS, s // tq),
                 [pl.BlockSpec((None, tq, ML_QK), lambda h, i: (h, i, 0)),
                  pl.BlockSpec((None, s, ML_QK), lambda h, i: (h, 0, 0)),
                  pl.BlockSpec((None, s, ML_V), lambda h, i: (h, 0, 0))],
                 pl.BlockSpec((tq, ML_V), lambda h, i: (i, h)),
                 jax.ShapeDtypeStruct((s, ML_HEADS * ML_V), BF), [q, k, v], comm)


def _mla_bwd(q, k, v, do, tq=256, comm=()):
    _, s, _ = q.shape
    tq = _tile(s, tq)

    def body(q_ref, k_ref, v_ref, do_ref, dq_ref, dk_ref, dv_ref):
        i = pl.program_id(1)
        qv, kv, vv, dov = q_ref[...], k_ref[...], v_ref[...], do_ref[...]
        p = _mla_probs(qv, kv)
        dp = lax.dot_general(dov, vv, _DN["nt"], preferred_element_type=F32)
        ds = (p * (dp - jnp.sum(p * dp, axis=-1, keepdims=True)) * (ML_QK ** -0.5)).astype(BF)
        dq_ref[...] = jnp.dot(ds, kv, preferred_element_type=F32)
        _acc_rows(dk_ref, lax.dot_general(ds, qv, _DN["tn"], preferred_element_type=F32), i)
        _acc_rows(dv_ref, lax.dot_general(p.astype(BF), dov, _DN["tn"], preferred_element_type=F32), i)

    return _call(
        "mla_bwd", body, (ML_HEADS, s // tq),
        [pl.BlockSpec((None, tq, ML_QK), lambda h, i: (h, i, 0)),
         pl.BlockSpec((None, s, ML_QK), lambda h, i: (h, 0, 0)),
         pl.BlockSpec((None, s, ML_V), lambda h, i: (h, 0, 0)),
         pl.BlockSpec((tq, ML_V), lambda h, i: (i, h))],
        [pl.BlockSpec((None, tq, ML_QK), lambda h, i: (h, i, 0)),
         pl.BlockSpec((None, s, ML_QK), lambda h, i: (h, 0, 0)),
         pl.BlockSpec((None, s, ML_V), lambda h, i: (h, 0, 0))],
        [jax.ShapeDtypeStruct((ML_HEADS, s, ML_QK), F32), jax.ShapeDtypeStruct((ML_HEADS, s, ML_QK), F32),
         jax.ShapeDtypeStruct((ML_HEADS, s, ML_V), F32)],
        [q, k, v, do], comm)


def _mla_post(dq, dk, dv, cos, sin, rot_t, tm=256):
    _, s, _ = dq.shape
    tm = _tile(s, tm)

    def body(dq_ref, dk_ref, dv_ref, c_ref, s_ref, r_ref, dqp_ref, dkv_ref, dkr_ref):
        h = pl.program_id(1)
        dqv, dkk = dq_ref[...], dk_ref[...]
        dqp_ref[:, :ML_NOPE] = dqv[:, :ML_NOPE].astype(BF)
        dqp_ref[:, ML_NOPE:] = _unrope(dqv[:, ML_NOPE:], c_ref[...], s_ref[...], r_ref[...]).astype(BF)
        dkv_ref[:, :ML_NOPE] = dkk[:, :ML_NOPE].astype(BF)
        dkv_ref[:, ML_NOPE:] = dv_ref[...].astype(BF)
        _acc_rows(dkr_ref, dkk[:, ML_NOPE:], h)

    gspec = lambda c: pl.BlockSpec((None, tm, c), lambda i, h: (h, i, 0))
    rspec = pl.BlockSpec((tm, ML_ROPE), lambda i, h: (i, 0))
    return pl.pallas_call(
        body, name="mla_post", grid=(s // tm, ML_HEADS),
        in_specs=[gspec(ML_QK), gspec(ML_QK), gspec(ML_V), rspec, rspec,
                  pl.BlockSpec(rot_t.shape, lambda i, h: (0, 0))],
        out_specs=[gspec(ML_QK), gspec(ML_NOPE + ML_V), rspec],
        out_shape=[jax.ShapeDtypeStruct((ML_HEADS, s, ML_QK), BF),
                   jax.ShapeDtypeStruct((ML_HEADS, s, ML_NOPE + ML_V), BF),
                   jax.ShapeDtypeStruct((s, ML_ROPE), F32)],
        compiler_params=_params(2))(dq, dk, dv, cos, sin, rot_t)


def _mla_lat_bwd(dcq, dckv, dkr, lat, gq, gkv, cos, sin, rot_t, tm=256):
    s, w = lat.shape
    tm = _tile(s, tm)

    def body(dcq_ref, dckv_ref, dkr_ref, l_ref, gq_ref, gkv_ref, c_ref, s_ref, r_ref, dl_ref, dgq_ref, dgkv_ref):
        i = pl.program_id(0)
        dql, pq = _rms_bwd_math(dcq_ref[...], l_ref[:, :ML_RANK], gq_ref[...])
        dkl, pkv = _rms_bwd_math(dckv_ref[...], l_ref[:, ML_RANK:2 * ML_RANK], gkv_ref[...])
        dl_ref[:, :ML_RANK] = dql.astype(BF)
        dl_ref[:, ML_RANK:2 * ML_RANK] = dkl.astype(BF)
        dl_ref[:, 2 * ML_RANK:] = _unrope(dkr_ref[...], c_ref[...], s_ref[...], r_ref[...]).astype(BF)
        _acc_rows(dgq_ref, pq, i)
        _acc_rows(dgkv_ref, pkv, i)

    row = lambda c: pl.BlockSpec((tm, c), lambda i: (i, 0))
    full = lambda a: pl.BlockSpec(a.shape, lambda i: (0, 0))
    return pl.pallas_call(
        body, name="mla_lat_bwd", grid=(s // tm,),
        in_specs=[row(ML_RANK), row(ML_RANK), row(ML_ROPE), row(w), full(gq), full(gkv), row(ML_ROPE), row(ML_ROPE),
                  full(rot_t)],
        out_specs=[row(w), full(gq), full(gkv)],
        out_shape=[jax.ShapeDtypeStruct((s, w), BF), jax.ShapeDtypeStruct(gq.shape, F32),
                   jax.ShapeDtypeStruct(gkv.shape, F32)],
        compiler_params=_params(1))(dcq, dckv, dkr, lat, gq, gkv, cos, sin, rot_t)


def _grp_dw(name, a, dout, ta=512):
    s, k = a.shape
    ta = _tile(k, ta)
    if dout.ndim == 3:
        g, _, nb = dout.shape
        b_blk, b_map = (None, s, nb), lambda j, i: (j, 0, 0)
    else:
        g, nb = NDEV, dout.shape[1] // NDEV
        b_blk, b_map = (s, nb), lambda j, i: (0, j)
    return _mm(name, (g, k // ta),
               [(a, (s, ta), lambda j, i: (0, i), dout, b_blk, b_map, "tn", 0, 0)], [],
               [((g, k, nb), BF, (None, ta, nb), lambda j, i: (j, i, 0))], _store)[0]


def _grp_dw_t(name, dout, a, ta=512):
    g, s, nb = dout.shape
    k = a.shape[1]
    ta = _tile(k, ta)
    return _mm(name, (g, k // ta),
               [(dout, (None, s, nb), lambda j, i: (j, 0, 0), a, (s, ta), lambda j, i: (0, i), "tn", 0, 0)], [],
               [((g, nb, k), BF, (None, nb, ta), lambda j, i: (j, 0, i))], _store)[0]


def _grp_dx_t(name, dout, wt, tm=512, tn=512, comm=()):
    g, s, nb = dout.shape
    k = wt.shape[2]
    tm, tn = _tile(s, tm), _tile(k, tn)
    return _mm(name, (k // tn, s // tm),
               [(dout, (g, tm, nb), lambda j, i: (0, i, 0), wt, (g, nb, tn), lambda j, i: (0, 0, j), "nn", 0, g)], [],
               [((s, k), F32, (tm, tn), lambda j, i: (i, j))], _store, comm=comm)[0]


def _grp_dx(name, dout, w, tm=512, tn=512, comm=()):
    g, s, nb = dout.shape
    k = w.shape[1]
    tm, tn = _tile(s, tm), _tile(k, tn)
    return _mm(name, (k // tn, s // tm),
               [(dout, (g, tm, nb), lambda j, i: (0, i, 0), w, (g, tn, nb), lambda j, i: (0, j, 0), "nt", 0, g)], [],
               [((s, k), F32, (tm, tn), lambda j, i: (i, j))], _store, comm=comm)[0]


def _row_dw(name, a, dout, tn=512):
    s, n = dout.shape
    tn = _tile(n, tn)
    if a.ndim == 3:
        kb = a.shape[2]
        a_blk, a_map = (None, s, kb), lambda j, i: (j, 0, 0)
    else:
        kb = a.shape[1] // NDEV
        a_blk, a_map = (s, kb), lambda j, i: (0, j)
    return _mm(name, (NDEV, n // tn),
               [(a, a_blk, a_map, dout, (s, tn), lambda j, i: (0, i), "tn", 0, 0)], [],
               [((NDEV, kb, n), BF, (None, kb, tn), lambda j, i: (j, 0, i))], _store)[0]


def _mix_merge(oa, ob, wa, wb, ga, gb, tm=512, comm=()):
    s, k = oa.shape
    g, _, nb = wa.shape
    tm = _tile(s, tm)

    def epi(accs, ex, out):
        ya, yb = accs
        out[0][...] = ya.astype(BF)
        out[1][...] = yb.astype(BF)
        out[2][...] = (_sig(ex[0][...]) * ya + _sig(ex[1][...]) * yb).astype(BF)

    rmap = lambda j, i: (i, 0)
    wmap = lambda j, i: (j, 0, 0)
    o = ((g, s, nb), BF, (None, tm, nb), lambda j, i: (j, i, 0))
    cmap = lambda j, i: (i, j)
    return _mm("mix_merge", (g, s // tm),
               [(oa, (tm, k), rmap, wa, (None, k, nb), wmap, "nn", 0, 0),
                (ob, (tm, k), rmap, wb, (None, k, nb), wmap, "nn", 1, 0)],
               [(ga, (tm, nb), cmap), (gb, (tm, nb), cmap)], [o, o, o], epi, nacc=2, comm=comm)


def _mix_out(merged, wout, resid, tm=512, tn=512):
    g, s, kb = merged.shape
    d = wout.shape[2]
    tm, tn = _tile(s, tm), _tile(d, tn)

    def epi(accs, ex, out):
        out[0][...] = ex[0][...] + accs[0]

    return _mm("mix_out", (d // tn, s // tm),
               [(merged, (g, tm, kb), lambda j, i: (0, i, 0), wout, (g, kb, tn), lambda j, i: (0, 0, j), "nn", 0, g)],
               [(resid, (tm, tn), lambda j, i: (i, j))],
               [((s, d), F32, (tm, tn), lambda j, i: (i, j))], epi)[0]


def _mix_out_bwd(dh, wout, ga, gb, ya, yb, tm=512, comm=()):
    s, d = dh.shape
    g, kb, _ = wout.shape
    tm = _tile(s, tm)

    def epi(accs, ex, out):
        dm = accs[0]
        sa, sb = _sig(ex[0][...]), _sig(ex[1][...])
        out[0][...] = (dm * sa).astype(BF)
        out[1][...] = (dm * sb).astype(BF)
        out[2][...] = (dm * ex[2][...].astype(F32) * sa * (1.0 - sa)).astype(BF)
        out[3][...] = (dm * ex[3][...].astype(F32) * sb * (1.0 - sb)).astype(BF)

    cmap = lambda j, i: (i, j)
    gmap = lambda j, i: (j, i, 0)
    og = ((g, s, kb), BF, (None, tm, kb), gmap)
    oc = ((s, g * kb), BF, (tm, kb), cmap)
    return _mm("mix_out_bwd", (g, s // tm),
               [(dh, (tm, d), lambda j, i: (i, 0), wout, (None, kb, d), lambda j, i: (j, 0, 0), "nt", 0, 0)],
               [(ga, (tm, kb), cmap), (gb, (tm, kb), cmap), (ya, (None, tm, kb), gmap), (yb, (None, tm, kb), gmap)],
               [og, og, oc, oc], epi, comm=comm)


def _pl_forward(n4, wplg, p, wpl, h3, tm=512):
    s, d = n4.shape
    g, kb, _ = wplg.shape
    kp, nb = wpl.shape[1], wpl.shape[2]
    tm = _tile(s, tm)
    wplg_nat = wplg.reshape(g * kb, d)

    def epi(accs, ex, out):
        t, pe = accs
        out[0][...] = ex[0][...] + _sig(t) * pe
        out[1][...] = t
        out[2][...] = pe.astype(BF)

    rmap = lambda j, i: (i, 0)
    cmap = lambda j, i: (i, j)
    return _mm("pl_forward", (g, s // tm),
               [(n4, (tm, d), rmap, wplg_nat, (g * kb, nb), lambda j, i: (0, j), "nn", 0, 0),
                (p, (tm, kp), rmap, wpl, (None, kp, nb), lambda j, i: (j, 0, 0), "nn", 1, 0)],
               [(h3, (tm, nb), cmap)],
               [((s, d), F32, (tm, nb), cmap), ((s, d), F32, (tm, nb), cmap), ((s, d), BF, (tm, nb), cmap)],
               epi, nacc=2)


def _row_dx(name, dout, w, tm=512, comm=()):
    s, n = dout.shape
    g, kb, _ = w.shape
    tm = _tile(s, tm)
    return _mm(name, (g, s // tm),
               [(dout, (tm, n), lambda j, i: (i, 0), w, (None, kb, n), lambda j, i: (j, 0, 0), "nt", 0, 0)], [],
               [((s, g * kb), F32, (tm, kb), lambda j, i: (i, j))], _store, comm=comm)[0]


def _in_proj_bwd_x(pieces, weights, tm=512, tn=512, comm=()):
    s = pieces[0].shape[0]
    d = weights[0].shape[1]
    tm, tn = _tile(s, tm), _tile(d, tn)
    prods = [(pc, (tm, pc.shape[1]), lambda j, i: (i, 0), w, (w.shape[0], tn), lambda j, i: (0, j), "nn", 0, 0)
             for pc, w in zip(pieces, weights)]
    return _mm("in_proj_dx", (d // tn, s // tm), prods, [],
               [((s, d), F32, (tm, tn), lambda j, i: (i, j))], _store, comm=comm)[0]


def _split_w_in(w_in_t):
    g, nb, d = w_in_t.shape
    nat = w_in_t.reshape(g * nb, d)
    na, lat = 3 * NA_HEADS * NA_DIM, 2 * ML_RANK + ML_ROPE
    return nat[:na], nat[na:na + lat], nat[na + lat:na + lat + d], nat[na + lat + d:]


def _pair_sum(name, part, landed, core):
    _, _, r, c = part.shape
    tr, tc = _ew_tile(r, c)

    def body(core_ref, a_ref, b_ref, o_ref):
        o_ref[...] = (a_ref[...].astype(F32) + b_ref[...].astype(F32)).astype(o_ref.dtype)

    return pl.pallas_call(
        body, name=name,
        grid_spec=pltpu.PrefetchScalarGridSpec(
            num_scalar_prefetch=1, grid=(NCHIP, r // tr, c // tc),
            in_specs=[pl.BlockSpec((None, None, tr, tc), lambda j, i, k, core_ref: (j, core_ref[0], i, k)),
                      pl.BlockSpec((None, tr, tc), lambda j, i, k, core_ref: (j, i, k))],
            out_specs=pl.BlockSpec((None, tr, tc), lambda j, i, k, core_ref: (j, i, k))),
        out_shape=jax.ShapeDtypeStruct(landed.shape, landed.dtype), compiler_params=_params(3),
    )(core, part, landed)


def _device_step(x, p, target, sp, own, core):
    s, d = x.shape
    rows = s // GRID_W
    cos, sin, rot, rot_t = _rope_consts(s)
    w, dw4, sums, chip_parts, dsp = {}, {}, {}, {}, {}

    def gather(*names):
        return _GatherPart(names, [own[n] for n in names])

    def got(part):
        w.update(zip(part.names, part.results))

    def grad(name, g):
        dw4[name] = g.reshape((NCHIP, 2) + g.shape[1:])

    def to_sibling(*names):
        return _SiblingPart(names, [dw4[n] for n in names])

    def add_pairs(part):
        for n, landed in zip(part.names, part.results):
            sums[n] = _pair_sum("pair_sum_" + n, dw4[n], landed, core)

    def to_chips(*names):
        return _ChipsPart(names, [sums[n] for n in names])

    def done(part):
        chip_parts.update(zip(part.names, part.results))

    c0 = gather("ffn1_w_gate", "ffn1_w_up")
    _comm_only("gather_ffn1", [c0])
    got(c0)
    c1 = gather("ffn1_w_down")
    c2 = gather("w_in")

    def ffn1_wd():
        got(c1)
        return w["ffn1_w_down"]

    h1, ffn1_saved = _ffn_forward("ffn1", x, sp["ffn1_norm"], w["ffn1_w_gate"], w["ffn1_w_up"], ffn1_wd,
                                  up_comm=[c1], down_comm=[c2])
    got(c2)
    wqkv, wlat, wga, wgb = _split_w_in(w["w_in"])
    u = _rms_fwd("mix_norm", h1, sp["mix_norm"])
    c3 = gather("w_uq", "w_ukv")
    qkv = _mm_nt("in_qkv", u, wqkv, BF, tn=1024, comm=[c3])
    got(c3)
    lat = _mm_nt("in_lat", u, wlat, F32)
    c3a = gather("w_branch_a")
    ga = _mm_nt("in_ga", u, wga, F32, tn=1024, comm=[c3a])
    got(c3a)
    c3b = gather("w_branch_b")
    gb = _mm_nt("in_gb", u, wgb, F32, tn=1024, comm=[c3b])
    got(c3b)
    tb = _na_bias_tables(sp["na_rpb"], rows)
    c4 = gather("ffn2_w_gate")
    oa = _na_fwd(qkv, tb, comm=[c4])
    got(c4)
    cq, ckv, kr = _mla_prep(lat, sp["q_a_norm"], sp["kv_a_norm"], cos, sin, rot)
    c4a = gather("w_out")
    qf = _mla_q_proj(cq, w["w_uq"], cos, sin, rot, comm=[c4a])
    got(c4a)
    c4b = gather("w_pl_gate")
    kf, vf = _mla_kv_proj(ckv, w["w_ukv"], kr, comm=[c4b])
    got(c4b)
    c5 = gather("ffn2_w_up")
    ob = _mla_fwd(qf, kf, vf, comm=[c5])
    got(c5)
    c5a = gather("w_pl")
    ya, yb, merged = _mix_merge(oa, ob, w["w_branch_a"], w["w_branch_b"], ga, gb, comm=[c5a])
    got(c5a)
    h2 = _mix_out(merged, w["w_out"], h1)
    c6 = gather("ffn2_w_down")

    def ffn2_wd():
        got(c6)
        return w["ffn2_w_down"]

    h3, ffn2_saved = _ffn_forward("ffn2", h2, sp["ffn2_norm"], w["ffn2_w_gate"], w["ffn2_w_up"], ffn2_wd,
                                  up_comm=[c6])
    n4 = _rms_fwd("pl_norm", h3, sp["pl_norm"])
    pb = p.astype(BF)
    h4, t, pe = _pl_forward(n4, w["w_pl_gate"], pb, w["w_pl"], h3)

    dh4, dsp["final_norm"], loss = _loss_head(h4, target, sp["final_norm"])
    dt, dpe = _pl_bwd_elem(dh4, pe, t)
    grad("w_pl", _grp_dw("pl_dw", pb, dpe))
    grad("w_pl_gate", _row_dw("plg_dw", n4, dt))
    s1 = to_sibling("w_pl", "w_pl_gate")
    dn4 = _row_dx("plg_dx", dt, w["w_pl_gate"], comm=[s1])
    add_pairs(s1)
    dh3, dsp["pl_norm"] = _rms_bwd("pl_dnorm", dn4, h3, sp["pl_norm"], dh4)

    xn, hg, hu, a = ffn2_saved
    dhb = dh3.astype(BF)
    k1 = to_chips("w_pl", "w_pl_gate")
    grad("ffn2_w_down", _ffn_bwd_wd("ffn2_dwd", a, dhb, comm=[k1]))
    done(k1)
    s2 = to_sibling("ffn2_w_down")
    dhg, dhu = _ffn_bwd_act("ffn2_dact", dhb, w["ffn2_w_down"], hg, hu, comm=[s2])
    add_pairs(s2)
    k2 = to_chips("ffn2_w_down")
    dwg, dwu = _ffn_bwd_wup("ffn2_dwup", xn, dhg, dhu, comm=[k2])
    done(k2)
    grad("ffn2_w_gate", dwg)
    grad("ffn2_w_up", dwu)
    s3 = to_sibling("ffn2_w_gate", "ffn2_w_up")
    dxn = _ffn_bwd_x("ffn2_dx", dhg, dhu, w["ffn2_w_gate"], w["ffn2_w_up"], comm=[s3])
    add_pairs(s3)
    dh2, dsp["ffn2_norm"] = _rms_bwd("ffn2_dnorm", dxn, h2, sp["ffn2_norm"], dh3)

    dh2b = dh2.astype(BF)
    grad("w_out", _row_dw("out_dw", merged, dh2b))
    s4 = to_sibling("w_out")
    dya, dyb, dga, dgb = _mix_out_bwd(dh2b, w["w_out"], ga, gb, ya, yb, comm=[s4])
    add_pairs(s4)
    grad("w_branch_a", _grp_dw("bra_dw", oa, dya))
    grad("w_branch_b", _grp_dw("brb_dw", ob, dyb))
    doa = _grp_dx("bra_dx", dya, w["w_branch_a"]).astype(BF)
    s5 = to_sibling("w_branch_a", "w_branch_b")
    dob = _grp_dx("brb_dx", dyb, w["w_branch_b"], comm=[s5]).astype(BF)
    add_pairs(s5)

    k3 = to_chips("ffn2_w_gate", "w_out")
    dqf, dkf, dvf = _mla_bwd(qf, kf, vf, dob, comm=[k3])
    done(k3)
    dqp, dkv, dkr = _mla_post(dqf, dkf, dvf, cos, sin, rot_t)
    grad("w_uq", _grp_dw_t("uq_dw", dqp, cq))
    grad("w_ukv", _grp_dw("ukv_dw", ckv, dkv))
    dcq = _grp_dx_t("uq_dx", dqp, w["w_uq"])
    s6 = to_sibling("w_uq", "w_ukv")
    dckv = _grp_dx("ukv_dx", dkv, w["w_ukv"], comm=[s6])
    add_pairs(s6)
    dlat, dsp["q_a_norm"], dsp["kv_a_norm"] = _mla_lat_bwd(dcq, dckv, dkr, lat, sp["q_a_norm"], sp["kv_a_norm"],
                                                         cos, sin, rot_t)
    k4 = to_chips("ffn2_w_up", "w_branch_a", "w_branch_b")
    dq_na, dk_na, dv_na, dtab = _na_bwd(qkv, tb, doa, comm=[k4])
    done(k4)
    dsp["na_rpb"] = _na_rpb_grad(dtab, rows)
    dqkv = jnp.concatenate([dq_na, dk_na.astype(BF), dv_na.astype(BF)], axis=1)

    pieces = [dqkv, dlat, dga, dgb]
    dwin = jnp.zeros((sum(pc.shape[1] for pc in pieces), d), BF)
    row0 = 0
    for i, pc in enumerate(pieces):
        dwin = _mm_tn_into("in_dw%d" % i, pc, u, dwin, row0)
        row0 += pc.shape[1]
    grad("w_in", dwin.reshape(NDEV, -1, d))
    s7 = to_sibling("w_in")
    k5 = to_chips("w_uq", "w_ukv")
    du = _in_proj_bwd_x(pieces, [wqkv, wlat, wga, wgb], comm=[s7, k5])
    add_pairs(s7)
    done(k5)
    dh1, dsp["mix_norm"] = _rms_bwd("mix_dnorm", du, h1, sp["mix_norm"], dh2)

    xn, hg, hu, a = ffn1_saved
    dhb = dh1.astype(BF)
    k6 = to_chips("w_in")
    grad("ffn1_w_down", _ffn_bwd_wd("ffn1_dwd", a, dhb, comm=[k6]))
    done(k6)
    s8 = to_sibling("ffn1_w_down")
    dhg, dhu = _ffn_bwd_act("ffn1_dact", dhb, w["ffn1_w_down"], hg, hu, comm=[s8])
    add_pairs(s8)
    k7 = to_chips("ffn1_w_down")
    dwg, dwu = _ffn_bwd_wup("ffn1_dwup", xn, dhg, dhu, comm=[k7])
    done(k7)
    grad("ffn1_w_gate", dwg)
    grad("ffn1_w_up", dwu)
    s9 = to_sibling("ffn1_w_gate", "ffn1_w_up")
    _comm_only("rs_sibling_ffn1", [s9])
    add_pairs(s9)
    k8 = to_chips("ffn1_w_gate", "ffn1_w_up")
    dxn = _ffn_bwd_x("ffn1_dx", dhg, dhu, w["ffn1_w_gate"], w["ffn1_w_up"], comm=[k8])
    done(k8)
    dx, dsp["ffn1_norm"] = _rms_bwd("ffn1_dnorm", dxn, x, sp["ffn1_norm"], dh1)
    return loss, dx, chip_parts, dsp


def _gather_small(buf):
    def body(in_ref, out_ref, send_sems, recv_sems, local_sem):
        x, y, c = _coords()
        mine = pltpu.make_async_copy(in_ref, out_ref.at[4 * x + 2 * y + c], local_sem)
        mine.start()
        cps = []
        for k in range(1, NDEV):
            fx, fy, fc = (k >> 2) & 1, (k >> 1) & 1, k & 1
            peer = (x ^ fx, y ^ fy, c ^ fc)
            cps.append(pltpu.make_async_remote_copy(
                src_ref=in_ref, dst_ref=out_ref.at[4 * x + 2 * y + c], send_sem=send_sems.at[k - 1],
                recv_sem=recv_sems.at[k - 1], device_id=peer, device_id_type=MESH))
        for cp in cps:
            cp.start()
        for k in range(1, NDEV):
            fx, fy, fc = (k >> 2) & 1, (k >> 1) & 1, k & 1
            px, py, pc = x ^ fx, y ^ fy, c ^ fc
            pltpu.make_async_remote_copy(
                src_ref=in_ref, dst_ref=out_ref.at[4 * px + 2 * py + pc], send_sem=send_sems.at[k - 1],
                recv_sem=recv_sems.at[k - 1], device_id=(px, py, pc), device_id_type=MESH).wait_recv()
        for cp in cps:
            cp.wait_send()
        mine.wait()

    return pl.pallas_call(
        body, name="gather_small", in_specs=[ANY], out_specs=ANY,
        out_shape=jax.ShapeDtypeStruct((NDEV,) + buf.shape, buf.dtype),
        scratch_shapes=[pltpu.SemaphoreType.DMA((NDEV - 1,)), pltpu.SemaphoreType.DMA((NDEV - 1,)),
                        pltpu.SemaphoreType.DMA],
    )(buf)


def _adam_math(wv, g, m, v):
    m_new = B1 * m + (1.0 - B1) * g
    v_new = B2 * v + (1.0 - B2) * (g * g)
    m_hat = m_new / (1.0 - B1 ** STEP)
    v_hat = v_new / (1.0 - B2 ** STEP)
    return -LR * (m_hat / (jnp.sqrt(v_hat) + ADAM_EPS) + WD * wv), m_new, v_new


def _adam(name, parts, wv, m, v):
    npart, r, c = parts.shape
    tr, tc = _ew_tile(r, c)

    def body(p_ref, w_ref, m_ref, v_ref, g_ref, d_ref, mo_ref, vo_ref):
        g = p_ref[0].astype(F32)
        for j in range(1, npart):
            g = g + p_ref[j].astype(F32)
        g_ref[...] = g
        d_ref[...], mo_ref[...], vo_ref[...] = _adam_math(w_ref[...], g, m_ref[...], v_ref[...])

    blk = pl.BlockSpec((tr, tc), lambda i, k: (i, k))
    return pl.pallas_call(
        body, name=name, grid=(r // tr, c // tc),
        in_specs=[pl.BlockSpec((npart, tr, tc), lambda i, k: (0, i, k)), blk, blk, blk],
        out_specs=[blk] * 4, out_shape=[jax.ShapeDtypeStruct((r, c), F32)] * 4, compiler_params=_params(2),
    )(parts, wv, m, v)


SHARDED = ("ffn1_w_gate", "ffn1_w_up", "ffn1_w_down", "w_in", "w_uq", "w_ukv", "w_branch_a", "w_branch_b", "w_out",
           "ffn2_w_gate", "ffn2_w_up", "ffn2_w_down", "w_pl", "w_pl_gate")
TRANSPOSED = ("ffn1_w_gate", "ffn1_w_up", "ffn2_w_gate", "ffn2_w_up", "w_in", "w_uq")
REPLICATED = ("ffn1_norm", "mix_norm", "q_a_norm", "kv_a_norm", "na_rpb", "ffn2_norm", "pl_norm", "final_norm")
WEIGHTS = ("ffn1_norm", "ffn1_w_gate", "ffn1_w_up", "ffn1_w_down", "mix_norm", "w_in", "q_a_norm", "w_uq",
           "kv_a_norm", "w_ukv", "na_rpb", "w_branch_a", "w_branch_b", "w_out", "ffn2_norm", "ffn2_w_gate",
           "ffn2_w_up", "ffn2_w_down", "pl_norm", "w_pl", "w_pl_gate", "final_norm")
SMALL_W = 2048


def _pack_small(vals):
    rows = []
    for name in REPLICATED:
        flat = vals[name].reshape(-1).astype(F32)
        n = -(-flat.shape[0] // SMALL_W) * SMALL_W
        rows.append(jnp.pad(flat, (0, n - flat.shape[0])).reshape(-1, SMALL_W))
    return jnp.concatenate(rows, axis=0)


def _unpack_small(buf, shapes):
    out, r = {}, 0
    for name in REPLICATED:
        size = int(np.prod(shapes[name]))
        nrow = -(-size // SMALL_W)
        out[name] = buf[r:r + nrow].reshape(-1)[:size].reshape(shapes[name])
        r += nrow
    return out


def kernel(x, p, ffn1_norm, ffn1_w_gate, ffn1_w_up, ffn1_w_down, mix_norm, w_in, q_a_norm, w_uq, kv_a_norm, w_ukv, na_rpb, w_branch_a, w_branch_b, w_out, ffn2_norm, ffn2_w_gate, ffn2_w_up, ffn2_w_down, pl_norm, w_pl, w_pl_gate, final_norm, loss_target, m_ffn1_norm, m_ffn1_w_gate, m_ffn1_w_up, m_ffn1_w_down, m_mix_norm, m_w_in, m_q_a_norm, m_w_uq, m_kv_a_norm, m_w_ukv, m_na_rpb, m_w_branch_a, m_w_branch_b, m_w_out, m_ffn2_norm, m_ffn2_w_gate, m_ffn2_w_up, m_ffn2_w_down, m_pl_norm, m_w_pl, m_w_pl_gate, m_final_norm, v_ffn1_norm, v_ffn1_w_gate, v_ffn1_w_up, v_ffn1_w_down, v_mix_norm, v_w_in, v_q_a_norm, v_w_uq, v_kv_a_norm, v_w_ukv, v_na_rpb, v_w_branch_a, v_w_branch_b, v_w_out, v_ffn2_norm, v_ffn2_w_gate, v_ffn2_w_up, v_ffn2_w_down, v_pl_norm, v_w_pl, v_w_pl_gate, v_final_norm):
    args = dict(locals())
    wts = {n: args[n] for n in WEIGHTS}
    mom = {n: args["m_" + n] for n in WEIGHTS}
    var = {n: args["v_" + n] for n in WEIGHTS}
    shapes = {n: wts[n].shape for n in WEIGHTS}
    core = lax.axis_index("c").astype(jnp.int32).reshape(1)

    local = lambda n, a: a[0].T if n in TRANSPOSED else a[0]
    own = {n: local(n, wts[n]).astype(BF) for n in SHARDED}
    sp = {n: wts[n].reshape(1, -1) for n in REPLICATED if n != "na_rpb"}
    sp["na_rpb"] = wts["na_rpb"][0]
    loss_part, grad_x, chip_parts, dsp = _device_step(x[0], p[0, 0], loss_target[0], sp, own, core)

    out = {}
    for n in SHARDED:
        res4 = _adam("adam_" + n, chip_parts[n], local(n, wts[n]), local(n, mom[n]), local(n, var[n]))
        out[n] = tuple((a.T if n in TRANSPOSED else a)[None] for a in res4)

    small = jnp.concatenate([_pack_small(dsp), jnp.pad(loss_part, ((0, 0), (0, SMALL_W - loss_part.shape[1])))], 0)
    pad_rows = -small.shape[0] % 8
    small = jnp.pad(small, ((0, pad_rows), (0, 0)))
    every = _gather_small(small)
    zeros = jnp.zeros((1 + pad_rows, SMALL_W), F32)
    pack = lambda d: jnp.concatenate([_pack_small(d), zeros], 0)
    g_s, d_s, m_s, v_s = _adam("adam_small", every, pack(wts), pack(mom), pack(var))
    n_rows = small.shape[0] - 1 - pad_rows
    loss = g_s[n_rows, 0]
    small_out = [_unpack_small(b, shapes) for b in (g_s, d_s, m_s, v_s)]
    for n in REPLICATED:
        out[n] = tuple(b[n] for b in small_out)

    res = [loss, grad_x[None]]
    for k in range(4):
        res += [out[n][k] for n in WEIGHTS]
    return tuple(res)
```

```python
import functools

import numpy as np
import jax
import jax.numpy as jnp
from jax import lax
from jax.experimental import pallas as pl
from jax.experimental.pallas import tpu as pltpu

F32 = jnp.float32
BF = jnp.bfloat16
MESH = pl.DeviceIdType.MESH

NDEV = 8
NCHIP = 4
VMEM_LIMIT = 56 * 1024 * 1024
EPS = 1e-6
NEG = -1e30
GRID_W = 64
NA_HEADS, NA_DIM = 8, 128
NA_ROWS_WIN, NA_COLS_WIN = 8, 16
NA_HG = 4
NA_QROWS = 4
ML_HEADS, ML_NOPE, ML_ROPE, ML_V = 8, 128, 64, 128
ML_QK = ML_NOPE + ML_ROPE
ML_RANK = 512
ROPE_THETA = 10000.0
LR, B1, B2, ADAM_EPS, WD, STEP = 0.001, 0.9, 0.999, 1e-08, 0.01, 10
HI = lax.Precision.HIGHEST

_DN = {"nn": (((1,), (0,)), ((), ())), "nt": (((1,), (1,)), ((), ())), "tn": (((0,), (0,)), ((), ()))}


def _params(n):
    return pltpu.CompilerParams(dimension_semantics=("arbitrary",) * n, vmem_limit_bytes=VMEM_LIMIT)


def _sig(v):
    return jax.nn.sigmoid(v)


ANY = pl.BlockSpec(memory_space=pl.ANY)


def _coords():
    return lax.axis_index("x"), lax.axis_index("y"), lax.axis_index("c")


class _Part:
    inputs, out_shapes, sem_shapes, results = (), (), (), None

    def mid(self, ins, outs, sems):
        pass

    def late(self, ins, outs, sems):
        pass


class _GatherPart(_Part):
    def __init__(self, names, shards):
        n = len(shards)
        self.names, self.inputs = list(names), list(shards)
        self.out_shapes = [jax.ShapeDtypeStruct((NDEV,) + a.shape, a.dtype) for a in shards]
        self.sem_shapes = [pltpu.SemaphoreType.DMA((n, 7)), pltpu.SemaphoreType.DMA((n, 7)),
                           pltpu.SemaphoreType.DMA((n,))]

    def _plan(self, ins, outs, sems):
        send_sems, recv_sems, local_sems = sems
        x, y, c = _coords()
        me, sib, diag = (x, y, c), (x, y, 1 - c), (1 - x, 1 - y, c)
        n1, n2 = (x ^ (1 - c), y ^ c, c), (x ^ c, y ^ (1 - c), c)

        def copy(i, k, block, to, src=None):
            px, py, pc = block
            dst = outs[i].at[4 * px + 2 * py + pc]
            return pltpu.make_async_remote_copy(
                src_ref=dst if src is None else src, dst_ref=dst, send_sem=send_sems.at[i, k],
                recv_sem=recv_sems.at[i, k], device_id=to, device_id_type=MESH)

        mine = [pltpu.make_async_copy(ins[i], outs[i].at[4 * x + 2 * y + c], local_sems.at[i])
                for i in range(len(ins))]
        return copy, mine, me, sib, n1, n2, diag

    def _own_sends(self, ins, copy, me, sib, n1, n2):
        return [copy(i, k, me, to, src=ins[i]) for i in range(len(ins)) for k, to in enumerate((sib, n1, n2))]

    def start(self, ins, outs, sems):
        copy, mine, me, sib, n1, n2, _ = self._plan(ins, outs, sems)
        for cp in mine + self._own_sends(ins, copy, me, sib, n1, n2):
            cp.start()

    def mid(self, ins, outs, sems):
        copy, _, me, sib, n1, n2, _ = self._plan(ins, outs, sems)
        for i in range(len(ins)):
            copy(i, 1, n1, me).wait_recv()
            copy(i, 3, n1, n2).start()
            copy(i, 4, n1, sib).start()

    def late(self, ins, outs, sems):
        copy, _, me, sib, _, n2, diag = self._plan(ins, outs, sems)
        for i in range(len(ins)):
            copy(i, 2, n2, me).wait_recv()
            copy(i, 5, n2, sib).start()
        for i in range(len(ins)):
            copy(i, 3, diag, me).wait_recv()
            copy(i, 6, diag, sib).start()

    def finish(self, ins, outs, sems):
        copy, mine, me, sib, n1, n2, diag = self._plan(ins, outs, sems)
        other = lambda dev: (dev[0], dev[1], sib[2])
        n = len(ins)
        for i in range(n):
            copy(i, 0, sib, me).wait_recv()
            for k, block in ((4, other(n2)), (5, other(n1)), (6, other(diag))):
                copy(i, k, block, me).wait_recv()
        for cp in self._own_sends(ins, copy, me, sib, n1, n2):
            cp.wait_send()
        for i in range(n):
            for k, block in ((3, n1), (4, n1), (5, n2), (6, diag)):
                copy(i, k, block, sib).wait_send()
        for cp in mine:
            cp.wait()


class _SiblingPart(_Part):
    def __init__(self, names, parts):
        n = len(parts)
        self.names, self.inputs = list(names), list(parts)
        self.out_shapes = [jax.ShapeDtypeStruct((NCHIP,) + a.shape[2:], a.dtype) for a in parts]
        self.sem_shapes = [pltpu.SemaphoreType.DMA((n,)), pltpu.SemaphoreType.DMA((n,))]

    def _copies(self, ins, outs, sems):
        x, y, c = _coords()
        return [pltpu.make_async_remote_copy(
            src_ref=ins[i].at[:, 1 - c], dst_ref=outs[i], send_sem=sems[0].at[i], recv_sem=sems[1].at[i],
            device_id=(x, y, 1 - c), device_id_type=MESH) for i in range(len(ins))]

    def start(self, ins, outs, sems):
        for cp in self._copies(ins, outs, sems):
            cp.start()

    def finish(self, ins, outs, sems):
        cps = self._copies(ins, outs, sems)
        for cp in cps:
            cp.wait_recv()
        for cp in cps:
            cp.wait_send()


class _ChipsPart(_Part):
    def __init__(self, names, sums):
        n = len(sums)
        self.names, self.inputs = list(names), list(sums)
        self.out_shapes = [jax.ShapeDtypeStruct(a.shape, a.dtype) for a in sums]
        self.sem_shapes = [pltpu.SemaphoreType.DMA((n, 3)), pltpu.SemaphoreType.DMA((n, 3)),
                           pltpu.SemaphoreType.DMA((n,))]

    def _plan(self, ins, outs, sems):
        send_sems, recv_sems, local_sems = sems
        x, y, c = _coords()
        my_chip = 2 * x + y
        chips = [(1 - x, y), (x, 1 - y), (1 - x, 1 - y)]
        n = len(ins)
        mine = [pltpu.make_async_copy(ins[i].at[my_chip], outs[i].at[my_chip], local_sems.at[i]) for i in range(n)]
        sends, recvs = [], []
        for i in range(n):
            for k, (px, py) in enumerate(chips):
                sends.append(pltpu.make_async_remote_copy(
                    src_ref=ins[i].at[2 * px + py], dst_ref=outs[i].at[my_chip], send_sem=send_sems.at[i, k],
                    recv_sem=recv_sems.at[i, k], device_id=(px, py, c), device_id_type=MESH))
                recvs.append(pltpu.make_async_remote_copy(
                    src_ref=ins[i].at[my_chip], dst_ref=outs[i].at[2 * px + py], send_sem=send_sems.at[i, k],
                    recv_sem=recv_sems.at[i, k], device_id=(px, py, c), device_id_type=MESH))
        return mine, sends, recvs

    def start(self, ins, outs, sems):
        mine, sends, _ = self._plan(ins, outs, sems)
        for cp in mine + sends:
            cp.start()

    def finish(self, ins, outs, sems):
        mine, sends, recvs = self._plan(ins, outs, sems)
        for cp in recvs:
            cp.wait_recv()
        for cp in sends:
            cp.wait_send()
        for cp in mine:
            cp.wait()


def _call(name, body, grid, in_specs, out_specs, out_shape, args, comm=(), scratch=()):
    comm = [p for p in comm if p is not None]
    single = not isinstance(out_shape, (list, tuple))
    o_specs = [out_specs] if single else list(out_specs)
    o_shape = [out_shape] if single else list(out_shape)
    n_in, n_out = len(in_specs), len(o_specs)
    c_in = [a for p in comm for a in p.inputs]
    c_out = [s for p in comm for s in p.out_shapes]
    c_sem = [s for p in comm for s in p.sem_shapes]

    def wrapped(*refs):
        ins, outs = refs[:n_in], refs[n_in + len(c_in):n_in + len(c_in) + n_out]
        pos = [n_in, n_in + len(c_in) + n_out, n_in + len(c_in) + n_out + len(c_out)]
        own = refs[pos[2]:pos[2] + len(scratch)]
        pos[2] += len(scratch)
        split = []
        for p in comm:
            sizes = [len(p.inputs), len(p.out_shapes), len(p.sem_shapes)]
            split.append([refs[o:o + n] for o, n in zip(pos, sizes)])
            pos = [o + n for o, n in zip(pos, sizes)]
        step, steps = 0, 1
        for a, g in enumerate(grid):
            step, steps = step * g + pl.program_id(a), steps * g

        def run(which, at):
            def go():
                for p, cut in zip(comm, split):
                    getattr(p, which)(*cut)
            if not comm:
                return
            if grid:
                pl.when(step == at)(go)
            else:
                go()

        run("start", 0)
        body(*ins, *outs, *own)
        run("mid", steps // 2)
        run("late", max(steps // 2, steps - 1 - max(1, steps // 8)))
        run("finish", steps - 1)

    res = pl.pallas_call(
        wrapped, name=name, grid=grid, in_specs=list(in_specs) + [ANY] * len(c_in),
        out_specs=o_specs + [ANY] * len(c_out), out_shape=o_shape + c_out, scratch_shapes=list(scratch) + c_sem,
        compiler_params=_params(len(grid)),
    )(*args, *c_in)
    pos = n_out
    for p in comm:
        p.results = list(res[pos:pos + len(p.out_shapes)])
        pos += len(p.out_shapes)
    return res[0] if single else list(res[:n_out])


def _comm_only(name, comm):
    def body(o_ref):
        o_ref[...] = jnp.zeros_like(o_ref)

    _call(name, body, (), [], pl.BlockSpec(memory_space=pltpu.VMEM), jax.ShapeDtypeStruct((8, 128), F32), [], comm)


def _mm(name, grid, prods, extras, outs, epi, nacc=1, comm=()):
    n_p, n_e = len(prods), len(extras)

    def body(*refs):
        ab, ex, out = refs[:2 * n_p], refs[2 * n_p:2 * n_p + n_e], refs[2 * n_p + n_e:]
        accs = [None] * nacc
        for i, prod in enumerate(prods):
            dn, acc, loop = prod[6], prod[7], prod[8]
            a_ref, b_ref = ab[2 * i], ab[2 * i + 1]
            if loop:
                for g in range(loop):
                    t = lax.dot_general(a_ref[g], b_ref[g], _DN[dn], preferred_element_type=F32)
                    accs[acc] = t if accs[acc] is None else accs[acc] + t
            else:
                t = lax.dot_general(a_ref[...], b_ref[...], _DN[dn], preferred_element_type=F32)
                accs[acc] = t if accs[acc] is None else accs[acc] + t
        epi(accs, ex, out)

    in_specs, args = [], []
    for prod in prods:
        in_specs += [pl.BlockSpec(prod[1], prod[2]), pl.BlockSpec(prod[4], prod[5])]
        args += [prod[0], prod[3]]
    for e, e_blk, e_map in extras:
        in_specs.append(pl.BlockSpec(e_blk, e_map))
        args.append(e)
    return _call(name, body, grid, in_specs, [pl.BlockSpec(blk, mp) for _, _, blk, mp in outs],
                 [jax.ShapeDtypeStruct(s, d) for s, d, _, _ in outs], args, comm)


def _store(accs, ex, out):
    out[0][...] = accs[0].astype(out[0].dtype)


def _ew_tile(r, c, budget=3 << 19):
    for t in range(r - r % 16, 0, -16):
        if r % t == 0 and t * c * 4 <= budget:
            return t, c
    for t in range(c - c % 128, 0, -128):
        if c % t == 0 and r * t * 4 <= budget:
            return r, t
    return r, c


def _tile(n, want):
    t = min(n, want)
    assert n % t == 0, (n, want)
    return t


def _mm_nn(name, a, b, out_dtype, tm=512, tn=512, comm=()):
    m, k = a.shape
    n = b.shape[1]
    tm, tn = _tile(m, tm), (tn if n % tn == 0 else n)
    return _mm(name, (n // tn, m // tm),
               [(a, (tm, k), lambda j, i: (i, 0), b, (k, tn), lambda j, i: (0, j), "nn", 0, 0)], [],
               [((m, n), out_dtype, (tm, tn), lambda j, i: (i, j))], _store, comm=comm)[0]


def _mm_nt(name, a, bt, out_dtype, tm=512, tn=512, comm=()):
    m, k = a.shape
    n = bt.shape[0]
    tm, tn = _tile(m, tm), (tn if n % tn == 0 else n)
    return _mm(name, (n // tn, m // tm),
               [(a, (tm, k), lambda j, i: (i, 0), bt, (tn, k), lambda j, i: (j, 0), "nt", 0, 0)], [],
               [((m, n), out_dtype, (tm, tn), lambda j, i: (i, j))], _store, comm=comm)[0]


def _mm_tn_into(name, a, b, buf, row0, ta=512, tb=512):
    t, ka = a.shape
    nb = b.shape[1]
    ta, tb = (ta if ka % ta == 0 else ka), (tb if nb % tb == 0 else nb)

    def body(a_ref, b_ref, buf_in, buf_out, tile, sem):
        i, j = pl.program_id(0), pl.program_id(1)
        tile[...] = lax.dot_general(a_ref[...], b_ref[...], _DN["tn"], preferred_element_type=F32).astype(tile.dtype)
        rows = pl.ds(pl.multiple_of(row0 + i * ta, 16), ta)
        cp = pltpu.make_async_copy(tile, buf_out.at[rows, pl.ds(pl.multiple_of(j * tb, 128), tb)], sem)
        cp.start()
        cp.wait()

    return pl.pallas_call(
        body, name=name, grid=(ka // ta, nb // tb),
        in_specs=[pl.BlockSpec((t, ta), lambda i, j: (0, i)), pl.BlockSpec((t, tb), lambda i, j: (0, j)), ANY],
        out_specs=ANY, out_shape=jax.ShapeDtypeStruct(buf.shape, buf.dtype), input_output_aliases={2: 0},
        scratch_shapes=[pltpu.VMEM((ta, tb), buf.dtype), pltpu.SemaphoreType.DMA],
        compiler_params=_params(2))(a, b, buf)


def _mm_tn(name, a, b, out_dtype, ta=512, tb=512, scale=None):
    t, ka = a.shape
    nb = b.shape[1]
    ta, tb = (ta if ka % ta == 0 else ka), (tb if nb % tb == 0 else nb)

    def epi(accs, ex, out):
        v = accs[0] if scale is None else accs[0] * scale
        out[0][...] = v.astype(out[0].dtype)

    return _mm(name, (ka // ta, nb // tb),
               [(a, (t, ta), lambda i, j: (0, i), b, (t, tb), lambda i, j: (0, j), "tn", 0, 0)], [],
               [((ka, nb), out_dtype, (ta, tb), lambda i, j: (i, j))], epi)[0]


def _rms_fwd(name, x, g, tm=256):
    s, d = x.shape
    tm = _tile(s, tm)

    def body(x_ref, g_ref, o_ref):
        v = x_ref[...]
        o_ref[...] = (v * lax.rsqrt(jnp.mean(v * v, axis=-1, keepdims=True) + EPS) * g_ref[...]).astype(o_ref.dtype)

    return pl.pallas_call(
        body, name=name, grid=(s // tm,),
        in_specs=[pl.BlockSpec((tm, d), lambda i: (i, 0)), pl.BlockSpec((1, d), lambda i: (0, 0))],
        out_specs=pl.BlockSpec((tm, d), lambda i: (i, 0)), out_shape=jax.ShapeDtypeStruct((s, d), BF),
        compiler_params=_params(1))(x, g)


def _acc_rows(ref, part, i):
    @pl.when(i == 0)
    def _():
        ref[...] = part

    @pl.when(i > 0)
    def _():
        ref[...] += part


def _rms_bwd_math(dn, v, g):
    rstd = lax.rsqrt(jnp.mean(v * v, axis=-1, keepdims=True) + EPS)
    xh = v * rstd
    dxh = dn * g
    dx = rstd * (dxh - xh * jnp.mean(dxh * xh, axis=-1, keepdims=True))
    return dx, jnp.sum(dn * xh, axis=0, keepdims=True)


def _rms_bwd(name, dn, x, g, resid, tm=256):
    s, d = x.shape
    tm = _tile(s, tm)

    def body(dn_ref, x_ref, g_ref, r_ref, dx_ref, dg_ref):
        dx, part = _rms_bwd_math(dn_ref[...].astype(F32), x_ref[...], g_ref[...])
        dx_ref[...] = r_ref[...] + dx
        _acc_rows(dg_ref, part, pl.program_id(0))

    row = pl.BlockSpec((tm, d), lambda i: (i, 0))
    one = pl.BlockSpec((1, d), lambda i: (0, 0))
    return pl.pallas_call(
        body, name=name, grid=(s // tm,), in_specs=[row, row, one, row], out_specs=[row, one],
        out_shape=[jax.ShapeDtypeStruct((s, d), F32), jax.ShapeDtypeStruct((1, d), F32)],
        compiler_params=_params(1))(dn, x, g, resid)


def _loss_head(h, target, g, tm=256):
    s, d = h.shape
    tm = _tile(s, tm)

    def body(h_ref, t_ref, g_ref, dh_ref, dg_ref, loss_ref):
        v, gv = h_ref[...], g_ref[...]
        rstd = lax.rsqrt(jnp.mean(v * v, axis=-1, keepdims=True) + EPS)
        xh = v * rstd
        err = xh * gv - t_ref[...]
        part_loss = 0.5 * jnp.sum(jnp.mean(err * err, axis=-1, keepdims=True), axis=0, keepdims=True)
        dy = err * (1.0 / d)
        dxh = dy * gv
        dh_ref[...] = rstd * (dxh - xh * jnp.mean(dxh * xh, axis=-1, keepdims=True))
        i = pl.program_id(0)
        _acc_rows(dg_ref, jnp.sum(dy * xh, axis=0, keepdims=True), i)
        _acc_rows(loss_ref, jnp.broadcast_to(part_loss, loss_ref.shape), i)

    row = pl.BlockSpec((tm, d), lambda i: (i, 0))
    one = pl.BlockSpec((1, d), lambda i: (0, 0))
    return pl.pallas_call(
        body, name="loss_head", grid=(s // tm,), in_specs=[row, row, one],
        out_specs=[row, one, pl.BlockSpec((1, 128), lambda i: (0, 0))],
        out_shape=[jax.ShapeDtypeStruct((s, d), F32), jax.ShapeDtypeStruct((1, d), F32),
                   jax.ShapeDtypeStruct((1, 128), F32)],
        compiler_params=_params(1))(h, target, g)


def _pl_bwd_elem(dh, pe, t, tm=256):
    s, d = dh.shape
    tm = _tile(s, tm)

    def body(dh_ref, pe_ref, t_ref, dt_ref, dpe_ref):
        dh_v, sg = dh_ref[...], _sig(t_ref[...])
        dt_ref[...] = (dh_v * pe_ref[...].astype(F32) * sg * (1.0 - sg)).astype(BF)
        dpe_ref[...] = (dh_v * sg).astype(BF)

    row = pl.BlockSpec((tm, d), lambda i: (i, 0))
    return pl.pallas_call(
        body, name="pl_bwd_elem", grid=(s // tm,), in_specs=[row, row, row], out_specs=[row, row],
        out_shape=[jax.ShapeDtypeStruct((s, d), BF)] * 2, compiler_params=_params(1))(dh, pe, t)


def _ffn_up(name, xn, wg, wu, tm=512, comm=()):
    s, d = xn.shape
    g, fb, _ = wg.shape
    tm = _tile(s, tm)

    def epi(accs, ex, out):
        hg, hu = accs
        out[0][...] = hg.astype(BF)
        out[1][...] = hu.astype(BF)
        out[2][...] = (hg * _sig(hg) * hu).astype(BF)

    a_map = lambda j, i: (i, 0)
    w_map = lambda j, i: (j, 0, 0)
    o = ((g, s, fb), BF, (None, tm, fb), lambda j, i: (j, i, 0))
    return _mm(name, (g, s // tm),
               [(xn, (tm, d), a_map, wg, (None, fb, d), w_map, "nt", 0, 0),
                (xn, (tm, d), a_map, wu, (None, fb, d), w_map, "nt", 1, 0)], [], [o, o, o], epi, nacc=2, comm=comm)


def _ffn_down(name, a, wd, resid, tm=512, tn=512, comm=()):
    g, s, fb = a.shape
    d = wd.shape[2]
    tm, tn = _tile(s, tm), _tile(d, tn)

    def epi(accs, ex, out):
        out[0][...] = ex[0][...] + 0.5 * accs[0]

    return _mm(name, (d // tn, s // tm),
               [(a, (g, tm, fb), lambda j, i: (0, i, 0), wd, (g, fb, tn), lambda j, i: (0, 0, j), "nn", 0, g)],
               [(resid, (tm, tn), lambda j, i: (i, j))],
               [((s, d), F32, (tm, tn), lambda j, i: (i, j))], epi, comm=comm)[0]


def _ffn_bwd_act(name, dh, wd, hg, hu, tm=512, comm=()):
    s, d = dh.shape
    g, fb, _ = wd.shape
    tm = _tile(s, tm)

    def epi(accs, ex, out):
        da = 0.5 * accs[0]
        hg_v, hu_v = ex[0][...].astype(F32), ex[1][...].astype(F32)
        sg = _sig(hg_v)
        out[0][...] = (da * hu_v * (sg * (1.0 + hg_v * (1.0 - sg)))).astype(BF)
        out[1][...] = (da * (hg_v * sg)).astype(BF)

    blk = (None, tm, fb)
    gmap = lambda j, i: (j, i, 0)
    return _mm(name, (g, s // tm),
               [(dh, (tm, d), lambda j, i: (i, 0), wd, (None, fb, d), lambda j, i: (j, 0, 0), "nt", 0, 0)],
               [(hg, blk, gmap), (hu, blk, gmap)],
               [((g, s, fb), BF, blk, gmap), ((g, s, fb), BF, blk, gmap)], epi, comm=comm)


def _ffn_bwd_wd(name, a, dh, tn=512, comm=()):
    g, s, fb = a.shape
    d = dh.shape[1]
    tn = _tile(d, tn)

    def epi(accs, ex, out):
        out[0][...] = (0.5 * accs[0]).astype(BF)

    return _mm(name, (g, d // tn),
               [(a, (None, s, fb), lambda j, i: (j, 0, 0), dh, (s, tn), lambda j, i: (0, i), "tn", 0, 0)], [],
               [((g, fb, d), BF, (None, fb, tn), lambda j, i: (j, 0, i))], epi, comm=comm)[0]


def _ffn_bwd_wup(name, xn, dhg, dhu, tk=512, comm=()):
    s, d = xn.shape
    g, _, fb = dhg.shape
    tk = _tile(d, tk)

    def epi(accs, ex, out):
        out[0][...] = accs[0].astype(BF)
        out[1][...] = accs[1].astype(BF)

    a_map = lambda j, i: (j, 0, 0)
    b_map = lambda j, i: (0, i)
    o = ((g, fb, d), BF, (None, fb, tk), lambda j, i: (j, 0, i))
    return _mm(name, (g, d // tk),
               [(dhg, (None, s, fb), a_map, xn, (s, tk), b_map, "tn", 0, 0),
                (dhu, (None, s, fb), a_map, xn, (s, tk), b_map, "tn", 1, 0)], [], [o, o], epi, nacc=2, comm=comm)


def _ffn_bwd_x(name, dhg, dhu, wg, wu, tm=512, tn=512, comm=()):
    g, s, fb = dhg.shape
    d = wg.shape[2]
    tm, tn = _tile(s, tm), _tile(d, tn)
    a_blk, a_map = (g, tm, fb), lambda j, i: (0, i, 0)
    b_blk, b_map = (g, fb, tn), lambda j, i: (0, 0, j)
    return _mm(name, (d // tn, s // tm),
               [(dhg, a_blk, a_map, wg, b_blk, b_map, "nn", 0, g), (dhu, a_blk, a_map, wu, b_blk, b_map, "nn", 0, g)],
               [], [((s, d), F32, (tm, tn), lambda j, i: (i, j))], _store, comm=comm)[0]


def _ffn_forward(tag, h, gain, wg, wu, get_wd, up_comm=(), down_comm=()):
    xn = _rms_fwd(tag + "_norm", h, gain)
    hg, hu, a = _ffn_up(tag + "_up", xn, wg, wu, comm=up_comm)
    return _ffn_down(tag + "_down", a, get_wd(), h, comm=down_comm), (xn, hg, hu, a)


def _na_geometry(rows):
    kh = min(NA_ROWS_WIN, rows)
    cols = np.arange(GRID_W)
    col_start = np.clip(cols - NA_COLS_WIN // 2, 0, GRID_W - NA_COLS_WIN)
    mask = (cols[None, :] >= col_start[:, None]) & (cols[None, :] < col_start[:, None] + NA_COLS_WIN)
    dc = np.clip(cols[None, :] - cols[:, None], -(NA_COLS_WIN - 1), NA_COLS_WIN - 1) + (NA_COLS_WIN - 1)
    return kh, mask, dc


def _na_table(rpb, rows):
    _, mask, dc = _na_geometry(rows)
    return jnp.where(jnp.asarray(mask)[None, None], rpb[:, :, dc], NEG)


class _NaPlan:
    def __init__(self, s):
        self.s, self.rows = s, s // GRID_W
        self.kh = min(NA_ROWS_WIN, self.rows)
        self.qr = min(NA_QROWS, self.rows)
        self.kr = min(self.rows, self.kh + self.qr - 1)
        self.groups = self.rows // self.qr
        self.nd = 2 * NA_ROWS_WIN - 1
        self.hw, self.nq = NA_HG * NA_DIM, NA_HEADS // NA_HG
        clip = lambda v, hi: min(max(v, 0), hi)
        pats = [(clip(g * self.qr - self.kh // 2, self.rows - self.kr) - g * self.qr,)
                + tuple(clip(g * self.qr + a - self.kh // 2, self.rows - self.kh) - g * self.qr for a in range(self.qr))
                for g in range(self.groups)]
        self.rebuild = [g for g in range(self.groups) if g == 0 or pats[g] != pats[g - 1]]

    def first_key_row(self, g):
        return jnp.clip(g * self.qr - self.kh // 2, 0, self.rows - self.kr)

    def specs(self):
        blk = pl.BlockSpec((self.qr * GRID_W, self.hw), lambda j, g: (g, j))
        k_spec = pl.BlockSpec((self.s, self.hw), lambda j, g: (0, self.nq + j))
        v_spec = pl.BlockSpec((self.s, self.hw), lambda j, g: (0, 2 * self.nq + j))
        t_spec = pl.BlockSpec((NA_HG, self.nd, GRID_W, GRID_W), lambda j, g: (j, 0, 0, 0))
        return blk, k_spec, v_spec, t_spec

    def bias_scratch(self):
        return pltpu.VMEM((NA_HG, self.qr * GRID_W, self.kr * GRID_W), F32)

    def fill_bias(self, t_ref, bias_ref, g):
        def build():
            r0, ks = g * self.qr, self.first_key_row(g)
            for a in range(self.qr):
                rs = jnp.clip(r0 + a - self.kh // 2, 0, self.rows - self.kh)
                for i in range(self.kr):
                    valid = jnp.logical_and(ks + i >= rs, ks + i < rs + self.kh)
                    idx = jnp.clip(ks + i - r0 - a + NA_ROWS_WIN - 1, 0, self.nd - 1)
                    for h in range(NA_HG):
                        bias_ref[h, a * GRID_W:(a + 1) * GRID_W, i * GRID_W:(i + 1) * GRID_W] = jnp.where(
                            valid, t_ref[h, idx], NEG)

        pl.when(functools.reduce(jnp.logical_or, [g == r for r in self.rebuild]))(build)

    def window(self, g):
        return pl.ds(pl.multiple_of(self.first_key_row(g) * GRID_W, GRID_W), self.kr * GRID_W)


def _na_probs(q, k, bias):
    sc = lax.dot_general(q, k, _DN["nt"], preferred_element_type=F32) * (NA_DIM ** -0.5) + bias
    e = jnp.exp(sc - jnp.max(sc, axis=-1, keepdims=True))
    return e / jnp.sum(e, axis=-1, keepdims=True)


def _na_fwd(qkv, table, comm=()):
    plan = _NaPlan(qkv.shape[0])
    blk, k_spec, v_spec, t_spec = plan.specs()

    def body(q_ref, k_ref, v_ref, t_ref, o_ref, bias_ref):
        g = pl.program_id(1)
        plan.fill_bias(t_ref, bias_ref, g)
        win = plan.window(g)
        for h in range(NA_HG):
            cs = slice(h * NA_DIM, (h + 1) * NA_DIM)
            p = _na_probs(q_ref[:, cs], k_ref[win, cs], bias_ref[h])
            o_ref[:, cs] = jnp.dot(p.astype(BF), v_ref[win, cs], preferred_element_type=F32).astype(BF)

    return _call("na_fwd", body, (plan.nq, plan.groups), [blk, k_spec, v_spec, t_spec], blk,
                 jax.ShapeDtypeStruct((plan.s, NA_HEADS * NA_DIM), BF), [qkv, qkv, qkv, table], comm,
                 scratch=[plan.bias_scratch()])


def _na_bwd(qkv, table, do, comm=()):
    plan = _NaPlan(qkv.shape[0])
    blk, k_spec, v_spec, t_spec = plan.specs()
    qr, kr = plan.qr, plan.kr

    def body(q_ref, k_ref, v_ref, t_ref, do_ref, dq_ref, dk_ref, dv_ref, dt_ref, bias_ref):
        g = pl.program_id(1)

        @pl.when(g == 0)
        def _():
            dk_ref[...] = jnp.zeros_like(dk_ref)
            dv_ref[...] = jnp.zeros_like(dv_ref)
            dt_ref[...] = jnp.zeros_like(dt_ref)

        plan.fill_bias(t_ref, bias_ref, g)
        win = plan.window(g)
        base = plan.first_key_row(g) - g * qr + NA_ROWS_WIN - 1
        for h in range(NA_HG):
            cs = slice(h * NA_DIM, (h + 1) * NA_DIM)
            q, k, v, do_h = q_ref[:, cs], k_ref[win, cs], v_ref[win, cs], do_ref[:, cs]
            p = _na_probs(q, k, bias_ref[h])
            dp = lax.dot_general(do_h, v, _DN["nt"], preferred_element_type=F32)
            ds = p * (dp - jnp.sum(p * dp, axis=-1, keepdims=True))
            for dlt in range(1 - qr, kr):
                tiles = [ds[a * GRID_W:(a + 1) * GRID_W, (a + dlt) * GRID_W:(a + dlt + 1) * GRID_W]
                         for a in range(qr) if 0 <= a + dlt < kr]
                dt_ref[h, jnp.clip(base + dlt, 0, plan.nd - 1)] += functools.reduce(jnp.add, tiles)
            dsb = (ds * (NA_DIM ** -0.5)).astype(BF)
            dq_ref[:, cs] = jnp.dot(dsb, k, preferred_element_type=F32).astype(BF)
            dk_ref[win, cs] += lax.dot_general(dsb, q, _DN["tn"], preferred_element_type=F32)
            dv_ref[win, cs] += lax.dot_general(p.astype(BF), do_h, _DN["tn"], preferred_element_type=F32)

    width = NA_HEADS * NA_DIM
    whole = pl.BlockSpec((plan.s, plan.hw), lambda j, g: (0, j))
    return _call(
        "na_bwd", body, (plan.nq, plan.groups), [blk, k_spec, v_spec, t_spec, blk], [blk, whole, whole, t_spec],
        [jax.ShapeDtypeStruct((plan.s, width), BF), jax.ShapeDtypeStruct((plan.s, width), F32),
         jax.ShapeDtypeStruct((plan.s, width), F32),
         jax.ShapeDtypeStruct((NA_HEADS, plan.nd, GRID_W, GRID_W), F32)],
        [qkv, qkv, qkv, table, do], comm, scratch=[plan.bias_scratch()])


def _na_rpb_grad(dt, rows):
    _, mask, dc = _na_geometry(rows)
    nd, nc = 2 * NA_ROWS_WIN - 1, 2 * NA_COLS_WIN - 1
    onehot = np.zeros((GRID_W * GRID_W, 128), np.float32)
    onehot[np.arange(GRID_W * GRID_W), dc.reshape(-1)] = mask.reshape(-1).astype(np.float32)
    flat = dt.reshape(NA_HEADS * nd, GRID_W * GRID_W)

    def body(a_ref, e_ref, o_ref):
        o_ref[...] = jnp.dot(a_ref[...], e_ref[...], precision=HI, preferred_element_type=F32)

    out = pl.pallas_call(body, name="na_rpb_grad", out_shape=jax.ShapeDtypeStruct((NA_HEADS * nd, 128), F32),
                         compiler_params=_params(0))(flat, jnp.asarray(onehot))
    return out[:, :nc].reshape(NA_HEADS, nd, nc)


def _rope_consts(s):
    pos = np.arange(s, dtype=np.float32)
    inv = (1.0 / (ROPE_THETA ** (np.arange(0, ML_ROPE, 2, dtype=np.float32) / ML_ROPE))).astype(np.float32)
    ang = pos[:, None] * inv[None, :]
    cos, sin = np.cos(ang).astype(np.float32), np.sin(ang).astype(np.float32)
    half = ML_ROPE // 2
    rot = np.zeros((ML_ROPE, ML_ROPE), np.float32)
    rot[np.arange(half) + half, np.arange(half)] = -1.0
    rot[np.arange(half), np.arange(half) + half] = 1.0
    return (jnp.asarray(np.concatenate([cos, cos], 1)), jnp.asarray(np.concatenate([sin, sin], 1)),
            jnp.asarray(rot), jnp.asarray(rot.T.copy()))


def _rope(v, cos, sin, rot):
    return v * cos + jnp.dot(v, rot, precision=HI, preferred_element_type=F32) * sin


def _unrope(dv, cos, sin, rot_t):
    return dv * cos + jnp.dot(dv * sin, rot_t, precision=HI, preferred_element_type=F32)


def _rms(v, g):
    return v * lax.rsqrt(jnp.mean(v * v, axis=-1, keepdims=True) + EPS) * g


def _mla_prep(lat, gq, gkv, cos, sin, rot, tm=256):
    s, w = lat.shape
    tm = _tile(s, tm)

    def body(l_ref, gq_ref, gkv_ref, c_ref, s_ref, r_ref, cq_ref, ckv_ref, kr_ref):
        cq_ref[...] = _rms(l_ref[:, :ML_RANK], gq_ref[...]).astype(BF)
        ckv_ref[...] = _rms(l_ref[:, ML_RANK:2 * ML_RANK], gkv_ref[...]).astype(BF)
        kr_ref[...] = _rope(l_ref[:, 2 * ML_RANK:], c_ref[...], s_ref[...], r_ref[...]).astype(BF)

    row = lambda c: pl.BlockSpec((tm, c), lambda i: (i, 0))
    full = lambda a: pl.BlockSpec(a.shape, lambda i: (0, 0))
    return pl.pallas_call(
        body, name="mla_prep", grid=(s // tm,),
        in_specs=[row(w), full(gq), full(gkv), row(ML_ROPE), row(ML_ROPE), full(rot)],
        out_specs=[row(ML_RANK), row(ML_RANK), row(ML_ROPE)],
        out_shape=[jax.ShapeDtypeStruct((s, ML_RANK), BF), jax.ShapeDtypeStruct((s, ML_RANK), BF),
                   jax.ShapeDtypeStruct((s, ML_ROPE), BF)],
        compiler_params=_params(1))(lat, gq, gkv, cos, sin, rot)


def _mla_q_proj(cq, wuq, cos, sin, rot, tm=512, comm=()):
    s, k = cq.shape
    tm = _tile(s, tm)

    def epi(accs, ex, out):
        acc = accs[0]
        out[0][:, :ML_NOPE] = acc[:, :ML_NOPE].astype(BF)
        out[0][:, ML_NOPE:] = _rope(acc[:, ML_NOPE:], ex[0][...], ex[1][...], ex[2][...]).astype(BF)

    rmap = lambda j, i: (i, 0)
    return _mm("mla_q_proj", (ML_HEADS, s // tm),
               [(cq, (tm, k), rmap, wuq, (None, ML_QK, k), lambda j, i: (j, 0, 0), "nt", 0, 0)],
               [(cos, (tm, ML_ROPE), rmap), (sin, (tm, ML_ROPE), rmap), (rot, rot.shape, lambda j, i: (0, 0))],
               [((ML_HEADS, s, ML_QK), BF, (None, tm, ML_QK), lambda j, i: (j, i, 0))], epi, comm=comm)[0]


def _mla_kv_proj(ckv, wukv, kr, tm=512, comm=()):
    s, k = ckv.shape
    tm = _tile(s, tm)

    def epi(accs, ex, out):
        acc = accs[0]
        out[0][:, :ML_NOPE] = acc[:, :ML_NOPE].astype(BF)
        out[0][:, ML_NOPE:] = ex[0][...]
        out[1][...] = acc[:, ML_NOPE:].astype(BF)

    rmap = lambda j, i: (i, 0)
    gmap = lambda j, i: (j, i, 0)
    return _mm("mla_kv_proj", (ML_HEADS, s // tm),
               [(ckv, (tm, k), rmap, wukv, (None, k, ML_NOPE + ML_V), lambda j, i: (j, 0, 0), "nn", 0, 0)],
               [(kr, (tm, ML_ROPE), rmap)],
               [((ML_HEADS, s, ML_QK), BF, (None, tm, ML_QK), gmap), ((ML_HEADS, s, ML_V), BF, (None, tm, ML_V), gmap)],
               epi, comm=comm)


def _mla_probs(q, k):
    sc = lax.dot_general(q, k, _DN["nt"], preferred_element_type=F32) * (ML_QK ** -0.5)
    e = jnp.exp(sc - jnp.max(sc, axis=-1, keepdims=True))
    return e / jnp.sum(e, axis=-1, keepdims=True)


def _mla_fwd(q, k, v, tq=512, comm=()):
    _, s, _ = q.shape
    tq = _tile(s, tq)

    def body(q_ref, k_ref, v_ref, o_ref):
        p = _mla_probs(q_ref[...], k_ref[...])
        o_ref[...] = jnp.dot(p.astype(BF), v_ref[...], preferred_element_type=F32).astype(BF)

    return _call("mla_fwd", body, (ML_HEADS, s // tq),
                 [pl.BlockSpec((None, tq, ML_QK), lambda h, i: (h, i, 0)),
                  pl.BlockSpec((None, s, ML_QK), lambda h, i: (h, 0, 0)),
                  pl.BlockSpec((None, s, ML_V), lambda h, i: (h, 0, 0))],
                 pl.BlockSpec((tq, ML_V), lambda h, i: (i, h)),
                 jax.ShapeDtypeStruct((s, ML_HEADS * ML_V), BF), [q, k, v], comm)


def _mla_bwd(q, k, v, do, tq=256, comm=()):
    _, s, _ = q.shape
    tq = _tile(s, tq)

    def body(q_ref, k_ref, v_ref, do_ref, dq_ref, dk_ref, dv_ref):
        i = pl.program_id(1)
        qv, kv, vv, dov = q_ref[...], k_ref[...], v_ref[...], do_ref[...]
        p = _mla_probs(qv, kv)
        dp = lax.dot_general(dov, vv, _DN["nt"], preferred_element_type=F32)
        ds = (p * (dp - jnp.sum(p * dp, axis=-1, keepdims=True)) * (ML_QK ** -0.5)).astype(BF)
        dq_ref[...] = jnp.dot(ds, kv, preferred_element_type=F32)
        _acc_rows(dk_ref, lax.dot_general(ds, qv, _DN["tn"], preferred_element_type=F32), i)
        _acc_rows(dv_ref, lax.dot_general(p.astype(BF), dov, _DN["tn"], preferred_element_type=F32), i)

    return _call(
        "mla_bwd", body, (ML_HEADS, s // tq),
        [pl.BlockSpec((None, tq, ML_QK), lambda h, i: (h, i, 0)),
         pl.BlockSpec((None, s, ML_QK), lambda h, i: (h, 0, 0)),
         pl.BlockSpec((None, s, ML_V), lambda h, i: (h, 0, 0)),
         pl.BlockSpec((tq, ML_V), lambda h, i: (i, h))],
        [pl.BlockSpec((None, tq, ML_QK), lambda h, i: (h, i, 0)),
         pl.BlockSpec((None, s, ML_QK), lambda h, i: (h, 0, 0)),
         pl.BlockSpec((None, s, ML_V), lambda h, i: (h, 0, 0))],
        [jax.ShapeDtypeStruct((ML_HEADS, s, ML_QK), F32), jax.ShapeDtypeStruct((ML_HEADS, s, ML_QK), F32),
         jax.ShapeDtypeStruct((ML_HEADS, s, ML_V), F32)],
        [q, k, v, do], comm)


def _mla_post(dq, dk, dv, cos, sin, rot_t, tm=256):
    _, s, _ = dq.shape
    tm = _tile(s, tm)

    def body(dq_ref, dk_ref, dv_ref, c_ref, s_ref, r_ref, dqp_ref, dkv_ref, dkr_ref):
        h = pl.program_id(1)
        dqv, dkk = dq_ref[...], dk_ref[...]
        dqp_ref[:, :ML_NOPE] = dqv[:, :ML_NOPE].astype(BF)
        dqp_ref[:, ML_NOPE:] = _unrope(dqv[:, ML_NOPE:], c_ref[...], s_ref[...], r_ref[...]).astype(BF)
        dkv_ref[:, :ML_NOPE] = dkk[:, :ML_NOPE].astype(BF)
        dkv_ref[:, ML_NOPE:] = dv_ref[...].astype(BF)
        _acc_rows(dkr_ref, dkk[:, ML_NOPE:], h)

    gspec = lambda c: pl.BlockSpec((None, tm, c), lambda i, h: (h, i, 0))
    rspec = pl.BlockSpec((tm, ML_ROPE), lambda i, h: (i, 0))
    return pl.pallas_call(
        body, name="mla_post", grid=(s // tm, ML_HEADS),
        in_specs=[gspec(ML_QK), gspec(ML_QK), gspec(ML_V), rspec, rspec,
                  pl.BlockSpec(rot_t.shape, lambda i, h: (0, 0))],
        out_specs=[gspec(ML_QK), gspec(ML_NOPE + ML_V), rspec],
        out_shape=[jax.ShapeDtypeStruct((ML_HEADS, s, ML_QK), BF),
                   jax.ShapeDtypeStruct((ML_HEADS, s, ML_NOPE + ML_V), BF),
                   jax.ShapeDtypeStruct((s, ML_ROPE), F32)],
        compiler_params=_params(2))(dq, dk, dv, cos, sin, rot_t)


def _mla_lat_bwd(dcq, dckv, dkr, lat, gq, gkv, cos, sin, rot_t, tm=256):
    s, w = lat.shape
    tm = _tile(s, tm)

    def body(dcq_ref, dckv_ref, dkr_ref, l_ref, gq_ref, gkv_ref, c_ref, s_ref, r_ref, dl_ref, dgq_ref, dgkv_ref):
        i = pl.program_id(0)
        dql, pq = _rms_bwd_math(dcq_ref[...], l_ref[:, :ML_RANK], gq_ref[...])
        dkl, pkv = _rms_bwd_math(dckv_ref[...], l_ref[:, ML_RANK:2 * ML_RANK], gkv_ref[...])
        dl_ref[:, :ML_RANK] = dql.astype(BF)
        dl_ref[:, ML_RANK:2 * ML_RANK] = dkl.astype(BF)
        dl_ref[:, 2 * ML_RANK:] = _unrope(dkr_ref[...], c_ref[...], s_ref[...], r_ref[...]).astype(BF)
        _acc_rows(dgq_ref, pq, i)
        _acc_rows(dgkv_ref, pkv, i)

    row = lambda c: pl.BlockSpec((tm, c), lambda i: (i, 0))
    full = lambda a: pl.BlockSpec(a.shape, lambda i: (0, 0))
    return pl.pallas_call(
        body, name="mla_lat_bwd", grid=(s // tm,),
        in_specs=[row(ML_RANK), row(ML_RANK), row(ML_ROPE), row(w), full(gq), full(gkv), row(ML_ROPE), row(ML_ROPE),
                  full(rot_t)],
        out_specs=[row(w), full(gq), full(gkv)],
        out_shape=[jax.ShapeDtypeStruct((s, w), BF), jax.ShapeDtypeStruct(gq.shape, F32),
                   jax.ShapeDtypeStruct(gkv.shape, F32)],
        compiler_params=_params(1))(dcq, dckv, dkr, lat, gq, gkv, cos, sin, rot_t)


def _grp_dw(name, a, dout, ta=512):
    s, k = a.shape
    ta = _tile(k, ta)
    if dout.ndim == 3:
        g, _, nb = dout.shape
        b_blk, b_map = (None, s, nb), lambda j, i: (j, 0, 0)
    else:
        g, nb = NDEV, dout.shape[1] // NDEV
        b_blk, b_map = (s, nb), lambda j, i: (0, j)
    return _mm(name, (g, k // ta),
               [(a, (s, ta), lambda j, i: (0, i), dout, b_blk, b_map, "tn", 0, 0)], [],
               [((g, k, nb), BF, (None, ta, nb), lambda j, i: (j, i, 0))], _store)[0]


def _grp_dw_t(name, dout, a, ta=512):
    g, s, nb = dout.shape
    k = a.shape[1]
    ta = _tile(k, ta)
    return _mm(name, (g, k // ta),
               [(dout, (None, s, nb), lambda j, i: (j, 0, 0), a, (s, ta), lambda j, i: (0, i), "tn", 0, 0)], [],
               [((g, nb, k), BF, (None, nb, ta), lambda j, i: (j, 0, i))], _store)[0]


def _grp_dx_t(name, dout, wt, tm=512, tn=512, comm=()):
    g, s, nb = dout.shape
    k = wt.shape[2]
    tm, tn = _tile(s, tm), _tile(k, tn)
    return _mm(name, (k // tn, s // tm),
               [(dout, (g, tm, nb), lambda j, i: (0, i, 0), wt, (g, nb, tn), lambda j, i: (0, 0, j), "nn", 0, g)], [],
               [((s, k), F32, (tm, tn), lambda j, i: (i, j))], _store, comm=comm)[0]


def _grp_dx(name, dout, w, tm=512, tn=512, comm=()):
    g, s, nb = dout.shape
    k = w.shape[1]
    tm, tn = _tile(s, tm), _tile(k, tn)
    return _mm(name, (k // tn, s // tm),
               [(dout, (g, tm, nb), lambda j, i: (0, i, 0), w, (g, tn, nb), lambda j, i: (0, j, 0), "nt", 0, g)], [],
               [((s, k), F32, (tm, tn), lambda j, i: (i, j))], _store, comm=comm)[0]


def _row_dw(name, a, dout, tn=512):
    s, n = dout.shape
    tn = _tile(n, tn)
    if a.ndim == 3:
        kb = a.shape[2]
        a_blk, a_map = (None, s, kb), lambda j, i: (j, 0, 0)
    else:
        kb = a.shape[1] // NDEV
        a_blk, a_map = (s, kb), lambda j, i: (0, j)
    return _mm(name, (NDEV, n // tn),
               [(a, a_blk, a_map, dout, (s, tn), lambda j, i: (0, i), "tn", 0, 0)], [],
               [((NDEV, kb, n), BF, (None, kb, tn), lambda j, i: (j, 0, i))], _store)[0]


def _mix_merge(oa, ob, wa, wb, ga, gb, tm=512, comm=()):
    s, k = oa.shape
    g, _, nb = wa.shape
    tm = _tile(s, tm)

    def epi(accs, ex, out):
        ya, yb = accs
        out[0][...] = ya.astype(BF)
        out[1][...] = yb.astype(BF)
        out[2][...] = (_sig(ex[0][...]) * ya + _sig(ex[1][...]) * yb).astype(BF)

    rmap = lambda j, i: (i, 0)
    wmap = lambda j, i: (j, 0, 0)
    o = ((g, s, nb), BF, (None, tm, nb), lambda j, i: (j, i, 0))
    cmap = lambda j, i: (i, j)
    return _mm("mix_merge", (g, s // tm),
               [(oa, (tm, k), rmap, wa, (None, k, nb), wmap, "nn", 0, 0),
                (ob, (tm, k), rmap, wb, (None, k, nb), wmap, "nn", 1, 0)],
               [(ga, (tm, nb), cmap), (gb, (tm, nb), cmap)], [o, o, o], epi, nacc=2, comm=comm)


def _mix_out(merged, wout, resid, tm=512, tn=512):
    g, s, kb = merged.shape
    d = wout.shape[2]
    tm, tn = _tile(s, tm), _tile(d, tn)

    def epi(accs, ex, out):
        out[0][...] = ex[0][...] + accs[0]

    return _mm("mix_out", (d // tn, s // tm),
               [(merged, (g, tm, kb), lambda j, i: (0, i, 0), wout, (g, kb, tn), lambda j, i: (0, 0, j), "nn", 0, g)],
               [(resid, (tm, tn), lambda j, i: (i, j))],
               [((s, d), F32, (tm, tn), lambda j, i: (i, j))], epi)[0]


def _mix_out_bwd(dh, wout, ga, gb, ya, yb, tm=512, comm=()):
    s, d = dh.shape
    g, kb, _ = wout.shape
    tm = _tile(s, tm)

    def epi(accs, ex, out):
        dm = accs[0]
        sa, sb = _sig(ex[0][...]), _sig(ex[1][...])
        out[0][...] = (dm * sa).astype(BF)
        out[1][...] = (dm * sb).astype(BF)
        out[2][...] = (dm * ex[2][...].astype(F32) * sa * (1.0 - sa)).astype(BF)
        out[3][...] = (dm * ex[3][...].astype(F32) * sb * (1.0 - sb)).astype(BF)

    cmap = lambda j, i: (i, j)
    gmap = lambda j, i: (j, i, 0)
    og = ((g, s, kb), BF, (None, tm, kb), gmap)
    oc = ((s, g * kb), BF, (tm, kb), cmap)
    return _mm("mix_out_bwd", (g, s // tm),
               [(dh, (tm, d), lambda j, i: (i, 0), wout, (None, kb, d), lambda j, i: (j, 0, 0), "nt", 0, 0)],
               [(ga, (tm, kb), cmap), (gb, (tm, kb), cmap), (ya, (None, tm, kb), gmap), (yb, (None, tm, kb), gmap)],
               [og, og, oc, oc], epi, comm=comm)


def _pl_forward(n4, wplg, p, wpl, h3, tm=512):
    s, d = n4.shape
    g, kb, _ = wplg.shape
    kp, nb = wpl.shape[1], wpl.shape[2]
    tm = _tile(s, tm)
    wplg_nat = wplg.reshape(g * kb, d)

    def epi(accs, ex, out):
        t, pe = accs
        out[0][...] = ex[0][...] + _sig(t) * pe
        out[1][...] = t
        out[2][...] = pe.astype(BF)

    rmap = lambda j, i: (i, 0)
    cmap = lambda j, i: (i, j)
    return _mm("pl_forward", (g, s // tm),
               [(n4, (tm, d), rmap, wplg_nat, (g * kb, nb), lambda j, i: (0, j), "nn", 0, 0),
                (p, (tm, kp), rmap, wpl, (None, kp, nb), lambda j, i: (j, 0, 0), "nn", 1, 0)],
               [(h3, (tm, nb), cmap)],
               [((s, d), F32, (tm, nb), cmap), ((s, d), F32, (tm, nb), cmap), ((s, d), BF, (tm, nb), cmap)],
               epi, nacc=2)


def _row_dx(name, dout, w, tm=512, comm=()):
    s, n = dout.shape
    g, kb, _ = w.shape
    tm = _tile(s, tm)
    return _mm(name, (g, s // tm),
               [(dout, (tm, n), lambda j, i: (i, 0), w, (None, kb, n), lambda j, i: (j, 0, 0), "nt", 0, 0)], [],
               [((s, g * kb), F32, (tm, kb), lambda j, i: (i, j))], _store, comm=comm)[0]


def _in_proj_bwd_x(pieces, weights, tm=512, tn=512, comm=()):
    s = pieces[0].shape[0]
    d = weights[0].shape[1]
    tm, tn = _tile(s, tm), _tile(d, tn)
    prods = [(pc, (tm, pc.shape[1]), lambda j, i: (i, 0), w, (w.shape[0], tn), lambda j, i: (0, j), "nn", 0, 0)
             for pc, w in zip(pieces, weights)]
    return _mm("in_proj_dx", (d // tn, s // tm), prods, [],
               [((s, d), F32, (tm, tn), lambda j, i: (i, j))], _store, comm=comm)[0]


def _split_w_in(w_in_t):
    g, nb, d = w_in_t.shape
    nat = w_in_t.reshape(g * nb, d)
    na, lat = 3 * NA_HEADS * NA_DIM, 2 * ML_RANK + ML_ROPE
    return nat[:na], nat[na:na + lat], nat[na + lat:na + lat + d], nat[na + lat + d:]


def _pair_sum(name, part, landed, core):
    _, _, r, c = part.shape
    tr, tc = _ew_tile(r, c)

    def body(core_ref, a_ref, b_ref, o_ref):
        o_ref[...] = (a_ref[...].astype(F32) + b_ref[...].astype(F32)).astype(o_ref.dtype)

    return pl.pallas_call(
        body, name=name,
        grid_spec=pltpu.PrefetchScalarGridSpec(
            num_scalar_prefetch=1, grid=(NCHIP, r // tr, c // tc),
            in_specs=[pl.BlockSpec((None, None, tr, tc), lambda j, i, k, core_ref: (j, core_ref[0], i, k)),
                      pl.BlockSpec((None, tr, tc), lambda j, i, k, core_ref: (j, i, k))],
            out_specs=pl.BlockSpec((None, tr, tc), lambda j, i, k, core_ref: (j, i, k))),
        out_shape=jax.ShapeDtypeStruct(landed.shape, landed.dtype), compiler_params=_params(3),
    )(core, part, landed)


def _device_step(x, p, target, sp, own, core):
    s, d = x.shape
    rows = s // GRID_W
    cos, sin, rot, rot_t = _rope_consts(s)
    w, dw4, sums, chip_parts, dsp = {}, {}, {}, {}, {}

    def gather(*names):
        return _GatherPart(names, [own[n] for n in names])

    def got(part):
        w.update(zip(part.names, part.results))

    def grad(name, g):
        dw4[name] = g.reshape((NCHIP, 2) + g.shape[1:])

    def to_sibling(*names):
        return _SiblingPart(names, [dw4[n] for n in names])

    def add_pairs(part):
        for n, landed in zip(part.names, part.results):
            sums[n] = _pair_sum("pair_sum_" + n, dw4[n], landed, core)

    def to_chips(*names):
        return _ChipsPart(names, [sums[n] for n in names])

    def done(part):
        chip_parts.update(zip(part.names, part.results))

    c0 = gather("ffn1_w_gate", "ffn1_w_up")
    _comm_only("gather_ffn1", [c0])
    got(c0)
    c1 = gather("ffn1_w_down")
    c2 = gather("w_in")

    def ffn1_wd():
        got(c1)
        return w["ffn1_w_down"]

    h1, ffn1_saved = _ffn_forward("ffn1", x, sp["ffn1_norm"], w["ffn1_w_gate"], w["ffn1_w_up"], ffn1_wd,
                                  up_comm=[c1], down_comm=[c2])
    got(c2)
    wqkv, wlat, wga, wgb = _split_w_in(w["w_in"])
    u = _rms_fwd("mix_norm", h1, sp["mix_norm"])
    c3 = gather("w_uq", "w_ukv")
    qkv = _mm_nt("in_qkv", u, wqkv, BF, tn=1024, comm=[c3])
    got(c3)
    lat = _mm_nt("in_lat", u, wlat, F32)
    c3a = gather("w_branch_a")
    ga = _mm_nt("in_ga", u, wga, F32, tn=1024, comm=[c3a])
    got(c3a)
    c3b = gather("w_branch_b")
    gb = _mm_nt("in_gb", u, wgb, F32, tn=1024, comm=[c3b])
    got(c3b)
    tb = _na_table(sp["na_rpb"], rows)
    c4 = gather("ffn2_w_gate")
    oa = _na_fwd(qkv, tb, comm=[c4])
    got(c4)
    cq, ckv, kr = _mla_prep(lat, sp["q_a_norm"], sp["kv_a_norm"], cos, sin, rot)
    c4a = gather("w_out")
    qf = _mla_q_proj(cq, w["w_uq"], cos, sin, rot, comm=[c4a])
    got(c4a)
    c4b = gather("w_pl_gate")
    kf, vf = _mla_kv_proj(ckv, w["w_ukv"], kr, comm=[c4b])
    got(c4b)
    c5 = gather("ffn2_w_up")
    ob = _mla_fwd(qf, kf, vf, comm=[c5])
    got(c5)
    c5a = gather("w_pl")
    ya, yb, merged = _mix_merge(oa, ob, w["w_branch_a"], w["w_branch_b"], ga, gb, comm=[c5a])
    got(c5a)
    h2 = _mix_out(merged, w["w_out"], h1)
    c6 = gather("ffn2_w_down")

    def ffn2_wd():
        got(c6)
        return w["ffn2_w_down"]

    h3, ffn2_saved = _ffn_forward("ffn2", h2, sp["ffn2_norm"], w["ffn2_w_gate"], w["ffn2_w_up"], ffn2_wd,
                                  up_comm=[c6])
    n4 = _rms_fwd("pl_norm", h3, sp["pl_norm"])
    pb = p.astype(BF)
    h4, t, pe = _pl_forward(n4, w["w_pl_gate"], pb, w["w_pl"], h3)

    dh4, dsp["final_norm"], loss = _loss_head(h4, target, sp["final_norm"])
    dt, dpe = _pl_bwd_elem(dh4, pe, t)
    grad("w_pl", _grp_dw("pl_dw", pb, dpe))
    grad("w_pl_gate", _row_dw("plg_dw", n4, dt))
    s1 = to_sibling("w_pl", "w_pl_gate")
    dn4 = _row_dx("plg_dx", dt, w["w_pl_gate"], comm=[s1])
    add_pairs(s1)
    dh3, dsp["pl_norm"] = _rms_bwd("pl_dnorm", dn4, h3, sp["pl_norm"], dh4)

    xn, hg, hu, a = ffn2_saved
    dhb = dh3.astype(BF)
    k1 = to_chips("w_pl", "w_pl_gate")
    grad("ffn2_w_down", _ffn_bwd_wd("ffn2_dwd", a, dhb, comm=[k1]))
    done(k1)
    s2 = to_sibling("ffn2_w_down")
    dhg, dhu = _ffn_bwd_act("ffn2_dact", dhb, w["ffn2_w_down"], hg, hu, comm=[s2])
    add_pairs(s2)
    k2 = to_chips("ffn2_w_down")
    dwg, dwu = _ffn_bwd_wup("ffn2_dwup", xn, dhg, dhu, comm=[k2])
    done(k2)
    grad("ffn2_w_gate", dwg)
    grad("ffn2_w_up", dwu)
    s3 = to_sibling("ffn2_w_gate", "ffn2_w_up")
    dxn = _ffn_bwd_x("ffn2_dx", dhg, dhu, w["ffn2_w_gate"], w["ffn2_w_up"], comm=[s3])
    add_pairs(s3)
    dh2, dsp["ffn2_norm"] = _rms_bwd("ffn2_dnorm", dxn, h2, sp["ffn2_norm"], dh3)

    dh2b = dh2.astype(BF)
    grad("w_out", _row_dw("out_dw", merged, dh2b))
    s4 = to_sibling("w_out")
    dya, dyb, dga, dgb = _mix_out_bwd(dh2b, w["w_out"], ga, gb, ya, yb, comm=[s4])
    add_pairs(s4)
    grad("w_branch_a", _grp_dw("bra_dw", oa, dya))
    grad("w_branch_b", _grp_dw("brb_dw", ob, dyb))
    doa = _grp_dx("bra_dx", dya, w["w_branch_a"]).astype(BF)
    s5 = to_sibling("w_branch_a", "w_branch_b")
    dob = _grp_dx("brb_dx", dyb, w["w_branch_b"], comm=[s5]).astype(BF)
    add_pairs(s5)

    k3 = to_chips("ffn2_w_gate", "w_out")
    dqf, dkf, dvf = _mla_bwd(qf, kf, vf, dob, comm=[k3])
    done(k3)
    dqp, dkv, dkr = _mla_post(dqf, dkf, dvf, cos, sin, rot_t)
    grad("w_uq", _grp_dw_t("uq_dw", dqp, cq))
    grad("w_ukv", _grp_dw("ukv_dw", ckv, dkv))
    dcq = _grp_dx_t("uq_dx", dqp, w["w_uq"])
    s6 = to_sibling("w_uq", "w_ukv")
    dckv = _grp_dx("ukv_dx", dkv, w["w_ukv"], comm=[s6])
    add_pairs(s6)
    dlat, dsp["q_a_norm"], dsp["kv_a_norm"] = _mla_lat_bwd(dcq, dckv, dkr, lat, sp["q_a_norm"], sp["kv_a_norm"],
                                                         cos, sin, rot_t)
    k4 = to_chips("ffn2_w_up", "w_branch_a", "w_branch_b")
    dq_na, dk_na, dv_na, dtab = _na_bwd(qkv, tb, doa, comm=[k4])
    done(k4)
    dsp["na_rpb"] = _na_rpb_grad(dtab, rows)
    dqkv = jnp.concatenate([dq_na, dk_na.astype(BF), dv_na.astype(BF)], axis=1)

    pieces = [dqkv, dlat, dga, dgb]
    dwin = jnp.zeros((sum(pc.shape[1] for pc in pieces), d), BF)
    row0 = 0
    for i, pc in enumerate(pieces):
        dwin = _mm_tn_into("in_dw%d" % i, pc, u, dwin, row0)
        row0 += pc.shape[1]
    grad("w_in", dwin.reshape(NDEV, -1, d))
    s7 = to_sibling("w_in")
    k5 = to_chips("w_uq", "w_ukv")
    du = _in_proj_bwd_x(pieces, [wqkv, wlat, wga, wgb], comm=[s7, k5])
    add_pairs(s7)
    done(k5)
    dh1, dsp["mix_norm"] = _rms_bwd("mix_dnorm", du, h1, sp["mix_norm"], dh2)

    xn, hg, hu, a = ffn1_saved
    dhb = dh1.astype(BF)
    k6 = to_chips("w_in")
    grad("ffn1_w_down", _ffn_bwd_wd("ffn1_dwd", a, dhb, comm=[k6]))
    done(k6)
    s8 = to_sibling("ffn1_w_down")
    dhg, dhu = _ffn_bwd_act("ffn1_dact", dhb, w["ffn1_w_down"], hg, hu, comm=[s8])
    add_pairs(s8)
    k7 = to_chips("ffn1_w_down")
    dwg, dwu = _ffn_bwd_wup("ffn1_dwup", xn, dhg, dhu, comm=[k7])
    done(k7)
    grad("ffn1_w_gate", dwg)
    grad("ffn1_w_up", dwu)
    s9 = to_sibling("ffn1_w_gate", "ffn1_w_up")
    _comm_only("rs_sibling_ffn1", [s9])
    add_pairs(s9)
    k8 = to_chips("ffn1_w_gate", "ffn1_w_up")
    dxn = _ffn_bwd_x("ffn1_dx", dhg, dhu, w["ffn1_w_gate"], w["ffn1_w_up"], comm=[k8])
    done(k8)
    dx, dsp["ffn1_norm"] = _rms_bwd("ffn1_dnorm", dxn, x, sp["ffn1_norm"], dh1)
    return loss, dx, chip_parts, dsp


def _gather_small(buf):
    def body(in_ref, out_ref, send_sems, recv_sems, local_sem):
        x, y, c = _coords()
        mine = pltpu.make_async_copy(in_ref, out_ref.at[4 * x + 2 * y + c], local_sem)
        mine.start()
        cps = []
        for k in range(1, NDEV):
            fx, fy, fc = (k >> 2) & 1, (k >> 1) & 1, k & 1
            peer = (x ^ fx, y ^ fy, c ^ fc)
            cps.append(pltpu.make_async_remote_copy(
                src_ref=in_ref, dst_ref=out_ref.at[4 * x + 2 * y + c], send_sem=send_sems.at[k - 1],
                recv_sem=recv_sems.at[k - 1], device_id=peer, device_id_type=MESH))
        for cp in cps:
            cp.start()
        for k in range(1, NDEV):
            fx, fy, fc = (k >> 2) & 1, (k >> 1) & 1, k & 1
            px, py, pc = x ^ fx, y ^ fy, c ^ fc
            pltpu.make_async_remote_copy(
                src_ref=in_ref, dst_ref=out_ref.at[4 * px + 2 * py + pc], send_sem=send_sems.at[k - 1],
                recv_sem=recv_sems.at[k - 1], device_id=(px, py, pc), device_id_type=MESH).wait_recv()
        for cp in cps:
            cp.wait_send()
        mine.wait()

    return pl.pallas_call(
        body, name="gather_small", in_specs=[ANY], out_specs=ANY,
        out_shape=jax.ShapeDtypeStruct((NDEV,) + buf.shape, buf.dtype),
        scratch_shapes=[pltpu.SemaphoreType.DMA((NDEV - 1,)), pltpu.SemaphoreType.DMA((NDEV - 1,)),
                        pltpu.SemaphoreType.DMA],
    )(buf)


def _adam_math(wv, g, m, v):
    m_new = B1 * m + (1.0 - B1) * g
    v_new = B2 * v + (1.0 - B2) * (g * g)
    m_hat = m_new / (1.0 - B1 ** STEP)
    v_hat = v_new / (1.0 - B2 ** STEP)
    return -LR * (m_hat / (jnp.sqrt(v_hat) + ADAM_EPS) + WD * wv), m_new, v_new


def _adam(name, parts, wv, m, v):
    npart, r, c = parts.shape
    tr, tc = _ew_tile(r, c)

    def body(p_ref, w_ref, m_ref, v_ref, g_ref, d_ref, mo_ref, vo_ref):
        g = p_ref[0].astype(F32)
        for j in range(1, npart):
            g = g + p_ref[j].astype(F32)
        g_ref[...] = g
        d_ref[...], mo_ref[...], vo_ref[...] = _adam_math(w_ref[...], g, m_ref[...], v_ref[...])

    blk = pl.BlockSpec((tr, tc), lambda i, k: (i, k))
    return pl.pallas_call(
        body, name=name, grid=(r // tr, c // tc),
        in_specs=[pl.BlockSpec((npart, tr, tc), lambda i, k: (0, i, k)), blk, blk, blk],
        out_specs=[blk] * 4, out_shape=[jax.ShapeDtypeStruct((r, c), F32)] * 4, compiler_params=_params(2),
    )(parts, wv, m, v)


SHARDED = ("ffn1_w_gate", "ffn1_w_up", "ffn1_w_down", "w_in", "w_uq", "w_ukv", "w_branch_a", "w_branch_b", "w_out",
           "ffn2_w_gate", "ffn2_w_up", "ffn2_w_down", "w_pl", "w_pl_gate")
TRANSPOSED = ("ffn1_w_gate", "ffn1_w_up", "ffn2_w_gate", "ffn2_w_up", "w_in", "w_uq")
REPLICATED = ("ffn1_norm", "mix_norm", "q_a_norm", "kv_a_norm", "na_rpb", "ffn2_norm", "pl_norm", "final_norm")
WEIGHTS = ("ffn1_norm", "ffn1_w_gate", "ffn1_w_up", "ffn1_w_down", "mix_norm", "w_in", "q_a_norm", "w_uq",
           "kv_a_norm", "w_ukv", "na_rpb", "w_branch_a", "w_branch_b", "w_out", "ffn2_norm", "ffn2_w_gate",
           "ffn2_w_up", "ffn2_w_down", "pl_norm", "w_pl", "w_pl_gate", "final_norm")
SMALL_W = 2048


def _pack_small(vals):
    rows = []
    for name in REPLICATED:
        flat = vals[name].reshape(-1).astype(F32)
        n = -(-flat.shape[0] // SMALL_W) * SMALL_W
        rows.append(jnp.pad(flat, (0, n - flat.shape[0])).reshape(-1, SMALL_W))
    return jnp.concatenate(rows, axis=0)


def _unpack_small(buf, shapes):
    out, r = {}, 0
    for name in REPLICATED:
        size = int(np.prod(shapes[name]))
        nrow = -(-size // SMALL_W)
        out[name] = buf[r:r + nrow].reshape(-1)[:size].reshape(shapes[name])
        r += nrow
    return out


def kernel(x, p, ffn1_norm, ffn1_w_gate, ffn1_w_up, ffn1_w_down, mix_norm, w_in, q_a_norm, w_uq, kv_a_norm, w_ukv, na_rpb, w_branch_a, w_branch_b, w_out, ffn2_norm, ffn2_w_gate, ffn2_w_up, ffn2_w_down, pl_norm, w_pl, w_pl_gate, final_norm, loss_target, m_ffn1_norm, m_ffn1_w_gate, m_ffn1_w_up, m_ffn1_w_down, m_mix_norm, m_w_in, m_q_a_norm, m_w_uq, m_kv_a_norm, m_w_ukv, m_na_rpb, m_w_branch_a, m_w_branch_b, m_w_out, m_ffn2_norm, m_ffn2_w_gate, m_ffn2_w_up, m_ffn2_w_down, m_pl_norm, m_w_pl, m_w_pl_gate, m_final_norm, v_ffn1_norm, v_ffn1_w_gate, v_ffn1_w_up, v_ffn1_w_down, v_mix_norm, v_w_in, v_q_a_norm, v_w_uq, v_kv_a_norm, v_w_ukv, v_na_rpb, v_w_branch_a, v_w_branch_b, v_w_out, v_ffn2_norm, v_ffn2_w_gate, v_ffn2_w_up, v_ffn2_w_down, v_pl_norm, v_w_pl, v_w_pl_gate, v_final_norm):
    args = dict(locals())
    wts = {n: args[n] for n in WEIGHTS}
    mom = {n: args["m_" + n] for n in WEIGHTS}
    var = {n: args["v_" + n] for n in WEIGHTS}
    shapes = {n: wts[n].shape for n in WEIGHTS}
    core = lax.axis_index("c").astype(jnp.int32).reshape(1)

    local = lambda n, a: a[0].T if n in TRANSPOSED else a[0]
    own = {n: local(n, wts[n]).astype(BF) for n in SHARDED}
    sp = {n: wts[n].reshape(1, -1) for n in REPLICATED if n != "na_rpb"}
    sp["na_rpb"] = wts["na_rpb"][0]
    loss_part, grad_x, chip_parts, dsp = _device_step(x[0], p[0, 0], loss_target[0], sp, own, core)

    out = {}
    for n in SHARDED:
        res4 = _adam("adam_" + n, chip_parts[n], local(n, wts[n]), local(n, mom[n]), local(n, var[n]))
        out[n] = tuple((a.T if n in TRANSPOSED else a)[None] for a in res4)

    small = jnp.concatenate([_pack_small(dsp), jnp.pad(loss_part, ((0, 0), (0, SMALL_W - loss_part.shape[1])))], 0)
    pad_rows = -small.shape[0] % 8
    small = jnp.pad(small, ((0, pad_rows), (0, 0)))
    every = _gather_small(small)
    zeros = jnp.zeros((1 + pad_rows, SMALL_W), F32)
    pack = lambda d: jnp.concatenate([_pack_small(d), zeros], 0)
    g_s, d_s, m_s, v_s = _adam("adam_small", every, pack(wts), pack(mom), pack(var))
    n_rows = small.shape[0] - 1 - pad_rows
    loss = g_s[n_rows, 0]
    small_out = [_unpack_small(b, shapes) for b in (g_s, d_s, m_s, v_s)]
    for n in REPLICATED:
        out[n] = tuple(b[n] for b in small_out)

    res = [loss, grad_x[None]]
    for k in range(4):
        res += [out[n][k] for n in WEIGHTS]
    return tuple(res)
```

```python
import functools

import numpy as np
import jax
import jax.numpy as jnp
from jax import lax
from jax.experimental import pallas as pl
from jax.experimental.pallas import tpu as pltpu

F32 = jnp.float32
BF = jnp.bfloat16
MESH = pl.DeviceIdType.MESH

NDEV = 8
NCHIP = 4
VMEM_LIMIT = 56 * 1024 * 1024
EPS = 1e-6
NEG = -1e30
GRID_W = 64
NA_HEADS, NA_DIM = 8, 128
NA_ROWS_WIN, NA_COLS_WIN = 8, 16
NA_HG = 4
NA_QROWS = 4
ML_HEADS, ML_NOPE, ML_ROPE, ML_V = 8, 128, 64, 128
ML_QK = ML_NOPE + ML_ROPE
ML_RANK = 512
ROPE_THETA = 10000.0
LR, B1, B2, ADAM_EPS, WD, STEP = 0.001, 0.9, 0.999, 1e-08, 0.01, 10
HI = lax.Precision.HIGHEST

_DN = {"nn": (((1,), (0,)), ((), ())), "nt": (((1,), (1,)), ((), ())), "tn": (((0,), (0,)), ((), ()))}


def _params(n):
    return pltpu.CompilerParams(dimension_semantics=("arbitrary",) * n, vmem_limit_bytes=VMEM_LIMIT)


def _sig(v):
    return jax.nn.sigmoid(v)


ANY = pl.BlockSpec(memory_space=pl.ANY)


def _coords():
    return lax.axis_index("x"), lax.axis_index("y"), lax.axis_index("c")


class _Part:
    inputs, out_shapes, sem_shapes, results = (), (), (), None

    def mid(self, ins, outs, sems):
        pass

    def late(self, ins, outs, sems):
        pass


class _GatherPart(_Part):
    def __init__(self, names, shards):
        n = len(shards)
        self.names, self.inputs = list(names), list(shards)
        self.out_shapes = [jax.ShapeDtypeStruct((NDEV,) + a.shape, a.dtype) for a in shards]
        self.sem_shapes = [pltpu.SemaphoreType.DMA((n, 7)), pltpu.SemaphoreType.DMA((n, 7)),
                           pltpu.SemaphoreType.DMA((n,))]

    def _plan(self, ins, outs, sems):
        send_sems, recv_sems, local_sems = sems
        x, y, c = _coords()
        me, sib, diag = (x, y, c), (x, y, 1 - c), (1 - x, 1 - y, c)
        n1, n2 = (x ^ (1 - c), y ^ c, c), (x ^ c, y ^ (1 - c), c)

        def copy(i, k, block, to, src=None):
            px, py, pc = block
            dst = outs[i].at[4 * px + 2 * py + pc]
            return pltpu.make_async_remote_copy(
                src_ref=dst if src is None else src, dst_ref=dst, send_sem=send_sems.at[i, k],
                recv_sem=recv_sems.at[i, k], device_id=to, device_id_type=MESH)

        mine = [pltpu.make_async_copy(ins[i], outs[i].at[4 * x + 2 * y + c], local_sems.at[i])
                for i in range(len(ins))]
        return copy, mine, me, sib, n1, n2, diag

    def _own_sends(self, ins, copy, me, sib, n1, n2):
        return [copy(i, k, me, to, src=ins[i]) for i in range(len(ins)) for k, to in enumerate((sib, n1, n2))]

    def start(self, ins, outs, sems):
        copy, mine, me, sib, n1, n2, _ = self._plan(ins, outs, sems)
        for cp in mine + self._own_sends(ins, copy, me, sib, n1, n2):
            cp.start()

    def mid(self, ins, outs, sems):
        copy, _, me, sib, n1, n2, _ = self._plan(ins, outs, sems)
        for i in range(len(ins)):
            copy(i, 1, n1, me).wait_recv()
            copy(i, 3, n1, n2).start()
            copy(i, 4, n1, sib).start()

    def late(self, ins, outs, sems):
        copy, _, me, sib, _, n2, diag = self._plan(ins, outs, sems)
        for i in range(len(ins)):
            copy(i, 2, n2, me).wait_recv()
            copy(i, 5, n2, sib).start()
        for i in range(len(ins)):
            copy(i, 3, diag, me).wait_recv()
            copy(i, 6, diag, sib).start()

    def finish(self, ins, outs, sems):
        copy, mine, me, sib, n1, n2, diag = self._plan(ins, outs, sems)
        other = lambda dev: (dev[0], dev[1], sib[2])
        n = len(ins)
        for i in range(n):
            copy(i, 0, sib, me).wait_recv()
            for k, block in ((4, other(n2)), (5, other(n1)), (6, other(diag))):
                copy(i, k, block, me).wait_recv()
        for cp in self._own_sends(ins, copy, me, sib, n1, n2):
            cp.wait_send()
        for i in range(n):
            for k, block in ((3, n1), (4, n1), (5, n2), (6, diag)):
                copy(i, k, block, sib).wait_send()
        for cp in mine:
            cp.wait()


class _SiblingPart(_Part):
    def __init__(self, names, parts):
        n = len(parts)
        self.names, self.inputs = list(names), list(parts)
        self.out_shapes = [jax.ShapeDtypeStruct((NCHIP,) + a.shape[2:], a.dtype) for a in parts]
        self.sem_shapes = [pltpu.SemaphoreType.DMA((n,)), pltpu.SemaphoreType.DMA((n,))]

    def _copies(self, ins, outs, sems):
        x, y, c = _coords()
        return [pltpu.make_async_remote_copy(
            src_ref=ins[i].at[:, 1 - c], dst_ref=outs[i], send_sem=sems[0].at[i], recv_sem=sems[1].at[i],
            device_id=(x, y, 1 - c), device_id_type=MESH) for i in range(len(ins))]

    def start(self, ins, outs, sems):
        for cp in self._copies(ins, outs, sems):
            cp.start()

    def finish(self, ins, outs, sems):
        cps = self._copies(ins, outs, sems)
        for cp in cps:
            cp.wait_recv()
        for cp in cps:
            cp.wait_send()


HBM = pl.BlockSpec(memory_space=pltpu.HBM)
SEM = pl.BlockSpec(memory_space=pltpu.SEMAPHORE)


def _chip_peers():
    x, y, c = _coords()
    return [(1 - x, y, c), (x, 1 - y, c), (1 - x, 1 - y, c)]


def _chips_start(name, sums):
    n = len(sums)

    def body(*refs):
        ins, lands, send_sems, recv_sems = refs[:n], refs[n:2 * n], refs[2 * n], refs[2 * n + 1]
        for i in range(n):
            for k, (px, py, pc) in enumerate(_chip_peers()):
                pltpu.make_async_remote_copy(
                    src_ref=ins[i].at[2 * px + py], dst_ref=lands[i].at[k], send_sem=send_sems.at[3 * i + k],
                    recv_sem=recv_sems.at[3 * i + k], device_id=(px, py, pc), device_id_type=MESH).start()
        refs[-1][...] = jnp.zeros_like(refs[-1])

    lands = [lax.empty((3,) + a.shape[1:], a.dtype) for a in sums]
    bufs = list(sums) + lands
    res = pl.pallas_call(
        body, name=name, in_specs=[HBM] * (2 * n),
        out_specs=(SEM, SEM, *[HBM] * (2 * n), pl.BlockSpec(memory_space=pltpu.VMEM)),
        out_shape=(pltpu.SemaphoreType.DMA((3 * n,)), pltpu.SemaphoreType.DMA((3 * n,)),
                   *[pltpu.HBM(a.shape, a.dtype) for a in bufs], jax.ShapeDtypeStruct((8, 128), F32)),
        input_output_aliases={i: 2 + i for i in range(2 * n)},
        compiler_params=pltpu.CompilerParams(has_side_effects=pltpu.SideEffectType.DATAFLOW_SIDE_EFFECTING),
    )(*[pltpu.with_memory_space_constraint(a, pltpu.HBM) for a in bufs])
    return res[0], res[1], list(res[2:2 + n]), list(res[2 + n:2 + 2 * n]), res[-1]


def _chips_wait(name, send_sems, recv_sems, sums, lands, after):
    n = len(sums)

    def body(*refs):
        ins, zones, send, recv = refs[:n], refs[n:2 * n], refs[2 * n], refs[2 * n + 1]
        for i in range(n):
            for k, peer in enumerate(_chip_peers()):
                cp = pltpu.make_async_remote_copy(
                    src_ref=ins[i].at[0], dst_ref=zones[i].at[k], send_sem=send.at[3 * i + k],
                    recv_sem=recv.at[3 * i + k],
                    device_id=peer, device_id_type=MESH)
                cp.wait_send()
                cp.wait_recv()

    bufs = list(sums) + list(lands)
    res = pl.pallas_call(
        body, name=name, in_specs=[HBM] * (2 * n) + [SEM, SEM, ANY], out_specs=[HBM] * (2 * n),
        out_shape=[pltpu.HBM(a.shape, a.dtype) for a in bufs], input_output_aliases={i: i for i in range(2 * n)},
        compiler_params=pltpu.CompilerParams(has_side_effects=pltpu.SideEffectType.DATAFLOW_SIDE_EFFECTING),
    )(*bufs, send_sems, recv_sems, after)
    return list(res[:n]), list(res[n:])


def _call(name, body, grid, in_specs, out_specs, out_shape, args, comm=(), scratch=()):
    comm = [p for p in comm if p is not None]
    single = not isinstance(out_shape, (list, tuple))
    o_specs = [out_specs] if single else list(out_specs)
    o_shape = [out_shape] if single else list(out_shape)
    n_in, n_out = len(in_specs), len(o_specs)
    c_in = [a for p in comm for a in p.inputs]
    c_out = [s for p in comm for s in p.out_shapes]
    c_sem = [s for p in comm for s in p.sem_shapes]

    def wrapped(*refs):
        ins, outs = refs[:n_in], refs[n_in + len(c_in):n_in + len(c_in) + n_out]
        pos = [n_in, n_in + len(c_in) + n_out, n_in + len(c_in) + n_out + len(c_out)]
        own = refs[pos[2]:pos[2] + len(scratch)]
        pos[2] += len(scratch)
        split = []
        for p in comm:
            sizes = [len(p.inputs), len(p.out_shapes), len(p.sem_shapes)]
            split.append([refs[o:o + n] for o, n in zip(pos, sizes)])
            pos = [o + n for o, n in zip(pos, sizes)]
        step, steps = 0, 1
        for a, g in enumerate(grid):
            step, steps = step * g + pl.program_id(a), steps * g

        def run(which, at):
            def go():
                for p, cut in zip(comm, split):
                    getattr(p, which)(*cut)
            if not comm:
                return
            if grid:
                pl.when(step == at)(go)
            else:
                go()

        run("start", 0)
        body(*ins, *outs, *own)
        run("mid", steps // 2)
        run("late", max(steps // 2, steps - 1 - max(1, steps // 8)))
        run("finish", steps - 1)

    res = pl.pallas_call(
        wrapped, name=name, grid=grid, in_specs=list(in_specs) + [ANY] * len(c_in),
        out_specs=o_specs + [ANY] * len(c_out), out_shape=o_shape + c_out, scratch_shapes=list(scratch) + c_sem,
        compiler_params=_params(len(grid)),
    )(*args, *c_in)
    pos = n_out
    for p in comm:
        p.results = list(res[pos:pos + len(p.out_shapes)])
        pos += len(p.out_shapes)
    return res[0] if single else list(res[:n_out])


def _comm_only(name, comm):
    def body(o_ref):
        o_ref[...] = jnp.zeros_like(o_ref)

    _call(name, body, (), [], pl.BlockSpec(memory_space=pltpu.VMEM), jax.ShapeDtypeStruct((8, 128), F32), [], comm)


def _mm(name, grid, prods, extras, outs, epi, nacc=1, comm=()):
    n_p, n_e = len(prods), len(extras)

    def body(*refs):
        ab, ex, out = refs[:2 * n_p], refs[2 * n_p:2 * n_p + n_e], refs[2 * n_p + n_e:]
        accs = [None] * nacc
        for i, prod in enumerate(prods):
            dn, acc, loop = prod[6], prod[7], prod[8]
            a_ref, b_ref = ab[2 * i], ab[2 * i + 1]
            if loop:
                for g in range(loop):
                    t = lax.dot_general(a_ref[g], b_ref[g], _DN[dn], preferred_element_type=F32)
                    accs[acc] = t if accs[acc] is None else accs[acc] + t
            else:
                t = lax.dot_general(a_ref[...], b_ref[...], _DN[dn], preferred_element_type=F32)
                accs[acc] = t if accs[acc] is None else accs[acc] + t
        epi(accs, ex, out)

    in_specs, args = [], []
    for prod in prods:
        in_specs += [pl.BlockSpec(prod[1], prod[2]), pl.BlockSpec(prod[4], prod[5])]
        args += [prod[0], prod[3]]
    for e, e_blk, e_map in extras:
        in_specs.append(pl.BlockSpec(e_blk, e_map))
        args.append(e)
    return _call(name, body, grid, in_specs, [pl.BlockSpec(blk, mp) for _, _, blk, mp in outs],
                 [jax.ShapeDtypeStruct(s, d) for s, d, _, _ in outs], args, comm)


def _store(accs, ex, out):
    out[0][...] = accs[0].astype(out[0].dtype)


def _ew_tile(r, c, budget=3 << 19):
    for t in range(r - r % 16, 0, -16):
        if r % t == 0 and t * c * 4 <= budget:
            return t, c
    for t in range(c - c % 128, 0, -128):
        if c % t == 0 and r * t * 4 <= budget:
            return r, t
    return r, c


def _tile(n, want):
    t = min(n, want)
    assert n % t == 0, (n, want)
    return t


def _mm_nn(name, a, b, out_dtype, tm=512, tn=512, comm=()):
    m, k = a.shape
    n = b.shape[1]
    tm, tn = _tile(m, tm), (tn if n % tn == 0 else n)
    return _mm(name, (n // tn, m // tm),
               [(a, (tm, k), lambda j, i: (i, 0), b, (k, tn), lambda j, i: (0, j), "nn", 0, 0)], [],
               [((m, n), out_dtype, (tm, tn), lambda j, i: (i, j))], _store, comm=comm)[0]


def _mm_nt(name, a, bt, out_dtype, tm=512, tn=512, comm=()):
    m, k = a.shape
    n = bt.shape[0]
    tm, tn = _tile(m, tm), (tn if n % tn == 0 else n)
    return _mm(name, (n // tn, m // tm),
               [(a, (tm, k), lambda j, i: (i, 0), bt, (tn, k), lambda j, i: (j, 0), "nt", 0, 0)], [],
               [((m, n), out_dtype, (tm, tn), lambda j, i: (i, j))], _store, comm=comm)[0]


def _mm_tn_into(name, a, b, buf, row0, ta=512, tb=512):
    t, ka = a.shape
    nb = b.shape[1]
    ta, tb = (ta if ka % ta == 0 else ka), (tb if nb % tb == 0 else nb)

    def body(a_ref, b_ref, buf_in, buf_out, tile, sem):
        i, j = pl.program_id(0), pl.program_id(1)
        tile[...] = lax.dot_general(a_ref[...], b_ref[...], _DN["tn"], preferred_element_type=F32).astype(tile.dtype)
        rows = pl.ds(pl.multiple_of(row0 + i * ta, 16), ta)
        cp = pltpu.make_async_copy(tile, buf_out.at[rows, pl.ds(pl.multiple_of(j * tb, 128), tb)], sem)
        cp.start()
        cp.wait()

    return pl.pallas_call(
        body, name=name, grid=(ka // ta, nb // tb),
        in_specs=[pl.BlockSpec((t, ta), lambda i, j: (0, i)), pl.BlockSpec((t, tb), lambda i, j: (0, j)), ANY],
        out_specs=ANY, out_shape=jax.ShapeDtypeStruct(buf.shape, buf.dtype), input_output_aliases={2: 0},
        scratch_shapes=[pltpu.VMEM((ta, tb), buf.dtype), pltpu.SemaphoreType.DMA],
        compiler_params=_params(2))(a, b, buf)


def _mm_tn(name, a, b, out_dtype, ta=512, tb=512, scale=None):
    t, ka = a.shape
    nb = b.shape[1]
    ta, tb = (ta if ka % ta == 0 else ka), (tb if nb % tb == 0 else nb)

    def epi(accs, ex, out):
        v = accs[0] if scale is None else accs[0] * scale
        out[0][...] = v.astype(out[0].dtype)

    return _mm(name, (ka // ta, nb // tb),
               [(a, (t, ta), lambda i, j: (0, i), b, (t, tb), lambda i, j: (0, j), "tn", 0, 0)], [],
               [((ka, nb), out_dtype, (ta, tb), lambda i, j: (i, j))], epi)[0]


def _rms_fwd(name, x, g, tm=256):
    s, d = x.shape
    tm = _tile(s, tm)

    def body(x_ref, g_ref, o_ref):
        v = x_ref[...]
        o_ref[...] = (v * lax.rsqrt(jnp.mean(v * v, axis=-1, keepdims=True) + EPS) * g_ref[...]).astype(o_ref.dtype)

    return pl.pallas_call(
        body, name=name, grid=(s // tm,),
        in_specs=[pl.BlockSpec((tm, d), lambda i: (i, 0)), pl.BlockSpec((1, d), lambda i: (0, 0))],
        out_specs=pl.BlockSpec((tm, d), lambda i: (i, 0)), out_shape=jax.ShapeDtypeStruct((s, d), BF),
        compiler_params=_params(1))(x, g)


def _acc_rows(ref, part, i):
    @pl.when(i == 0)
    def _():
        ref[...] = part

    @pl.when(i > 0)
    def _():
        ref[...] += part


def _rms_bwd_math(dn, v, g):
    rstd = lax.rsqrt(jnp.mean(v * v, axis=-1, keepdims=True) + EPS)
    xh = v * rstd
    dxh = dn * g
    dx = rstd * (dxh - xh * jnp.mean(dxh * xh, axis=-1, keepdims=True))
    return dx, jnp.sum(dn * xh, axis=0, keepdims=True)


def _rms_bwd(name, dn, x, g, resid, tm=256):
    s, d = x.shape
    tm = _tile(s, tm)

    def body(dn_ref, x_ref, g_ref, r_ref, dx_ref, dg_ref):
        dx, part = _rms_bwd_math(dn_ref[...].astype(F32), x_ref[...], g_ref[...])
        dx_ref[...] = r_ref[...] + dx
        _acc_rows(dg_ref, part, pl.program_id(0))

    row = pl.BlockSpec((tm, d), lambda i: (i, 0))
    one = pl.BlockSpec((1, d), lambda i: (0, 0))
    return pl.pallas_call(
        body, name=name, grid=(s // tm,), in_specs=[row, row, one, row], out_specs=[row, one],
        out_shape=[jax.ShapeDtypeStruct((s, d), F32), jax.ShapeDtypeStruct((1, d), F32)],
        compiler_params=_params(1))(dn, x, g, resid)


def _loss_head(h, target, g, tm=256):
    s, d = h.shape
    tm = _tile(s, tm)

    def body(h_ref, t_ref, g_ref, dh_ref, dg_ref, loss_ref):
        v, gv = h_ref[...], g_ref[...]
        rstd = lax.rsqrt(jnp.mean(v * v, axis=-1, keepdims=True) + EPS)
        xh = v * rstd
        err = xh * gv - t_ref[...]
        part_loss = 0.5 * jnp.sum(jnp.mean(err * err, axis=-1, keepdims=True), axis=0, keepdims=True)
        dy = err * (1.0 / d)
        dxh = dy * gv
        dh_ref[...] = rstd * (dxh - xh * jnp.mean(dxh * xh, axis=-1, keepdims=True))
        i = pl.program_id(0)
        _acc_rows(dg_ref, jnp.sum(dy * xh, axis=0, keepdims=True), i)
        _acc_rows(loss_ref, jnp.broadcast_to(part_loss, loss_ref.shape), i)

    row = pl.BlockSpec((tm, d), lambda i: (i, 0))
    one = pl.BlockSpec((1, d), lambda i: (0, 0))
    return pl.pallas_call(
        body, name="loss_head", grid=(s // tm,), in_specs=[row, row, one],
        out_specs=[row, one, pl.BlockSpec((1, 128), lambda i: (0, 0))],
        out_shape=[jax.ShapeDtypeStruct((s, d), F32), jax.ShapeDtypeStruct((1, d), F32),
                   jax.ShapeDtypeStruct((1, 128), F32)],
        compiler_params=_params(1))(h, target, g)


def _pl_bwd_elem(dh, pe, t, tm=256):
    s, d = dh.shape
    tm = _tile(s, tm)

    def body(dh_ref, pe_ref, t_ref, dt_ref, dpe_ref):
        dh_v, sg = dh_ref[...], _sig(t_ref[...])
        dt_ref[...] = (dh_v * pe_ref[...].astype(F32) * sg * (1.0 - sg)).astype(BF)
        dpe_ref[...] = (dh_v * sg).astype(BF)

    row = pl.BlockSpec((tm, d), lambda i: (i, 0))
    return pl.pallas_call(
        body, name="pl_bwd_elem", grid=(s // tm,), in_specs=[row, row, row], out_specs=[row, row],
        out_shape=[jax.ShapeDtypeStruct((s, d), BF)] * 2, compiler_params=_params(1))(dh, pe, t)


def _ffn_up(name, xn, wg, wu, tm=512, comm=()):
    s, d = xn.shape
    g, fb, _ = wg.shape
    tm = _tile(s, tm)

    def epi(accs, ex, out):
        hg, hu = accs
        out[0][...] = hg.astype(BF)
        out[1][...] = hu.astype(BF)
        out[2][...] = (hg * _sig(hg) * hu).astype(BF)

    a_map = lambda j, i: (i, 0)
    w_map = lambda j, i: (j, 0, 0)
    o = ((g, s, fb), BF, (None, tm, fb), lambda j, i: (j, i, 0))
    return _mm(name, (g, s // tm),
               [(xn, (tm, d), a_map, wg, (None, fb, d), w_map, "nt", 0, 0),
                (xn, (tm, d), a_map, wu, (None, fb, d), w_map, "nt", 1, 0)], [], [o, o, o], epi, nacc=2, comm=comm)


def _ffn_down(name, a, wd, resid, tm=512, tn=512, comm=()):
    g, s, fb = a.shape
    d = wd.shape[2]
    tm, tn = _tile(s, tm), _tile(d, tn)

    def epi(accs, ex, out):
        out[0][...] = ex[0][...] + 0.5 * accs[0]

    return _mm(name, (d // tn, s // tm),
               [(a, (g, tm, fb), lambda j, i: (0, i, 0), wd, (g, fb, tn), lambda j, i: (0, 0, j), "nn", 0, g)],
               [(resid, (tm, tn), lambda j, i: (i, j))],
               [((s, d), F32, (tm, tn), lambda j, i: (i, j))], epi, comm=comm)[0]


def _ffn_bwd_act(name, dh, wd, hg, hu, tm=512, comm=()):
    s, d = dh.shape
    g, fb, _ = wd.shape
    tm = _tile(s, tm)

    def epi(accs, ex, out):
        da = 0.5 * accs[0]
        hg_v, hu_v = ex[0][...].astype(F32), ex[1][...].astype(F32)
        sg = _sig(hg_v)
        out[0][...] = (da * hu_v * (sg * (1.0 + hg_v * (1.0 - sg)))).astype(BF)
        out[1][...] = (da * (hg_v * sg)).astype(BF)

    blk = (None, tm, fb)
    gmap = lambda j, i: (j, i, 0)
    return _mm(name, (g, s // tm),
               [(dh, (tm, d), lambda j, i: (i, 0), wd, (None, fb, d), lambda j, i: (j, 0, 0), "nt", 0, 0)],
               [(hg, blk, gmap), (hu, blk, gmap)],
               [((g, s, fb), BF, blk, gmap), ((g, s, fb), BF, blk, gmap)], epi, comm=comm)


def _ffn_bwd_wd(name, a, dh, tn=512, comm=()):
    g, s, fb = a.shape
    d = dh.shape[1]
    tn = _tile(d, tn)

    def epi(accs, ex, out):
        out[0][...] = (0.5 * accs[0]).astype(BF)

    return _mm(name, (g, d // tn),
               [(a, (None, s, fb), lambda j, i: (j, 0, 0), dh, (s, tn), lambda j, i: (0, i), "tn", 0, 0)], [],
               [((g, fb, d), BF, (None, fb, tn), lambda j, i: (j, 0, i))], epi, comm=comm)[0]


def _ffn_bwd_wup(name, xn, dhg, dhu, tk=512, comm=()):
    s, d = xn.shape
    g, _, fb = dhg.shape
    tk = _tile(d, tk)

    def epi(accs, ex, out):
        out[0][...] = accs[0].astype(BF)
        out[1][...] = accs[1].astype(BF)

    a_map = lambda j, i: (j, 0, 0)
    b_map = lambda j, i: (0, i)
    o = ((g, fb, d), BF, (None, fb, tk), lambda j, i: (j, 0, i))
    return _mm(name, (g, d // tk),
               [(dhg, (None, s, fb), a_map, xn, (s, tk), b_map, "tn", 0, 0),
                (dhu, (None, s, fb), a_map, xn, (s, tk), b_map, "tn", 1, 0)], [], [o, o], epi, nacc=2, comm=comm)


def _ffn_bwd_x(name, dhg, dhu, wg, wu, tm=512, tn=512, comm=()):
    g, s, fb = dhg.shape
    d = wg.shape[2]
    tm, tn = _tile(s, tm), _tile(d, tn)
    a_blk, a_map = (g, tm, fb), lambda j, i: (0, i, 0)
    b_blk, b_map = (g, fb, tn), lambda j, i: (0, 0, j)
    return _mm(name, (d // tn, s // tm),
               [(dhg, a_blk, a_map, wg, b_blk, b_map, "nn", 0, g), (dhu, a_blk, a_map, wu, b_blk, b_map, "nn", 0, g)],
               [], [((s, d), F32, (tm, tn), lambda j, i: (i, j))], _store, comm=comm)[0]


def _ffn_forward(tag, h, gain, wg, wu, get_wd, up_comm=(), down_comm=()):
    xn = _rms_fwd(tag + "_norm", h, gain)
    hg, hu, a = _ffn_up(tag + "_up", xn, wg, wu, comm=up_comm)
    return _ffn_down(tag + "_down", a, get_wd(), h, comm=down_comm), (xn, hg, hu, a)


def _na_geometry(rows):
    kh = min(NA_ROWS_WIN, rows)
    cols = np.arange(GRID_W)
    col_start = np.clip(cols - NA_COLS_WIN // 2, 0, GRID_W - NA_COLS_WIN)
    mask = (cols[None, :] >= col_start[:, None]) & (cols[None, :] < col_start[:, None] + NA_COLS_WIN)
    dc = np.clip(cols[None, :] - cols[:, None], -(NA_COLS_WIN - 1), NA_COLS_WIN - 1) + (NA_COLS_WIN - 1)
    return kh, mask, dc


def _na_table(rpb, rows):
    _, mask, dc = _na_geometry(rows)
    return jnp.where(jnp.asarray(mask)[None, None], rpb[:, :, dc], NEG)


class _NaPlan:
    def __init__(self, s):
        self.s, self.rows = s, s // GRID_W
        self.kh = min(NA_ROWS_WIN, self.rows)
        self.qr = min(NA_QROWS, self.rows)
        self.kr = min(self.rows, self.kh + self.qr - 1)
        self.groups = self.rows // self.qr
        self.nd = 2 * NA_ROWS_WIN - 1
        self.hw, self.nq = NA_HG * NA_DIM, NA_HEADS // NA_HG
        clip = lambda v, hi: min(max(v, 0), hi)
        pats = [(clip(g * self.qr - self.kh // 2, self.rows - self.kr) - g * self.qr,)
                + tuple(clip(g * self.qr + a - self.kh // 2, self.rows - self.kh) - g * self.qr for a in range(self.qr))
                for g in range(self.groups)]
        self.rebuild = [g for g in range(self.groups) if g == 0 or pats[g] != pats[g - 1]]

    def first_key_row(self, g):
        return jnp.clip(g * self.qr - self.kh // 2, 0, self.rows - self.kr)

    def specs(self):
        blk = pl.BlockSpec((self.qr * GRID_W, self.hw), lambda j, g: (g, j))
        k_spec = pl.BlockSpec((self.s, self.hw), lambda j, g: (0, self.nq + j))
        v_spec = pl.BlockSpec((self.s, self.hw), lambda j, g: (0, 2 * self.nq + j))
        t_spec = pl.BlockSpec((NA_HG, self.nd, GRID_W, GRID_W), lambda j, g: (j, 0, 0, 0))
        return blk, k_spec, v_spec, t_spec

    def bias_scratch(self):
        return pltpu.VMEM((NA_HG, self.qr * GRID_W, self.kr * GRID_W), F32)

    def fill_bias(self, t_ref, bias_ref, g):
        def build():
            r0, ks = g * self.qr, self.first_key_row(g)
            for a in range(self.qr):
                rs = jnp.clip(r0 + a - self.kh // 2, 0, self.rows - self.kh)
                for i in range(self.kr):
                    valid = jnp.logical_and(ks + i >= rs, ks + i < rs + self.kh)
                    idx = jnp.clip(ks + i - r0 - a + NA_ROWS_WIN - 1, 0, self.nd - 1)
                    for h in range(NA_HG):
                        bias_ref[h, a * GRID_W:(a + 1) * GRID_W, i * GRID_W:(i + 1) * GRID_W] = jnp.where(
                            valid, t_ref[h, idx], NEG)

        pl.when(functools.reduce(jnp.logical_or, [g == r for r in self.rebuild]))(build)

    def window(self, g):
        return pl.ds(pl.multiple_of(self.first_key_row(g) * GRID_W, GRID_W), self.kr * GRID_W)


def _na_probs(q, k, bias):
    sc = lax.dot_general(q, k, _DN["nt"], preferred_element_type=F32) * (NA_DIM ** -0.5) + bias
    e = jnp.exp(sc - jnp.max(sc, axis=-1, keepdims=True))
    return e / jnp.sum(e, axis=-1, keepdims=True)


def _na_fwd(qkv, table, comm=()):
    plan = _NaPlan(qkv.shape[0])
    blk, k_spec, v_spec, t_spec = plan.specs()

    def body(q_ref, k_ref, v_ref, t_ref, o_ref, bias_ref):
        g = pl.program_id(1)
        plan.fill_bias(t_ref, bias_ref, g)
        win = plan.window(g)
        for h in range(NA_HG):
            cs = slice(h * NA_DIM, (h + 1) * NA_DIM)
            p = _na_probs(q_ref[:, cs], k_ref[win, cs], bias_ref[h])
            o_ref[:, cs] = jnp.dot(p.astype(BF), v_ref[win, cs], preferred_element_type=F32).astype(BF)

    return _call("na_fwd", body, (plan.nq, plan.groups), [blk, k_spec, v_spec, t_spec], blk,
                 jax.ShapeDtypeStruct((plan.s, NA_HEADS * NA_DIM), BF), [qkv, qkv, qkv, table], comm,
                 scratch=[plan.bias_scratch()])


def _na_bwd(qkv, table, do, comm=()):
    plan = _NaPlan(qkv.shape[0])
    blk, k_spec, v_spec, t_spec = plan.specs()
    qr, kr = plan.qr, plan.kr

    def body(q_ref, k_ref, v_ref, t_ref, do_ref, dq_ref, dk_ref, dv_ref, dt_ref, bias_ref):
        g = pl.program_id(1)

        @pl.when(g == 0)
        def _():
            dk_ref[...] = jnp.zeros_like(dk_ref)
            dv_ref[...] = jnp.zeros_like(dv_ref)
            dt_ref[...] = jnp.zeros_like(dt_ref)

        plan.fill_bias(t_ref, bias_ref, g)
        win = plan.window(g)
        base = plan.first_key_row(g) - g * qr + NA_ROWS_WIN - 1
        for h in range(NA_HG):
            cs = slice(h * NA_DIM, (h + 1) * NA_DIM)
            q, k, v, do_h = q_ref[:, cs], k_ref[win, cs], v_ref[win, cs], do_ref[:, cs]
            p = _na_probs(q, k, bias_ref[h])
            dp = lax.dot_general(do_h, v, _DN["nt"], preferred_element_type=F32)
            ds = p * (dp - jnp.sum(p * dp, axis=-1, keepdims=True))
            for dlt in range(1 - qr, kr):
                tiles = [ds[a * GRID_W:(a + 1) * GRID_W, (a + dlt) * GRID_W:(a + dlt + 1) * GRID_W]
                         for a in range(qr) if 0 <= a + dlt < kr]
                dt_ref[h, jnp.clip(base + dlt, 0, plan.nd - 1)] += functools.reduce(jnp.add, tiles)
            dsb = (ds * (NA_DIM ** -0.5)).astype(BF)
            dq_ref[:, cs] = jnp.dot(dsb, k, preferred_element_type=F32).astype(BF)
            dk_ref[win, cs] += lax.dot_general(dsb, q, _DN["tn"], preferred_element_type=F32)
            dv_ref[win, cs] += lax.dot_general(p.astype(BF), do_h, _DN["tn"], preferred_element_type=F32)

    width = NA_HEADS * NA_DIM
    whole = pl.BlockSpec((plan.s, plan.hw), lambda j, g: (0, j))
    return _call(
        "na_bwd", body, (plan.nq, plan.groups), [blk, k_spec, v_spec, t_spec, blk], [blk, whole, whole, t_spec],
        [jax.ShapeDtypeStruct((plan.s, width), BF), jax.ShapeDtypeStruct((plan.s, width), F32),
         jax.ShapeDtypeStruct((plan.s, width), F32),
         jax.ShapeDtypeStruct((NA_HEADS, plan.nd, GRID_W, GRID_W), F32)],
        [qkv, qkv, qkv, table, do], comm, scratch=[plan.bias_scratch()])


def _na_rpb_grad(dt, rows):
    _, mask, dc = _na_geometry(rows)
    nd, nc = 2 * NA_ROWS_WIN - 1, 2 * NA_COLS_WIN - 1
    onehot = np.zeros((GRID_W * GRID_W, 128), np.float32)
    onehot[np.arange(GRID_W * GRID_W), dc.reshape(-1)] = mask.reshape(-1).astype(np.float32)
    flat = dt.reshape(NA_HEADS * nd, GRID_W * GRID_W)

    def body(a_ref, e_ref, o_ref):
        o_ref[...] = jnp.dot(a_ref[...], e_ref[...], precision=HI, preferred_element_type=F32)

    out = pl.pallas_call(body, name="na_rpb_grad", out_shape=jax.ShapeDtypeStruct((NA_HEADS * nd, 128), F32),
                         compiler_params=_params(0))(flat, jnp.asarray(onehot))
    return out[:, :nc].reshape(NA_HEADS, nd, nc)


def _rope_consts(s):
    pos = np.arange(s, dtype=np.float32)
    inv = (1.0 / (ROPE_THETA ** (np.arange(0, ML_ROPE, 2, dtype=np.float32) / ML_ROPE))).astype(np.float32)
    ang = pos[:, None] * inv[None, :]
    cos, sin = np.cos(ang).astype(np.float32), np.sin(ang).astype(np.float32)
    half = ML_ROPE // 2
    rot = np.zeros((ML_ROPE, ML_ROPE), np.float32)
    rot[np.arange(half) + half, np.arange(half)] = -1.0
    rot[np.arange(half), np.arange(half) + half] = 1.0
    return (jnp.asarray(np.concatenate([cos, cos], 1)), jnp.asarray(np.concatenate([sin, sin], 1)),
            jnp.asarray(rot), jnp.asarray(rot.T.copy()))


def _rope(v, cos, sin, rot):
    return v * cos + jnp.dot(v, rot, precision=HI, preferred_element_type=F32) * sin


def _unrope(dv, cos, sin, rot_t):
    return dv * cos + jnp.dot(dv * sin, rot_t, precision=HI, preferred_element_type=F32)


def _rms(v, g):
    return v * lax.rsqrt(jnp.mean(v * v, axis=-1, keepdims=True) + EPS) * g


def _mla_prep(lat, gq, gkv, cos, sin, rot, tm=256):
    s, w = lat.shape
    tm = _tile(s, tm)

    def body(l_ref, gq_ref, gkv_ref, c_ref, s_ref, r_ref, cq_ref, ckv_ref, kr_ref):
        cq_ref[...] = _rms(l_ref[:, :ML_RANK], gq_ref[...]).astype(BF)
        ckv_ref[...] = _rms(l_ref[:, ML_RANK:2 * ML_RANK], gkv_ref[...]).astype(BF)
        kr_ref[...] = _rope(l_ref[:, 2 * ML_RANK:], c_ref[...], s_ref[...], r_ref[...]).astype(BF)

    row = lambda c: pl.BlockSpec((tm, c), lambda i: (i, 0))
    full = lambda a: pl.BlockSpec(a.shape, lambda i: (0, 0))
    return pl.pallas_call(
        body, name="mla_prep", grid=(s // tm,),
        in_specs=[row(w), full(gq), full(gkv), row(ML_ROPE), row(ML_ROPE), full(rot)],
        out_specs=[row(ML_RANK), row(ML_RANK), row(ML_ROPE)],
        out_shape=[jax.ShapeDtypeStruct((s, ML_RANK), BF), jax.ShapeDtypeStruct((s, ML_RANK), BF),
                   jax.ShapeDtypeStruct((s, ML_ROPE), BF)],
        compiler_params=_params(1))(lat, gq, gkv, cos, sin, rot)


def _mla_q_proj(cq, wuq, cos, sin, rot, tm=512, comm=()):
    s, k = cq.shape
    tm = _tile(s, tm)

    def epi(accs, ex, out):
        acc = accs[0]
        out[0][:, :ML_NOPE] = acc[:, :ML_NOPE].astype(BF)
        out[0][:, ML_NOPE:] = _rope(acc[:, ML_NOPE:], ex[0][...], ex[1][...], ex[2][...]).astype(BF)

    rmap = lambda j, i: (i, 0)
    return _mm("mla_q_proj", (ML_HEADS, s // tm),
               [(cq, (tm, k), rmap, wuq, (None, ML_QK, k), lambda j, i: (j, 0, 0), "nt", 0, 0)],
               [(cos, (tm, ML_ROPE), rmap), (sin, (tm, ML_ROPE), rmap), (rot, rot.shape, lambda j, i: (0, 0))],
               [((ML_HEADS, s, ML_QK), BF, (None, tm, ML_QK), lambda j, i: (j, i, 0))], epi, comm=comm)[0]


def _mla_kv_proj(ckv, wukv, kr, tm=512, comm=()):
    s, k = ckv.shape
    tm = _tile(s, tm)

    def epi(accs, ex, out):
        acc = accs[0]
        out[0][:, :ML_NOPE] = acc[:, :ML_NOPE].astype(BF)
        out[0][:, ML_NOPE:] = ex[0][...]
        out[1][...] = acc[:, ML_NOPE:].astype(BF)

    rmap = lambda j, i: (i, 0)
    gmap = lambda j, i: (j, i, 0)
    return _mm("mla_kv_proj", (ML_HEADS, s // tm),
               [(ckv, (tm, k), rmap, wukv, (None, k, ML_NOPE + ML_V), lambda j, i: (j, 0, 0), "nn", 0, 0)],
               [(kr, (tm, ML_ROPE), rmap)],
               [((ML_HEADS, s, ML_QK), BF, (None, tm, ML_QK), gmap), ((ML_HEADS, s, ML_V), BF, (None, tm, ML_V), gmap)],
               epi, comm=comm)


def _mla_probs(q, k):
    sc = lax.dot_general(q, k, _DN["nt"], preferred_element_type=F32) * (ML_QK ** -0.5)
    e = jnp.exp(sc - jnp.max(sc, axis=-1, keepdims=True))
    return e / jnp.sum(e, axis=-1, keepdims=True)


def _mla_fwd(q, k, v, tq=512, comm=()):
    _, s, _ = q.shape
    tq = _tile(s, tq)

    def body(q_ref, k_ref, v_ref, o_ref):
        p = _mla_probs(q_ref[...], k_ref[...])
        o_ref[...] = jnp.dot(p.astype(BF), v_ref[...], preferred_element_type=F32).astype(BF)

    return _call("mla_fwd", body, (ML_HEADS, s // tq),
                 [pl.BlockSpec((None, tq, ML_QK), lambda h, i: (h, i, 0)),
                  pl.BlockSpec((None, s, ML_QK), lambda h, i: (h, 0, 0)),
                  pl.BlockSpec((None, s, ML_V), lambda h, i: (h, 0, 0))],
                 pl.BlockSpec((tq, ML_V), lambda h, i: (i, h)),
                 jax.ShapeDtypeStruct((s, ML_HEADS * ML_V), BF), [q, k, v], comm)


def _mla_bwd(q, k, v, do, tq=256, comm=()):
    _, s, _ = q.shape
    tq = _tile(s, tq)

    def body(q_ref, k_ref, v_ref, do_ref, dq_ref, dk_ref, dv_ref):
        i = pl.program_id(1)
        qv, kv, vv, dov = q_ref[...], k_ref[...], v_ref[...], do_ref[...]
        p = _mla_probs(qv, kv)
        dp = lax.dot_general(dov, vv, _DN["nt"], preferred_element_type=F32)
        ds = (p * (dp - jnp.sum(p * dp, axis=-1, keepdims=True)) * (ML_QK ** -0.5)).astype(BF)
        dq_ref[...] = jnp.dot(ds, kv, preferred_element_type=F32)
        _acc_rows(dk_ref, lax.dot_general(ds, qv, _DN["tn"], preferred_element_type=F32), i)
        _acc_rows(dv_ref, lax.dot_general(p.astype(BF), dov, _DN["tn"], preferred_element_type=F32), i)

    return _call(
        "mla_bwd", body, (ML_HEADS, s // tq),
        [pl.BlockSpec((None, tq, ML_QK), lambda h, i: (h, i, 0)),
         pl.BlockSpec((None, s, ML_QK), lambda h, i: (h, 0, 0)),
         pl.BlockSpec((None, s, ML_V), lambda h, i: (h, 0, 0)),
         pl.BlockSpec((tq, ML_V), lambda h, i: (i, h))],
        [pl.BlockSpec((None, tq, ML_QK), lambda h, i: (h, i, 0)),
         pl.BlockSpec((None, s, ML_QK), lambda h, i: (h, 0, 0)),
         pl.BlockSpec((None, s, ML_V), lambda h, i: (h, 0, 0))],
        [jax.ShapeDtypeStruct((ML_HEADS, s, ML_QK), F32), jax.ShapeDtypeStruct((ML_HEADS, s, ML_QK), F32),
         jax.ShapeDtypeStruct((ML_HEADS, s, ML_V), F32)],
        [q, k, v, do], comm)


def _mla_post(dq, dk, dv, cos, sin, rot_t, tm=256):
    _, s, _ = dq.shape
    tm = _tile(s, tm)

    def body(dq_ref, dk_ref, dv_ref, c_ref, s_ref, r_ref, dqp_ref, dkv_ref, dkr_ref):
        h = pl.program_id(1)
        dqv, dkk = dq_ref[...], dk_ref[...]
        dqp_ref[:, :ML_NOPE] = dqv[:, :ML_NOPE].astype(BF)
        dqp_ref[:, ML_NOPE:] = _unrope(dqv[:, ML_NOPE:], c_ref[...], s_ref[...], r_ref[...]).astype(BF)
        dkv_ref[:, :ML_NOPE] = dkk[:, :ML_NOPE].astype(BF)
        dkv_ref[:, ML_NOPE:] = dv_ref[...].astype(BF)
        _acc_rows(dkr_ref, dkk[:, ML_NOPE:], h)

    gspec = lambda c: pl.BlockSpec((None, tm, c), lambda i, h: (h, i, 0))
    rspec = pl.BlockSpec((tm, ML_ROPE), lambda i, h: (i, 0))
    return pl.pallas_call(
        body, name="mla_post", grid=(s // tm, ML_HEADS),
        in_specs=[gspec(ML_QK), gspec(ML_QK), gspec(ML_V), rspec, rspec,
                  pl.BlockSpec(rot_t.shape, lambda i, h: (0, 0))],
        out_specs=[gspec(ML_QK), gspec(ML_NOPE + ML_V), rspec],
        out_shape=[jax.ShapeDtypeStruct((ML_HEADS, s, ML_QK), BF),
                   jax.ShapeDtypeStruct((ML_HEADS, s, ML_NOPE + ML_V), BF),
                   jax.ShapeDtypeStruct((s, ML_ROPE), F32)],
        compiler_params=_params(2))(dq, dk, dv, cos, sin, rot_t)


def _mla_lat_bwd(dcq, dckv, dkr, lat, gq, gkv, cos, sin, rot_t, tm=256):
    s, w = lat.shape
    tm = _tile(s, tm)

    def body(dcq_ref, dckv_ref, dkr_ref, l_ref, gq_ref, gkv_ref, c_ref, s_ref, r_ref, dl_ref, dgq_ref, dgkv_ref):
        i = pl.program_id(0)
        dql, pq = _rms_bwd_math(dcq_ref[...], l_ref[:, :ML_RANK], gq_ref[...])
        dkl, pkv = _rms_bwd_math(dckv_ref[...], l_ref[:, ML_RANK:2 * ML_RANK], gkv_ref[...])
        dl_ref[:, :ML_RANK] = dql.astype(BF)
        dl_ref[:, ML_RANK:2 * ML_RANK] = dkl.astype(BF)
        dl_ref[:, 2 * ML_RANK:] = _unrope(dkr_ref[...], c_ref[...], s_ref[...], r_ref[...]).astype(BF)
        _acc_rows(dgq_ref, pq, i)
        _acc_rows(dgkv_ref, pkv, i)

    row = lambda c: pl.BlockSpec((tm, c), lambda i: (i, 0))
    full = lambda a: pl.BlockSpec(a.shape, lambda i: (0, 0))
    return pl.pallas_call(
        body, name="mla_lat_bwd", grid=(s // tm,),
        in_specs=[row(ML_RANK), row(ML_RANK), row(ML_ROPE), row(w), full(gq), full(gkv), row(ML_ROPE), row(ML_ROPE),
                  full(rot_t)],
        out_specs=[row(w), full(gq), full(gkv)],
        out_shape=[jax.ShapeDtypeStruct((s, w), BF), jax.ShapeDtypeStruct(gq.shape, F32),
                   jax.ShapeDtypeStruct(gkv.shape, F32)],
        compiler_params=_params(1))(dcq, dckv, dkr, lat, gq, gkv, cos, sin, rot_t)


def _grp_dw(name, a, dout, ta=512):
    s, k = a.shape
    ta = _tile(k, ta)
    if dout.ndim == 3:
        g, _, nb = dout.shape
        b_blk, b_map = (None, s, nb), lambda j, i: (j, 0, 0)
    else:
        g, nb = NDEV, dout.shape[1] // NDEV
        b_blk, b_map = (s, nb), lambda j, i: (0, j)
    return _mm(name, (g, k // ta),
               [(a, (s, ta), lambda j, i: (0, i), dout, b_blk, b_map, "tn", 0, 0)], [],
               [((g, k, nb), BF, (None, ta, nb), lambda j, i: (j, i, 0))], _store)[0]


def _grp_dw_t(name, dout, a, ta=512):
    g, s, nb = dout.shape
    k = a.shape[1]
    ta = _tile(k, ta)
    return _mm(name, (g, k // ta),
               [(dout, (None, s, nb), lambda j, i: (j, 0, 0), a, (s, ta), lambda j, i: (0, i), "tn", 0, 0)], [],
               [((g, nb, k), BF, (None, nb, ta), lambda j, i: (j, 0, i))], _store)[0]


def _grp_dx_t(name, dout, wt, tm=512, tn=512, comm=()):
    g, s, nb = dout.shape
    k = wt.shape[2]
    tm, tn = _tile(s, tm), _tile(k, tn)
    return _mm(name, (k // tn, s // tm),
               [(dout, (g, tm, nb), lambda j, i: (0, i, 0), wt, (g, nb, tn), lambda j, i: (0, 0, j), "nn", 0, g)], [],
               [((s, k), F32, (tm, tn), lambda j, i: (i, j))], _store, comm=comm)[0]


def _grp_dx(name, dout, w, tm=512, tn=512, comm=()):
    g, s, nb = dout.shape
    k = w.shape[1]
    tm, tn = _tile(s, tm), _tile(k, tn)
    return _mm(name, (k // tn, s // tm),
               [(dout, (g, tm, nb), lambda j, i: (0, i, 0), w, (g, tn, nb), lambda j, i: (0, j, 0), "nt", 0, g)], [],
               [((s, k), F32, (tm, tn), lambda j, i: (i, j))], _store, comm=comm)[0]


def _row_dw(name, a, dout, tn=512):
    s, n = dout.shape
    tn = _tile(n, tn)
    if a.ndim == 3:
        kb = a.shape[2]
        a_blk, a_map = (None, s, kb), lambda j, i: (j, 0, 0)
    else:
        kb = a.shape[1] // NDEV
        a_blk, a_map = (s, kb), lambda j, i: (0, j)
    return _mm(name, (NDEV, n // tn),
               [(a, a_blk, a_map, dout, (s, tn), lambda j, i: (0, i), "tn", 0, 0)], [],
               [((NDEV, kb, n), BF, (None, kb, tn), lambda j, i: (j, 0, i))], _store)[0]


def _mix_merge(oa, ob, wa, wb, ga, gb, tm=512, comm=()):
    s, k = oa.shape
    g, _, nb = wa.shape
    tm = _tile(s, tm)

    def epi(accs, ex, out):
        ya, yb = accs
        out[0][...] = ya.astype(BF)
        out[1][...] = yb.astype(BF)
        out[2][...] = (_sig(ex[0][...]) * ya + _sig(ex[1][...]) * yb).astype(BF)

    rmap = lambda j, i: (i, 0)
    wmap = lambda j, i: (j, 0, 0)
    o = ((g, s, nb), BF, (None, tm, nb), lambda j, i: (j, i, 0))
    cmap = lambda j, i: (i, j)
    return _mm("mix_merge", (g, s // tm),
               [(oa, (tm, k), rmap, wa, (None, k, nb), wmap, "nn", 0, 0),
                (ob, (tm, k), rmap, wb, (None, k, nb), wmap, "nn", 1, 0)],
               [(ga, (tm, nb), cmap), (gb, (tm, nb), cmap)], [o, o, o], epi, nacc=2, comm=comm)


def _mix_out(merged, wout, resid, tm=512, tn=512):
    g, s, kb = merged.shape
    d = wout.shape[2]
    tm, tn = _tile(s, tm), _tile(d, tn)

    def epi(accs, ex, out):
        out[0][...] = ex[0][...] + accs[0]

    return _mm("mix_out", (d // tn, s // tm),
               [(merged, (g, tm, kb), lambda j, i: (0, i, 0), wout, (g, kb, tn), lambda j, i: (0, 0, j), "nn", 0, g)],
               [(resid, (tm, tn), lambda j, i: (i, j))],
               [((s, d), F32, (tm, tn), lambda j, i: (i, j))], epi)[0]


def _mix_out_bwd(dh, wout, ga, gb, ya, yb, tm=512, comm=()):
    s, d = dh.shape
    g, kb, _ = wout.shape
    tm = _tile(s, tm)

    def epi(accs, ex, out):
        dm = accs[0]
        sa, sb = _sig(ex[0][...]), _sig(ex[1][...])
        out[0][...] = (dm * sa).astype(BF)
        out[1][...] = (dm * sb).astype(BF)
        out[2][...] = (dm * ex[2][...].astype(F32) * sa * (1.0 - sa)).astype(BF)
        out[3][...] = (dm * ex[3][...].astype(F32) * sb * (1.0 - sb)).astype(BF)

    cmap = lambda j, i: (i, j)
    gmap = lambda j, i: (j, i, 0)
    og = ((g, s, kb), BF, (None, tm, kb), gmap)
    oc = ((s, g * kb), BF, (tm, kb), cmap)
    return _mm("mix_out_bwd", (g, s // tm),
               [(dh, (tm, d), lambda j, i: (i, 0), wout, (None, kb, d), lambda j, i: (j, 0, 0), "nt", 0, 0)],
               [(ga, (tm, kb), cmap), (gb, (tm, kb), cmap), (ya, (None, tm, kb), gmap), (yb, (None, tm, kb), gmap)],
               [og, og, oc, oc], epi, comm=comm)


def _pl_forward(n4, wplg, p, wpl, h3, tm=512):
    s, d = n4.shape
    g, kb, _ = wplg.shape
    kp, nb = wpl.shape[1], wpl.shape[2]
    tm = _tile(s, tm)
    wplg_nat = wplg.reshape(g * kb, d)

    def epi(accs, ex, out):
        t, pe = accs
        out[0][...] = ex[0][...] + _sig(t) * pe
        out[1][...] = t
        out[2][...] = pe.astype(BF)

    rmap = lambda j, i: (i, 0)
    cmap = lambda j, i: (i, j)
    return _mm("pl_forward", (g, s // tm),
               [(n4, (tm, d), rmap, wplg_nat, (g * kb, nb), lambda j, i: (0, j), "nn", 0, 0),
                (p, (tm, kp), rmap, wpl, (None, kp, nb), lambda j, i: (j, 0, 0), "nn", 1, 0)],
               [(h3, (tm, nb), cmap)],
               [((s, d), F32, (tm, nb), cmap), ((s, d), F32, (tm, nb), cmap), ((s, d), BF, (tm, nb), cmap)],
               epi, nacc=2)


def _row_dx(name, dout, w, tm=512, comm=()):
    s, n = dout.shape
    g, kb, _ = w.shape
    tm = _tile(s, tm)
    return _mm(name, (g, s // tm),
               [(dout, (tm, n), lambda j, i: (i, 0), w, (None, kb, n), lambda j, i: (j, 0, 0), "nt", 0, 0)], [],
               [((s, g * kb), F32, (tm, kb), lambda j, i: (i, j))], _store, comm=comm)[0]


def _in_proj_bwd_x(pieces, weights, tm=512, tn=512, comm=()):
    s = pieces[0].shape[0]
    d = weights[0].shape[1]
    tm, tn = _tile(s, tm), _tile(d, tn)
    prods = [(pc, (tm, pc.shape[1]), lambda j, i: (i, 0), w, (w.shape[0], tn), lambda j, i: (0, j), "nn", 0, 0)
             for pc, w in zip(pieces, weights)]
    return _mm("in_proj_dx", (d // tn, s // tm), prods, [],
               [((s, d), F32, (tm, tn), lambda j, i: (i, j))], _store, comm=comm)[0]


def _split_w_in(w_in_t):
    g, nb, d = w_in_t.shape
    nat = w_in_t.reshape(g * nb, d)
    na, lat = 3 * NA_HEADS * NA_DIM, 2 * ML_RANK + ML_ROPE
    return nat[:na], nat[na:na + lat], nat[na + lat:na + lat + d], nat[na + lat + d:]


def _pair_sum(name, part, landed, core):
    _, _, r, c = part.shape
    tr, tc = _ew_tile(r, c)

    def body(core_ref, a_ref, b_ref, o_ref):
        o_ref[...] = (a_ref[...].astype(F32) + b_ref[...].astype(F32)).astype(o_ref.dtype)

    return pl.pallas_call(
        body, name=name,
        grid_spec=pltpu.PrefetchScalarGridSpec(
            num_scalar_prefetch=1, grid=(NCHIP, r // tr, c // tc),
            in_specs=[pl.BlockSpec((None, None, tr, tc), lambda j, i, k, core_ref: (j, core_ref[0], i, k)),
                      pl.BlockSpec((None, tr, tc), lambda j, i, k, core_ref: (j, i, k))],
            out_specs=pl.BlockSpec((None, tr, tc), lambda j, i, k, core_ref: (j, i, k))),
        out_shape=jax.ShapeDtypeStruct(landed.shape, landed.dtype), compiler_params=_params(3),
    )(core, part, landed)


def _device_step(x, p, target, sp, own, core):
    s, d = x.shape
    rows = s // GRID_W
    cos, sin, rot, rot_t = _rope_consts(s)
    w, dw4, sums, dsp, pending = {}, {}, {}, {}, []

    def gather(*names):
        return _GatherPart(names, [own[n] for n in names])

    def got(part):
        w.update(zip(part.names, part.results))

    def grad(name, g):
        dw4[name] = g.reshape((NCHIP, 2) + g.shape[1:])

    def to_sibling(*names):
        return _SiblingPart(names, [dw4[n] for n in names])

    def add_pairs(part):
        for n, landed in zip(part.names, part.results):
            sums[n] = _pair_sum("pair_sum_" + n, dw4[n], landed, core)

    def start_chips(tag, *names):
        send, recv, thru, lands, _ = _chips_start("rs_start_" + tag, [sums[n] for n in names])
        pending.append((tag, names, send, recv, thru, lands))

    c0 = gather("ffn1_w_gate", "ffn1_w_up")
    _comm_only("gather_ffn1", [c0])
    got(c0)
    c1 = gather("ffn1_w_down")
    c2 = gather("w_in")

    def ffn1_wd():
        got(c1)
        return w["ffn1_w_down"]

    h1, ffn1_saved = _ffn_forward("ffn1", x, sp["ffn1_norm"], w["ffn1_w_gate"], w["ffn1_w_up"], ffn1_wd,
                                  up_comm=[c1], down_comm=[c2])
    got(c2)
    wqkv, wlat, wga, wgb = _split_w_in(w["w_in"])
    u = _rms_fwd("mix_norm", h1, sp["mix_norm"])
    c3 = gather("w_uq", "w_ukv")
    qkv = _mm_nt("in_qkv", u, wqkv, BF, tn=1024, comm=[c3])
    got(c3)
    lat = _mm_nt("in_lat", u, wlat, F32)
    c3a = gather("w_branch_a")
    ga = _mm_nt("in_ga", u, wga, F32, tn=1024, comm=[c3a])
    got(c3a)
    c3b = gather("w_branch_b")
    gb = _mm_nt("in_gb", u, wgb, F32, tn=1024, comm=[c3b])
    got(c3b)
    tb = _na_table(sp["na_rpb"], rows)
    c4 = gather("ffn2_w_gate")
    oa = _na_fwd(qkv, tb, comm=[c4])
    got(c4)
    cq, ckv, kr = _mla_prep(lat, sp["q_a_norm"], sp["kv_a_norm"], cos, sin, rot)
    c4a = gather("w_out")
    qf = _mla_q_proj(cq, w["w_uq"], cos, sin, rot, comm=[c4a])
    got(c4a)
    c4b = gather("w_pl_gate")
    kf, vf = _mla_kv_proj(ckv, w["w_ukv"], kr, comm=[c4b])
    got(c4b)
    c5 = gather("ffn2_w_up")
    ob = _mla_fwd(qf, kf, vf, comm=[c5])
    got(c5)
    c5a = gather("w_pl")
    ya, yb, merged = _mix_merge(oa, ob, w["w_branch_a"], w["w_branch_b"], ga, gb, comm=[c5a])
    got(c5a)
    h2 = _mix_out(merged, w["w_out"], h1)
    c6 = gather("ffn2_w_down")

    def ffn2_wd():
        got(c6)
        return w["ffn2_w_down"]

    h3, ffn2_saved = _ffn_forward("ffn2", h2, sp["ffn2_norm"], w["ffn2_w_gate"], w["ffn2_w_up"], ffn2_wd,
                                  up_comm=[c6])
    n4 = _rms_fwd("pl_norm", h3, sp["pl_norm"])
    pb = p.astype(BF)
    h4, t, pe = _pl_forward(n4, w["w_pl_gate"], pb, w["w_pl"], h3)

    dh4, dsp["final_norm"], loss = _loss_head(h4, target, sp["final_norm"])
    dt, dpe = _pl_bwd_elem(dh4, pe, t)
    grad("w_pl", _grp_dw("pl_dw", pb, dpe))
    grad("w_pl_gate", _row_dw("plg_dw", n4, dt))
    s1 = to_sibling("w_pl", "w_pl_gate")
    dn4 = _row_dx("plg_dx", dt, w["w_pl_gate"], comm=[s1])
    add_pairs(s1)
    start_chips("pl", "w_pl", "w_pl_gate")
    dh3, dsp["pl_norm"] = _rms_bwd("pl_dnorm", dn4, h3, sp["pl_norm"], dh4)

    xn, hg, hu, a = ffn2_saved
    dhb = dh3.astype(BF)
    grad("ffn2_w_down", _ffn_bwd_wd("ffn2_dwd", a, dhb))
    s2 = to_sibling("ffn2_w_down")
    dhg, dhu = _ffn_bwd_act("ffn2_dact", dhb, w["ffn2_w_down"], hg, hu, comm=[s2])
    add_pairs(s2)
    start_chips("ffn2_down", "ffn2_w_down")
    dwg, dwu = _ffn_bwd_wup("ffn2_dwup", xn, dhg, dhu)
    grad("ffn2_w_gate", dwg)
    grad("ffn2_w_up", dwu)
    s3 = to_sibling("ffn2_w_gate", "ffn2_w_up")
    dxn = _ffn_bwd_x("ffn2_dx", dhg, dhu, w["ffn2_w_gate"], w["ffn2_w_up"], comm=[s3])
    add_pairs(s3)
    start_chips("ffn2_up", "ffn2_w_gate", "ffn2_w_up")
    dh2, dsp["ffn2_norm"] = _rms_bwd("ffn2_dnorm", dxn, h2, sp["ffn2_norm"], dh3)

    dh2b = dh2.astype(BF)
    grad("w_out", _row_dw("out_dw", merged, dh2b))
    s4 = to_sibling("w_out")
    dya, dyb, dga, dgb = _mix_out_bwd(dh2b, w["w_out"], ga, gb, ya, yb, comm=[s4])
    add_pairs(s4)
    grad("w_branch_a", _grp_dw("bra_dw", oa, dya))
    grad("w_branch_b", _grp_dw("brb_dw", ob, dyb))
    doa = _grp_dx("bra_dx", dya, w["w_branch_a"]).astype(BF)
    s5 = to_sibling("w_branch_a", "w_branch_b")
    dob = _grp_dx("brb_dx", dyb, w["w_branch_b"], comm=[s5]).astype(BF)
    add_pairs(s5)
    start_chips("mix", "w_out", "w_branch_a", "w_branch_b")

    dqf, dkf, dvf = _mla_bwd(qf, kf, vf, dob)
    dqp, dkv, dkr = _mla_post(dqf, dkf, dvf, cos, sin, rot_t)
    grad("w_uq", _grp_dw_t("uq_dw", dqp, cq))
    grad("w_ukv", _grp_dw("ukv_dw", ckv, dkv))
    dcq = _grp_dx_t("uq_dx", dqp, w["w_uq"])
    s6 = to_sibling("w_uq", "w_ukv")
    dckv = _grp_dx("ukv_dx", dkv, w["w_ukv"], comm=[s6])
    add_pairs(s6)
    start_chips("mla", "w_uq", "w_ukv")
    dlat, dsp["q_a_norm"], dsp["kv_a_norm"] = _mla_lat_bwd(dcq, dckv, dkr, lat, sp["q_a_norm"], sp["kv_a_norm"],
                                                         cos, sin, rot_t)
    dq_na, dk_na, dv_na, dtab = _na_bwd(qkv, tb, doa)
    dsp["na_rpb"] = _na_rpb_grad(dtab, rows)
    dqkv = jnp.concatenate([dq_na, dk_na.astype(BF), dv_na.astype(BF)], axis=1)

    pieces = [dqkv, dlat, dga, dgb]
    dwin = jnp.zeros((sum(pc.shape[1] for pc in pieces), d), BF)
    row0 = 0
    for i, pc in enumerate(pieces):
        dwin = _mm_tn_into("in_dw%d" % i, pc, u, dwin, row0)
        row0 += pc.shape[1]
    grad("w_in", dwin.reshape(NDEV, -1, d))
    s7 = to_sibling("w_in")
    du = _in_proj_bwd_x(pieces, [wqkv, wlat, wga, wgb], comm=[s7])
    add_pairs(s7)
    start_chips("w_in", "w_in")
    dh1, dsp["mix_norm"] = _rms_bwd("mix_dnorm", du, h1, sp["mix_norm"], dh2)

    xn, hg, hu, a = ffn1_saved
    dhb = dh1.astype(BF)
    grad("ffn1_w_down", _ffn_bwd_wd("ffn1_dwd", a, dhb))
    s8 = to_sibling("ffn1_w_down")
    dhg, dhu = _ffn_bwd_act("ffn1_dact", dhb, w["ffn1_w_down"], hg, hu, comm=[s8])
    add_pairs(s8)
    start_chips("ffn1_down", "ffn1_w_down")
    dwg, dwu = _ffn_bwd_wup("ffn1_dwup", xn, dhg, dhu)
    grad("ffn1_w_gate", dwg)
    grad("ffn1_w_up", dwu)
    s9 = to_sibling("ffn1_w_gate", "ffn1_w_up")
    _comm_only("rs_sibling_ffn1", [s9])
    add_pairs(s9)
    start_chips("ffn1_up", "ffn1_w_gate", "ffn1_w_up")
    dxn = _ffn_bwd_x("ffn1_dx", dhg, dhu, w["ffn1_w_gate"], w["ffn1_w_up"])
    dx, dsp["ffn1_norm"] = _rms_bwd("ffn1_dnorm", dxn, x, sp["ffn1_norm"], dh1)
    return loss, dx, pending, dsp


def _gather_small(buf):
    def body(in_ref, out_ref, send_sems, recv_sems, local_sem):
        x, y, c = _coords()
        mine = pltpu.make_async_copy(in_ref, out_ref.at[4 * x + 2 * y + c], local_sem)
        mine.start()
        cps = []
        for k in range(1, NDEV):
            fx, fy, fc = (k >> 2) & 1, (k >> 1) & 1, k & 1
            peer = (x ^ fx, y ^ fy, c ^ fc)
            cps.append(pltpu.make_async_remote_copy(
                src_ref=in_ref, dst_ref=out_ref.at[4 * x + 2 * y + c], send_sem=send_sems.at[k - 1],
                recv_sem=recv_sems.at[k - 1], device_id=peer, device_id_type=MESH))
        for cp in cps:
            cp.start()
        for k in range(1, NDEV):
            fx, fy, fc = (k >> 2) & 1, (k >> 1) & 1, k & 1
            px, py, pc = x ^ fx, y ^ fy, c ^ fc
            pltpu.make_async_remote_copy(
                src_ref=in_ref, dst_ref=out_ref.at[4 * px + 2 * py + pc], send_sem=send_sems.at[k - 1],
                recv_sem=recv_sems.at[k - 1], device_id=(px, py, pc), device_id_type=MESH).wait_recv()
        for cp in cps:
            cp.wait_send()
        mine.wait()

    return pl.pallas_call(
        body, name="gather_small", in_specs=[ANY], out_specs=ANY,
        out_shape=jax.ShapeDtypeStruct((NDEV,) + buf.shape, buf.dtype),
        scratch_shapes=[pltpu.SemaphoreType.DMA((NDEV - 1,)), pltpu.SemaphoreType.DMA((NDEV - 1,)),
                        pltpu.SemaphoreType.DMA],
    )(buf)


def _adam_math(wv, g, m, v):
    m_new = B1 * m + (1.0 - B1) * g
    v_new = B2 * v + (1.0 - B2) * (g * g)
    m_hat = m_new / (1.0 - B1 ** STEP)
    v_hat = v_new / (1.0 - B2 ** STEP)
    return -LR * (m_hat / (jnp.sqrt(v_hat) + ADAM_EPS) + WD * wv), m_new, v_new


def _adam(name, parts, wv, m, v, after=None):
    npart, r, c = parts.shape
    tr, tc = _ew_tile(r, c)

    def body(p_ref, w_ref, m_ref, v_ref, *rest):
        g_ref, d_ref, mo_ref, vo_ref = rest[-4:]
        g = p_ref[0].astype(F32)
        for j in range(1, npart):
            g = g + p_ref[j].astype(F32)
        g_ref[...] = g
        d_ref[...], mo_ref[...], vo_ref[...] = _adam_math(w_ref[...], g, m_ref[...], v_ref[...])

    blk = pl.BlockSpec((tr, tc), lambda i, k: (i, k))
    extra = [] if after is None else [after]
    return pl.pallas_call(
        body, name=name, grid=(r // tr, c // tc),
        in_specs=[pl.BlockSpec((npart, tr, tc), lambda i, k: (0, i, k)), blk, blk, blk] + [ANY] * len(extra),
        out_specs=[blk] * 4, out_shape=[jax.ShapeDtypeStruct((r, c), F32)] * 4, compiler_params=_params(2),
    )(parts, wv, m, v, *extra)


def _adam_exchanged(name, sums, land, wv, m, v, my_chip):
    _, r, c = sums.shape
    tr, tc = _ew_tile(r, c)

    def body(chip_ref, s_ref, l_ref, w_ref, m_ref, v_ref, g_ref, d_ref, mo_ref, vo_ref):
        g = s_ref[...].astype(F32)
        for j in range(3):
            g = g + l_ref[j].astype(F32)
        g_ref[...] = g
        d_ref[...], mo_ref[...], vo_ref[...] = _adam_math(w_ref[...], g, m_ref[...], v_ref[...])

    blk = pl.BlockSpec((tr, tc), lambda i, k, chip_ref: (i, k))
    return pl.pallas_call(
        body, name=name,
        grid_spec=pltpu.PrefetchScalarGridSpec(
            num_scalar_prefetch=1, grid=(r // tr, c // tc),
            in_specs=[pl.BlockSpec((None, tr, tc), lambda i, k, chip_ref: (chip_ref[0], i, k)),
                      pl.BlockSpec((3, tr, tc), lambda i, k, chip_ref: (0, i, k)), blk, blk, blk],
            out_specs=[blk] * 4),
        out_shape=[jax.ShapeDtypeStruct((r, c), F32)] * 4, compiler_params=_params(2),
    )(my_chip, sums, land, wv, m, v)


SHARDED = ("ffn1_w_gate", "ffn1_w_up", "ffn1_w_down", "w_in", "w_uq", "w_ukv", "w_branch_a", "w_branch_b", "w_out",
           "ffn2_w_gate", "ffn2_w_up", "ffn2_w_down", "w_pl", "w_pl_gate")
TRANSPOSED = ("ffn1_w_gate", "ffn1_w_up", "ffn2_w_gate", "ffn2_w_up", "w_in", "w_uq")
REPLICATED = ("ffn1_norm", "mix_norm", "q_a_norm", "kv_a_norm", "na_rpb", "ffn2_norm", "pl_norm", "final_norm")
WEIGHTS = ("ffn1_norm", "ffn1_w_gate", "ffn1_w_up", "ffn1_w_down", "mix_norm", "w_in", "q_a_norm", "w_uq",
           "kv_a_norm", "w_ukv", "na_rpb", "w_branch_a", "w_branch_b", "w_out", "ffn2_norm", "ffn2_w_gate",
           "ffn2_w_up", "ffn2_w_down", "pl_norm", "w_pl", "w_pl_gate", "final_norm")
SMALL_W = 2048


def _pack_small(vals):
    rows = []
    for name in REPLICATED:
        flat = vals[name].reshape(-1).astype(F32)
        n = -(-flat.shape[0] // SMALL_W) * SMALL_W
        rows.append(jnp.pad(flat, (0, n - flat.shape[0])).reshape(-1, SMALL_W))
    return jnp.concatenate(rows, axis=0)


def _unpack_small(buf, shapes):
    out, r = {}, 0
    for name in REPLICATED:
        size = int(np.prod(shapes[name]))
        nrow = -(-size // SMALL_W)
        out[name] = buf[r:r + nrow].reshape(-1)[:size].reshape(shapes[name])
        r += nrow
    return out


def kernel(x, p, ffn1_norm, ffn1_w_gate, ffn1_w_up, ffn1_w_down, mix_norm, w_in, q_a_norm, w_uq, kv_a_norm, w_ukv, na_rpb, w_branch_a, w_branch_b, w_out, ffn2_norm, ffn2_w_gate, ffn2_w_up, ffn2_w_down, pl_norm, w_pl, w_pl_gate, final_norm, loss_target, m_ffn1_norm, m_ffn1_w_gate, m_ffn1_w_up, m_ffn1_w_down, m_mix_norm, m_w_in, m_q_a_norm, m_w_uq, m_kv_a_norm, m_w_ukv, m_na_rpb, m_w_branch_a, m_w_branch_b, m_w_out, m_ffn2_norm, m_ffn2_w_gate, m_ffn2_w_up, m_ffn2_w_down, m_pl_norm, m_w_pl, m_w_pl_gate, m_final_norm, v_ffn1_norm, v_ffn1_w_gate, v_ffn1_w_up, v_ffn1_w_down, v_mix_norm, v_w_in, v_q_a_norm, v_w_uq, v_kv_a_norm, v_w_ukv, v_na_rpb, v_w_branch_a, v_w_branch_b, v_w_out, v_ffn2_norm, v_ffn2_w_gate, v_ffn2_w_up, v_ffn2_w_down, v_pl_norm, v_w_pl, v_w_pl_gate, v_final_norm):
    args = dict(locals())
    wts = {n: args[n] for n in WEIGHTS}
    mom = {n: args["m_" + n] for n in WEIGHTS}
    var = {n: args["v_" + n] for n in WEIGHTS}
    shapes = {n: wts[n].shape for n in WEIGHTS}
    core = lax.axis_index("c").astype(jnp.int32).reshape(1)

    local = lambda n, a: a[0].T if n in TRANSPOSED else a[0]
    own = {n: local(n, wts[n]).astype(BF) for n in SHARDED}
    sp = {n: wts[n].reshape(1, -1) for n in REPLICATED if n != "na_rpb"}
    sp["na_rpb"] = wts["na_rpb"][0]
    loss_part, grad_x, pending, dsp = _device_step(x[0], p[0, 0], loss_target[0], sp, own, core)

    out = {}
    last = grad_x
    my_chip = (2 * lax.axis_index("x") + lax.axis_index("y")).astype(jnp.int32).reshape(1)
    for tag, names, send, recv, thru, lands in pending:
        thru, lands = _chips_wait("rs_wait_" + tag, send, recv, thru, lands, last)
        for n, s4, l3 in zip(names, thru, lands):
            res4 = _adam_exchanged("adam_" + n, s4, l3, local(n, wts[n]), local(n, mom[n]), local(n, var[n]), my_chip)
            out[n] = tuple((a.T if n in TRANSPOSED else a)[None] for a in res4)
            last = res4[1]

    small = jnp.concatenate([_pack_small(dsp), jnp.pad(loss_part, ((0, 0), (0, SMALL_W - loss_part.shape[1])))], 0)
    pad_rows = -small.shape[0] % 8
    small = jnp.pad(small, ((0, pad_rows), (0, 0)))
    every = _gather_small(small)
    zeros = jnp.zeros((1 + pad_rows, SMALL_W), F32)
    pack = lambda d: jnp.concatenate([_pack_small(d), zeros], 0)
    g_s, d_s, m_s, v_s = _adam("adam_small", every, pack(wts), pack(mom), pack(var))
    n_rows = small.shape[0] - 1 - pad_rows
    loss = g_s[n_rows, 0]
    small_out = [_unpack_small(b, shapes) for b in (g_s, d_s, m_s, v_s)]
    for n in REPLICATED:
        out[n] = tuple(b[n] for b in small_out)

    res = [loss, grad_x[None]]
    for k in range(4):
        res += [out[n][k] for n in WEIGHTS]
    return tuple(res)
```

```python
import functools

import numpy as np
import jax
import jax.numpy as jnp
from jax import lax
from jax.experimental import pallas as pl
from jax.experimental.pallas import tpu as pltpu

F32 = jnp.float32
BF = jnp.bfloat16
MESH = pl.DeviceIdType.MESH

NDEV = 8
NCHIP = 4
VMEM_LIMIT = 56 * 1024 * 1024
EPS = 1e-6
NEG = -1e30
GRID_W = 64
NA_HEADS, NA_DIM = 8, 128
NA_ROWS_WIN, NA_COLS_WIN = 8, 16
NA_HG = 4
NA_QROWS = 4
ML_HEADS, ML_NOPE, ML_ROPE, ML_V = 8, 128, 64, 128
ML_QK = ML_NOPE + ML_ROPE
ML_RANK = 512
ROPE_THETA = 10000.0
LR, B1, B2, ADAM_EPS, WD, STEP = 0.001, 0.9, 0.999, 1e-08, 0.01, 10
HI = lax.Precision.HIGHEST

_DN = {"nn": (((1,), (0,)), ((), ())), "nt": (((1,), (1,)), ((), ())), "tn": (((0,), (0,)), ((), ()))}


def _params(n):
    return pltpu.CompilerParams(dimension_semantics=("arbitrary",) * n, vmem_limit_bytes=VMEM_LIMIT)


def _sig(v):
    return jax.nn.sigmoid(v)


ANY = pl.BlockSpec(memory_space=pl.ANY)


def _coords():
    return lax.axis_index("x"), lax.axis_index("y"), lax.axis_index("c")


class _Part:
    inputs, out_shapes, sem_shapes, results = (), (), (), None

    def mid(self, ins, outs, sems):
        pass

    def late(self, ins, outs, sems):
        pass


class _After(_Part):
    def __init__(self, token):
        self.inputs = [token]

    def start(self, ins, outs, sems):
        pass

    finish = start


class _GatherPart(_Part):
    def __init__(self, names, shards):
        n = len(shards)
        self.names, self.inputs = list(names), list(shards)
        self.out_shapes = [jax.ShapeDtypeStruct((NDEV,) + a.shape, a.dtype) for a in shards]
        self.sem_shapes = [pltpu.SemaphoreType.DMA((n, 7)), pltpu.SemaphoreType.DMA((n, 7)),
                           pltpu.SemaphoreType.DMA((n,))]

    def _plan(self, ins, outs, sems):
        send_sems, recv_sems, local_sems = sems
        x, y, c = _coords()
        me, sib, diag = (x, y, c), (x, y, 1 - c), (1 - x, 1 - y, c)
        n1, n2 = (x ^ (1 - c), y ^ c, c), (x ^ c, y ^ (1 - c), c)

        def copy(i, k, block, to, src=None):
            px, py, pc = block
            dst = outs[i].at[4 * px + 2 * py + pc]
            return pltpu.make_async_remote_copy(
                src_ref=dst if src is None else src, dst_ref=dst, send_sem=send_sems.at[i, k],
                recv_sem=recv_sems.at[i, k], device_id=to, device_id_type=MESH)

        mine = [pltpu.make_async_copy(ins[i], outs[i].at[4 * x + 2 * y + c], local_sems.at[i])
                for i in range(len(ins))]
        return copy, mine, me, sib, n1, n2, diag

    def _own_sends(self, ins, copy, me, sib, n1, n2):
        return [copy(i, k, me, to, src=ins[i]) for i in range(len(ins)) for k, to in enumerate((sib, n1, n2))]

    def start(self, ins, outs, sems):
        copy, mine, me, sib, n1, n2, _ = self._plan(ins, outs, sems)
        for cp in mine + self._own_sends(ins, copy, me, sib, n1, n2):
            cp.start()

    def mid(self, ins, outs, sems):
        copy, _, me, sib, n1, n2, _ = self._plan(ins, outs, sems)
        for i in range(len(ins)):
            copy(i, 1, n1, me).wait_recv()
            copy(i, 3, n1, n2).start()
            copy(i, 4, n1, sib).start()

    def late(self, ins, outs, sems):
        copy, _, me, sib, _, n2, diag = self._plan(ins, outs, sems)
        for i in range(len(ins)):
            copy(i, 2, n2, me).wait_recv()
            copy(i, 5, n2, sib).start()
        for i in range(len(ins)):
            copy(i, 3, diag, me).wait_recv()
            copy(i, 6, diag, sib).start()

    def finish(self, ins, outs, sems):
        copy, mine, me, sib, n1, n2, diag = self._plan(ins, outs, sems)
        other = lambda dev: (dev[0], dev[1], sib[2])
        n = len(ins)
        for i in range(n):
            copy(i, 0, sib, me).wait_recv()
            for k, block in ((4, other(n2)), (5, other(n1)), (6, other(diag))):
                copy(i, k, block, me).wait_recv()
        for cp in self._own_sends(ins, copy, me, sib, n1, n2):
            cp.wait_send()
        for i in range(n):
            for k, block in ((3, n1), (4, n1), (5, n2), (6, diag)):
                copy(i, k, block, sib).wait_send()
        for cp in mine:
            cp.wait()


class _SiblingPart(_Part):
    def __init__(self, names, parts):
        n = len(parts)
        self.names, self.inputs = list(names), list(parts)
        self.out_shapes = [jax.ShapeDtypeStruct((NCHIP,) + a.shape[2:], a.dtype) for a in parts]
        self.sem_shapes = [pltpu.SemaphoreType.DMA((n,)), pltpu.SemaphoreType.DMA((n,))]

    def _copies(self, ins, outs, sems):
        x, y, c = _coords()
        return [pltpu.make_async_remote_copy(
            src_ref=ins[i].at[:, 1 - c], dst_ref=outs[i], send_sem=sems[0].at[i], recv_sem=sems[1].at[i],
            device_id=(x, y, 1 - c), device_id_type=MESH) for i in range(len(ins))]

    def start(self, ins, outs, sems):
        for cp in self._copies(ins, outs, sems):
            cp.start()

    def finish(self, ins, outs, sems):
        cps = self._copies(ins, outs, sems)
        for cp in cps:
            cp.wait_recv()
        for cp in cps:
            cp.wait_send()


HBM = pl.BlockSpec(memory_space=pltpu.HBM)
SEM = pl.BlockSpec(memory_space=pltpu.SEMAPHORE)


def _chip_peers():
    x, y, c = _coords()
    return [(1 - x, y, c), (x, 1 - y, c), (1 - x, 1 - y, c)]


def _chips_start(name, sums):
    n = len(sums)

    def body(*refs):
        ins, lands, send_sems, recv_sems = refs[:n], refs[n:2 * n], refs[2 * n], refs[2 * n + 1]
        for i in range(n):
            for k, (px, py, pc) in enumerate(_chip_peers()):
                pltpu.make_async_remote_copy(
                    src_ref=ins[i].at[2 * px + py], dst_ref=lands[i].at[k], send_sem=send_sems.at[3 * i + k],
                    recv_sem=recv_sems.at[3 * i + k], device_id=(px, py, pc), device_id_type=MESH).start()
        refs[-1][...] = jnp.zeros_like(refs[-1])

    lands = [lax.empty((3,) + a.shape[1:], a.dtype) for a in sums]
    bufs = list(sums) + lands
    res = pl.pallas_call(
        body, name=name, in_specs=[HBM] * (2 * n),
        out_specs=(SEM, SEM, *[HBM] * (2 * n), pl.BlockSpec(memory_space=pltpu.VMEM)),
        out_shape=(pltpu.SemaphoreType.DMA((3 * n,)), pltpu.SemaphoreType.DMA((3 * n,)),
                   *[pltpu.HBM(a.shape, a.dtype) for a in bufs], jax.ShapeDtypeStruct((8, 128), F32)),
        input_output_aliases={i: 2 + i for i in range(2 * n)},
        compiler_params=pltpu.CompilerParams(has_side_effects=pltpu.SideEffectType.DATAFLOW_SIDE_EFFECTING),
    )(*[pltpu.with_memory_space_constraint(a, pltpu.HBM) for a in bufs])
    return res[0], res[1], list(res[2:2 + n]), list(res[2 + n:2 + 2 * n]), res[-1]


def _chips_wait(name, send_sems, recv_sems, sums, lands, after):
    n = len(sums)

    def body(*refs):
        ins, zones, send, recv = refs[:n], refs[n:2 * n], refs[2 * n], refs[2 * n + 1]
        for i in range(n):
            for k, peer in enumerate(_chip_peers()):
                cp = pltpu.make_async_remote_copy(
                    src_ref=ins[i].at[0], dst_ref=zones[i].at[k], send_sem=send.at[3 * i + k],
                    recv_sem=recv.at[3 * i + k],
                    device_id=peer, device_id_type=MESH)
                cp.wait_send()
                cp.wait_recv()

    bufs = list(sums) + list(lands)
    res = pl.pallas_call(
        body, name=name, in_specs=[HBM] * (2 * n) + [SEM, SEM, ANY], out_specs=[HBM] * (2 * n),
        out_shape=[pltpu.HBM(a.shape, a.dtype) for a in bufs], input_output_aliases={i: i for i in range(2 * n)},
        compiler_params=pltpu.CompilerParams(has_side_effects=pltpu.SideEffectType.DATAFLOW_SIDE_EFFECTING),
    )(*bufs, send_sems, recv_sems, after)
    return list(res[:n]), list(res[n:])


def _call(name, body, grid, in_specs, out_specs, out_shape, args, comm=(), scratch=()):
    comm = [p for p in comm if p is not None]
    single = not isinstance(out_shape, (list, tuple))
    o_specs = [out_specs] if single else list(out_specs)
    o_shape = [out_shape] if single else list(out_shape)
    n_in, n_out = len(in_specs), len(o_specs)
    c_in = [a for p in comm for a in p.inputs]
    c_out = [s for p in comm for s in p.out_shapes]
    c_sem = [s for p in comm for s in p.sem_shapes]

    def wrapped(*refs):
        ins, outs = refs[:n_in], refs[n_in + len(c_in):n_in + len(c_in) + n_out]
        pos = [n_in, n_in + len(c_in) + n_out, n_in + len(c_in) + n_out + len(c_out)]
        own = refs[pos[2]:pos[2] + len(scratch)]
        pos[2] += len(scratch)
        split = []
        for p in comm:
            sizes = [len(p.inputs), len(p.out_shapes), len(p.sem_shapes)]
            split.append([refs[o:o + n] for o, n in zip(pos, sizes)])
            pos = [o + n for o, n in zip(pos, sizes)]
        step, steps = 0, 1
        for a, g in enumerate(grid):
            step, steps = step * g + pl.program_id(a), steps * g

        def run(which, at):
            def go():
                for p, cut in zip(comm, split):
                    getattr(p, which)(*cut)
            if not comm:
                return
            if grid:
                pl.when(step == at)(go)
            else:
                go()

        run("start", 0)
        body(*ins, *outs, *own)
        run("mid", steps // 2)
        run("late", max(steps // 2, steps - 1 - max(1, steps // 8)))
        run("finish", steps - 1)

    res = pl.pallas_call(
        wrapped, name=name, grid=grid, in_specs=list(in_specs) + [ANY] * len(c_in),
        out_specs=o_specs + [ANY] * len(c_out), out_shape=o_shape + c_out, scratch_shapes=list(scratch) + c_sem,
        compiler_params=_params(len(grid)),
    )(*args, *c_in)
    pos = n_out
    for p in comm:
        p.results = list(res[pos:pos + len(p.out_shapes)])
        pos += len(p.out_shapes)
    return res[0] if single else list(res[:n_out])


def _comm_only(name, comm):
    def body(o_ref):
        o_ref[...] = jnp.zeros_like(o_ref)

    _call(name, body, (), [], pl.BlockSpec(memory_space=pltpu.VMEM), jax.ShapeDtypeStruct((8, 128), F32), [], comm)


def _mm(name, grid, prods, extras, outs, epi, nacc=1, comm=()):
    n_p, n_e = len(prods), len(extras)

    def body(*refs):
        ab, ex, out = refs[:2 * n_p], refs[2 * n_p:2 * n_p + n_e], refs[2 * n_p + n_e:]
        accs = [None] * nacc
        for i, prod in enumerate(prods):
            dn, acc, loop = prod[6], prod[7], prod[8]
            a_ref, b_ref = ab[2 * i], ab[2 * i + 1]
            if loop:
                for g in range(loop):
                    t = lax.dot_general(a_ref[g], b_ref[g], _DN[dn], preferred_element_type=F32)
                    accs[acc] = t if accs[acc] is None else accs[acc] + t
            else:
                t = lax.dot_general(a_ref[...], b_ref[...], _DN[dn], preferred_element_type=F32)
                accs[acc] = t if accs[acc] is None else accs[acc] + t
        epi(accs, ex, out)

    in_specs, args = [], []
    for prod in prods:
        in_specs += [pl.BlockSpec(prod[1], prod[2]), pl.BlockSpec(prod[4], prod[5])]
        args += [prod[0], prod[3]]
    for e, e_blk, e_map in extras:
        in_specs.append(pl.BlockSpec(e_blk, e_map))
        args.append(e)
    return _call(name, body, grid, in_specs, [pl.BlockSpec(blk, mp) for _, _, blk, mp in outs],
                 [jax.ShapeDtypeStruct(s, d) for s, d, _, _ in outs], args, comm)


def _store(accs, ex, out):
    out[0][...] = accs[0].astype(out[0].dtype)


def _ew_tile(r, c, budget=3 << 19):
    for t in range(r - r % 16, 0, -16):
        if r % t == 0 and t * c * 4 <= budget:
            return t, c
    for t in range(c - c % 128, 0, -128):
        if c % t == 0 and r * t * 4 <= budget:
            return r, t
    return r, c


def _tile(n, want):
    t = min(n, want)
    assert n % t == 0, (n, want)
    return t


def _mm_nn(name, a, b, out_dtype, tm=512, tn=512, comm=()):
    m, k = a.shape
    n = b.shape[1]
    tm, tn = _tile(m, tm), (tn if n % tn == 0 else n)
    return _mm(name, (n // tn, m // tm),
               [(a, (tm, k), lambda j, i: (i, 0), b, (k, tn), lambda j, i: (0, j), "nn", 0, 0)], [],
               [((m, n), out_dtype, (tm, tn), lambda j, i: (i, j))], _store, comm=comm)[0]


def _mm_nt(name, a, bt, out_dtype, tm=512, tn=512, comm=()):
    m, k = a.shape
    n = bt.shape[0]
    tm, tn = _tile(m, tm), (tn if n % tn == 0 else n)
    return _mm(name, (n // tn, m // tm),
               [(a, (tm, k), lambda j, i: (i, 0), bt, (tn, k), lambda j, i: (j, 0), "nt", 0, 0)], [],
               [((m, n), out_dtype, (tm, tn), lambda j, i: (i, j))], _store, comm=comm)[0]


def _mm_tn_into(name, a, b, buf, row0, ta=512, tb=512):
    t, ka = a.shape
    nb = b.shape[1]
    ta, tb = (ta if ka % ta == 0 else ka), (tb if nb % tb == 0 else nb)

    def body(a_ref, b_ref, buf_in, buf_out, tile, sem):
        i, j = pl.program_id(0), pl.program_id(1)
        tile[...] = lax.dot_general(a_ref[...], b_ref[...], _DN["tn"], preferred_element_type=F32).astype(tile.dtype)
        rows = pl.ds(pl.multiple_of(row0 + i * ta, 16), ta)
        cp = pltpu.make_async_copy(tile, buf_out.at[rows, pl.ds(pl.multiple_of(j * tb, 128), tb)], sem)
        cp.start()
        cp.wait()

    return pl.pallas_call(
        body, name=name, grid=(ka // ta, nb // tb),
        in_specs=[pl.BlockSpec((t, ta), lambda i, j: (0, i)), pl.BlockSpec((t, tb), lambda i, j: (0, j)), ANY],
        out_specs=ANY, out_shape=jax.ShapeDtypeStruct(buf.shape, buf.dtype), input_output_aliases={2: 0},
        scratch_shapes=[pltpu.VMEM((ta, tb), buf.dtype), pltpu.SemaphoreType.DMA],
        compiler_params=_params(2))(a, b, buf)


def _mm_tn(name, a, b, out_dtype, ta=512, tb=512, scale=None):
    t, ka = a.shape
    nb = b.shape[1]
    ta, tb = (ta if ka % ta == 0 else ka), (tb if nb % tb == 0 else nb)

    def epi(accs, ex, out):
        v = accs[0] if scale is None else accs[0] * scale
        out[0][...] = v.astype(out[0].dtype)

    return _mm(name, (ka // ta, nb // tb),
               [(a, (t, ta), lambda i, j: (0, i), b, (t, tb), lambda i, j: (0, j), "tn", 0, 0)], [],
               [((ka, nb), out_dtype, (ta, tb), lambda i, j: (i, j))], epi)[0]


def _rms_fwd(name, x, g, tm=256):
    s, d = x.shape
    tm = _tile(s, tm)

    def body(x_ref, g_ref, o_ref):
        v = x_ref[...]
        o_ref[...] = (v * lax.rsqrt(jnp.mean(v * v, axis=-1, keepdims=True) + EPS) * g_ref[...]).astype(o_ref.dtype)

    return pl.pallas_call(
        body, name=name, grid=(s // tm,),
        in_specs=[pl.BlockSpec((tm, d), lambda i: (i, 0)), pl.BlockSpec((1, d), lambda i: (0, 0))],
        out_specs=pl.BlockSpec((tm, d), lambda i: (i, 0)), out_shape=jax.ShapeDtypeStruct((s, d), BF),
        compiler_params=_params(1))(x, g)


def _acc_rows(ref, part, i):
    @pl.when(i == 0)
    def _():
        ref[...] = part

    @pl.when(i > 0)
    def _():
        ref[...] += part


def _rms_bwd_math(dn, v, g):
    rstd = lax.rsqrt(jnp.mean(v * v, axis=-1, keepdims=True) + EPS)
    xh = v * rstd
    dxh = dn * g
    dx = rstd * (dxh - xh * jnp.mean(dxh * xh, axis=-1, keepdims=True))
    return dx, jnp.sum(dn * xh, axis=0, keepdims=True)


def _rms_bwd(name, dn, x, g, resid, tm=256, comm=()):
    s, d = x.shape
    tm = _tile(s, tm)

    def body(dn_ref, x_ref, g_ref, r_ref, dx_ref, dg_ref):
        dx, part = _rms_bwd_math(dn_ref[...].astype(F32), x_ref[...], g_ref[...])
        dx_ref[...] = r_ref[...] + dx
        _acc_rows(dg_ref, part, pl.program_id(0))

    row = pl.BlockSpec((tm, d), lambda i: (i, 0))
    one = pl.BlockSpec((1, d), lambda i: (0, 0))
    return _call(name, body, (s // tm,), [row, row, one, row], [row, one],
                 [jax.ShapeDtypeStruct((s, d), F32), jax.ShapeDtypeStruct((1, d), F32)], [dn, x, g, resid], comm)


def _loss_head(h, target, g, tm=256):
    s, d = h.shape
    tm = _tile(s, tm)

    def body(h_ref, t_ref, g_ref, dh_ref, dg_ref, loss_ref):
        v, gv = h_ref[...], g_ref[...]
        rstd = lax.rsqrt(jnp.mean(v * v, axis=-1, keepdims=True) + EPS)
        xh = v * rstd
        err = xh * gv - t_ref[...]
        part_loss = 0.5 * jnp.sum(jnp.mean(err * err, axis=-1, keepdims=True), axis=0, keepdims=True)
        dy = err * (1.0 / d)
        dxh = dy * gv
        dh_ref[...] = rstd * (dxh - xh * jnp.mean(dxh * xh, axis=-1, keepdims=True))
        i = pl.program_id(0)
        _acc_rows(dg_ref, jnp.sum(dy * xh, axis=0, keepdims=True), i)
        _acc_rows(loss_ref, jnp.broadcast_to(part_loss, loss_ref.shape), i)

    row = pl.BlockSpec((tm, d), lambda i: (i, 0))
    one = pl.BlockSpec((1, d), lambda i: (0, 0))
    return pl.pallas_call(
        body, name="loss_head", grid=(s // tm,), in_specs=[row, row, one],
        out_specs=[row, one, pl.BlockSpec((1, 128), lambda i: (0, 0))],
        out_shape=[jax.ShapeDtypeStruct((s, d), F32), jax.ShapeDtypeStruct((1, d), F32),
                   jax.ShapeDtypeStruct((1, 128), F32)],
        compiler_params=_params(1))(h, target, g)


def _pl_bwd_elem(dh, pe, t, tm=256):
    s, d = dh.shape
    tm = _tile(s, tm)

    def body(dh_ref, pe_ref, t_ref, dt_ref, dpe_ref):
        dh_v, sg = dh_ref[...], _sig(t_ref[...])
        dt_ref[...] = (dh_v * pe_ref[...].astype(F32) * sg * (1.0 - sg)).astype(BF)
        dpe_ref[...] = (dh_v * sg).astype(BF)

    row = pl.BlockSpec((tm, d), lambda i: (i, 0))
    return pl.pallas_call(
        body, name="pl_bwd_elem", grid=(s // tm,), in_specs=[row, row, row], out_specs=[row, row],
        out_shape=[jax.ShapeDtypeStruct((s, d), BF)] * 2, compiler_params=_params(1))(dh, pe, t)


def _ffn_up(name, xn, wg, wu, tm=512, comm=()):
    s, d = xn.shape
    g, fb, _ = wg.shape
    tm = _tile(s, tm)

    def epi(accs, ex, out):
        hg, hu = accs
        out[0][...] = hg.astype(BF)
        out[1][...] = hu.astype(BF)
        out[2][...] = (hg * _sig(hg) * hu).astype(BF)

    a_map = lambda j, i: (i, 0)
    w_map = lambda j, i: (j, 0, 0)
    o = ((g, s, fb), BF, (None, tm, fb), lambda j, i: (j, i, 0))
    return _mm(name, (g, s // tm),
               [(xn, (tm, d), a_map, wg, (None, fb, d), w_map, "nt", 0, 0),
                (xn, (tm, d), a_map, wu, (None, fb, d), w_map, "nt", 1, 0)], [], [o, o, o], epi, nacc=2, comm=comm)


def _ffn_down(name, a, wd, resid, tm=512, tn=512, comm=()):
    g, s, fb = a.shape
    d = wd.shape[2]
    tm, tn = _tile(s, tm), _tile(d, tn)

    def epi(accs, ex, out):
        out[0][...] = ex[0][...] + 0.5 * accs[0]

    return _mm(name, (d // tn, s // tm),
               [(a, (g, tm, fb), lambda j, i: (0, i, 0), wd, (g, fb, tn), lambda j, i: (0, 0, j), "nn", 0, g)],
               [(resid, (tm, tn), lambda j, i: (i, j))],
               [((s, d), F32, (tm, tn), lambda j, i: (i, j))], epi, comm=comm)[0]


def _ffn_bwd_act(name, dh, wd, hg, hu, tm=512, comm=()):
    s, d = dh.shape
    g, fb, _ = wd.shape
    tm = _tile(s, tm)

    def epi(accs, ex, out):
        da = 0.5 * accs[0]
        hg_v, hu_v = ex[0][...].astype(F32), ex[1][...].astype(F32)
        sg = _sig(hg_v)
        out[0][...] = (da * hu_v * (sg * (1.0 + hg_v * (1.0 - sg)))).astype(BF)
        out[1][...] = (da * (hg_v * sg)).astype(BF)

    blk = (None, tm, fb)
    gmap = lambda j, i: (j, i, 0)
    return _mm(name, (g, s // tm),
               [(dh, (tm, d), lambda j, i: (i, 0), wd, (None, fb, d), lambda j, i: (j, 0, 0), "nt", 0, 0)],
               [(hg, blk, gmap), (hu, blk, gmap)],
               [((g, s, fb), BF, blk, gmap), ((g, s, fb), BF, blk, gmap)], epi, comm=comm)


def _ffn_bwd_wd(name, a, dh, tn=512, comm=()):
    g, s, fb = a.shape
    d = dh.shape[1]
    tn = _tile(d, tn)

    def epi(accs, ex, out):
        out[0][...] = (0.5 * accs[0]).astype(BF)

    return _mm(name, (g, d // tn),
               [(a, (None, s, fb), lambda j, i: (j, 0, 0), dh, (s, tn), lambda j, i: (0, i), "tn", 0, 0)], [],
               [((g, fb, d), BF, (None, fb, tn), lambda j, i: (j, 0, i))], epi, comm=comm)[0]


def _ffn_bwd_wup(name, xn, dhg, dhu, tk=512, comm=()):
    s, d = xn.shape
    g, _, fb = dhg.shape
    tk = _tile(d, tk)

    def epi(accs, ex, out):
        out[0][...] = accs[0].astype(BF)
        out[1][...] = accs[1].astype(BF)

    a_map = lambda j, i: (j, 0, 0)
    b_map = lambda j, i: (0, i)
    o = ((g, fb, d), BF, (None, fb, tk), lambda j, i: (j, 0, i))
    return _mm(name, (g, d // tk),
               [(dhg, (None, s, fb), a_map, xn, (s, tk), b_map, "tn", 0, 0),
                (dhu, (None, s, fb), a_map, xn, (s, tk), b_map, "tn", 1, 0)], [], [o, o], epi, nacc=2, comm=comm)


def _ffn_bwd_x(name, dhg, dhu, wg, wu, tm=512, tn=512, comm=()):
    g, s, fb = dhg.shape
    d = wg.shape[2]
    tm, tn = _tile(s, tm), _tile(d, tn)
    a_blk, a_map = (g, tm, fb), lambda j, i: (0, i, 0)
    b_blk, b_map = (g, fb, tn), lambda j, i: (0, 0, j)
    return _mm(name, (d // tn, s // tm),
               [(dhg, a_blk, a_map, wg, b_blk, b_map, "nn", 0, g), (dhu, a_blk, a_map, wu, b_blk, b_map, "nn", 0, g)],
               [], [((s, d), F32, (tm, tn), lambda j, i: (i, j))], _store, comm=comm)[0]


def _ffn_forward(tag, h, gain, wg, wu, get_wd, up_comm=(), down_comm=()):
    xn = _rms_fwd(tag + "_norm", h, gain)
    hg, hu, a = _ffn_up(tag + "_up", xn, wg, wu, comm=up_comm)
    return _ffn_down(tag + "_down", a, get_wd(), h, comm=down_comm), (xn, hg, hu, a)


def _na_geometry(rows):
    kh = min(NA_ROWS_WIN, rows)
    cols = np.arange(GRID_W)
    col_start = np.clip(cols - NA_COLS_WIN // 2, 0, GRID_W - NA_COLS_WIN)
    mask = (cols[None, :] >= col_start[:, None]) & (cols[None, :] < col_start[:, None] + NA_COLS_WIN)
    dc = np.clip(cols[None, :] - cols[:, None], -(NA_COLS_WIN - 1), NA_COLS_WIN - 1) + (NA_COLS_WIN - 1)
    return kh, mask, dc


def _na_table(rpb, rows):
    _, mask, dc = _na_geometry(rows)
    return jnp.where(jnp.asarray(mask)[None, None], rpb[:, :, dc], NEG)


class _NaPlan:
    def __init__(self, s):
        self.s, self.rows = s, s // GRID_W
        self.kh = min(NA_ROWS_WIN, self.rows)
        self.qr = min(NA_QROWS, self.rows)
        self.kr = min(self.rows, self.kh + self.qr - 1)
        self.groups = self.rows // self.qr
        self.nd = 2 * NA_ROWS_WIN - 1
        self.hw, self.nq = NA_HG * NA_DIM, NA_HEADS // NA_HG
        clip = lambda v, hi: min(max(v, 0), hi)
        pats = [(clip(g * self.qr - self.kh // 2, self.rows - self.kr) - g * self.qr,)
                + tuple(clip(g * self.qr + a - self.kh // 2, self.rows - self.kh) - g * self.qr for a in range(self.qr))
                for g in range(self.groups)]
        self.rebuild = [g for g in range(self.groups) if g == 0 or pats[g] != pats[g - 1]]

    def first_key_row(self, g):
        return jnp.clip(g * self.qr - self.kh // 2, 0, self.rows - self.kr)

    def specs(self):
        blk = pl.BlockSpec((self.qr * GRID_W, self.hw), lambda j, g: (g, j))
        k_spec = pl.BlockSpec((self.s, self.hw), lambda j, g: (0, self.nq + j))
        v_spec = pl.BlockSpec((self.s, self.hw), lambda j, g: (0, 2 * self.nq + j))
        t_spec = pl.BlockSpec((NA_HG, self.nd, GRID_W, GRID_W), lambda j, g: (j, 0, 0, 0))
        return blk, k_spec, v_spec, t_spec

    def bias_scratch(self):
        return pltpu.VMEM((NA_HG, self.qr * GRID_W, self.kr * GRID_W), F32)

    def fill_bias(self, t_ref, bias_ref, g):
        def build():
            r0, ks = g * self.qr, self.first_key_row(g)
            for a in range(self.qr):
                rs = jnp.clip(r0 + a - self.kh // 2, 0, self.rows - self.kh)
                for i in range(self.kr):
                    valid = jnp.logical_and(ks + i >= rs, ks + i < rs + self.kh)
                    idx = jnp.clip(ks + i - r0 - a + NA_ROWS_WIN - 1, 0, self.nd - 1)
                    for h in range(NA_HG):
                        bias_ref[h, a * GRID_W:(a + 1) * GRID_W, i * GRID_W:(i + 1) * GRID_W] = jnp.where(
                            valid, t_ref[h, idx], NEG)

        pl.when(functools.reduce(jnp.logical_or, [g == r for r in self.rebuild]))(build)

    def window(self, g):
        return pl.ds(pl.multiple_of(self.first_key_row(g) * GRID_W, GRID_W), self.kr * GRID_W)


def _na_probs(q, k, bias):
    sc = lax.dot_general(q, k, _DN["nt"], preferred_element_type=F32) * (NA_DIM ** -0.5) + bias
    e = jnp.exp(sc - jnp.max(sc, axis=-1, keepdims=True))
    return e / jnp.sum(e, axis=-1, keepdims=True)


def _na_fwd(qkv, table, comm=()):
    plan = _NaPlan(qkv.shape[0])
    blk, k_spec, v_spec, t_spec = plan.specs()

    def body(q_ref, k_ref, v_ref, t_ref, o_ref, bias_ref):
        g = pl.program_id(1)
        plan.fill_bias(t_ref, bias_ref, g)
        win = plan.window(g)
        for h in range(NA_HG):
            cs = slice(h * NA_DIM, (h + 1) * NA_DIM)
            p = _na_probs(q_ref[:, cs], k_ref[win, cs], bias_ref[h])
            o_ref[:, cs] = jnp.dot(p.astype(BF), v_ref[win, cs], preferred_element_type=F32).astype(BF)

    return _call("na_fwd", body, (plan.nq, plan.groups), [blk, k_spec, v_spec, t_spec], blk,
                 jax.ShapeDtypeStruct((plan.s, NA_HEADS * NA_DIM), BF), [qkv, qkv, qkv, table], comm,
                 scratch=[plan.bias_scratch()])


def _na_bwd(qkv, table, do, comm=()):
    plan = _NaPlan(qkv.shape[0])
    blk, k_spec, v_spec, t_spec = plan.specs()
    qr, kr = plan.qr, plan.kr

    def body(q_ref, k_ref, v_ref, t_ref, do_ref, dq_ref, dk_ref, dv_ref, dt_ref, bias_ref):
        g = pl.program_id(1)

        @pl.when(g == 0)
        def _():
            dk_ref[...] = jnp.zeros_like(dk_ref)
            dv_ref[...] = jnp.zeros_like(dv_ref)
            dt_ref[...] = jnp.zeros_like(dt_ref)

        plan.fill_bias(t_ref, bias_ref, g)
        win = plan.window(g)
        base = plan.first_key_row(g) - g * qr + NA_ROWS_WIN - 1
        for h in range(NA_HG):
            cs = slice(h * NA_DIM, (h + 1) * NA_DIM)
            q, k, v, do_h = q_ref[:, cs], k_ref[win, cs], v_ref[win, cs], do_ref[:, cs]
            p = _na_probs(q, k, bias_ref[h])
            dp = lax.dot_general(do_h, v, _DN["nt"], preferred_element_type=F32)
            ds = p * (dp - jnp.sum(p * dp, axis=-1, keepdims=True))
            for dlt in range(1 - qr, kr):
                tiles = [ds[a * GRID_W:(a + 1) * GRID_W, (a + dlt) * GRID_W:(a + dlt + 1) * GRID_W]
                         for a in range(qr) if 0 <= a + dlt < kr]
                dt_ref[h, jnp.clip(base + dlt, 0, plan.nd - 1)] += functools.reduce(jnp.add, tiles)
            dsb = (ds * (NA_DIM ** -0.5)).astype(BF)
            dq_ref[:, cs] = jnp.dot(dsb, k, preferred_element_type=F32).astype(BF)
            dk_ref[win, cs] += lax.dot_general(dsb, q, _DN["tn"], preferred_element_type=F32)
            dv_ref[win, cs] += lax.dot_general(p.astype(BF), do_h, _DN["tn"], preferred_element_type=F32)

    width = NA_HEADS * NA_DIM
    whole = pl.BlockSpec((plan.s, plan.hw), lambda j, g: (0, j))
    return _call(
        "na_bwd", body, (plan.nq, plan.groups), [blk, k_spec, v_spec, t_spec, blk], [blk, whole, whole, t_spec],
        [jax.ShapeDtypeStruct((plan.s, width), BF), jax.ShapeDtypeStruct((plan.s, width), F32),
         jax.ShapeDtypeStruct((plan.s, width), F32),
         jax.ShapeDtypeStruct((NA_HEADS, plan.nd, GRID_W, GRID_W), F32)],
        [qkv, qkv, qkv, table, do], comm, scratch=[plan.bias_scratch()])


def _na_rpb_grad(dt, rows):
    _, mask, dc = _na_geometry(rows)
    nd, nc = 2 * NA_ROWS_WIN - 1, 2 * NA_COLS_WIN - 1
    onehot = np.zeros((GRID_W * GRID_W, 128), np.float32)
    onehot[np.arange(GRID_W * GRID_W), dc.reshape(-1)] = mask.reshape(-1).astype(np.float32)
    flat = dt.reshape(NA_HEADS * nd, GRID_W * GRID_W)

    def body(a_ref, e_ref, o_ref):
        o_ref[...] = jnp.dot(a_ref[...], e_ref[...], precision=HI, preferred_element_type=F32)

    out = pl.pallas_call(body, name="na_rpb_grad", out_shape=jax.ShapeDtypeStruct((NA_HEADS * nd, 128), F32),
                         compiler_params=_params(0))(flat, jnp.asarray(onehot))
    return out[:, :nc].reshape(NA_HEADS, nd, nc)


def _rope_consts(s):
    pos = np.arange(s, dtype=np.float32)
    inv = (1.0 / (ROPE_THETA ** (np.arange(0, ML_ROPE, 2, dtype=np.float32) / ML_ROPE))).astype(np.float32)
    ang = pos[:, None] * inv[None, :]
    cos, sin = np.cos(ang).astype(np.float32), np.sin(ang).astype(np.float32)
    half = ML_ROPE // 2
    rot = np.zeros((ML_ROPE, ML_ROPE), np.float32)
    rot[np.arange(half) + half, np.arange(half)] = -1.0
    rot[np.arange(half), np.arange(half) + half] = 1.0
    return (jnp.asarray(np.concatenate([cos, cos], 1)), jnp.asarray(np.concatenate([sin, sin], 1)),
            jnp.asarray(rot), jnp.asarray(rot.T.copy()))


def _rope(v, cos, sin, rot):
    return v * cos + jnp.dot(v, rot, precision=HI, preferred_element_type=F32) * sin


def _unrope(dv, cos, sin, rot_t):
    return dv * cos + jnp.dot(dv * sin, rot_t, precision=HI, preferred_element_type=F32)


def _rms(v, g):
    return v * lax.rsqrt(jnp.mean(v * v, axis=-1, keepdims=True) + EPS) * g


def _mla_prep(lat, gq, gkv, cos, sin, rot, tm=256):
    s, w = lat.shape
    tm = _tile(s, tm)

    def body(l_ref, gq_ref, gkv_ref, c_ref, s_ref, r_ref, cq_ref, ckv_ref, kr_ref):
        cq_ref[...] = _rms(l_ref[:, :ML_RANK], gq_ref[...]).astype(BF)
        ckv_ref[...] = _rms(l_ref[:, ML_RANK:2 * ML_RANK], gkv_ref[...]).astype(BF)
        kr_ref[...] = _rope(l_ref[:, 2 * ML_RANK:], c_ref[...], s_ref[...], r_ref[...]).astype(BF)

    row = lambda c: pl.BlockSpec((tm, c), lambda i: (i, 0))
    full = lambda a: pl.BlockSpec(a.shape, lambda i: (0, 0))
    return pl.pallas_call(
        body, name="mla_prep", grid=(s // tm,),
        in_specs=[row(w), full(gq), full(gkv), row(ML_ROPE), row(ML_ROPE), full(rot)],
        out_specs=[row(ML_RANK), row(ML_RANK), row(ML_ROPE)],
        out_shape=[jax.ShapeDtypeStruct((s, ML_RANK), BF), jax.ShapeDtypeStruct((s, ML_RANK), BF),
                   jax.ShapeDtypeStruct((s, ML_ROPE), BF)],
        compiler_params=_params(1))(lat, gq, gkv, cos, sin, rot)


def _mla_q_proj(cq, wuq, cos, sin, rot, tm=512, comm=()):
    s, k = cq.shape
    tm = _tile(s, tm)

    def epi(accs, ex, out):
        acc = accs[0]
        out[0][:, :ML_NOPE] = acc[:, :ML_NOPE].astype(BF)
        out[0][:, ML_NOPE:] = _rope(acc[:, ML_NOPE:], ex[0][...], ex[1][...], ex[2][...]).astype(BF)

    rmap = lambda j, i: (i, 0)
    return _mm("mla_q_proj", (ML_HEADS, s // tm),
               [(cq, (tm, k), rmap, wuq, (None, ML_QK, k), lambda j, i: (j, 0, 0), "nt", 0, 0)],
               [(cos, (tm, ML_ROPE), rmap), (sin, (tm, ML_ROPE), rmap), (rot, rot.shape, lambda j, i: (0, 0))],
               [((ML_HEADS, s, ML_QK), BF, (None, tm, ML_QK), lambda j, i: (j, i, 0))], epi, comm=comm)[0]


def _mla_kv_proj(ckv, wukv, kr, tm=512, comm=()):
    s, k = ckv.shape
    tm = _tile(s, tm)

    def epi(accs, ex, out):
        acc = accs[0]
        out[0][:, :ML_NOPE] = acc[:, :ML_NOPE].astype(BF)
        out[0][:, ML_NOPE:] = ex[0][...]
        out[1][...] = acc[:, ML_NOPE:].astype(BF)

    rmap = lambda j, i: (i, 0)
    gmap = lambda j, i: (j, i, 0)
    return _mm("mla_kv_proj", (ML_HEADS, s // tm),
               [(ckv, (tm, k), rmap, wukv, (None, k, ML_NOPE + ML_V), lambda j, i: (j, 0, 0), "nn", 0, 0)],
               [(kr, (tm, ML_ROPE), rmap)],
               [((ML_HEADS, s, ML_QK), BF, (None, tm, ML_QK), gmap), ((ML_HEADS, s, ML_V), BF, (None, tm, ML_V), gmap)],
               epi, comm=comm)


def _mla_probs(q, k):
    sc = lax.dot_general(q, k, _DN["nt"], preferred_element_type=F32) * (ML_QK ** -0.5)
    e = jnp.exp(sc - jnp.max(sc, axis=-1, keepdims=True))
    return e / jnp.sum(e, axis=-1, keepdims=True)


def _mla_fwd(q, k, v, tq=512, comm=()):
    _, s, _ = q.shape
    tq = _tile(s, tq)

    def body(q_ref, k_ref, v_ref, o_ref):
        p = _mla_probs(q_ref[...], k_ref[...])
        o_ref[...] = jnp.dot(p.astype(BF), v_ref[...], preferred_element_type=F32).astype(BF)

    return _call("mla_fwd", body, (ML_HEADS, s // tq),
                 [pl.BlockSpec((None, tq, ML_QK), lambda h, i: (h, i, 0)),
                  pl.BlockSpec((None, s, ML_QK), lambda h, i: (h, 0, 0)),
                  pl.BlockSpec((None, s, ML_V), lambda h, i: (h, 0, 0))],
                 pl.BlockSpec((tq, ML_V), lambda h, i: (i, h)),
                 jax.ShapeDtypeStruct((s, ML_HEADS * ML_V), BF), [q, k, v], comm)


def _mla_bwd(q, k, v, do, tq=256, comm=()):
    _, s, _ = q.shape
    tq = _tile(s, tq)

    def body(q_ref, k_ref, v_ref, do_ref, dq_ref, dk_ref, dv_ref):
        i = pl.program_id(1)
        qv, kv, vv, dov = q_ref[...], k_ref[...], v_ref[...], do_ref[...]
        p = _mla_probs(qv, kv)
        dp = lax.dot_general(dov, vv, _DN["nt"], preferred_element_type=F32)
        ds = (p * (dp - jnp.sum(p * dp, axis=-1, keepdims=True)) * (ML_QK ** -0.5)).astype(BF)
        dq_ref[...] = jnp.dot(ds, kv, preferred_element_type=F32)
        _acc_rows(dk_ref, lax.dot_general(ds, qv, _DN["tn"], preferred_element_type=F32), i)
        _acc_rows(dv_ref, lax.dot_general(p.astype(BF), dov, _DN["tn"], preferred_element_type=F32), i)

    return _call(
        "mla_bwd", body, (ML_HEADS, s // tq),
        [pl.BlockSpec((None, tq, ML_QK), lambda h, i: (h, i, 0)),
         pl.BlockSpec((None, s, ML_QK), lambda h, i: (h, 0, 0)),
         pl.BlockSpec((None, s, ML_V), lambda h, i: (h, 0, 0)),
         pl.BlockSpec((tq, ML_V), lambda h, i: (i, h))],
        [pl.BlockSpec((None, tq, ML_QK), lambda h, i: (h, i, 0)),
         pl.BlockSpec((None, s, ML_QK), lambda h, i: (h, 0, 0)),
         pl.BlockSpec((None, s, ML_V), lambda h, i: (h, 0, 0))],
        [jax.ShapeDtypeStruct((ML_HEADS, s, ML_QK), F32), jax.ShapeDtypeStruct((ML_HEADS, s, ML_QK), F32),
         jax.ShapeDtypeStruct((ML_HEADS, s, ML_V), F32)],
        [q, k, v, do], comm)


def _mla_post(dq, dk, dv, cos, sin, rot_t, tm=256):
    _, s, _ = dq.shape
    tm = _tile(s, tm)

    def body(dq_ref, dk_ref, dv_ref, c_ref, s_ref, r_ref, dqp_ref, dkv_ref, dkr_ref):
        h = pl.program_id(1)
        dqv, dkk = dq_ref[...], dk_ref[...]
        dqp_ref[:, :ML_NOPE] = dqv[:, :ML_NOPE].astype(BF)
        dqp_ref[:, ML_NOPE:] = _unrope(dqv[:, ML_NOPE:], c_ref[...], s_ref[...], r_ref[...]).astype(BF)
        dkv_ref[:, :ML_NOPE] = dkk[:, :ML_NOPE].astype(BF)
        dkv_ref[:, ML_NOPE:] = dv_ref[...].astype(BF)
        _acc_rows(dkr_ref, dkk[:, ML_NOPE:], h)

    gspec = lambda c: pl.BlockSpec((None, tm, c), lambda i, h: (h, i, 0))
    rspec = pl.BlockSpec((tm, ML_ROPE), lambda i, h: (i, 0))
    return pl.pallas_call(
        body, name="mla_post", grid=(s // tm, ML_HEADS),
        in_specs=[gspec(ML_QK), gspec(ML_QK), gspec(ML_V), rspec, rspec,
                  pl.BlockSpec(rot_t.shape, lambda i, h: (0, 0))],
        out_specs=[gspec(ML_QK), gspec(ML_NOPE + ML_V), rspec],
        out_shape=[jax.ShapeDtypeStruct((ML_HEADS, s, ML_QK), BF),
                   jax.ShapeDtypeStruct((ML_HEADS, s, ML_NOPE + ML_V), BF),
                   jax.ShapeDtypeStruct((s, ML_ROPE), F32)],
        compiler_params=_params(2))(dq, dk, dv, cos, sin, rot_t)


def _mla_lat_bwd(dcq, dckv, dkr, lat, gq, gkv, cos, sin, rot_t, tm=256):
    s, w = lat.shape
    tm = _tile(s, tm)

    def body(dcq_ref, dckv_ref, dkr_ref, l_ref, gq_ref, gkv_ref, c_ref, s_ref, r_ref, dl_ref, dgq_ref, dgkv_ref):
        i = pl.program_id(0)
        dql, pq = _rms_bwd_math(dcq_ref[...], l_ref[:, :ML_RANK], gq_ref[...])
        dkl, pkv = _rms_bwd_math(dckv_ref[...], l_ref[:, ML_RANK:2 * ML_RANK], gkv_ref[...])
        dl_ref[:, :ML_RANK] = dql.astype(BF)
        dl_ref[:, ML_RANK:2 * ML_RANK] = dkl.astype(BF)
        dl_ref[:, 2 * ML_RANK:] = _unrope(dkr_ref[...], c_ref[...], s_ref[...], r_ref[...]).astype(BF)
        _acc_rows(dgq_ref, pq, i)
        _acc_rows(dgkv_ref, pkv, i)

    row = lambda c: pl.BlockSpec((tm, c), lambda i: (i, 0))
    full = lambda a: pl.BlockSpec(a.shape, lambda i: (0, 0))
    return pl.pallas_call(
        body, name="mla_lat_bwd", grid=(s // tm,),
        in_specs=[row(ML_RANK), row(ML_RANK), row(ML_ROPE), row(w), full(gq), full(gkv), row(ML_ROPE), row(ML_ROPE),
                  full(rot_t)],
        out_specs=[row(w), full(gq), full(gkv)],
        out_shape=[jax.ShapeDtypeStruct((s, w), BF), jax.ShapeDtypeStruct(gq.shape, F32),
                   jax.ShapeDtypeStruct(gkv.shape, F32)],
        compiler_params=_params(1))(dcq, dckv, dkr, lat, gq, gkv, cos, sin, rot_t)


def _grp_dw(name, a, dout, ta=512):
    s, k = a.shape
    ta = _tile(k, ta)
    if dout.ndim == 3:
        g, _, nb = dout.shape
        b_blk, b_map = (None, s, nb), lambda j, i: (j, 0, 0)
    else:
        g, nb = NDEV, dout.shape[1] // NDEV
        b_blk, b_map = (s, nb), lambda j, i: (0, j)
    return _mm(name, (g, k // ta),
               [(a, (s, ta), lambda j, i: (0, i), dout, b_blk, b_map, "tn", 0, 0)], [],
               [((g, k, nb), BF, (None, ta, nb), lambda j, i: (j, i, 0))], _store)[0]


def _grp_dw_t(name, dout, a, ta=512):
    g, s, nb = dout.shape
    k = a.shape[1]
    ta = _tile(k, ta)
    return _mm(name, (g, k // ta),
               [(dout, (None, s, nb), lambda j, i: (j, 0, 0), a, (s, ta), lambda j, i: (0, i), "tn", 0, 0)], [],
               [((g, nb, k), BF, (None, nb, ta), lambda j, i: (j, 0, i))], _store)[0]


def _grp_dx_t(name, dout, wt, tm=512, tn=512, comm=()):
    g, s, nb = dout.shape
    k = wt.shape[2]
    tm, tn = _tile(s, tm), _tile(k, tn)
    return _mm(name, (k // tn, s // tm),
               [(dout, (g, tm, nb), lambda j, i: (0, i, 0), wt, (g, nb, tn), lambda j, i: (0, 0, j), "nn", 0, g)], [],
               [((s, k), F32, (tm, tn), lambda j, i: (i, j))], _store, comm=comm)[0]


def _grp_dx(name, dout, w, tm=512, tn=512, comm=()):
    g, s, nb = dout.shape
    k = w.shape[1]
    tm, tn = _tile(s, tm), _tile(k, tn)
    return _mm(name, (k // tn, s // tm),
               [(dout, (g, tm, nb), lambda j, i: (0, i, 0), w, (g, tn, nb), lambda j, i: (0, j, 0), "nt", 0, g)], [],
               [((s, k), F32, (tm, tn), lambda j, i: (i, j))], _store, comm=comm)[0]


def _row_dw(name, a, dout, tn=512):
    s, n = dout.shape
    tn = _tile(n, tn)
    if a.ndim == 3:
        kb = a.shape[2]
        a_blk, a_map = (None, s, kb), lambda j, i: (j, 0, 0)
    else:
        kb = a.shape[1] // NDEV
        a_blk, a_map = (s, kb), lambda j, i: (0, j)
    return _mm(name, (NDEV, n // tn),
               [(a, a_blk, a_map, dout, (s, tn), lambda j, i: (0, i), "tn", 0, 0)], [],
               [((NDEV, kb, n), BF, (None, kb, tn), lambda j, i: (j, 0, i))], _store)[0]


def _mix_merge(oa, ob, wa, wb, ga, gb, tm=512, comm=()):
    s, k = oa.shape
    g, _, nb = wa.shape
    tm = _tile(s, tm)

    def epi(accs, ex, out):
        ya, yb = accs
        out[0][...] = ya.astype(BF)
        out[1][...] = yb.astype(BF)
        out[2][...] = (_sig(ex[0][...]) * ya + _sig(ex[1][...]) * yb).astype(BF)

    rmap = lambda j, i: (i, 0)
    wmap = lambda j, i: (j, 0, 0)
    o = ((g, s, nb), BF, (None, tm, nb), lambda j, i: (j, i, 0))
    cmap = lambda j, i: (i, j)
    return _mm("mix_merge", (g, s // tm),
               [(oa, (tm, k), rmap, wa, (None, k, nb), wmap, "nn", 0, 0),
                (ob, (tm, k), rmap, wb, (None, k, nb), wmap, "nn", 1, 0)],
               [(ga, (tm, nb), cmap), (gb, (tm, nb), cmap)], [o, o, o], epi, nacc=2, comm=comm)


def _mix_out(merged, wout, resid, tm=512, tn=512):
    g, s, kb = merged.shape
    d = wout.shape[2]
    tm, tn = _tile(s, tm), _tile(d, tn)

    def epi(accs, ex, out):
        out[0][...] = ex[0][...] + accs[0]

    return _mm("mix_out", (d // tn, s // tm),
               [(merged, (g, tm, kb), lambda j, i: (0, i, 0), wout, (g, kb, tn), lambda j, i: (0, 0, j), "nn", 0, g)],
               [(resid, (tm, tn), lambda j, i: (i, j))],
               [((s, d), F32, (tm, tn), lambda j, i: (i, j))], epi)[0]


def _mix_out_bwd(dh, wout, ga, gb, ya, yb, tm=512, comm=()):
    s, d = dh.shape
    g, kb, _ = wout.shape
    tm = _tile(s, tm)

    def epi(accs, ex, out):
        dm = accs[0]
        sa, sb = _sig(ex[0][...]), _sig(ex[1][...])
        out[0][...] = (dm * sa).astype(BF)
        out[1][...] = (dm * sb).astype(BF)
        out[2][...] = (dm * ex[2][...].astype(F32) * sa * (1.0 - sa)).astype(BF)
        out[3][...] = (dm * ex[3][...].astype(F32) * sb * (1.0 - sb)).astype(BF)

    cmap = lambda j, i: (i, j)
    gmap = lambda j, i: (j, i, 0)
    og = ((g, s, kb), BF, (None, tm, kb), gmap)
    oc = ((s, g * kb), BF, (tm, kb), cmap)
    return _mm("mix_out_bwd", (g, s // tm),
               [(dh, (tm, d), lambda j, i: (i, 0), wout, (None, kb, d), lambda j, i: (j, 0, 0), "nt", 0, 0)],
               [(ga, (tm, kb), cmap), (gb, (tm, kb), cmap), (ya, (None, tm, kb), gmap), (yb, (None, tm, kb), gmap)],
               [og, og, oc, oc], epi, comm=comm)


def _pl_forward(n4, wplg, p, wpl, h3, tm=512):
    s, d = n4.shape
    g, kb, _ = wplg.shape
    kp, nb = wpl.shape[1], wpl.shape[2]
    tm = _tile(s, tm)
    wplg_nat = wplg.reshape(g * kb, d)

    def epi(accs, ex, out):
        t, pe = accs
        out[0][...] = ex[0][...] + _sig(t) * pe
        out[1][...] = t
        out[2][...] = pe.astype(BF)

    rmap = lambda j, i: (i, 0)
    cmap = lambda j, i: (i, j)
    return _mm("pl_forward", (g, s // tm),
               [(n4, (tm, d), rmap, wplg_nat, (g * kb, nb), lambda j, i: (0, j), "nn", 0, 0),
                (p, (tm, kp), rmap, wpl, (None, kp, nb), lambda j, i: (j, 0, 0), "nn", 1, 0)],
               [(h3, (tm, nb), cmap)],
               [((s, d), F32, (tm, nb), cmap), ((s, d), F32, (tm, nb), cmap), ((s, d), BF, (tm, nb), cmap)],
               epi, nacc=2)


def _row_dx(name, dout, w, tm=512, comm=()):
    s, n = dout.shape
    g, kb, _ = w.shape
    tm = _tile(s, tm)
    return _mm(name, (g, s // tm),
               [(dout, (tm, n), lambda j, i: (i, 0), w, (None, kb, n), lambda j, i: (j, 0, 0), "nt", 0, 0)], [],
               [((s, g * kb), F32, (tm, kb), lambda j, i: (i, j))], _store, comm=comm)[0]


def _in_proj_bwd_x(pieces, weights, tm=512, tn=512, comm=()):
    s = pieces[0].shape[0]
    d = weights[0].shape[1]
    tm, tn = _tile(s, tm), _tile(d, tn)
    prods = [(pc, (tm, pc.shape[1]), lambda j, i: (i, 0), w, (w.shape[0], tn), lambda j, i: (0, j), "nn", 0, 0)
             for pc, w in zip(pieces, weights)]
    return _mm("in_proj_dx", (d // tn, s // tm), prods, [],
               [((s, d), F32, (tm, tn), lambda j, i: (i, j))], _store, comm=comm)[0]


def _split_w_in(w_in_t):
    g, nb, d = w_in_t.shape
    nat = w_in_t.reshape(g * nb, d)
    na, lat = 3 * NA_HEADS * NA_DIM, 2 * ML_RANK + ML_ROPE
    return nat[:na], nat[na:na + lat], nat[na + lat:na + lat + d], nat[na + lat + d:]


def _pair_sum(name, part, landed, core):
    _, _, r, c = part.shape
    tr, tc = _ew_tile(r, c)

    def body(core_ref, a_ref, b_ref, o_ref):
        o_ref[...] = (a_ref[...].astype(F32) + b_ref[...].astype(F32)).astype(o_ref.dtype)

    return pl.pallas_call(
        body, name=name,
        grid_spec=pltpu.PrefetchScalarGridSpec(
            num_scalar_prefetch=1, grid=(NCHIP, r // tr, c // tc),
            in_specs=[pl.BlockSpec((None, None, tr, tc), lambda j, i, k, core_ref: (j, core_ref[0], i, k)),
                      pl.BlockSpec((None, tr, tc), lambda j, i, k, core_ref: (j, i, k))],
            out_specs=pl.BlockSpec((None, tr, tc), lambda j, i, k, core_ref: (j, i, k))),
        out_shape=jax.ShapeDtypeStruct(landed.shape, landed.dtype), compiler_params=_params(3),
    )(core, part, landed)


def _device_step(x, p, target, sp, own, core):
    s, d = x.shape
    rows = s // GRID_W
    cos, sin, rot, rot_t = _rope_consts(s)
    w, dw4, sums, dsp, pending = {}, {}, {}, {}, []

    def gather(*names):
        return _GatherPart(names, [own[n] for n in names])

    def got(part):
        w.update(zip(part.names, part.results))

    def grad(name, g):
        dw4[name] = g.reshape((NCHIP, 2) + g.shape[1:])

    def to_sibling(*names):
        return _SiblingPart(names, [dw4[n] for n in names])

    def add_pairs(part):
        for n, landed in zip(part.names, part.results):
            sums[n] = _pair_sum("pair_sum_" + n, dw4[n], landed, core)

    def start_chips(tag, *names):
        send, recv, thru, lands, token = _chips_start("rs_start_" + tag, [sums[n] for n in names])
        pending.append((tag, names, send, recv, thru, lands))
        return token

    c0 = gather("ffn1_w_gate", "ffn1_w_up")
    _comm_only("gather_ffn1", [c0])
    got(c0)
    c1 = gather("ffn1_w_down")
    c2 = gather("w_in")

    def ffn1_wd():
        got(c1)
        return w["ffn1_w_down"]

    h1, ffn1_saved = _ffn_forward("ffn1", x, sp["ffn1_norm"], w["ffn1_w_gate"], w["ffn1_w_up"], ffn1_wd,
                                  up_comm=[c1], down_comm=[c2])
    got(c2)
    wqkv, wlat, wga, wgb = _split_w_in(w["w_in"])
    u = _rms_fwd("mix_norm", h1, sp["mix_norm"])
    c3 = gather("w_uq", "w_ukv")
    qkv = _mm_nt("in_qkv", u, wqkv, BF, tn=1024, comm=[c3])
    got(c3)
    lat = _mm_nt("in_lat", u, wlat, F32)
    c3a = gather("w_branch_a")
    ga = _mm_nt("in_ga", u, wga, F32, tn=1024, comm=[c3a])
    got(c3a)
    c3b = gather("w_branch_b")
    gb = _mm_nt("in_gb", u, wgb, F32, tn=1024, comm=[c3b])
    got(c3b)
    tb = _na_table(sp["na_rpb"], rows)
    c4 = gather("ffn2_w_gate")
    oa = _na_fwd(qkv, tb, comm=[c4])
    got(c4)
    cq, ckv, kr = _mla_prep(lat, sp["q_a_norm"], sp["kv_a_norm"], cos, sin, rot)
    c4a = gather("w_out")
    qf = _mla_q_proj(cq, w["w_uq"], cos, sin, rot, comm=[c4a])
    got(c4a)
    c4b = gather("w_pl_gate")
    kf, vf = _mla_kv_proj(ckv, w["w_ukv"], kr, comm=[c4b])
    got(c4b)
    c5 = gather("ffn2_w_up")
    ob = _mla_fwd(qf, kf, vf, comm=[c5])
    got(c5)
    c5a = gather("w_pl")
    ya, yb, merged = _mix_merge(oa, ob, w["w_branch_a"], w["w_branch_b"], ga, gb, comm=[c5a])
    got(c5a)
    h2 = _mix_out(merged, w["w_out"], h1)
    c6 = gather("ffn2_w_down")

    def ffn2_wd():
        got(c6)
        return w["ffn2_w_down"]

    h3, ffn2_saved = _ffn_forward("ffn2", h2, sp["ffn2_norm"], w["ffn2_w_gate"], w["ffn2_w_up"], ffn2_wd,
                                  up_comm=[c6])
    n4 = _rms_fwd("pl_norm", h3, sp["pl_norm"])
    pb = p.astype(BF)
    h4, t, pe = _pl_forward(n4, w["w_pl_gate"], pb, w["w_pl"], h3)

    dh4, dsp["final_norm"], loss = _loss_head(h4, target, sp["final_norm"])
    dt, dpe = _pl_bwd_elem(dh4, pe, t)
    grad("w_pl", _grp_dw("pl_dw", pb, dpe))
    grad("w_pl_gate", _row_dw("plg_dw", n4, dt))
    s1 = to_sibling("w_pl", "w_pl_gate")
    dn4 = _row_dx("plg_dx", dt, w["w_pl_gate"], comm=[s1])
    add_pairs(s1)
    tok = start_chips("pl", "w_pl", "w_pl_gate")
    dh3, dsp["pl_norm"] = _rms_bwd("pl_dnorm", dn4, h3, sp["pl_norm"], dh4, comm=[_After(tok)])

    xn, hg, hu, a = ffn2_saved
    dhb = dh3.astype(BF)
    grad("ffn2_w_down", _ffn_bwd_wd("ffn2_dwd", a, dhb))
    s2 = to_sibling("ffn2_w_down")
    dhg, dhu = _ffn_bwd_act("ffn2_dact", dhb, w["ffn2_w_down"], hg, hu, comm=[s2])
    add_pairs(s2)
    tok = start_chips("ffn2_down", "ffn2_w_down")
    dwg, dwu = _ffn_bwd_wup("ffn2_dwup", xn, dhg, dhu, comm=[_After(tok)])
    grad("ffn2_w_gate", dwg)
    grad("ffn2_w_up", dwu)
    s3 = to_sibling("ffn2_w_gate", "ffn2_w_up")
    dxn = _ffn_bwd_x("ffn2_dx", dhg, dhu, w["ffn2_w_gate"], w["ffn2_w_up"], comm=[s3])
    add_pairs(s3)
    tok = start_chips("ffn2_up", "ffn2_w_gate", "ffn2_w_up")
    dh2, dsp["ffn2_norm"] = _rms_bwd("ffn2_dnorm", dxn, h2, sp["ffn2_norm"], dh3, comm=[_After(tok)])

    dh2b = dh2.astype(BF)
    grad("w_out", _row_dw("out_dw", merged, dh2b))
    s4 = to_sibling("w_out")
    dya, dyb, dga, dgb = _mix_out_bwd(dh2b, w["w_out"], ga, gb, ya, yb, comm=[s4])
    add_pairs(s4)
    grad("w_branch_a", _grp_dw("bra_dw", oa, dya))
    grad("w_branch_b", _grp_dw("brb_dw", ob, dyb))
    doa = _grp_dx("bra_dx", dya, w["w_branch_a"]).astype(BF)
    s5 = to_sibling("w_branch_a", "w_branch_b")
    dob = _grp_dx("brb_dx", dyb, w["w_branch_b"], comm=[s5]).astype(BF)
    add_pairs(s5)
    tok = start_chips("mix", "w_out", "w_branch_a", "w_branch_b")

    dqf, dkf, dvf = _mla_bwd(qf, kf, vf, dob, comm=[_After(tok)])
    dqp, dkv, dkr = _mla_post(dqf, dkf, dvf, cos, sin, rot_t)
    grad("w_uq", _grp_dw_t("uq_dw", dqp, cq))
    grad("w_ukv", _grp_dw("ukv_dw", ckv, dkv))
    dcq = _grp_dx_t("uq_dx", dqp, w["w_uq"])
    s6 = to_sibling("w_uq", "w_ukv")
    dckv = _grp_dx("ukv_dx", dkv, w["w_ukv"], comm=[s6])
    add_pairs(s6)
    tok = start_chips("mla", "w_uq", "w_ukv")
    dlat, dsp["q_a_norm"], dsp["kv_a_norm"] = _mla_lat_bwd(dcq, dckv, dkr, lat, sp["q_a_norm"], sp["kv_a_norm"],
                                                         cos, sin, rot_t)
    dq_na, dk_na, dv_na, dtab = _na_bwd(qkv, tb, doa, comm=[_After(tok)])
    dsp["na_rpb"] = _na_rpb_grad(dtab, rows)
    dqkv = jnp.concatenate([dq_na, dk_na.astype(BF), dv_na.astype(BF)], axis=1)

    pieces = [dqkv, dlat, dga, dgb]
    dwin = jnp.zeros((sum(pc.shape[1] for pc in pieces), d), BF)
    row0 = 0
    for i, pc in enumerate(pieces):
        dwin = _mm_tn_into("in_dw%d" % i, pc, u, dwin, row0)
        row0 += pc.shape[1]
    grad("w_in", dwin.reshape(NDEV, -1, d))
    s7 = to_sibling("w_in")
    du = _in_proj_bwd_x(pieces, [wqkv, wlat, wga, wgb], comm=[s7])
    add_pairs(s7)
    tok = start_chips("w_in", "w_in")
    dh1, dsp["mix_norm"] = _rms_bwd("mix_dnorm", du, h1, sp["mix_norm"], dh2, comm=[_After(tok)])

    xn, hg, hu, a = ffn1_saved
    dhb = dh1.astype(BF)
    grad("ffn1_w_down", _ffn_bwd_wd("ffn1_dwd", a, dhb))
    s8 = to_sibling("ffn1_w_down")
    dhg, dhu = _ffn_bwd_act("ffn1_dact", dhb, w["ffn1_w_down"], hg, hu, comm=[s8])
    add_pairs(s8)
    tok = start_chips("ffn1_down", "ffn1_w_down")
    dwg, dwu = _ffn_bwd_wup("ffn1_dwup", xn, dhg, dhu, comm=[_After(tok)])
    grad("ffn1_w_gate", dwg)
    grad("ffn1_w_up", dwu)
    s9 = to_sibling("ffn1_w_gate", "ffn1_w_up")
    _comm_only("rs_sibling_ffn1", [s9])
    add_pairs(s9)
    tok = start_chips("ffn1_up", "ffn1_w_gate", "ffn1_w_up")
    dxn = _ffn_bwd_x("ffn1_dx", dhg, dhu, w["ffn1_w_gate"], w["ffn1_w_up"], comm=[_After(tok)])
    dx, dsp["ffn1_norm"] = _rms_bwd("ffn1_dnorm", dxn, x, sp["ffn1_norm"], dh1)
    return loss, dx, pending, dsp


def _gather_small(buf):
    def body(in_ref, out_ref, send_sems, recv_sems, local_sem):
        x, y, c = _coords()
        mine = pltpu.make_async_copy(in_ref, out_ref.at[4 * x + 2 * y + c], local_sem)
        mine.start()
        cps = []
        for k in range(1, NDEV):
            fx, fy, fc = (k >> 2) & 1, (k >> 1) & 1, k & 1
            peer = (x ^ fx, y ^ fy, c ^ fc)
            cps.append(pltpu.make_async_remote_copy(
                src_ref=in_ref, dst_ref=out_ref.at[4 * x + 2 * y + c], send_sem=send_sems.at[k - 1],
                recv_sem=recv_sems.at[k - 1], device_id=peer, device_id_type=MESH))
        for cp in cps:
            cp.start()
        for k in range(1, NDEV):
            fx, fy, fc = (k >> 2) & 1, (k >> 1) & 1, k & 1
            px, py, pc = x ^ fx, y ^ fy, c ^ fc
            pltpu.make_async_remote_copy(
                src_ref=in_ref, dst_ref=out_ref.at[4 * px + 2 * py + pc], send_sem=send_sems.at[k - 1],
                recv_sem=recv_sems.at[k - 1], device_id=(px, py, pc), device_id_type=MESH).wait_recv()
        for cp in cps:
            cp.wait_send()
        mine.wait()

    return pl.pallas_call(
        body, name="gather_small", in_specs=[ANY], out_specs=ANY,
        out_shape=jax.ShapeDtypeStruct((NDEV,) + buf.shape, buf.dtype),
        scratch_shapes=[pltpu.SemaphoreType.DMA((NDEV - 1,)), pltpu.SemaphoreType.DMA((NDEV - 1,)),
                        pltpu.SemaphoreType.DMA],
    )(buf)


def _adam_math(wv, g, m, v):
    m_new = B1 * m + (1.0 - B1) * g
    v_new = B2 * v + (1.0 - B2) * (g * g)
    m_hat = m_new / (1.0 - B1 ** STEP)
    v_hat = v_new / (1.0 - B2 ** STEP)
    return -LR * (m_hat / (jnp.sqrt(v_hat) + ADAM_EPS) + WD * wv), m_new, v_new


def _adam(name, parts, wv, m, v, after=None):
    npart, r, c = parts.shape
    tr, tc = _ew_tile(r, c)

    def body(p_ref, w_ref, m_ref, v_ref, *rest):
        g_ref, d_ref, mo_ref, vo_ref = rest[-4:]
        g = p_ref[0].astype(F32)
        for j in range(1, npart):
            g = g + p_ref[j].astype(F32)
        g_ref[...] = g
        d_ref[...], mo_ref[...], vo_ref[...] = _adam_math(w_ref[...], g, m_ref[...], v_ref[...])

    blk = pl.BlockSpec((tr, tc), lambda i, k: (i, k))
    extra = [] if after is None else [after]
    return pl.pallas_call(
        body, name=name, grid=(r // tr, c // tc),
        in_specs=[pl.BlockSpec((npart, tr, tc), lambda i, k: (0, i, k)), blk, blk, blk] + [ANY] * len(extra),
        out_specs=[blk] * 4, out_shape=[jax.ShapeDtypeStruct((r, c), F32)] * 4, compiler_params=_params(2),
    )(parts, wv, m, v, *extra)


def _adam_exchanged(name, sums, land, wv, m, v, my_chip):
    _, r, c = sums.shape
    tr, tc = _ew_tile(r, c)

    def body(chip_ref, s_ref, l_ref, w_ref, m_ref, v_ref, g_ref, d_ref, mo_ref, vo_ref):
        g = s_ref[...].astype(F32)
        for j in range(3):
            g = g + l_ref[j].astype(F32)
        g_ref[...] = g
        d_ref[...], mo_ref[...], vo_ref[...] = _adam_math(w_ref[...], g, m_ref[...], v_ref[...])

    blk = pl.BlockSpec((tr, tc), lambda i, k, chip_ref: (i, k))
    return pl.pallas_call(
        body, name=name,
        grid_spec=pltpu.PrefetchScalarGridSpec(
            num_scalar_prefetch=1, grid=(r // tr, c // tc),
            in_specs=[pl.BlockSpec((None, tr, tc), lambda i, k, chip_ref: (chip_ref[0], i, k)),
                      pl.BlockSpec((3, tr, tc), lambda i, k, chip_ref: (0, i, k)), blk, blk, blk],
            out_specs=[blk] * 4),
        out_shape=[jax.ShapeDtypeStruct((r, c), F32)] * 4, compiler_params=_params(2),
    )(my_chip, sums, land, wv, m, v)


SHARDED = ("ffn1_w_gate", "ffn1_w_up", "ffn1_w_down", "w_in", "w_uq", "w_ukv", "w_branch_a", "w_branch_b", "w_out",
           "ffn2_w_gate", "ffn2_w_up", "ffn2_w_down", "w_pl", "w_pl_gate")
TRANSPOSED = ("ffn1_w_gate", "ffn1_w_up", "ffn2_w_gate", "ffn2_w_up", "w_in", "w_uq")
REPLICATED = ("ffn1_norm", "mix_norm", "q_a_norm", "kv_a_norm", "na_rpb", "ffn2_norm", "pl_norm", "final_norm")
WEIGHTS = ("ffn1_norm", "ffn1_w_gate", "ffn1_w_up", "ffn1_w_down", "mix_norm", "w_in", "q_a_norm", "w_uq",
           "kv_a_norm", "w_ukv", "na_rpb", "w_branch_a", "w_branch_b", "w_out", "ffn2_norm", "ffn2_w_gate",
           "ffn2_w_up", "ffn2_w_down", "pl_norm", "w_pl", "w_pl_gate", "final_norm")
SMALL_W = 2048


def _pack_small(vals):
    rows = []
    for name in REPLICATED:
        flat = vals[name].reshape(-1).astype(F32)
        n = -(-flat.shape[0] // SMALL_W) * SMALL_W
        rows.append(jnp.pad(flat, (0, n - flat.shape[0])).reshape(-1, SMALL_W))
    return jnp.concatenate(rows, axis=0)


def _unpack_small(buf, shapes):
    out, r = {}, 0
    for name in REPLICATED:
        size = int(np.prod(shapes[name]))
        nrow = -(-size // SMALL_W)
        out[name] = buf[r:r + nrow].reshape(-1)[:size].reshape(shapes[name])
        r += nrow
    return out


def kernel(x, p, ffn1_norm, ffn1_w_gate, ffn1_w_up, ffn1_w_down, mix_norm, w_in, q_a_norm, w_uq, kv_a_norm, w_ukv, na_rpb, w_branch_a, w_branch_b, w_out, ffn2_norm, ffn2_w_gate, ffn2_w_up, ffn2_w_down, pl_norm, w_pl, w_pl_gate, final_norm, loss_target, m_ffn1_norm, m_ffn1_w_gate, m_ffn1_w_up, m_ffn1_w_down, m_mix_norm, m_w_in, m_q_a_norm, m_w_uq, m_kv_a_norm, m_w_ukv, m_na_rpb, m_w_branch_a, m_w_branch_b, m_w_out, m_ffn2_norm, m_ffn2_w_gate, m_ffn2_w_up, m_ffn2_w_down, m_pl_norm, m_w_pl, m_w_pl_gate, m_final_norm, v_ffn1_norm, v_ffn1_w_gate, v_ffn1_w_up, v_ffn1_w_down, v_mix_norm, v_w_in, v_q_a_norm, v_w_uq, v_kv_a_norm, v_w_ukv, v_na_rpb, v_w_branch_a, v_w_branch_b, v_w_out, v_ffn2_norm, v_ffn2_w_gate, v_ffn2_w_up, v_ffn2_w_down, v_pl_norm, v_w_pl, v_w_pl_gate, v_final_norm):
    args = dict(locals())
    wts = {n: args[n] for n in WEIGHTS}
    mom = {n: args["m_" + n] for n in WEIGHTS}
    var = {n: args["v_" + n] for n in WEIGHTS}
    shapes = {n: wts[n].shape for n in WEIGHTS}
    core = lax.axis_index("c").astype(jnp.int32).reshape(1)

    local = lambda n, a: a[0].T if n in TRANSPOSED else a[0]
    own = {n: local(n, wts[n]).astype(BF) for n in SHARDED}
    sp = {n: wts[n].reshape(1, -1) for n in REPLICATED if n != "na_rpb"}
    sp["na_rpb"] = wts["na_rpb"][0]
    loss_part, grad_x, pending, dsp = _device_step(x[0], p[0, 0], loss_target[0], sp, own, core)

    out = {}
    last = grad_x
    my_chip = (2 * lax.axis_index("x") + lax.axis_index("y")).astype(jnp.int32).reshape(1)
    for tag, names, send, recv, thru, lands in pending:
        thru, lands = _chips_wait("rs_wait_" + tag, send, recv, thru, lands, last)
        for n, s4, l3 in zip(names, thru, lands):
            res4 = _adam_exchanged("adam_" + n, s4, l3, local(n, wts[n]), local(n, mom[n]), local(n, var[n]), my_chip)
            out[n] = tuple((a.T if n in TRANSPOSED else a)[None] for a in res4)
            last = res4[1]

    small = jnp.concatenate([_pack_small(dsp), jnp.pad(loss_part, ((0, 0), (0, SMALL_W - loss_part.shape[1])))], 0)
    pad_rows = -small.shape[0] % 8
    small = jnp.pad(small, ((0, pad_rows), (0, 0)))
    every = _gather_small(small)
    zeros = jnp.zeros((1 + pad_rows, SMALL_W), F32)
    pack = lambda d: jnp.concatenate([_pack_small(d), zeros], 0)
    g_s, d_s, m_s, v_s = _adam("adam_small", every, pack(wts), pack(mom), pack(var))
    n_rows = small.shape[0] - 1 - pad_rows
    loss = g_s[n_rows, 0]
    small_out = [_unpack_small(b, shapes) for b in (g_s, d_s, m_s, v_s)]
    for n in REPLICATED:
        out[n] = tuple(b[n] for b in small_out)

    res = [loss, grad_x[None]]
    for k in range(4):
        res += [out[n][k] for n in WEIGHTS]
    return tuple(res)
```

```python
import functools

import numpy as np
import jax
import jax.numpy as jnp
from jax import lax
from jax.experimental import pallas as pl
from jax.experimental.pallas import tpu as pltpu

F32 = jnp.float32
BF = jnp.bfloat16
MESH = pl.DeviceIdType.MESH

NDEV = 8
NCHIP = 4
VMEM_LIMIT = 56 * 1024 * 1024
EPS = 1e-6
NEG = -1e30
GRID_W = 64
NA_HEADS, NA_DIM = 8, 128
NA_ROWS_WIN, NA_COLS_WIN = 8, 16
NA_HG = 4
NA_QROWS = 4
ML_HEADS, ML_NOPE, ML_ROPE, ML_V = 8, 128, 64, 128
ML_QK = ML_NOPE + ML_ROPE
ML_RANK = 512
ROPE_THETA = 10000.0
LR, B1, B2, ADAM_EPS, WD, STEP = 0.001, 0.9, 0.999, 1e-08, 0.01, 10
HI = lax.Precision.HIGHEST

_DN = {"nn": (((1,), (0,)), ((), ())), "nt": (((1,), (1,)), ((), ())), "tn": (((0,), (0,)), ((), ()))}


def _params(n):
    return pltpu.CompilerParams(dimension_semantics=("arbitrary",) * n, vmem_limit_bytes=VMEM_LIMIT)


def _sig(v):
    return jax.nn.sigmoid(v)


ANY = pl.BlockSpec(memory_space=pl.ANY)


def _coords():
    return lax.axis_index("x"), lax.axis_index("y"), lax.axis_index("c")


class _Part:
    inputs, out_shapes, sem_shapes, results = (), (), (), None

    def mid(self, ins, outs, sems):
        pass

    def late(self, ins, outs, sems):
        pass


class _After(_Part):
    def __init__(self, token):
        self.inputs = [token]

    def start(self, ins, outs, sems):
        pass

    finish = start


class _GatherPart(_Part):
    def __init__(self, names, shards):
        n = len(shards)
        self.names, self.inputs = list(names), list(shards)
        self.out_shapes = [jax.ShapeDtypeStruct((NDEV,) + a.shape, a.dtype) for a in shards]
        self.sem_shapes = [pltpu.SemaphoreType.DMA((n, 7)), pltpu.SemaphoreType.DMA((n, 7)),
                           pltpu.SemaphoreType.DMA((n,))]

    def _plan(self, ins, outs, sems):
        send_sems, recv_sems, local_sems = sems
        x, y, c = _coords()
        me, sib, diag = (x, y, c), (x, y, 1 - c), (1 - x, 1 - y, c)
        n1, n2 = (x ^ (1 - c), y ^ c, c), (x ^ c, y ^ (1 - c), c)

        def copy(i, k, block, to, src=None):
            px, py, pc = block
            dst = outs[i].at[4 * px + 2 * py + pc]
            return pltpu.make_async_remote_copy(
                src_ref=dst if src is None else src, dst_ref=dst, send_sem=send_sems.at[i, k],
                recv_sem=recv_sems.at[i, k], device_id=to, device_id_type=MESH)

        mine = [pltpu.make_async_copy(ins[i], outs[i].at[4 * x + 2 * y + c], local_sems.at[i])
                for i in range(len(ins))]
        return copy, mine, me, sib, n1, n2, diag

    def _own_sends(self, ins, copy, me, sib, n1, n2):
        return [copy(i, k, me, to, src=ins[i]) for i in range(len(ins)) for k, to in enumerate((sib, n1, n2))]

    def start(self, ins, outs, sems):
        copy, mine, me, sib, n1, n2, _ = self._plan(ins, outs, sems)
        for cp in mine + self._own_sends(ins, copy, me, sib, n1, n2):
            cp.start()

    def mid(self, ins, outs, sems):
        copy, _, me, sib, n1, n2, _ = self._plan(ins, outs, sems)
        for i in range(len(ins)):
            copy(i, 1, n1, me).wait_recv()
            copy(i, 3, n1, n2).start()
            copy(i, 4, n1, sib).start()

    def late(self, ins, outs, sems):
        copy, _, me, sib, _, n2, diag = self._plan(ins, outs, sems)
        for i in range(len(ins)):
            copy(i, 2, n2, me).wait_recv()
            copy(i, 5, n2, sib).start()
        for i in range(len(ins)):
            copy(i, 3, diag, me).wait_recv()
            copy(i, 6, diag, sib).start()

    def finish(self, ins, outs, sems):
        copy, mine, me, sib, n1, n2, diag = self._plan(ins, outs, sems)
        other = lambda dev: (dev[0], dev[1], sib[2])
        n = len(ins)
        for i in range(n):
            copy(i, 0, sib, me).wait_recv()
            for k, block in ((4, other(n2)), (5, other(n1)), (6, other(diag))):
                copy(i, k, block, me).wait_recv()
        for cp in self._own_sends(ins, copy, me, sib, n1, n2):
            cp.wait_send()
        for i in range(n):
            for k, block in ((3, n1), (4, n1), (5, n2), (6, diag)):
                copy(i, k, block, sib).wait_send()
        for cp in mine:
            cp.wait()


class _SiblingPart(_Part):
    def __init__(self, names, parts):
        n = len(parts)
        self.names, self.inputs = list(names), list(parts)
        self.out_shapes = [jax.ShapeDtypeStruct((NCHIP,) + a.shape[2:], a.dtype) for a in parts]
        self.sem_shapes = [pltpu.SemaphoreType.DMA((n,)), pltpu.SemaphoreType.DMA((n,))]

    def _copies(self, ins, outs, sems):
        x, y, c = _coords()
        return [pltpu.make_async_remote_copy(
            src_ref=ins[i].at[:, 1 - c], dst_ref=outs[i], send_sem=sems[0].at[i], recv_sem=sems[1].at[i],
            device_id=(x, y, 1 - c), device_id_type=MESH) for i in range(len(ins))]

    def start(self, ins, outs, sems):
        for cp in self._copies(ins, outs, sems):
            cp.start()

    def finish(self, ins, outs, sems):
        cps = self._copies(ins, outs, sems)
        for cp in cps:
            cp.wait_recv()
        for cp in cps:
            cp.wait_send()


HBM = pl.BlockSpec(memory_space=pltpu.HBM)
SEM = pl.BlockSpec(memory_space=pltpu.SEMAPHORE)


def _chip_peers():
    x, y, c = _coords()
    return [(1 - x, y, c), (x, 1 - y, c), (1 - x, 1 - y, c)]


def _chips_start(name, sums):
    n = len(sums)

    def body(*refs):
        ins, lands, send_sems, recv_sems = refs[:n], refs[n:2 * n], refs[2 * n], refs[2 * n + 1]
        for i in range(n):
            for k, (px, py, pc) in enumerate(_chip_peers()):
                pltpu.make_async_remote_copy(
                    src_ref=ins[i].at[2 * px + py], dst_ref=lands[i].at[k], send_sem=send_sems.at[3 * i + k],
                    recv_sem=recv_sems.at[3 * i + k], device_id=(px, py, pc), device_id_type=MESH).start()
        refs[-1][...] = jnp.zeros_like(refs[-1])

    lands = [lax.empty((3,) + a.shape[1:], a.dtype) for a in sums]
    bufs = list(sums) + lands
    res = pl.pallas_call(
        body, name=name, in_specs=[HBM] * (2 * n),
        out_specs=(SEM, SEM, *[HBM] * (2 * n), pl.BlockSpec(memory_space=pltpu.VMEM)),
        out_shape=(pltpu.SemaphoreType.DMA((3 * n,)), pltpu.SemaphoreType.DMA((3 * n,)),
                   *[pltpu.HBM(a.shape, a.dtype) for a in bufs], jax.ShapeDtypeStruct((8, 128), F32)),
        input_output_aliases={i: 2 + i for i in range(2 * n)},
        compiler_params=pltpu.CompilerParams(has_side_effects=pltpu.SideEffectType.DATAFLOW_SIDE_EFFECTING),
    )(*[pltpu.with_memory_space_constraint(a, pltpu.HBM) for a in bufs])
    return res[0], res[1], list(res[2:2 + n]), list(res[2 + n:2 + 2 * n]), res[-1]


def _chips_wait(name, send_sems, recv_sems, sums, lands, after):
    n = len(sums)

    def body(*refs):
        ins, zones, send, recv = refs[:n], refs[n:2 * n], refs[2 * n], refs[2 * n + 1]
        for i in range(n):
            for k, peer in enumerate(_chip_peers()):
                cp = pltpu.make_async_remote_copy(
                    src_ref=ins[i].at[0], dst_ref=zones[i].at[k], send_sem=send.at[3 * i + k],
                    recv_sem=recv.at[3 * i + k],
                    device_id=peer, device_id_type=MESH)
                cp.wait_send()
                cp.wait_recv()

    bufs = list(sums) + list(lands)
    res = pl.pallas_call(
        body, name=name, in_specs=[HBM] * (2 * n) + [SEM, SEM, ANY], out_specs=[HBM] * (2 * n),
        out_shape=[pltpu.HBM(a.shape, a.dtype) for a in bufs], input_output_aliases={i: i for i in range(2 * n)},
        compiler_params=pltpu.CompilerParams(has_side_effects=pltpu.SideEffectType.DATAFLOW_SIDE_EFFECTING),
    )(*bufs, send_sems, recv_sems, after)
    return list(res[:n]), list(res[n:])


def _call(name, body, grid, in_specs, out_specs, out_shape, args, comm=(), scratch=()):
    comm = [p for p in comm if p is not None]
    single = not isinstance(out_shape, (list, tuple))
    o_specs = [out_specs] if single else list(out_specs)
    o_shape = [out_shape] if single else list(out_shape)
    n_in, n_out = len(in_specs), len(o_specs)
    c_in = [a for p in comm for a in p.inputs]
    c_out = [s for p in comm for s in p.out_shapes]
    c_sem = [s for p in comm for s in p.sem_shapes]

    def wrapped(*refs):
        ins, outs = refs[:n_in], refs[n_in + len(c_in):n_in + len(c_in) + n_out]
        pos = [n_in, n_in + len(c_in) + n_out, n_in + len(c_in) + n_out + len(c_out)]
        own = refs[pos[2]:pos[2] + len(scratch)]
        pos[2] += len(scratch)
        split = []
        for p in comm:
            sizes = [len(p.inputs), len(p.out_shapes), len(p.sem_shapes)]
            split.append([refs[o:o + n] for o, n in zip(pos, sizes)])
            pos = [o + n for o, n in zip(pos, sizes)]
        step, steps = 0, 1
        for a, g in enumerate(grid):
            step, steps = step * g + pl.program_id(a), steps * g

        def run(which, at):
            def go():
                for p, cut in zip(comm, split):
                    getattr(p, which)(*cut)
            if not comm:
                return
            if grid:
                pl.when(step == at)(go)
            else:
                go()

        run("start", 0)
        body(*ins, *outs, *own)
        run("mid", steps // 2)
        run("late", max(steps // 2, steps - 1 - max(1, steps // 8)))
        run("finish", steps - 1)

    res = pl.pallas_call(
        wrapped, name=name, grid=grid, in_specs=list(in_specs) + [ANY] * len(c_in),
        out_specs=o_specs + [ANY] * len(c_out), out_shape=o_shape + c_out, scratch_shapes=list(scratch) + c_sem,
        compiler_params=_params(len(grid)),
    )(*args, *c_in)
    pos = n_out
    for p in comm:
        p.results = list(res[pos:pos + len(p.out_shapes)])
        pos += len(p.out_shapes)
    return res[0] if single else list(res[:n_out])


def _comm_only(name, comm):
    def body(o_ref):
        o_ref[...] = jnp.zeros_like(o_ref)

    _call(name, body, (), [], pl.BlockSpec(memory_space=pltpu.VMEM), jax.ShapeDtypeStruct((8, 128), F32), [], comm)


def _mm(name, grid, prods, extras, outs, epi, nacc=1, comm=()):
    n_p, n_e = len(prods), len(extras)

    def body(*refs):
        ab, ex, out = refs[:2 * n_p], refs[2 * n_p:2 * n_p + n_e], refs[2 * n_p + n_e:]
        accs = [None] * nacc
        for i, prod in enumerate(prods):
            dn, acc, loop = prod[6], prod[7], prod[8]
            a_ref, b_ref = ab[2 * i], ab[2 * i + 1]
            if loop:
                for g in range(loop):
                    t = lax.dot_general(a_ref[g], b_ref[g], _DN[dn], preferred_element_type=F32)
                    accs[acc] = t if accs[acc] is None else accs[acc] + t
            else:
                t = lax.dot_general(a_ref[...], b_ref[...], _DN[dn], preferred_element_type=F32)
                accs[acc] = t if accs[acc] is None else accs[acc] + t
        epi(accs, ex, out)

    in_specs, args = [], []
    for prod in prods:
        in_specs += [pl.BlockSpec(prod[1], prod[2]), pl.BlockSpec(prod[4], prod[5])]
        args += [prod[0], prod[3]]
    for e, e_blk, e_map in extras:
        in_specs.append(pl.BlockSpec(e_blk, e_map))
        args.append(e)
    return _call(name, body, grid, in_specs, [pl.BlockSpec(blk, mp) for _, _, blk, mp in outs],
                 [jax.ShapeDtypeStruct(s, d) for s, d, _, _ in outs], args, comm)


def _store(accs, ex, out):
    out[0][...] = accs[0].astype(out[0].dtype)


def _ew_tile(r, c, budget=3 << 19):
    for t in range(r - r % 16, 0, -16):
        if r % t == 0 and t * c * 4 <= budget:
            return t, c
    for t in range(c - c % 128, 0, -128):
        if c % t == 0 and r * t * 4 <= budget:
            return r, t
    return r, c


def _tile(n, want):
    t = min(n, want)
    assert n % t == 0, (n, want)
    return t


def _mm_nn(name, a, b, out_dtype, tm=512, tn=512, comm=()):
    m, k = a.shape
    n = b.shape[1]
    tm, tn = _tile(m, tm), (tn if n % tn == 0 else n)
    return _mm(name, (n // tn, m // tm),
               [(a, (tm, k), lambda j, i: (i, 0), b, (k, tn), lambda j, i: (0, j), "nn", 0, 0)], [],
               [((m, n), out_dtype, (tm, tn), lambda j, i: (i, j))], _store, comm=comm)[0]


def _mm_nt(name, a, bt, out_dtype, tm=512, tn=512, comm=()):
    m, k = a.shape
    n = bt.shape[0]
    tm, tn = _tile(m, tm), (tn if n % tn == 0 else n)
    return _mm(name, (n // tn, m // tm),
               [(a, (tm, k), lambda j, i: (i, 0), bt, (tn, k), lambda j, i: (j, 0), "nt", 0, 0)], [],
               [((m, n), out_dtype, (tm, tn), lambda j, i: (i, j))], _store, comm=comm)[0]


def _mm_tn_into(name, a, b, buf, row0, ta=1024, tb=512):
    t, ka = a.shape
    nb = b.shape[1]
    ta, tb = (ta if ka % ta == 0 else ka), (tb if nb % tb == 0 else nb)

    def body(a_ref, b_ref, buf_in, buf_out, tile, sem):
        i, j = pl.program_id(0), pl.program_id(1)
        tile[...] = lax.dot_general(a_ref[...], b_ref[...], _DN["tn"], preferred_element_type=F32).astype(tile.dtype)
        rows = pl.ds(pl.multiple_of(row0 + i * ta, 16), ta)
        cp = pltpu.make_async_copy(tile, buf_out.at[rows, pl.ds(pl.multiple_of(j * tb, 128), tb)], sem)
        cp.start()
        cp.wait()

    return pl.pallas_call(
        body, name=name, grid=(ka // ta, nb // tb),
        in_specs=[pl.BlockSpec((t, ta), lambda i, j: (0, i)), pl.BlockSpec((t, tb), lambda i, j: (0, j)), ANY],
        out_specs=ANY, out_shape=jax.ShapeDtypeStruct(buf.shape, buf.dtype), input_output_aliases={2: 0},
        scratch_shapes=[pltpu.VMEM((ta, tb), buf.dtype), pltpu.SemaphoreType.DMA],
        compiler_params=_params(2))(a, b, buf)


def _mm_tn(name, a, b, out_dtype, ta=512, tb=512, scale=None):
    t, ka = a.shape
    nb = b.shape[1]
    ta, tb = (ta if ka % ta == 0 else ka), (tb if nb % tb == 0 else nb)

    def epi(accs, ex, out):
        v = accs[0] if scale is None else accs[0] * scale
        out[0][...] = v.astype(out[0].dtype)

    return _mm(name, (ka // ta, nb // tb),
               [(a, (t, ta), lambda i, j: (0, i), b, (t, tb), lambda i, j: (0, j), "tn", 0, 0)], [],
               [((ka, nb), out_dtype, (ta, tb), lambda i, j: (i, j))], epi)[0]


def _rms_fwd(name, x, g, tm=256, comm=()):
    s, d = x.shape
    tm = _tile(s, tm)

    def body(x_ref, g_ref, o_ref):
        v = x_ref[...]
        o_ref[...] = (v * lax.rsqrt(jnp.mean(v * v, axis=-1, keepdims=True) + EPS) * g_ref[...]).astype(o_ref.dtype)

    return _call(name, body, (s // tm,),
                 [pl.BlockSpec((tm, d), lambda i: (i, 0)), pl.BlockSpec((1, d), lambda i: (0, 0))],
                 pl.BlockSpec((tm, d), lambda i: (i, 0)), jax.ShapeDtypeStruct((s, d), BF), [x, g], comm)


def _acc_rows(ref, part, i):
    @pl.when(i == 0)
    def _():
        ref[...] = part

    @pl.when(i > 0)
    def _():
        ref[...] += part


def _rms_bwd_math(dn, v, g):
    rstd = lax.rsqrt(jnp.mean(v * v, axis=-1, keepdims=True) + EPS)
    xh = v * rstd
    dxh = dn * g
    dx = rstd * (dxh - xh * jnp.mean(dxh * xh, axis=-1, keepdims=True))
    return dx, jnp.sum(dn * xh, axis=0, keepdims=True)


def _rms_bwd(name, dn, x, g, resid, tm=256, comm=()):
    s, d = x.shape
    tm = _tile(s, tm)

    def body(dn_ref, x_ref, g_ref, r_ref, dx_ref, dg_ref):
        dx, part = _rms_bwd_math(dn_ref[...].astype(F32), x_ref[...], g_ref[...])
        dx_ref[...] = r_ref[...] + dx
        _acc_rows(dg_ref, part, pl.program_id(0))

    row = pl.BlockSpec((tm, d), lambda i: (i, 0))
    one = pl.BlockSpec((1, d), lambda i: (0, 0))
    return _call(name, body, (s // tm,), [row, row, one, row], [row, one],
                 [jax.ShapeDtypeStruct((s, d), F32), jax.ShapeDtypeStruct((1, d), F32)], [dn, x, g, resid], comm)


def _loss_head(h, target, g, tm=256):
    s, d = h.shape
    tm = _tile(s, tm)

    def body(h_ref, t_ref, g_ref, dh_ref, dg_ref, loss_ref):
        v, gv = h_ref[...], g_ref[...]
        rstd = lax.rsqrt(jnp.mean(v * v, axis=-1, keepdims=True) + EPS)
        xh = v * rstd
        err = xh * gv - t_ref[...]
        part_loss = 0.5 * jnp.sum(jnp.mean(err * err, axis=-1, keepdims=True), axis=0, keepdims=True)
        dy = err * (1.0 / d)
        dxh = dy * gv
        dh_ref[...] = rstd * (dxh - xh * jnp.mean(dxh * xh, axis=-1, keepdims=True))
        i = pl.program_id(0)
        _acc_rows(dg_ref, jnp.sum(dy * xh, axis=0, keepdims=True), i)
        _acc_rows(loss_ref, jnp.broadcast_to(part_loss, loss_ref.shape), i)

    row = pl.BlockSpec((tm, d), lambda i: (i, 0))
    one = pl.BlockSpec((1, d), lambda i: (0, 0))
    return pl.pallas_call(
        body, name="loss_head", grid=(s // tm,), in_specs=[row, row, one],
        out_specs=[row, one, pl.BlockSpec((1, 128), lambda i: (0, 0))],
        out_shape=[jax.ShapeDtypeStruct((s, d), F32), jax.ShapeDtypeStruct((1, d), F32),
                   jax.ShapeDtypeStruct((1, 128), F32)],
        compiler_params=_params(1))(h, target, g)


def _pl_bwd_elem(dh, pe, t, tm=256):
    s, d = dh.shape
    tm = _tile(s, tm)

    def body(dh_ref, pe_ref, t_ref, dt_ref, dpe_ref):
        dh_v, sg = dh_ref[...], _sig(t_ref[...])
        dt_ref[...] = (dh_v * pe_ref[...].astype(F32) * sg * (1.0 - sg)).astype(BF)
        dpe_ref[...] = (dh_v * sg).astype(BF)

    row = pl.BlockSpec((tm, d), lambda i: (i, 0))
    return pl.pallas_call(
        body, name="pl_bwd_elem", grid=(s // tm,), in_specs=[row, row, row], out_specs=[row, row],
        out_shape=[jax.ShapeDtypeStruct((s, d), BF)] * 2, compiler_params=_params(1))(dh, pe, t)


def _ffn_up(name, xn, wg, wu, tm=1024, comm=()):
    s, d = xn.shape
    g, fb, _ = wg.shape
    tm = _tile(s, tm)

    def epi(accs, ex, out):
        hg, hu = accs
        out[0][...] = hg.astype(BF)
        out[1][...] = hu.astype(BF)
        out[2][...] = (hg * _sig(hg) * hu).astype(BF)

    a_map = lambda j, i: (i, 0)
    w_map = lambda j, i: (j, 0, 0)
    o = ((g, s, fb), BF, (None, tm, fb), lambda j, i: (j, i, 0))
    return _mm(name, (g, s // tm),
               [(xn, (tm, d), a_map, wg, (None, fb, d), w_map, "nt", 0, 0),
                (xn, (tm, d), a_map, wu, (None, fb, d), w_map, "nt", 1, 0)], [], [o, o, o], epi, nacc=2, comm=comm)


def _ffn_down(name, a, wd, resid, tm=512, tn=512, comm=()):
    g, s, fb = a.shape
    d = wd.shape[2]
    tm, tn = _tile(s, tm), _tile(d, tn)

    def epi(accs, ex, out):
        out[0][...] = ex[0][...] + 0.5 * accs[0]

    return _mm(name, (d // tn, s // tm),
               [(a, (g, tm, fb), lambda j, i: (0, i, 0), wd, (g, fb, tn), lambda j, i: (0, 0, j), "nn", 0, g)],
               [(resid, (tm, tn), lambda j, i: (i, j))],
               [((s, d), F32, (tm, tn), lambda j, i: (i, j))], epi, comm=comm)[0]


def _ffn_bwd_act(name, dh, wd, hg, hu, tm=1024, comm=()):
    s, d = dh.shape
    g, fb, _ = wd.shape
    tm = _tile(s, tm)

    def epi(accs, ex, out):
        da = 0.5 * accs[0]
        hg_v, hu_v = ex[0][...].astype(F32), ex[1][...].astype(F32)
        sg = _sig(hg_v)
        out[0][...] = (da * hu_v * (sg * (1.0 + hg_v * (1.0 - sg)))).astype(BF)
        out[1][...] = (da * (hg_v * sg)).astype(BF)

    blk = (None, tm, fb)
    gmap = lambda j, i: (j, i, 0)
    return _mm(name, (g, s // tm),
               [(dh, (tm, d), lambda j, i: (i, 0), wd, (None, fb, d), lambda j, i: (j, 0, 0), "nt", 0, 0)],
               [(hg, blk, gmap), (hu, blk, gmap)],
               [((g, s, fb), BF, blk, gmap), ((g, s, fb), BF, blk, gmap)], epi, comm=comm)


def _ffn_bwd_wd(name, a, dh, tn=1024, comm=()):
    g, s, fb = a.shape
    d = dh.shape[1]
    tn = _tile(d, tn)

    def epi(accs, ex, out):
        out[0][...] = (0.5 * accs[0]).astype(BF)

    return _mm(name, (g, d // tn),
               [(a, (None, s, fb), lambda j, i: (j, 0, 0), dh, (s, tn), lambda j, i: (0, i), "tn", 0, 0)], [],
               [((g, fb, d), BF, (None, fb, tn), lambda j, i: (j, 0, i))], epi, comm=comm)[0]


def _ffn_bwd_wup(name, xn, dhg, dhu, tk=1024, comm=()):
    s, d = xn.shape
    g, _, fb = dhg.shape
    tk = _tile(d, tk)

    def epi(accs, ex, out):
        out[0][...] = accs[0].astype(BF)
        out[1][...] = accs[1].astype(BF)

    a_map = lambda j, i: (j, 0, 0)
    b_map = lambda j, i: (0, i)
    o = ((g, fb, d), BF, (None, fb, tk), lambda j, i: (j, 0, i))
    return _mm(name, (g, d // tk),
               [(dhg, (None, s, fb), a_map, xn, (s, tk), b_map, "tn", 0, 0),
                (dhu, (None, s, fb), a_map, xn, (s, tk), b_map, "tn", 1, 0)], [], [o, o], epi, nacc=2, comm=comm)


def _ffn_bwd_x(name, dhg, dhu, wg, wu, tm=512, tn=512, comm=()):
    g, s, fb = dhg.shape
    d = wg.shape[2]
    tm, tn = _tile(s, tm), _tile(d, tn)
    a_blk, a_map = (g, tm, fb), lambda j, i: (0, i, 0)
    b_blk, b_map = (g, fb, tn), lambda j, i: (0, 0, j)
    return _mm(name, (d // tn, s // tm),
               [(dhg, a_blk, a_map, wg, b_blk, b_map, "nn", 0, g), (dhu, a_blk, a_map, wu, b_blk, b_map, "nn", 0, g)],
               [], [((s, d), F32, (tm, tn), lambda j, i: (i, j))], _store, comm=comm)[0]


def _ffn_forward(tag, h, gain, get_wgu, get_wd, norm_comm=(), up_comm=(), down_comm=()):
    xn = _rms_fwd(tag + "_norm", h, gain, comm=norm_comm)
    wg, wu = get_wgu()
    hg, hu, a = _ffn_up(tag + "_up", xn, wg, wu, comm=up_comm)
    return _ffn_down(tag + "_down", a, get_wd(), h, comm=down_comm), (xn, hg, hu, a)


def _na_geometry(rows):
    kh = min(NA_ROWS_WIN, rows)
    cols = np.arange(GRID_W)
    col_start = np.clip(cols - NA_COLS_WIN // 2, 0, GRID_W - NA_COLS_WIN)
    mask = (cols[None, :] >= col_start[:, None]) & (cols[None, :] < col_start[:, None] + NA_COLS_WIN)
    dc = np.clip(cols[None, :] - cols[:, None], -(NA_COLS_WIN - 1), NA_COLS_WIN - 1) + (NA_COLS_WIN - 1)
    return kh, mask, dc


def _na_table(rpb, rows):
    _, mask, dc = _na_geometry(rows)
    return jnp.where(jnp.asarray(mask)[None, None], rpb[:, :, dc], NEG)


class _NaPlan:
    def __init__(self, s):
        self.s, self.rows = s, s // GRID_W
        self.kh = min(NA_ROWS_WIN, self.rows)
        self.qr = min(NA_QROWS, self.rows)
        self.kr = min(self.rows, self.kh + self.qr - 1)
        self.groups = self.rows // self.qr
        self.nd = 2 * NA_ROWS_WIN - 1
        self.hw, self.nq = NA_HG * NA_DIM, NA_HEADS // NA_HG
        clip = lambda v, hi: min(max(v, 0), hi)
        pats = [(clip(g * self.qr - self.kh // 2, self.rows - self.kr) - g * self.qr,)
                + tuple(clip(g * self.qr + a - self.kh // 2, self.rows - self.kh) - g * self.qr for a in range(self.qr))
                for g in range(self.groups)]
        self.rebuild = [g for g in range(self.groups) if g == 0 or pats[g] != pats[g - 1]]

    def first_key_row(self, g):
        return jnp.clip(g * self.qr - self.kh // 2, 0, self.rows - self.kr)

    def specs(self):
        blk = pl.BlockSpec((self.qr * GRID_W, self.hw), lambda j, g: (g, j))
        k_spec = pl.BlockSpec((self.s, self.hw), lambda j, g: (0, self.nq + j))
        v_spec = pl.BlockSpec((self.s, self.hw), lambda j, g: (0, 2 * self.nq + j))
        t_spec = pl.BlockSpec((NA_HG, self.nd, GRID_W, GRID_W), lambda j, g: (j, 0, 0, 0))
        return blk, k_spec, v_spec, t_spec

    def bias_scratch(self):
        return pltpu.VMEM((NA_HG, self.qr * GRID_W, self.kr * GRID_W), F32)

    def fill_bias(self, t_ref, bias_ref, g):
        def build():
            r0, ks = g * self.qr, self.first_key_row(g)
            for a in range(self.qr):
                rs = jnp.clip(r0 + a - self.kh // 2, 0, self.rows - self.kh)
                for i in range(self.kr):
                    valid = jnp.logical_and(ks + i >= rs, ks + i < rs + self.kh)
                    idx = jnp.clip(ks + i - r0 - a + NA_ROWS_WIN - 1, 0, self.nd - 1)
                    for h in range(NA_HG):
                        bias_ref[h, a * GRID_W:(a + 1) * GRID_W, i * GRID_W:(i + 1) * GRID_W] = jnp.where(
                            valid, t_ref[h, idx], NEG)

        pl.when(functools.reduce(jnp.logical_or, [g == r for r in self.rebuild]))(build)

    def window(self, g):
        return pl.ds(pl.multiple_of(self.first_key_row(g) * GRID_W, GRID_W), self.kr * GRID_W)


def _na_probs(q, k, bias):
    sc = lax.dot_general(q, k, _DN["nt"], preferred_element_type=F32) * (NA_DIM ** -0.5) + bias
    e = jnp.exp(sc - jnp.max(sc, axis=-1, keepdims=True))
    return e / jnp.sum(e, axis=-1, keepdims=True)


def _na_fwd(qkv, table, comm=()):
    plan = _NaPlan(qkv.shape[0])
    blk, k_spec, v_spec, t_spec = plan.specs()

    def body(q_ref, k_ref, v_ref, t_ref, o_ref, bias_ref):
        g = pl.program_id(1)
        plan.fill_bias(t_ref, bias_ref, g)
        win = plan.window(g)
        for h in range(NA_HG):
            cs = slice(h * NA_DIM, (h + 1) * NA_DIM)
            p = _na_probs(q_ref[:, cs], k_ref[win, cs], bias_ref[h])
            o_ref[:, cs] = jnp.dot(p.astype(BF), v_ref[win, cs], preferred_element_type=F32).astype(BF)

    return _call("na_fwd", body, (plan.nq, plan.groups), [blk, k_spec, v_spec, t_spec], blk,
                 jax.ShapeDtypeStruct((plan.s, NA_HEADS * NA_DIM), BF), [qkv, qkv, qkv, table], comm,
                 scratch=[plan.bias_scratch()])


def _na_bwd(qkv, table, do, comm=()):
    plan = _NaPlan(qkv.shape[0])
    blk, k_spec, v_spec, t_spec = plan.specs()
    qr, kr = plan.qr, plan.kr

    def body(q_ref, k_ref, v_ref, t_ref, do_ref, dq_ref, dk_ref, dv_ref, dt_ref, bias_ref):
        g = pl.program_id(1)

        @pl.when(g == 0)
        def _():
            dk_ref[...] = jnp.zeros_like(dk_ref)
            dv_ref[...] = jnp.zeros_like(dv_ref)
            dt_ref[...] = jnp.zeros_like(dt_ref)

        plan.fill_bias(t_ref, bias_ref, g)
        win = plan.window(g)
        base = plan.first_key_row(g) - g * qr + NA_ROWS_WIN - 1
        for h in range(NA_HG):
            cs = slice(h * NA_DIM, (h + 1) * NA_DIM)
            q, k, v, do_h = q_ref[:, cs], k_ref[win, cs], v_ref[win, cs], do_ref[:, cs]
            p = _na_probs(q, k, bias_ref[h])
            dp = lax.dot_general(do_h, v, _DN["nt"], preferred_element_type=F32)
            ds = p * (dp - jnp.sum(p * dp, axis=-1, keepdims=True))
            for dlt in range(1 - qr, kr):
                tiles = [ds[a * GRID_W:(a + 1) * GRID_W, (a + dlt) * GRID_W:(a + dlt + 1) * GRID_W]
                         for a in range(qr) if 0 <= a + dlt < kr]
                dt_ref[h, jnp.clip(base + dlt, 0, plan.nd - 1)] += functools.reduce(jnp.add, tiles)
            dsb = (ds * (NA_DIM ** -0.5)).astype(BF)
            dq_ref[:, cs] = jnp.dot(dsb, k, preferred_element_type=F32).astype(BF)
            dk_ref[win, cs] += lax.dot_general(dsb, q, _DN["tn"], preferred_element_type=F32)
            dv_ref[win, cs] += lax.dot_general(p.astype(BF), do_h, _DN["tn"], preferred_element_type=F32)

    width = NA_HEADS * NA_DIM
    whole = pl.BlockSpec((plan.s, plan.hw), lambda j, g: (0, j))
    return _call(
        "na_bwd", body, (plan.nq, plan.groups), [blk, k_spec, v_spec, t_spec, blk], [blk, whole, whole, t_spec],
        [jax.ShapeDtypeStruct((plan.s, width), BF), jax.ShapeDtypeStruct((plan.s, width), F32),
         jax.ShapeDtypeStruct((plan.s, width), F32),
         jax.ShapeDtypeStruct((NA_HEADS, plan.nd, GRID_W, GRID_W), F32)],
        [qkv, qkv, qkv, table, do], comm, scratch=[plan.bias_scratch()])


def _na_rpb_grad(dt, rows):
    _, mask, dc = _na_geometry(rows)
    nd, nc = 2 * NA_ROWS_WIN - 1, 2 * NA_COLS_WIN - 1
    onehot = np.zeros((GRID_W * GRID_W, 128), np.float32)
    onehot[np.arange(GRID_W * GRID_W), dc.reshape(-1)] = mask.reshape(-1).astype(np.float32)
    flat = dt.reshape(NA_HEADS * nd, GRID_W * GRID_W)

    def body(a_ref, e_ref, o_ref):
        o_ref[...] = jnp.dot(a_ref[...], e_ref[...], precision=HI, preferred_element_type=F32)

    out = pl.pallas_call(body, name="na_rpb_grad", out_shape=jax.ShapeDtypeStruct((NA_HEADS * nd, 128), F32),
                         compiler_params=_params(0))(flat, jnp.asarray(onehot))
    return out[:, :nc].reshape(NA_HEADS, nd, nc)


def _rope_consts(s):
    pos = np.arange(s, dtype=np.float32)
    inv = (1.0 / (ROPE_THETA ** (np.arange(0, ML_ROPE, 2, dtype=np.float32) / ML_ROPE))).astype(np.float32)
    ang = pos[:, None] * inv[None, :]
    cos, sin = np.cos(ang).astype(np.float32), np.sin(ang).astype(np.float32)
    half = ML_ROPE // 2
    rot = np.zeros((ML_ROPE, ML_ROPE), np.float32)
    rot[np.arange(half) + half, np.arange(half)] = -1.0
    rot[np.arange(half), np.arange(half) + half] = 1.0
    return (jnp.asarray(np.concatenate([cos, cos], 1)), jnp.asarray(np.concatenate([sin, sin], 1)),
            jnp.asarray(rot), jnp.asarray(rot.T.copy()))


def _rope(v, cos, sin, rot):
    return v * cos + jnp.dot(v, rot, precision=HI, preferred_element_type=F32) * sin


def _unrope(dv, cos, sin, rot_t):
    return dv * cos + jnp.dot(dv * sin, rot_t, precision=HI, preferred_element_type=F32)


def _rms(v, g):
    return v * lax.rsqrt(jnp.mean(v * v, axis=-1, keepdims=True) + EPS) * g


def _mla_prep(lat, gq, gkv, cos, sin, rot, tm=256):
    s, w = lat.shape
    tm = _tile(s, tm)

    def body(l_ref, gq_ref, gkv_ref, c_ref, s_ref, r_ref, cq_ref, ckv_ref, kr_ref):
        cq_ref[...] = _rms(l_ref[:, :ML_RANK], gq_ref[...]).astype(BF)
        ckv_ref[...] = _rms(l_ref[:, ML_RANK:2 * ML_RANK], gkv_ref[...]).astype(BF)
        kr_ref[...] = _rope(l_ref[:, 2 * ML_RANK:], c_ref[...], s_ref[...], r_ref[...]).astype(BF)

    row = lambda c: pl.BlockSpec((tm, c), lambda i: (i, 0))
    full = lambda a: pl.BlockSpec(a.shape, lambda i: (0, 0))
    return pl.pallas_call(
        body, name="mla_prep", grid=(s // tm,),
        in_specs=[row(w), full(gq), full(gkv), row(ML_ROPE), row(ML_ROPE), full(rot)],
        out_specs=[row(ML_RANK), row(ML_RANK), row(ML_ROPE)],
        out_shape=[jax.ShapeDtypeStruct((s, ML_RANK), BF), jax.ShapeDtypeStruct((s, ML_RANK), BF),
                   jax.ShapeDtypeStruct((s, ML_ROPE), BF)],
        compiler_params=_params(1))(lat, gq, gkv, cos, sin, rot)


def _mla_q_proj(cq, wuq, cos, sin, rot, tm=512, comm=()):
    s, k = cq.shape
    tm = _tile(s, tm)

    def epi(accs, ex, out):
        acc = accs[0]
        out[0][:, :ML_NOPE] = acc[:, :ML_NOPE].astype(BF)
        out[0][:, ML_NOPE:] = _rope(acc[:, ML_NOPE:], ex[0][...], ex[1][...], ex[2][...]).astype(BF)

    rmap = lambda j, i: (i, 0)
    return _mm("mla_q_proj", (ML_HEADS, s // tm),
               [(cq, (tm, k), rmap, wuq, (None, ML_QK, k), lambda j, i: (j, 0, 0), "nt", 0, 0)],
               [(cos, (tm, ML_ROPE), rmap), (sin, (tm, ML_ROPE), rmap), (rot, rot.shape, lambda j, i: (0, 0))],
               [((ML_HEADS, s, ML_QK), BF, (None, tm, ML_QK), lambda j, i: (j, i, 0))], epi, comm=comm)[0]


def _mla_kv_proj(ckv, wukv, kr, tm=512, comm=()):
    s, k = ckv.shape
    tm = _tile(s, tm)

    def epi(accs, ex, out):
        acc = accs[0]
        out[0][:, :ML_NOPE] = acc[:, :ML_NOPE].astype(BF)
        out[0][:, ML_NOPE:] = ex[0][...]
        out[1][...] = acc[:, ML_NOPE:].astype(BF)

    rmap = lambda j, i: (i, 0)
    gmap = lambda j, i: (j, i, 0)
    return _mm("mla_kv_proj", (ML_HEADS, s // tm),
               [(ckv, (tm, k), rmap, wukv, (None, k, ML_NOPE + ML_V), lambda j, i: (j, 0, 0), "nn", 0, 0)],
               [(kr, (tm, ML_ROPE), rmap)],
               [((ML_HEADS, s, ML_QK), BF, (None, tm, ML_QK), gmap), ((ML_HEADS, s, ML_V), BF, (None, tm, ML_V), gmap)],
               epi, comm=comm)


def _mla_probs(q, k):
    sc = lax.dot_general(q, k, _DN["nt"], preferred_element_type=F32) * (ML_QK ** -0.5)
    e = jnp.exp(sc - jnp.max(sc, axis=-1, keepdims=True))
    return e / jnp.sum(e, axis=-1, keepdims=True)


def _mla_fwd(q, k, v, tq=512, comm=()):
    _, s, _ = q.shape
    tq = _tile(s, tq)

    def body(q_ref, k_ref, v_ref, o_ref):
        p = _mla_probs(q_ref[...], k_ref[...])
        o_ref[...] = jnp.dot(p.astype(BF), v_ref[...], preferred_element_type=F32).astype(BF)

    return _call("mla_fwd", body, (ML_HEADS, s // tq),
                 [pl.BlockSpec((None, tq, ML_QK), lambda h, i: (h, i, 0)),
                  pl.BlockSpec((None, s, ML_QK), lambda h, i: (h, 0, 0)),
                  pl.BlockSpec((None, s, ML_V), lambda h, i: (h, 0, 0))],
                 pl.BlockSpec((tq, ML_V), lambda h, i: (i, h)),
                 jax.ShapeDtypeStruct((s, ML_HEADS * ML_V), BF), [q, k, v], comm)


def _mla_bwd(q, k, v, do, tq=256, comm=()):
    _, s, _ = q.shape
    tq = _tile(s, tq)

    def body(q_ref, k_ref, v_ref, do_ref, dq_ref, dk_ref, dv_ref):
        i = pl.program_id(1)
        qv, kv, vv, dov = q_ref[...], k_ref[...], v_ref[...], do_ref[...]
        p = _mla_probs(qv, kv)
        dp = lax.dot_general(dov, vv, _DN["nt"], preferred_element_type=F32)
        ds = (p * (dp - jnp.sum(p * dp, axis=-1, keepdims=True)) * (ML_QK ** -0.5)).astype(BF)
        dq_ref[...] = jnp.dot(ds, kv, preferred_element_type=F32)
        _acc_rows(dk_ref, lax.dot_general(ds, qv, _DN["tn"], preferred_element_type=F32), i)
        _acc_rows(dv_ref, lax.dot_general(p.astype(BF), dov, _DN["tn"], preferred_element_type=F32), i)

    return _call(
        "mla_bwd", body, (ML_HEADS, s // tq),
        [pl.BlockSpec((None, tq, ML_QK), lambda h, i: (h, i, 0)),
         pl.BlockSpec((None, s, ML_QK), lambda h, i: (h, 0, 0)),
         pl.BlockSpec((None, s, ML_V), lambda h, i: (h, 0, 0)),
         pl.BlockSpec((tq, ML_V), lambda h, i: (i, h))],
        [pl.BlockSpec((None, tq, ML_QK), lambda h, i: (h, i, 0)),
         pl.BlockSpec((None, s, ML_QK), lambda h, i: (h, 0, 0)),
         pl.BlockSpec((None, s, ML_V), lambda h, i: (h, 0, 0))],
        [jax.ShapeDtypeStruct((ML_HEADS, s, ML_QK), F32), jax.ShapeDtypeStruct((ML_HEADS, s, ML_QK), F32),
         jax.ShapeDtypeStruct((ML_HEADS, s, ML_V), F32)],
        [q, k, v, do], comm)


def _mla_post(dq, dk, dv, cos, sin, rot_t, tm=256):
    _, s, _ = dq.shape
    tm = _tile(s, tm)

    def body(dq_ref, dk_ref, dv_ref, c_ref, s_ref, r_ref, dqp_ref, dkv_ref, dkr_ref):
        h = pl.program_id(1)
        dqv, dkk = dq_ref[...], dk_ref[...]
        dqp_ref[:, :ML_NOPE] = dqv[:, :ML_NOPE].astype(BF)
        dqp_ref[:, ML_NOPE:] = _unrope(dqv[:, ML_NOPE:], c_ref[...], s_ref[...], r_ref[...]).astype(BF)
        dkv_ref[:, :ML_NOPE] = dkk[:, :ML_NOPE].astype(BF)
        dkv_ref[:, ML_NOPE:] = dv_ref[...].astype(BF)
        _acc_rows(dkr_ref, dkk[:, ML_NOPE:], h)

    gspec = lambda c: pl.BlockSpec((None, tm, c), lambda i, h: (h, i, 0))
    rspec = pl.BlockSpec((tm, ML_ROPE), lambda i, h: (i, 0))
    return pl.pallas_call(
        body, name="mla_post", grid=(s // tm, ML_HEADS),
        in_specs=[gspec(ML_QK), gspec(ML_QK), gspec(ML_V), rspec, rspec,
                  pl.BlockSpec(rot_t.shape, lambda i, h: (0, 0))],
        out_specs=[gspec(ML_QK), gspec(ML_NOPE + ML_V), rspec],
        out_shape=[jax.ShapeDtypeStruct((ML_HEADS, s, ML_QK), BF),
                   jax.ShapeDtypeStruct((ML_HEADS, s, ML_NOPE + ML_V), BF),
                   jax.ShapeDtypeStruct((s, ML_ROPE), F32)],
        compiler_params=_params(2))(dq, dk, dv, cos, sin, rot_t)


def _mla_lat_bwd(dcq, dckv, dkr, lat, gq, gkv, cos, sin, rot_t, tm=256):
    s, w = lat.shape
    tm = _tile(s, tm)

    def body(dcq_ref, dckv_ref, dkr_ref, l_ref, gq_ref, gkv_ref, c_ref, s_ref, r_ref, dl_ref, dgq_ref, dgkv_ref):
        i = pl.program_id(0)
        dql, pq = _rms_bwd_math(dcq_ref[...], l_ref[:, :ML_RANK], gq_ref[...])
        dkl, pkv = _rms_bwd_math(dckv_ref[...], l_ref[:, ML_RANK:2 * ML_RANK], gkv_ref[...])
        dl_ref[:, :ML_RANK] = dql.astype(BF)
        dl_ref[:, ML_RANK:2 * ML_RANK] = dkl.astype(BF)
        dl_ref[:, 2 * ML_RANK:] = _unrope(dkr_ref[...], c_ref[...], s_ref[...], r_ref[...]).astype(BF)
        _acc_rows(dgq_ref, pq, i)
        _acc_rows(dgkv_ref, pkv, i)

    row = lambda c: pl.BlockSpec((tm, c), lambda i: (i, 0))
    full = lambda a: pl.BlockSpec(a.shape, lambda i: (0, 0))
    return pl.pallas_call(
        body, name="mla_lat_bwd", grid=(s // tm,),
        in_specs=[row(ML_RANK), row(ML_RANK), row(ML_ROPE), row(w), full(gq), full(gkv), row(ML_ROPE), row(ML_ROPE),
                  full(rot_t)],
        out_specs=[row(w), full(gq), full(gkv)],
        out_shape=[jax.ShapeDtypeStruct((s, w), BF), jax.ShapeDtypeStruct(gq.shape, F32),
                   jax.ShapeDtypeStruct(gkv.shape, F32)],
        compiler_params=_params(1))(dcq, dckv, dkr, lat, gq, gkv, cos, sin, rot_t)


def _grp_dw(name, a, dout, ta=1024):
    s, k = a.shape
    ta = _tile(k, ta)
    if dout.ndim == 3:
        g, _, nb = dout.shape
        b_blk, b_map = (None, s, nb), lambda j, i: (j, 0, 0)
    else:
        g, nb = NDEV, dout.shape[1] // NDEV
        b_blk, b_map = (s, nb), lambda j, i: (0, j)
    return _mm(name, (g, k // ta),
               [(a, (s, ta), lambda j, i: (0, i), dout, b_blk, b_map, "tn", 0, 0)], [],
               [((g, k, nb), BF, (None, ta, nb), lambda j, i: (j, i, 0))], _store)[0]


def _grp_dw_t(name, dout, a, ta=512):
    g, s, nb = dout.shape
    k = a.shape[1]
    ta = _tile(k, ta)
    return _mm(name, (g, k // ta),
               [(dout, (None, s, nb), lambda j, i: (j, 0, 0), a, (s, ta), lambda j, i: (0, i), "tn", 0, 0)], [],
               [((g, nb, k), BF, (None, nb, ta), lambda j, i: (j, 0, i))], _store)[0]


def _grp_dx_t(name, dout, wt, tm=512, tn=512, comm=()):
    g, s, nb = dout.shape
    k = wt.shape[2]
    tm, tn = _tile(s, tm), _tile(k, tn)
    return _mm(name, (k // tn, s // tm),
               [(dout, (g, tm, nb), lambda j, i: (0, i, 0), wt, (g, nb, tn), lambda j, i: (0, 0, j), "nn", 0, g)], [],
               [((s, k), F32, (tm, tn), lambda j, i: (i, j))], _store, comm=comm)[0]


def _grp_dx(name, dout, w, tm=512, tn=512, comm=()):
    g, s, nb = dout.shape
    k = w.shape[1]
    tm, tn = _tile(s, tm), _tile(k, tn)
    return _mm(name, (k // tn, s // tm),
               [(dout, (g, tm, nb), lambda j, i: (0, i, 0), w, (g, tn, nb), lambda j, i: (0, j, 0), "nt", 0, g)], [],
               [((s, k), F32, (tm, tn), lambda j, i: (i, j))], _store, comm=comm)[0]


def _row_dw(name, a, dout, tn=2048):
    s, n = dout.shape
    tn = _tile(n, tn)
    if a.ndim == 3:
        kb = a.shape[2]
        a_blk, a_map = (None, s, kb), lambda j, i: (j, 0, 0)
    else:
        kb = a.shape[1] // NDEV
        a_blk, a_map = (s, kb), lambda j, i: (0, j)
    return _mm(name, (NDEV, n // tn),
               [(a, a_blk, a_map, dout, (s, tn), lambda j, i: (0, i), "tn", 0, 0)], [],
               [((NDEV, kb, n), BF, (None, kb, tn), lambda j, i: (j, 0, i))], _store)[0]


def _mix_merge(oa, ob, wa, wb, ga, gb, tm=1024, comm=()):
    s, k = oa.shape
    g, _, nb = wa.shape
    tm = _tile(s, tm)

    def epi(accs, ex, out):
        ya, yb = accs
        out[0][...] = ya.astype(BF)
        out[1][...] = yb.astype(BF)
        out[2][...] = (_sig(ex[0][...]) * ya + _sig(ex[1][...]) * yb).astype(BF)

    rmap = lambda j, i: (i, 0)
    wmap = lambda j, i: (j, 0, 0)
    o = ((g, s, nb), BF, (None, tm, nb), lambda j, i: (j, i, 0))
    cmap = lambda j, i: (i, j)
    return _mm("mix_merge", (g, s // tm),
               [(oa, (tm, k), rmap, wa, (None, k, nb), wmap, "nn", 0, 0),
                (ob, (tm, k), rmap, wb, (None, k, nb), wmap, "nn", 1, 0)],
               [(ga, (tm, nb), cmap), (gb, (tm, nb), cmap)], [o, o, o], epi, nacc=2, comm=comm)


def _mix_out(merged, wout, resid, tm=512, tn=512):
    g, s, kb = merged.shape
    d = wout.shape[2]
    tm, tn = _tile(s, tm), _tile(d, tn)

    def epi(accs, ex, out):
        out[0][...] = ex[0][...] + accs[0]

    return _mm("mix_out", (d // tn, s // tm),
               [(merged, (g, tm, kb), lambda j, i: (0, i, 0), wout, (g, kb, tn), lambda j, i: (0, 0, j), "nn", 0, g)],
               [(resid, (tm, tn), lambda j, i: (i, j))],
               [((s, d), F32, (tm, tn), lambda j, i: (i, j))], epi)[0]


def _mix_out_bwd(dh, wout, ga, gb, ya, yb, tm=1024, comm=()):
    s, d = dh.shape
    g, kb, _ = wout.shape
    tm = _tile(s, tm)

    def epi(accs, ex, out):
        dm = accs[0]
        sa, sb = _sig(ex[0][...]), _sig(ex[1][...])
        out[0][...] = (dm * sa).astype(BF)
        out[1][...] = (dm * sb).astype(BF)
        out[2][...] = (dm * ex[2][...].astype(F32) * sa * (1.0 - sa)).astype(BF)
        out[3][...] = (dm * ex[3][...].astype(F32) * sb * (1.0 - sb)).astype(BF)

    cmap = lambda j, i: (i, j)
    gmap = lambda j, i: (j, i, 0)
    og = ((g, s, kb), BF, (None, tm, kb), gmap)
    oc = ((s, g * kb), BF, (tm, kb), cmap)
    return _mm("mix_out_bwd", (g, s // tm),
               [(dh, (tm, d), lambda j, i: (i, 0), wout, (None, kb, d), lambda j, i: (j, 0, 0), "nt", 0, 0)],
               [(ga, (tm, kb), cmap), (gb, (tm, kb), cmap), (ya, (None, tm, kb), gmap), (yb, (None, tm, kb), gmap)],
               [og, og, oc, oc], epi, comm=comm)


def _pl_forward(n4, wplg, p, wpl, h3, tm=1024):
    s, d = n4.shape
    g, kb, _ = wplg.shape
    kp, nb = wpl.shape[1], wpl.shape[2]
    tm = _tile(s, tm)
    wplg_nat = wplg.reshape(g * kb, d)

    def epi(accs, ex, out):
        t, pe = accs
        out[0][...] = ex[0][...] + _sig(t) * pe
        out[1][...] = t
        out[2][...] = pe.astype(BF)

    rmap = lambda j, i: (i, 0)
    cmap = lambda j, i: (i, j)
    return _mm("pl_forward", (g, s // tm),
               [(n4, (tm, d), rmap, wplg_nat, (g * kb, nb), lambda j, i: (0, j), "nn", 0, 0),
                (p, (tm, kp), rmap, wpl, (None, kp, nb), lambda j, i: (j, 0, 0), "nn", 1, 0)],
               [(h3, (tm, nb), cmap)],
               [((s, d), F32, (tm, nb), cmap), ((s, d), F32, (tm, nb), cmap), ((s, d), BF, (tm, nb), cmap)],
               epi, nacc=2)


def _row_dx(name, dout, w, tm=1024, comm=()):
    s, n = dout.shape
    g, kb, _ = w.shape
    tm = _tile(s, tm)
    return _mm(name, (g, s // tm),
               [(dout, (tm, n), lambda j, i: (i, 0), w, (None, kb, n), lambda j, i: (j, 0, 0), "nt", 0, 0)], [],
               [((s, g * kb), F32, (tm, kb), lambda j, i: (i, j))], _store, comm=comm)[0]


def _in_proj_bwd_x(pieces, weights, tm=512, tn=512, comm=()):
    s = pieces[0].shape[0]
    d = weights[0].shape[1]
    tm, tn = _tile(s, tm), _tile(d, tn)
    prods = [(pc, (tm, pc.shape[1]), lambda j, i: (i, 0), w, (w.shape[0], tn), lambda j, i: (0, j), "nn", 0, 0)
             for pc, w in zip(pieces, weights)]
    return _mm("in_proj_dx", (d // tn, s // tm), prods, [],
               [((s, d), F32, (tm, tn), lambda j, i: (i, j))], _store, comm=comm)[0]


def _split_w_in(w_in_t):
    g, nb, d = w_in_t.shape
    nat = w_in_t.reshape(g * nb, d)
    na, lat = 3 * NA_HEADS * NA_DIM, 2 * ML_RANK + ML_ROPE
    return nat[:na], nat[na:na + lat], nat[na + lat:na + lat + d], nat[na + lat + d:]


def _pair_sum(name, part, landed, core):
    _, _, r, c = part.shape
    tr, tc = _ew_tile(r, c)

    def body(core_ref, a_ref, b_ref, o_ref):
        o_ref[...] = (a_ref[...].astype(F32) + b_ref[...].astype(F32)).astype(o_ref.dtype)

    return pl.pallas_call(
        body, name=name,
        grid_spec=pltpu.PrefetchScalarGridSpec(
            num_scalar_prefetch=1, grid=(NCHIP, r // tr, c // tc),
            in_specs=[pl.BlockSpec((None, None, tr, tc), lambda j, i, k, core_ref: (j, core_ref[0], i, k)),
                      pl.BlockSpec((None, tr, tc), lambda j, i, k, core_ref: (j, i, k))],
            out_specs=pl.BlockSpec((None, tr, tc), lambda j, i, k, core_ref: (j, i, k))),
        out_shape=jax.ShapeDtypeStruct(landed.shape, landed.dtype), compiler_params=_params(3),
    )(core, part, landed)


def _device_step(x, p, target, sp, own, core):
    s, d = x.shape
    rows = s // GRID_W
    cos, sin, rot, rot_t = _rope_consts(s)
    w, dw4, sums, dsp, pending = {}, {}, {}, {}, []

    def gather(*names):
        return _GatherPart(names, [own[n] for n in names])

    def got(part):
        w.update(zip(part.names, part.results))

    def grad(name, g):
        dw4[name] = g.reshape((NCHIP, 2) + g.shape[1:])

    def to_sibling(*names):
        return _SiblingPart(names, [dw4[n] for n in names])

    def add_pairs(part):
        for n, landed in zip(part.names, part.results):
            sums[n] = _pair_sum("pair_sum_" + n, dw4[n], landed, core)

    def start_chips(tag, *names):
        send, recv, thru, lands, token = _chips_start("rs_start_" + tag, [sums[n] for n in names])
        pending.append((tag, names, send, recv, thru, lands))
        return token

    c0 = gather("ffn1_w_gate", "ffn1_w_up")
    c1 = gather("ffn1_w_down")
    c2 = gather("w_in")

    def ffn1_wgu():
        got(c0)
        return w["ffn1_w_gate"], w["ffn1_w_up"]

    def ffn1_wd():
        got(c1)
        return w["ffn1_w_down"]

    h1, ffn1_saved = _ffn_forward("ffn1", x, sp["ffn1_norm"], ffn1_wgu, ffn1_wd,
                                  norm_comm=[c0], up_comm=[c1], down_comm=[c2])
    got(c2)
    wqkv, wlat, wga, wgb = _split_w_in(w["w_in"])
    u = _rms_fwd("mix_norm", h1, sp["mix_norm"])
    c3 = gather("w_uq", "w_ukv")
    qkv = _mm_nt("in_qkv", u, wqkv, BF, tn=1024, comm=[c3])
    got(c3)
    lat = _mm_nt("in_lat", u, wlat, F32)
    c3a = gather("w_branch_a")
    ga = _mm_nt("in_ga", u, wga, F32, tn=1024, comm=[c3a])
    got(c3a)
    c3b = gather("w_branch_b")
    gb = _mm_nt("in_gb", u, wgb, F32, tn=1024, comm=[c3b])
    got(c3b)
    tb = _na_table(sp["na_rpb"], rows)
    c4 = gather("ffn2_w_gate")
    oa = _na_fwd(qkv, tb, comm=[c4])
    got(c4)
    cq, ckv, kr = _mla_prep(lat, sp["q_a_norm"], sp["kv_a_norm"], cos, sin, rot)
    c4a = gather("w_out")
    qf = _mla_q_proj(cq, w["w_uq"], cos, sin, rot, comm=[c4a])
    got(c4a)
    c4b = gather("w_pl_gate")
    kf, vf = _mla_kv_proj(ckv, w["w_ukv"], kr, comm=[c4b])
    got(c4b)
    c5 = gather("ffn2_w_up")
    ob = _mla_fwd(qf, kf, vf, comm=[c5])
    got(c5)
    c5a = gather("w_pl")
    ya, yb, merged = _mix_merge(oa, ob, w["w_branch_a"], w["w_branch_b"], ga, gb, comm=[c5a])
    got(c5a)
    h2 = _mix_out(merged, w["w_out"], h1)
    c6 = gather("ffn2_w_down")

    def ffn2_wd():
        got(c6)
        return w["ffn2_w_down"]

    h3, ffn2_saved = _ffn_forward("ffn2", h2, sp["ffn2_norm"], lambda: (w["ffn2_w_gate"], w["ffn2_w_up"]), ffn2_wd,
                                  up_comm=[c6])
    n4 = _rms_fwd("pl_norm", h3, sp["pl_norm"])
    pb = p.astype(BF)
    h4, t, pe = _pl_forward(n4, w["w_pl_gate"], pb, w["w_pl"], h3)

    dh4, dsp["final_norm"], loss = _loss_head(h4, target, sp["final_norm"])
    dt, dpe = _pl_bwd_elem(dh4, pe, t)
    grad("w_pl", _grp_dw("pl_dw", pb, dpe))
    grad("w_pl_gate", _row_dw("plg_dw", n4, dt))
    s1 = to_sibling("w_pl", "w_pl_gate")
    dn4 = _row_dx("plg_dx", dt, w["w_pl_gate"], comm=[s1])
    add_pairs(s1)
    tok = start_chips("pl", "w_pl", "w_pl_gate")
    dh3, dsp["pl_norm"] = _rms_bwd("pl_dnorm", dn4, h3, sp["pl_norm"], dh4, comm=[_After(tok)])

    xn, hg, hu, a = ffn2_saved
    dhb = dh3.astype(BF)
    grad("ffn2_w_down", _ffn_bwd_wd("ffn2_dwd", a, dhb))
    s2 = to_sibling("ffn2_w_down")
    dhg, dhu = _ffn_bwd_act("ffn2_dact", dhb, w["ffn2_w_down"], hg, hu, comm=[s2])
    add_pairs(s2)
    tok = start_chips("ffn2_down", "ffn2_w_down")
    dwg, dwu = _ffn_bwd_wup("ffn2_dwup", xn, dhg, dhu, comm=[_After(tok)])
    grad("ffn2_w_gate", dwg)
    grad("ffn2_w_up", dwu)
    s3 = to_sibling("ffn2_w_gate", "ffn2_w_up")
    dxn = _ffn_bwd_x("ffn2_dx", dhg, dhu, w["ffn2_w_gate"], w["ffn2_w_up"], comm=[s3])
    add_pairs(s3)
    tok = start_chips("ffn2_up", "ffn2_w_gate", "ffn2_w_up")
    dh2, dsp["ffn2_norm"] = _rms_bwd("ffn2_dnorm", dxn, h2, sp["ffn2_norm"], dh3, comm=[_After(tok)])

    dh2b = dh2.astype(BF)
    grad("w_out", _row_dw("out_dw", merged, dh2b))
    s4 = to_sibling("w_out")
    dya, dyb, dga, dgb = _mix_out_bwd(dh2b, w["w_out"], ga, gb, ya, yb, comm=[s4])
    add_pairs(s4)
    grad("w_branch_a", _grp_dw("bra_dw", oa, dya))
    grad("w_branch_b", _grp_dw("brb_dw", ob, dyb))
    doa = _grp_dx("bra_dx", dya, w["w_branch_a"]).astype(BF)
    s5 = to_sibling("w_branch_a", "w_branch_b")
    dob = _grp_dx("brb_dx", dyb, w["w_branch_b"], comm=[s5]).astype(BF)
    add_pairs(s5)
    tok = start_chips("mix", "w_out", "w_branch_a", "w_branch_b")

    dqf, dkf, dvf = _mla_bwd(qf, kf, vf, dob, comm=[_After(tok)])
    dqp, dkv, dkr = _mla_post(dqf, dkf, dvf, cos, sin, rot_t)
    grad("w_uq", _grp_dw_t("uq_dw", dqp, cq))
    grad("w_ukv", _grp_dw("ukv_dw", ckv, dkv))
    dcq = _grp_dx_t("uq_dx", dqp, w["w_uq"])
    s6 = to_sibling("w_uq", "w_ukv")
    dckv = _grp_dx("ukv_dx", dkv, w["w_ukv"], comm=[s6])
    add_pairs(s6)
    tok = start_chips("mla", "w_uq", "w_ukv")
    dlat, dsp["q_a_norm"], dsp["kv_a_norm"] = _mla_lat_bwd(dcq, dckv, dkr, lat, sp["q_a_norm"], sp["kv_a_norm"],
                                                         cos, sin, rot_t)
    dq_na, dk_na, dv_na, dtab = _na_bwd(qkv, tb, doa, comm=[_After(tok)])
    dsp["na_rpb"] = _na_rpb_grad(dtab, rows)
    dqkv = jnp.concatenate([dq_na, dk_na.astype(BF), dv_na.astype(BF)], axis=1)

    pieces = [dqkv, dlat, dga, dgb]
    dwin = jnp.zeros((sum(pc.shape[1] for pc in pieces), d), BF)
    row0 = 0
    for i, pc in enumerate(pieces):
        dwin = _mm_tn_into("in_dw%d" % i, pc, u, dwin, row0)
        row0 += pc.shape[1]
    grad("w_in", dwin.reshape(NDEV, -1, d))
    s7 = to_sibling("w_in")
    du = _in_proj_bwd_x(pieces, [wqkv, wlat, wga, wgb], comm=[s7])
    add_pairs(s7)
    tok = start_chips("w_in", "w_in")
    dh1, dsp["mix_norm"] = _rms_bwd("mix_dnorm", du, h1, sp["mix_norm"], dh2, comm=[_After(tok)])

    xn, hg, hu, a = ffn1_saved
    dhb = dh1.astype(BF)
    grad("ffn1_w_down", _ffn_bwd_wd("ffn1_dwd", a, dhb))
    s8 = to_sibling("ffn1_w_down")
    dhg, dhu = _ffn_bwd_act("ffn1_dact", dhb, w["ffn1_w_down"], hg, hu, comm=[s8])
    add_pairs(s8)
    tok = start_chips("ffn1_down", "ffn1_w_down")
    dwg, dwu = _ffn_bwd_wup("ffn1_dwup", xn, dhg, dhu, comm=[_After(tok)])
    grad("ffn1_w_gate", dwg)
    grad("ffn1_w_up", dwu)
    s9 = to_sibling("ffn1_w_gate", "ffn1_w_up")
    _comm_only("rs_sibling_ffn1", [s9])
    add_pairs(s9)
    tok = start_chips("ffn1_up", "ffn1_w_gate", "ffn1_w_up")
    dxn = _ffn_bwd_x("ffn1_dx", dhg, dhu, w["ffn1_w_gate"], w["ffn1_w_up"], comm=[_After(tok)])
    dx, dsp["ffn1_norm"] = _rms_bwd("ffn1_dnorm", dxn, x, sp["ffn1_norm"], dh1)
    return loss, dx, pending, dsp


def _gather_small(buf):
    def body(in_ref, out_ref, send_sems, recv_sems, local_sem):
        x, y, c = _coords()
        mine = pltpu.make_async_copy(in_ref, out_ref.at[4 * x + 2 * y + c], local_sem)
        mine.start()
        cps = []
        for k in range(1, NDEV):
            fx, fy, fc = (k >> 2) & 1, (k >> 1) & 1, k & 1
            peer = (x ^ fx, y ^ fy, c ^ fc)
            cps.append(pltpu.make_async_remote_copy(
                src_ref=in_ref, dst_ref=out_ref.at[4 * x + 2 * y + c], send_sem=send_sems.at[k - 1],
                recv_sem=recv_sems.at[k - 1], device_id=peer, device_id_type=MESH))
        for cp in cps:
            cp.start()
        for k in range(1, NDEV):
            fx, fy, fc = (k >> 2) & 1, (k >> 1) & 1, k & 1
            px, py, pc = x ^ fx, y ^ fy, c ^ fc
            pltpu.make_async_remote_copy(
                src_ref=in_ref, dst_ref=out_ref.at[4 * px + 2 * py + pc], send_sem=send_sems.at[k - 1],
                recv_sem=recv_sems.at[k - 1], device_id=(px, py, pc), device_id_type=MESH).wait_recv()
        for cp in cps:
            cp.wait_send()
        mine.wait()

    return pl.pallas_call(
        body, name="gather_small", in_specs=[ANY], out_specs=ANY,
        out_shape=jax.ShapeDtypeStruct((NDEV,) + buf.shape, buf.dtype),
        scratch_shapes=[pltpu.SemaphoreType.DMA((NDEV - 1,)), pltpu.SemaphoreType.DMA((NDEV - 1,)),
                        pltpu.SemaphoreType.DMA],
    )(buf)


def _adam_math(wv, g, m, v):
    m_new = B1 * m + (1.0 - B1) * g
    v_new = B2 * v + (1.0 - B2) * (g * g)
    m_hat = m_new / (1.0 - B1 ** STEP)
    v_hat = v_new / (1.0 - B2 ** STEP)
    return -LR * (m_hat / (jnp.sqrt(v_hat) + ADAM_EPS) + WD * wv), m_new, v_new


def _adam(name, parts, wv, m, v, after=None):
    npart, r, c = parts.shape
    tr, tc = _ew_tile(r, c)

    def body(p_ref, w_ref, m_ref, v_ref, *rest):
        g_ref, d_ref, mo_ref, vo_ref = rest[-4:]
        g = p_ref[0].astype(F32)
        for j in range(1, npart):
            g = g + p_ref[j].astype(F32)
        g_ref[...] = g
        d_ref[...], mo_ref[...], vo_ref[...] = _adam_math(w_ref[...], g, m_ref[...], v_ref[...])

    blk = pl.BlockSpec((tr, tc), lambda i, k: (i, k))
    extra = [] if after is None else [after]
    return pl.pallas_call(
        body, name=name, grid=(r // tr, c // tc),
        in_specs=[pl.BlockSpec((npart, tr, tc), lambda i, k: (0, i, k)), blk, blk, blk] + [ANY] * len(extra),
        out_specs=[blk] * 4, out_shape=[jax.ShapeDtypeStruct((r, c), F32)] * 4, compiler_params=_params(2),
    )(parts, wv, m, v, *extra)


def _adam_exchanged(name, sums, land, wv, m, v, my_chip):
    _, r, c = sums.shape
    tr, tc = _ew_tile(r, c)

    def body(chip_ref, s_ref, l_ref, w_ref, m_ref, v_ref, g_ref, d_ref, mo_ref, vo_ref):
        g = s_ref[...].astype(F32)
        for j in range(3):
            g = g + l_ref[j].astype(F32)
        g_ref[...] = g
        d_ref[...], mo_ref[...], vo_ref[...] = _adam_math(w_ref[...], g, m_ref[...], v_ref[...])

    blk = pl.BlockSpec((tr, tc), lambda i, k, chip_ref: (i, k))
    return pl.pallas_call(
        body, name=name,
        grid_spec=pltpu.PrefetchScalarGridSpec(
            num_scalar_prefetch=1, grid=(r // tr, c // tc),
            in_specs=[pl.BlockSpec((None, tr, tc), lambda i, k, chip_ref: (chip_ref[0], i, k)),
                      pl.BlockSpec((3, tr, tc), lambda i, k, chip_ref: (0, i, k)), blk, blk, blk],
            out_specs=[blk] * 4),
        out_shape=[jax.ShapeDtypeStruct((r, c), F32)] * 4, compiler_params=_params(2),
    )(my_chip, sums, land, wv, m, v)


SHARDED = ("ffn1_w_gate", "ffn1_w_up", "ffn1_w_down", "w_in", "w_uq", "w_ukv", "w_branch_a", "w_branch_b", "w_out",
           "ffn2_w_gate", "ffn2_w_up", "ffn2_w_down", "w_pl", "w_pl_gate")
TRANSPOSED = ("ffn1_w_gate", "ffn1_w_up", "ffn2_w_gate", "ffn2_w_up", "w_in", "w_uq")
REPLICATED = ("ffn1_norm", "mix_norm", "q_a_norm", "kv_a_norm", "na_rpb", "ffn2_norm", "pl_norm", "final_norm")
WEIGHTS = ("ffn1_norm", "ffn1_w_gate", "ffn1_w_up", "ffn1_w_down", "mix_norm", "w_in", "q_a_norm", "w_uq",
           "kv_a_norm", "w_ukv", "na_rpb", "w_branch_a", "w_branch_b", "w_out", "ffn2_norm", "ffn2_w_gate",
           "ffn2_w_up", "ffn2_w_down", "pl_norm", "w_pl", "w_pl_gate", "final_norm")
SMALL_W = 2048


def _pack_small(vals):
    rows = []
    for name in REPLICATED:
        flat = vals[name].reshape(-1).astype(F32)
        n = -(-flat.shape[0] // SMALL_W) * SMALL_W
        rows.append(jnp.pad(flat, (0, n - flat.shape[0])).reshape(-1, SMALL_W))
    return jnp.concatenate(rows, axis=0)


def _unpack_small(buf, shapes):
    out, r = {}, 0
    for name in REPLICATED:
        size = int(np.prod(shapes[name]))
        nrow = -(-size // SMALL_W)
        out[name] = buf[r:r + nrow].reshape(-1)[:size].reshape(shapes[name])
        r += nrow
    return out


def kernel(x, p, ffn1_norm, ffn1_w_gate, ffn1_w_up, ffn1_w_down, mix_norm, w_in, q_a_norm, w_uq, kv_a_norm, w_ukv, na_rpb, w_branch_a, w_branch_b, w_out, ffn2_norm, ffn2_w_gate, ffn2_w_up, ffn2_w_down, pl_norm, w_pl, w_pl_gate, final_norm, loss_target, m_ffn1_norm, m_ffn1_w_gate, m_ffn1_w_up, m_ffn1_w_down, m_mix_norm, m_w_in, m_q_a_norm, m_w_uq, m_kv_a_norm, m_w_ukv, m_na_rpb, m_w_branch_a, m_w_branch_b, m_w_out, m_ffn2_norm, m_ffn2_w_gate, m_ffn2_w_up, m_ffn2_w_down, m_pl_norm, m_w_pl, m_w_pl_gate, m_final_norm, v_ffn1_norm, v_ffn1_w_gate, v_ffn1_w_up, v_ffn1_w_down, v_mix_norm, v_w_in, v_q_a_norm, v_w_uq, v_kv_a_norm, v_w_ukv, v_na_rpb, v_w_branch_a, v_w_branch_b, v_w_out, v_ffn2_norm, v_ffn2_w_gate, v_ffn2_w_up, v_ffn2_w_down, v_pl_norm, v_w_pl, v_w_pl_gate, v_final_norm):
    args = dict(locals())
    wts = {n: args[n] for n in WEIGHTS}
    mom = {n: args["m_" + n] for n in WEIGHTS}
    var = {n: args["v_" + n] for n in WEIGHTS}
    shapes = {n: wts[n].shape for n in WEIGHTS}
    core = lax.axis_index("c").astype(jnp.int32).reshape(1)

    local = lambda n, a: a[0].T if n in TRANSPOSED else a[0]
    own = {n: local(n, wts[n]).astype(BF) for n in SHARDED}
    sp = {n: wts[n].reshape(1, -1) for n in REPLICATED if n != "na_rpb"}
    sp["na_rpb"] = wts["na_rpb"][0]
    loss_part, grad_x, pending, dsp = _device_step(x[0], p[0, 0], loss_target[0], sp, own, core)

    out = {}
    last = grad_x
    my_chip = (2 * lax.axis_index("x") + lax.axis_index("y")).astype(jnp.int32).reshape(1)
    for tag, names, send, recv, thru, lands in pending:
        thru, lands = _chips_wait("rs_wait_" + tag, send, recv, thru, lands, last)
        for n, s4, l3 in zip(names, thru, lands):
            res4 = _adam_exchanged("adam_" + n, s4, l3, local(n, wts[n]), local(n, mom[n]), local(n, var[n]), my_chip)
            out[n] = tuple((a.T if n in TRANSPOSED else a)[None] for a in res4)
            last = res4[1]

    small = jnp.concatenate([_pack_small(dsp), jnp.pad(loss_part, ((0, 0), (0, SMALL_W - loss_part.shape[1])))], 0)
    pad_rows = -small.shape[0] % 8
    small = jnp.pad(small, ((0, pad_rows), (0, 0)))
    every = _gather_small(small)
    zeros = jnp.zeros((1 + pad_rows, SMALL_W), F32)
    pack = lambda d: jnp.concatenate([_pack_small(d), zeros], 0)
    g_s, d_s, m_s, v_s = _adam("adam_small", every, pack(wts), pack(mom), pack(var))
    n_rows = small.shape[0] - 1 - pad_rows
    loss = g_s[n_rows, 0]
    small_out = [_unpack_small(b, shapes) for b in (g_s, d_s, m_s, v_s)]
    for n in REPLICATED:
        out[n] = tuple(b[n] for b in small_out)

    res = [loss, grad_x[None]]
    for k in range(4):
        res += [out[n][k] for n in WEIGHTS]
    return tuple(res)
```

```python
import functools

import numpy as np
import jax
import jax.numpy as jnp
from jax import lax
from jax.experimental import pallas as pl
from jax.experimental.pallas import tpu as pltpu

F32 = jnp.float32
BF = jnp.bfloat16
MESH = pl.DeviceIdType.MESH

NDEV = 8
NCHIP = 4
VMEM_LIMIT = 56 * 1024 * 1024
EPS = 1e-6
NEG = -1e30
GRID_W = 64
NA_HEADS, NA_DIM = 8, 128
NA_ROWS_WIN, NA_COLS_WIN = 8, 16
NA_HG = 4
NA_QROWS = 4
ML_HEADS, ML_NOPE, ML_ROPE, ML_V = 8, 128, 64, 128
ML_QK = ML_NOPE + ML_ROPE
ML_RANK = 512
ROPE_THETA = 10000.0
LR, B1, B2, ADAM_EPS, WD, STEP = 0.001, 0.9, 0.999, 1e-08, 0.01, 10
HI = lax.Precision.HIGHEST

_DN = {"nn": (((1,), (0,)), ((), ())), "nt": (((1,), (1,)), ((), ())), "tn": (((0,), (0,)), ((), ()))}


def _params(n):
    return pltpu.CompilerParams(dimension_semantics=("arbitrary",) * n, vmem_limit_bytes=VMEM_LIMIT)


def _sig(v):
    return jax.nn.sigmoid(v)


ANY = pl.BlockSpec(memory_space=pl.ANY)


def _coords():
    return lax.axis_index("x"), lax.axis_index("y"), lax.axis_index("c")


class _Part:
    inputs, out_shapes, sem_shapes, results = (), (), (), None

    def mid(self, ins, outs, sems):
        pass

    def late(self, ins, outs, sems):
        pass


class _After(_Part):
    def __init__(self, token):
        self.inputs = [token]

    def start(self, ins, outs, sems):
        pass

    finish = start


class _GatherPart(_Part):
    def __init__(self, names, shards):
        n = len(shards)
        self.names, self.inputs = list(names), list(shards)
        self.out_shapes = [jax.ShapeDtypeStruct((NDEV,) + a.shape, a.dtype) for a in shards]
        self.sem_shapes = [pltpu.SemaphoreType.DMA((n, 7)), pltpu.SemaphoreType.DMA((n, 7)),
                           pltpu.SemaphoreType.DMA((n,))]

    def _plan(self, ins, outs, sems):
        send_sems, recv_sems, local_sems = sems
        x, y, c = _coords()
        me, sib, diag = (x, y, c), (x, y, 1 - c), (1 - x, 1 - y, c)
        n1, n2 = (x ^ (1 - c), y ^ c, c), (x ^ c, y ^ (1 - c), c)

        def copy(i, k, block, to, src=None):
            px, py, pc = block
            dst = outs[i].at[4 * px + 2 * py + pc]
            return pltpu.make_async_remote_copy(
                src_ref=dst if src is None else src, dst_ref=dst, send_sem=send_sems.at[i, k],
                recv_sem=recv_sems.at[i, k], device_id=to, device_id_type=MESH)

        mine = [pltpu.make_async_copy(ins[i], outs[i].at[4 * x + 2 * y + c], local_sems.at[i])
                for i in range(len(ins))]
        return copy, mine, me, sib, n1, n2, diag

    def _own_sends(self, ins, copy, me, sib, n1, n2):
        return [copy(i, k, me, to, src=ins[i]) for i in range(len(ins)) for k, to in enumerate((sib, n1, n2))]

    def start(self, ins, outs, sems):
        copy, mine, me, sib, n1, n2, _ = self._plan(ins, outs, sems)
        for cp in mine + self._own_sends(ins, copy, me, sib, n1, n2):
            cp.start()

    def mid(self, ins, outs, sems):
        copy, _, me, sib, n1, n2, _ = self._plan(ins, outs, sems)
        for i in range(len(ins)):
            copy(i, 1, n1, me).wait_recv()
            copy(i, 3, n1, n2).start()
            copy(i, 4, n1, sib).start()

    def late(self, ins, outs, sems):
        copy, _, me, sib, _, n2, diag = self._plan(ins, outs, sems)
        for i in range(len(ins)):
            copy(i, 2, n2, me).wait_recv()
            copy(i, 5, n2, sib).start()
        for i in range(len(ins)):
            copy(i, 3, diag, me).wait_recv()
            copy(i, 6, diag, sib).start()

    def finish(self, ins, outs, sems):
        copy, mine, me, sib, n1, n2, diag = self._plan(ins, outs, sems)
        other = lambda dev: (dev[0], dev[1], sib[2])
        n = len(ins)
        for i in range(n):
            copy(i, 0, sib, me).wait_recv()
            for k, block in ((4, other(n2)), (5, other(n1)), (6, other(diag))):
                copy(i, k, block, me).wait_recv()
        for cp in self._own_sends(ins, copy, me, sib, n1, n2):
            cp.wait_send()
        for i in range(n):
            for k, block in ((3, n1), (4, n1), (5, n2), (6, diag)):
                copy(i, k, block, sib).wait_send()
        for cp in mine:
            cp.wait()


class _SiblingPart(_Part):
    def __init__(self, names, parts):
        n = len(parts)
        self.names, self.inputs = list(names), list(parts)
        self.out_shapes = [jax.ShapeDtypeStruct((NCHIP,) + a.shape[2:], a.dtype) for a in parts]
        self.sem_shapes = [pltpu.SemaphoreType.DMA((n,)), pltpu.SemaphoreType.DMA((n,))]

    def _copies(self, ins, outs, sems):
        x, y, c = _coords()
        return [pltpu.make_async_remote_copy(
            src_ref=ins[i].at[:, 1 - c], dst_ref=outs[i], send_sem=sems[0].at[i], recv_sem=sems[1].at[i],
            device_id=(x, y, 1 - c), device_id_type=MESH) for i in range(len(ins))]

    def start(self, ins, outs, sems):
        for cp in self._copies(ins, outs, sems):
            cp.start()

    def finish(self, ins, outs, sems):
        cps = self._copies(ins, outs, sems)
        for cp in cps:
            cp.wait_recv()
        for cp in cps:
            cp.wait_send()


HBM = pl.BlockSpec(memory_space=pltpu.HBM)
SEM = pl.BlockSpec(memory_space=pltpu.SEMAPHORE)


def _chip_peers():
    x, y, c = _coords()
    return [(1 - x, y, c), (x, 1 - y, c), (1 - x, 1 - y, c)]


def _chips_start(name, sums):
    n = len(sums)

    def body(*refs):
        ins, lands, send_sems, recv_sems = refs[:n], refs[n:2 * n], refs[2 * n], refs[2 * n + 1]
        for i in range(n):
            for k, (px, py, pc) in enumerate(_chip_peers()):
                pltpu.make_async_remote_copy(
                    src_ref=ins[i].at[2 * px + py], dst_ref=lands[i].at[k], send_sem=send_sems.at[3 * i + k],
                    recv_sem=recv_sems.at[3 * i + k], device_id=(px, py, pc), device_id_type=MESH).start()
        refs[-1][...] = jnp.zeros_like(refs[-1])

    lands = [lax.empty((3,) + a.shape[1:], a.dtype) for a in sums]
    bufs = list(sums) + lands
    res = pl.pallas_call(
        body, name=name, in_specs=[HBM] * (2 * n),
        out_specs=(SEM, SEM, *[HBM] * (2 * n), pl.BlockSpec(memory_space=pltpu.VMEM)),
        out_shape=(pltpu.SemaphoreType.DMA((3 * n,)), pltpu.SemaphoreType.DMA((3 * n,)),
                   *[pltpu.HBM(a.shape, a.dtype) for a in bufs], jax.ShapeDtypeStruct((8, 128), F32)),
        input_output_aliases={i: 2 + i for i in range(2 * n)},
        compiler_params=pltpu.CompilerParams(has_side_effects=pltpu.SideEffectType.DATAFLOW_SIDE_EFFECTING),
    )(*[pltpu.with_memory_space_constraint(a, pltpu.HBM) for a in bufs])
    return res[0], res[1], list(res[2:2 + n]), list(res[2 + n:2 + 2 * n]), res[-1]


def _chips_wait(name, send_sems, recv_sems, sums, lands, after):
    n = len(sums)

    def body(*refs):
        ins, zones, send, recv = refs[:n], refs[n:2 * n], refs[2 * n], refs[2 * n + 1]
        for i in range(n):
            for k, peer in enumerate(_chip_peers()):
                cp = pltpu.make_async_remote_copy(
                    src_ref=ins[i].at[0], dst_ref=zones[i].at[k], send_sem=send.at[3 * i + k],
                    recv_sem=recv.at[3 * i + k],
                    device_id=peer, device_id_type=MESH)
                cp.wait_send()
                cp.wait_recv()

    bufs = list(sums) + list(lands)
    res = pl.pallas_call(
        body, name=name, in_specs=[HBM] * (2 * n) + [SEM, SEM, ANY], out_specs=[HBM] * (2 * n),
        out_shape=[pltpu.HBM(a.shape, a.dtype) for a in bufs], input_output_aliases={i: i for i in range(2 * n)},
        compiler_params=pltpu.CompilerParams(has_side_effects=pltpu.SideEffectType.DATAFLOW_SIDE_EFFECTING),
    )(*bufs, send_sems, recv_sems, after)
    return list(res[:n]), list(res[n:])


def _call(name, body, grid, in_specs, out_specs, out_shape, args, comm=(), scratch=()):
    comm = [p for p in comm if p is not None]
    single = not isinstance(out_shape, (list, tuple))
    o_specs = [out_specs] if single else list(out_specs)
    o_shape = [out_shape] if single else list(out_shape)
    n_in, n_out = len(in_specs), len(o_specs)
    c_in = [a for p in comm for a in p.inputs]
    c_out = [s for p in comm for s in p.out_shapes]
    c_sem = [s for p in comm for s in p.sem_shapes]

    def wrapped(*refs):
        ins, outs = refs[:n_in], refs[n_in + len(c_in):n_in + len(c_in) + n_out]
        pos = [n_in, n_in + len(c_in) + n_out, n_in + len(c_in) + n_out + len(c_out)]
        own = refs[pos[2]:pos[2] + len(scratch)]
        pos[2] += len(scratch)
        split = []
        for p in comm:
            sizes = [len(p.inputs), len(p.out_shapes), len(p.sem_shapes)]
            split.append([refs[o:o + n] for o, n in zip(pos, sizes)])
            pos = [o + n for o, n in zip(pos, sizes)]
        step, steps = 0, 1
        for a, g in enumerate(grid):
            step, steps = step * g + pl.program_id(a), steps * g

        def run(which, at):
            def go():
                for p, cut in zip(comm, split):
                    getattr(p, which)(*cut)
            if not comm:
                return
            if grid:
                pl.when(step == at)(go)
            else:
                go()

        run("start", 0)
        body(*ins, *outs, *own)
        run("mid", steps // 2)
        run("late", max(steps // 2, steps - 1 - max(1, steps // 8)))
        run("finish", steps - 1)

    res = pl.pallas_call(
        wrapped, name=name, grid=grid, in_specs=list(in_specs) + [ANY] * len(c_in),
        out_specs=o_specs + [ANY] * len(c_out), out_shape=o_shape + c_out, scratch_shapes=list(scratch) + c_sem,
        compiler_params=_params(len(grid)),
    )(*args, *c_in)
    pos = n_out
    for p in comm:
        p.results = list(res[pos:pos + len(p.out_shapes)])
        pos += len(p.out_shapes)
    return res[0] if single else list(res[:n_out])


def _comm_only(name, comm):
    def body(o_ref):
        o_ref[...] = jnp.zeros_like(o_ref)

    _call(name, body, (), [], pl.BlockSpec(memory_space=pltpu.VMEM), jax.ShapeDtypeStruct((8, 128), F32), [], comm)


def _mm(name, grid, prods, extras, outs, epi, nacc=1, comm=()):
    n_p, n_e = len(prods), len(extras)

    def body(*refs):
        ab, ex, out = refs[:2 * n_p], refs[2 * n_p:2 * n_p + n_e], refs[2 * n_p + n_e:]
        accs = [None] * nacc
        for i, prod in enumerate(prods):
            dn, acc, loop = prod[6], prod[7], prod[8]
            a_ref, b_ref = ab[2 * i], ab[2 * i + 1]
            if loop:
                for g in range(loop):
                    t = lax.dot_general(a_ref[g], b_ref[g], _DN[dn], preferred_element_type=F32)
                    accs[acc] = t if accs[acc] is None else accs[acc] + t
            else:
                t = lax.dot_general(a_ref[...], b_ref[...], _DN[dn], preferred_element_type=F32)
                accs[acc] = t if accs[acc] is None else accs[acc] + t
        epi(accs, ex, out)

    in_specs, args = [], []
    for prod in prods:
        in_specs += [pl.BlockSpec(prod[1], prod[2]), pl.BlockSpec(prod[4], prod[5])]
        args += [prod[0], prod[3]]
    for e, e_blk, e_map in extras:
        in_specs.append(pl.BlockSpec(e_blk, e_map))
        args.append(e)
    return _call(name, body, grid, in_specs, [pl.BlockSpec(blk, mp) for _, _, blk, mp in outs],
                 [jax.ShapeDtypeStruct(s, d) for s, d, _, _ in outs], args, comm)


def _store(accs, ex, out):
    out[0][...] = accs[0].astype(out[0].dtype)


def _ew_tile(r, c, budget=3 << 19):
    for t in range(r - r % 16, 0, -16):
        if r % t == 0 and t * c * 4 <= budget:
            return t, c
    for t in range(c - c % 128, 0, -128):
        if c % t == 0 and r * t * 4 <= budget:
            return r, t
    return r, c


def _tile(n, want):
    t = min(n, want)
    assert n % t == 0, (n, want)
    return t


def _mm_nn(name, a, b, out_dtype, tm=512, tn=512, comm=()):
    m, k = a.shape
    n = b.shape[1]
    tm, tn = _tile(m, tm), (tn if n % tn == 0 else n)
    return _mm(name, (n // tn, m // tm),
               [(a, (tm, k), lambda j, i: (i, 0), b, (k, tn), lambda j, i: (0, j), "nn", 0, 0)], [],
               [((m, n), out_dtype, (tm, tn), lambda j, i: (i, j))], _store, comm=comm)[0]


def _mm_nt(name, a, bt, out_dtype, tm=512, tn=512, comm=()):
    m, k = a.shape
    n = bt.shape[0]
    tm, tn = _tile(m, tm), (tn if n % tn == 0 else n)
    return _mm(name, (n // tn, m // tm),
               [(a, (tm, k), lambda j, i: (i, 0), bt, (tn, k), lambda j, i: (j, 0), "nt", 0, 0)], [],
               [((m, n), out_dtype, (tm, tn), lambda j, i: (i, j))], _store, comm=comm)[0]


def _mm_tn_into(name, a, b, buf, row0, ta=1024, tb=512):
    t, ka = a.shape
    nb = b.shape[1]
    ta, tb = (ta if ka % ta == 0 else ka), (tb if nb % tb == 0 else nb)

    def body(a_ref, b_ref, buf_in, buf_out, tile, sem):
        i, j = pl.program_id(0), pl.program_id(1)
        tile[...] = lax.dot_general(a_ref[...], b_ref[...], _DN["tn"], preferred_element_type=F32).astype(tile.dtype)
        rows = pl.ds(pl.multiple_of(row0 + i * ta, 16), ta)
        cp = pltpu.make_async_copy(tile, buf_out.at[rows, pl.ds(pl.multiple_of(j * tb, 128), tb)], sem)
        cp.start()
        cp.wait()

    return pl.pallas_call(
        body, name=name, grid=(ka // ta, nb // tb),
        in_specs=[pl.BlockSpec((t, ta), lambda i, j: (0, i)), pl.BlockSpec((t, tb), lambda i, j: (0, j)), ANY],
        out_specs=ANY, out_shape=jax.ShapeDtypeStruct(buf.shape, buf.dtype), input_output_aliases={2: 0},
        scratch_shapes=[pltpu.VMEM((ta, tb), buf.dtype), pltpu.SemaphoreType.DMA],
        compiler_params=_params(2))(a, b, buf)


def _mm_tn(name, a, b, out_dtype, ta=512, tb=512, scale=None):
    t, ka = a.shape
    nb = b.shape[1]
    ta, tb = (ta if ka % ta == 0 else ka), (tb if nb % tb == 0 else nb)

    def epi(accs, ex, out):
        v = accs[0] if scale is None else accs[0] * scale
        out[0][...] = v.astype(out[0].dtype)

    return _mm(name, (ka // ta, nb // tb),
               [(a, (t, ta), lambda i, j: (0, i), b, (t, tb), lambda i, j: (0, j), "tn", 0, 0)], [],
               [((ka, nb), out_dtype, (ta, tb), lambda i, j: (i, j))], epi)[0]


def _rms_fwd(name, x, g, tm=256, comm=()):
    s, d = x.shape
    tm = _tile(s, tm)

    def body(x_ref, g_ref, o_ref):
        v = x_ref[...]
        o_ref[...] = (v * lax.rsqrt(jnp.mean(v * v, axis=-1, keepdims=True) + EPS) * g_ref[...]).astype(o_ref.dtype)

    return _call(name, body, (s // tm,),
                 [pl.BlockSpec((tm, d), lambda i: (i, 0)), pl.BlockSpec((1, d), lambda i: (0, 0))],
                 pl.BlockSpec((tm, d), lambda i: (i, 0)), jax.ShapeDtypeStruct((s, d), BF), [x, g], comm)


def _acc_rows(ref, part, i):
    @pl.when(i == 0)
    def _():
        ref[...] = part

    @pl.when(i > 0)
    def _():
        ref[...] += part


def _rms_bwd_math(dn, v, g):
    rstd = lax.rsqrt(jnp.mean(v * v, axis=-1, keepdims=True) + EPS)
    xh = v * rstd
    dxh = dn * g
    dx = rstd * (dxh - xh * jnp.mean(dxh * xh, axis=-1, keepdims=True))
    return dx, jnp.sum(dn * xh, axis=0, keepdims=True)


def _rms_bwd(name, dn, x, g, resid, tm=256, comm=()):
    s, d = x.shape
    tm = _tile(s, tm)

    def body(dn_ref, x_ref, g_ref, r_ref, dx_ref, dg_ref):
        dx, part = _rms_bwd_math(dn_ref[...].astype(F32), x_ref[...], g_ref[...])
        dx_ref[...] = r_ref[...] + dx
        _acc_rows(dg_ref, part, pl.program_id(0))

    row = pl.BlockSpec((tm, d), lambda i: (i, 0))
    one = pl.BlockSpec((1, d), lambda i: (0, 0))
    return _call(name, body, (s // tm,), [row, row, one, row], [row, one],
                 [jax.ShapeDtypeStruct((s, d), F32), jax.ShapeDtypeStruct((1, d), F32)], [dn, x, g, resid], comm)


def _loss_head(h, target, g, tm=256):
    s, d = h.shape
    tm = _tile(s, tm)

    def body(h_ref, t_ref, g_ref, dh_ref, dg_ref, loss_ref):
        v, gv = h_ref[...], g_ref[...]
        rstd = lax.rsqrt(jnp.mean(v * v, axis=-1, keepdims=True) + EPS)
        xh = v * rstd
        err = xh * gv - t_ref[...]
        part_loss = 0.5 * jnp.sum(jnp.mean(err * err, axis=-1, keepdims=True), axis=0, keepdims=True)
        dy = err * (1.0 / d)
        dxh = dy * gv
        dh_ref[...] = rstd * (dxh - xh * jnp.mean(dxh * xh, axis=-1, keepdims=True))
        i = pl.program_id(0)
        _acc_rows(dg_ref, jnp.sum(dy * xh, axis=0, keepdims=True), i)
        _acc_rows(loss_ref, jnp.broadcast_to(part_loss, loss_ref.shape), i)

    row = pl.BlockSpec((tm, d), lambda i: (i, 0))
    one = pl.BlockSpec((1, d), lambda i: (0, 0))
    return pl.pallas_call(
        body, name="loss_head", grid=(s // tm,), in_specs=[row, row, one],
        out_specs=[row, one, pl.BlockSpec((1, 128), lambda i: (0, 0))],
        out_shape=[jax.ShapeDtypeStruct((s, d), F32), jax.ShapeDtypeStruct((1, d), F32),
                   jax.ShapeDtypeStruct((1, 128), F32)],
        compiler_params=_params(1))(h, target, g)


def _pl_bwd_elem(dh, pe, t, tm=256):
    s, d = dh.shape
    tm = _tile(s, tm)

    def body(dh_ref, pe_ref, t_ref, dt_ref, dpe_ref):
        dh_v, sg = dh_ref[...], _sig(t_ref[...])
        dt_ref[...] = (dh_v * pe_ref[...].astype(F32) * sg * (1.0 - sg)).astype(BF)
        dpe_ref[...] = (dh_v * sg).astype(BF)

    row = pl.BlockSpec((tm, d), lambda i: (i, 0))
    return pl.pallas_call(
        body, name="pl_bwd_elem", grid=(s // tm,), in_specs=[row, row, row], out_specs=[row, row],
        out_shape=[jax.ShapeDtypeStruct((s, d), BF)] * 2, compiler_params=_params(1))(dh, pe, t)


def _ffn_up(name, xn, wg, wu, tm=1024, comm=()):
    s, d = xn.shape
    g, fb, _ = wg.shape
    tm = _tile(s, tm)

    def epi(accs, ex, out):
        hg, hu = accs
        out[0][...] = hg.astype(BF)
        out[1][...] = hu.astype(BF)
        out[2][...] = (hg * _sig(hg) * hu).astype(BF)

    a_map = lambda j, i: (i, 0)
    w_map = lambda j, i: (j, 0, 0)
    o = ((g, s, fb), BF, (None, tm, fb), lambda j, i: (j, i, 0))
    return _mm(name, (g, s // tm),
               [(xn, (tm, d), a_map, wg, (None, fb, d), w_map, "nt", 0, 0),
                (xn, (tm, d), a_map, wu, (None, fb, d), w_map, "nt", 1, 0)], [], [o, o, o], epi, nacc=2, comm=comm)


def _ffn_down(name, a, wd, resid, tm=512, tn=512, comm=()):
    g, s, fb = a.shape
    d = wd.shape[2]
    tm, tn = _tile(s, tm), _tile(d, tn)

    def epi(accs, ex, out):
        out[0][...] = ex[0][...] + 0.5 * accs[0]

    return _mm(name, (d // tn, s // tm),
               [(a, (g, tm, fb), lambda j, i: (0, i, 0), wd, (g, fb, tn), lambda j, i: (0, 0, j), "nn", 0, g)],
               [(resid, (tm, tn), lambda j, i: (i, j))],
               [((s, d), F32, (tm, tn), lambda j, i: (i, j))], epi, comm=comm)[0]


def _ffn_bwd_act(name, dh, wd, hg, hu, tm=1024, comm=()):
    s, d = dh.shape
    g, fb, _ = wd.shape
    tm = _tile(s, tm)

    def epi(accs, ex, out):
        da = 0.5 * accs[0]
        hg_v, hu_v = ex[0][...].astype(F32), ex[1][...].astype(F32)
        sg = _sig(hg_v)
        out[0][...] = (da * hu_v * (sg * (1.0 + hg_v * (1.0 - sg)))).astype(BF)
        out[1][...] = (da * (hg_v * sg)).astype(BF)

    blk = (None, tm, fb)
    gmap = lambda j, i: (j, i, 0)
    return _mm(name, (g, s // tm),
               [(dh, (tm, d), lambda j, i: (i, 0), wd, (None, fb, d), lambda j, i: (j, 0, 0), "nt", 0, 0)],
               [(hg, blk, gmap), (hu, blk, gmap)],
               [((g, s, fb), BF, blk, gmap), ((g, s, fb), BF, blk, gmap)], epi, comm=comm)


def _ffn_bwd_wd(name, a, dh, tn=1024, comm=()):
    g, s, fb = a.shape
    d = dh.shape[1]
    tn = _tile(d, tn)

    def epi(accs, ex, out):
        out[0][...] = (0.5 * accs[0]).astype(BF)

    return _mm(name, (g, d // tn),
               [(a, (None, s, fb), lambda j, i: (j, 0, 0), dh, (s, tn), lambda j, i: (0, i), "tn", 0, 0)], [],
               [((g, fb, d), BF, (None, fb, tn), lambda j, i: (j, 0, i))], epi, comm=comm)[0]


def _ffn_bwd_wup(name, xn, dhg, dhu, tk=1024, comm=()):
    s, d = xn.shape
    g, _, fb = dhg.shape
    tk = _tile(d, tk)

    def epi(accs, ex, out):
        out[0][...] = accs[0].astype(BF)
        out[1][...] = accs[1].astype(BF)

    a_map = lambda j, i: (j, 0, 0)
    b_map = lambda j, i: (0, i)
    o = ((g, fb, d), BF, (None, fb, tk), lambda j, i: (j, 0, i))
    return _mm(name, (g, d // tk),
               [(dhg, (None, s, fb), a_map, xn, (s, tk), b_map, "tn", 0, 0),
                (dhu, (None, s, fb), a_map, xn, (s, tk), b_map, "tn", 1, 0)], [], [o, o], epi, nacc=2, comm=comm)


def _ffn_bwd_x(name, dhg, dhu, wg, wu, tm=512, tn=512, comm=()):
    g, s, fb = dhg.shape
    d = wg.shape[2]
    tm, tn = _tile(s, tm), _tile(d, tn)
    a_blk, a_map = (g, tm, fb), lambda j, i: (0, i, 0)
    b_blk, b_map = (g, fb, tn), lambda j, i: (0, 0, j)
    return _mm(name, (d // tn, s // tm),
               [(dhg, a_blk, a_map, wg, b_blk, b_map, "nn", 0, g), (dhu, a_blk, a_map, wu, b_blk, b_map, "nn", 0, g)],
               [], [((s, d), F32, (tm, tn), lambda j, i: (i, j))], _store, comm=comm)[0]


def _ffn_forward(tag, h, gain, get_wgu, get_wd, norm_comm=(), up_comm=(), down_comm=()):
    xn = _rms_fwd(tag + "_norm", h, gain, comm=norm_comm)
    wg, wu = get_wgu()
    hg, hu, a = _ffn_up(tag + "_up", xn, wg, wu, comm=up_comm)
    return _ffn_down(tag + "_down", a, get_wd(), h, comm=down_comm), (xn, hg, hu, a)


def _na_geometry(rows):
    kh = min(NA_ROWS_WIN, rows)
    cols = np.arange(GRID_W)
    col_start = np.clip(cols - NA_COLS_WIN // 2, 0, GRID_W - NA_COLS_WIN)
    mask = (cols[None, :] >= col_start[:, None]) & (cols[None, :] < col_start[:, None] + NA_COLS_WIN)
    dc = np.clip(cols[None, :] - cols[:, None], -(NA_COLS_WIN - 1), NA_COLS_WIN - 1) + (NA_COLS_WIN - 1)
    return kh, mask, dc


def _na_table(rpb, rows):
    _, mask, dc = _na_geometry(rows)
    return jnp.where(jnp.asarray(mask)[None, None], rpb[:, :, dc], NEG)


class _NaPlan:
    def __init__(self, s):
        self.s, self.rows = s, s // GRID_W
        self.kh = min(NA_ROWS_WIN, self.rows)
        self.qr = min(NA_QROWS, self.rows)
        self.kr = min(self.rows, self.kh + self.qr - 1)
        self.groups = self.rows // self.qr
        self.nd = 2 * NA_ROWS_WIN - 1
        self.hw, self.nq = NA_HG * NA_DIM, NA_HEADS // NA_HG
        clip = lambda v, hi: min(max(v, 0), hi)
        pats = [(clip(g * self.qr - self.kh // 2, self.rows - self.kr) - g * self.qr,)
                + tuple(clip(g * self.qr + a - self.kh // 2, self.rows - self.kh) - g * self.qr for a in range(self.qr))
                for g in range(self.groups)]
        self.rebuild = [g for g in range(self.groups) if g == 0 or pats[g] != pats[g - 1]]

    def first_key_row(self, g):
        return jnp.clip(g * self.qr - self.kh // 2, 0, self.rows - self.kr)

    def specs(self):
        blk = pl.BlockSpec((self.qr * GRID_W, self.hw), lambda j, g: (g, j))
        k_spec = pl.BlockSpec((self.s, self.hw), lambda j, g: (0, self.nq + j))
        v_spec = pl.BlockSpec((self.s, self.hw), lambda j, g: (0, 2 * self.nq + j))
        t_spec = pl.BlockSpec((NA_HG, self.nd, GRID_W, GRID_W), lambda j, g: (j, 0, 0, 0))
        return blk, k_spec, v_spec, t_spec

    def bias_scratch(self):
        return pltpu.VMEM((NA_HG, self.qr * GRID_W, self.kr * GRID_W), F32)

    def fill_bias(self, t_ref, bias_ref, g):
        def build():
            r0, ks = g * self.qr, self.first_key_row(g)
            for a in range(self.qr):
                rs = jnp.clip(r0 + a - self.kh // 2, 0, self.rows - self.kh)
                for i in range(self.kr):
                    valid = jnp.logical_and(ks + i >= rs, ks + i < rs + self.kh)
                    idx = jnp.clip(ks + i - r0 - a + NA_ROWS_WIN - 1, 0, self.nd - 1)
                    for h in range(NA_HG):
                        bias_ref[h, a * GRID_W:(a + 1) * GRID_W, i * GRID_W:(i + 1) * GRID_W] = jnp.where(
                            valid, t_ref[h, idx], NEG)

        pl.when(functools.reduce(jnp.logical_or, [g == r for r in self.rebuild]))(build)

    def window(self, g):
        return pl.ds(pl.multiple_of(self.first_key_row(g) * GRID_W, GRID_W), self.kr * GRID_W)


def _na_probs(q, k, bias):
    sc = lax.dot_general(q, k, _DN["nt"], preferred_element_type=F32) * (NA_DIM ** -0.5) + bias
    e = jnp.exp(sc - jnp.max(sc, axis=-1, keepdims=True))
    return e / jnp.sum(e, axis=-1, keepdims=True)


def _na_fwd(qkv, table, comm=()):
    plan = _NaPlan(qkv.shape[0])
    blk, k_spec, v_spec, t_spec = plan.specs()

    def body(q_ref, k_ref, v_ref, t_ref, o_ref, bias_ref):
        g = pl.program_id(1)
        plan.fill_bias(t_ref, bias_ref, g)
        win = plan.window(g)
        for h in range(NA_HG):
            cs = slice(h * NA_DIM, (h + 1) * NA_DIM)
            p = _na_probs(q_ref[:, cs], k_ref[win, cs], bias_ref[h])
            o_ref[:, cs] = jnp.dot(p.astype(BF), v_ref[win, cs], preferred_element_type=F32).astype(BF)

    return _call("na_fwd", body, (plan.nq, plan.groups), [blk, k_spec, v_spec, t_spec], blk,
                 jax.ShapeDtypeStruct((plan.s, NA_HEADS * NA_DIM), BF), [qkv, qkv, qkv, table], comm,
                 scratch=[plan.bias_scratch()])


def _na_bwd(qkv, table, do, comm=()):
    plan = _NaPlan(qkv.shape[0])
    blk, k_spec, v_spec, t_spec = plan.specs()
    qr, kr = plan.qr, plan.kr

    def body(q_ref, k_ref, v_ref, t_ref, do_ref, dq_ref, dk_ref, dv_ref, dt_ref, bias_ref):
        g = pl.program_id(1)

        @pl.when(g == 0)
        def _():
            dk_ref[...] = jnp.zeros_like(dk_ref)
            dv_ref[...] = jnp.zeros_like(dv_ref)
            dt_ref[...] = jnp.zeros_like(dt_ref)

        plan.fill_bias(t_ref, bias_ref, g)
        win = plan.window(g)
        base = plan.first_key_row(g) - g * qr + NA_ROWS_WIN - 1
        for h in range(NA_HG):
            cs = slice(h * NA_DIM, (h + 1) * NA_DIM)
            q, k, v, do_h = q_ref[:, cs], k_ref[win, cs], v_ref[win, cs], do_ref[:, cs]
            p = _na_probs(q, k, bias_ref[h])
            dp = lax.dot_general(do_h, v, _DN["nt"], preferred_element_type=F32)
            ds = p * (dp - jnp.sum(p * dp, axis=-1, keepdims=True))
            for dlt in range(1 - qr, kr):
                tiles = [ds[a * GRID_W:(a + 1) * GRID_W, (a + dlt) * GRID_W:(a + dlt + 1) * GRID_W]
                         for a in range(qr) if 0 <= a + dlt < kr]
                dt_ref[h, jnp.clip(base + dlt, 0, plan.nd - 1)] += functools.reduce(jnp.add, tiles)
            dsb = (ds * (NA_DIM ** -0.5)).astype(BF)
            dq_ref[:, cs] = jnp.dot(dsb, k, preferred_element_type=F32).astype(BF)
            dk_ref[win, cs] += lax.dot_general(dsb, q, _DN["tn"], preferred_element_type=F32)
            dv_ref[win, cs] += lax.dot_general(p.astype(BF), do_h, _DN["tn"], preferred_element_type=F32)

    width = NA_HEADS * NA_DIM
    whole = pl.BlockSpec((plan.s, plan.hw), lambda j, g: (0, j))
    return _call(
        "na_bwd", body, (plan.nq, plan.groups), [blk, k_spec, v_spec, t_spec, blk], [blk, whole, whole, t_spec],
        [jax.ShapeDtypeStruct((plan.s, width), BF), jax.ShapeDtypeStruct((plan.s, width), F32),
         jax.ShapeDtypeStruct((plan.s, width), F32),
         jax.ShapeDtypeStruct((NA_HEADS, plan.nd, GRID_W, GRID_W), F32)],
        [qkv, qkv, qkv, table, do], comm, scratch=[plan.bias_scratch()])


def _na_rpb_grad(dt, rows):
    _, mask, dc = _na_geometry(rows)
    nd, nc = 2 * NA_ROWS_WIN - 1, 2 * NA_COLS_WIN - 1
    onehot = np.zeros((GRID_W * GRID_W, 128), np.float32)
    onehot[np.arange(GRID_W * GRID_W), dc.reshape(-1)] = mask.reshape(-1).astype(np.float32)
    flat = dt.reshape(NA_HEADS * nd, GRID_W * GRID_W)

    def body(a_ref, e_ref, o_ref):
        o_ref[...] = jnp.dot(a_ref[...], e_ref[...], precision=HI, preferred_element_type=F32)

    out = pl.pallas_call(body, name="na_rpb_grad", out_shape=jax.ShapeDtypeStruct((NA_HEADS * nd, 128), F32),
                         compiler_params=_params(0))(flat, jnp.asarray(onehot))
    return out[:, :nc].reshape(NA_HEADS, nd, nc)


def _rope_consts(s):
    pos = np.arange(s, dtype=np.float32)
    inv = (1.0 / (ROPE_THETA ** (np.arange(0, ML_ROPE, 2, dtype=np.float32) / ML_ROPE))).astype(np.float32)
    ang = pos[:, None] * inv[None, :]
    cos, sin = np.cos(ang).astype(np.float32), np.sin(ang).astype(np.float32)
    half = ML_ROPE // 2
    rot = np.zeros((ML_ROPE, ML_ROPE), np.float32)
    rot[np.arange(half) + half, np.arange(half)] = -1.0
    rot[np.arange(half), np.arange(half) + half] = 1.0
    return (jnp.asarray(np.concatenate([cos, cos], 1)), jnp.asarray(np.concatenate([sin, sin], 1)),
            jnp.asarray(rot), jnp.asarray(rot.T.copy()))


def _rope(v, cos, sin, rot):
    return v * cos + jnp.dot(v, rot, precision=HI, preferred_element_type=F32) * sin


def _unrope(dv, cos, sin, rot_t):
    return dv * cos + jnp.dot(dv * sin, rot_t, precision=HI, preferred_element_type=F32)


def _rms(v, g):
    return v * lax.rsqrt(jnp.mean(v * v, axis=-1, keepdims=True) + EPS) * g


def _mla_prep(lat, gq, gkv, cos, sin, rot, tm=256):
    s, w = lat.shape
    tm = _tile(s, tm)

    def body(l_ref, gq_ref, gkv_ref, c_ref, s_ref, r_ref, cq_ref, ckv_ref, kr_ref):
        cq_ref[...] = _rms(l_ref[:, :ML_RANK], gq_ref[...]).astype(BF)
        ckv_ref[...] = _rms(l_ref[:, ML_RANK:2 * ML_RANK], gkv_ref[...]).astype(BF)
        kr_ref[...] = _rope(l_ref[:, 2 * ML_RANK:], c_ref[...], s_ref[...], r_ref[...]).astype(BF)

    row = lambda c: pl.BlockSpec((tm, c), lambda i: (i, 0))
    full = lambda a: pl.BlockSpec(a.shape, lambda i: (0, 0))
    return pl.pallas_call(
        body, name="mla_prep", grid=(s // tm,),
        in_specs=[row(w), full(gq), full(gkv), row(ML_ROPE), row(ML_ROPE), full(rot)],
        out_specs=[row(ML_RANK), row(ML_RANK), row(ML_ROPE)],
        out_shape=[jax.ShapeDtypeStruct((s, ML_RANK), BF), jax.ShapeDtypeStruct((s, ML_RANK), BF),
                   jax.ShapeDtypeStruct((s, ML_ROPE), BF)],
        compiler_params=_params(1))(lat, gq, gkv, cos, sin, rot)


def _mla_q_proj(cq, wuq, cos, sin, rot, tm=512, comm=()):
    s, k = cq.shape
    tm = _tile(s, tm)

    def epi(accs, ex, out):
        acc = accs[0]
        out[0][:, :ML_NOPE] = acc[:, :ML_NOPE].astype(BF)
        out[0][:, ML_NOPE:] = _rope(acc[:, ML_NOPE:], ex[0][...], ex[1][...], ex[2][...]).astype(BF)

    rmap = lambda j, i: (i, 0)
    return _mm("mla_q_proj", (ML_HEADS, s // tm),
               [(cq, (tm, k), rmap, wuq, (None, ML_QK, k), lambda j, i: (j, 0, 0), "nt", 0, 0)],
               [(cos, (tm, ML_ROPE), rmap), (sin, (tm, ML_ROPE), rmap), (rot, rot.shape, lambda j, i: (0, 0))],
               [((ML_HEADS, s, ML_QK), BF, (None, tm, ML_QK), lambda j, i: (j, i, 0))], epi, comm=comm)[0]


def _mla_kv_proj(ckv, wukv, kr, tm=512, comm=()):
    s, k = ckv.shape
    tm = _tile(s, tm)

    def epi(accs, ex, out):
        acc = accs[0]
        out[0][:, :ML_NOPE] = acc[:, :ML_NOPE].astype(BF)
        out[0][:, ML_NOPE:] = ex[0][...]
        out[1][...] = acc[:, ML_NOPE:].astype(BF)

    rmap = lambda j, i: (i, 0)
    gmap = lambda j, i: (j, i, 0)
    return _mm("mla_kv_proj", (ML_HEADS, s // tm),
               [(ckv, (tm, k), rmap, wukv, (None, k, ML_NOPE + ML_V), lambda j, i: (j, 0, 0), "nn", 0, 0)],
               [(kr, (tm, ML_ROPE), rmap)],
               [((ML_HEADS, s, ML_QK), BF, (None, tm, ML_QK), gmap), ((ML_HEADS, s, ML_V), BF, (None, tm, ML_V), gmap)],
               epi, comm=comm)


def _mla_probs(q, k):
    sc = lax.dot_general(q, k, _DN["nt"], preferred_element_type=F32) * (ML_QK ** -0.5)
    e = jnp.exp(sc - jnp.max(sc, axis=-1, keepdims=True))
    return e / jnp.sum(e, axis=-1, keepdims=True)


def _mla_fwd(q, k, v, tq=1024, comm=()):
    _, s, _ = q.shape
    tq = _tile(s, tq)

    def body(q_ref, k_ref, v_ref, o_ref):
        p = _mla_probs(q_ref[...], k_ref[...])
        o_ref[...] = jnp.dot(p.astype(BF), v_ref[...], preferred_element_type=F32).astype(BF)

    return _call("mla_fwd", body, (ML_HEADS, s // tq),
                 [pl.BlockSpec((None, tq, ML_QK), lambda h, i: (h, i, 0)),
                  pl.BlockSpec((None, s, ML_QK), lambda h, i: (h, 0, 0)),
                  pl.BlockSpec((None, s, ML_V), lambda h, i: (h, 0, 0))],
                 pl.BlockSpec((tq, ML_V), lambda h, i: (i, h)),
                 jax.ShapeDtypeStruct((s, ML_HEADS * ML_V), BF), [q, k, v], comm)


def _mla_bwd(q, k, v, do, tq=512, comm=()):
    _, s, _ = q.shape
    tq = _tile(s, tq)

    def body(q_ref, k_ref, v_ref, do_ref, dq_ref, dk_ref, dv_ref):
        i = pl.program_id(1)
        qv, kv, vv, dov = q_ref[...], k_ref[...], v_ref[...], do_ref[...]
        p = _mla_probs(qv, kv)
        dp = lax.dot_general(dov, vv, _DN["nt"], preferred_element_type=F32)
        ds = (p * (dp - jnp.sum(p * dp, axis=-1, keepdims=True)) * (ML_QK ** -0.5)).astype(BF)
        dq_ref[...] = jnp.dot(ds, kv, preferred_element_type=F32)
        _acc_rows(dk_ref, lax.dot_general(ds, qv, _DN["tn"], preferred_element_type=F32), i)
        _acc_rows(dv_ref, lax.dot_general(p.astype(BF), dov, _DN["tn"], preferred_element_type=F32), i)

    return _call(
        "mla_bwd", body, (ML_HEADS, s // tq),
        [pl.BlockSpec((None, tq, ML_QK), lambda h, i: (h, i, 0)),
         pl.BlockSpec((None, s, ML_QK), lambda h, i: (h, 0, 0)),
         pl.BlockSpec((None, s, ML_V), lambda h, i: (h, 0, 0)),
         pl.BlockSpec((tq, ML_V), lambda h, i: (i, h))],
        [pl.BlockSpec((None, tq, ML_QK), lambda h, i: (h, i, 0)),
         pl.BlockSpec((None, s, ML_QK), lambda h, i: (h, 0, 0)),
         pl.BlockSpec((None, s, ML_V), lambda h, i: (h, 0, 0))],
        [jax.ShapeDtypeStruct((ML_HEADS, s, ML_QK), F32), jax.ShapeDtypeStruct((ML_HEADS, s, ML_QK), F32),
         jax.ShapeDtypeStruct((ML_HEADS, s, ML_V), F32)],
        [q, k, v, do], comm)


def _mla_post(dq, dk, dv, cos, sin, rot_t, tm=1024):
    _, s, _ = dq.shape
    tm = _tile(s, tm)

    def body(dq_ref, dk_ref, dv_ref, c_ref, s_ref, r_ref, dqp_ref, dkv_ref, dkr_ref):
        h = pl.program_id(1)
        dqv, dkk = dq_ref[...], dk_ref[...]
        dqp_ref[:, :ML_NOPE] = dqv[:, :ML_NOPE].astype(BF)
        dqp_ref[:, ML_NOPE:] = _unrope(dqv[:, ML_NOPE:], c_ref[...], s_ref[...], r_ref[...]).astype(BF)
        dkv_ref[:, :ML_NOPE] = dkk[:, :ML_NOPE].astype(BF)
        dkv_ref[:, ML_NOPE:] = dv_ref[...].astype(BF)
        _acc_rows(dkr_ref, dkk[:, ML_NOPE:], h)

    gspec = lambda c: pl.BlockSpec((None, tm, c), lambda i, h: (h, i, 0))
    rspec = pl.BlockSpec((tm, ML_ROPE), lambda i, h: (i, 0))
    return pl.pallas_call(
        body, name="mla_post", grid=(s // tm, ML_HEADS),
        in_specs=[gspec(ML_QK), gspec(ML_QK), gspec(ML_V), rspec, rspec,
                  pl.BlockSpec(rot_t.shape, lambda i, h: (0, 0))],
        out_specs=[gspec(ML_QK), gspec(ML_NOPE + ML_V), rspec],
        out_shape=[jax.ShapeDtypeStruct((ML_HEADS, s, ML_QK), BF),
                   jax.ShapeDtypeStruct((ML_HEADS, s, ML_NOPE + ML_V), BF),
                   jax.ShapeDtypeStruct((s, ML_ROPE), F32)],
        compiler_params=_params(2))(dq, dk, dv, cos, sin, rot_t)


def _mla_lat_bwd(dcq, dckv, dkr, lat, gq, gkv, cos, sin, rot_t, tm=256):
    s, w = lat.shape
    tm = _tile(s, tm)

    def body(dcq_ref, dckv_ref, dkr_ref, l_ref, gq_ref, gkv_ref, c_ref, s_ref, r_ref, dl_ref, dgq_ref, dgkv_ref):
        i = pl.program_id(0)
        dql, pq = _rms_bwd_math(dcq_ref[...], l_ref[:, :ML_RANK], gq_ref[...])
        dkl, pkv = _rms_bwd_math(dckv_ref[...], l_ref[:, ML_RANK:2 * ML_RANK], gkv_ref[...])
        dl_ref[:, :ML_RANK] = dql.astype(BF)
        dl_ref[:, ML_RANK:2 * ML_RANK] = dkl.astype(BF)
        dl_ref[:, 2 * ML_RANK:] = _unrope(dkr_ref[...], c_ref[...], s_ref[...], r_ref[...]).astype(BF)
        _acc_rows(dgq_ref, pq, i)
        _acc_rows(dgkv_ref, pkv, i)

    row = lambda c: pl.BlockSpec((tm, c), lambda i: (i, 0))
    full = lambda a: pl.BlockSpec(a.shape, lambda i: (0, 0))
    return pl.pallas_call(
        body, name="mla_lat_bwd", grid=(s // tm,),
        in_specs=[row(ML_RANK), row(ML_RANK), row(ML_ROPE), row(w), full(gq), full(gkv), row(ML_ROPE), row(ML_ROPE),
                  full(rot_t)],
        out_specs=[row(w), full(gq), full(gkv)],
        out_shape=[jax.ShapeDtypeStruct((s, w), BF), jax.ShapeDtypeStruct(gq.shape, F32),
                   jax.ShapeDtypeStruct(gkv.shape, F32)],
        compiler_params=_params(1))(dcq, dckv, dkr, lat, gq, gkv, cos, sin, rot_t)


def _grp_dw(name, a, dout, ta=1024):
    s, k = a.shape
    ta = _tile(k, ta)
    if dout.ndim == 3:
        g, _, nb = dout.shape
        b_blk, b_map = (None, s, nb), lambda j, i: (j, 0, 0)
    else:
        g, nb = NDEV, dout.shape[1] // NDEV
        b_blk, b_map = (s, nb), lambda j, i: (0, j)
    return _mm(name, (g, k // ta),
               [(a, (s, ta), lambda j, i: (0, i), dout, b_blk, b_map, "tn", 0, 0)], [],
               [((g, k, nb), BF, (None, ta, nb), lambda j, i: (j, i, 0))], _store)[0]


def _grp_dw_t(name, dout, a, ta=512):
    g, s, nb = dout.shape
    k = a.shape[1]
    ta = _tile(k, ta)
    return _mm(name, (g, k // ta),
               [(dout, (None, s, nb), lambda j, i: (j, 0, 0), a, (s, ta), lambda j, i: (0, i), "tn", 0, 0)], [],
               [((g, nb, k), BF, (None, nb, ta), lambda j, i: (j, 0, i))], _store)[0]


def _grp_dx_t(name, dout, wt, tm=512, tn=512, comm=()):
    g, s, nb = dout.shape
    k = wt.shape[2]
    tm, tn = _tile(s, tm), _tile(k, tn)
    return _mm(name, (k // tn, s // tm),
               [(dout, (g, tm, nb), lambda j, i: (0, i, 0), wt, (g, nb, tn), lambda j, i: (0, 0, j), "nn", 0, g)], [],
               [((s, k), F32, (tm, tn), lambda j, i: (i, j))], _store, comm=comm)[0]


def _grp_dx(name, dout, w, tm=512, tn=512, comm=()):
    g, s, nb = dout.shape
    k = w.shape[1]
    tm, tn = _tile(s, tm), _tile(k, tn)
    return _mm(name, (k // tn, s // tm),
               [(dout, (g, tm, nb), lambda j, i: (0, i, 0), w, (g, tn, nb), lambda j, i: (0, j, 0), "nt", 0, g)], [],
               [((s, k), F32, (tm, tn), lambda j, i: (i, j))], _store, comm=comm)[0]


def _row_dw(name, a, dout, tn=2048):
    s, n = dout.shape
    tn = _tile(n, tn)
    if a.ndim == 3:
        kb = a.shape[2]
        a_blk, a_map = (None, s, kb), lambda j, i: (j, 0, 0)
    else:
        kb = a.shape[1] // NDEV
        a_blk, a_map = (s, kb), lambda j, i: (0, j)
    return _mm(name, (NDEV, n // tn),
               [(a, a_blk, a_map, dout, (s, tn), lambda j, i: (0, i), "tn", 0, 0)], [],
               [((NDEV, kb, n), BF, (None, kb, tn), lambda j, i: (j, 0, i))], _store)[0]


def _mix_merge(oa, ob, wa, wb, ga, gb, tm=1024, comm=()):
    s, k = oa.shape
    g, _, nb = wa.shape
    tm = _tile(s, tm)

    def epi(accs, ex, out):
        ya, yb = accs
        out[0][...] = ya.astype(BF)
        out[1][...] = yb.astype(BF)
        out[2][...] = (_sig(ex[0][...]) * ya + _sig(ex[1][...]) * yb).astype(BF)

    rmap = lambda j, i: (i, 0)
    wmap = lambda j, i: (j, 0, 0)
    o = ((g, s, nb), BF, (None, tm, nb), lambda j, i: (j, i, 0))
    cmap = lambda j, i: (i, j)
    return _mm("mix_merge", (g, s // tm),
               [(oa, (tm, k), rmap, wa, (None, k, nb), wmap, "nn", 0, 0),
                (ob, (tm, k), rmap, wb, (None, k, nb), wmap, "nn", 1, 0)],
               [(ga, (tm, nb), cmap), (gb, (tm, nb), cmap)], [o, o, o], epi, nacc=2, comm=comm)


def _mix_out(merged, wout, resid, tm=512, tn=512):
    g, s, kb = merged.shape
    d = wout.shape[2]
    tm, tn = _tile(s, tm), _tile(d, tn)

    def epi(accs, ex, out):
        out[0][...] = ex[0][...] + accs[0]

    return _mm("mix_out", (d // tn, s // tm),
               [(merged, (g, tm, kb), lambda j, i: (0, i, 0), wout, (g, kb, tn), lambda j, i: (0, 0, j), "nn", 0, g)],
               [(resid, (tm, tn), lambda j, i: (i, j))],
               [((s, d), F32, (tm, tn), lambda j, i: (i, j))], epi)[0]


def _mix_out_bwd(dh, wout, ga, gb, ya, yb, tm=1024, comm=()):
    s, d = dh.shape
    g, kb, _ = wout.shape
    tm = _tile(s, tm)

    def epi(accs, ex, out):
        dm = accs[0]
        sa, sb = _sig(ex[0][...]), _sig(ex[1][...])
        out[0][...] = (dm * sa).astype(BF)
        out[1][...] = (dm * sb).astype(BF)
        out[2][...] = (dm * ex[2][...].astype(F32) * sa * (1.0 - sa)).astype(BF)
        out[3][...] = (dm * ex[3][...].astype(F32) * sb * (1.0 - sb)).astype(BF)

    cmap = lambda j, i: (i, j)
    gmap = lambda j, i: (j, i, 0)
    og = ((g, s, kb), BF, (None, tm, kb), gmap)
    oc = ((s, g * kb), BF, (tm, kb), cmap)
    return _mm("mix_out_bwd", (g, s // tm),
               [(dh, (tm, d), lambda j, i: (i, 0), wout, (None, kb, d), lambda j, i: (j, 0, 0), "nt", 0, 0)],
               [(ga, (tm, kb), cmap), (gb, (tm, kb), cmap), (ya, (None, tm, kb), gmap), (yb, (None, tm, kb), gmap)],
               [og, og, oc, oc], epi, comm=comm)


def _pl_forward(n4, wplg, p, wpl, h3, tm=1024):
    s, d = n4.shape
    g, kb, _ = wplg.shape
    kp, nb = wpl.shape[1], wpl.shape[2]
    tm = _tile(s, tm)
    wplg_nat = wplg.reshape(g * kb, d)

    def epi(accs, ex, out):
        t, pe = accs
        out[0][...] = ex[0][...] + _sig(t) * pe
        out[1][...] = t
        out[2][...] = pe.astype(BF)

    rmap = lambda j, i: (i, 0)
    cmap = lambda j, i: (i, j)
    return _mm("pl_forward", (g, s // tm),
               [(n4, (tm, d), rmap, wplg_nat, (g * kb, nb), lambda j, i: (0, j), "nn", 0, 0),
                (p, (tm, kp), rmap, wpl, (None, kp, nb), lambda j, i: (j, 0, 0), "nn", 1, 0)],
               [(h3, (tm, nb), cmap)],
               [((s, d), F32, (tm, nb), cmap), ((s, d), F32, (tm, nb), cmap), ((s, d), BF, (tm, nb), cmap)],
               epi, nacc=2)


def _row_dx(name, dout, w, tm=1024, comm=()):
    s, n = dout.shape
    g, kb, _ = w.shape
    tm = _tile(s, tm)
    return _mm(name, (g, s // tm),
               [(dout, (tm, n), lambda j, i: (i, 0), w, (None, kb, n), lambda j, i: (j, 0, 0), "nt", 0, 0)], [],
               [((s, g * kb), F32, (tm, kb), lambda j, i: (i, j))], _store, comm=comm)[0]


def _in_proj_bwd_x(pieces, weights, tm=512, tn=512, comm=()):
    s = pieces[0].shape[0]
    d = weights[0].shape[1]
    tm, tn = _tile(s, tm), _tile(d, tn)
    prods = [(pc, (tm, pc.shape[1]), lambda j, i: (i, 0), w, (w.shape[0], tn), lambda j, i: (0, j), "nn", 0, 0)
             for pc, w in zip(pieces, weights)]
    return _mm("in_proj_dx", (d // tn, s // tm), prods, [],
               [((s, d), F32, (tm, tn), lambda j, i: (i, j))], _store, comm=comm)[0]


def _split_w_in(w_in_t):
    g, nb, d = w_in_t.shape
    nat = w_in_t.reshape(g * nb, d)
    na, lat = 3 * NA_HEADS * NA_DIM, 2 * ML_RANK + ML_ROPE
    return nat[:na], nat[na:na + lat], nat[na + lat:na + lat + d], nat[na + lat + d:]


def _pair_sum(name, part, landed, core):
    _, _, r, c = part.shape
    tr, tc = _ew_tile(r, c)

    def body(core_ref, a_ref, b_ref, o_ref):
        o_ref[...] = (a_ref[...].astype(F32) + b_ref[...].astype(F32)).astype(o_ref.dtype)

    return pl.pallas_call(
        body, name=name,
        grid_spec=pltpu.PrefetchScalarGridSpec(
            num_scalar_prefetch=1, grid=(NCHIP, r // tr, c // tc),
            in_specs=[pl.BlockSpec((None, None, tr, tc), lambda j, i, k, core_ref: (j, core_ref[0], i, k)),
                      pl.BlockSpec((None, tr, tc), lambda j, i, k, core_ref: (j, i, k))],
            out_specs=pl.BlockSpec((None, tr, tc), lambda j, i, k, core_ref: (j, i, k))),
        out_shape=jax.ShapeDtypeStruct(landed.shape, landed.dtype), compiler_params=_params(3),
    )(core, part, landed)


def _device_step(x, p, target, sp, own, core):
    s, d = x.shape
    rows = s // GRID_W
    cos, sin, rot, rot_t = _rope_consts(s)
    w, dw4, sums, dsp, pending = {}, {}, {}, {}, []

    def gather(*names):
        return _GatherPart(names, [own[n] for n in names])

    def got(part):
        w.update(zip(part.names, part.results))

    def grad(name, g):
        dw4[name] = g.reshape((NCHIP, 2) + g.shape[1:])

    def to_sibling(*names):
        return _SiblingPart(names, [dw4[n] for n in names])

    def add_pairs(part):
        for n, landed in zip(part.names, part.results):
            sums[n] = _pair_sum("pair_sum_" + n, dw4[n], landed, core)

    def start_chips(tag, *names):
        send, recv, thru, lands, token = _chips_start("rs_start_" + tag, [sums[n] for n in names])
        pending.append((tag, names, send, recv, thru, lands))
        return token

    c0 = gather("ffn1_w_gate", "ffn1_w_up")
    c1 = gather("ffn1_w_down")
    c2 = gather("w_in")

    def ffn1_wgu():
        got(c0)
        return w["ffn1_w_gate"], w["ffn1_w_up"]

    def ffn1_wd():
        got(c1)
        return w["ffn1_w_down"]

    h1, ffn1_saved = _ffn_forward("ffn1", x, sp["ffn1_norm"], ffn1_wgu, ffn1_wd,
                                  norm_comm=[c0], up_comm=[c1], down_comm=[c2])
    got(c2)
    wqkv, wlat, wga, wgb = _split_w_in(w["w_in"])
    u = _rms_fwd("mix_norm", h1, sp["mix_norm"])
    c3 = gather("w_uq", "w_ukv")
    qkv = _mm_nt("in_qkv", u, wqkv, BF, tn=1024, comm=[c3])
    got(c3)
    lat = _mm_nt("in_lat", u, wlat, F32)
    c3a = gather("w_branch_a")
    ga = _mm_nt("in_ga", u, wga, F32, tn=1024, comm=[c3a])
    got(c3a)
    c3b = gather("w_branch_b")
    gb = _mm_nt("in_gb", u, wgb, F32, tn=1024, comm=[c3b])
    got(c3b)
    tb = _na_table(sp["na_rpb"], rows)
    c4 = gather("ffn2_w_gate")
    oa = _na_fwd(qkv, tb, comm=[c4])
    got(c4)
    cq, ckv, kr = _mla_prep(lat, sp["q_a_norm"], sp["kv_a_norm"], cos, sin, rot)
    c4a = gather("w_out")
    qf = _mla_q_proj(cq, w["w_uq"], cos, sin, rot, comm=[c4a])
    got(c4a)
    c4b = gather("w_pl_gate")
    kf, vf = _mla_kv_proj(ckv, w["w_ukv"], kr, comm=[c4b])
    got(c4b)
    c5 = gather("ffn2_w_up")
    ob = _mla_fwd(qf, kf, vf, comm=[c5])
    got(c5)
    c5a = gather("w_pl")
    ya, yb, merged = _mix_merge(oa, ob, w["w_branch_a"], w["w_branch_b"], ga, gb, comm=[c5a])
    got(c5a)
    h2 = _mix_out(merged, w["w_out"], h1)
    c6 = gather("ffn2_w_down")

    def ffn2_wd():
        got(c6)
        return w["ffn2_w_down"]

    h3, ffn2_saved = _ffn_forward("ffn2", h2, sp["ffn2_norm"], lambda: (w["ffn2_w_gate"], w["ffn2_w_up"]), ffn2_wd,
                                  up_comm=[c6])
    n4 = _rms_fwd("pl_norm", h3, sp["pl_norm"])
    pb = p.astype(BF)
    h4, t, pe = _pl_forward(n4, w["w_pl_gate"], pb, w["w_pl"], h3)

    dh4, dsp["final_norm"], loss = _loss_head(h4, target, sp["final_norm"])
    dt, dpe = _pl_bwd_elem(dh4, pe, t)
    grad("w_pl", _grp_dw("pl_dw", pb, dpe))
    grad("w_pl_gate", _row_dw("plg_dw", n4, dt))
    s1 = to_sibling("w_pl", "w_pl_gate")
    dn4 = _row_dx("plg_dx", dt, w["w_pl_gate"], comm=[s1])
    add_pairs(s1)
    tok = start_chips("pl", "w_pl", "w_pl_gate")
    dh3, dsp["pl_norm"] = _rms_bwd("pl_dnorm", dn4, h3, sp["pl_norm"], dh4, comm=[_After(tok)])

    xn, hg, hu, a = ffn2_saved
    dhb = dh3.astype(BF)
    grad("ffn2_w_down", _ffn_bwd_wd("ffn2_dwd", a, dhb))
    s2 = to_sibling("ffn2_w_down")
    dhg, dhu = _ffn_bwd_act("ffn2_dact", dhb, w["ffn2_w_down"], hg, hu, comm=[s2])
    add_pairs(s2)
    tok = start_chips("ffn2_down", "ffn2_w_down")
    dwg, dwu = _ffn_bwd_wup("ffn2_dwup", xn, dhg, dhu, comm=[_After(tok)])
    grad("ffn2_w_gate", dwg)
    grad("ffn2_w_up", dwu)
    s3 = to_sibling("ffn2_w_gate", "ffn2_w_up")
    dxn = _ffn_bwd_x("ffn2_dx", dhg, dhu, w["ffn2_w_gate"], w["ffn2_w_up"], comm=[s3])
    add_pairs(s3)
    tok = start_chips("ffn2_up", "ffn2_w_gate", "ffn2_w_up")
    dh2, dsp["ffn2_norm"] = _rms_bwd("ffn2_dnorm", dxn, h2, sp["ffn2_norm"], dh3, comm=[_After(tok)])

    dh2b = dh2.astype(BF)
    grad("w_out", _row_dw("out_dw", merged, dh2b))
    s4 = to_sibling("w_out")
    dya, dyb, dga, dgb = _mix_out_bwd(dh2b, w["w_out"], ga, gb, ya, yb, comm=[s4])
    add_pairs(s4)
    grad("w_branch_a", _grp_dw("bra_dw", oa, dya))
    grad("w_branch_b", _grp_dw("brb_dw", ob, dyb))
    doa = _grp_dx("bra_dx", dya, w["w_branch_a"]).astype(BF)
    s5 = to_sibling("w_branch_a", "w_branch_b")
    dob = _grp_dx("brb_dx", dyb, w["w_branch_b"], comm=[s5]).astype(BF)
    add_pairs(s5)
    tok = start_chips("mix", "w_out", "w_branch_a", "w_branch_b")

    dqf, dkf, dvf = _mla_bwd(qf, kf, vf, dob, comm=[_After(tok)])
    dqp, dkv, dkr = _mla_post(dqf, dkf, dvf, cos, sin, rot_t)
    grad("w_uq", _grp_dw_t("uq_dw", dqp, cq))
    grad("w_ukv", _grp_dw("ukv_dw", ckv, dkv))
    dcq = _grp_dx_t("uq_dx", dqp, w["w_uq"])
    s6 = to_sibling("w_uq", "w_ukv")
    dckv = _grp_dx("ukv_dx", dkv, w["w_ukv"], comm=[s6])
    add_pairs(s6)
    tok = start_chips("mla", "w_uq", "w_ukv")
    dlat, dsp["q_a_norm"], dsp["kv_a_norm"] = _mla_lat_bwd(dcq, dckv, dkr, lat, sp["q_a_norm"], sp["kv_a_norm"],
                                                         cos, sin, rot_t)
    dq_na, dk_na, dv_na, dtab = _na_bwd(qkv, tb, doa, comm=[_After(tok)])
    dsp["na_rpb"] = _na_rpb_grad(dtab, rows)
    dqkv = jnp.concatenate([dq_na, dk_na.astype(BF), dv_na.astype(BF)], axis=1)

    pieces = [dqkv, dlat, dga, dgb]
    dwin = jnp.zeros((sum(pc.shape[1] for pc in pieces), d), BF)
    row0 = 0
    for i, pc in enumerate(pieces):
        dwin = _mm_tn_into("in_dw%d" % i, pc, u, dwin, row0)
        row0 += pc.shape[1]
    grad("w_in", dwin.reshape(NDEV, -1, d))
    s7 = to_sibling("w_in")
    du = _in_proj_bwd_x(pieces, [wqkv, wlat, wga, wgb], comm=[s7])
    add_pairs(s7)
    tok = start_chips("w_in", "w_in")
    dh1, dsp["mix_norm"] = _rms_bwd("mix_dnorm", du, h1, sp["mix_norm"], dh2, comm=[_After(tok)])

    xn, hg, hu, a = ffn1_saved
    dhb = dh1.astype(BF)
    grad("ffn1_w_down", _ffn_bwd_wd("ffn1_dwd", a, dhb))
    s8 = to_sibling("ffn1_w_down")
    dhg, dhu = _ffn_bwd_act("ffn1_dact", dhb, w["ffn1_w_down"], hg, hu, comm=[s8])
    add_pairs(s8)
    tok = start_chips("ffn1_down", "ffn1_w_down")
    dwg, dwu = _ffn_bwd_wup("ffn1_dwup", xn, dhg, dhu, comm=[_After(tok)])
    grad("ffn1_w_gate", dwg)
    grad("ffn1_w_up", dwu)
    s9 = to_sibling("ffn1_w_gate", "ffn1_w_up")
    _comm_only("rs_sibling_ffn1", [s9])
    add_pairs(s9)
    tok = start_chips("ffn1_up", "ffn1_w_gate", "ffn1_w_up")
    dxn = _ffn_bwd_x("ffn1_dx", dhg, dhu, w["ffn1_w_gate"], w["ffn1_w_up"], comm=[_After(tok)])
    dx, dsp["ffn1_norm"] = _rms_bwd("ffn1_dnorm", dxn, x, sp["ffn1_norm"], dh1)
    return loss, dx, pending, dsp


def _gather_small(buf):
    def body(in_ref, out_ref, send_sems, recv_sems, local_sem):
        x, y, c = _coords()
        mine = pltpu.make_async_copy(in_ref, out_ref.at[4 * x + 2 * y + c], local_sem)
        mine.start()
        cps = []
        for k in range(1, NDEV):
            fx, fy, fc = (k >> 2) & 1, (k >> 1) & 1, k & 1
            peer = (x ^ fx, y ^ fy, c ^ fc)
            cps.append(pltpu.make_async_remote_copy(
                src_ref=in_ref, dst_ref=out_ref.at[4 * x + 2 * y + c], send_sem=send_sems.at[k - 1],
                recv_sem=recv_sems.at[k - 1], device_id=peer, device_id_type=MESH))
        for cp in cps:
            cp.start()
        for k in range(1, NDEV):
            fx, fy, fc = (k >> 2) & 1, (k >> 1) & 1, k & 1
            px, py, pc = x ^ fx, y ^ fy, c ^ fc
            pltpu.make_async_remote_copy(
                src_ref=in_ref, dst_ref=out_ref.at[4 * px + 2 * py + pc], send_sem=send_sems.at[k - 1],
                recv_sem=recv_sems.at[k - 1], device_id=(px, py, pc), device_id_type=MESH).wait_recv()
        for cp in cps:
            cp.wait_send()
        mine.wait()

    return pl.pallas_call(
        body, name="gather_small", in_specs=[ANY], out_specs=ANY,
        out_shape=jax.ShapeDtypeStruct((NDEV,) + buf.shape, buf.dtype),
        scratch_shapes=[pltpu.SemaphoreType.DMA((NDEV - 1,)), pltpu.SemaphoreType.DMA((NDEV - 1,)),
                        pltpu.SemaphoreType.DMA],
    )(buf)


def _adam_math(wv, g, m, v):
    m_new = B1 * m + (1.0 - B1) * g
    v_new = B2 * v + (1.0 - B2) * (g * g)
    m_hat = m_new / (1.0 - B1 ** STEP)
    v_hat = v_new / (1.0 - B2 ** STEP)
    return -LR * (m_hat / (jnp.sqrt(v_hat) + ADAM_EPS) + WD * wv), m_new, v_new


def _adam(name, parts, wv, m, v, after=None):
    npart, r, c = parts.shape
    tr, tc = _ew_tile(r, c)

    def body(p_ref, w_ref, m_ref, v_ref, *rest):
        g_ref, d_ref, mo_ref, vo_ref = rest[-4:]
        g = p_ref[0].astype(F32)
        for j in range(1, npart):
            g = g + p_ref[j].astype(F32)
        g_ref[...] = g
        d_ref[...], mo_ref[...], vo_ref[...] = _adam_math(w_ref[...], g, m_ref[...], v_ref[...])

    blk = pl.BlockSpec((tr, tc), lambda i, k: (i, k))
    extra = [] if after is None else [after]
    return pl.pallas_call(
        body, name=name, grid=(r // tr, c // tc),
        in_specs=[pl.BlockSpec((npart, tr, tc), lambda i, k: (0, i, k)), blk, blk, blk] + [ANY] * len(extra),
        out_specs=[blk] * 4, out_shape=[jax.ShapeDtypeStruct((r, c), F32)] * 4, compiler_params=_params(2),
    )(parts, wv, m, v, *extra)


def _adam_exchanged(name, sums, land, wv, m, v, my_chip):
    _, r, c = sums.shape
    tr, tc = _ew_tile(r, c)

    def body(chip_ref, s_ref, l_ref, w_ref, m_ref, v_ref, g_ref, d_ref, mo_ref, vo_ref):
        g = s_ref[...].astype(F32)
        for j in range(3):
            g = g + l_ref[j].astype(F32)
        g_ref[...] = g
        d_ref[...], mo_ref[...], vo_ref[...] = _adam_math(w_ref[...], g, m_ref[...], v_ref[...])

    blk = pl.BlockSpec((tr, tc), lambda i, k, chip_ref: (i, k))
    return pl.pallas_call(
        body, name=name,
        grid_spec=pltpu.PrefetchScalarGridSpec(
            num_scalar_prefetch=1, grid=(r // tr, c // tc),
            in_specs=[pl.BlockSpec((None, tr, tc), lambda i, k, chip_ref: (chip_ref[0], i, k)),
                      pl.BlockSpec((3, tr, tc), lambda i, k, chip_ref: (0, i, k)), blk, blk, blk],
            out_specs=[blk] * 4),
        out_shape=[jax.ShapeDtypeStruct((r, c), F32)] * 4, compiler_params=_params(2),
    )(my_chip, sums, land, wv, m, v)


SHARDED = ("ffn1_w_gate", "ffn1_w_up", "ffn1_w_down", "w_in", "w_uq", "w_ukv", "w_branch_a", "w_branch_b", "w_out",
           "ffn2_w_gate", "ffn2_w_up", "ffn2_w_down", "w_pl", "w_pl_gate")
TRANSPOSED = ("ffn1_w_gate", "ffn1_w_up", "ffn2_w_gate", "ffn2_w_up", "w_in", "w_uq")
REPLICATED = ("ffn1_norm", "mix_norm", "q_a_norm", "kv_a_norm", "na_rpb", "ffn2_norm", "pl_norm", "final_norm")
WEIGHTS = ("ffn1_norm", "ffn1_w_gate", "ffn1_w_up", "ffn1_w_down", "mix_norm", "w_in", "q_a_norm", "w_uq",
           "kv_a_norm", "w_ukv", "na_rpb", "w_branch_a", "w_branch_b", "w_out", "ffn2_norm", "ffn2_w_gate",
           "ffn2_w_up", "ffn2_w_down", "pl_norm", "w_pl", "w_pl_gate", "final_norm")
SMALL_W = 2048


def _pack_small(vals):
    rows = []
    for name in REPLICATED:
        flat = vals[name].reshape(-1).astype(F32)
        n = -(-flat.shape[0] // SMALL_W) * SMALL_W
        rows.append(jnp.pad(flat, (0, n - flat.shape[0])).reshape(-1, SMALL_W))
    return jnp.concatenate(rows, axis=0)


def _unpack_small(buf, shapes):
    out, r = {}, 0
    for name in REPLICATED:
        size = int(np.prod(shapes[name]))
        nrow = -(-size // SMALL_W)
        out[name] = buf[r:r + nrow].reshape(-1)[:size].reshape(shapes[name])
        r += nrow
    return out


def kernel(x, p, ffn1_norm, ffn1_w_gate, ffn1_w_up, ffn1_w_down, mix_norm, w_in, q_a_norm, w_uq, kv_a_norm, w_ukv, na_rpb, w_branch_a, w_branch_b, w_out, ffn2_norm, ffn2_w_gate, ffn2_w_up, ffn2_w_down, pl_norm, w_pl, w_pl_gate, final_norm, loss_target, m_ffn1_norm, m_ffn1_w_gate, m_ffn1_w_up, m_ffn1_w_down, m_mix_norm, m_w_in, m_q_a_norm, m_w_uq, m_kv_a_norm, m_w_ukv, m_na_rpb, m_w_branch_a, m_w_branch_b, m_w_out, m_ffn2_norm, m_ffn2_w_gate, m_ffn2_w_up, m_ffn2_w_down, m_pl_norm, m_w_pl, m_w_pl_gate, m_final_norm, v_ffn1_norm, v_ffn1_w_gate, v_ffn1_w_up, v_ffn1_w_down, v_mix_norm, v_w_in, v_q_a_norm, v_w_uq, v_kv_a_norm, v_w_ukv, v_na_rpb, v_w_branch_a, v_w_branch_b, v_w_out, v_ffn2_norm, v_ffn2_w_gate, v_ffn2_w_up, v_ffn2_w_down, v_pl_norm, v_w_pl, v_w_pl_gate, v_final_norm):
    args = dict(locals())
    wts = {n: args[n] for n in WEIGHTS}
    mom = {n: args["m_" + n] for n in WEIGHTS}
    var = {n: args["v_" + n] for n in WEIGHTS}
    shapes = {n: wts[n].shape for n in WEIGHTS}
    core = lax.axis_index("c").astype(jnp.int32).reshape(1)

    local = lambda n, a: a[0].T if n in TRANSPOSED else a[0]
    own = {n: local(n, wts[n]).astype(BF) for n in SHARDED}
    sp = {n: wts[n].reshape(1, -1) for n in REPLICATED if n != "na_rpb"}
    sp["na_rpb"] = wts["na_rpb"][0]
    loss_part, grad_x, pending, dsp = _device_step(x[0], p[0, 0], loss_target[0], sp, own, core)

    out = {}
    last = grad_x
    my_chip = (2 * lax.axis_index("x") + lax.axis_index("y")).astype(jnp.int32).reshape(1)
    for tag, names, send, recv, thru, lands in pending:
        thru, lands = _chips_wait("rs_wait_" + tag, send, recv, thru, lands, last)
        for n, s4, l3 in zip(names, thru, lands):
            res4 = _adam_exchanged("adam_" + n, s4, l3, local(n, wts[n]), local(n, mom[n]), local(n, var[n]), my_chip)
            out[n] = tuple((a.T if n in TRANSPOSED else a)[None] for a in res4)
            last = res4[1]

    small = jnp.concatenate([_pack_small(dsp), jnp.pad(loss_part, ((0, 0), (0, SMALL_W - loss_part.shape[1])))], 0)
    pad_rows = -small.shape[0] % 8
    small = jnp.pad(small, ((0, pad_rows), (0, 0)))
    every = _gather_small(small)
    zeros = jnp.zeros((1 + pad_rows, SMALL_W), F32)
    pack = lambda d: jnp.concatenate([_pack_small(d), zeros], 0)
    g_s, d_s, m_s, v_s = _adam("adam_small", every, pack(wts), pack(mom), pack(var))
    n_rows = small.shape[0] - 1 - pad_rows
    loss = g_s[n_rows, 0]
    small_out = [_unpack_small(b, shapes) for b in (g_s, d_s, m_s, v_s)]
    for n in REPLICATED:
        out[n] = tuple(b[n] for b in small_out)

    res = [loss, grad_x[None]]
    for k in range(4):
        res += [out[n][k] for n in WEIGHTS]
    return tuple(res)
```

```python
import functools

import numpy as np
import jax
import jax.numpy as jnp
from jax import lax
from jax.experimental import pallas as pl
from jax.experimental.pallas import tpu as pltpu

F32 = jnp.float32
BF = jnp.bfloat16
MESH = pl.DeviceIdType.MESH

NDEV = 8
NCHIP = 4
VMEM_LIMIT = 56 * 1024 * 1024
EPS = 1e-6
NEG = -1e30
GRID_W = 64
NA_HEADS, NA_DIM = 8, 128
NA_ROWS_WIN, NA_COLS_WIN = 8, 16
NA_HG = 4
NA_QROWS = 4
ML_HEADS, ML_NOPE, ML_ROPE, ML_V = 8, 128, 64, 128
ML_QK = ML_NOPE + ML_ROPE
ML_RANK = 512
ROPE_THETA = 10000.0
LR, B1, B2, ADAM_EPS, WD, STEP = 0.001, 0.9, 0.999, 1e-08, 0.01, 10
HI = lax.Precision.HIGHEST

_DN = {"nn": (((1,), (0,)), ((), ())), "nt": (((1,), (1,)), ((), ())), "tn": (((0,), (0,)), ((), ()))}


def _params(n):
    return pltpu.CompilerParams(dimension_semantics=("arbitrary",) * n, vmem_limit_bytes=VMEM_LIMIT)


def _sig(v):
    return jax.nn.sigmoid(v)


ANY = pl.BlockSpec(memory_space=pl.ANY)


def _coords():
    return lax.axis_index("x"), lax.axis_index("y"), lax.axis_index("c")


class _Part:
    inputs, out_shapes, sem_shapes, results = (), (), (), None

    def mid(self, ins, outs, sems):
        pass

    def late(self, ins, outs, sems):
        pass


class _After(_Part):
    def __init__(self, token):
        self.inputs = [token]

    def start(self, ins, outs, sems):
        pass

    finish = start


class _GatherPart(_Part):
    def __init__(self, names, shards):
        n = len(shards)
        self.names, self.inputs = list(names), list(shards)
        self.out_shapes = [jax.ShapeDtypeStruct((NDEV,) + a.shape, a.dtype) for a in shards]
        self.sem_shapes = [pltpu.SemaphoreType.DMA((n, 7)), pltpu.SemaphoreType.DMA((n, 7)),
                           pltpu.SemaphoreType.DMA((n,))]

    def _plan(self, ins, outs, sems):
        send_sems, recv_sems, local_sems = sems
        x, y, c = _coords()
        me, sib, diag = (x, y, c), (x, y, 1 - c), (1 - x, 1 - y, c)
        n1, n2 = (x ^ (1 - c), y ^ c, c), (x ^ c, y ^ (1 - c), c)

        def copy(i, k, block, to, src=None):
            px, py, pc = block
            dst = outs[i].at[4 * px + 2 * py + pc]
            return pltpu.make_async_remote_copy(
                src_ref=dst if src is None else src, dst_ref=dst, send_sem=send_sems.at[i, k],
                recv_sem=recv_sems.at[i, k], device_id=to, device_id_type=MESH)

        mine = [pltpu.make_async_copy(ins[i], outs[i].at[4 * x + 2 * y + c], local_sems.at[i])
                for i in range(len(ins))]
        return copy, mine, me, sib, n1, n2, diag

    def _own_sends(self, ins, copy, me, sib, n1, n2):
        return [copy(i, k, me, to, src=ins[i]) for i in range(len(ins)) for k, to in enumerate((sib, n1, n2))]

    def start(self, ins, outs, sems):
        copy, mine, me, sib, n1, n2, _ = self._plan(ins, outs, sems)
        for cp in mine + self._own_sends(ins, copy, me, sib, n1, n2):
            cp.start()

    def mid(self, ins, outs, sems):
        copy, _, me, sib, n1, n2, _ = self._plan(ins, outs, sems)
        for i in range(len(ins)):
            copy(i, 1, n1, me).wait_recv()
            copy(i, 3, n1, n2).start()
            copy(i, 4, n1, sib).start()

    def late(self, ins, outs, sems):
        copy, _, me, sib, _, n2, diag = self._plan(ins, outs, sems)
        for i in range(len(ins)):
            copy(i, 2, n2, me).wait_recv()
            copy(i, 5, n2, sib).start()
        for i in range(len(ins)):
            copy(i, 3, diag, me).wait_recv()
            copy(i, 6, diag, sib).start()

    def finish(self, ins, outs, sems):
        copy, mine, me, sib, n1, n2, diag = self._plan(ins, outs, sems)
        other = lambda dev: (dev[0], dev[1], sib[2])
        n = len(ins)
        for i in range(n):
            copy(i, 0, sib, me).wait_recv()
            for k, block in ((4, other(n2)), (5, other(n1)), (6, other(diag))):
                copy(i, k, block, me).wait_recv()
        for cp in self._own_sends(ins, copy, me, sib, n1, n2):
            cp.wait_send()
        for i in range(n):
            for k, block in ((3, n1), (4, n1), (5, n2), (6, diag)):
                copy(i, k, block, sib).wait_send()
        for cp in mine:
            cp.wait()


class _SiblingPart(_Part):
    def __init__(self, names, parts):
        n = len(parts)
        self.names, self.inputs = list(names), list(parts)
        self.out_shapes = [jax.ShapeDtypeStruct((NCHIP,) + a.shape[2:], a.dtype) for a in parts]
        self.sem_shapes = [pltpu.SemaphoreType.DMA((n,)), pltpu.SemaphoreType.DMA((n,))]

    def _copies(self, ins, outs, sems):
        x, y, c = _coords()
        return [pltpu.make_async_remote_copy(
            src_ref=ins[i].at[:, 1 - c], dst_ref=outs[i], send_sem=sems[0].at[i], recv_sem=sems[1].at[i],
            device_id=(x, y, 1 - c), device_id_type=MESH) for i in range(len(ins))]

    def start(self, ins, outs, sems):
        for cp in self._copies(ins, outs, sems):
            cp.start()

    def finish(self, ins, outs, sems):
        cps = self._copies(ins, outs, sems)
        for cp in cps:
            cp.wait_recv()
        for cp in cps:
            cp.wait_send()


HBM = pl.BlockSpec(memory_space=pltpu.HBM)
SEM = pl.BlockSpec(memory_space=pltpu.SEMAPHORE)


def _chip_peers():
    x, y, c = _coords()
    return [(1 - x, y, c), (x, 1 - y, c), (1 - x, 1 - y, c)]


def _chips_start(name, sums):
    n = len(sums)

    def body(*refs):
        ins, lands, send_sems, recv_sems = refs[:n], refs[n:2 * n], refs[2 * n], refs[2 * n + 1]
        for i in range(n):
            for k, (px, py, pc) in enumerate(_chip_peers()):
                pltpu.make_async_remote_copy(
                    src_ref=ins[i].at[2 * px + py], dst_ref=lands[i].at[k], send_sem=send_sems.at[3 * i + k],
                    recv_sem=recv_sems.at[3 * i + k], device_id=(px, py, pc), device_id_type=MESH).start()
        refs[-1][...] = jnp.zeros_like(refs[-1])

    lands = [lax.empty((3,) + a.shape[1:], a.dtype) for a in sums]
    bufs = list(sums) + lands
    res = pl.pallas_call(
        body, name=name, in_specs=[HBM] * (2 * n),
        out_specs=(SEM, SEM, *[HBM] * (2 * n), pl.BlockSpec(memory_space=pltpu.VMEM)),
        out_shape=(pltpu.SemaphoreType.DMA((3 * n,)), pltpu.SemaphoreType.DMA((3 * n,)),
                   *[pltpu.HBM(a.shape, a.dtype) for a in bufs], jax.ShapeDtypeStruct((8, 128), F32)),
        input_output_aliases={i: 2 + i for i in range(2 * n)},
        compiler_params=pltpu.CompilerParams(has_side_effects=pltpu.SideEffectType.DATAFLOW_SIDE_EFFECTING),
    )(*[pltpu.with_memory_space_constraint(a, pltpu.HBM) for a in bufs])
    return res[0], res[1], list(res[2:2 + n]), list(res[2 + n:2 + 2 * n]), res[-1]


def _chips_wait(name, send_sems, recv_sems, sums, lands, after):
    n = len(sums)

    def body(*refs):
        ins, zones, send, recv = refs[:n], refs[n:2 * n], refs[2 * n], refs[2 * n + 1]
        for i in range(n):
            for k, peer in enumerate(_chip_peers()):
                cp = pltpu.make_async_remote_copy(
                    src_ref=ins[i].at[0], dst_ref=zones[i].at[k], send_sem=send.at[3 * i + k],
                    recv_sem=recv.at[3 * i + k],
                    device_id=peer, device_id_type=MESH)
                cp.wait_send()
                cp.wait_recv()

    bufs = list(sums) + list(lands)
    res = pl.pallas_call(
        body, name=name, in_specs=[HBM] * (2 * n) + [SEM, SEM, ANY], out_specs=[HBM] * (2 * n),
        out_shape=[pltpu.HBM(a.shape, a.dtype) for a in bufs], input_output_aliases={i: i for i in range(2 * n)},
        compiler_params=pltpu.CompilerParams(has_side_effects=pltpu.SideEffectType.DATAFLOW_SIDE_EFFECTING),
    )(*bufs, send_sems, recv_sems, after)
    return list(res[:n]), list(res[n:])


def _call(name, body, grid, in_specs, out_specs, out_shape, args, comm=(), scratch=()):
    comm = [p for p in comm if p is not None]
    single = not isinstance(out_shape, (list, tuple))
    o_specs = [out_specs] if single else list(out_specs)
    o_shape = [out_shape] if single else list(out_shape)
    n_in, n_out = len(in_specs), len(o_specs)
    c_in = [a for p in comm for a in p.inputs]
    c_out = [s for p in comm for s in p.out_shapes]
    c_sem = [s for p in comm for s in p.sem_shapes]

    def wrapped(*refs):
        ins, outs = refs[:n_in], refs[n_in + len(c_in):n_in + len(c_in) + n_out]
        pos = [n_in, n_in + len(c_in) + n_out, n_in + len(c_in) + n_out + len(c_out)]
        own = refs[pos[2]:pos[2] + len(scratch)]
        pos[2] += len(scratch)
        split = []
        for p in comm:
            sizes = [len(p.inputs), len(p.out_shapes), len(p.sem_shapes)]
            split.append([refs[o:o + n] for o, n in zip(pos, sizes)])
            pos = [o + n for o, n in zip(pos, sizes)]
        step, steps = 0, 1
        for a, g in enumerate(grid):
            step, steps = step * g + pl.program_id(a), steps * g

        def run(which, at):
            def go():
                for p, cut in zip(comm, split):
                    getattr(p, which)(*cut)
            if not comm:
                return
            if grid:
                pl.when(step == at)(go)
            else:
                go()

        run("start", 0)
        body(*ins, *outs, *own)
        run("mid", steps // 2)
        run("late", max(steps // 2, steps - 1 - max(1, steps // 8)))
        run("finish", steps - 1)

    res = pl.pallas_call(
        wrapped, name=name, grid=grid, in_specs=list(in_specs) + [ANY] * len(c_in),
        out_specs=o_specs + [ANY] * len(c_out), out_shape=o_shape + c_out, scratch_shapes=list(scratch) + c_sem,
        compiler_params=_params(len(grid)),
    )(*args, *c_in)
    pos = n_out
    for p in comm:
        p.results = list(res[pos:pos + len(p.out_shapes)])
        pos += len(p.out_shapes)
    return res[0] if single else list(res[:n_out])


def _comm_only(name, comm):
    def body(o_ref):
        o_ref[...] = jnp.zeros_like(o_ref)

    _call(name, body, (), [], pl.BlockSpec(memory_space=pltpu.VMEM), jax.ShapeDtypeStruct((8, 128), F32), [], comm)


def _mm(name, grid, prods, extras, outs, epi, nacc=1, comm=()):
    n_p, n_e = len(prods), len(extras)

    def body(*refs):
        ab, ex, out = refs[:2 * n_p], refs[2 * n_p:2 * n_p + n_e], refs[2 * n_p + n_e:]
        accs = [None] * nacc
        for i, prod in enumerate(prods):
            dn, acc, loop = prod[6], prod[7], prod[8]
            a_ref, b_ref = ab[2 * i], ab[2 * i + 1]
            if loop:
                for g in range(loop):
                    t = lax.dot_general(a_ref[g], b_ref[g], _DN[dn], preferred_element_type=F32)
                    accs[acc] = t if accs[acc] is None else accs[acc] + t
            else:
                t = lax.dot_general(a_ref[...], b_ref[...], _DN[dn], preferred_element_type=F32)
                accs[acc] = t if accs[acc] is None else accs[acc] + t
        epi(accs, ex, out)

    in_specs, args = [], []
    for prod in prods:
        in_specs += [pl.BlockSpec(prod[1], prod[2]), pl.BlockSpec(prod[4], prod[5])]
        args += [prod[0], prod[3]]
    for e, e_blk, e_map in extras:
        in_specs.append(pl.BlockSpec(e_blk, e_map))
        args.append(e)
    return _call(name, body, grid, in_specs, [pl.BlockSpec(blk, mp) for _, _, blk, mp in outs],
                 [jax.ShapeDtypeStruct(s, d) for s, d, _, _ in outs], args, comm)


def _store(accs, ex, out):
    out[0][...] = accs[0].astype(out[0].dtype)


def _ew_tile(r, c, budget=3 << 19):
    for t in range(r - r % 16, 0, -16):
        if r % t == 0 and t * c * 4 <= budget:
            return t, c
    for t in range(c - c % 128, 0, -128):
        if c % t == 0 and r * t * 4 <= budget:
            return r, t
    return r, c


def _tile(n, want):
    t = min(n, want)
    assert n % t == 0, (n, want)
    return t


def _mm_nn(name, a, b, out_dtype, tm=512, tn=512, comm=()):
    m, k = a.shape
    n = b.shape[1]
    tm, tn = _tile(m, tm), (tn if n % tn == 0 else n)
    return _mm(name, (n // tn, m // tm),
               [(a, (tm, k), lambda j, i: (i, 0), b, (k, tn), lambda j, i: (0, j), "nn", 0, 0)], [],
               [((m, n), out_dtype, (tm, tn), lambda j, i: (i, j))], _store, comm=comm)[0]


def _mm_nt(name, a, bt, out_dtype, tm=512, tn=512, comm=()):
    m, k = a.shape
    n = bt.shape[0]
    tm, tn = _tile(m, tm), (tn if n % tn == 0 else n)
    return _mm(name, (n // tn, m // tm),
               [(a, (tm, k), lambda j, i: (i, 0), bt, (tn, k), lambda j, i: (j, 0), "nt", 0, 0)], [],
               [((m, n), out_dtype, (tm, tn), lambda j, i: (i, j))], _store, comm=comm)[0]


def _mm_tn_into(name, a, b, buf, row0, ta=1024, tb=512):
    t, ka = a.shape
    nb = b.shape[1]
    ta, tb = (ta if ka % ta == 0 else ka), (tb if nb % tb == 0 else nb)

    def body(a_ref, b_ref, buf_in, buf_out, tile, sem):
        i, j = pl.program_id(0), pl.program_id(1)
        tile[...] = lax.dot_general(a_ref[...], b_ref[...], _DN["tn"], preferred_element_type=F32).astype(tile.dtype)
        rows = pl.ds(pl.multiple_of(row0 + i * ta, 16), ta)
        cp = pltpu.make_async_copy(tile, buf_out.at[rows, pl.ds(pl.multiple_of(j * tb, 128), tb)], sem)
        cp.start()
        cp.wait()

    return pl.pallas_call(
        body, name=name, grid=(ka // ta, nb // tb),
        in_specs=[pl.BlockSpec((t, ta), lambda i, j: (0, i)), pl.BlockSpec((t, tb), lambda i, j: (0, j)), ANY],
        out_specs=ANY, out_shape=jax.ShapeDtypeStruct(buf.shape, buf.dtype), input_output_aliases={2: 0},
        scratch_shapes=[pltpu.VMEM((ta, tb), buf.dtype), pltpu.SemaphoreType.DMA],
        compiler_params=_params(2))(a, b, buf)


def _mm_tn(name, a, b, out_dtype, ta=512, tb=512, scale=None):
    t, ka = a.shape
    nb = b.shape[1]
    ta, tb = (ta if ka % ta == 0 else ka), (tb if nb % tb == 0 else nb)

    def epi(accs, ex, out):
        v = accs[0] if scale is None else accs[0] * scale
        out[0][...] = v.astype(out[0].dtype)

    return _mm(name, (ka // ta, nb // tb),
               [(a, (t, ta), lambda i, j: (0, i), b, (t, tb), lambda i, j: (0, j), "tn", 0, 0)], [],
               [((ka, nb), out_dtype, (ta, tb), lambda i, j: (i, j))], epi)[0]


def _rms_fwd(name, x, g, tm=256, comm=()):
    s, d = x.shape
    tm = _tile(s, tm)

    def body(x_ref, g_ref, o_ref):
        v = x_ref[...]
        o_ref[...] = (v * lax.rsqrt(jnp.mean(v * v, axis=-1, keepdims=True) + EPS) * g_ref[...]).astype(o_ref.dtype)

    return _call(name, body, (s // tm,),
                 [pl.BlockSpec((tm, d), lambda i: (i, 0)), pl.BlockSpec((1, d), lambda i: (0, 0))],
                 pl.BlockSpec((tm, d), lambda i: (i, 0)), jax.ShapeDtypeStruct((s, d), BF), [x, g], comm)


def _acc_rows(ref, part, i):
    @pl.when(i == 0)
    def _():
        ref[...] = part

    @pl.when(i > 0)
    def _():
        ref[...] += part


def _rms_bwd_math(dn, v, g):
    rstd = lax.rsqrt(jnp.mean(v * v, axis=-1, keepdims=True) + EPS)
    xh = v * rstd
    dxh = dn * g
    dx = rstd * (dxh - xh * jnp.mean(dxh * xh, axis=-1, keepdims=True))
    return dx, jnp.sum(dn * xh, axis=0, keepdims=True)


def _rms_bwd(name, dn, x, g, resid, tm=256, comm=()):
    s, d = x.shape
    tm = _tile(s, tm)

    def body(dn_ref, x_ref, g_ref, r_ref, dx_ref, dg_ref):
        dx, part = _rms_bwd_math(dn_ref[...].astype(F32), x_ref[...], g_ref[...])
        dx_ref[...] = r_ref[...] + dx
        _acc_rows(dg_ref, part, pl.program_id(0))

    row = pl.BlockSpec((tm, d), lambda i: (i, 0))
    one = pl.BlockSpec((1, d), lambda i: (0, 0))
    return _call(name, body, (s // tm,), [row, row, one, row], [row, one],
                 [jax.ShapeDtypeStruct((s, d), F32), jax.ShapeDtypeStruct((1, d), F32)], [dn, x, g, resid], comm)


def _loss_head(h, target, g, tm=256):
    s, d = h.shape
    tm = _tile(s, tm)

    def body(h_ref, t_ref, g_ref, dh_ref, dg_ref, loss_ref):
        v, gv = h_ref[...], g_ref[...]
        rstd = lax.rsqrt(jnp.mean(v * v, axis=-1, keepdims=True) + EPS)
        xh = v * rstd
        err = xh * gv - t_ref[...]
        part_loss = 0.5 * jnp.sum(jnp.mean(err * err, axis=-1, keepdims=True), axis=0, keepdims=True)
        dy = err * (1.0 / d)
        dxh = dy * gv
        dh_ref[...] = rstd * (dxh - xh * jnp.mean(dxh * xh, axis=-1, keepdims=True))
        i = pl.program_id(0)
        _acc_rows(dg_ref, jnp.sum(dy * xh, axis=0, keepdims=True), i)
        _acc_rows(loss_ref, jnp.broadcast_to(part_loss, loss_ref.shape), i)

    row = pl.BlockSpec((tm, d), lambda i: (i, 0))
    one = pl.BlockSpec((1, d), lambda i: (0, 0))
    return pl.pallas_call(
        body, name="loss_head", grid=(s // tm,), in_specs=[row, row, one],
        out_specs=[row, one, pl.BlockSpec((1, 128), lambda i: (0, 0))],
        out_shape=[jax.ShapeDtypeStruct((s, d), F32), jax.ShapeDtypeStruct((1, d), F32),
                   jax.ShapeDtypeStruct((1, 128), F32)],
        compiler_params=_params(1))(h, target, g)


def _pl_bwd_elem(dh, pe, t, tm=256):
    s, d = dh.shape
    tm = _tile(s, tm)

    def body(dh_ref, pe_ref, t_ref, dt_ref, dpe_ref):
        dh_v, sg = dh_ref[...], _sig(t_ref[...])
        dt_ref[...] = (dh_v * pe_ref[...].astype(F32) * sg * (1.0 - sg)).astype(BF)
        dpe_ref[...] = (dh_v * sg).astype(BF)

    row = pl.BlockSpec((tm, d), lambda i: (i, 0))
    return pl.pallas_call(
        body, name="pl_bwd_elem", grid=(s // tm,), in_specs=[row, row, row], out_specs=[row, row],
        out_shape=[jax.ShapeDtypeStruct((s, d), BF)] * 2, compiler_params=_params(1))(dh, pe, t)


def _ffn_up(name, xn, wg, wu, tm=1024, comm=()):
    s, d = xn.shape
    g, fb, _ = wg.shape
    tm = _tile(s, tm)

    def epi(accs, ex, out):
        hg, hu = accs
        out[0][...] = hg.astype(BF)
        out[1][...] = hu.astype(BF)
        out[2][...] = (hg * _sig(hg) * hu).astype(BF)

    a_map = lambda j, i: (i, 0)
    w_map = lambda j, i: (j, 0, 0)
    o = ((g, s, fb), BF, (None, tm, fb), lambda j, i: (j, i, 0))
    return _mm(name, (g, s // tm),
               [(xn, (tm, d), a_map, wg, (None, fb, d), w_map, "nt", 0, 0),
                (xn, (tm, d), a_map, wu, (None, fb, d), w_map, "nt", 1, 0)], [], [o, o, o], epi, nacc=2, comm=comm)


def _ffn_down(name, a, wd, resid, tm=512, tn=512, comm=()):
    g, s, fb = a.shape
    d = wd.shape[2]
    tm, tn = _tile(s, tm), _tile(d, tn)

    def epi(accs, ex, out):
        out[0][...] = ex[0][...] + 0.5 * accs[0]

    return _mm(name, (d // tn, s // tm),
               [(a, (g, tm, fb), lambda j, i: (0, i, 0), wd, (g, fb, tn), lambda j, i: (0, 0, j), "nn", 0, g)],
               [(resid, (tm, tn), lambda j, i: (i, j))],
               [((s, d), F32, (tm, tn), lambda j, i: (i, j))], epi, comm=comm)[0]


def _ffn_bwd_act(name, dh, wd, hg, hu, tm=1024, comm=()):
    s, d = dh.shape
    g, fb, _ = wd.shape
    tm = _tile(s, tm)

    def epi(accs, ex, out):
        da = 0.5 * accs[0]
        hg_v, hu_v = ex[0][...].astype(F32), ex[1][...].astype(F32)
        sg = _sig(hg_v)
        out[0][...] = (da * hu_v * (sg * (1.0 + hg_v * (1.0 - sg)))).astype(BF)
        out[1][...] = (da * (hg_v * sg)).astype(BF)

    blk = (None, tm, fb)
    gmap = lambda j, i: (j, i, 0)
    return _mm(name, (g, s // tm),
               [(dh, (tm, d), lambda j, i: (i, 0), wd, (None, fb, d), lambda j, i: (j, 0, 0), "nt", 0, 0)],
               [(hg, blk, gmap), (hu, blk, gmap)],
               [((g, s, fb), BF, blk, gmap), ((g, s, fb), BF, blk, gmap)], epi, comm=comm)


def _ffn_bwd_wd(name, a, dh, tn=1024, comm=()):
    g, s, fb = a.shape
    d = dh.shape[1]
    tn = _tile(d, tn)

    def epi(accs, ex, out):
        out[0][...] = (0.5 * accs[0]).astype(BF)

    return _mm(name, (g, d // tn),
               [(a, (None, s, fb), lambda j, i: (j, 0, 0), dh, (s, tn), lambda j, i: (0, i), "tn", 0, 0)], [],
               [((g, fb, d), BF, (None, fb, tn), lambda j, i: (j, 0, i))], epi, comm=comm)[0]


def _ffn_bwd_wup(name, xn, dhg, dhu, tk=1024, comm=()):
    s, d = xn.shape
    g, _, fb = dhg.shape
    tk = _tile(d, tk)

    def epi(accs, ex, out):
        out[0][...] = accs[0].astype(BF)
        out[1][...] = accs[1].astype(BF)

    a_map = lambda j, i: (j, 0, 0)
    b_map = lambda j, i: (0, i)
    o = ((g, fb, d), BF, (None, fb, tk), lambda j, i: (j, 0, i))
    return _mm(name, (g, d // tk),
               [(dhg, (None, s, fb), a_map, xn, (s, tk), b_map, "tn", 0, 0),
                (dhu, (None, s, fb), a_map, xn, (s, tk), b_map, "tn", 1, 0)], [], [o, o], epi, nacc=2, comm=comm)


def _ffn_bwd_x(name, dhg, dhu, wg, wu, tm=512, tn=512, comm=()):
    g, s, fb = dhg.shape
    d = wg.shape[2]
    tm, tn = _tile(s, tm), _tile(d, tn)
    a_blk, a_map = (g, tm, fb), lambda j, i: (0, i, 0)
    b_blk, b_map = (g, fb, tn), lambda j, i: (0, 0, j)
    return _mm(name, (d // tn, s // tm),
               [(dhg, a_blk, a_map, wg, b_blk, b_map, "nn", 0, g), (dhu, a_blk, a_map, wu, b_blk, b_map, "nn", 0, g)],
               [], [((s, d), F32, (tm, tn), lambda j, i: (i, j))], _store, comm=comm)[0]


def _ffn_forward(tag, h, gain, get_wgu, get_wd, norm_comm=(), up_comm=(), down_comm=()):
    xn = _rms_fwd(tag + "_norm", h, gain, comm=norm_comm)
    wg, wu = get_wgu()
    hg, hu, a = _ffn_up(tag + "_up", xn, wg, wu, comm=up_comm)
    return _ffn_down(tag + "_down", a, get_wd(), h, comm=down_comm), (xn, hg, hu, a)


def _na_geometry(rows):
    kh = min(NA_ROWS_WIN, rows)
    cols = np.arange(GRID_W)
    col_start = np.clip(cols - NA_COLS_WIN // 2, 0, GRID_W - NA_COLS_WIN)
    mask = (cols[None, :] >= col_start[:, None]) & (cols[None, :] < col_start[:, None] + NA_COLS_WIN)
    dc = np.clip(cols[None, :] - cols[:, None], -(NA_COLS_WIN - 1), NA_COLS_WIN - 1) + (NA_COLS_WIN - 1)
    return kh, mask, dc


def _na_table(rpb, rows):
    _, mask, dc = _na_geometry(rows)
    nd, nc, cells = 2 * NA_ROWS_WIN - 1, 2 * NA_COLS_WIN - 1, GRID_W * GRID_W
    onehot = np.zeros((128, cells), np.float32)
    onehot[dc.reshape(-1), np.arange(cells)] = mask.reshape(-1).astype(np.float32)
    off = np.where(mask.reshape(1, -1), 0.0, NEG).astype(np.float32)

    def body(r_ref, e_ref, off_ref, o_ref):
        o_ref[...] = jnp.dot(r_ref[...], e_ref[...], precision=HI, preferred_element_type=F32) + off_ref[...]

    flat = pl.pallas_call(body, name="na_table", out_shape=jax.ShapeDtypeStruct((NA_HEADS * nd, cells), F32),
                          compiler_params=_params(0))(
        jnp.pad(rpb.reshape(NA_HEADS * nd, nc), ((0, 0), (0, 128 - nc))), jnp.asarray(onehot), jnp.asarray(off))
    return flat.reshape(NA_HEADS, nd, GRID_W, GRID_W)


class _NaPlan:
    def __init__(self, s):
        self.s, self.rows = s, s // GRID_W
        self.kh = min(NA_ROWS_WIN, self.rows)
        self.qr = min(NA_QROWS, self.rows)
        self.kr = min(self.rows, self.kh + self.qr - 1)
        self.groups = self.rows // self.qr
        self.nd = 2 * NA_ROWS_WIN - 1
        self.hw, self.nq = NA_HG * NA_DIM, NA_HEADS // NA_HG
        clip = lambda v, hi: min(max(v, 0), hi)
        pats = [(clip(g * self.qr - self.kh // 2, self.rows - self.kr) - g * self.qr,)
                + tuple(clip(g * self.qr + a - self.kh // 2, self.rows - self.kh) - g * self.qr for a in range(self.qr))
                for g in range(self.groups)]
        self.rebuild = [g for g in range(self.groups) if g == 0 or pats[g] != pats[g - 1]]

    def first_key_row(self, g):
        return jnp.clip(g * self.qr - self.kh // 2, 0, self.rows - self.kr)

    def specs(self):
        blk = pl.BlockSpec((self.qr * GRID_W, self.hw), lambda j, g: (g, j))
        k_spec = pl.BlockSpec((self.s, self.hw), lambda j, g: (0, self.nq + j))
        v_spec = pl.BlockSpec((self.s, self.hw), lambda j, g: (0, 2 * self.nq + j))
        t_spec = pl.BlockSpec((NA_HG, self.nd, GRID_W, GRID_W), lambda j, g: (j, 0, 0, 0))
        return blk, k_spec, v_spec, t_spec

    def bias_scratch(self):
        return pltpu.VMEM((NA_HG, self.qr * GRID_W, self.kr * GRID_W), F32)

    def fill_bias(self, t_ref, bias_ref, g):
        def build():
            r0, ks = g * self.qr, self.first_key_row(g)
            for a in range(self.qr):
                rs = jnp.clip(r0 + a - self.kh // 2, 0, self.rows - self.kh)
                for i in range(self.kr):
                    valid = jnp.logical_and(ks + i >= rs, ks + i < rs + self.kh)
                    idx = jnp.clip(ks + i - r0 - a + NA_ROWS_WIN - 1, 0, self.nd - 1)
                    for h in range(NA_HG):
                        bias_ref[h, a * GRID_W:(a + 1) * GRID_W, i * GRID_W:(i + 1) * GRID_W] = jnp.where(
                            valid, t_ref[h, idx], NEG)

        pl.when(functools.reduce(jnp.logical_or, [g == r for r in self.rebuild]))(build)

    def window(self, g):
        return pl.ds(pl.multiple_of(self.first_key_row(g) * GRID_W, GRID_W), self.kr * GRID_W)


def _na_probs(q, k, bias):
    sc = lax.dot_general(q, k, _DN["nt"], preferred_element_type=F32) * (NA_DIM ** -0.5) + bias
    e = jnp.exp(sc - jnp.max(sc, axis=-1, keepdims=True))
    return e / jnp.sum(e, axis=-1, keepdims=True)


def _na_fwd(qkv, table, comm=()):
    plan = _NaPlan(qkv.shape[0])
    blk, k_spec, v_spec, t_spec = plan.specs()

    def body(q_ref, k_ref, v_ref, t_ref, o_ref, bias_ref):
        g = pl.program_id(1)
        plan.fill_bias(t_ref, bias_ref, g)
        win = plan.window(g)
        for h in range(NA_HG):
            cs = slice(h * NA_DIM, (h + 1) * NA_DIM)
            p = _na_probs(q_ref[:, cs], k_ref[win, cs], bias_ref[h])
            o_ref[:, cs] = jnp.dot(p.astype(BF), v_ref[win, cs], preferred_element_type=F32).astype(BF)

    return _call("na_fwd", body, (plan.nq, plan.groups), [blk, k_spec, v_spec, t_spec], blk,
                 jax.ShapeDtypeStruct((plan.s, NA_HEADS * NA_DIM), BF), [qkv, qkv, qkv, table], comm,
                 scratch=[plan.bias_scratch()])


def _na_bwd(qkv, table, do, comm=()):
    plan = _NaPlan(qkv.shape[0])
    blk, k_spec, v_spec, t_spec = plan.specs()
    qr, kr = plan.qr, plan.kr

    def body(q_ref, k_ref, v_ref, t_ref, do_ref, dq_ref, dk_ref, dv_ref, dt_ref, bias_ref):
        g = pl.program_id(1)

        @pl.when(g == 0)
        def _():
            dk_ref[...] = jnp.zeros_like(dk_ref)
            dv_ref[...] = jnp.zeros_like(dv_ref)
            dt_ref[...] = jnp.zeros_like(dt_ref)

        plan.fill_bias(t_ref, bias_ref, g)
        win = plan.window(g)
        base = plan.first_key_row(g) - g * qr + NA_ROWS_WIN - 1
        for h in range(NA_HG):
            cs = slice(h * NA_DIM, (h + 1) * NA_DIM)
            q, k, v, do_h = q_ref[:, cs], k_ref[win, cs], v_ref[win, cs], do_ref[:, cs]
            p = _na_probs(q, k, bias_ref[h])
            dp = lax.dot_general(do_h, v, _DN["nt"], preferred_element_type=F32)
            ds = p * (dp - jnp.sum(p * dp, axis=-1, keepdims=True))
            for dlt in range(1 - qr, kr):
                tiles = [ds[a * GRID_W:(a + 1) * GRID_W, (a + dlt) * GRID_W:(a + dlt + 1) * GRID_W]
                         for a in range(qr) if 0 <= a + dlt < kr]
                dt_ref[h, jnp.clip(base + dlt, 0, plan.nd - 1)] += functools.reduce(jnp.add, tiles)
            dsb = (ds * (NA_DIM ** -0.5)).astype(BF)
            dq_ref[:, cs] = jnp.dot(dsb, k, preferred_element_type=F32).astype(BF)
            dk_ref[win, cs] += lax.dot_general(dsb, q, _DN["tn"], preferred_element_type=F32)
            dv_ref[win, cs] += lax.dot_general(p.astype(BF), do_h, _DN["tn"], preferred_element_type=F32)

    width = NA_HEADS * NA_DIM
    whole = pl.BlockSpec((plan.s, plan.hw), lambda j, g: (0, j))
    return _call(
        "na_bwd", body, (plan.nq, plan.groups), [blk, k_spec, v_spec, t_spec, blk], [blk, whole, whole, t_spec],
        [jax.ShapeDtypeStruct((plan.s, width), BF), jax.ShapeDtypeStruct((plan.s, width), F32),
         jax.ShapeDtypeStruct((plan.s, width), F32),
         jax.ShapeDtypeStruct((NA_HEADS, plan.nd, GRID_W, GRID_W), F32)],
        [qkv, qkv, qkv, table, do], comm, scratch=[plan.bias_scratch()])


def _na_rpb_grad(dt, rows):
    _, mask, dc = _na_geometry(rows)
    nd, nc = 2 * NA_ROWS_WIN - 1, 2 * NA_COLS_WIN - 1
    onehot = np.zeros((GRID_W * GRID_W, 128), np.float32)
    onehot[np.arange(GRID_W * GRID_W), dc.reshape(-1)] = mask.reshape(-1).astype(np.float32)
    flat = dt.reshape(NA_HEADS * nd, GRID_W * GRID_W)

    def body(a_ref, e_ref, o_ref):
        o_ref[...] = jnp.dot(a_ref[...], e_ref[...], precision=HI, preferred_element_type=F32)

    out = pl.pallas_call(body, name="na_rpb_grad", out_shape=jax.ShapeDtypeStruct((NA_HEADS * nd, 128), F32),
                         compiler_params=_params(0))(flat, jnp.asarray(onehot))
    return out[:, :nc].reshape(NA_HEADS, nd, nc)


def _rope_consts(s):
    pos = np.arange(s, dtype=np.float32)
    inv = (1.0 / (ROPE_THETA ** (np.arange(0, ML_ROPE, 2, dtype=np.float32) / ML_ROPE))).astype(np.float32)
    ang = pos[:, None] * inv[None, :]
    cos, sin = np.cos(ang).astype(np.float32), np.sin(ang).astype(np.float32)
    half = ML_ROPE // 2
    rot = np.zeros((ML_ROPE, ML_ROPE), np.float32)
    rot[np.arange(half) + half, np.arange(half)] = -1.0
    rot[np.arange(half), np.arange(half) + half] = 1.0
    return (jnp.asarray(np.concatenate([cos, cos], 1)), jnp.asarray(np.concatenate([sin, sin], 1)),
            jnp.asarray(rot), jnp.asarray(rot.T.copy()))


def _rope(v, cos, sin, rot):
    return v * cos + jnp.dot(v, rot, precision=HI, preferred_element_type=F32) * sin


def _unrope(dv, cos, sin, rot_t):
    return dv * cos + jnp.dot(dv * sin, rot_t, precision=HI, preferred_element_type=F32)


def _rms(v, g):
    return v * lax.rsqrt(jnp.mean(v * v, axis=-1, keepdims=True) + EPS) * g


def _mla_prep(lat, gq, gkv, cos, sin, rot, tm=256):
    s, w = lat.shape
    tm = _tile(s, tm)

    def body(l_ref, gq_ref, gkv_ref, c_ref, s_ref, r_ref, cq_ref, ckv_ref, kr_ref):
        cq_ref[...] = _rms(l_ref[:, :ML_RANK], gq_ref[...]).astype(BF)
        ckv_ref[...] = _rms(l_ref[:, ML_RANK:2 * ML_RANK], gkv_ref[...]).astype(BF)
        kr_ref[...] = _rope(l_ref[:, 2 * ML_RANK:], c_ref[...], s_ref[...], r_ref[...]).astype(BF)

    row = lambda c: pl.BlockSpec((tm, c), lambda i: (i, 0))
    full = lambda a: pl.BlockSpec(a.shape, lambda i: (0, 0))
    return pl.pallas_call(
        body, name="mla_prep", grid=(s // tm,),
        in_specs=[row(w), full(gq), full(gkv), row(ML_ROPE), row(ML_ROPE), full(rot)],
        out_specs=[row(ML_RANK), row(ML_RANK), row(ML_ROPE)],
        out_shape=[jax.ShapeDtypeStruct((s, ML_RANK), BF), jax.ShapeDtypeStruct((s, ML_RANK), BF),
                   jax.ShapeDtypeStruct((s, ML_ROPE), BF)],
        compiler_params=_params(1))(lat, gq, gkv, cos, sin, rot)


def _mla_q_proj(cq, wuq, cos, sin, rot, tm=512, comm=()):
    s, k = cq.shape
    tm = _tile(s, tm)

    def epi(accs, ex, out):
        acc = accs[0]
        out[0][:, :ML_NOPE] = acc[:, :ML_NOPE].astype(BF)
        out[0][:, ML_NOPE:] = _rope(acc[:, ML_NOPE:], ex[0][...], ex[1][...], ex[2][...]).astype(BF)

    rmap = lambda j, i: (i, 0)
    return _mm("mla_q_proj", (ML_HEADS, s // tm),
               [(cq, (tm, k), rmap, wuq, (None, ML_QK, k), lambda j, i: (j, 0, 0), "nt", 0, 0)],
               [(cos, (tm, ML_ROPE), rmap), (sin, (tm, ML_ROPE), rmap), (rot, rot.shape, lambda j, i: (0, 0))],
               [((ML_HEADS, s, ML_QK), BF, (None, tm, ML_QK), lambda j, i: (j, i, 0))], epi, comm=comm)[0]


def _mla_kv_proj(ckv, wukv, kr, tm=512, comm=()):
    s, k = ckv.shape
    tm = _tile(s, tm)

    def epi(accs, ex, out):
        acc = accs[0]
        out[0][:, :ML_NOPE] = acc[:, :ML_NOPE].astype(BF)
        out[0][:, ML_NOPE:] = ex[0][...]
        out[1][...] = acc[:, ML_NOPE:].astype(BF)

    rmap = lambda j, i: (i, 0)
    gmap = lambda j, i: (j, i, 0)
    return _mm("mla_kv_proj", (ML_HEADS, s // tm),
               [(ckv, (tm, k), rmap, wukv, (None, k, ML_NOPE + ML_V), lambda j, i: (j, 0, 0), "nn", 0, 0)],
               [(kr, (tm, ML_ROPE), rmap)],
               [((ML_HEADS, s, ML_QK), BF, (None, tm, ML_QK), gmap), ((ML_HEADS, s, ML_V), BF, (None, tm, ML_V), gmap)],
               epi, comm=comm)


def _mla_probs(q, k):
    sc = lax.dot_general(q, k, _DN["nt"], preferred_element_type=F32) * (ML_QK ** -0.5)
    e = jnp.exp(sc - jnp.max(sc, axis=-1, keepdims=True))
    return e / jnp.sum(e, axis=-1, keepdims=True)


def _mla_fwd(q, k, v, tq=1024, comm=()):
    _, s, _ = q.shape
    tq = _tile(s, tq)

    def body(q_ref, k_ref, v_ref, o_ref):
        p = _mla_probs(q_ref[...], k_ref[...])
        o_ref[...] = jnp.dot(p.astype(BF), v_ref[...], preferred_element_type=F32).astype(BF)

    return _call("mla_fwd", body, (ML_HEADS, s // tq),
                 [pl.BlockSpec((None, tq, ML_QK), lambda h, i: (h, i, 0)),
                  pl.BlockSpec((None, s, ML_QK), lambda h, i: (h, 0, 0)),
                  pl.BlockSpec((None, s, ML_V), lambda h, i: (h, 0, 0))],
                 pl.BlockSpec((tq, ML_V), lambda h, i: (i, h)),
                 jax.ShapeDtypeStruct((s, ML_HEADS * ML_V), BF), [q, k, v], comm)


def _mla_bwd(q, k, v, do, tq=1024, comm=()):
    _, s, _ = q.shape
    tq = _tile(s, tq)

    def body(q_ref, k_ref, v_ref, do_ref, dq_ref, dk_ref, dv_ref):
        i = pl.program_id(1)
        qv, kv, vv, dov = q_ref[...], k_ref[...], v_ref[...], do_ref[...]
        p = _mla_probs(qv, kv)
        dp = lax.dot_general(dov, vv, _DN["nt"], preferred_element_type=F32)
        ds = (p * (dp - jnp.sum(p * dp, axis=-1, keepdims=True)) * (ML_QK ** -0.5)).astype(BF)
        dq_ref[...] = jnp.dot(ds, kv, preferred_element_type=F32)
        _acc_rows(dk_ref, lax.dot_general(ds, qv, _DN["tn"], preferred_element_type=F32), i)
        _acc_rows(dv_ref, lax.dot_general(p.astype(BF), dov, _DN["tn"], preferred_element_type=F32), i)

    return _call(
        "mla_bwd", body, (ML_HEADS, s // tq),
        [pl.BlockSpec((None, tq, ML_QK), lambda h, i: (h, i, 0)),
         pl.BlockSpec((None, s, ML_QK), lambda h, i: (h, 0, 0)),
         pl.BlockSpec((None, s, ML_V), lambda h, i: (h, 0, 0)),
         pl.BlockSpec((tq, ML_V), lambda h, i: (i, h))],
        [pl.BlockSpec((None, tq, ML_QK), lambda h, i: (h, i, 0)),
         pl.BlockSpec((None, s, ML_QK), lambda h, i: (h, 0, 0)),
         pl.BlockSpec((None, s, ML_V), lambda h, i: (h, 0, 0))],
        [jax.ShapeDtypeStruct((ML_HEADS, s, ML_QK), F32), jax.ShapeDtypeStruct((ML_HEADS, s, ML_QK), F32),
         jax.ShapeDtypeStruct((ML_HEADS, s, ML_V), F32)],
        [q, k, v, do], comm)


def _mla_post(dq, dk, dv, cos, sin, rot_t, tm=1024):
    _, s, _ = dq.shape
    tm = _tile(s, tm)

    def body(dq_ref, dk_ref, dv_ref, c_ref, s_ref, r_ref, dqp_ref, dkv_ref, dkr_ref):
        h = pl.program_id(1)
        dqv, dkk = dq_ref[...], dk_ref[...]
        dqp_ref[:, :ML_NOPE] = dqv[:, :ML_NOPE].astype(BF)
        dqp_ref[:, ML_NOPE:] = _unrope(dqv[:, ML_NOPE:], c_ref[...], s_ref[...], r_ref[...]).astype(BF)
        dkv_ref[:, :ML_NOPE] = dkk[:, :ML_NOPE].astype(BF)
        dkv_ref[:, ML_NOPE:] = dv_ref[...].astype(BF)
        _acc_rows(dkr_ref, dkk[:, ML_NOPE:], h)

    gspec = lambda c: pl.BlockSpec((None, tm, c), lambda i, h: (h, i, 0))
    rspec = pl.BlockSpec((tm, ML_ROPE), lambda i, h: (i, 0))
    return pl.pallas_call(
        body, name="mla_post", grid=(s // tm, ML_HEADS),
        in_specs=[gspec(ML_QK), gspec(ML_QK), gspec(ML_V), rspec, rspec,
                  pl.BlockSpec(rot_t.shape, lambda i, h: (0, 0))],
        out_specs=[gspec(ML_QK), gspec(ML_NOPE + ML_V), rspec],
        out_shape=[jax.ShapeDtypeStruct((ML_HEADS, s, ML_QK), BF),
                   jax.ShapeDtypeStruct((ML_HEADS, s, ML_NOPE + ML_V), BF),
                   jax.ShapeDtypeStruct((s, ML_ROPE), F32)],
        compiler_params=_params(2))(dq, dk, dv, cos, sin, rot_t)


def _mla_lat_bwd(dcq, dckv, dkr, lat, gq, gkv, cos, sin, rot_t, tm=256):
    s, w = lat.shape
    tm = _tile(s, tm)

    def body(dcq_ref, dckv_ref, dkr_ref, l_ref, gq_ref, gkv_ref, c_ref, s_ref, r_ref, dl_ref, dgq_ref, dgkv_ref):
        i = pl.program_id(0)
        dql, pq = _rms_bwd_math(dcq_ref[...], l_ref[:, :ML_RANK], gq_ref[...])
        dkl, pkv = _rms_bwd_math(dckv_ref[...], l_ref[:, ML_RANK:2 * ML_RANK], gkv_ref[...])
        dl_ref[:, :ML_RANK] = dql.astype(BF)
        dl_ref[:, ML_RANK:2 * ML_RANK] = dkl.astype(BF)
        dl_ref[:, 2 * ML_RANK:] = _unrope(dkr_ref[...], c_ref[...], s_ref[...], r_ref[...]).astype(BF)
        _acc_rows(dgq_ref, pq, i)
        _acc_rows(dgkv_ref, pkv, i)

    row = lambda c: pl.BlockSpec((tm, c), lambda i: (i, 0))
    full = lambda a: pl.BlockSpec(a.shape, lambda i: (0, 0))
    return pl.pallas_call(
        body, name="mla_lat_bwd", grid=(s // tm,),
        in_specs=[row(ML_RANK), row(ML_RANK), row(ML_ROPE), row(w), full(gq), full(gkv), row(ML_ROPE), row(ML_ROPE),
                  full(rot_t)],
        out_specs=[row(w), full(gq), full(gkv)],
        out_shape=[jax.ShapeDtypeStruct((s, w), BF), jax.ShapeDtypeStruct(gq.shape, F32),
                   jax.ShapeDtypeStruct(gkv.shape, F32)],
        compiler_params=_params(1))(dcq, dckv, dkr, lat, gq, gkv, cos, sin, rot_t)


def _grp_dw(name, a, dout, ta=1024):
    s, k = a.shape
    ta = _tile(k, ta)
    if dout.ndim == 3:
        g, _, nb = dout.shape
        b_blk, b_map = (None, s, nb), lambda j, i: (j, 0, 0)
    else:
        g, nb = NDEV, dout.shape[1] // NDEV
        b_blk, b_map = (s, nb), lambda j, i: (0, j)
    return _mm(name, (g, k // ta),
               [(a, (s, ta), lambda j, i: (0, i), dout, b_blk, b_map, "tn", 0, 0)], [],
               [((g, k, nb), BF, (None, ta, nb), lambda j, i: (j, i, 0))], _store)[0]


def _grp_dw_t(name, dout, a, ta=512):
    g, s, nb = dout.shape
    k = a.shape[1]
    ta = _tile(k, ta)
    return _mm(name, (g, k // ta),
               [(dout, (None, s, nb), lambda j, i: (j, 0, 0), a, (s, ta), lambda j, i: (0, i), "tn", 0, 0)], [],
               [((g, nb, k), BF, (None, nb, ta), lambda j, i: (j, 0, i))], _store)[0]


def _grp_dx_t(name, dout, wt, tm=512, tn=512, comm=()):
    g, s, nb = dout.shape
    k = wt.shape[2]
    tm, tn = _tile(s, tm), _tile(k, tn)
    return _mm(name, (k // tn, s // tm),
               [(dout, (g, tm, nb), lambda j, i: (0, i, 0), wt, (g, nb, tn), lambda j, i: (0, 0, j), "nn", 0, g)], [],
               [((s, k), F32, (tm, tn), lambda j, i: (i, j))], _store, comm=comm)[0]


def _grp_dx(name, dout, w, tm=512, tn=512, comm=()):
    g, s, nb = dout.shape
    k = w.shape[1]
    tm, tn = _tile(s, tm), _tile(k, tn)
    return _mm(name, (k // tn, s // tm),
               [(dout, (g, tm, nb), lambda j, i: (0, i, 0), w, (g, tn, nb), lambda j, i: (0, j, 0), "nt", 0, g)], [],
               [((s, k), F32, (tm, tn), lambda j, i: (i, j))], _store, comm=comm)[0]


def _row_dw(name, a, dout, tn=2048):
    s, n = dout.shape
    tn = _tile(n, tn)
    if a.ndim == 3:
        kb = a.shape[2]
        a_blk, a_map = (None, s, kb), lambda j, i: (j, 0, 0)
    else:
        kb = a.shape[1] // NDEV
        a_blk, a_map = (s, kb), lambda j, i: (0, j)
    return _mm(name, (NDEV, n // tn),
               [(a, a_blk, a_map, dout, (s, tn), lambda j, i: (0, i), "tn", 0, 0)], [],
               [((NDEV, kb, n), BF, (None, kb, tn), lambda j, i: (j, 0, i))], _store)[0]


def _mix_merge(oa, ob, wa, wb, ga, gb, tm=1024, comm=()):
    s, k = oa.shape
    g, _, nb = wa.shape
    tm = _tile(s, tm)

    def epi(accs, ex, out):
        ya, yb = accs
        out[0][...] = ya.astype(BF)
        out[1][...] = yb.astype(BF)
        out[2][...] = (_sig(ex[0][...]) * ya + _sig(ex[1][...]) * yb).astype(BF)

    rmap = lambda j, i: (i, 0)
    wmap = lambda j, i: (j, 0, 0)
    o = ((g, s, nb), BF, (None, tm, nb), lambda j, i: (j, i, 0))
    cmap = lambda j, i: (i, j)
    return _mm("mix_merge", (g, s // tm),
               [(oa, (tm, k), rmap, wa, (None, k, nb), wmap, "nn", 0, 0),
                (ob, (tm, k), rmap, wb, (None, k, nb), wmap, "nn", 1, 0)],
               [(ga, (tm, nb), cmap), (gb, (tm, nb), cmap)], [o, o, o], epi, nacc=2, comm=comm)


def _mix_out(merged, wout, resid, tm=512, tn=512):
    g, s, kb = merged.shape
    d = wout.shape[2]
    tm, tn = _tile(s, tm), _tile(d, tn)

    def epi(accs, ex, out):
        out[0][...] = ex[0][...] + accs[0]

    return _mm("mix_out", (d // tn, s // tm),
               [(merged, (g, tm, kb), lambda j, i: (0, i, 0), wout, (g, kb, tn), lambda j, i: (0, 0, j), "nn", 0, g)],
               [(resid, (tm, tn), lambda j, i: (i, j))],
               [((s, d), F32, (tm, tn), lambda j, i: (i, j))], epi)[0]


def _mix_out_bwd(dh, wout, ga, gb, ya, yb, tm=1024, comm=()):
    s, d = dh.shape
    g, kb, _ = wout.shape
    tm = _tile(s, tm)

    def epi(accs, ex, out):
        dm = accs[0]
        sa, sb = _sig(ex[0][...]), _sig(ex[1][...])
        out[0][...] = (dm * sa).astype(BF)
        out[1][...] = (dm * sb).astype(BF)
        out[2][...] = (dm * ex[2][...].astype(F32) * sa * (1.0 - sa)).astype(BF)
        out[3][...] = (dm * ex[3][...].astype(F32) * sb * (1.0 - sb)).astype(BF)

    cmap = lambda j, i: (i, j)
    gmap = lambda j, i: (j, i, 0)
    og = ((g, s, kb), BF, (None, tm, kb), gmap)
    oc = ((s, g * kb), BF, (tm, kb), cmap)
    return _mm("mix_out_bwd", (g, s // tm),
               [(dh, (tm, d), lambda j, i: (i, 0), wout, (None, kb, d), lambda j, i: (j, 0, 0), "nt", 0, 0)],
               [(ga, (tm, kb), cmap), (gb, (tm, kb), cmap), (ya, (None, tm, kb), gmap), (yb, (None, tm, kb), gmap)],
               [og, og, oc, oc], epi, comm=comm)


def _pl_forward(n4, wplg, p, wpl, h3, tm=1024):
    s, d = n4.shape
    g, kb, _ = wplg.shape
    kp, nb = wpl.shape[1], wpl.shape[2]
    tm = _tile(s, tm)
    wplg_nat = wplg.reshape(g * kb, d)

    def epi(accs, ex, out):
        t, pe = accs
        out[0][...] = ex[0][...] + _sig(t) * pe
        out[1][...] = t
        out[2][...] = pe.astype(BF)

    rmap = lambda j, i: (i, 0)
    cmap = lambda j, i: (i, j)
    return _mm("pl_forward", (g, s // tm),
               [(n4, (tm, d), rmap, wplg_nat, (g * kb, nb), lambda j, i: (0, j), "nn", 0, 0),
                (p, (tm, kp), rmap, wpl, (None, kp, nb), lambda j, i: (j, 0, 0), "nn", 1, 0)],
               [(h3, (tm, nb), cmap)],
               [((s, d), F32, (tm, nb), cmap), ((s, d), F32, (tm, nb), cmap), ((s, d), BF, (tm, nb), cmap)],
               epi, nacc=2)


def _row_dx(name, dout, w, tm=1024, comm=()):
    s, n = dout.shape
    g, kb, _ = w.shape
    tm = _tile(s, tm)
    return _mm(name, (g, s // tm),
               [(dout, (tm, n), lambda j, i: (i, 0), w, (None, kb, n), lambda j, i: (j, 0, 0), "nt", 0, 0)], [],
               [((s, g * kb), F32, (tm, kb), lambda j, i: (i, j))], _store, comm=comm)[0]


def _in_proj_bwd_x(pieces, weights, tm=512, tn=512, comm=()):
    s = pieces[0].shape[0]
    d = weights[0].shape[1]
    tm, tn = _tile(s, tm), _tile(d, tn)
    prods = [(pc, (tm, pc.shape[1]), lambda j, i: (i, 0), w, (w.shape[0], tn), lambda j, i: (0, j), "nn", 0, 0)
             for pc, w in zip(pieces, weights)]
    return _mm("in_proj_dx", (d // tn, s // tm), prods, [],
               [((s, d), F32, (tm, tn), lambda j, i: (i, j))], _store, comm=comm)[0]


def _split_w_in(w_in_t):
    g, nb, d = w_in_t.shape
    nat = w_in_t.reshape(g * nb, d)
    na, lat = 3 * NA_HEADS * NA_DIM, 2 * ML_RANK + ML_ROPE
    return nat[:na], nat[na:na + lat], nat[na + lat:na + lat + d], nat[na + lat + d:]


def _pair_sum(name, part, landed, core):
    _, _, r, c = part.shape
    tr, tc = _ew_tile(r, c)

    def body(core_ref, a_ref, b_ref, o_ref):
        o_ref[...] = (a_ref[...].astype(F32) + b_ref[...].astype(F32)).astype(o_ref.dtype)

    return pl.pallas_call(
        body, name=name,
        grid_spec=pltpu.PrefetchScalarGridSpec(
            num_scalar_prefetch=1, grid=(NCHIP, r // tr, c // tc),
            in_specs=[pl.BlockSpec((None, None, tr, tc), lambda j, i, k, core_ref: (j, core_ref[0], i, k)),
                      pl.BlockSpec((None, tr, tc), lambda j, i, k, core_ref: (j, i, k))],
            out_specs=pl.BlockSpec((None, tr, tc), lambda j, i, k, core_ref: (j, i, k))),
        out_shape=jax.ShapeDtypeStruct(landed.shape, landed.dtype), compiler_params=_params(3),
    )(core, part, landed)


def _device_step(x, p, target, sp, own, core):
    s, d = x.shape
    rows = s // GRID_W
    cos, sin, rot, rot_t = _rope_consts(s)
    w, dw4, sums, dsp, pending = {}, {}, {}, {}, []

    def gather(*names):
        return _GatherPart(names, [own[n] for n in names])

    def got(part):
        w.update(zip(part.names, part.results))

    def grad(name, g):
        dw4[name] = g.reshape((NCHIP, 2) + g.shape[1:])

    def to_sibling(*names):
        return _SiblingPart(names, [dw4[n] for n in names])

    def add_pairs(part):
        for n, landed in zip(part.names, part.results):
            sums[n] = _pair_sum("pair_sum_" + n, dw4[n], landed, core)

    def start_chips(tag, *names):
        send, recv, thru, lands, token = _chips_start("rs_start_" + tag, [sums[n] for n in names])
        pending.append((tag, names, send, recv, thru, lands))
        return token

    c0 = gather("ffn1_w_gate", "ffn1_w_up")
    c1 = gather("ffn1_w_down")
    c2 = gather("w_in")

    def ffn1_wgu():
        got(c0)
        return w["ffn1_w_gate"], w["ffn1_w_up"]

    def ffn1_wd():
        got(c1)
        return w["ffn1_w_down"]

    h1, ffn1_saved = _ffn_forward("ffn1", x, sp["ffn1_norm"], ffn1_wgu, ffn1_wd,
                                  norm_comm=[c0], up_comm=[c1], down_comm=[c2])
    got(c2)
    wqkv, wlat, wga, wgb = _split_w_in(w["w_in"])
    u = _rms_fwd("mix_norm", h1, sp["mix_norm"])
    c3 = gather("w_uq", "w_ukv")
    qkv = _mm_nt("in_qkv", u, wqkv, BF, tn=1024, comm=[c3])
    got(c3)
    lat = _mm_nt("in_lat", u, wlat, F32)
    c3a = gather("w_branch_a")
    ga = _mm_nt("in_ga", u, wga, F32, tn=1024, comm=[c3a])
    got(c3a)
    c3b = gather("w_branch_b")
    gb = _mm_nt("in_gb", u, wgb, F32, tn=1024, comm=[c3b])
    got(c3b)
    tb = _na_table(sp["na_rpb"], rows)
    c4 = gather("ffn2_w_gate")
    oa = _na_fwd(qkv, tb, comm=[c4])
    got(c4)
    cq, ckv, kr = _mla_prep(lat, sp["q_a_norm"], sp["kv_a_norm"], cos, sin, rot)
    c4a = gather("w_out")
    qf = _mla_q_proj(cq, w["w_uq"], cos, sin, rot, comm=[c4a])
    got(c4a)
    c4b = gather("w_pl_gate")
    kf, vf = _mla_kv_proj(ckv, w["w_ukv"], kr, comm=[c4b])
    got(c4b)
    c5 = gather("ffn2_w_up")
    ob = _mla_fwd(qf, kf, vf, comm=[c5])
    got(c5)
    c5a = gather("w_pl")
    ya, yb, merged = _mix_merge(oa, ob, w["w_branch_a"], w["w_branch_b"], ga, gb, comm=[c5a])
    got(c5a)
    h2 = _mix_out(merged, w["w_out"], h1)
    c6 = gather("ffn2_w_down")

    def ffn2_wd():
        got(c6)
        return w["ffn2_w_down"]

    h3, ffn2_saved = _ffn_forward("ffn2", h2, sp["ffn2_norm"], lambda: (w["ffn2_w_gate"], w["ffn2_w_up"]), ffn2_wd,
                                  up_comm=[c6])
    n4 = _rms_fwd("pl_norm", h3, sp["pl_norm"])
    pb = p.astype(BF)
    h4, t, pe = _pl_forward(n4, w["w_pl_gate"], pb, w["w_pl"], h3)

    dh4, dsp["final_norm"], loss = _loss_head(h4, target, sp["final_norm"])
    dt, dpe = _pl_bwd_elem(dh4, pe, t)
    grad("w_pl", _grp_dw("pl_dw", pb, dpe))
    grad("w_pl_gate", _row_dw("plg_dw", n4, dt))
    s1 = to_sibling("w_pl", "w_pl_gate")
    dn4 = _row_dx("plg_dx", dt, w["w_pl_gate"], comm=[s1])
    add_pairs(s1)
    tok = start_chips("pl", "w_pl", "w_pl_gate")
    dh3, dsp["pl_norm"] = _rms_bwd("pl_dnorm", dn4, h3, sp["pl_norm"], dh4, comm=[_After(tok)])

    xn, hg, hu, a = ffn2_saved
    dhb = dh3.astype(BF)
    grad("ffn2_w_down", _ffn_bwd_wd("ffn2_dwd", a, dhb))
    s2 = to_sibling("ffn2_w_down")
    dhg, dhu = _ffn_bwd_act("ffn2_dact", dhb, w["ffn2_w_down"], hg, hu, comm=[s2])
    add_pairs(s2)
    tok = start_chips("ffn2_down", "ffn2_w_down")
    dwg, dwu = _ffn_bwd_wup("ffn2_dwup", xn, dhg, dhu, comm=[_After(tok)])
    grad("ffn2_w_gate", dwg)
    grad("ffn2_w_up", dwu)
    s3 = to_sibling("ffn2_w_gate", "ffn2_w_up")
    dxn = _ffn_bwd_x("ffn2_dx", dhg, dhu, w["ffn2_w_gate"], w["ffn2_w_up"], comm=[s3])
    add_pairs(s3)
    tok = start_chips("ffn2_up", "ffn2_w_gate", "ffn2_w_up")
    dh2, dsp["ffn2_norm"] = _rms_bwd("ffn2_dnorm", dxn, h2, sp["ffn2_norm"], dh3, comm=[_After(tok)])

    dh2b = dh2.astype(BF)
    grad("w_out", _row_dw("out_dw", merged, dh2b))
    s4 = to_sibling("w_out")
    dya, dyb, dga, dgb = _mix_out_bwd(dh2b, w["w_out"], ga, gb, ya, yb, comm=[s4])
    add_pairs(s4)
    grad("w_branch_a", _grp_dw("bra_dw", oa, dya))
    grad("w_branch_b", _grp_dw("brb_dw", ob, dyb))
    doa = _grp_dx("bra_dx", dya, w["w_branch_a"]).astype(BF)
    s5 = to_sibling("w_branch_a", "w_branch_b")
    dob = _grp_dx("brb_dx", dyb, w["w_branch_b"], comm=[s5]).astype(BF)
    add_pairs(s5)
    tok = start_chips("mix", "w_out", "w_branch_a", "w_branch_b")

    dqf, dkf, dvf = _mla_bwd(qf, kf, vf, dob, comm=[_After(tok)])
    dqp, dkv, dkr = _mla_post(dqf, dkf, dvf, cos, sin, rot_t)
    grad("w_uq", _grp_dw_t("uq_dw", dqp, cq))
    grad("w_ukv", _grp_dw("ukv_dw", ckv, dkv))
    dcq = _grp_dx_t("uq_dx", dqp, w["w_uq"])
    s6 = to_sibling("w_uq", "w_ukv")
    dckv = _grp_dx("ukv_dx", dkv, w["w_ukv"], comm=[s6])
    add_pairs(s6)
    tok = start_chips("mla", "w_uq", "w_ukv")
    dlat, dsp["q_a_norm"], dsp["kv_a_norm"] = _mla_lat_bwd(dcq, dckv, dkr, lat, sp["q_a_norm"], sp["kv_a_norm"],
                                                         cos, sin, rot_t)
    dq_na, dk_na, dv_na, dtab = _na_bwd(qkv, tb, doa, comm=[_After(tok)])
    dsp["na_rpb"] = _na_rpb_grad(dtab, rows)
    dqkv = jnp.concatenate([dq_na, dk_na.astype(BF), dv_na.astype(BF)], axis=1)

    pieces = [dqkv, dlat, dga, dgb]
    dwin = jnp.zeros((sum(pc.shape[1] for pc in pieces), d), BF)
    row0 = 0
    for i, pc in enumerate(pieces):
        dwin = _mm_tn_into("in_dw%d" % i, pc, u, dwin, row0)
        row0 += pc.shape[1]
    grad("w_in", dwin.reshape(NDEV, -1, d))
    s7 = to_sibling("w_in")
    du = _in_proj_bwd_x(pieces, [wqkv, wlat, wga, wgb], comm=[s7])
    add_pairs(s7)
    tok = start_chips("w_in", "w_in")
    dh1, dsp["mix_norm"] = _rms_bwd("mix_dnorm", du, h1, sp["mix_norm"], dh2, comm=[_After(tok)])

    xn, hg, hu, a = ffn1_saved
    dhb = dh1.astype(BF)
    grad("ffn1_w_down", _ffn_bwd_wd("ffn1_dwd", a, dhb))
    s8 = to_sibling("ffn1_w_down")
    dhg, dhu = _ffn_bwd_act("ffn1_dact", dhb, w["ffn1_w_down"], hg, hu, comm=[s8])
    add_pairs(s8)
    tok = start_chips("ffn1_down", "ffn1_w_down")
    dwg, dwu = _ffn_bwd_wup("ffn1_dwup", xn, dhg, dhu, comm=[_After(tok)])
    grad("ffn1_w_gate", dwg)
    grad("ffn1_w_up", dwu)
    s9 = to_sibling("ffn1_w_gate", "ffn1_w_up")
    _comm_only("rs_sibling_ffn1", [s9])
    add_pairs(s9)
    tok = start_chips("ffn1_up", "ffn1_w_gate", "ffn1_w_up")
    dxn = _ffn_bwd_x("ffn1_dx", dhg, dhu, w["ffn1_w_gate"], w["ffn1_w_up"], comm=[_After(tok)])
    dx, dsp["ffn1_norm"] = _rms_bwd("ffn1_dnorm", dxn, x, sp["ffn1_norm"], dh1)
    return loss, dx, pending, dsp


def _gather_small(buf):
    def body(in_ref, out_ref, send_sems, recv_sems, local_sem):
        x, y, c = _coords()
        mine = pltpu.make_async_copy(in_ref, out_ref.at[4 * x + 2 * y + c], local_sem)
        mine.start()
        cps = []
        for k in range(1, NDEV):
            fx, fy, fc = (k >> 2) & 1, (k >> 1) & 1, k & 1
            peer = (x ^ fx, y ^ fy, c ^ fc)
            cps.append(pltpu.make_async_remote_copy(
                src_ref=in_ref, dst_ref=out_ref.at[4 * x + 2 * y + c], send_sem=send_sems.at[k - 1],
                recv_sem=recv_sems.at[k - 1], device_id=peer, device_id_type=MESH))
        for cp in cps:
            cp.start()
        for k in range(1, NDEV):
            fx, fy, fc = (k >> 2) & 1, (k >> 1) & 1, k & 1
            px, py, pc = x ^ fx, y ^ fy, c ^ fc
            pltpu.make_async_remote_copy(
                src_ref=in_ref, dst_ref=out_ref.at[4 * px + 2 * py + pc], send_sem=send_sems.at[k - 1],
                recv_sem=recv_sems.at[k - 1], device_id=(px, py, pc), device_id_type=MESH).wait_recv()
        for cp in cps:
            cp.wait_send()
        mine.wait()

    return pl.pallas_call(
        body, name="gather_small", in_specs=[ANY], out_specs=ANY,
        out_shape=jax.ShapeDtypeStruct((NDEV,) + buf.shape, buf.dtype),
        scratch_shapes=[pltpu.SemaphoreType.DMA((NDEV - 1,)), pltpu.SemaphoreType.DMA((NDEV - 1,)),
                        pltpu.SemaphoreType.DMA],
    )(buf)


def _adam_math(wv, g, m, v):
    m_new = B1 * m + (1.0 - B1) * g
    v_new = B2 * v + (1.0 - B2) * (g * g)
    m_hat = m_new / (1.0 - B1 ** STEP)
    v_hat = v_new / (1.0 - B2 ** STEP)
    return -LR * (m_hat / (jnp.sqrt(v_hat) + ADAM_EPS) + WD * wv), m_new, v_new


def _adam(name, parts, wv, m, v, after=None):
    npart, r, c = parts.shape
    tr, tc = _ew_tile(r, c)

    def body(p_ref, w_ref, m_ref, v_ref, *rest):
        g_ref, d_ref, mo_ref, vo_ref = rest[-4:]
        g = p_ref[0].astype(F32)
        for j in range(1, npart):
            g = g + p_ref[j].astype(F32)
        g_ref[...] = g
        d_ref[...], mo_ref[...], vo_ref[...] = _adam_math(w_ref[...], g, m_ref[...], v_ref[...])

    blk = pl.BlockSpec((tr, tc), lambda i, k: (i, k))
    extra = [] if after is None else [after]
    return pl.pallas_call(
        body, name=name, grid=(r // tr, c // tc),
        in_specs=[pl.BlockSpec((npart, tr, tc), lambda i, k: (0, i, k)), blk, blk, blk] + [ANY] * len(extra),
        out_specs=[blk] * 4, out_shape=[jax.ShapeDtypeStruct((r, c), F32)] * 4, compiler_params=_params(2),
    )(parts, wv, m, v, *extra)


def _adam_exchanged(name, sums, land, wv, m, v, my_chip):
    _, r, c = sums.shape
    tr, tc = _ew_tile(r, c)

    def body(chip_ref, s_ref, l_ref, w_ref, m_ref, v_ref, g_ref, d_ref, mo_ref, vo_ref):
        g = s_ref[...].astype(F32)
        for j in range(3):
            g = g + l_ref[j].astype(F32)
        g_ref[...] = g
        d_ref[...], mo_ref[...], vo_ref[...] = _adam_math(w_ref[...], g, m_ref[...], v_ref[...])

    blk = pl.BlockSpec((tr, tc), lambda i, k, chip_ref: (i, k))
    return pl.pallas_call(
        body, name=name,
        grid_spec=pltpu.PrefetchScalarGridSpec(
            num_scalar_prefetch=1, grid=(r // tr, c // tc),
            in_specs=[pl.BlockSpec((None, tr, tc), lambda i, k, chip_ref: (chip_ref[0], i, k)),
                      pl.BlockSpec((3, tr, tc), lambda i, k, chip_ref: (0, i, k)), blk, blk, blk],
            out_specs=[blk] * 4),
        out_shape=[jax.ShapeDtypeStruct((r, c), F32)] * 4, compiler_params=_params(2),
    )(my_chip, sums, land, wv, m, v)


SHARDED = ("ffn1_w_gate", "ffn1_w_up", "ffn1_w_down", "w_in", "w_uq", "w_ukv", "w_branch_a", "w_branch_b", "w_out",
           "ffn2_w_gate", "ffn2_w_up", "ffn2_w_down", "w_pl", "w_pl_gate")
TRANSPOSED = ("ffn1_w_gate", "ffn1_w_up", "ffn2_w_gate", "ffn2_w_up", "w_in", "w_uq")
REPLICATED = ("ffn1_norm", "mix_norm", "q_a_norm", "kv_a_norm", "na_rpb", "ffn2_norm", "pl_norm", "final_norm")
WEIGHTS = ("ffn1_norm", "ffn1_w_gate", "ffn1_w_up", "ffn1_w_down", "mix_norm", "w_in", "q_a_norm", "w_uq",
           "kv_a_norm", "w_ukv", "na_rpb", "w_branch_a", "w_branch_b", "w_out", "ffn2_norm", "ffn2_w_gate",
           "ffn2_w_up", "ffn2_w_down", "pl_norm", "w_pl", "w_pl_gate", "final_norm")
SMALL_W = 2048


def _pack_small(vals):
    rows = []
    for name in REPLICATED:
        flat = vals[name].reshape(-1).astype(F32)
        n = -(-flat.shape[0] // SMALL_W) * SMALL_W
        rows.append(jnp.pad(flat, (0, n - flat.shape[0])).reshape(-1, SMALL_W))
    return jnp.concatenate(rows, axis=0)


def _unpack_small(buf, shapes):
    out, r = {}, 0
    for name in REPLICATED:
        size = int(np.prod(shapes[name]))
        nrow = -(-size // SMALL_W)
        out[name] = buf[r:r + nrow].reshape(-1)[:size].reshape(shapes[name])
        r += nrow
    return out


def kernel(x, p, ffn1_norm, ffn1_w_gate, ffn1_w_up, ffn1_w_down, mix_norm, w_in, q_a_norm, w_uq, kv_a_norm, w_ukv, na_rpb, w_branch_a, w_branch_b, w_out, ffn2_norm, ffn2_w_gate, ffn2_w_up, ffn2_w_down, pl_norm, w_pl, w_pl_gate, final_norm, loss_target, m_ffn1_norm, m_ffn1_w_gate, m_ffn1_w_up, m_ffn1_w_down, m_mix_norm, m_w_in, m_q_a_norm, m_w_uq, m_kv_a_norm, m_w_ukv, m_na_rpb, m_w_branch_a, m_w_branch_b, m_w_out, m_ffn2_norm, m_ffn2_w_gate, m_ffn2_w_up, m_ffn2_w_down, m_pl_norm, m_w_pl, m_w_pl_gate, m_final_norm, v_ffn1_norm, v_ffn1_w_gate, v_ffn1_w_up, v_ffn1_w_down, v_mix_norm, v_w_in, v_q_a_norm, v_w_uq, v_kv_a_norm, v_w_ukv, v_na_rpb, v_w_branch_a, v_w_branch_b, v_w_out, v_ffn2_norm, v_ffn2_w_gate, v_ffn2_w_up, v_ffn2_w_down, v_pl_norm, v_w_pl, v_w_pl_gate, v_final_norm):
    args = dict(locals())
    wts = {n: args[n] for n in WEIGHTS}
    mom = {n: args["m_" + n] for n in WEIGHTS}
    var = {n: args["v_" + n] for n in WEIGHTS}
    shapes = {n: wts[n].shape for n in WEIGHTS}
    core = lax.axis_index("c").astype(jnp.int32).reshape(1)

    local = lambda n, a: a[0].T if n in TRANSPOSED else a[0]
    own = {n: local(n, wts[n]).astype(BF) for n in SHARDED}
    sp = {n: wts[n].reshape(1, -1) for n in REPLICATED if n != "na_rpb"}
    sp["na_rpb"] = wts["na_rpb"][0]
    loss_part, grad_x, pending, dsp = _device_step(x[0], p[0, 0], loss_target[0], sp, own, core)

    out = {}
    last = grad_x
    my_chip = (2 * lax.axis_index("x") + lax.axis_index("y")).astype(jnp.int32).reshape(1)
    for tag, names, send, recv, thru, lands in pending:
        thru, lands = _chips_wait("rs_wait_" + tag, send, recv, thru, lands, last)
        for n, s4, l3 in zip(names, thru, lands):
            res4 = _adam_exchanged("adam_" + n, s4, l3, local(n, wts[n]), local(n, mom[n]), local(n, var[n]), my_chip)
            out[n] = tuple((a.T if n in TRANSPOSED else a)[None] for a in res4)
            last = res4[1]

    small = jnp.concatenate([_pack_small(dsp), jnp.pad(loss_part, ((0, 0), (0, SMALL_W - loss_part.shape[1])))], 0)
    pad_rows = -small.shape[0] % 8
    small = jnp.pad(small, ((0, pad_rows), (0, 0)))
    every = _gather_small(small)
    zeros = jnp.zeros((1 + pad_rows, SMALL_W), F32)
    pack = lambda d: jnp.concatenate([_pack_small(d), zeros], 0)
    g_s, d_s, m_s, v_s = _adam("adam_small", every, pack(wts), pack(mom), pack(var))
    n_rows = small.shape[0] - 1 - pad_rows
    loss = g_s[n_rows, 0]
    small_out = [_unpack_small(b, shapes) for b in (g_s, d_s, m_s, v_s)]
    for n in REPLICATED:
        out[n] = tuple(b[n] for b in small_out)

    res = [loss, grad_x[None]]
    for k in range(4):
        res += [out[n][k] for n in WEIGHTS]
    return tuple(res)
```

```python
import functools

import numpy as np
import jax
import jax.numpy as jnp
from jax import lax
from jax.experimental import pallas as pl
from jax.experimental.pallas import tpu as pltpu

F32 = jnp.float32
BF = jnp.bfloat16
MESH = pl.DeviceIdType.MESH

NDEV = 8
NCHIP = 4
VMEM_LIMIT = 56 * 1024 * 1024
EPS = 1e-6
NEG = -1e30
GRID_W = 64
NA_HEADS, NA_DIM = 8, 128
NA_ROWS_WIN, NA_COLS_WIN = 8, 16
NA_HG = 4
NA_QROWS = 4
ML_HEADS, ML_NOPE, ML_ROPE, ML_V = 8, 128, 64, 128
ML_QK = ML_NOPE + ML_ROPE
ML_RANK = 512
ROPE_THETA = 10000.0
LR, B1, B2, ADAM_EPS, WD, STEP = 0.001, 0.9, 0.999, 1e-08, 0.01, 10
HI = lax.Precision.HIGHEST

_DN = {"nn": (((1,), (0,)), ((), ())), "nt": (((1,), (1,)), ((), ())), "tn": (((0,), (0,)), ((), ()))}


def _params(n):
    return pltpu.CompilerParams(dimension_semantics=("arbitrary",) * n, vmem_limit_bytes=VMEM_LIMIT)


def _sig(v):
    return jax.nn.sigmoid(v)


ANY = pl.BlockSpec(memory_space=pl.ANY)


def _coords():
    return lax.axis_index("x"), lax.axis_index("y"), lax.axis_index("c")


class _Part:
    inputs, out_shapes, sem_shapes, results = (), (), (), None

    def mid(self, ins, outs, sems):
        pass

    def late(self, ins, outs, sems):
        pass


class _After(_Part):
    def __init__(self, token):
        self.inputs = [token]

    def start(self, ins, outs, sems):
        pass

    finish = start


class _GatherPart(_Part):
    def __init__(self, names, shards):
        n = len(shards)
        self.names, self.inputs = list(names), list(shards)
        self.out_shapes = [jax.ShapeDtypeStruct((NDEV,) + a.shape, a.dtype) for a in shards]
        self.sem_shapes = [pltpu.SemaphoreType.DMA((n, 7)), pltpu.SemaphoreType.DMA((n, 7)),
                           pltpu.SemaphoreType.DMA((n,))]

    def _plan(self, ins, outs, sems):
        send_sems, recv_sems, local_sems = sems
        x, y, c = _coords()
        me, sib, diag = (x, y, c), (x, y, 1 - c), (1 - x, 1 - y, c)
        n1, n2 = (x ^ (1 - c), y ^ c, c), (x ^ c, y ^ (1 - c), c)

        def copy(i, k, block, to, src=None):
            px, py, pc = block
            dst = outs[i].at[4 * px + 2 * py + pc]
            return pltpu.make_async_remote_copy(
                src_ref=dst if src is None else src, dst_ref=dst, send_sem=send_sems.at[i, k],
                recv_sem=recv_sems.at[i, k], device_id=to, device_id_type=MESH)

        mine = [pltpu.make_async_copy(ins[i], outs[i].at[4 * x + 2 * y + c], local_sems.at[i])
                for i in range(len(ins))]
        return copy, mine, me, sib, n1, n2, diag

    def _own_sends(self, ins, copy, me, sib, n1, n2):
        return [copy(i, k, me, to, src=ins[i]) for i in range(len(ins)) for k, to in enumerate((sib, n1, n2))]

    def start(self, ins, outs, sems):
        copy, mine, me, sib, n1, n2, _ = self._plan(ins, outs, sems)
        for cp in mine + self._own_sends(ins, copy, me, sib, n1, n2):
            cp.start()

    def mid(self, ins, outs, sems):
        copy, _, me, sib, n1, n2, _ = self._plan(ins, outs, sems)
        for i in range(len(ins)):
            copy(i, 1, n1, me).wait_recv()
            copy(i, 3, n1, n2).start()
            copy(i, 4, n1, sib).start()

    def late(self, ins, outs, sems):
        copy, _, me, sib, _, n2, diag = self._plan(ins, outs, sems)
        for i in range(len(ins)):
            copy(i, 2, n2, me).wait_recv()
            copy(i, 5, n2, sib).start()
        for i in range(len(ins)):
            copy(i, 3, diag, me).wait_recv()
            copy(i, 6, diag, sib).start()

    def finish(self, ins, outs, sems):
        copy, mine, me, sib, n1, n2, diag = self._plan(ins, outs, sems)
        other = lambda dev: (dev[0], dev[1], sib[2])
        n = len(ins)
        for i in range(n):
            copy(i, 0, sib, me).wait_recv()
            for k, block in ((4, other(n2)), (5, other(n1)), (6, other(diag))):
                copy(i, k, block, me).wait_recv()
        for cp in self._own_sends(ins, copy, me, sib, n1, n2):
            cp.wait_send()
        for i in range(n):
            for k, block in ((3, n1), (4, n1), (5, n2), (6, diag)):
                copy(i, k, block, sib).wait_send()
        for cp in mine:
            cp.wait()


class _SiblingPart(_Part):
    def __init__(self, names, parts):
        n = len(parts)
        self.names, self.inputs = list(names), list(parts)
        self.out_shapes = [jax.ShapeDtypeStruct((NCHIP,) + a.shape[2:], a.dtype) for a in parts]
        self.sem_shapes = [pltpu.SemaphoreType.DMA((n,)), pltpu.SemaphoreType.DMA((n,))]

    def _copies(self, ins, outs, sems):
        x, y, c = _coords()
        return [pltpu.make_async_remote_copy(
            src_ref=ins[i].at[:, 1 - c], dst_ref=outs[i], send_sem=sems[0].at[i], recv_sem=sems[1].at[i],
            device_id=(x, y, 1 - c), device_id_type=MESH) for i in range(len(ins))]

    def start(self, ins, outs, sems):
        for cp in self._copies(ins, outs, sems):
            cp.start()

    def finish(self, ins, outs, sems):
        cps = self._copies(ins, outs, sems)
        for cp in cps:
            cp.wait_recv()
        for cp in cps:
            cp.wait_send()


HBM = pl.BlockSpec(memory_space=pltpu.HBM)
SEM = pl.BlockSpec(memory_space=pltpu.SEMAPHORE)


def _chip_peers():
    x, y, c = _coords()
    return [(1 - x, y, c), (x, 1 - y, c), (1 - x, 1 - y, c)]


def _chips_start(name, sums):
    n = len(sums)

    def body(*refs):
        ins, lands, send_sems, recv_sems = refs[:n], refs[n:2 * n], refs[2 * n], refs[2 * n + 1]
        for i in range(n):
            for k, (px, py, pc) in enumerate(_chip_peers()):
                pltpu.make_async_remote_copy(
                    src_ref=ins[i].at[2 * px + py], dst_ref=lands[i].at[k], send_sem=send_sems.at[3 * i + k],
                    recv_sem=recv_sems.at[3 * i + k], device_id=(px, py, pc), device_id_type=MESH).start()
        refs[-1][...] = jnp.zeros_like(refs[-1])

    lands = [lax.empty((3,) + a.shape[1:], a.dtype) for a in sums]
    bufs = list(sums) + lands
    res = pl.pallas_call(
        body, name=name, in_specs=[HBM] * (2 * n),
        out_specs=(SEM, SEM, *[HBM] * (2 * n), pl.BlockSpec(memory_space=pltpu.VMEM)),
        out_shape=(pltpu.SemaphoreType.DMA((3 * n,)), pltpu.SemaphoreType.DMA((3 * n,)),
                   *[pltpu.HBM(a.shape, a.dtype) for a in bufs], jax.ShapeDtypeStruct((8, 128), F32)),
        input_output_aliases={i: 2 + i for i in range(2 * n)},
        compiler_params=pltpu.CompilerParams(has_side_effects=pltpu.SideEffectType.DATAFLOW_SIDE_EFFECTING),
    )(*[pltpu.with_memory_space_constraint(a, pltpu.HBM) for a in bufs])
    return res[0], res[1], list(res[2:2 + n]), list(res[2 + n:2 + 2 * n]), res[-1]


def _chips_wait(name, send_sems, recv_sems, sums, lands, after):
    n = len(sums)

    def body(*refs):
        ins, zones, send, recv = refs[:n], refs[n:2 * n], refs[2 * n], refs[2 * n + 1]
        for i in range(n):
            for k, peer in enumerate(_chip_peers()):
                cp = pltpu.make_async_remote_copy(
                    src_ref=ins[i].at[0], dst_ref=zones[i].at[k], send_sem=send.at[3 * i + k],
                    recv_sem=recv.at[3 * i + k],
                    device_id=peer, device_id_type=MESH)
                cp.wait_send()
                cp.wait_recv()

    bufs = list(sums) + list(lands)
    res = pl.pallas_call(
        body, name=name, in_specs=[HBM] * (2 * n) + [SEM, SEM, ANY], out_specs=[HBM] * (2 * n),
        out_shape=[pltpu.HBM(a.shape, a.dtype) for a in bufs], input_output_aliases={i: i for i in range(2 * n)},
        compiler_params=pltpu.CompilerParams(has_side_effects=pltpu.SideEffectType.DATAFLOW_SIDE_EFFECTING),
    )(*bufs, send_sems, recv_sems, after)
    return list(res[:n]), list(res[n:])


def _call(name, body, grid, in_specs, out_specs, out_shape, args, comm=(), scratch=()):
    comm = [p for p in comm if p is not None]
    single = not isinstance(out_shape, (list, tuple))
    o_specs = [out_specs] if single else list(out_specs)
    o_shape = [out_shape] if single else list(out_shape)
    n_in, n_out = len(in_specs), len(o_specs)
    c_in = [a for p in comm for a in p.inputs]
    c_out = [s for p in comm for s in p.out_shapes]
    c_sem = [s for p in comm for s in p.sem_shapes]

    def wrapped(*refs):
        ins, outs = refs[:n_in], refs[n_in + len(c_in):n_in + len(c_in) + n_out]
        pos = [n_in, n_in + len(c_in) + n_out, n_in + len(c_in) + n_out + len(c_out)]
        own = refs[pos[2]:pos[2] + len(scratch)]
        pos[2] += len(scratch)
        split = []
        for p in comm:
            sizes = [len(p.inputs), len(p.out_shapes), len(p.sem_shapes)]
            split.append([refs[o:o + n] for o, n in zip(pos, sizes)])
            pos = [o + n for o, n in zip(pos, sizes)]
        step, steps = 0, 1
        for a, g in enumerate(grid):
            step, steps = step * g + pl.program_id(a), steps * g

        def run(which, at):
            def go():
                for p, cut in zip(comm, split):
                    getattr(p, which)(*cut)
            if not comm:
                return
            if grid:
                pl.when(step == at)(go)
            else:
                go()

        run("start", 0)
        body(*ins, *outs, *own)
        run("mid", steps // 2)
        run("late", max(steps // 2, steps - 1 - max(1, steps // 8)))
        run("finish", steps - 1)

    res = pl.pallas_call(
        wrapped, name=name, grid=grid, in_specs=list(in_specs) + [ANY] * len(c_in),
        out_specs=o_specs + [ANY] * len(c_out), out_shape=o_shape + c_out, scratch_shapes=list(scratch) + c_sem,
        compiler_params=_params(len(grid)),
    )(*args, *c_in)
    pos = n_out
    for p in comm:
        p.results = list(res[pos:pos + len(p.out_shapes)])
        pos += len(p.out_shapes)
    return res[0] if single else list(res[:n_out])


def _comm_only(name, comm):
    def body(o_ref):
        o_ref[...] = jnp.zeros_like(o_ref)

    _call(name, body, (), [], pl.BlockSpec(memory_space=pltpu.VMEM), jax.ShapeDtypeStruct((8, 128), F32), [], comm)


def _mm(name, grid, prods, extras, outs, epi, nacc=1, comm=()):
    n_p, n_e = len(prods), len(extras)

    def body(*refs):
        ab, ex, out = refs[:2 * n_p], refs[2 * n_p:2 * n_p + n_e], refs[2 * n_p + n_e:]
        accs = [None] * nacc
        for i, prod in enumerate(prods):
            dn, acc, loop = prod[6], prod[7], prod[8]
            a_ref, b_ref = ab[2 * i], ab[2 * i + 1]
            if loop:
                for g in range(loop):
                    t = lax.dot_general(a_ref[g], b_ref[g], _DN[dn], preferred_element_type=F32)
                    accs[acc] = t if accs[acc] is None else accs[acc] + t
            else:
                t = lax.dot_general(a_ref[...], b_ref[...], _DN[dn], preferred_element_type=F32)
                accs[acc] = t if accs[acc] is None else accs[acc] + t
        epi(accs, ex, out)

    in_specs, args = [], []
    for prod in prods:
        in_specs += [pl.BlockSpec(prod[1], prod[2]), pl.BlockSpec(prod[4], prod[5])]
        args += [prod[0], prod[3]]
    for e, e_blk, e_map in extras:
        in_specs.append(pl.BlockSpec(e_blk, e_map))
        args.append(e)
    return _call(name, body, grid, in_specs, [pl.BlockSpec(blk, mp) for _, _, blk, mp in outs],
                 [jax.ShapeDtypeStruct(s, d) for s, d, _, _ in outs], args, comm)


def _store(accs, ex, out):
    out[0][...] = accs[0].astype(out[0].dtype)


def _ew_tile(r, c, budget=3 << 19):
    for t in range(r - r % 16, 0, -16):
        if r % t == 0 and t * c * 4 <= budget:
            return t, c
    for t in range(c - c % 128, 0, -128):
        if c % t == 0 and r * t * 4 <= budget:
            return r, t
    return r, c


def _tile(n, want):
    t = min(n, want)
    assert n % t == 0, (n, want)
    return t


def _mm_nn(name, a, b, out_dtype, tm=512, tn=512, comm=()):
    m, k = a.shape
    n = b.shape[1]
    tm, tn = _tile(m, tm), (tn if n % tn == 0 else n)
    return _mm(name, (n // tn, m // tm),
               [(a, (tm, k), lambda j, i: (i, 0), b, (k, tn), lambda j, i: (0, j), "nn", 0, 0)], [],
               [((m, n), out_dtype, (tm, tn), lambda j, i: (i, j))], _store, comm=comm)[0]


def _mm_nt(name, a, bt, out_dtype, tm=512, tn=512, comm=(), rows=None):
    m, k = a.shape
    n = rows or bt.shape[0]
    tm, tn = _tile(m, tm), (tn if n % tn == 0 else n)
    return _mm(name, (n // tn, m // tm),
               [(a, (tm, k), lambda j, i: (i, 0), bt, (tn, k), lambda j, i: (j, 0), "nt", 0, 0)], [],
               [((m, n), out_dtype, (tm, tn), lambda j, i: (i, j))], _store, comm=comm)[0]


def _mm_tn_into(name, a, b, buf, row0, ta=1024, tb=512):
    t, ka = a.shape
    nb = b.shape[1]
    ta, tb = (ta if ka % ta == 0 else ka), (tb if nb % tb == 0 else nb)

    def body(a_ref, b_ref, buf_in, buf_out, tile, sem):
        i, j = pl.program_id(0), pl.program_id(1)
        tile[...] = lax.dot_general(a_ref[...], b_ref[...], _DN["tn"], preferred_element_type=F32).astype(tile.dtype)
        rows = pl.ds(pl.multiple_of(row0 + i * ta, 16), ta)
        cp = pltpu.make_async_copy(tile, buf_out.at[rows, pl.ds(pl.multiple_of(j * tb, 128), tb)], sem)
        cp.start()
        cp.wait()

    return pl.pallas_call(
        body, name=name, grid=(ka // ta, nb // tb),
        in_specs=[pl.BlockSpec((t, ta), lambda i, j: (0, i)), pl.BlockSpec((t, tb), lambda i, j: (0, j)), ANY],
        out_specs=ANY, out_shape=jax.ShapeDtypeStruct(buf.shape, buf.dtype), input_output_aliases={2: 0},
        scratch_shapes=[pltpu.VMEM((ta, tb), buf.dtype), pltpu.SemaphoreType.DMA],
        compiler_params=_params(2))(a, b, buf)


def _mm_tn(name, a, b, out_dtype, ta=512, tb=512, scale=None):
    t, ka = a.shape
    nb = b.shape[1]
    ta, tb = (ta if ka % ta == 0 else ka), (tb if nb % tb == 0 else nb)

    def epi(accs, ex, out):
        v = accs[0] if scale is None else accs[0] * scale
        out[0][...] = v.astype(out[0].dtype)

    return _mm(name, (ka // ta, nb // tb),
               [(a, (t, ta), lambda i, j: (0, i), b, (t, tb), lambda i, j: (0, j), "tn", 0, 0)], [],
               [((ka, nb), out_dtype, (ta, tb), lambda i, j: (i, j))], epi)[0]


def _rms_fwd(name, x, g, tm=256, comm=()):
    s, d = x.shape
    tm = _tile(s, tm)

    def body(x_ref, g_ref, o_ref):
        v = x_ref[...]
        o_ref[...] = (v * lax.rsqrt(jnp.mean(v * v, axis=-1, keepdims=True) + EPS) * g_ref[...]).astype(o_ref.dtype)

    return _call(name, body, (s // tm,),
                 [pl.BlockSpec((tm, d), lambda i: (i, 0)), pl.BlockSpec((1, d), lambda i: (0, 0))],
                 pl.BlockSpec((tm, d), lambda i: (i, 0)), jax.ShapeDtypeStruct((s, d), BF), [x, g], comm)


def _acc_rows(ref, part, i):
    @pl.when(i == 0)
    def _():
        ref[...] = part

    @pl.when(i > 0)
    def _():
        ref[...] += part


def _rms_bwd_math(dn, v, g):
    rstd = lax.rsqrt(jnp.mean(v * v, axis=-1, keepdims=True) + EPS)
    xh = v * rstd
    dxh = dn * g
    dx = rstd * (dxh - xh * jnp.mean(dxh * xh, axis=-1, keepdims=True))
    return dx, jnp.sum(dn * xh, axis=0, keepdims=True)


def _rms_bwd(name, dn, x, g, resid, tm=256, comm=()):
    s, d = x.shape
    tm = _tile(s, tm)

    def body(dn_ref, x_ref, g_ref, r_ref, dx_ref, dxb_ref, dg_ref):
        dx, part = _rms_bwd_math(dn_ref[...].astype(F32), x_ref[...], g_ref[...])
        tot = r_ref[...] + dx
        dx_ref[...] = tot
        dxb_ref[...] = tot.astype(BF)
        _acc_rows(dg_ref, part, pl.program_id(0))

    row = pl.BlockSpec((tm, d), lambda i: (i, 0))
    one = pl.BlockSpec((1, d), lambda i: (0, 0))
    return _call(name, body, (s // tm,), [row, row, one, row], [row, row, one],
                 [jax.ShapeDtypeStruct((s, d), F32), jax.ShapeDtypeStruct((s, d), BF),
                  jax.ShapeDtypeStruct((1, d), F32)], [dn, x, g, resid], comm)


def _loss_head(h, target, g, tm=256):
    s, d = h.shape
    tm = _tile(s, tm)

    def body(h_ref, t_ref, g_ref, dh_ref, dg_ref, loss_ref):
        v, gv = h_ref[...], g_ref[...]
        rstd = lax.rsqrt(jnp.mean(v * v, axis=-1, keepdims=True) + EPS)
        xh = v * rstd
        err = xh * gv - t_ref[...]
        part_loss = 0.5 * jnp.sum(jnp.mean(err * err, axis=-1, keepdims=True), axis=0, keepdims=True)
        dy = err * (1.0 / d)
        dxh = dy * gv
        dh_ref[...] = rstd * (dxh - xh * jnp.mean(dxh * xh, axis=-1, keepdims=True))
        i = pl.program_id(0)
        _acc_rows(dg_ref, jnp.sum(dy * xh, axis=0, keepdims=True), i)
        _acc_rows(loss_ref, jnp.broadcast_to(part_loss, loss_ref.shape), i)

    row = pl.BlockSpec((tm, d), lambda i: (i, 0))
    one = pl.BlockSpec((1, d), lambda i: (0, 0))
    return pl.pallas_call(
        body, name="loss_head", grid=(s // tm,), in_specs=[row, row, one],
        out_specs=[row, one, pl.BlockSpec((1, 128), lambda i: (0, 0))],
        out_shape=[jax.ShapeDtypeStruct((s, d), F32), jax.ShapeDtypeStruct((1, d), F32),
                   jax.ShapeDtypeStruct((1, 128), F32)],
        compiler_params=_params(1))(h, target, g)


def _pl_bwd_elem(dh, pe, t, tm=256):
    s, d = dh.shape
    tm = _tile(s, tm)

    def body(dh_ref, pe_ref, t_ref, dt_ref, dpe_ref):
        dh_v, sg = dh_ref[...], _sig(t_ref[...])
        dt_ref[...] = (dh_v * pe_ref[...].astype(F32) * sg * (1.0 - sg)).astype(BF)
        dpe_ref[...] = (dh_v * sg).astype(BF)

    row = pl.BlockSpec((tm, d), lambda i: (i, 0))
    return pl.pallas_call(
        body, name="pl_bwd_elem", grid=(s // tm,), in_specs=[row, row, row], out_specs=[row, row],
        out_shape=[jax.ShapeDtypeStruct((s, d), BF)] * 2, compiler_params=_params(1))(dh, pe, t)


def _ffn_up(name, xn, wg, wu, tm=1024, comm=()):
    s, d = xn.shape
    g, fb, _ = wg.shape
    tm = _tile(s, tm)

    def epi(accs, ex, out):
        hg, hu = accs
        out[0][...] = hg.astype(BF)
        out[1][...] = hu.astype(BF)
        out[2][...] = (hg * _sig(hg) * hu).astype(BF)

    a_map = lambda j, i: (i, 0)
    w_map = lambda j, i: (j, 0, 0)
    o = ((g, s, fb), BF, (None, tm, fb), lambda j, i: (j, i, 0))
    return _mm(name, (g, s // tm),
               [(xn, (tm, d), a_map, wg, (None, fb, d), w_map, "nt", 0, 0),
                (xn, (tm, d), a_map, wu, (None, fb, d), w_map, "nt", 1, 0)], [], [o, o, o], epi, nacc=2, comm=comm)


def _ffn_down(name, a, wd, resid, tm=512, tn=512, comm=()):
    g, s, fb = a.shape
    d = wd.shape[2]
    tm, tn = _tile(s, tm), _tile(d, tn)

    def epi(accs, ex, out):
        out[0][...] = ex[0][...] + 0.5 * accs[0]

    return _mm(name, (d // tn, s // tm),
               [(a, (g, tm, fb), lambda j, i: (0, i, 0), wd, (g, fb, tn), lambda j, i: (0, 0, j), "nn", 0, g)],
               [(resid, (tm, tn), lambda j, i: (i, j))],
               [((s, d), F32, (tm, tn), lambda j, i: (i, j))], epi, comm=comm)[0]


def _ffn_bwd_act(name, dh, wd, hg, hu, tm=1024, comm=()):
    s, d = dh.shape
    g, fb, _ = wd.shape
    tm = _tile(s, tm)

    def epi(accs, ex, out):
        da = 0.5 * accs[0]
        hg_v, hu_v = ex[0][...].astype(F32), ex[1][...].astype(F32)
        sg = _sig(hg_v)
        out[0][...] = (da * hu_v * (sg * (1.0 + hg_v * (1.0 - sg)))).astype(BF)
        out[1][...] = (da * (hg_v * sg)).astype(BF)

    blk = (None, tm, fb)
    gmap = lambda j, i: (j, i, 0)
    return _mm(name, (g, s // tm),
               [(dh, (tm, d), lambda j, i: (i, 0), wd, (None, fb, d), lambda j, i: (j, 0, 0), "nt", 0, 0)],
               [(hg, blk, gmap), (hu, blk, gmap)],
               [((g, s, fb), BF, blk, gmap), ((g, s, fb), BF, blk, gmap)], epi, comm=comm)


def _ffn_bwd_wd(name, a, dh, tn=1024, comm=()):
    g, s, fb = a.shape
    d = dh.shape[1]
    tn = _tile(d, tn)

    def epi(accs, ex, out):
        out[0][...] = (0.5 * accs[0]).astype(BF)

    return _mm(name, (g, d // tn),
               [(a, (None, s, fb), lambda j, i: (j, 0, 0), dh, (s, tn), lambda j, i: (0, i), "tn", 0, 0)], [],
               [((g, fb, d), BF, (None, fb, tn), lambda j, i: (j, 0, i))], epi, comm=comm)[0]


def _ffn_bwd_wup(name, xn, dhg, dhu, tk=1024, comm=()):
    s, d = xn.shape
    g, _, fb = dhg.shape
    tk = _tile(d, tk)

    def epi(accs, ex, out):
        out[0][...] = accs[0].astype(BF)
        out[1][...] = accs[1].astype(BF)

    a_map = lambda j, i: (j, 0, 0)
    b_map = lambda j, i: (0, i)
    o = ((g, fb, d), BF, (None, fb, tk), lambda j, i: (j, 0, i))
    return _mm(name, (g, d // tk),
               [(dhg, (None, s, fb), a_map, xn, (s, tk), b_map, "tn", 0, 0),
                (dhu, (None, s, fb), a_map, xn, (s, tk), b_map, "tn", 1, 0)], [], [o, o], epi, nacc=2, comm=comm)


def _ffn_bwd_x(name, dhg, dhu, wg, wu, tm=512, tn=512, comm=()):
    g, s, fb = dhg.shape
    d = wg.shape[2]
    tm, tn = _tile(s, tm), _tile(d, tn)
    a_blk, a_map = (g, tm, fb), lambda j, i: (0, i, 0)
    b_blk, b_map = (g, fb, tn), lambda j, i: (0, 0, j)
    return _mm(name, (d // tn, s // tm),
               [(dhg, a_blk, a_map, wg, b_blk, b_map, "nn", 0, g), (dhu, a_blk, a_map, wu, b_blk, b_map, "nn", 0, g)],
               [], [((s, d), F32, (tm, tn), lambda j, i: (i, j))], _store, comm=comm)[0]


def _ffn_forward(tag, h, gain, get_wgu, get_wd, norm_comm=(), up_comm=(), down_comm=()):
    xn = _rms_fwd(tag + "_norm", h, gain, comm=norm_comm)
    wg, wu = get_wgu()
    hg, hu, a = _ffn_up(tag + "_up", xn, wg, wu, comm=up_comm)
    return _ffn_down(tag + "_down", a, get_wd(), h, comm=down_comm), (xn, hg, hu, a)


def _na_geometry(rows):
    kh = min(NA_ROWS_WIN, rows)
    cols = np.arange(GRID_W)
    col_start = np.clip(cols - NA_COLS_WIN // 2, 0, GRID_W - NA_COLS_WIN)
    mask = (cols[None, :] >= col_start[:, None]) & (cols[None, :] < col_start[:, None] + NA_COLS_WIN)
    dc = np.clip(cols[None, :] - cols[:, None], -(NA_COLS_WIN - 1), NA_COLS_WIN - 1) + (NA_COLS_WIN - 1)
    return kh, mask, dc


def _na_table(rpb, rows):
    _, mask, dc = _na_geometry(rows)
    nd, nc, cells = 2 * NA_ROWS_WIN - 1, 2 * NA_COLS_WIN - 1, GRID_W * GRID_W
    onehot = np.zeros((128, cells), np.float32)
    onehot[dc.reshape(-1), np.arange(cells)] = mask.reshape(-1).astype(np.float32)
    off = np.where(mask.reshape(1, -1), 0.0, NEG).astype(np.float32)

    def body(r_ref, e_ref, off_ref, o_ref):
        o_ref[...] = jnp.dot(r_ref[...], e_ref[...], precision=HI, preferred_element_type=F32) + off_ref[...]

    flat = pl.pallas_call(body, name="na_table", out_shape=jax.ShapeDtypeStruct((NA_HEADS * nd, cells), F32),
                          compiler_params=_params(0))(
        jnp.pad(rpb.reshape(NA_HEADS * nd, nc), ((0, 0), (0, 128 - nc))), jnp.asarray(onehot), jnp.asarray(off))
    return flat.reshape(NA_HEADS, nd, GRID_W, GRID_W)


class _NaPlan:
    def __init__(self, s):
        self.s, self.rows = s, s // GRID_W
        self.kh = min(NA_ROWS_WIN, self.rows)
        self.qr = min(NA_QROWS, self.rows)
        self.kr = min(self.rows, self.kh + self.qr - 1)
        self.groups = self.rows // self.qr
        self.nd = 2 * NA_ROWS_WIN - 1
        self.hw, self.nq = NA_HG * NA_DIM, NA_HEADS // NA_HG
        clip = lambda v, hi: min(max(v, 0), hi)
        pats = [(clip(g * self.qr - self.kh // 2, self.rows - self.kr) - g * self.qr,)
                + tuple(clip(g * self.qr + a - self.kh // 2, self.rows - self.kh) - g * self.qr for a in range(self.qr))
                for g in range(self.groups)]
        self.rebuild = [g for g in range(self.groups) if g == 0 or pats[g] != pats[g - 1]]

    def first_key_row(self, g):
        return jnp.clip(g * self.qr - self.kh // 2, 0, self.rows - self.kr)

    def specs(self):
        blk = pl.BlockSpec((self.qr * GRID_W, self.hw), lambda j, g: (g, j))
        k_spec = pl.BlockSpec((self.s, self.hw), lambda j, g: (0, self.nq + j))
        v_spec = pl.BlockSpec((self.s, self.hw), lambda j, g: (0, 2 * self.nq + j))
        t_spec = pl.BlockSpec((NA_HG, self.nd, GRID_W, GRID_W), lambda j, g: (j, 0, 0, 0))
        return blk, k_spec, v_spec, t_spec

    def bias_scratch(self):
        return pltpu.VMEM((NA_HG, self.qr * GRID_W, self.kr * GRID_W), F32)

    def fill_bias(self, t_ref, bias_ref, g):
        def build():
            r0, ks = g * self.qr, self.first_key_row(g)
            for a in range(self.qr):
                rs = jnp.clip(r0 + a - self.kh // 2, 0, self.rows - self.kh)
                for i in range(self.kr):
                    valid = jnp.logical_and(ks + i >= rs, ks + i < rs + self.kh)
                    idx = jnp.clip(ks + i - r0 - a + NA_ROWS_WIN - 1, 0, self.nd - 1)
                    for h in range(NA_HG):
                        bias_ref[h, a * GRID_W:(a + 1) * GRID_W, i * GRID_W:(i + 1) * GRID_W] = jnp.where(
                            valid, t_ref[h, idx], NEG)

        pl.when(functools.reduce(jnp.logical_or, [g == r for r in self.rebuild]))(build)

    def window(self, g):
        return pl.ds(pl.multiple_of(self.first_key_row(g) * GRID_W, GRID_W), self.kr * GRID_W)


def _na_probs(q, k, bias):
    sc = lax.dot_general(q, k, _DN["nt"], preferred_element_type=F32) * (NA_DIM ** -0.5) + bias
    e = jnp.exp(sc - jnp.max(sc, axis=-1, keepdims=True))
    return e / jnp.sum(e, axis=-1, keepdims=True)


def _na_fwd(qkv, table, comm=()):
    plan = _NaPlan(qkv.shape[0])
    blk, k_spec, v_spec, t_spec = plan.specs()

    def body(q_ref, k_ref, v_ref, t_ref, o_ref, bias_ref):
        g = pl.program_id(1)
        plan.fill_bias(t_ref, bias_ref, g)
        win = plan.window(g)
        for h in range(NA_HG):
            cs = slice(h * NA_DIM, (h + 1) * NA_DIM)
            p = _na_probs(q_ref[:, cs], k_ref[win, cs], bias_ref[h])
            o_ref[:, cs] = jnp.dot(p.astype(BF), v_ref[win, cs], preferred_element_type=F32).astype(BF)

    return _call("na_fwd", body, (plan.nq, plan.groups), [blk, k_spec, v_spec, t_spec], blk,
                 jax.ShapeDtypeStruct((plan.s, NA_HEADS * NA_DIM), BF), [qkv, qkv, qkv, table], comm,
                 scratch=[plan.bias_scratch()])


def _na_bwd(qkv, table, do, comm=()):
    plan = _NaPlan(qkv.shape[0])
    blk, k_spec, v_spec, t_spec = plan.specs()
    qr, kr = plan.qr, plan.kr

    def body(q_ref, k_ref, v_ref, t_ref, do_ref, dq_ref, dk_ref, dv_ref, dt_ref, bias_ref):
        g = pl.program_id(1)

        @pl.when(g == 0)
        def _():
            dk_ref[...] = jnp.zeros_like(dk_ref)
            dv_ref[...] = jnp.zeros_like(dv_ref)
            dt_ref[...] = jnp.zeros_like(dt_ref)

        plan.fill_bias(t_ref, bias_ref, g)
        win = plan.window(g)
        base = plan.first_key_row(g) - g * qr + NA_ROWS_WIN - 1
        for h in range(NA_HG):
            cs = slice(h * NA_DIM, (h + 1) * NA_DIM)
            q, k, v, do_h = q_ref[:, cs], k_ref[win, cs], v_ref[win, cs], do_ref[:, cs]
            p = _na_probs(q, k, bias_ref[h])
            dp = lax.dot_general(do_h, v, _DN["nt"], preferred_element_type=F32)
            ds = p * (dp - jnp.sum(p * dp, axis=-1, keepdims=True))
            for dlt in range(1 - qr, kr):
                tiles = [ds[a * GRID_W:(a + 1) * GRID_W, (a + dlt) * GRID_W:(a + dlt + 1) * GRID_W]
                         for a in range(qr) if 0 <= a + dlt < kr]
                dt_ref[h, jnp.clip(base + dlt, 0, plan.nd - 1)] += functools.reduce(jnp.add, tiles)
            dsb = (ds * (NA_DIM ** -0.5)).astype(BF)
            dq_ref[:, cs] = jnp.dot(dsb, k, preferred_element_type=F32).astype(BF)
            dk_ref[win, cs] += lax.dot_general(dsb, q, _DN["tn"], preferred_element_type=F32)
            dv_ref[win, cs] += lax.dot_general(p.astype(BF), do_h, _DN["tn"], preferred_element_type=F32)

    width = NA_HEADS * NA_DIM
    whole = pl.BlockSpec((plan.s, plan.hw), lambda j, g: (0, j))
    return _call(
        "na_bwd", body, (plan.nq, plan.groups), [blk, k_spec, v_spec, t_spec, blk], [blk, whole, whole, t_spec],
        [jax.ShapeDtypeStruct((plan.s, width), BF), jax.ShapeDtypeStruct((plan.s, width), F32),
         jax.ShapeDtypeStruct((plan.s, width), F32),
         jax.ShapeDtypeStruct((NA_HEADS, plan.nd, GRID_W, GRID_W), F32)],
        [qkv, qkv, qkv, table, do], comm, scratch=[plan.bias_scratch()])


def _na_rpb_grad(dt, rows):
    _, mask, dc = _na_geometry(rows)
    nd, nc = 2 * NA_ROWS_WIN - 1, 2 * NA_COLS_WIN - 1
    onehot = np.zeros((GRID_W * GRID_W, 128), np.float32)
    onehot[np.arange(GRID_W * GRID_W), dc.reshape(-1)] = mask.reshape(-1).astype(np.float32)
    flat = dt.reshape(NA_HEADS * nd, GRID_W * GRID_W)

    def body(a_ref, e_ref, o_ref):
        o_ref[...] = jnp.dot(a_ref[...], e_ref[...], precision=HI, preferred_element_type=F32)

    out = pl.pallas_call(body, name="na_rpb_grad", out_shape=jax.ShapeDtypeStruct((NA_HEADS * nd, 128), F32),
                         compiler_params=_params(0))(flat, jnp.asarray(onehot))
    return out[:, :nc].reshape(NA_HEADS, nd, nc)


def _rope_consts(s):
    pos = np.arange(s, dtype=np.float32)
    inv = (1.0 / (ROPE_THETA ** (np.arange(0, ML_ROPE, 2, dtype=np.float32) / ML_ROPE))).astype(np.float32)
    ang = pos[:, None] * inv[None, :]
    cos, sin = np.cos(ang).astype(np.float32), np.sin(ang).astype(np.float32)
    half = ML_ROPE // 2
    rot = np.zeros((ML_ROPE, ML_ROPE), np.float32)
    rot[np.arange(half) + half, np.arange(half)] = -1.0
    rot[np.arange(half), np.arange(half) + half] = 1.0
    return (jnp.asarray(np.concatenate([cos, cos], 1)), jnp.asarray(np.concatenate([sin, sin], 1)),
            jnp.asarray(rot), jnp.asarray(rot.T.copy()))


def _rope(v, cos, sin, rot):
    return v * cos + jnp.dot(v, rot, precision=HI, preferred_element_type=F32) * sin


def _unrope(dv, cos, sin, rot_t):
    return dv * cos + jnp.dot(dv * sin, rot_t, precision=HI, preferred_element_type=F32)


def _rms(v, g):
    return v * lax.rsqrt(jnp.mean(v * v, axis=-1, keepdims=True) + EPS) * g


def _mla_prep(lat, gq, gkv, cos, sin, rot, tm=256):
    s, w = lat.shape
    tm = _tile(s, tm)

    def body(l_ref, gq_ref, gkv_ref, c_ref, s_ref, r_ref, cq_ref, ckv_ref, kr_ref):
        cq_ref[...] = _rms(l_ref[:, :ML_RANK], gq_ref[...]).astype(BF)
        ckv_ref[...] = _rms(l_ref[:, ML_RANK:2 * ML_RANK], gkv_ref[...]).astype(BF)
        kr_ref[...] = _rope(l_ref[:, 2 * ML_RANK:], c_ref[...], s_ref[...], r_ref[...]).astype(BF)

    row = lambda c: pl.BlockSpec((tm, c), lambda i: (i, 0))
    full = lambda a: pl.BlockSpec(a.shape, lambda i: (0, 0))
    return pl.pallas_call(
        body, name="mla_prep", grid=(s // tm,),
        in_specs=[row(w), full(gq), full(gkv), row(ML_ROPE), row(ML_ROPE), full(rot)],
        out_specs=[row(ML_RANK), row(ML_RANK), row(ML_ROPE)],
        out_shape=[jax.ShapeDtypeStruct((s, ML_RANK), BF), jax.ShapeDtypeStruct((s, ML_RANK), BF),
                   jax.ShapeDtypeStruct((s, ML_ROPE), BF)],
        compiler_params=_params(1))(lat, gq, gkv, cos, sin, rot)


def _mla_q_proj(cq, wuq, cos, sin, rot, tm=512, comm=()):
    s, k = cq.shape
    tm = _tile(s, tm)

    def epi(accs, ex, out):
        acc = accs[0]
        out[0][:, :ML_NOPE] = acc[:, :ML_NOPE].astype(BF)
        out[0][:, ML_NOPE:] = _rope(acc[:, ML_NOPE:], ex[0][...], ex[1][...], ex[2][...]).astype(BF)

    rmap = lambda j, i: (i, 0)
    return _mm("mla_q_proj", (ML_HEADS, s // tm),
               [(cq, (tm, k), rmap, wuq, (None, ML_QK, k), lambda j, i: (j, 0, 0), "nt", 0, 0)],
               [(cos, (tm, ML_ROPE), rmap), (sin, (tm, ML_ROPE), rmap), (rot, rot.shape, lambda j, i: (0, 0))],
               [((ML_HEADS, s, ML_QK), BF, (None, tm, ML_QK), lambda j, i: (j, i, 0))], epi, comm=comm)[0]


def _mla_kv_proj(ckv, wukv, kr, tm=512, comm=()):
    s, k = ckv.shape
    tm = _tile(s, tm)

    def epi(accs, ex, out):
        acc = accs[0]
        out[0][:, :ML_NOPE] = acc[:, :ML_NOPE].astype(BF)
        out[0][:, ML_NOPE:] = ex[0][...]
        out[1][...] = acc[:, ML_NOPE:].astype(BF)

    rmap = lambda j, i: (i, 0)
    gmap = lambda j, i: (j, i, 0)
    return _mm("mla_kv_proj", (ML_HEADS, s // tm),
               [(ckv, (tm, k), rmap, wukv, (None, k, ML_NOPE + ML_V), lambda j, i: (j, 0, 0), "nn", 0, 0)],
               [(kr, (tm, ML_ROPE), rmap)],
               [((ML_HEADS, s, ML_QK), BF, (None, tm, ML_QK), gmap), ((ML_HEADS, s, ML_V), BF, (None, tm, ML_V), gmap)],
               epi, comm=comm)


def _mla_probs(q, k):
    sc = lax.dot_general(q, k, _DN["nt"], preferred_element_type=F32) * (ML_QK ** -0.5)
    e = jnp.exp(sc - jnp.max(sc, axis=-1, keepdims=True))
    return e / jnp.sum(e, axis=-1, keepdims=True)


def _mla_fwd(q, k, v, tq=1024, comm=()):
    _, s, _ = q.shape
    tq = _tile(s, tq)

    def body(q_ref, k_ref, v_ref, o_ref):
        p = _mla_probs(q_ref[...], k_ref[...])
        o_ref[...] = jnp.dot(p.astype(BF), v_ref[...], preferred_element_type=F32).astype(BF)

    return _call("mla_fwd", body, (ML_HEADS, s // tq),
                 [pl.BlockSpec((None, tq, ML_QK), lambda h, i: (h, i, 0)),
                  pl.BlockSpec((None, s, ML_QK), lambda h, i: (h, 0, 0)),
                  pl.BlockSpec((None, s, ML_V), lambda h, i: (h, 0, 0))],
                 pl.BlockSpec((tq, ML_V), lambda h, i: (i, h)),
                 jax.ShapeDtypeStruct((s, ML_HEADS * ML_V), BF), [q, k, v], comm)


def _mla_bwd(q, k, v, do, tq=1024, comm=()):
    _, s, _ = q.shape
    tq = _tile(s, tq)

    def body(q_ref, k_ref, v_ref, do_ref, dq_ref, dk_ref, dv_ref):
        i = pl.program_id(1)
        qv, kv, vv, dov = q_ref[...], k_ref[...], v_ref[...], do_ref[...]
        p = _mla_probs(qv, kv)
        dp = lax.dot_general(dov, vv, _DN["nt"], preferred_element_type=F32)
        ds = (p * (dp - jnp.sum(p * dp, axis=-1, keepdims=True)) * (ML_QK ** -0.5)).astype(BF)
        dq_ref[...] = jnp.dot(ds, kv, preferred_element_type=F32)
        _acc_rows(dk_ref, lax.dot_general(ds, qv, _DN["tn"], preferred_element_type=F32), i)
        _acc_rows(dv_ref, lax.dot_general(p.astype(BF), dov, _DN["tn"], preferred_element_type=F32), i)

    return _call(
        "mla_bwd", body, (ML_HEADS, s // tq),
        [pl.BlockSpec((None, tq, ML_QK), lambda h, i: (h, i, 0)),
         pl.BlockSpec((None, s, ML_QK), lambda h, i: (h, 0, 0)),
         pl.BlockSpec((None, s, ML_V), lambda h, i: (h, 0, 0)),
         pl.BlockSpec((tq, ML_V), lambda h, i: (i, h))],
        [pl.BlockSpec((None, tq, ML_QK), lambda h, i: (h, i, 0)),
         pl.BlockSpec((None, s, ML_QK), lambda h, i: (h, 0, 0)),
         pl.BlockSpec((None, s, ML_V), lambda h, i: (h, 0, 0))],
        [jax.ShapeDtypeStruct((ML_HEADS, s, ML_QK), F32), jax.ShapeDtypeStruct((ML_HEADS, s, ML_QK), F32),
         jax.ShapeDtypeStruct((ML_HEADS, s, ML_V), F32)],
        [q, k, v, do], comm)


def _mla_post(dq, dk, dv, cos, sin, rot_t, tm=1024):
    _, s, _ = dq.shape
    tm = _tile(s, tm)

    def body(dq_ref, dk_ref, dv_ref, c_ref, s_ref, r_ref, dqp_ref, dkv_ref, dkr_ref):
        h = pl.program_id(1)
        dqv, dkk = dq_ref[...], dk_ref[...]
        dqp_ref[:, :ML_NOPE] = dqv[:, :ML_NOPE].astype(BF)
        dqp_ref[:, ML_NOPE:] = _unrope(dqv[:, ML_NOPE:], c_ref[...], s_ref[...], r_ref[...]).astype(BF)
        dkv_ref[:, :ML_NOPE] = dkk[:, :ML_NOPE].astype(BF)
        dkv_ref[:, ML_NOPE:] = dv_ref[...].astype(BF)
        _acc_rows(dkr_ref, dkk[:, ML_NOPE:], h)

    gspec = lambda c: pl.BlockSpec((None, tm, c), lambda i, h: (h, i, 0))
    rspec = pl.BlockSpec((tm, ML_ROPE), lambda i, h: (i, 0))
    return pl.pallas_call(
        body, name="mla_post", grid=(s // tm, ML_HEADS),
        in_specs=[gspec(ML_QK), gspec(ML_QK), gspec(ML_V), rspec, rspec,
                  pl.BlockSpec(rot_t.shape, lambda i, h: (0, 0))],
        out_specs=[gspec(ML_QK), gspec(ML_NOPE + ML_V), rspec],
        out_shape=[jax.ShapeDtypeStruct((ML_HEADS, s, ML_QK), BF),
                   jax.ShapeDtypeStruct((ML_HEADS, s, ML_NOPE + ML_V), BF),
                   jax.ShapeDtypeStruct((s, ML_ROPE), F32)],
        compiler_params=_params(2))(dq, dk, dv, cos, sin, rot_t)


def _mla_lat_bwd(dcq, dckv, dkr, lat, gq, gkv, cos, sin, rot_t, tm=256):
    s, w = lat.shape
    tm = _tile(s, tm)

    def body(dcq_ref, dckv_ref, dkr_ref, l_ref, gq_ref, gkv_ref, c_ref, s_ref, r_ref, dl_ref, dgq_ref, dgkv_ref):
        i = pl.program_id(0)
        dql, pq = _rms_bwd_math(dcq_ref[...], l_ref[:, :ML_RANK], gq_ref[...])
        dkl, pkv = _rms_bwd_math(dckv_ref[...], l_ref[:, ML_RANK:2 * ML_RANK], gkv_ref[...])
        dl_ref[:, :ML_RANK] = dql.astype(BF)
        dl_ref[:, ML_RANK:2 * ML_RANK] = dkl.astype(BF)
        dl_ref[:, 2 * ML_RANK:] = _unrope(dkr_ref[...], c_ref[...], s_ref[...], r_ref[...]).astype(BF)
        _acc_rows(dgq_ref, pq, i)
        _acc_rows(dgkv_ref, pkv, i)

    row = lambda c: pl.BlockSpec((tm, c), lambda i: (i, 0))
    full = lambda a: pl.BlockSpec(a.shape, lambda i: (0, 0))
    return pl.pallas_call(
        body, name="mla_lat_bwd", grid=(s // tm,),
        in_specs=[row(ML_RANK), row(ML_RANK), row(ML_ROPE), row(w), full(gq), full(gkv), row(ML_ROPE), row(ML_ROPE),
                  full(rot_t)],
        out_specs=[row(w), full(gq), full(gkv)],
        out_shape=[jax.ShapeDtypeStruct((s, w), BF), jax.ShapeDtypeStruct(gq.shape, F32),
                   jax.ShapeDtypeStruct(gkv.shape, F32)],
        compiler_params=_params(1))(dcq, dckv, dkr, lat, gq, gkv, cos, sin, rot_t)


def _grp_dw(name, a, dout, ta=1024):
    s, k = a.shape
    ta = _tile(k, ta)
    if dout.ndim == 3:
        g, _, nb = dout.shape
        b_blk, b_map = (None, s, nb), lambda j, i: (j, 0, 0)
    else:
        g, nb = NDEV, dout.shape[1] // NDEV
        b_blk, b_map = (s, nb), lambda j, i: (0, j)
    return _mm(name, (g, k // ta),
               [(a, (s, ta), lambda j, i: (0, i), dout, b_blk, b_map, "tn", 0, 0)], [],
               [((g, k, nb), BF, (None, ta, nb), lambda j, i: (j, i, 0))], _store)[0]


def _grp_dw_t(name, dout, a, ta=512):
    g, s, nb = dout.shape
    k = a.shape[1]
    ta = _tile(k, ta)
    return _mm(name, (g, k // ta),
               [(dout, (None, s, nb), lambda j, i: (j, 0, 0), a, (s, ta), lambda j, i: (0, i), "tn", 0, 0)], [],
               [((g, nb, k), BF, (None, nb, ta), lambda j, i: (j, 0, i))], _store)[0]


def _grp_dx_t(name, dout, wt, tm=512, tn=512, comm=()):
    g, s, nb = dout.shape
    k = wt.shape[2]
    tm, tn = _tile(s, tm), _tile(k, tn)
    return _mm(name, (k // tn, s // tm),
               [(dout, (g, tm, nb), lambda j, i: (0, i, 0), wt, (g, nb, tn), lambda j, i: (0, 0, j), "nn", 0, g)], [],
               [((s, k), F32, (tm, tn), lambda j, i: (i, j))], _store, comm=comm)[0]


def _grp_dx(name, dout, w, tm=512, tn=512, out_dtype=F32, comm=()):
    g, s, nb = dout.shape
    k = w.shape[1]
    tm, tn = _tile(s, tm), _tile(k, tn)
    return _mm(name, (k // tn, s // tm),
               [(dout, (g, tm, nb), lambda j, i: (0, i, 0), w, (g, tn, nb), lambda j, i: (0, j, 0), "nt", 0, g)], [],
               [((s, k), out_dtype, (tm, tn), lambda j, i: (i, j))], _store, comm=comm)[0]


def _row_dw(name, a, dout, tn=2048):
    s, n = dout.shape
    tn = _tile(n, tn)
    if a.ndim == 3:
        kb = a.shape[2]
        a_blk, a_map = (None, s, kb), lambda j, i: (j, 0, 0)
    else:
        kb = a.shape[1] // NDEV
        a_blk, a_map = (s, kb), lambda j, i: (0, j)
    return _mm(name, (NDEV, n // tn),
               [(a, a_blk, a_map, dout, (s, tn), lambda j, i: (0, i), "tn", 0, 0)], [],
               [((NDEV, kb, n), BF, (None, kb, tn), lambda j, i: (j, 0, i))], _store)[0]


def _mix_merge(oa, ob, wa, wb, ga, gb, tm=1024, comm=()):
    s, k = oa.shape
    g, _, nb = wa.shape
    tm = _tile(s, tm)

    def epi(accs, ex, out):
        ya, yb = accs
        out[0][...] = ya.astype(BF)
        out[1][...] = yb.astype(BF)
        out[2][...] = (_sig(ex[0][...]) * ya + _sig(ex[1][...]) * yb).astype(BF)

    rmap = lambda j, i: (i, 0)
    wmap = lambda j, i: (j, 0, 0)
    o = ((g, s, nb), BF, (None, tm, nb), lambda j, i: (j, i, 0))
    cmap = lambda j, i: (i, j)
    return _mm("mix_merge", (g, s // tm),
               [(oa, (tm, k), rmap, wa, (None, k, nb), wmap, "nn", 0, 0),
                (ob, (tm, k), rmap, wb, (None, k, nb), wmap, "nn", 1, 0)],
               [(ga, (tm, nb), cmap), (gb, (tm, nb), cmap)], [o, o, o], epi, nacc=2, comm=comm)


def _mix_out(merged, wout, resid, tm=512, tn=512):
    g, s, kb = merged.shape
    d = wout.shape[2]
    tm, tn = _tile(s, tm), _tile(d, tn)

    def epi(accs, ex, out):
        out[0][...] = ex[0][...] + accs[0]

    return _mm("mix_out", (d // tn, s // tm),
               [(merged, (g, tm, kb), lambda j, i: (0, i, 0), wout, (g, kb, tn), lambda j, i: (0, 0, j), "nn", 0, g)],
               [(resid, (tm, tn), lambda j, i: (i, j))],
               [((s, d), F32, (tm, tn), lambda j, i: (i, j))], epi)[0]


def _mix_out_bwd(dh, wout, ga, gb, ya, yb, tm=1024, comm=()):
    s, d = dh.shape
    g, kb, _ = wout.shape
    tm = _tile(s, tm)

    def epi(accs, ex, out):
        dm = accs[0]
        sa, sb = _sig(ex[0][...]), _sig(ex[1][...])
        out[0][...] = (dm * sa).astype(BF)
        out[1][...] = (dm * sb).astype(BF)
        out[2][...] = (dm * ex[2][...].astype(F32) * sa * (1.0 - sa)).astype(BF)
        out[3][...] = (dm * ex[3][...].astype(F32) * sb * (1.0 - sb)).astype(BF)

    cmap = lambda j, i: (i, j)
    gmap = lambda j, i: (j, i, 0)
    og = ((g, s, kb), BF, (None, tm, kb), gmap)
    oc = ((s, g * kb), BF, (tm, kb), cmap)
    return _mm("mix_out_bwd", (g, s // tm),
               [(dh, (tm, d), lambda j, i: (i, 0), wout, (None, kb, d), lambda j, i: (j, 0, 0), "nt", 0, 0)],
               [(ga, (tm, kb), cmap), (gb, (tm, kb), cmap), (ya, (None, tm, kb), gmap), (yb, (None, tm, kb), gmap)],
               [og, og, oc, oc], epi, comm=comm)


def _pl_forward(n4, wplg, p, wpl, h3, tm=1024):
    s, d = n4.shape
    g, kb, _ = wplg.shape
    kp, nb = wpl.shape[1], wpl.shape[2]
    tm = _tile(s, tm)
    wplg_nat = wplg.reshape(g * kb, d)

    def epi(accs, ex, out):
        t, pe = accs
        out[0][...] = ex[0][...] + _sig(t) * pe
        out[1][...] = t
        out[2][...] = pe.astype(BF)

    rmap = lambda j, i: (i, 0)
    cmap = lambda j, i: (i, j)
    return _mm("pl_forward", (g, s // tm),
               [(n4, (tm, d), rmap, wplg_nat, (g * kb, nb), lambda j, i: (0, j), "nn", 0, 0),
                (p, (tm, kp), rmap, wpl, (None, kp, nb), lambda j, i: (j, 0, 0), "nn", 1, 0)],
               [(h3, (tm, nb), cmap)],
               [((s, d), F32, (tm, nb), cmap), ((s, d), F32, (tm, nb), cmap), ((s, d), BF, (tm, nb), cmap)],
               epi, nacc=2)


def _row_dx(name, dout, w, tm=1024, comm=()):
    s, n = dout.shape
    g, kb, _ = w.shape
    tm = _tile(s, tm)
    return _mm(name, (g, s // tm),
               [(dout, (tm, n), lambda j, i: (i, 0), w, (None, kb, n), lambda j, i: (j, 0, 0), "nt", 0, 0)], [],
               [((s, g * kb), F32, (tm, kb), lambda j, i: (i, j))], _store, comm=comm)[0]


def _in_proj_bwd_x(pieces, weights, tm=512, tn=512, comm=()):
    s = pieces[0].shape[0]
    d = weights[0].shape[1]
    tm, tn = _tile(s, tm), _tile(d, tn)
    prods = [(pc, (tm, pc.shape[1]), lambda j, i: (i, 0), w, (pc.shape[1], tn), lambda j, i: (0, j), "nn", 0, 0)
             for pc, w in zip(pieces, weights)]
    return _mm("in_proj_dx", (d // tn, s // tm), prods, [],
               [((s, d), F32, (tm, tn), lambda j, i: (i, j))], _store, comm=comm)[0]


def _split_w_in(w_in_t):
    g, nb, d = w_in_t.shape
    nat = w_in_t.reshape(g * nb, d)
    na, lat = 3 * NA_HEADS * NA_DIM, 2 * ML_RANK + ML_ROPE
    return nat, nat[na:na + lat], nat[na + lat:na + lat + d], nat[na + lat + d:]


def _pair_sum(name, part, landed, core):
    _, _, r, c = part.shape
    tr, tc = _ew_tile(r, c)

    def body(core_ref, a_ref, b_ref, o_ref):
        o_ref[...] = (a_ref[...].astype(F32) + b_ref[...].astype(F32)).astype(o_ref.dtype)

    return pl.pallas_call(
        body, name=name,
        grid_spec=pltpu.PrefetchScalarGridSpec(
            num_scalar_prefetch=1, grid=(NCHIP, r // tr, c // tc),
            in_specs=[pl.BlockSpec((None, None, tr, tc), lambda j, i, k, core_ref: (j, core_ref[0], i, k)),
                      pl.BlockSpec((None, tr, tc), lambda j, i, k, core_ref: (j, i, k))],
            out_specs=pl.BlockSpec((None, tr, tc), lambda j, i, k, core_ref: (j, i, k))),
        out_shape=jax.ShapeDtypeStruct(landed.shape, landed.dtype), compiler_params=_params(3),
    )(core, part, landed)


def _device_step(x, p, target, sp, own, core):
    s, d = x.shape
    rows = s // GRID_W
    cos, sin, rot, rot_t = _rope_consts(s)
    w, dw4, sums, dsp, pending = {}, {}, {}, {}, []

    def gather(*names):
        return _GatherPart(names, [own[n] for n in names])

    def got(part):
        w.update(zip(part.names, part.results))

    def grad(name, g):
        dw4[name] = g.reshape((NCHIP, 2) + g.shape[1:])

    def to_sibling(*names):
        return _SiblingPart(names, [dw4[n] for n in names])

    def add_pairs(part):
        for n, landed in zip(part.names, part.results):
            sums[n] = _pair_sum("pair_sum_" + n, dw4[n], landed, core)

    def start_chips(tag, *names):
        send, recv, thru, lands, token = _chips_start("rs_start_" + tag, [sums[n] for n in names])
        pending.append((tag, names, send, recv, thru, lands))
        return token

    c0 = gather("ffn1_w_gate", "ffn1_w_up")
    c1 = gather("ffn1_w_down")
    c2 = gather("w_in")

    def ffn1_wgu():
        got(c0)
        return w["ffn1_w_gate"], w["ffn1_w_up"]

    def ffn1_wd():
        got(c1)
        return w["ffn1_w_down"]

    h1, ffn1_saved = _ffn_forward("ffn1", x, sp["ffn1_norm"], ffn1_wgu, ffn1_wd,
                                  norm_comm=[c0], up_comm=[c1], down_comm=[c2])
    got(c2)
    wqkv, wlat, wga, wgb = _split_w_in(w["w_in"])
    u = _rms_fwd("mix_norm", h1, sp["mix_norm"])
    c3 = gather("w_uq", "w_ukv")
    qkv = _mm_nt("in_qkv", u, wqkv, BF, tn=1024, comm=[c3], rows=3 * NA_HEADS * NA_DIM)
    got(c3)
    lat = _mm_nt("in_lat", u, wlat, F32)
    c3a = gather("w_branch_a")
    ga = _mm_nt("in_ga", u, wga, F32, tn=1024, comm=[c3a])
    got(c3a)
    c3b = gather("w_branch_b")
    gb = _mm_nt("in_gb", u, wgb, F32, tn=1024, comm=[c3b])
    got(c3b)
    tb = _na_table(sp["na_rpb"], rows)
    c4 = gather("ffn2_w_gate")
    oa = _na_fwd(qkv, tb, comm=[c4])
    got(c4)
    cq, ckv, kr = _mla_prep(lat, sp["q_a_norm"], sp["kv_a_norm"], cos, sin, rot)
    c4a = gather("w_out")
    qf = _mla_q_proj(cq, w["w_uq"], cos, sin, rot, comm=[c4a])
    got(c4a)
    c4b = gather("w_pl_gate")
    kf, vf = _mla_kv_proj(ckv, w["w_ukv"], kr, comm=[c4b])
    got(c4b)
    c5 = gather("ffn2_w_up")
    ob = _mla_fwd(qf, kf, vf, comm=[c5])
    got(c5)
    c5a = gather("w_pl")
    ya, yb, merged = _mix_merge(oa, ob, w["w_branch_a"], w["w_branch_b"], ga, gb, comm=[c5a])
    got(c5a)
    h2 = _mix_out(merged, w["w_out"], h1)
    c6 = gather("ffn2_w_down")

    def ffn2_wd():
        got(c6)
        return w["ffn2_w_down"]

    h3, ffn2_saved = _ffn_forward("ffn2", h2, sp["ffn2_norm"], lambda: (w["ffn2_w_gate"], w["ffn2_w_up"]), ffn2_wd,
                                  up_comm=[c6])
    n4 = _rms_fwd("pl_norm", h3, sp["pl_norm"])
    pb = p.astype(BF)
    h4, t, pe = _pl_forward(n4, w["w_pl_gate"], pb, w["w_pl"], h3)

    dh4, dsp["final_norm"], loss = _loss_head(h4, target, sp["final_norm"])
    dt, dpe = _pl_bwd_elem(dh4, pe, t)
    grad("w_pl", _grp_dw("pl_dw", pb, dpe))
    grad("w_pl_gate", _row_dw("plg_dw", n4, dt))
    s1 = to_sibling("w_pl", "w_pl_gate")
    dn4 = _row_dx("plg_dx", dt, w["w_pl_gate"], comm=[s1])
    add_pairs(s1)
    tok = start_chips("pl", "w_pl", "w_pl_gate")
    dh3, dhb, dsp["pl_norm"] = _rms_bwd("pl_dnorm", dn4, h3, sp["pl_norm"], dh4, comm=[_After(tok)])

    xn, hg, hu, a = ffn2_saved
    grad("ffn2_w_down", _ffn_bwd_wd("ffn2_dwd", a, dhb))
    s2 = to_sibling("ffn2_w_down")
    dhg, dhu = _ffn_bwd_act("ffn2_dact", dhb, w["ffn2_w_down"], hg, hu, comm=[s2])
    add_pairs(s2)
    tok = start_chips("ffn2_down", "ffn2_w_down")
    dwg, dwu = _ffn_bwd_wup("ffn2_dwup", xn, dhg, dhu, comm=[_After(tok)])
    grad("ffn2_w_gate", dwg)
    grad("ffn2_w_up", dwu)
    s3 = to_sibling("ffn2_w_gate", "ffn2_w_up")
    dxn = _ffn_bwd_x("ffn2_dx", dhg, dhu, w["ffn2_w_gate"], w["ffn2_w_up"], comm=[s3])
    add_pairs(s3)
    tok = start_chips("ffn2_up", "ffn2_w_gate", "ffn2_w_up")
    dh2, dh2b, dsp["ffn2_norm"] = _rms_bwd("ffn2_dnorm", dxn, h2, sp["ffn2_norm"], dh3, comm=[_After(tok)])

    grad("w_out", _row_dw("out_dw", merged, dh2b))
    s4 = to_sibling("w_out")
    dya, dyb, dga, dgb = _mix_out_bwd(dh2b, w["w_out"], ga, gb, ya, yb, comm=[s4])
    add_pairs(s4)
    grad("w_branch_a", _grp_dw("bra_dw", oa, dya))
    grad("w_branch_b", _grp_dw("brb_dw", ob, dyb))
    doa = _grp_dx("bra_dx", dya, w["w_branch_a"], out_dtype=BF)
    s5 = to_sibling("w_branch_a", "w_branch_b")
    dob = _grp_dx("brb_dx", dyb, w["w_branch_b"], out_dtype=BF, comm=[s5])
    add_pairs(s5)
    tok = start_chips("mix", "w_out", "w_branch_a", "w_branch_b")

    dqf, dkf, dvf = _mla_bwd(qf, kf, vf, dob, comm=[_After(tok)])
    dqp, dkv, dkr = _mla_post(dqf, dkf, dvf, cos, sin, rot_t)
    grad("w_uq", _grp_dw_t("uq_dw", dqp, cq))
    grad("w_ukv", _grp_dw("ukv_dw", ckv, dkv))
    dcq = _grp_dx_t("uq_dx", dqp, w["w_uq"])
    s6 = to_sibling("w_uq", "w_ukv")
    dckv = _grp_dx("ukv_dx", dkv, w["w_ukv"], comm=[s6])
    add_pairs(s6)
    tok = start_chips("mla", "w_uq", "w_ukv")
    dlat, dsp["q_a_norm"], dsp["kv_a_norm"] = _mla_lat_bwd(dcq, dckv, dkr, lat, sp["q_a_norm"], sp["kv_a_norm"],
                                                         cos, sin, rot_t)
    dq_na, dk_na, dv_na, dtab = _na_bwd(qkv, tb, doa, comm=[_After(tok)])
    dsp["na_rpb"] = _na_rpb_grad(dtab, rows)
    dqkv = jnp.concatenate([dq_na, dk_na.astype(BF), dv_na.astype(BF)], axis=1)

    pieces = [dqkv, dlat, dga, dgb]
    dwin = jnp.zeros((sum(pc.shape[1] for pc in pieces), d), BF)
    row0 = 0
    for i, pc in enumerate(pieces):
        dwin = _mm_tn_into("in_dw%d" % i, pc, u, dwin, row0)
        row0 += pc.shape[1]
    grad("w_in", dwin.reshape(NDEV, -1, d))
    s7 = to_sibling("w_in")
    du = _in_proj_bwd_x(pieces, [wqkv, wlat, wga, wgb], comm=[s7])
    add_pairs(s7)
    tok = start_chips("w_in", "w_in")
    dh1, dhb, dsp["mix_norm"] = _rms_bwd("mix_dnorm", du, h1, sp["mix_norm"], dh2, comm=[_After(tok)])

    xn, hg, hu, a = ffn1_saved
    grad("ffn1_w_down", _ffn_bwd_wd("ffn1_dwd", a, dhb))
    s8 = to_sibling("ffn1_w_down")
    dhg, dhu = _ffn_bwd_act("ffn1_dact", dhb, w["ffn1_w_down"], hg, hu, comm=[s8])
    add_pairs(s8)
    tok = start_chips("ffn1_down", "ffn1_w_down")
    dwg, dwu = _ffn_bwd_wup("ffn1_dwup", xn, dhg, dhu, comm=[_After(tok)])
    grad("ffn1_w_gate", dwg)
    grad("ffn1_w_up", dwu)
    s9 = to_sibling("ffn1_w_gate", "ffn1_w_up")
    _comm_only("rs_sibling_ffn1", [s9])
    add_pairs(s9)
    tok = start_chips("ffn1_up", "ffn1_w_gate", "ffn1_w_up")
    dxn = _ffn_bwd_x("ffn1_dx", dhg, dhu, w["ffn1_w_gate"], w["ffn1_w_up"], comm=[_After(tok)])
    dx, _, dsp["ffn1_norm"] = _rms_bwd("ffn1_dnorm", dxn, x, sp["ffn1_norm"], dh1)
    return loss, dx, pending, dsp


def _gather_small(buf):
    def body(in_ref, out_ref, send_sems, recv_sems, local_sem):
        x, y, c = _coords()
        mine = pltpu.make_async_copy(in_ref, out_ref.at[4 * x + 2 * y + c], local_sem)
        mine.start()
        cps = []
        for k in range(1, NDEV):
            fx, fy, fc = (k >> 2) & 1, (k >> 1) & 1, k & 1
            peer = (x ^ fx, y ^ fy, c ^ fc)
            cps.append(pltpu.make_async_remote_copy(
                src_ref=in_ref, dst_ref=out_ref.at[4 * x + 2 * y + c], send_sem=send_sems.at[k - 1],
                recv_sem=recv_sems.at[k - 1], device_id=peer, device_id_type=MESH))
        for cp in cps:
            cp.start()
        for k in range(1, NDEV):
            fx, fy, fc = (k >> 2) & 1, (k >> 1) & 1, k & 1
            px, py, pc = x ^ fx, y ^ fy, c ^ fc
            pltpu.make_async_remote_copy(
                src_ref=in_ref, dst_ref=out_ref.at[4 * px + 2 * py + pc], send_sem=send_sems.at[k - 1],
                recv_sem=recv_sems.at[k - 1], device_id=(px, py, pc), device_id_type=MESH).wait_recv()
        for cp in cps:
            cp.wait_send()
        mine.wait()

    return pl.pallas_call(
        body, name="gather_small", in_specs=[ANY], out_specs=ANY,
        out_shape=jax.ShapeDtypeStruct((NDEV,) + buf.shape, buf.dtype),
        scratch_shapes=[pltpu.SemaphoreType.DMA((NDEV - 1,)), pltpu.SemaphoreType.DMA((NDEV - 1,)),
                        pltpu.SemaphoreType.DMA],
    )(buf)


def _adam_math(wv, g, m, v):
    m_new = B1 * m + (1.0 - B1) * g
    v_new = B2 * v + (1.0 - B2) * (g * g)
    m_hat = m_new / (1.0 - B1 ** STEP)
    v_hat = v_new / (1.0 - B2 ** STEP)
    return -LR * (m_hat / (jnp.sqrt(v_hat) + ADAM_EPS) + WD * wv), m_new, v_new


def _adam(name, parts, wv, m, v, after=None):
    npart, r, c = parts.shape
    tr, tc = _ew_tile(r, c)

    def body(p_ref, w_ref, m_ref, v_ref, *rest):
        g_ref, d_ref, mo_ref, vo_ref = rest[-4:]
        g = p_ref[0].astype(F32)
        for j in range(1, npart):
            g = g + p_ref[j].astype(F32)
        g_ref[...] = g
        d_ref[...], mo_ref[...], vo_ref[...] = _adam_math(w_ref[...], g, m_ref[...], v_ref[...])

    blk = pl.BlockSpec((tr, tc), lambda i, k: (i, k))
    extra = [] if after is None else [after]
    return pl.pallas_call(
        body, name=name, grid=(r // tr, c // tc),
        in_specs=[pl.BlockSpec((npart, tr, tc), lambda i, k: (0, i, k)), blk, blk, blk] + [ANY] * len(extra),
        out_specs=[blk] * 4, out_shape=[jax.ShapeDtypeStruct((r, c), F32)] * 4, compiler_params=_params(2),
    )(parts, wv, m, v, *extra)


def _adam_exchanged(name, sums, land, wv, m, v, my_chip):
    _, r, c = sums.shape
    tr, tc = _ew_tile(r, c)

    def body(chip_ref, s_ref, l_ref, w_ref, m_ref, v_ref, g_ref, d_ref, mo_ref, vo_ref):
        g = s_ref[...].astype(F32)
        for j in range(3):
            g = g + l_ref[j].astype(F32)
        g_ref[...] = g
        d_ref[...], mo_ref[...], vo_ref[...] = _adam_math(w_ref[...], g, m_ref[...], v_ref[...])

    blk = pl.BlockSpec((tr, tc), lambda i, k, chip_ref: (i, k))
    return pl.pallas_call(
        body, name=name,
        grid_spec=pltpu.PrefetchScalarGridSpec(
            num_scalar_prefetch=1, grid=(r // tr, c // tc),
            in_specs=[pl.BlockSpec((None, tr, tc), lambda i, k, chip_ref: (chip_ref[0], i, k)),
                      pl.BlockSpec((3, tr, tc), lambda i, k, chip_ref: (0, i, k)), blk, blk, blk],
            out_specs=[blk] * 4),
        out_shape=[jax.ShapeDtypeStruct((r, c), F32)] * 4, compiler_params=_params(2),
    )(my_chip, sums, land, wv, m, v)


SHARDED = ("ffn1_w_gate", "ffn1_w_up", "ffn1_w_down", "w_in", "w_uq", "w_ukv", "w_branch_a", "w_branch_b", "w_out",
           "ffn2_w_gate", "ffn2_w_up", "ffn2_w_down", "w_pl", "w_pl_gate")
TRANSPOSED = ("ffn1_w_gate", "ffn1_w_up", "ffn2_w_gate", "ffn2_w_up", "w_in", "w_uq")
REPLICATED = ("ffn1_norm", "mix_norm", "q_a_norm", "kv_a_norm", "na_rpb", "ffn2_norm", "pl_norm", "final_norm")
WEIGHTS = ("ffn1_norm", "ffn1_w_gate", "ffn1_w_up", "ffn1_w_down", "mix_norm", "w_in", "q_a_norm", "w_uq",
           "kv_a_norm", "w_ukv", "na_rpb", "w_branch_a", "w_branch_b", "w_out", "ffn2_norm", "ffn2_w_gate",
           "ffn2_w_up", "ffn2_w_down", "pl_norm", "w_pl", "w_pl_gate", "final_norm")
SMALL_W = 2048


def _pack_small(vals):
    rows = []
    for name in REPLICATED:
        flat = vals[name].reshape(-1).astype(F32)
        n = -(-flat.shape[0] // SMALL_W) * SMALL_W
        rows.append(jnp.pad(flat, (0, n - flat.shape[0])).reshape(-1, SMALL_W))
    return jnp.concatenate(rows, axis=0)


def _unpack_small(buf, shapes):
    out, r = {}, 0
    for name in REPLICATED:
        size = int(np.prod(shapes[name]))
        nrow = -(-size // SMALL_W)
        out[name] = buf[r:r + nrow].reshape(-1)[:size].reshape(shapes[name])
        r += nrow
    return out


def kernel(x, p, ffn1_norm, ffn1_w_gate, ffn1_w_up, ffn1_w_down, mix_norm, w_in, q_a_norm, w_uq, kv_a_norm, w_ukv, na_rpb, w_branch_a, w_branch_b, w_out, ffn2_norm, ffn2_w_gate, ffn2_w_up, ffn2_w_down, pl_norm, w_pl, w_pl_gate, final_norm, loss_target, m_ffn1_norm, m_ffn1_w_gate, m_ffn1_w_up, m_ffn1_w_down, m_mix_norm, m_w_in, m_q_a_norm, m_w_uq, m_kv_a_norm, m_w_ukv, m_na_rpb, m_w_branch_a, m_w_branch_b, m_w_out, m_ffn2_norm, m_ffn2_w_gate, m_ffn2_w_up, m_ffn2_w_down, m_pl_norm, m_w_pl, m_w_pl_gate, m_final_norm, v_ffn1_norm, v_ffn1_w_gate, v_ffn1_w_up, v_ffn1_w_down, v_mix_norm, v_w_in, v_q_a_norm, v_w_uq, v_kv_a_norm, v_w_ukv, v_na_rpb, v_w_branch_a, v_w_branch_b, v_w_out, v_ffn2_norm, v_ffn2_w_gate, v_ffn2_w_up, v_ffn2_w_down, v_pl_norm, v_w_pl, v_w_pl_gate, v_final_norm):
    args = dict(locals())
    wts = {n: args[n] for n in WEIGHTS}
    mom = {n: args["m_" + n] for n in WEIGHTS}
    var = {n: args["v_" + n] for n in WEIGHTS}
    shapes = {n: wts[n].shape for n in WEIGHTS}
    core = lax.axis_index("c").astype(jnp.int32).reshape(1)

    local = lambda n, a: a[0].T if n in TRANSPOSED else a[0]
    own = {n: local(n, wts[n]).astype(BF) for n in SHARDED}
    sp = {n: wts[n].reshape(1, -1) for n in REPLICATED if n != "na_rpb"}
    sp["na_rpb"] = wts["na_rpb"][0]
    loss_part, grad_x, pending, dsp = _device_step(x[0], p[0, 0], loss_target[0], sp, own, core)

    out = {}
    last = grad_x
    my_chip = (2 * lax.axis_index("x") + lax.axis_index("y")).astype(jnp.int32).reshape(1)
    for tag, names, send, recv, thru, lands in pending:
        thru, lands = _chips_wait("rs_wait_" + tag, send, recv, thru, lands, last)
        for n, s4, l3 in zip(names, thru, lands):
            res4 = _adam_exchanged("adam_" + n, s4, l3, local(n, wts[n]), local(n, mom[n]), local(n, var[n]), my_chip)
            out[n] = tuple((a.T if n in TRANSPOSED else a)[None] for a in res4)
            last = res4[1]

    small = jnp.concatenate([_pack_small(dsp), jnp.pad(loss_part, ((0, 0), (0, SMALL_W - loss_part.shape[1])))], 0)
    pad_rows = -small.shape[0] % 8
    small = jnp.pad(small, ((0, pad_rows), (0, 0)))
    every = _gather_small(small)
    zeros = jnp.zeros((1 + pad_rows, SMALL_W), F32)
    pack = lambda d: jnp.concatenate([_pack_small(d), zeros], 0)
    g_s, d_s, m_s, v_s = _adam("adam_small", every, pack(wts), pack(mom), pack(var))
    n_rows = small.shape[0] - 1 - pad_rows
    loss = g_s[n_rows, 0]
    small_out = [_unpack_small(b, shapes) for b in (g_s, d_s, m_s, v_s)]
    for n in REPLICATED:
        out[n] = tuple(b[n] for b in small_out)

    res = [loss, grad_x[None]]
    for k in range(4):
        res += [out[n][k] for n in WEIGHTS]
    return tuple(res)
```

```python
import functools

import numpy as np
import jax
import jax.numpy as jnp
from jax import lax
from jax.experimental import pallas as pl
from jax.experimental.pallas import tpu as pltpu

F32 = jnp.float32
BF = jnp.bfloat16
MESH = pl.DeviceIdType.MESH

NDEV = 8
NCHIP = 4
VMEM_LIMIT = 56 * 1024 * 1024
EPS = 1e-6
NEG = -1e30
GRID_W = 64
NA_HEADS, NA_DIM = 8, 128
NA_ROWS_WIN, NA_COLS_WIN = 8, 16
NA_HG = 4
NA_QROWS = 4
ML_HEADS, ML_NOPE, ML_ROPE, ML_V = 8, 128, 64, 128
ML_QK = ML_NOPE + ML_ROPE
ML_RANK = 512
ROPE_THETA = 10000.0
LR, B1, B2, ADAM_EPS, WD, STEP = 0.001, 0.9, 0.999, 1e-08, 0.01, 10
HI = lax.Precision.HIGHEST

_DN = {"nn": (((1,), (0,)), ((), ())), "nt": (((1,), (1,)), ((), ())), "tn": (((0,), (0,)), ((), ()))}


def _params(n):
    return pltpu.CompilerParams(dimension_semantics=("arbitrary",) * n, vmem_limit_bytes=VMEM_LIMIT)


def _sig(v):
    return jax.nn.sigmoid(v)


ANY = pl.BlockSpec(memory_space=pl.ANY)


def _coords():
    return lax.axis_index("x"), lax.axis_index("y"), lax.axis_index("c")


class _Part:
    inputs, out_shapes, sem_shapes, results = (), (), (), None

    def mid(self, ins, outs, sems):
        pass

    def late(self, ins, outs, sems):
        pass


class _After(_Part):
    def __init__(self, token):
        self.inputs = [token]

    def start(self, ins, outs, sems):
        pass

    finish = start


class _GatherPart(_Part):
    def __init__(self, names, shards):
        n = len(shards)
        self.names, self.inputs = list(names), list(shards)
        self.out_shapes = [jax.ShapeDtypeStruct((NDEV,) + a.shape, a.dtype) for a in shards]
        self.sem_shapes = [pltpu.SemaphoreType.DMA((n, 7)), pltpu.SemaphoreType.DMA((n, 7)),
                           pltpu.SemaphoreType.DMA((n,))]

    def _plan(self, ins, outs, sems):
        send_sems, recv_sems, local_sems = sems
        x, y, c = _coords()
        me, sib, diag = (x, y, c), (x, y, 1 - c), (1 - x, 1 - y, c)
        n1, n2 = (x ^ (1 - c), y ^ c, c), (x ^ c, y ^ (1 - c), c)

        def copy(i, k, block, to, src=None):
            px, py, pc = block
            dst = outs[i].at[4 * px + 2 * py + pc]
            return pltpu.make_async_remote_copy(
                src_ref=dst if src is None else src, dst_ref=dst, send_sem=send_sems.at[i, k],
                recv_sem=recv_sems.at[i, k], device_id=to, device_id_type=MESH)

        mine = [pltpu.make_async_copy(ins[i], outs[i].at[4 * x + 2 * y + c], local_sems.at[i])
                for i in range(len(ins))]
        return copy, mine, me, sib, n1, n2, diag

    def _own_sends(self, ins, copy, me, sib, n1, n2):
        return [copy(i, k, me, to, src=ins[i]) for i in range(len(ins)) for k, to in enumerate((sib, n1, n2))]

    def start(self, ins, outs, sems):
        copy, mine, me, sib, n1, n2, _ = self._plan(ins, outs, sems)
        for cp in mine + self._own_sends(ins, copy, me, sib, n1, n2):
            cp.start()

    def mid(self, ins, outs, sems):
        copy, _, me, sib, n1, n2, _ = self._plan(ins, outs, sems)
        for i in range(len(ins)):
            copy(i, 1, n1, me).wait_recv()
            copy(i, 3, n1, n2).start()
            copy(i, 4, n1, sib).start()

    def late(self, ins, outs, sems):
        copy, _, me, sib, _, n2, diag = self._plan(ins, outs, sems)
        for i in range(len(ins)):
            copy(i, 2, n2, me).wait_recv()
            copy(i, 5, n2, sib).start()
        for i in range(len(ins)):
            copy(i, 3, diag, me).wait_recv()
            copy(i, 6, diag, sib).start()

    def finish(self, ins, outs, sems):
        copy, mine, me, sib, n1, n2, diag = self._plan(ins, outs, sems)
        other = lambda dev: (dev[0], dev[1], sib[2])
        n = len(ins)
        for i in range(n):
            copy(i, 0, sib, me).wait_recv()
            for k, block in ((4, other(n2)), (5, other(n1)), (6, other(diag))):
                copy(i, k, block, me).wait_recv()
        for cp in self._own_sends(ins, copy, me, sib, n1, n2):
            cp.wait_send()
        for i in range(n):
            for k, block in ((3, n1), (4, n1), (5, n2), (6, diag)):
                copy(i, k, block, sib).wait_send()
        for cp in mine:
            cp.wait()


class _SiblingPart(_Part):
    def __init__(self, names, parts):
        n = len(parts)
        self.names, self.inputs = list(names), list(parts)
        self.out_shapes = [jax.ShapeDtypeStruct((NCHIP,) + a.shape[2:], a.dtype) for a in parts]
        self.sem_shapes = [pltpu.SemaphoreType.DMA((n,)), pltpu.SemaphoreType.DMA((n,))]

    def _copies(self, ins, outs, sems):
        x, y, c = _coords()
        return [pltpu.make_async_remote_copy(
            src_ref=ins[i].at[:, 1 - c], dst_ref=outs[i], send_sem=sems[0].at[i], recv_sem=sems[1].at[i],
            device_id=(x, y, 1 - c), device_id_type=MESH) for i in range(len(ins))]

    def start(self, ins, outs, sems):
        for cp in self._copies(ins, outs, sems):
            cp.start()

    def finish(self, ins, outs, sems):
        cps = self._copies(ins, outs, sems)
        for cp in cps:
            cp.wait_recv()
        for cp in cps:
            cp.wait_send()


HBM = pl.BlockSpec(memory_space=pltpu.HBM)
SEM = pl.BlockSpec(memory_space=pltpu.SEMAPHORE)


def _chip_peers():
    x, y, c = _coords()
    return [(1 - x, y, c), (x, 1 - y, c), (1 - x, 1 - y, c)]


def _chips_start(name, sums):
    n = len(sums)

    def body(*refs):
        ins, lands, send_sems, recv_sems = refs[:n], refs[n:2 * n], refs[2 * n], refs[2 * n + 1]
        for i in range(n):
            for k, (px, py, pc) in enumerate(_chip_peers()):
                pltpu.make_async_remote_copy(
                    src_ref=ins[i].at[2 * px + py], dst_ref=lands[i].at[k], send_sem=send_sems.at[3 * i + k],
                    recv_sem=recv_sems.at[3 * i + k], device_id=(px, py, pc), device_id_type=MESH).start()
        refs[-1][...] = jnp.zeros_like(refs[-1])

    lands = [lax.empty((3,) + a.shape[1:], a.dtype) for a in sums]
    bufs = list(sums) + lands
    res = pl.pallas_call(
        body, name=name, in_specs=[HBM] * (2 * n),
        out_specs=(SEM, SEM, *[HBM] * (2 * n), pl.BlockSpec(memory_space=pltpu.VMEM)),
        out_shape=(pltpu.SemaphoreType.DMA((3 * n,)), pltpu.SemaphoreType.DMA((3 * n,)),
                   *[pltpu.HBM(a.shape, a.dtype) for a in bufs], jax.ShapeDtypeStruct((8, 128), F32)),
        input_output_aliases={i: 2 + i for i in range(2 * n)},
        compiler_params=pltpu.CompilerParams(has_side_effects=pltpu.SideEffectType.DATAFLOW_SIDE_EFFECTING),
    )(*[pltpu.with_memory_space_constraint(a, pltpu.HBM) for a in bufs])
    return res[0], res[1], list(res[2:2 + n]), list(res[2 + n:2 + 2 * n]), res[-1]


def _chips_wait(name, send_sems, recv_sems, sums, lands, after):
    n = len(sums)

    def body(*refs):
        ins, zones, send, recv = refs[:n], refs[n:2 * n], refs[2 * n], refs[2 * n + 1]
        for i in range(n):
            for k, peer in enumerate(_chip_peers()):
                cp = pltpu.make_async_remote_copy(
                    src_ref=ins[i].at[0], dst_ref=zones[i].at[k], send_sem=send.at[3 * i + k],
                    recv_sem=recv.at[3 * i + k],
                    device_id=peer, device_id_type=MESH)
                cp.wait_send()
                cp.wait_recv()

    bufs = list(sums) + list(lands)
    res = pl.pallas_call(
        body, name=name, in_specs=[HBM] * (2 * n) + [SEM, SEM, ANY], out_specs=[HBM] * (2 * n),
        out_shape=[pltpu.HBM(a.shape, a.dtype) for a in bufs], input_output_aliases={i: i for i in range(2 * n)},
        compiler_params=pltpu.CompilerParams(has_side_effects=pltpu.SideEffectType.DATAFLOW_SIDE_EFFECTING),
    )(*bufs, send_sems, recv_sems, after)
    return list(res[:n]), list(res[n:])


def _call(name, body, grid, in_specs, out_specs, out_shape, args, comm=(), scratch=()):
    comm = [p for p in comm if p is not None]
    single = not isinstance(out_shape, (list, tuple))
    o_specs = [out_specs] if single else list(out_specs)
    o_shape = [out_shape] if single else list(out_shape)
    n_in, n_out = len(in_specs), len(o_specs)
    c_in = [a for p in comm for a in p.inputs]
    c_out = [s for p in comm for s in p.out_shapes]
    c_sem = [s for p in comm for s in p.sem_shapes]

    def wrapped(*refs):
        ins, outs = refs[:n_in], refs[n_in + len(c_in):n_in + len(c_in) + n_out]
        pos = [n_in, n_in + len(c_in) + n_out, n_in + len(c_in) + n_out + len(c_out)]
        own = refs[pos[2]:pos[2] + len(scratch)]
        pos[2] += len(scratch)
        split = []
        for p in comm:
            sizes = [len(p.inputs), len(p.out_shapes), len(p.sem_shapes)]
            split.append([refs[o:o + n] for o, n in zip(pos, sizes)])
            pos = [o + n for o, n in zip(pos, sizes)]
        step, steps = 0, 1
        for a, g in enumerate(grid):
            step, steps = step * g + pl.program_id(a), steps * g

        def run(which, at):
            def go():
                for p, cut in zip(comm, split):
                    getattr(p, which)(*cut)
            if not comm:
                return
            if grid:
                pl.when(step == at)(go)
            else:
                go()

        run("start", 0)
        body(*ins, *outs, *own)
        run("mid", steps // 2)
        run("late", max(steps // 2, steps - 1 - max(1, steps // 8)))
        run("finish", steps - 1)

    res = pl.pallas_call(
        wrapped, name=name, grid=grid, in_specs=list(in_specs) + [ANY] * len(c_in),
        out_specs=o_specs + [ANY] * len(c_out), out_shape=o_shape + c_out, scratch_shapes=list(scratch) + c_sem,
        compiler_params=_params(len(grid)),
    )(*args, *c_in)
    pos = n_out
    for p in comm:
        p.results = list(res[pos:pos + len(p.out_shapes)])
        pos += len(p.out_shapes)
    return res[0] if single else list(res[:n_out])


def _comm_only(name, comm):
    def body(o_ref):
        o_ref[...] = jnp.zeros_like(o_ref)

    _call(name, body, (), [], pl.BlockSpec(memory_space=pltpu.VMEM), jax.ShapeDtypeStruct((8, 128), F32), [], comm)


def _mm(name, grid, prods, extras, outs, epi, nacc=1, comm=()):
    n_p, n_e = len(prods), len(extras)

    def body(*refs):
        ab, ex, out = refs[:2 * n_p], refs[2 * n_p:2 * n_p + n_e], refs[2 * n_p + n_e:]
        accs = [None] * nacc
        for i, prod in enumerate(prods):
            dn, acc, loop = prod[6], prod[7], prod[8]
            a_ref, b_ref = ab[2 * i], ab[2 * i + 1]
            if loop:
                for g in range(loop):
                    t = lax.dot_general(a_ref[g], b_ref[g], _DN[dn], preferred_element_type=F32)
                    accs[acc] = t if accs[acc] is None else accs[acc] + t
            else:
                t = lax.dot_general(a_ref[...], b_ref[...], _DN[dn], preferred_element_type=F32)
                accs[acc] = t if accs[acc] is None else accs[acc] + t
        epi(accs, ex, out)

    in_specs, args = [], []
    for prod in prods:
        in_specs += [pl.BlockSpec(prod[1], prod[2]), pl.BlockSpec(prod[4], prod[5])]
        args += [prod[0], prod[3]]
    for e, e_blk, e_map in extras:
        in_specs.append(pl.BlockSpec(e_blk, e_map))
        args.append(e)
    return _call(name, body, grid, in_specs, [pl.BlockSpec(blk, mp) for _, _, blk, mp in outs],
                 [jax.ShapeDtypeStruct(s, d) for s, d, _, _ in outs], args, comm)


def _store(accs, ex, out):
    out[0][...] = accs[0].astype(out[0].dtype)


def _ew_tile(r, c, budget=3 << 19):
    for t in range(r - r % 16, 0, -16):
        if r % t == 0 and t * c * 4 <= budget:
            return t, c
    for t in range(c - c % 128, 0, -128):
        if c % t == 0 and r * t * 4 <= budget:
            return r, t
    return r, c


def _tile(n, want):
    t = min(n, want)
    assert n % t == 0, (n, want)
    return t


def _mm_nn(name, a, b, out_dtype, tm=512, tn=512, comm=()):
    m, k = a.shape
    n = b.shape[1]
    tm, tn = _tile(m, tm), (tn if n % tn == 0 else n)
    return _mm(name, (n // tn, m // tm),
               [(a, (tm, k), lambda j, i: (i, 0), b, (k, tn), lambda j, i: (0, j), "nn", 0, 0)], [],
               [((m, n), out_dtype, (tm, tn), lambda j, i: (i, j))], _store, comm=comm)[0]


def _mm_nt(name, a, bt, out_dtype, tm=512, tn=512, comm=(), rows=None):
    m, k = a.shape
    n = rows or bt.shape[0]
    tm, tn = _tile(m, tm), (tn if n % tn == 0 else n)
    return _mm(name, (n // tn, m // tm),
               [(a, (tm, k), lambda j, i: (i, 0), bt, (tn, k), lambda j, i: (j, 0), "nt", 0, 0)], [],
               [((m, n), out_dtype, (tm, tn), lambda j, i: (i, j))], _store, comm=comm)[0]


def _mm_tn_into(name, a, b, buf, row0, ta=1024, tb=512):
    t, ka = a.shape
    nb = b.shape[1]
    ta, tb = (ta if ka % ta == 0 else ka), (tb if nb % tb == 0 else nb)

    def body(a_ref, b_ref, buf_in, buf_out, tile, sem):
        i, j = pl.program_id(0), pl.program_id(1)
        tile[...] = lax.dot_general(a_ref[...], b_ref[...], _DN["tn"], preferred_element_type=F32).astype(tile.dtype)
        rows = pl.ds(pl.multiple_of(row0 + i * ta, 16), ta)
        cp = pltpu.make_async_copy(tile, buf_out.at[rows, pl.ds(pl.multiple_of(j * tb, 128), tb)], sem)
        cp.start()
        cp.wait()

    return pl.pallas_call(
        body, name=name, grid=(ka // ta, nb // tb),
        in_specs=[pl.BlockSpec((t, ta), lambda i, j: (0, i)), pl.BlockSpec((t, tb), lambda i, j: (0, j)), ANY],
        out_specs=ANY, out_shape=jax.ShapeDtypeStruct(buf.shape, buf.dtype), input_output_aliases={2: 0},
        scratch_shapes=[pltpu.VMEM((ta, tb), buf.dtype), pltpu.SemaphoreType.DMA],
        compiler_params=_params(2))(a, b, buf)


def _mm_tn(name, a, b, out_dtype, ta=512, tb=512, scale=None):
    t, ka = a.shape
    nb = b.shape[1]
    ta, tb = (ta if ka % ta == 0 else ka), (tb if nb % tb == 0 else nb)

    def epi(accs, ex, out):
        v = accs[0] if scale is None else accs[0] * scale
        out[0][...] = v.astype(out[0].dtype)

    return _mm(name, (ka // ta, nb // tb),
               [(a, (t, ta), lambda i, j: (0, i), b, (t, tb), lambda i, j: (0, j), "tn", 0, 0)], [],
               [((ka, nb), out_dtype, (ta, tb), lambda i, j: (i, j))], epi)[0]


def _rms_fwd(name, x, g, tm=256, comm=()):
    s, d = x.shape
    tm = _tile(s, tm)

    def body(x_ref, g_ref, o_ref):
        v = x_ref[...]
        o_ref[...] = (v * lax.rsqrt(jnp.mean(v * v, axis=-1, keepdims=True) + EPS) * g_ref[...]).astype(o_ref.dtype)

    return _call(name, body, (s // tm,),
                 [pl.BlockSpec((tm, d), lambda i: (i, 0)), pl.BlockSpec((1, d), lambda i: (0, 0))],
                 pl.BlockSpec((tm, d), lambda i: (i, 0)), jax.ShapeDtypeStruct((s, d), BF), [x, g], comm)


def _acc_rows(ref, part, i):
    @pl.when(i == 0)
    def _():
        ref[...] = part

    @pl.when(i > 0)
    def _():
        ref[...] += part


def _rms_bwd_math(dn, v, g):
    rstd = lax.rsqrt(jnp.mean(v * v, axis=-1, keepdims=True) + EPS)
    xh = v * rstd
    dxh = dn * g
    dx = rstd * (dxh - xh * jnp.mean(dxh * xh, axis=-1, keepdims=True))
    return dx, jnp.sum(dn * xh, axis=0, keepdims=True)


def _rms_bwd(name, dn, x, g, resid, tm=256, comm=()):
    s, d = x.shape
    tm = _tile(s, tm)

    def body(dn_ref, x_ref, g_ref, r_ref, dx_ref, dxb_ref, dg_ref):
        dx, part = _rms_bwd_math(dn_ref[...].astype(F32), x_ref[...], g_ref[...])
        tot = r_ref[...] + dx
        dx_ref[...] = tot
        dxb_ref[...] = tot.astype(BF)
        _acc_rows(dg_ref, part, pl.program_id(0))

    row = pl.BlockSpec((tm, d), lambda i: (i, 0))
    one = pl.BlockSpec((1, d), lambda i: (0, 0))
    return _call(name, body, (s // tm,), [row, row, one, row], [row, row, one],
                 [jax.ShapeDtypeStruct((s, d), F32), jax.ShapeDtypeStruct((s, d), BF),
                  jax.ShapeDtypeStruct((1, d), F32)], [dn, x, g, resid], comm)


def _loss_head(h, target, g, tm=256):
    s, d = h.shape
    tm = _tile(s, tm)

    def body(h_ref, t_ref, g_ref, dh_ref, dg_ref, loss_ref):
        v, gv = h_ref[...], g_ref[...]
        rstd = lax.rsqrt(jnp.mean(v * v, axis=-1, keepdims=True) + EPS)
        xh = v * rstd
        err = xh * gv - t_ref[...]
        part_loss = 0.5 * jnp.sum(jnp.mean(err * err, axis=-1, keepdims=True), axis=0, keepdims=True)
        dy = err * (1.0 / d)
        dxh = dy * gv
        dh_ref[...] = rstd * (dxh - xh * jnp.mean(dxh * xh, axis=-1, keepdims=True))
        i = pl.program_id(0)
        _acc_rows(dg_ref, jnp.sum(dy * xh, axis=0, keepdims=True), i)
        _acc_rows(loss_ref, jnp.broadcast_to(part_loss, loss_ref.shape), i)

    row = pl.BlockSpec((tm, d), lambda i: (i, 0))
    one = pl.BlockSpec((1, d), lambda i: (0, 0))
    return pl.pallas_call(
        body, name="loss_head", grid=(s // tm,), in_specs=[row, row, one],
        out_specs=[row, one, pl.BlockSpec((1, 128), lambda i: (0, 0))],
        out_shape=[jax.ShapeDtypeStruct((s, d), F32), jax.ShapeDtypeStruct((1, d), F32),
                   jax.ShapeDtypeStruct((1, 128), F32)],
        compiler_params=_params(1))(h, target, g)


def _pl_bwd_elem(dh, pe, t, tm=256):
    s, d = dh.shape
    tm = _tile(s, tm)

    def body(dh_ref, pe_ref, t_ref, dt_ref, dpe_ref):
        dh_v, sg = dh_ref[...], _sig(t_ref[...])
        dt_ref[...] = (dh_v * pe_ref[...].astype(F32) * sg * (1.0 - sg)).astype(BF)
        dpe_ref[...] = (dh_v * sg).astype(BF)

    row = pl.BlockSpec((tm, d), lambda i: (i, 0))
    return pl.pallas_call(
        body, name="pl_bwd_elem", grid=(s // tm,), in_specs=[row, row, row], out_specs=[row, row],
        out_shape=[jax.ShapeDtypeStruct((s, d), BF)] * 2, compiler_params=_params(1))(dh, pe, t)


def _ffn_up(name, xn, wg, wu, tm=1024, comm=()):
    s, d = xn.shape
    g, fb, _ = wg.shape
    tm = _tile(s, tm)

    def epi(accs, ex, out):
        hg, hu = accs
        out[0][...] = hg.astype(BF)
        out[1][...] = hu.astype(BF)
        out[2][...] = (hg * _sig(hg) * hu).astype(BF)

    a_map = lambda j, i: (i, 0)
    w_map = lambda j, i: (j, 0, 0)
    o = ((g, s, fb), BF, (None, tm, fb), lambda j, i: (j, i, 0))
    return _mm(name, (g, s // tm),
               [(xn, (tm, d), a_map, wg, (None, fb, d), w_map, "nt", 0, 0),
                (xn, (tm, d), a_map, wu, (None, fb, d), w_map, "nt", 1, 0)], [], [o, o, o], epi, nacc=2, comm=comm)


def _ffn_down(name, a, wd, resid, tm=1024, tn=512, comm=()):
    g, s, fb = a.shape
    d = wd.shape[2]
    tm, tn = _tile(s, tm), _tile(d, tn)

    def epi(accs, ex, out):
        out[0][...] = ex[0][...] + 0.5 * accs[0]

    return _mm(name, (d // tn, s // tm),
               [(a, (g, tm, fb), lambda j, i: (0, i, 0), wd, (g, fb, tn), lambda j, i: (0, 0, j), "nn", 0, g)],
               [(resid, (tm, tn), lambda j, i: (i, j))],
               [((s, d), F32, (tm, tn), lambda j, i: (i, j))], epi, comm=comm)[0]


def _ffn_bwd_act(name, dh, wd, hg, hu, tm=1024, comm=()):
    s, d = dh.shape
    g, fb, _ = wd.shape
    tm = _tile(s, tm)

    def epi(accs, ex, out):
        da = 0.5 * accs[0]
        hg_v, hu_v = ex[0][...].astype(F32), ex[1][...].astype(F32)
        sg = _sig(hg_v)
        out[0][...] = (da * hu_v * (sg * (1.0 + hg_v * (1.0 - sg)))).astype(BF)
        out[1][...] = (da * (hg_v * sg)).astype(BF)

    blk = (None, tm, fb)
    gmap = lambda j, i: (j, i, 0)
    return _mm(name, (g, s // tm),
               [(dh, (tm, d), lambda j, i: (i, 0), wd, (None, fb, d), lambda j, i: (j, 0, 0), "nt", 0, 0)],
               [(hg, blk, gmap), (hu, blk, gmap)],
               [((g, s, fb), BF, blk, gmap), ((g, s, fb), BF, blk, gmap)], epi, comm=comm)


def _ffn_bwd_wd(name, a, dh, tn=1024, comm=()):
    g, s, fb = a.shape
    d = dh.shape[1]
    tn = _tile(d, tn)

    def epi(accs, ex, out):
        out[0][...] = (0.5 * accs[0]).astype(BF)

    return _mm(name, (g, d // tn),
               [(a, (None, s, fb), lambda j, i: (j, 0, 0), dh, (s, tn), lambda j, i: (0, i), "tn", 0, 0)], [],
               [((g, fb, d), BF, (None, fb, tn), lambda j, i: (j, 0, i))], epi, comm=comm)[0]


def _ffn_bwd_wup(name, xn, dhg, dhu, tk=1024, comm=()):
    s, d = xn.shape
    g, _, fb = dhg.shape
    tk = _tile(d, tk)

    def epi(accs, ex, out):
        out[0][...] = accs[0].astype(BF)
        out[1][...] = accs[1].astype(BF)

    a_map = lambda j, i: (j, 0, 0)
    b_map = lambda j, i: (0, i)
    o = ((g, fb, d), BF, (None, fb, tk), lambda j, i: (j, 0, i))
    return _mm(name, (g, d // tk),
               [(dhg, (None, s, fb), a_map, xn, (s, tk), b_map, "tn", 0, 0),
                (dhu, (None, s, fb), a_map, xn, (s, tk), b_map, "tn", 1, 0)], [], [o, o], epi, nacc=2, comm=comm)


def _ffn_bwd_x(name, dhg, dhu, wg, wu, tm=512, tn=512, comm=()):
    g, s, fb = dhg.shape
    d = wg.shape[2]
    tm, tn = _tile(s, tm), _tile(d, tn)
    a_blk, a_map = (g, tm, fb), lambda j, i: (0, i, 0)
    b_blk, b_map = (g, fb, tn), lambda j, i: (0, 0, j)
    return _mm(name, (d // tn, s // tm),
               [(dhg, a_blk, a_map, wg, b_blk, b_map, "nn", 0, g), (dhu, a_blk, a_map, wu, b_blk, b_map, "nn", 0, g)],
               [], [((s, d), F32, (tm, tn), lambda j, i: (i, j))], _store, comm=comm)[0]


def _ffn_forward(tag, h, gain, get_wgu, get_wd, norm_comm=(), up_comm=(), down_comm=()):
    xn = _rms_fwd(tag + "_norm", h, gain, comm=norm_comm)
    wg, wu = get_wgu()
    hg, hu, a = _ffn_up(tag + "_up", xn, wg, wu, comm=up_comm)
    return _ffn_down(tag + "_down", a, get_wd(), h, comm=down_comm), (xn, hg, hu, a)


def _na_geometry(rows):
    kh = min(NA_ROWS_WIN, rows)
    cols = np.arange(GRID_W)
    col_start = np.clip(cols - NA_COLS_WIN // 2, 0, GRID_W - NA_COLS_WIN)
    mask = (cols[None, :] >= col_start[:, None]) & (cols[None, :] < col_start[:, None] + NA_COLS_WIN)
    dc = np.clip(cols[None, :] - cols[:, None], -(NA_COLS_WIN - 1), NA_COLS_WIN - 1) + (NA_COLS_WIN - 1)
    return kh, mask, dc


def _na_table(rpb, rows):
    _, mask, dc = _na_geometry(rows)
    nd, nc, cells = 2 * NA_ROWS_WIN - 1, 2 * NA_COLS_WIN - 1, GRID_W * GRID_W
    onehot = np.zeros((128, cells), np.float32)
    onehot[dc.reshape(-1), np.arange(cells)] = mask.reshape(-1).astype(np.float32)
    off = np.where(mask.reshape(1, -1), 0.0, NEG).astype(np.float32)

    def body(r_ref, e_ref, off_ref, o_ref):
        o_ref[...] = jnp.dot(r_ref[...], e_ref[...], precision=HI, preferred_element_type=F32) + off_ref[...]

    flat = pl.pallas_call(body, name="na_table", out_shape=jax.ShapeDtypeStruct((NA_HEADS * nd, cells), F32),
                          compiler_params=_params(0))(
        jnp.pad(rpb.reshape(NA_HEADS * nd, nc), ((0, 0), (0, 128 - nc))), jnp.asarray(onehot), jnp.asarray(off))
    return flat.reshape(NA_HEADS, nd, GRID_W, GRID_W)


class _NaPlan:
    def __init__(self, s):
        self.s, self.rows = s, s // GRID_W
        self.kh = min(NA_ROWS_WIN, self.rows)
        self.qr = min(NA_QROWS, self.rows)
        self.kr = min(self.rows, self.kh + self.qr - 1)
        self.groups = self.rows // self.qr
        self.nd = 2 * NA_ROWS_WIN - 1
        self.hw, self.nq = NA_HG * NA_DIM, NA_HEADS // NA_HG
        clip = lambda v, hi: min(max(v, 0), hi)
        pats = [(clip(g * self.qr - self.kh // 2, self.rows - self.kr) - g * self.qr,)
                + tuple(clip(g * self.qr + a - self.kh // 2, self.rows - self.kh) - g * self.qr for a in range(self.qr))
                for g in range(self.groups)]
        self.rebuild = [g for g in range(self.groups) if g == 0 or pats[g] != pats[g - 1]]

    def first_key_row(self, g):
        return jnp.clip(g * self.qr - self.kh // 2, 0, self.rows - self.kr)

    def specs(self):
        blk = pl.BlockSpec((self.qr * GRID_W, self.hw), lambda j, g: (g, j))
        k_spec = pl.BlockSpec((self.s, self.hw), lambda j, g: (0, self.nq + j))
        v_spec = pl.BlockSpec((self.s, self.hw), lambda j, g: (0, 2 * self.nq + j))
        t_spec = pl.BlockSpec((NA_HG, self.nd, GRID_W, GRID_W), lambda j, g: (j, 0, 0, 0))
        return blk, k_spec, v_spec, t_spec

    def bias_scratch(self):
        return pltpu.VMEM((NA_HG, self.qr * GRID_W, self.kr * GRID_W), F32)

    def fill_bias(self, t_ref, bias_ref, g):
        def build():
            r0, ks = g * self.qr, self.first_key_row(g)
            for a in range(self.qr):
                rs = jnp.clip(r0 + a - self.kh // 2, 0, self.rows - self.kh)
                for i in range(self.kr):
                    valid = jnp.logical_and(ks + i >= rs, ks + i < rs + self.kh)
                    idx = jnp.clip(ks + i - r0 - a + NA_ROWS_WIN - 1, 0, self.nd - 1)
                    for h in range(NA_HG):
                        bias_ref[h, a * GRID_W:(a + 1) * GRID_W, i * GRID_W:(i + 1) * GRID_W] = jnp.where(
                            valid, t_ref[h, idx], NEG)

        pl.when(functools.reduce(jnp.logical_or, [g == r for r in self.rebuild]))(build)

    def window(self, g):
        return pl.ds(pl.multiple_of(self.first_key_row(g) * GRID_W, GRID_W), self.kr * GRID_W)


def _na_probs(q, k, bias):
    sc = lax.dot_general(q, k, _DN["nt"], preferred_element_type=F32) * (NA_DIM ** -0.5) + bias
    e = jnp.exp(sc - jnp.max(sc, axis=-1, keepdims=True))
    return e / jnp.sum(e, axis=-1, keepdims=True)


def _na_fwd(qkv, table, comm=()):
    plan = _NaPlan(qkv.shape[0])
    blk, k_spec, v_spec, t_spec = plan.specs()

    def body(q_ref, k_ref, v_ref, t_ref, o_ref, bias_ref):
        g = pl.program_id(1)
        plan.fill_bias(t_ref, bias_ref, g)
        win = plan.window(g)
        for h in range(NA_HG):
            cs = slice(h * NA_DIM, (h + 1) * NA_DIM)
            p = _na_probs(q_ref[:, cs], k_ref[win, cs], bias_ref[h])
            o_ref[:, cs] = jnp.dot(p.astype(BF), v_ref[win, cs], preferred_element_type=F32).astype(BF)

    return _call("na_fwd", body, (plan.nq, plan.groups), [blk, k_spec, v_spec, t_spec], blk,
                 jax.ShapeDtypeStruct((plan.s, NA_HEADS * NA_DIM), BF), [qkv, qkv, qkv, table], comm,
                 scratch=[plan.bias_scratch()])


def _na_bwd(qkv, table, do, comm=()):
    plan = _NaPlan(qkv.shape[0])
    blk, k_spec, v_spec, t_spec = plan.specs()
    qr, kr = plan.qr, plan.kr

    def body(q_ref, k_ref, v_ref, t_ref, do_ref, dq_ref, dk_ref, dv_ref, dt_ref, bias_ref):
        g = pl.program_id(1)

        @pl.when(g == 0)
        def _():
            dk_ref[...] = jnp.zeros_like(dk_ref)
            dv_ref[...] = jnp.zeros_like(dv_ref)
            dt_ref[...] = jnp.zeros_like(dt_ref)

        plan.fill_bias(t_ref, bias_ref, g)
        win = plan.window(g)
        base = plan.first_key_row(g) - g * qr + NA_ROWS_WIN - 1
        for h in range(NA_HG):
            cs = slice(h * NA_DIM, (h + 1) * NA_DIM)
            q, k, v, do_h = q_ref[:, cs], k_ref[win, cs], v_ref[win, cs], do_ref[:, cs]
            p = _na_probs(q, k, bias_ref[h])
            dp = lax.dot_general(do_h, v, _DN["nt"], preferred_element_type=F32)
            ds = p * (dp - jnp.sum(p * dp, axis=-1, keepdims=True))
            for dlt in range(1 - qr, kr):
                tiles = [ds[a * GRID_W:(a + 1) * GRID_W, (a + dlt) * GRID_W:(a + dlt + 1) * GRID_W]
                         for a in range(qr) if 0 <= a + dlt < kr]
                dt_ref[h, jnp.clip(base + dlt, 0, plan.nd - 1)] += functools.reduce(jnp.add, tiles)
            dsb = (ds * (NA_DIM ** -0.5)).astype(BF)
            dq_ref[:, cs] = jnp.dot(dsb, k, preferred_element_type=F32).astype(BF)
            dk_ref[win, cs] += lax.dot_general(dsb, q, _DN["tn"], preferred_element_type=F32)
            dv_ref[win, cs] += lax.dot_general(p.astype(BF), do_h, _DN["tn"], preferred_element_type=F32)

    width = NA_HEADS * NA_DIM
    whole = pl.BlockSpec((plan.s, plan.hw), lambda j, g: (0, j))
    return _call(
        "na_bwd", body, (plan.nq, plan.groups), [blk, k_spec, v_spec, t_spec, blk], [blk, whole, whole, t_spec],
        [jax.ShapeDtypeStruct((plan.s, width), BF), jax.ShapeDtypeStruct((plan.s, width), F32),
         jax.ShapeDtypeStruct((plan.s, width), F32),
         jax.ShapeDtypeStruct((NA_HEADS, plan.nd, GRID_W, GRID_W), F32)],
        [qkv, qkv, qkv, table, do], comm, scratch=[plan.bias_scratch()])


def _na_rpb_grad(dt, rows):
    _, mask, dc = _na_geometry(rows)
    nd, nc = 2 * NA_ROWS_WIN - 1, 2 * NA_COLS_WIN - 1
    onehot = np.zeros((GRID_W * GRID_W, 128), np.float32)
    onehot[np.arange(GRID_W * GRID_W), dc.reshape(-1)] = mask.reshape(-1).astype(np.float32)
    flat = dt.reshape(NA_HEADS * nd, GRID_W * GRID_W)

    def body(a_ref, e_ref, o_ref):
        o_ref[...] = jnp.dot(a_ref[...], e_ref[...], precision=HI, preferred_element_type=F32)

    out = pl.pallas_call(body, name="na_rpb_grad", out_shape=jax.ShapeDtypeStruct((NA_HEADS * nd, 128), F32),
                         compiler_params=_params(0))(flat, jnp.asarray(onehot))
    return out[:, :nc].reshape(NA_HEADS, nd, nc)


def _rope_consts(s):
    pos = np.arange(s, dtype=np.float32)
    inv = (1.0 / (ROPE_THETA ** (np.arange(0, ML_ROPE, 2, dtype=np.float32) / ML_ROPE))).astype(np.float32)
    ang = pos[:, None] * inv[None, :]
    cos, sin = np.cos(ang).astype(np.float32), np.sin(ang).astype(np.float32)
    half = ML_ROPE // 2
    rot = np.zeros((ML_ROPE, ML_ROPE), np.float32)
    rot[np.arange(half) + half, np.arange(half)] = -1.0
    rot[np.arange(half), np.arange(half) + half] = 1.0
    return (jnp.asarray(np.concatenate([cos, cos], 1)), jnp.asarray(np.concatenate([sin, sin], 1)),
            jnp.asarray(rot), jnp.asarray(rot.T.copy()))


def _rope(v, cos, sin, rot):
    return v * cos + jnp.dot(v, rot, precision=HI, preferred_element_type=F32) * sin


def _unrope(dv, cos, sin, rot_t):
    return dv * cos + jnp.dot(dv * sin, rot_t, precision=HI, preferred_element_type=F32)


def _rms(v, g):
    return v * lax.rsqrt(jnp.mean(v * v, axis=-1, keepdims=True) + EPS) * g


def _mla_prep(lat, gq, gkv, cos, sin, rot, tm=256):
    s, w = lat.shape
    tm = _tile(s, tm)

    def body(l_ref, gq_ref, gkv_ref, c_ref, s_ref, r_ref, cq_ref, ckv_ref, kr_ref):
        cq_ref[...] = _rms(l_ref[:, :ML_RANK], gq_ref[...]).astype(BF)
        ckv_ref[...] = _rms(l_ref[:, ML_RANK:2 * ML_RANK], gkv_ref[...]).astype(BF)
        kr_ref[...] = _rope(l_ref[:, 2 * ML_RANK:], c_ref[...], s_ref[...], r_ref[...]).astype(BF)

    row = lambda c: pl.BlockSpec((tm, c), lambda i: (i, 0))
    full = lambda a: pl.BlockSpec(a.shape, lambda i: (0, 0))
    return pl.pallas_call(
        body, name="mla_prep", grid=(s // tm,),
        in_specs=[row(w), full(gq), full(gkv), row(ML_ROPE), row(ML_ROPE), full(rot)],
        out_specs=[row(ML_RANK), row(ML_RANK), row(ML_ROPE)],
        out_shape=[jax.ShapeDtypeStruct((s, ML_RANK), BF), jax.ShapeDtypeStruct((s, ML_RANK), BF),
                   jax.ShapeDtypeStruct((s, ML_ROPE), BF)],
        compiler_params=_params(1))(lat, gq, gkv, cos, sin, rot)


def _mla_q_proj(cq, wuq, cos, sin, rot, tm=512, comm=()):
    s, k = cq.shape
    tm = _tile(s, tm)

    def epi(accs, ex, out):
        acc = accs[0]
        out[0][:, :ML_NOPE] = acc[:, :ML_NOPE].astype(BF)
        out[0][:, ML_NOPE:] = _rope(acc[:, ML_NOPE:], ex[0][...], ex[1][...], ex[2][...]).astype(BF)

    rmap = lambda j, i: (i, 0)
    return _mm("mla_q_proj", (ML_HEADS, s // tm),
               [(cq, (tm, k), rmap, wuq, (None, ML_QK, k), lambda j, i: (j, 0, 0), "nt", 0, 0)],
               [(cos, (tm, ML_ROPE), rmap), (sin, (tm, ML_ROPE), rmap), (rot, rot.shape, lambda j, i: (0, 0))],
               [((ML_HEADS, s, ML_QK), BF, (None, tm, ML_QK), lambda j, i: (j, i, 0))], epi, comm=comm)[0]


def _mla_kv_proj(ckv, wukv, kr, tm=512, comm=()):
    s, k = ckv.shape
    tm = _tile(s, tm)

    def epi(accs, ex, out):
        acc = accs[0]
        out[0][:, :ML_NOPE] = acc[:, :ML_NOPE].astype(BF)
        out[0][:, ML_NOPE:] = ex[0][...]
        out[1][...] = acc[:, ML_NOPE:].astype(BF)

    rmap = lambda j, i: (i, 0)
    gmap = lambda j, i: (j, i, 0)
    return _mm("mla_kv_proj", (ML_HEADS, s // tm),
               [(ckv, (tm, k), rmap, wukv, (None, k, ML_NOPE + ML_V), lambda j, i: (j, 0, 0), "nn", 0, 0)],
               [(kr, (tm, ML_ROPE), rmap)],
               [((ML_HEADS, s, ML_QK), BF, (None, tm, ML_QK), gmap), ((ML_HEADS, s, ML_V), BF, (None, tm, ML_V), gmap)],
               epi, comm=comm)


def _mla_probs(q, k):
    sc = lax.dot_general(q, k, _DN["nt"], preferred_element_type=F32) * (ML_QK ** -0.5)
    e = jnp.exp(sc - jnp.max(sc, axis=-1, keepdims=True))
    return e / jnp.sum(e, axis=-1, keepdims=True)


def _mla_fwd(q, k, v, tq=1024, comm=()):
    _, s, _ = q.shape
    tq = _tile(s, tq)

    def body(q_ref, k_ref, v_ref, o_ref):
        p = _mla_probs(q_ref[...], k_ref[...])
        o_ref[...] = jnp.dot(p.astype(BF), v_ref[...], preferred_element_type=F32).astype(BF)

    return _call("mla_fwd", body, (ML_HEADS, s // tq),
                 [pl.BlockSpec((None, tq, ML_QK), lambda h, i: (h, i, 0)),
                  pl.BlockSpec((None, s, ML_QK), lambda h, i: (h, 0, 0)),
                  pl.BlockSpec((None, s, ML_V), lambda h, i: (h, 0, 0))],
                 pl.BlockSpec((tq, ML_V), lambda h, i: (i, h)),
                 jax.ShapeDtypeStruct((s, ML_HEADS * ML_V), BF), [q, k, v], comm)


def _mla_bwd(q, k, v, do, tq=1024, comm=()):
    _, s, _ = q.shape
    tq = _tile(s, tq)

    def body(q_ref, k_ref, v_ref, do_ref, dq_ref, dk_ref, dv_ref):
        i = pl.program_id(1)
        qv, kv, vv, dov = q_ref[...], k_ref[...], v_ref[...], do_ref[...]
        p = _mla_probs(qv, kv)
        dp = lax.dot_general(dov, vv, _DN["nt"], preferred_element_type=F32)
        ds = (p * (dp - jnp.sum(p * dp, axis=-1, keepdims=True)) * (ML_QK ** -0.5)).astype(BF)
        dq_ref[...] = jnp.dot(ds, kv, preferred_element_type=F32)
        _acc_rows(dk_ref, lax.dot_general(ds, qv, _DN["tn"], preferred_element_type=F32), i)
        _acc_rows(dv_ref, lax.dot_general(p.astype(BF), dov, _DN["tn"], preferred_element_type=F32), i)

    return _call(
        "mla_bwd", body, (ML_HEADS, s // tq),
        [pl.BlockSpec((None, tq, ML_QK), lambda h, i: (h, i, 0)),
         pl.BlockSpec((None, s, ML_QK), lambda h, i: (h, 0, 0)),
         pl.BlockSpec((None, s, ML_V), lambda h, i: (h, 0, 0)),
         pl.BlockSpec((tq, ML_V), lambda h, i: (i, h))],
        [pl.BlockSpec((None, tq, ML_QK), lambda h, i: (h, i, 0)),
         pl.BlockSpec((None, s, ML_QK), lambda h, i: (h, 0, 0)),
         pl.BlockSpec((None, s, ML_V), lambda h, i: (h, 0, 0))],
        [jax.ShapeDtypeStruct((ML_HEADS, s, ML_QK), F32), jax.ShapeDtypeStruct((ML_HEADS, s, ML_QK), F32),
         jax.ShapeDtypeStruct((ML_HEADS, s, ML_V), F32)],
        [q, k, v, do], comm)


def _mla_post(dq, dk, dv, cos, sin, rot_t, tm=1024):
    _, s, _ = dq.shape
    tm = _tile(s, tm)

    def body(dq_ref, dk_ref, dv_ref, c_ref, s_ref, r_ref, dqp_ref, dkv_ref, dkr_ref):
        h = pl.program_id(1)
        dqv, dkk = dq_ref[...], dk_ref[...]
        dqp_ref[:, :ML_NOPE] = dqv[:, :ML_NOPE].astype(BF)
        dqp_ref[:, ML_NOPE:] = _unrope(dqv[:, ML_NOPE:], c_ref[...], s_ref[...], r_ref[...]).astype(BF)
        dkv_ref[:, :ML_NOPE] = dkk[:, :ML_NOPE].astype(BF)
        dkv_ref[:, ML_NOPE:] = dv_ref[...].astype(BF)
        _acc_rows(dkr_ref, dkk[:, ML_NOPE:], h)

    gspec = lambda c: pl.BlockSpec((None, tm, c), lambda i, h: (h, i, 0))
    rspec = pl.BlockSpec((tm, ML_ROPE), lambda i, h: (i, 0))
    return pl.pallas_call(
        body, name="mla_post", grid=(s // tm, ML_HEADS),
        in_specs=[gspec(ML_QK), gspec(ML_QK), gspec(ML_V), rspec, rspec,
                  pl.BlockSpec(rot_t.shape, lambda i, h: (0, 0))],
        out_specs=[gspec(ML_QK), gspec(ML_NOPE + ML_V), rspec],
        out_shape=[jax.ShapeDtypeStruct((ML_HEADS, s, ML_QK), BF),
                   jax.ShapeDtypeStruct((ML_HEADS, s, ML_NOPE + ML_V), BF),
                   jax.ShapeDtypeStruct((s, ML_ROPE), F32)],
        compiler_params=_params(2))(dq, dk, dv, cos, sin, rot_t)


def _mla_lat_bwd(dcq, dckv, dkr, lat, gq, gkv, cos, sin, rot_t, tm=256):
    s, w = lat.shape
    tm = _tile(s, tm)

    def body(dcq_ref, dckv_ref, dkr_ref, l_ref, gq_ref, gkv_ref, c_ref, s_ref, r_ref, dl_ref, dgq_ref, dgkv_ref):
        i = pl.program_id(0)
        dql, pq = _rms_bwd_math(dcq_ref[...], l_ref[:, :ML_RANK], gq_ref[...])
        dkl, pkv = _rms_bwd_math(dckv_ref[...], l_ref[:, ML_RANK:2 * ML_RANK], gkv_ref[...])
        dl_ref[:, :ML_RANK] = dql.astype(BF)
        dl_ref[:, ML_RANK:2 * ML_RANK] = dkl.astype(BF)
        dl_ref[:, 2 * ML_RANK:] = _unrope(dkr_ref[...], c_ref[...], s_ref[...], r_ref[...]).astype(BF)
        _acc_rows(dgq_ref, pq, i)
        _acc_rows(dgkv_ref, pkv, i)

    row = lambda c: pl.BlockSpec((tm, c), lambda i: (i, 0))
    full = lambda a: pl.BlockSpec(a.shape, lambda i: (0, 0))
    return pl.pallas_call(
        body, name="mla_lat_bwd", grid=(s // tm,),
        in_specs=[row(ML_RANK), row(ML_RANK), row(ML_ROPE), row(w), full(gq), full(gkv), row(ML_ROPE), row(ML_ROPE),
                  full(rot_t)],
        out_specs=[row(w), full(gq), full(gkv)],
        out_shape=[jax.ShapeDtypeStruct((s, w), BF), jax.ShapeDtypeStruct(gq.shape, F32),
                   jax.ShapeDtypeStruct(gkv.shape, F32)],
        compiler_params=_params(1))(dcq, dckv, dkr, lat, gq, gkv, cos, sin, rot_t)


def _grp_dw(name, a, dout, ta=1024):
    s, k = a.shape
    ta = _tile(k, ta)
    if dout.ndim == 3:
        g, _, nb = dout.shape
        b_blk, b_map = (None, s, nb), lambda j, i: (j, 0, 0)
    else:
        g, nb = NDEV, dout.shape[1] // NDEV
        b_blk, b_map = (s, nb), lambda j, i: (0, j)
    return _mm(name, (g, k // ta),
               [(a, (s, ta), lambda j, i: (0, i), dout, b_blk, b_map, "tn", 0, 0)], [],
               [((g, k, nb), BF, (None, ta, nb), lambda j, i: (j, i, 0))], _store)[0]


def _grp_dw_t(name, dout, a, ta=512):
    g, s, nb = dout.shape
    k = a.shape[1]
    ta = _tile(k, ta)
    return _mm(name, (g, k // ta),
               [(dout, (None, s, nb), lambda j, i: (j, 0, 0), a, (s, ta), lambda j, i: (0, i), "tn", 0, 0)], [],
               [((g, nb, k), BF, (None, nb, ta), lambda j, i: (j, 0, i))], _store)[0]


def _grp_dx_t(name, dout, wt, tm=512, tn=512, comm=()):
    g, s, nb = dout.shape
    k = wt.shape[2]
    tm, tn = _tile(s, tm), _tile(k, tn)
    return _mm(name, (k // tn, s // tm),
               [(dout, (g, tm, nb), lambda j, i: (0, i, 0), wt, (g, nb, tn), lambda j, i: (0, 0, j), "nn", 0, g)], [],
               [((s, k), F32, (tm, tn), lambda j, i: (i, j))], _store, comm=comm)[0]


def _grp_dx(name, dout, w, tm=512, tn=512, out_dtype=F32, comm=()):
    g, s, nb = dout.shape
    k = w.shape[1]
    tm, tn = _tile(s, tm), _tile(k, tn)
    return _mm(name, (k // tn, s // tm),
               [(dout, (g, tm, nb), lambda j, i: (0, i, 0), w, (g, tn, nb), lambda j, i: (0, j, 0), "nt", 0, g)], [],
               [((s, k), out_dtype, (tm, tn), lambda j, i: (i, j))], _store, comm=comm)[0]


def _row_dw(name, a, dout, tn=2048):
    s, n = dout.shape
    tn = _tile(n, tn)
    if a.ndim == 3:
        kb = a.shape[2]
        a_blk, a_map = (None, s, kb), lambda j, i: (j, 0, 0)
    else:
        kb = a.shape[1] // NDEV
        a_blk, a_map = (s, kb), lambda j, i: (0, j)
    return _mm(name, (NDEV, n // tn),
               [(a, a_blk, a_map, dout, (s, tn), lambda j, i: (0, i), "tn", 0, 0)], [],
               [((NDEV, kb, n), BF, (None, kb, tn), lambda j, i: (j, 0, i))], _store)[0]


def _mix_merge(oa, ob, wa, wb, ga, gb, tm=1024, comm=()):
    s, k = oa.shape
    g, _, nb = wa.shape
    tm = _tile(s, tm)

    def epi(accs, ex, out):
        ya, yb = accs
        out[0][...] = ya.astype(BF)
        out[1][...] = yb.astype(BF)
        out[2][...] = (_sig(ex[0][...]) * ya + _sig(ex[1][...]) * yb).astype(BF)

    rmap = lambda j, i: (i, 0)
    wmap = lambda j, i: (j, 0, 0)
    o = ((g, s, nb), BF, (None, tm, nb), lambda j, i: (j, i, 0))
    cmap = lambda j, i: (i, j)
    return _mm("mix_merge", (g, s // tm),
               [(oa, (tm, k), rmap, wa, (None, k, nb), wmap, "nn", 0, 0),
                (ob, (tm, k), rmap, wb, (None, k, nb), wmap, "nn", 1, 0)],
               [(ga, (tm, nb), cmap), (gb, (tm, nb), cmap)], [o, o, o], epi, nacc=2, comm=comm)


def _mix_out(merged, wout, resid, tm=1024, tn=512):
    g, s, kb = merged.shape
    d = wout.shape[2]
    tm, tn = _tile(s, tm), _tile(d, tn)

    def epi(accs, ex, out):
        out[0][...] = ex[0][...] + accs[0]

    return _mm("mix_out", (d // tn, s // tm),
               [(merged, (g, tm, kb), lambda j, i: (0, i, 0), wout, (g, kb, tn), lambda j, i: (0, 0, j), "nn", 0, g)],
               [(resid, (tm, tn), lambda j, i: (i, j))],
               [((s, d), F32, (tm, tn), lambda j, i: (i, j))], epi)[0]


def _mix_out_bwd(dh, wout, ga, gb, ya, yb, tm=1024, comm=()):
    s, d = dh.shape
    g, kb, _ = wout.shape
    tm = _tile(s, tm)

    def epi(accs, ex, out):
        dm = accs[0]
        sa, sb = _sig(ex[0][...]), _sig(ex[1][...])
        out[0][...] = (dm * sa).astype(BF)
        out[1][...] = (dm * sb).astype(BF)
        out[2][...] = (dm * ex[2][...].astype(F32) * sa * (1.0 - sa)).astype(BF)
        out[3][...] = (dm * ex[3][...].astype(F32) * sb * (1.0 - sb)).astype(BF)

    cmap = lambda j, i: (i, j)
    gmap = lambda j, i: (j, i, 0)
    og = ((g, s, kb), BF, (None, tm, kb), gmap)
    oc = ((s, g * kb), BF, (tm, kb), cmap)
    return _mm("mix_out_bwd", (g, s // tm),
               [(dh, (tm, d), lambda j, i: (i, 0), wout, (None, kb, d), lambda j, i: (j, 0, 0), "nt", 0, 0)],
               [(ga, (tm, kb), cmap), (gb, (tm, kb), cmap), (ya, (None, tm, kb), gmap), (yb, (None, tm, kb), gmap)],
               [og, og, oc, oc], epi, comm=comm)


def _pl_forward(n4, wplg, p, wpl, h3, tm=1024):
    s, d = n4.shape
    g, kb, _ = wplg.shape
    kp, nb = wpl.shape[1], wpl.shape[2]
    tm = _tile(s, tm)
    wplg_nat = wplg.reshape(g * kb, d)

    def epi(accs, ex, out):
        t, pe = accs
        out[0][...] = ex[0][...] + _sig(t) * pe
        out[1][...] = t
        out[2][...] = pe.astype(BF)

    rmap = lambda j, i: (i, 0)
    cmap = lambda j, i: (i, j)
    return _mm("pl_forward", (g, s // tm),
               [(n4, (tm, d), rmap, wplg_nat, (g * kb, nb), lambda j, i: (0, j), "nn", 0, 0),
                (p, (tm, kp), rmap, wpl, (None, kp, nb), lambda j, i: (j, 0, 0), "nn", 1, 0)],
               [(h3, (tm, nb), cmap)],
               [((s, d), F32, (tm, nb), cmap), ((s, d), F32, (tm, nb), cmap), ((s, d), BF, (tm, nb), cmap)],
               epi, nacc=2)


def _row_dx(name, dout, w, tm=1024, comm=()):
    s, n = dout.shape
    g, kb, _ = w.shape
    tm = _tile(s, tm)
    return _mm(name, (g, s // tm),
               [(dout, (tm, n), lambda j, i: (i, 0), w, (None, kb, n), lambda j, i: (j, 0, 0), "nt", 0, 0)], [],
               [((s, g * kb), F32, (tm, kb), lambda j, i: (i, j))], _store, comm=comm)[0]


def _in_proj_bwd_x(pieces, weights, tm=512, tn=512, comm=()):
    s = pieces[0].shape[0]
    d = weights[0].shape[1]
    tm, tn = _tile(s, tm), _tile(d, tn)
    prods = [(pc, (tm, pc.shape[1]), lambda j, i: (i, 0), w, (pc.shape[1], tn), lambda j, i: (0, j), "nn", 0, 0)
             for pc, w in zip(pieces, weights)]
    return _mm("in_proj_dx", (d // tn, s // tm), prods, [],
               [((s, d), F32, (tm, tn), lambda j, i: (i, j))], _store, comm=comm)[0]


def _split_w_in(w_in_t):
    g, nb, d = w_in_t.shape
    nat = w_in_t.reshape(g * nb, d)
    na, lat = 3 * NA_HEADS * NA_DIM, 2 * ML_RANK + ML_ROPE
    return nat, nat[na:na + lat], nat[na + lat:na + lat + d], nat[na + lat + d:]


def _pair_sum(name, part, landed, core):
    _, _, r, c = part.shape
    tr, tc = _ew_tile(r, c)

    def body(core_ref, a_ref, b_ref, o_ref):
        o_ref[...] = (a_ref[...].astype(F32) + b_ref[...].astype(F32)).astype(o_ref.dtype)

    return pl.pallas_call(
        body, name=name,
        grid_spec=pltpu.PrefetchScalarGridSpec(
            num_scalar_prefetch=1, grid=(NCHIP, r // tr, c // tc),
            in_specs=[pl.BlockSpec((None, None, tr, tc), lambda j, i, k, core_ref: (j, core_ref[0], i, k)),
                      pl.BlockSpec((None, tr, tc), lambda j, i, k, core_ref: (j, i, k))],
            out_specs=pl.BlockSpec((None, tr, tc), lambda j, i, k, core_ref: (j, i, k))),
        out_shape=jax.ShapeDtypeStruct(landed.shape, landed.dtype), compiler_params=_params(3),
    )(core, part, landed)


def _device_step(x, p, target, sp, own, core):
    s, d = x.shape
    rows = s // GRID_W
    cos, sin, rot, rot_t = _rope_consts(s)
    w, dw4, sums, dsp, pending = {}, {}, {}, {}, []

    def gather(*names):
        return _GatherPart(names, [own[n] for n in names])

    def got(part):
        w.update(zip(part.names, part.results))

    def grad(name, g):
        dw4[name] = g.reshape((NCHIP, 2) + g.shape[1:])

    def to_sibling(*names):
        return _SiblingPart(names, [dw4[n] for n in names])

    def add_pairs(part):
        for n, landed in zip(part.names, part.results):
            sums[n] = _pair_sum("pair_sum_" + n, dw4[n], landed, core)

    def start_chips(tag, *names):
        send, recv, thru, lands, token = _chips_start("rs_start_" + tag, [sums[n] for n in names])
        pending.append((tag, names, send, recv, thru, lands))
        return token

    c0 = gather("ffn1_w_gate", "ffn1_w_up")
    c1 = gather("ffn1_w_down")
    c2 = gather("w_in")

    def ffn1_wgu():
        got(c0)
        return w["ffn1_w_gate"], w["ffn1_w_up"]

    def ffn1_wd():
        got(c1)
        return w["ffn1_w_down"]

    h1, ffn1_saved = _ffn_forward("ffn1", x, sp["ffn1_norm"], ffn1_wgu, ffn1_wd,
                                  norm_comm=[c0], up_comm=[c1], down_comm=[c2])
    got(c2)
    wqkv, wlat, wga, wgb = _split_w_in(w["w_in"])
    u = _rms_fwd("mix_norm", h1, sp["mix_norm"])
    c3 = gather("w_uq", "w_ukv")
    qkv = _mm_nt("in_qkv", u, wqkv, BF, tn=1024, comm=[c3], rows=3 * NA_HEADS * NA_DIM)
    got(c3)
    lat = _mm_nt("in_lat", u, wlat, F32, tm=1024)
    c3a = gather("w_branch_a")
    ga = _mm_nt("in_ga", u, wga, F32, tm=1024, tn=1024, comm=[c3a])
    got(c3a)
    c3b = gather("w_branch_b")
    gb = _mm_nt("in_gb", u, wgb, F32, tm=1024, tn=1024, comm=[c3b])
    got(c3b)
    tb = _na_table(sp["na_rpb"], rows)
    c4 = gather("ffn2_w_gate")
    oa = _na_fwd(qkv, tb, comm=[c4])
    got(c4)
    cq, ckv, kr = _mla_prep(lat, sp["q_a_norm"], sp["kv_a_norm"], cos, sin, rot)
    c4a = gather("w_out")
    qf = _mla_q_proj(cq, w["w_uq"], cos, sin, rot, comm=[c4a])
    got(c4a)
    c4b = gather("w_pl_gate")
    kf, vf = _mla_kv_proj(ckv, w["w_ukv"], kr, comm=[c4b])
    got(c4b)
    c5 = gather("ffn2_w_up")
    ob = _mla_fwd(qf, kf, vf, comm=[c5])
    got(c5)
    c5a = gather("w_pl")
    ya, yb, merged = _mix_merge(oa, ob, w["w_branch_a"], w["w_branch_b"], ga, gb, comm=[c5a])
    got(c5a)
    h2 = _mix_out(merged, w["w_out"], h1)
    c6 = gather("ffn2_w_down")

    def ffn2_wd():
        got(c6)
        return w["ffn2_w_down"]

    h3, ffn2_saved = _ffn_forward("ffn2", h2, sp["ffn2_norm"], lambda: (w["ffn2_w_gate"], w["ffn2_w_up"]), ffn2_wd,
                                  up_comm=[c6])
    n4 = _rms_fwd("pl_norm", h3, sp["pl_norm"])
    pb = p.astype(BF)
    h4, t, pe = _pl_forward(n4, w["w_pl_gate"], pb, w["w_pl"], h3)

    dh4, dsp["final_norm"], loss = _loss_head(h4, target, sp["final_norm"])
    dt, dpe = _pl_bwd_elem(dh4, pe, t)
    grad("w_pl", _grp_dw("pl_dw", pb, dpe))
    grad("w_pl_gate", _row_dw("plg_dw", n4, dt))
    s1 = to_sibling("w_pl", "w_pl_gate")
    dn4 = _row_dx("plg_dx", dt, w["w_pl_gate"], comm=[s1])
    add_pairs(s1)
    tok = start_chips("pl", "w_pl", "w_pl_gate")
    dh3, dhb, dsp["pl_norm"] = _rms_bwd("pl_dnorm", dn4, h3, sp["pl_norm"], dh4, comm=[_After(tok)])

    xn, hg, hu, a = ffn2_saved
    grad("ffn2_w_down", _ffn_bwd_wd("ffn2_dwd", a, dhb))
    s2 = to_sibling("ffn2_w_down")
    dhg, dhu = _ffn_bwd_act("ffn2_dact", dhb, w["ffn2_w_down"], hg, hu, comm=[s2])
    add_pairs(s2)
    tok = start_chips("ffn2_down", "ffn2_w_down")
    dwg, dwu = _ffn_bwd_wup("ffn2_dwup", xn, dhg, dhu, comm=[_After(tok)])
    grad("ffn2_w_gate", dwg)
    grad("ffn2_w_up", dwu)
    s3 = to_sibling("ffn2_w_gate", "ffn2_w_up")
    dxn = _ffn_bwd_x("ffn2_dx", dhg, dhu, w["ffn2_w_gate"], w["ffn2_w_up"], comm=[s3])
    add_pairs(s3)
    tok = start_chips("ffn2_up", "ffn2_w_gate", "ffn2_w_up")
    dh2, dh2b, dsp["ffn2_norm"] = _rms_bwd("ffn2_dnorm", dxn, h2, sp["ffn2_norm"], dh3, comm=[_After(tok)])

    grad("w_out", _row_dw("out_dw", merged, dh2b))
    s4 = to_sibling("w_out")
    dya, dyb, dga, dgb = _mix_out_bwd(dh2b, w["w_out"], ga, gb, ya, yb, comm=[s4])
    add_pairs(s4)
    grad("w_branch_a", _grp_dw("bra_dw", oa, dya))
    grad("w_branch_b", _grp_dw("brb_dw", ob, dyb))
    doa = _grp_dx("bra_dx", dya, w["w_branch_a"], out_dtype=BF)
    s5 = to_sibling("w_branch_a", "w_branch_b")
    dob = _grp_dx("brb_dx", dyb, w["w_branch_b"], out_dtype=BF, comm=[s5])
    add_pairs(s5)
    tok = start_chips("mix", "w_out", "w_branch_a", "w_branch_b")

    dqf, dkf, dvf = _mla_bwd(qf, kf, vf, dob, comm=[_After(tok)])
    dqp, dkv, dkr = _mla_post(dqf, dkf, dvf, cos, sin, rot_t)
    grad("w_uq", _grp_dw_t("uq_dw", dqp, cq))
    grad("w_ukv", _grp_dw("ukv_dw", ckv, dkv))
    dcq = _grp_dx_t("uq_dx", dqp, w["w_uq"])
    s6 = to_sibling("w_uq", "w_ukv")
    dckv = _grp_dx("ukv_dx", dkv, w["w_ukv"], comm=[s6])
    add_pairs(s6)
    tok = start_chips("mla", "w_uq", "w_ukv")
    dlat, dsp["q_a_norm"], dsp["kv_a_norm"] = _mla_lat_bwd(dcq, dckv, dkr, lat, sp["q_a_norm"], sp["kv_a_norm"],
                                                         cos, sin, rot_t)
    dq_na, dk_na, dv_na, dtab = _na_bwd(qkv, tb, doa, comm=[_After(tok)])
    dsp["na_rpb"] = _na_rpb_grad(dtab, rows)
    dqkv = jnp.concatenate([dq_na, dk_na.astype(BF), dv_na.astype(BF)], axis=1)

    pieces = [dqkv, dlat, dga, dgb]
    dwin = jnp.zeros((sum(pc.shape[1] for pc in pieces), d), BF)
    row0 = 0
    for i, pc in enumerate(pieces):
        dwin = _mm_tn_into("in_dw%d" % i, pc, u, dwin, row0)
        row0 += pc.shape[1]
    grad("w_in", dwin.reshape(NDEV, -1, d))
    s7 = to_sibling("w_in")
    du = _in_proj_bwd_x(pieces, [wqkv, wlat, wga, wgb], comm=[s7])
    add_pairs(s7)
    tok = start_chips("w_in", "w_in")
    dh1, dhb, dsp["mix_norm"] = _rms_bwd("mix_dnorm", du, h1, sp["mix_norm"], dh2, comm=[_After(tok)])

    xn, hg, hu, a = ffn1_saved
    grad("ffn1_w_down", _ffn_bwd_wd("ffn1_dwd", a, dhb))
    s8 = to_sibling("ffn1_w_down")
    dhg, dhu = _ffn_bwd_act("ffn1_dact", dhb, w["ffn1_w_down"], hg, hu, comm=[s8])
    add_pairs(s8)
    tok = start_chips("ffn1_down", "ffn1_w_down")
    dwg, dwu = _ffn_bwd_wup("ffn1_dwup", xn, dhg, dhu, comm=[_After(tok)])
    grad("ffn1_w_gate", dwg)
    grad("ffn1_w_up", dwu)
    s9 = to_sibling("ffn1_w_gate", "ffn1_w_up")
    _comm_only("rs_sibling_ffn1", [s9])
    add_pairs(s9)
    tok = start_chips("ffn1_up", "ffn1_w_gate", "ffn1_w_up")
    dxn = _ffn_bwd_x("ffn1_dx", dhg, dhu, w["ffn1_w_gate"], w["ffn1_w_up"], comm=[_After(tok)])
    dx, _, dsp["ffn1_norm"] = _rms_bwd("ffn1_dnorm", dxn, x, sp["ffn1_norm"], dh1)
    return loss, dx, pending, dsp


def _gather_small(buf):
    def body(in_ref, out_ref, send_sems, recv_sems, local_sem):
        x, y, c = _coords()
        mine = pltpu.make_async_copy(in_ref, out_ref.at[4 * x + 2 * y + c], local_sem)
        mine.start()
        cps = []
        for k in range(1, NDEV):
            fx, fy, fc = (k >> 2) & 1, (k >> 1) & 1, k & 1
            peer = (x ^ fx, y ^ fy, c ^ fc)
            cps.append(pltpu.make_async_remote_copy(
                src_ref=in_ref, dst_ref=out_ref.at[4 * x + 2 * y + c], send_sem=send_sems.at[k - 1],
                recv_sem=recv_sems.at[k - 1], device_id=peer, device_id_type=MESH))
        for cp in cps:
            cp.start()
        for k in range(1, NDEV):
            fx, fy, fc = (k >> 2) & 1, (k >> 1) & 1, k & 1
            px, py, pc = x ^ fx, y ^ fy, c ^ fc
            pltpu.make_async_remote_copy(
                src_ref=in_ref, dst_ref=out_ref.at[4 * px + 2 * py + pc], send_sem=send_sems.at[k - 1],
                recv_sem=recv_sems.at[k - 1], device_id=(px, py, pc), device_id_type=MESH).wait_recv()
        for cp in cps:
            cp.wait_send()
        mine.wait()

    return pl.pallas_call(
        body, name="gather_small", in_specs=[ANY], out_specs=ANY,
        out_shape=jax.ShapeDtypeStruct((NDEV,) + buf.shape, buf.dtype),
        scratch_shapes=[pltpu.SemaphoreType.DMA((NDEV - 1,)), pltpu.SemaphoreType.DMA((NDEV - 1,)),
                        pltpu.SemaphoreType.DMA],
    )(buf)


def _adam_math(wv, g, m, v):
    m_new = B1 * m + (1.0 - B1) * g
    v_new = B2 * v + (1.0 - B2) * (g * g)
    m_hat = m_new / (1.0 - B1 ** STEP)
    v_hat = v_new / (1.0 - B2 ** STEP)
    return -LR * (m_hat / (jnp.sqrt(v_hat) + ADAM_EPS) + WD * wv), m_new, v_new


def _adam(name, parts, wv, m, v, after=None):
    npart, r, c = parts.shape
    tr, tc = _ew_tile(r, c)

    def body(p_ref, w_ref, m_ref, v_ref, *rest):
        g_ref, d_ref, mo_ref, vo_ref = rest[-4:]
        g = p_ref[0].astype(F32)
        for j in range(1, npart):
            g = g + p_ref[j].astype(F32)
        g_ref[...] = g
        d_ref[...], mo_ref[...], vo_ref[...] = _adam_math(w_ref[...], g, m_ref[...], v_ref[...])

    blk = pl.BlockSpec((tr, tc), lambda i, k: (i, k))
    extra = [] if after is None else [after]
    return pl.pallas_call(
        body, name=name, grid=(r // tr, c // tc),
        in_specs=[pl.BlockSpec((npart, tr, tc), lambda i, k: (0, i, k)), blk, blk, blk] + [ANY] * len(extra),
        out_specs=[blk] * 4, out_shape=[jax.ShapeDtypeStruct((r, c), F32)] * 4, compiler_params=_params(2),
    )(parts, wv, m, v, *extra)


def _adam_exchanged(name, sums, land, wv, m, v, my_chip):
    _, r, c = sums.shape
    tr, tc = _ew_tile(r, c)

    def body(chip_ref, s_ref, l_ref, w_ref, m_ref, v_ref, g_ref, d_ref, mo_ref, vo_ref):
        g = s_ref[...].astype(F32)
        for j in range(3):
            g = g + l_ref[j].astype(F32)
        g_ref[...] = g
        d_ref[...], mo_ref[...], vo_ref[...] = _adam_math(w_ref[...], g, m_ref[...], v_ref[...])

    blk = pl.BlockSpec((tr, tc), lambda i, k, chip_ref: (i, k))
    return pl.pallas_call(
        body, name=name,
        grid_spec=pltpu.PrefetchScalarGridSpec(
            num_scalar_prefetch=1, grid=(r // tr, c // tc),
            in_specs=[pl.BlockSpec((None, tr, tc), lambda i, k, chip_ref: (chip_ref[0], i, k)),
                      pl.BlockSpec((3, tr, tc), lambda i, k, chip_ref: (0, i, k)), blk, blk, blk],
            out_specs=[blk] * 4),
        out_shape=[jax.ShapeDtypeStruct((r, c), F32)] * 4, compiler_params=_params(2),
    )(my_chip, sums, land, wv, m, v)


SHARDED = ("ffn1_w_gate", "ffn1_w_up", "ffn1_w_down", "w_in", "w_uq", "w_ukv", "w_branch_a", "w_branch_b", "w_out",
           "ffn2_w_gate", "ffn2_w_up", "ffn2_w_down", "w_pl", "w_pl_gate")
TRANSPOSED = ("ffn1_w_gate", "ffn1_w_up", "ffn2_w_gate", "ffn2_w_up", "w_in", "w_uq")
REPLICATED = ("ffn1_norm", "mix_norm", "q_a_norm", "kv_a_norm", "na_rpb", "ffn2_norm", "pl_norm", "final_norm")
WEIGHTS = ("ffn1_norm", "ffn1_w_gate", "ffn1_w_up", "ffn1_w_down", "mix_norm", "w_in", "q_a_norm", "w_uq",
           "kv_a_norm", "w_ukv", "na_rpb", "w_branch_a", "w_branch_b", "w_out", "ffn2_norm", "ffn2_w_gate",
           "ffn2_w_up", "ffn2_w_down", "pl_norm", "w_pl", "w_pl_gate", "final_norm")
SMALL_W = 2048


def _pack_small(vals):
    rows = []
    for name in REPLICATED:
        flat = vals[name].reshape(-1).astype(F32)
        n = -(-flat.shape[0] // SMALL_W) * SMALL_W
        rows.append(jnp.pad(flat, (0, n - flat.shape[0])).reshape(-1, SMALL_W))
    return jnp.concatenate(rows, axis=0)


def _unpack_small(buf, shapes):
    out, r = {}, 0
    for name in REPLICATED:
        size = int(np.prod(shapes[name]))
        nrow = -(-size // SMALL_W)
        out[name] = buf[r:r + nrow].reshape(-1)[:size].reshape(shapes[name])
        r += nrow
    return out


def kernel(x, p, ffn1_norm, ffn1_w_gate, ffn1_w_up, ffn1_w_down, mix_norm, w_in, q_a_norm, w_uq, kv_a_norm, w_ukv, na_rpb, w_branch_a, w_branch_b, w_out, ffn2_norm, ffn2_w_gate, ffn2_w_up, ffn2_w_down, pl_norm, w_pl, w_pl_gate, final_norm, loss_target, m_ffn1_norm, m_ffn1_w_gate, m_ffn1_w_up, m_ffn1_w_down, m_mix_norm, m_w_in, m_q_a_norm, m_w_uq, m_kv_a_norm, m_w_ukv, m_na_rpb, m_w_branch_a, m_w_branch_b, m_w_out, m_ffn2_norm, m_ffn2_w_gate, m_ffn2_w_up, m_ffn2_w_down, m_pl_norm, m_w_pl, m_w_pl_gate, m_final_norm, v_ffn1_norm, v_ffn1_w_gate, v_ffn1_w_up, v_ffn1_w_down, v_mix_norm, v_w_in, v_q_a_norm, v_w_uq, v_kv_a_norm, v_w_ukv, v_na_rpb, v_w_branch_a, v_w_branch_b, v_w_out, v_ffn2_norm, v_ffn2_w_gate, v_ffn2_w_up, v_ffn2_w_down, v_pl_norm, v_w_pl, v_w_pl_gate, v_final_norm):
    args = dict(locals())
    wts = {n: args[n] for n in WEIGHTS}
    mom = {n: args["m_" + n] for n in WEIGHTS}
    var = {n: args["v_" + n] for n in WEIGHTS}
    shapes = {n: wts[n].shape for n in WEIGHTS}
    core = lax.axis_index("c").astype(jnp.int32).reshape(1)

    local = lambda n, a: a[0].T if n in TRANSPOSED else a[0]
    own = {n: local(n, wts[n]).astype(BF) for n in SHARDED}
    sp = {n: wts[n].reshape(1, -1) for n in REPLICATED if n != "na_rpb"}
    sp["na_rpb"] = wts["na_rpb"][0]
    loss_part, grad_x, pending, dsp = _device_step(x[0], p[0, 0], loss_target[0], sp, own, core)

    out = {}
    last = grad_x
    my_chip = (2 * lax.axis_index("x") + lax.axis_index("y")).astype(jnp.int32).reshape(1)
    for tag, names, send, recv, thru, lands in pending:
        thru, lands = _chips_wait("rs_wait_" + tag, send, recv, thru, lands, last)
        for n, s4, l3 in zip(names, thru, lands):
            res4 = _adam_exchanged("adam_" + n, s4, l3, local(n, wts[n]), local(n, mom[n]), local(n, var[n]), my_chip)
            out[n] = tuple((a.T if n in TRANSPOSED else a)[None] for a in res4)
            last = res4[1]

    small = jnp.concatenate([_pack_small(dsp), jnp.pad(loss_part, ((0, 0), (0, SMALL_W - loss_part.shape[1])))], 0)
    pad_rows = -small.shape[0] % 8
    small = jnp.pad(small, ((0, pad_rows), (0, 0)))
    every = _gather_small(small)
    zeros = jnp.zeros((1 + pad_rows, SMALL_W), F32)
    pack = lambda d: jnp.concatenate([_pack_small(d), zeros], 0)
    g_s, d_s, m_s, v_s = _adam("adam_small", every, pack(wts), pack(mom), pack(var))
    n_rows = small.shape[0] - 1 - pad_rows
    loss = g_s[n_rows, 0]
    small_out = [_unpack_small(b, shapes) for b in (g_s, d_s, m_s, v_s)]
    for n in REPLICATED:
        out[n] = tuple(b[n] for b in small_out)

    res = [loss, grad_x[None]]
    for k in range(4):
        res += [out[n][k] for n in WEIGHTS]
    return tuple(res)
```

```python
import functools

import numpy as np
import jax
import jax.numpy as jnp
from jax import lax
from jax.experimental import pallas as pl
from jax.experimental.pallas import tpu as pltpu

F32 = jnp.float32
BF = jnp.bfloat16
MESH = pl.DeviceIdType.MESH

NDEV = 8
NCHIP = 4
VMEM_LIMIT = 56 * 1024 * 1024
EPS = 1e-6
NEG = -1e30
GRID_W = 64
NA_HEADS, NA_DIM = 8, 128
NA_ROWS_WIN, NA_COLS_WIN = 8, 16
NA_HG = 4
NA_QROWS = 4
ML_HEADS, ML_NOPE, ML_ROPE, ML_V = 8, 128, 64, 128
ML_QK = ML_NOPE + ML_ROPE
ML_RANK = 512
ROPE_THETA = 10000.0
LR, B1, B2, ADAM_EPS, WD, STEP = 0.001, 0.9, 0.999, 1e-08, 0.01, 10
HI = lax.Precision.HIGHEST

_DN = {"nn": (((1,), (0,)), ((), ())), "nt": (((1,), (1,)), ((), ())), "tn": (((0,), (0,)), ((), ()))}


def _params(n):
    return pltpu.CompilerParams(dimension_semantics=("arbitrary",) * n, vmem_limit_bytes=VMEM_LIMIT)


def _sig(v):
    return jax.nn.sigmoid(v)


ANY = pl.BlockSpec(memory_space=pl.ANY)


def _coords():
    return lax.axis_index("x"), lax.axis_index("y"), lax.axis_index("c")


class _Part:
    inputs, out_shapes, sem_shapes, results = (), (), (), None

    def mid(self, ins, outs, sems):
        pass

    def late(self, ins, outs, sems):
        pass


class _After(_Part):
    def __init__(self, token):
        self.inputs = [token]

    def start(self, ins, outs, sems):
        pass

    finish = start


class _GatherPart(_Part):
    def __init__(self, names, shards):
        n = len(shards)
        self.names, self.inputs = list(names), list(shards)
        self.out_shapes = [jax.ShapeDtypeStruct((NDEV,) + a.shape, a.dtype) for a in shards]
        self.sem_shapes = [pltpu.SemaphoreType.DMA((n, 7)), pltpu.SemaphoreType.DMA((n, 7)),
                           pltpu.SemaphoreType.DMA((n,))]

    def _plan(self, ins, outs, sems):
        send_sems, recv_sems, local_sems = sems
        x, y, c = _coords()
        me, sib, diag = (x, y, c), (x, y, 1 - c), (1 - x, 1 - y, c)
        n1, n2 = (x ^ (1 - c), y ^ c, c), (x ^ c, y ^ (1 - c), c)

        def copy(i, k, block, to, src=None):
            px, py, pc = block
            dst = outs[i].at[4 * px + 2 * py + pc]
            return pltpu.make_async_remote_copy(
                src_ref=dst if src is None else src, dst_ref=dst, send_sem=send_sems.at[i, k],
                recv_sem=recv_sems.at[i, k], device_id=to, device_id_type=MESH)

        mine = [pltpu.make_async_copy(ins[i], outs[i].at[4 * x + 2 * y + c], local_sems.at[i])
                for i in range(len(ins))]
        return copy, mine, me, sib, n1, n2, diag

    def _own_sends(self, ins, copy, me, sib, n1, n2):
        return [copy(i, k, me, to, src=ins[i]) for i in range(len(ins)) for k, to in enumerate((sib, n1, n2))]

    def start(self, ins, outs, sems):
        copy, mine, me, sib, n1, n2, _ = self._plan(ins, outs, sems)
        for cp in mine + self._own_sends(ins, copy, me, sib, n1, n2):
            cp.start()

    def mid(self, ins, outs, sems):
        copy, _, me, sib, n1, n2, _ = self._plan(ins, outs, sems)
        for i in range(len(ins)):
            copy(i, 1, n1, me).wait_recv()
            copy(i, 3, n1, n2).start()
            copy(i, 4, n1, sib).start()

    def late(self, ins, outs, sems):
        copy, _, me, sib, _, n2, diag = self._plan(ins, outs, sems)
        for i in range(len(ins)):
            copy(i, 2, n2, me).wait_recv()
            copy(i, 5, n2, sib).start()
        for i in range(len(ins)):
            copy(i, 3, diag, me).wait_recv()
            copy(i, 6, diag, sib).start()

    def finish(self, ins, outs, sems):
        copy, mine, me, sib, n1, n2, diag = self._plan(ins, outs, sems)
        other = lambda dev: (dev[0], dev[1], sib[2])
        n = len(ins)
        for i in range(n):
            copy(i, 0, sib, me).wait_recv()
            for k, block in ((4, other(n2)), (5, other(n1)), (6, other(diag))):
                copy(i, k, block, me).wait_recv()
        for cp in self._own_sends(ins, copy, me, sib, n1, n2):
            cp.wait_send()
        for i in range(n):
            for k, block in ((3, n1), (4, n1), (5, n2), (6, diag)):
                copy(i, k, block, sib).wait_send()
        for cp in mine:
            cp.wait()


class _SiblingPart(_Part):
    def __init__(self, names, parts):
        n = len(parts)
        self.names, self.inputs = list(names), list(parts)
        self.out_shapes = [jax.ShapeDtypeStruct((NCHIP,) + a.shape[2:], a.dtype) for a in parts]
        self.sem_shapes = [pltpu.SemaphoreType.DMA((n,)), pltpu.SemaphoreType.DMA((n,))]

    def _copies(self, ins, outs, sems):
        x, y, c = _coords()
        return [pltpu.make_async_remote_copy(
            src_ref=ins[i].at[:, 1 - c], dst_ref=outs[i], send_sem=sems[0].at[i], recv_sem=sems[1].at[i],
            device_id=(x, y, 1 - c), device_id_type=MESH) for i in range(len(ins))]

    def start(self, ins, outs, sems):
        for cp in self._copies(ins, outs, sems):
            cp.start()

    def finish(self, ins, outs, sems):
        cps = self._copies(ins, outs, sems)
        for cp in cps:
            cp.wait_recv()
        for cp in cps:
            cp.wait_send()


HBM = pl.BlockSpec(memory_space=pltpu.HBM)
SEM = pl.BlockSpec(memory_space=pltpu.SEMAPHORE)


def _chip_peers():
    x, y, c = _coords()
    return [(1 - x, y, c), (x, 1 - y, c), (1 - x, 1 - y, c)]


def _chips_start(name, sums):
    n = len(sums)

    def body(*refs):
        ins, lands, send_sems, recv_sems = refs[:n], refs[n:2 * n], refs[2 * n], refs[2 * n + 1]
        for i in range(n):
            for k, (px, py, pc) in enumerate(_chip_peers()):
                pltpu.make_async_remote_copy(
                    src_ref=ins[i].at[2 * px + py], dst_ref=lands[i].at[k], send_sem=send_sems.at[3 * i + k],
                    recv_sem=recv_sems.at[3 * i + k], device_id=(px, py, pc), device_id_type=MESH).start()
        refs[-1][...] = jnp.zeros_like(refs[-1])

    lands = [lax.empty((3,) + a.shape[1:], a.dtype) for a in sums]
    bufs = list(sums) + lands
    res = pl.pallas_call(
        body, name=name, in_specs=[HBM] * (2 * n),
        out_specs=(SEM, SEM, *[HBM] * (2 * n), pl.BlockSpec(memory_space=pltpu.VMEM)),
        out_shape=(pltpu.SemaphoreType.DMA((3 * n,)), pltpu.SemaphoreType.DMA((3 * n,)),
                   *[pltpu.HBM(a.shape, a.dtype) for a in bufs], jax.ShapeDtypeStruct((8, 128), F32)),
        input_output_aliases={i: 2 + i for i in range(2 * n)},
        compiler_params=pltpu.CompilerParams(has_side_effects=pltpu.SideEffectType.DATAFLOW_SIDE_EFFECTING),
    )(*[pltpu.with_memory_space_constraint(a, pltpu.HBM) for a in bufs])
    return res[0], res[1], list(res[2:2 + n]), list(res[2 + n:2 + 2 * n]), res[-1]


def _chips_wait(name, send_sems, recv_sems, sums, lands, after):
    n = len(sums)

    def body(*refs):
        ins, zones, send, recv = refs[:n], refs[n:2 * n], refs[2 * n], refs[2 * n + 1]
        for i in range(n):
            for k, peer in enumerate(_chip_peers()):
                cp = pltpu.make_async_remote_copy(
                    src_ref=ins[i].at[0], dst_ref=zones[i].at[k], send_sem=send.at[3 * i + k],
                    recv_sem=recv.at[3 * i + k],
                    device_id=peer, device_id_type=MESH)
                cp.wait_send()
                cp.wait_recv()

    bufs = list(sums) + list(lands)
    res = pl.pallas_call(
        body, name=name, in_specs=[HBM] * (2 * n) + [SEM, SEM, ANY], out_specs=[HBM] * (2 * n),
        out_shape=[pltpu.HBM(a.shape, a.dtype) for a in bufs], input_output_aliases={i: i for i in range(2 * n)},
        compiler_params=pltpu.CompilerParams(has_side_effects=pltpu.SideEffectType.DATAFLOW_SIDE_EFFECTING),
    )(*bufs, send_sems, recv_sems, after)
    return list(res[:n]), list(res[n:])


def _call(name, body, grid, in_specs, out_specs, out_shape, args, comm=(), scratch=()):
    comm = [p for p in comm if p is not None]
    single = not isinstance(out_shape, (list, tuple))
    o_specs = [out_specs] if single else list(out_specs)
    o_shape = [out_shape] if single else list(out_shape)
    n_in, n_out = len(in_specs), len(o_specs)
    c_in = [a for p in comm for a in p.inputs]
    c_out = [s for p in comm for s in p.out_shapes]
    c_sem = [s for p in comm for s in p.sem_shapes]

    def wrapped(*refs):
        ins, outs = refs[:n_in], refs[n_in + len(c_in):n_in + len(c_in) + n_out]
        pos = [n_in, n_in + len(c_in) + n_out, n_in + len(c_in) + n_out + len(c_out)]
        own = refs[pos[2]:pos[2] + len(scratch)]
        pos[2] += len(scratch)
        split = []
        for p in comm:
            sizes = [len(p.inputs), len(p.out_shapes), len(p.sem_shapes)]
            split.append([refs[o:o + n] for o, n in zip(pos, sizes)])
            pos = [o + n for o, n in zip(pos, sizes)]
        step, steps = 0, 1
        for a, g in enumerate(grid):
            step, steps = step * g + pl.program_id(a), steps * g

        def run(which, at):
            def go():
                for p, cut in zip(comm, split):
                    getattr(p, which)(*cut)
            if not comm:
                return
            if grid:
                pl.when(step == at)(go)
            else:
                go()

        run("start", 0)
        body(*ins, *outs, *own)
        run("mid", steps // 2)
        run("late", max(steps // 2, steps - 1 - max(1, steps // 8)))
        run("finish", steps - 1)

    res = pl.pallas_call(
        wrapped, name=name, grid=grid, in_specs=list(in_specs) + [ANY] * len(c_in),
        out_specs=o_specs + [ANY] * len(c_out), out_shape=o_shape + c_out, scratch_shapes=list(scratch) + c_sem,
        compiler_params=_params(len(grid)),
    )(*args, *c_in)
    pos = n_out
    for p in comm:
        p.results = list(res[pos:pos + len(p.out_shapes)])
        pos += len(p.out_shapes)
    return res[0] if single else list(res[:n_out])


def _comm_only(name, comm):
    def body(o_ref):
        o_ref[...] = jnp.zeros_like(o_ref)

    _call(name, body, (), [], pl.BlockSpec(memory_space=pltpu.VMEM), jax.ShapeDtypeStruct((8, 128), F32), [], comm)


def _mm(name, grid, prods, extras, outs, epi, nacc=1, comm=(), sub=1):
    n_p, n_e = len(prods), len(extras)
    tm = prods[0][1][-2]
    assert tm % sub == 0 and (sub == 1 or all(p[6] != "tn" for p in prods))

    def body(*refs):
        ab, ex, out = refs[:2 * n_p], refs[2 * n_p:2 * n_p + n_e], refs[2 * n_p + n_e:]
        for r in range(sub):
            rows = slice(None) if sub == 1 else pl.ds(r * (tm // sub), tm // sub)
            cut = lambda ref: ref.at[rows] if sub > 1 and ref.shape[0] == tm else ref
            accs = [None] * nacc
            for i, prod in enumerate(prods):
                dn, acc, loop = prod[6], prod[7], prod[8]
                a_ref, b_ref = ab[2 * i], ab[2 * i + 1]
                for g in range(loop or 1):
                    a = a_ref[g, rows] if loop else a_ref[rows]
                    b = b_ref[g] if loop else b_ref[...]
                    t = lax.dot_general(a, b, _DN[dn], preferred_element_type=F32)
                    accs[acc] = t if accs[acc] is None else accs[acc] + t
            epi(accs, [cut(e) for e in ex], [cut(o) for o in out])

    in_specs, args = [], []
    for prod in prods:
        in_specs += [pl.BlockSpec(prod[1], prod[2]), pl.BlockSpec(prod[4], prod[5])]
        args += [prod[0], prod[3]]
    for e, e_blk, e_map in extras:
        in_specs.append(pl.BlockSpec(e_blk, e_map))
        args.append(e)
    return _call(name, body, grid, in_specs, [pl.BlockSpec(blk, mp) for _, _, blk, mp in outs],
                 [jax.ShapeDtypeStruct(s, d) for s, d, _, _ in outs], args, comm)


def _store(accs, ex, out):
    out[0][...] = accs[0].astype(out[0].dtype)


def _ew_tile(r, c, budget=3 << 19):
    for t in range(r - r % 16, 0, -16):
        if r % t == 0 and t * c * 4 <= budget:
            return t, c
    for t in range(c - c % 128, 0, -128):
        if c % t == 0 and r * t * 4 <= budget:
            return r, t
    return r, c


def _tile(n, want):
    t = min(n, want)
    assert n % t == 0, (n, want)
    return t


def _mm_nn(name, a, b, out_dtype, tm=512, tn=512, comm=()):
    m, k = a.shape
    n = b.shape[1]
    tm, tn = _tile(m, tm), (tn if n % tn == 0 else n)
    return _mm(name, (n // tn, m // tm),
               [(a, (tm, k), lambda j, i: (i, 0), b, (k, tn), lambda j, i: (0, j), "nn", 0, 0)], [],
               [((m, n), out_dtype, (tm, tn), lambda j, i: (i, j))], _store, comm=comm)[0]


def _mm_nt(name, a, bt, out_dtype, tm=512, tn=512, comm=(), rows=None):
    m, k = a.shape
    n = rows or bt.shape[0]
    tm, tn = _tile(m, tm), (tn if n % tn == 0 else n)
    return _mm(name, (n // tn, m // tm),
               [(a, (tm, k), lambda j, i: (i, 0), bt, (tn, k), lambda j, i: (j, 0), "nt", 0, 0)], [],
               [((m, n), out_dtype, (tm, tn), lambda j, i: (i, j))], _store, comm=comm)[0]


def _mm_tn_into(name, a, b, buf, row0, ta=1024, tb=512):
    t, ka = a.shape
    nb = b.shape[1]
    ta, tb = (ta if ka % ta == 0 else ka), (tb if nb % tb == 0 else nb)

    def body(a_ref, b_ref, buf_in, buf_out, tile, sem):
        i, j = pl.program_id(0), pl.program_id(1)
        tile[...] = lax.dot_general(a_ref[...], b_ref[...], _DN["tn"], preferred_element_type=F32).astype(tile.dtype)
        rows = pl.ds(pl.multiple_of(row0 + i * ta, 16), ta)
        cp = pltpu.make_async_copy(tile, buf_out.at[rows, pl.ds(pl.multiple_of(j * tb, 128), tb)], sem)
        cp.start()
        cp.wait()

    return pl.pallas_call(
        body, name=name, grid=(ka // ta, nb // tb),
        in_specs=[pl.BlockSpec((t, ta), lambda i, j: (0, i)), pl.BlockSpec((t, tb), lambda i, j: (0, j)), ANY],
        out_specs=ANY, out_shape=jax.ShapeDtypeStruct(buf.shape, buf.dtype), input_output_aliases={2: 0},
        scratch_shapes=[pltpu.VMEM((ta, tb), buf.dtype), pltpu.SemaphoreType.DMA],
        compiler_params=_params(2))(a, b, buf)


def _mm_tn(name, a, b, out_dtype, ta=512, tb=512, scale=None):
    t, ka = a.shape
    nb = b.shape[1]
    ta, tb = (ta if ka % ta == 0 else ka), (tb if nb % tb == 0 else nb)

    def epi(accs, ex, out):
        v = accs[0] if scale is None else accs[0] * scale
        out[0][...] = v.astype(out[0].dtype)

    return _mm(name, (ka // ta, nb // tb),
               [(a, (t, ta), lambda i, j: (0, i), b, (t, tb), lambda i, j: (0, j), "tn", 0, 0)], [],
               [((ka, nb), out_dtype, (ta, tb), lambda i, j: (i, j))], epi)[0]


def _rms_fwd(name, x, g, tm=256, comm=()):
    s, d = x.shape
    tm = _tile(s, tm)

    def body(x_ref, g_ref, o_ref):
        v = x_ref[...]
        o_ref[...] = (v * lax.rsqrt(jnp.mean(v * v, axis=-1, keepdims=True) + EPS) * g_ref[...]).astype(o_ref.dtype)

    return _call(name, body, (s // tm,),
                 [pl.BlockSpec((tm, d), lambda i: (i, 0)), pl.BlockSpec((1, d), lambda i: (0, 0))],
                 pl.BlockSpec((tm, d), lambda i: (i, 0)), jax.ShapeDtypeStruct((s, d), BF), [x, g], comm)


def _acc_rows(ref, part, i):
    @pl.when(i == 0)
    def _():
        ref[...] = part

    @pl.when(i > 0)
    def _():
        ref[...] += part


def _rms_bwd_math(dn, v, g):
    rstd = lax.rsqrt(jnp.mean(v * v, axis=-1, keepdims=True) + EPS)
    xh = v * rstd
    dxh = dn * g
    dx = rstd * (dxh - xh * jnp.mean(dxh * xh, axis=-1, keepdims=True))
    return dx, jnp.sum(dn * xh, axis=0, keepdims=True)


def _rms_bwd(name, dn, x, g, resid, tm=256, comm=()):
    s, d = x.shape
    tm = _tile(s, tm)

    def body(dn_ref, x_ref, g_ref, r_ref, dx_ref, dxb_ref, dg_ref):
        dx, part = _rms_bwd_math(dn_ref[...].astype(F32), x_ref[...], g_ref[...])
        tot = r_ref[...] + dx
        dx_ref[...] = tot
        dxb_ref[...] = tot.astype(BF)
        _acc_rows(dg_ref, part, pl.program_id(0))

    row = pl.BlockSpec((tm, d), lambda i: (i, 0))
    one = pl.BlockSpec((1, d), lambda i: (0, 0))
    return _call(name, body, (s // tm,), [row, row, one, row], [row, row, one],
                 [jax.ShapeDtypeStruct((s, d), F32), jax.ShapeDtypeStruct((s, d), BF),
                  jax.ShapeDtypeStruct((1, d), F32)], [dn, x, g, resid], comm)


def _loss_head(h, target, g, tm=256):
    s, d = h.shape
    tm = _tile(s, tm)

    def body(h_ref, t_ref, g_ref, dh_ref, dg_ref, loss_ref):
        v, gv = h_ref[...], g_ref[...]
        rstd = lax.rsqrt(jnp.mean(v * v, axis=-1, keepdims=True) + EPS)
        xh = v * rstd
        err = xh * gv - t_ref[...]
        part_loss = 0.5 * jnp.sum(jnp.mean(err * err, axis=-1, keepdims=True), axis=0, keepdims=True)
        dy = err * (1.0 / d)
        dxh = dy * gv
        dh_ref[...] = rstd * (dxh - xh * jnp.mean(dxh * xh, axis=-1, keepdims=True))
        i = pl.program_id(0)
        _acc_rows(dg_ref, jnp.sum(dy * xh, axis=0, keepdims=True), i)
        _acc_rows(loss_ref, jnp.broadcast_to(part_loss, loss_ref.shape), i)

    row = pl.BlockSpec((tm, d), lambda i: (i, 0))
    one = pl.BlockSpec((1, d), lambda i: (0, 0))
    return pl.pallas_call(
        body, name="loss_head", grid=(s // tm,), in_specs=[row, row, one],
        out_specs=[row, one, pl.BlockSpec((1, 128), lambda i: (0, 0))],
        out_shape=[jax.ShapeDtypeStruct((s, d), F32), jax.ShapeDtypeStruct((1, d), F32),
                   jax.ShapeDtypeStruct((1, 128), F32)],
        compiler_params=_params(1))(h, target, g)


def _pl_bwd_elem(dh, pe, t, tm=256):
    s, d = dh.shape
    tm = _tile(s, tm)

    def body(dh_ref, pe_ref, t_ref, dt_ref, dpe_ref):
        dh_v, sg = dh_ref[...], _sig(t_ref[...])
        dt_ref[...] = (dh_v * pe_ref[...].astype(F32) * sg * (1.0 - sg)).astype(BF)
        dpe_ref[...] = (dh_v * sg).astype(BF)

    row = pl.BlockSpec((tm, d), lambda i: (i, 0))
    return pl.pallas_call(
        body, name="pl_bwd_elem", grid=(s // tm,), in_specs=[row, row, row], out_specs=[row, row],
        out_shape=[jax.ShapeDtypeStruct((s, d), BF)] * 2, compiler_params=_params(1))(dh, pe, t)


def _ffn_up(name, xn, wg, wu, tm=1024, comm=()):
    s, d = xn.shape
    g, fb, _ = wg.shape
    tm = _tile(s, tm)

    def epi(accs, ex, out):
        hg, hu = accs
        out[0][...] = hg.astype(BF)
        out[1][...] = hu.astype(BF)
        out[2][...] = (hg * _sig(hg) * hu).astype(BF)

    a_map = lambda j, i: (i, 0)
    w_map = lambda j, i: (j, 0, 0)
    o = ((g, s, fb), BF, (None, tm, fb), lambda j, i: (j, i, 0))
    return _mm(name, (g, s // tm),
               [(xn, (tm, d), a_map, wg, (None, fb, d), w_map, "nt", 0, 0),
                (xn, (tm, d), a_map, wu, (None, fb, d), w_map, "nt", 1, 0)], [], [o, o, o], epi, nacc=2, comm=comm,
               sub=4)


def _ffn_down(name, a, wd, resid, tm=1024, tn=512, comm=()):
    g, s, fb = a.shape
    d = wd.shape[2]
    tm, tn = _tile(s, tm), _tile(d, tn)

    def epi(accs, ex, out):
        out[0][...] = ex[0][...] + 0.5 * accs[0]

    return _mm(name, (d // tn, s // tm),
               [(a, (g, tm, fb), lambda j, i: (0, i, 0), wd, (g, fb, tn), lambda j, i: (0, 0, j), "nn", 0, g)],
               [(resid, (tm, tn), lambda j, i: (i, j))],
               [((s, d), F32, (tm, tn), lambda j, i: (i, j))], epi, comm=comm)[0]


def _ffn_bwd_act(name, dh, wd, hg, hu, tm=1024, comm=()):
    s, d = dh.shape
    g, fb, _ = wd.shape
    tm = _tile(s, tm)

    def epi(accs, ex, out):
        da = 0.5 * accs[0]
        hg_v, hu_v = ex[0][...].astype(F32), ex[1][...].astype(F32)
        sg = _sig(hg_v)
        out[0][...] = (da * hu_v * (sg * (1.0 + hg_v * (1.0 - sg)))).astype(BF)
        out[1][...] = (da * (hg_v * sg)).astype(BF)

    blk = (None, tm, fb)
    gmap = lambda j, i: (j, i, 0)
    return _mm(name, (g, s // tm),
               [(dh, (tm, d), lambda j, i: (i, 0), wd, (None, fb, d), lambda j, i: (j, 0, 0), "nt", 0, 0)],
               [(hg, blk, gmap), (hu, blk, gmap)],
               [((g, s, fb), BF, blk, gmap), ((g, s, fb), BF, blk, gmap)], epi, comm=comm, sub=4)


def _ffn_bwd_wd(name, a, dh, tn=1024, comm=()):
    g, s, fb = a.shape
    d = dh.shape[1]
    tn = _tile(d, tn)

    def epi(accs, ex, out):
        out[0][...] = (0.5 * accs[0]).astype(BF)

    return _mm(name, (g, d // tn),
               [(a, (None, s, fb), lambda j, i: (j, 0, 0), dh, (s, tn), lambda j, i: (0, i), "tn", 0, 0)], [],
               [((g, fb, d), BF, (None, fb, tn), lambda j, i: (j, 0, i))], epi, comm=comm)[0]


def _ffn_bwd_wup(name, xn, dhg, dhu, tk=1024, comm=()):
    s, d = xn.shape
    g, _, fb = dhg.shape
    tk = _tile(d, tk)

    def epi(accs, ex, out):
        out[0][...] = accs[0].astype(BF)
        out[1][...] = accs[1].astype(BF)

    a_map = lambda j, i: (j, 0, 0)
    b_map = lambda j, i: (0, i)
    o = ((g, fb, d), BF, (None, fb, tk), lambda j, i: (j, 0, i))
    return _mm(name, (g, d // tk),
               [(dhg, (None, s, fb), a_map, xn, (s, tk), b_map, "tn", 0, 0),
                (dhu, (None, s, fb), a_map, xn, (s, tk), b_map, "tn", 1, 0)], [], [o, o], epi, nacc=2, comm=comm)


def _ffn_bwd_x(name, dhg, dhu, wg, wu, tm=512, tn=512, comm=()):
    g, s, fb = dhg.shape
    d = wg.shape[2]
    tm, tn = _tile(s, tm), _tile(d, tn)
    a_blk, a_map = (g, tm, fb), lambda j, i: (0, i, 0)
    b_blk, b_map = (g, fb, tn), lambda j, i: (0, 0, j)
    return _mm(name, (d // tn, s // tm),
               [(dhg, a_blk, a_map, wg, b_blk, b_map, "nn", 0, g), (dhu, a_blk, a_map, wu, b_blk, b_map, "nn", 0, g)],
               [], [((s, d), F32, (tm, tn), lambda j, i: (i, j))], _store, comm=comm)[0]


def _ffn_forward(tag, h, gain, get_wgu, get_wd, norm_comm=(), up_comm=(), down_comm=()):
    xn = _rms_fwd(tag + "_norm", h, gain, comm=norm_comm)
    wg, wu = get_wgu()
    hg, hu, a = _ffn_up(tag + "_up", xn, wg, wu, comm=up_comm)
    return _ffn_down(tag + "_down", a, get_wd(), h, comm=down_comm), (xn, hg, hu, a)


def _na_geometry(rows):
    kh = min(NA_ROWS_WIN, rows)
    cols = np.arange(GRID_W)
    col_start = np.clip(cols - NA_COLS_WIN // 2, 0, GRID_W - NA_COLS_WIN)
    mask = (cols[None, :] >= col_start[:, None]) & (cols[None, :] < col_start[:, None] + NA_COLS_WIN)
    dc = np.clip(cols[None, :] - cols[:, None], -(NA_COLS_WIN - 1), NA_COLS_WIN - 1) + (NA_COLS_WIN - 1)
    return kh, mask, dc


def _na_table(rpb, rows):
    _, mask, dc = _na_geometry(rows)
    nd, nc, cells = 2 * NA_ROWS_WIN - 1, 2 * NA_COLS_WIN - 1, GRID_W * GRID_W
    onehot = np.zeros((128, cells), np.float32)
    onehot[dc.reshape(-1), np.arange(cells)] = mask.reshape(-1).astype(np.float32)
    off = np.where(mask.reshape(1, -1), 0.0, NEG).astype(np.float32)

    def body(r_ref, e_ref, off_ref, o_ref):
        o_ref[...] = jnp.dot(r_ref[...], e_ref[...], precision=HI, preferred_element_type=F32) + off_ref[...]

    flat = pl.pallas_call(body, name="na_table", out_shape=jax.ShapeDtypeStruct((NA_HEADS * nd, cells), F32),
                          compiler_params=_params(0))(
        jnp.pad(rpb.reshape(NA_HEADS * nd, nc), ((0, 0), (0, 128 - nc))), jnp.asarray(onehot), jnp.asarray(off))
    return flat.reshape(NA_HEADS, nd, GRID_W, GRID_W)


class _NaPlan:
    def __init__(self, s):
        self.s, self.rows = s, s // GRID_W
        self.kh = min(NA_ROWS_WIN, self.rows)
        self.qr = min(NA_QROWS, self.rows)
        self.kr = min(self.rows, self.kh + self.qr - 1)
        self.groups = self.rows // self.qr
        self.nd = 2 * NA_ROWS_WIN - 1
        self.hw, self.nq = NA_HG * NA_DIM, NA_HEADS // NA_HG
        clip = lambda v, hi: min(max(v, 0), hi)
        pats = [(clip(g * self.qr - self.kh // 2, self.rows - self.kr) - g * self.qr,)
                + tuple(clip(g * self.qr + a - self.kh // 2, self.rows - self.kh) - g * self.qr for a in range(self.qr))
                for g in range(self.groups)]
        self.rebuild = [g for g in range(self.groups) if g == 0 or pats[g] != pats[g - 1]]

    def first_key_row(self, g):
        return jnp.clip(g * self.qr - self.kh // 2, 0, self.rows - self.kr)

    def specs(self):
        blk = pl.BlockSpec((self.qr * GRID_W, self.hw), lambda j, g: (g, j))
        k_spec = pl.BlockSpec((self.s, self.hw), lambda j, g: (0, self.nq + j))
        v_spec = pl.BlockSpec((self.s, self.hw), lambda j, g: (0, 2 * self.nq + j))
        t_spec = pl.BlockSpec((NA_HG, self.nd, GRID_W, GRID_W), lambda j, g: (j, 0, 0, 0))
        return blk, k_spec, v_spec, t_spec

    def bias_scratch(self):
        return pltpu.VMEM((NA_HG, self.qr * GRID_W, self.kr * GRID_W), F32)

    def fill_bias(self, t_ref, bias_ref, g):
        def build():
            r0, ks = g * self.qr, self.first_key_row(g)
            for a in range(self.qr):
                rs = jnp.clip(r0 + a - self.kh // 2, 0, self.rows - self.kh)
                for i in range(self.kr):
                    valid = jnp.logical_and(ks + i >= rs, ks + i < rs + self.kh)
                    idx = jnp.clip(ks + i - r0 - a + NA_ROWS_WIN - 1, 0, self.nd - 1)
                    for h in range(NA_HG):
                        bias_ref[h, a * GRID_W:(a + 1) * GRID_W, i * GRID_W:(i + 1) * GRID_W] = jnp.where(
                            valid, t_ref[h, idx], NEG)

        pl.when(functools.reduce(jnp.logical_or, [g == r for r in self.rebuild]))(build)

    def window(self, g):
        return pl.ds(pl.multiple_of(self.first_key_row(g) * GRID_W, GRID_W), self.kr * GRID_W)


def _na_probs(q, k, bias):
    sc = lax.dot_general(q, k, _DN["nt"], preferred_element_type=F32) * (NA_DIM ** -0.5) + bias
    e = jnp.exp(sc - jnp.max(sc, axis=-1, keepdims=True))
    return e / jnp.sum(e, axis=-1, keepdims=True)


def _na_fwd(qkv, table, comm=()):
    plan = _NaPlan(qkv.shape[0])
    blk, k_spec, v_spec, t_spec = plan.specs()

    def body(q_ref, k_ref, v_ref, t_ref, o_ref, bias_ref):
        g = pl.program_id(1)
        plan.fill_bias(t_ref, bias_ref, g)
        win = plan.window(g)
        for h in range(NA_HG):
            cs = slice(h * NA_DIM, (h + 1) * NA_DIM)
            p = _na_probs(q_ref[:, cs], k_ref[win, cs], bias_ref[h])
            o_ref[:, cs] = jnp.dot(p.astype(BF), v_ref[win, cs], preferred_element_type=F32).astype(BF)

    return _call("na_fwd", body, (plan.nq, plan.groups), [blk, k_spec, v_spec, t_spec], blk,
                 jax.ShapeDtypeStruct((plan.s, NA_HEADS * NA_DIM), BF), [qkv, qkv, qkv, table], comm,
                 scratch=[plan.bias_scratch()])


def _na_bwd(qkv, table, do, comm=()):
    plan = _NaPlan(qkv.shape[0])
    blk, k_spec, v_spec, t_spec = plan.specs()
    qr, kr = plan.qr, plan.kr

    def body(q_ref, k_ref, v_ref, t_ref, do_ref, dq_ref, dk_ref, dv_ref, dt_ref, bias_ref):
        g = pl.program_id(1)

        @pl.when(g == 0)
        def _():
            dk_ref[...] = jnp.zeros_like(dk_ref)
            dv_ref[...] = jnp.zeros_like(dv_ref)
            dt_ref[...] = jnp.zeros_like(dt_ref)

        plan.fill_bias(t_ref, bias_ref, g)
        win = plan.window(g)
        base = plan.first_key_row(g) - g * qr + NA_ROWS_WIN - 1
        for h in range(NA_HG):
            cs = slice(h * NA_DIM, (h + 1) * NA_DIM)
            q, k, v, do_h = q_ref[:, cs], k_ref[win, cs], v_ref[win, cs], do_ref[:, cs]
            p = _na_probs(q, k, bias_ref[h])
            dp = lax.dot_general(do_h, v, _DN["nt"], preferred_element_type=F32)
            ds = p * (dp - jnp.sum(p * dp, axis=-1, keepdims=True))
            for dlt in range(1 - qr, kr):
                tiles = [ds[a * GRID_W:(a + 1) * GRID_W, (a + dlt) * GRID_W:(a + dlt + 1) * GRID_W]
                         for a in range(qr) if 0 <= a + dlt < kr]
                dt_ref[h, jnp.clip(base + dlt, 0, plan.nd - 1)] += functools.reduce(jnp.add, tiles)
            dsb = (ds * (NA_DIM ** -0.5)).astype(BF)
            dq_ref[:, cs] = jnp.dot(dsb, k, preferred_element_type=F32).astype(BF)
            dk_ref[win, cs] += lax.dot_general(dsb, q, _DN["tn"], preferred_element_type=F32)
            dv_ref[win, cs] += lax.dot_general(p.astype(BF), do_h, _DN["tn"], preferred_element_type=F32)

    width = NA_HEADS * NA_DIM
    whole = pl.BlockSpec((plan.s, plan.hw), lambda j, g: (0, j))
    return _call(
        "na_bwd", body, (plan.nq, plan.groups), [blk, k_spec, v_spec, t_spec, blk], [blk, whole, whole, t_spec],
        [jax.ShapeDtypeStruct((plan.s, width), BF), jax.ShapeDtypeStruct((plan.s, width), F32),
         jax.ShapeDtypeStruct((plan.s, width), F32),
         jax.ShapeDtypeStruct((NA_HEADS, plan.nd, GRID_W, GRID_W), F32)],
        [qkv, qkv, qkv, table, do], comm, scratch=[plan.bias_scratch()])


def _na_rpb_grad(dt, rows):
    _, mask, dc = _na_geometry(rows)
    nd, nc = 2 * NA_ROWS_WIN - 1, 2 * NA_COLS_WIN - 1
    onehot = np.zeros((GRID_W * GRID_W, 128), np.float32)
    onehot[np.arange(GRID_W * GRID_W), dc.reshape(-1)] = mask.reshape(-1).astype(np.float32)
    flat = dt.reshape(NA_HEADS * nd, GRID_W * GRID_W)

    def body(a_ref, e_ref, o_ref):
        o_ref[...] = jnp.dot(a_ref[...], e_ref[...], precision=HI, preferred_element_type=F32)

    out = pl.pallas_call(body, name="na_rpb_grad", out_shape=jax.ShapeDtypeStruct((NA_HEADS * nd, 128), F32),
                         compiler_params=_params(0))(flat, jnp.asarray(onehot))
    return out[:, :nc].reshape(NA_HEADS, nd, nc)


def _rope_consts(s):
    pos = np.arange(s, dtype=np.float32)
    inv = (1.0 / (ROPE_THETA ** (np.arange(0, ML_ROPE, 2, dtype=np.float32) / ML_ROPE))).astype(np.float32)
    ang = pos[:, None] * inv[None, :]
    cos, sin = np.cos(ang).astype(np.float32), np.sin(ang).astype(np.float32)
    half = ML_ROPE // 2
    rot = np.zeros((ML_ROPE, ML_ROPE), np.float32)
    rot[np.arange(half) + half, np.arange(half)] = -1.0
    rot[np.arange(half), np.arange(half) + half] = 1.0
    return (jnp.asarray(np.concatenate([cos, cos], 1)), jnp.asarray(np.concatenate([sin, sin], 1)),
            jnp.asarray(rot), jnp.asarray(rot.T.copy()))


def _rope(v, cos, sin, rot):
    return v * cos + jnp.dot(v, rot, precision=HI, preferred_element_type=F32) * sin


def _unrope(dv, cos, sin, rot_t):
    return dv * cos + jnp.dot(dv * sin, rot_t, precision=HI, preferred_element_type=F32)


def _rms(v, g):
    return v * lax.rsqrt(jnp.mean(v * v, axis=-1, keepdims=True) + EPS) * g


def _mla_prep(lat, gq, gkv, cos, sin, rot, tm=256):
    s, w = lat.shape
    tm = _tile(s, tm)

    def body(l_ref, gq_ref, gkv_ref, c_ref, s_ref, r_ref, cq_ref, ckv_ref, kr_ref):
        cq_ref[...] = _rms(l_ref[:, :ML_RANK], gq_ref[...]).astype(BF)
        ckv_ref[...] = _rms(l_ref[:, ML_RANK:2 * ML_RANK], gkv_ref[...]).astype(BF)
        kr_ref[...] = _rope(l_ref[:, 2 * ML_RANK:], c_ref[...], s_ref[...], r_ref[...]).astype(BF)

    row = lambda c: pl.BlockSpec((tm, c), lambda i: (i, 0))
    full = lambda a: pl.BlockSpec(a.shape, lambda i: (0, 0))
    return pl.pallas_call(
        body, name="mla_prep", grid=(s // tm,),
        in_specs=[row(w), full(gq), full(gkv), row(ML_ROPE), row(ML_ROPE), full(rot)],
        out_specs=[row(ML_RANK), row(ML_RANK), row(ML_ROPE)],
        out_shape=[jax.ShapeDtypeStruct((s, ML_RANK), BF), jax.ShapeDtypeStruct((s, ML_RANK), BF),
                   jax.ShapeDtypeStruct((s, ML_ROPE), BF)],
        compiler_params=_params(1))(lat, gq, gkv, cos, sin, rot)


def _mla_q_proj(cq, wuq, cos, sin, rot, tm=512, comm=()):
    s, k = cq.shape
    tm = _tile(s, tm)

    def epi(accs, ex, out):
        acc = accs[0]
        out[0][:, :ML_NOPE] = acc[:, :ML_NOPE].astype(BF)
        out[0][:, ML_NOPE:] = _rope(acc[:, ML_NOPE:], ex[0][...], ex[1][...], ex[2][...]).astype(BF)

    rmap = lambda j, i: (i, 0)
    return _mm("mla_q_proj", (ML_HEADS, s // tm),
               [(cq, (tm, k), rmap, wuq, (None, ML_QK, k), lambda j, i: (j, 0, 0), "nt", 0, 0)],
               [(cos, (tm, ML_ROPE), rmap), (sin, (tm, ML_ROPE), rmap), (rot, rot.shape, lambda j, i: (0, 0))],
               [((ML_HEADS, s, ML_QK), BF, (None, tm, ML_QK), lambda j, i: (j, i, 0))], epi, comm=comm)[0]


def _mla_kv_proj(ckv, wukv, kr, tm=512, comm=()):
    s, k = ckv.shape
    tm = _tile(s, tm)

    def epi(accs, ex, out):
        acc = accs[0]
        out[0][:, :ML_NOPE] = acc[:, :ML_NOPE].astype(BF)
        out[0][:, ML_NOPE:] = ex[0][...]
        out[1][...] = acc[:, ML_NOPE:].astype(BF)

    rmap = lambda j, i: (i, 0)
    gmap = lambda j, i: (j, i, 0)
    return _mm("mla_kv_proj", (ML_HEADS, s // tm),
               [(ckv, (tm, k), rmap, wukv, (None, k, ML_NOPE + ML_V), lambda j, i: (j, 0, 0), "nn", 0, 0)],
               [(kr, (tm, ML_ROPE), rmap)],
               [((ML_HEADS, s, ML_QK), BF, (None, tm, ML_QK), gmap), ((ML_HEADS, s, ML_V), BF, (None, tm, ML_V), gmap)],
               epi, comm=comm)


def _mla_probs(q, k):
    sc = lax.dot_general(q, k, _DN["nt"], preferred_element_type=F32) * (ML_QK ** -0.5)
    e = jnp.exp(sc - jnp.max(sc, axis=-1, keepdims=True))
    return e / jnp.sum(e, axis=-1, keepdims=True)


def _mla_fwd(q, k, v, tq=1024, comm=()):
    _, s, _ = q.shape
    tq = _tile(s, tq)

    def body(q_ref, k_ref, v_ref, o_ref):
        p = _mla_probs(q_ref[...], k_ref[...])
        o_ref[...] = jnp.dot(p.astype(BF), v_ref[...], preferred_element_type=F32).astype(BF)

    return _call("mla_fwd", body, (ML_HEADS, s // tq),
                 [pl.BlockSpec((None, tq, ML_QK), lambda h, i: (h, i, 0)),
                  pl.BlockSpec((None, s, ML_QK), lambda h, i: (h, 0, 0)),
                  pl.BlockSpec((None, s, ML_V), lambda h, i: (h, 0, 0))],
                 pl.BlockSpec((tq, ML_V), lambda h, i: (i, h)),
                 jax.ShapeDtypeStruct((s, ML_HEADS * ML_V), BF), [q, k, v], comm)


def _mla_bwd(q, k, v, do, tq=1024, comm=()):
    _, s, _ = q.shape
    tq = _tile(s, tq)

    def body(q_ref, k_ref, v_ref, do_ref, dq_ref, dk_ref, dv_ref):
        i = pl.program_id(1)
        qv, kv, vv, dov = q_ref[...], k_ref[...], v_ref[...], do_ref[...]
        p = _mla_probs(qv, kv)
        dp = lax.dot_general(dov, vv, _DN["nt"], preferred_element_type=F32)
        ds = (p * (dp - jnp.sum(p * dp, axis=-1, keepdims=True)) * (ML_QK ** -0.5)).astype(BF)
        dq_ref[...] = jnp.dot(ds, kv, preferred_element_type=F32)
        _acc_rows(dk_ref, lax.dot_general(ds, qv, _DN["tn"], preferred_element_type=F32), i)
        _acc_rows(dv_ref, lax.dot_general(p.astype(BF), dov, _DN["tn"], preferred_element_type=F32), i)

    return _call(
        "mla_bwd", body, (ML_HEADS, s // tq),
        [pl.BlockSpec((None, tq, ML_QK), lambda h, i: (h, i, 0)),
         pl.BlockSpec((None, s, ML_QK), lambda h, i: (h, 0, 0)),
         pl.BlockSpec((None, s, ML_V), lambda h, i: (h, 0, 0)),
         pl.BlockSpec((tq, ML_V), lambda h, i: (i, h))],
        [pl.BlockSpec((None, tq, ML_QK), lambda h, i: (h, i, 0)),
         pl.BlockSpec((None, s, ML_QK), lambda h, i: (h, 0, 0)),
         pl.BlockSpec((None, s, ML_V), lambda h, i: (h, 0, 0))],
        [jax.ShapeDtypeStruct((ML_HEADS, s, ML_QK), F32), jax.ShapeDtypeStruct((ML_HEADS, s, ML_QK), F32),
         jax.ShapeDtypeStruct((ML_HEADS, s, ML_V), F32)],
        [q, k, v, do], comm)


def _mla_post(dq, dk, dv, cos, sin, rot_t, tm=1024):
    _, s, _ = dq.shape
    tm = _tile(s, tm)

    def body(dq_ref, dk_ref, dv_ref, c_ref, s_ref, r_ref, dqp_ref, dkv_ref, dkr_ref):
        h = pl.program_id(1)
        dqv, dkk = dq_ref[...], dk_ref[...]
        dqp_ref[:, :ML_NOPE] = dqv[:, :ML_NOPE].astype(BF)
        dqp_ref[:, ML_NOPE:] = _unrope(dqv[:, ML_NOPE:], c_ref[...], s_ref[...], r_ref[...]).astype(BF)
        dkv_ref[:, :ML_NOPE] = dkk[:, :ML_NOPE].astype(BF)
        dkv_ref[:, ML_NOPE:] = dv_ref[...].astype(BF)
        _acc_rows(dkr_ref, dkk[:, ML_NOPE:], h)

    gspec = lambda c: pl.BlockSpec((None, tm, c), lambda i, h: (h, i, 0))
    rspec = pl.BlockSpec((tm, ML_ROPE), lambda i, h: (i, 0))
    return pl.pallas_call(
        body, name="mla_post", grid=(s // tm, ML_HEADS),
        in_specs=[gspec(ML_QK), gspec(ML_QK), gspec(ML_V), rspec, rspec,
                  pl.BlockSpec(rot_t.shape, lambda i, h: (0, 0))],
        out_specs=[gspec(ML_QK), gspec(ML_NOPE + ML_V), rspec],
        out_shape=[jax.ShapeDtypeStruct((ML_HEADS, s, ML_QK), BF),
                   jax.ShapeDtypeStruct((ML_HEADS, s, ML_NOPE + ML_V), BF),
                   jax.ShapeDtypeStruct((s, ML_ROPE), F32)],
        compiler_params=_params(2))(dq, dk, dv, cos, sin, rot_t)


def _mla_lat_bwd(dcq, dckv, dkr, lat, gq, gkv, cos, sin, rot_t, tm=256):
    s, w = lat.shape
    tm = _tile(s, tm)

    def body(dcq_ref, dckv_ref, dkr_ref, l_ref, gq_ref, gkv_ref, c_ref, s_ref, r_ref, dl_ref, dgq_ref, dgkv_ref):
        i = pl.program_id(0)
        dql, pq = _rms_bwd_math(dcq_ref[...], l_ref[:, :ML_RANK], gq_ref[...])
        dkl, pkv = _rms_bwd_math(dckv_ref[...], l_ref[:, ML_RANK:2 * ML_RANK], gkv_ref[...])
        dl_ref[:, :ML_RANK] = dql.astype(BF)
        dl_ref[:, ML_RANK:2 * ML_RANK] = dkl.astype(BF)
        dl_ref[:, 2 * ML_RANK:] = _unrope(dkr_ref[...], c_ref[...], s_ref[...], r_ref[...]).astype(BF)
        _acc_rows(dgq_ref, pq, i)
        _acc_rows(dgkv_ref, pkv, i)

    row = lambda c: pl.BlockSpec((tm, c), lambda i: (i, 0))
    full = lambda a: pl.BlockSpec(a.shape, lambda i: (0, 0))
    return pl.pallas_call(
        body, name="mla_lat_bwd", grid=(s // tm,),
        in_specs=[row(ML_RANK), row(ML_RANK), row(ML_ROPE), row(w), full(gq), full(gkv), row(ML_ROPE), row(ML_ROPE),
                  full(rot_t)],
        out_specs=[row(w), full(gq), full(gkv)],
        out_shape=[jax.ShapeDtypeStruct((s, w), BF), jax.ShapeDtypeStruct(gq.shape, F32),
                   jax.ShapeDtypeStruct(gkv.shape, F32)],
        compiler_params=_params(1))(dcq, dckv, dkr, lat, gq, gkv, cos, sin, rot_t)


def _grp_dw(name, a, dout, ta=1024):
    s, k = a.shape
    ta = _tile(k, ta)
    if dout.ndim == 3:
        g, _, nb = dout.shape
        b_blk, b_map = (None, s, nb), lambda j, i: (j, 0, 0)
    else:
        g, nb = NDEV, dout.shape[1] // NDEV
        b_blk, b_map = (s, nb), lambda j, i: (0, j)
    return _mm(name, (g, k // ta),
               [(a, (s, ta), lambda j, i: (0, i), dout, b_blk, b_map, "tn", 0, 0)], [],
               [((g, k, nb), BF, (None, ta, nb), lambda j, i: (j, i, 0))], _store)[0]


def _grp_dw_t(name, dout, a, ta=512):
    g, s, nb = dout.shape
    k = a.shape[1]
    ta = _tile(k, ta)
    return _mm(name, (g, k // ta),
               [(dout, (None, s, nb), lambda j, i: (j, 0, 0), a, (s, ta), lambda j, i: (0, i), "tn", 0, 0)], [],
               [((g, nb, k), BF, (None, nb, ta), lambda j, i: (j, 0, i))], _store)[0]


def _grp_dx_t(name, dout, wt, tm=512, tn=512, comm=()):
    g, s, nb = dout.shape
    k = wt.shape[2]
    tm, tn = _tile(s, tm), _tile(k, tn)
    return _mm(name, (k // tn, s // tm),
               [(dout, (g, tm, nb), lambda j, i: (0, i, 0), wt, (g, nb, tn), lambda j, i: (0, 0, j), "nn", 0, g)], [],
               [((s, k), F32, (tm, tn), lambda j, i: (i, j))], _store, comm=comm)[0]


def _grp_dx(name, dout, w, tm=512, tn=512, out_dtype=F32, comm=()):
    g, s, nb = dout.shape
    k = w.shape[1]
    tm, tn = _tile(s, tm), _tile(k, tn)
    return _mm(name, (k // tn, s // tm),
               [(dout, (g, tm, nb), lambda j, i: (0, i, 0), w, (g, tn, nb), lambda j, i: (0, j, 0), "nt", 0, g)], [],
               [((s, k), out_dtype, (tm, tn), lambda j, i: (i, j))], _store, comm=comm)[0]


def _row_dw(name, a, dout, tn=2048):
    s, n = dout.shape
    tn = _tile(n, tn)
    if a.ndim == 3:
        kb = a.shape[2]
        a_blk, a_map = (None, s, kb), lambda j, i: (j, 0, 0)
    else:
        kb = a.shape[1] // NDEV
        a_blk, a_map = (s, kb), lambda j, i: (0, j)
    return _mm(name, (NDEV, n // tn),
               [(a, a_blk, a_map, dout, (s, tn), lambda j, i: (0, i), "tn", 0, 0)], [],
               [((NDEV, kb, n), BF, (None, kb, tn), lambda j, i: (j, 0, i))], _store)[0]


def _mix_merge(oa, ob, wa, wb, ga, gb, tm=1024, comm=()):
    s, k = oa.shape
    g, _, nb = wa.shape
    tm = _tile(s, tm)

    def epi(accs, ex, out):
        ya, yb = accs
        out[0][...] = ya.astype(BF)
        out[1][...] = yb.astype(BF)
        out[2][...] = (_sig(ex[0][...]) * ya + _sig(ex[1][...]) * yb).astype(BF)

    rmap = lambda j, i: (i, 0)
    wmap = lambda j, i: (j, 0, 0)
    o = ((g, s, nb), BF, (None, tm, nb), lambda j, i: (j, i, 0))
    cmap = lambda j, i: (i, j)
    return _mm("mix_merge", (g, s // tm),
               [(oa, (tm, k), rmap, wa, (None, k, nb), wmap, "nn", 0, 0),
                (ob, (tm, k), rmap, wb, (None, k, nb), wmap, "nn", 1, 0)],
               [(ga, (tm, nb), cmap), (gb, (tm, nb), cmap)], [o, o, o], epi, nacc=2, comm=comm, sub=4)


def _mix_out(merged, wout, resid, tm=1024, tn=512):
    g, s, kb = merged.shape
    d = wout.shape[2]
    tm, tn = _tile(s, tm), _tile(d, tn)

    def epi(accs, ex, out):
        out[0][...] = ex[0][...] + accs[0]

    return _mm("mix_out", (d // tn, s // tm),
               [(merged, (g, tm, kb), lambda j, i: (0, i, 0), wout, (g, kb, tn), lambda j, i: (0, 0, j), "nn", 0, g)],
               [(resid, (tm, tn), lambda j, i: (i, j))],
               [((s, d), F32, (tm, tn), lambda j, i: (i, j))], epi)[0]


def _mix_out_bwd(dh, wout, ga, gb, ya, yb, tm=1024, comm=()):
    s, d = dh.shape
    g, kb, _ = wout.shape
    tm = _tile(s, tm)

    def epi(accs, ex, out):
        dm = accs[0]
        sa, sb = _sig(ex[0][...]), _sig(ex[1][...])
        out[0][...] = (dm * sa).astype(BF)
        out[1][...] = (dm * sb).astype(BF)
        out[2][...] = (dm * ex[2][...].astype(F32) * sa * (1.0 - sa)).astype(BF)
        out[3][...] = (dm * ex[3][...].astype(F32) * sb * (1.0 - sb)).astype(BF)

    cmap = lambda j, i: (i, j)
    gmap = lambda j, i: (j, i, 0)
    og = ((g, s, kb), BF, (None, tm, kb), gmap)
    oc = ((s, g * kb), BF, (tm, kb), cmap)
    return _mm("mix_out_bwd", (g, s // tm),
               [(dh, (tm, d), lambda j, i: (i, 0), wout, (None, kb, d), lambda j, i: (j, 0, 0), "nt", 0, 0)],
               [(ga, (tm, kb), cmap), (gb, (tm, kb), cmap), (ya, (None, tm, kb), gmap), (yb, (None, tm, kb), gmap)],
               [og, og, oc, oc], epi, comm=comm, sub=4)


def _pl_forward(n4, wplg, p, wpl, h3, tm=1024):
    s, d = n4.shape
    g, kb, _ = wplg.shape
    kp, nb = wpl.shape[1], wpl.shape[2]
    tm = _tile(s, tm)
    wplg_nat = wplg.reshape(g * kb, d)

    def epi(accs, ex, out):
        t, pe = accs
        out[0][...] = ex[0][...] + _sig(t) * pe
        out[1][...] = t
        out[2][...] = pe.astype(BF)

    rmap = lambda j, i: (i, 0)
    cmap = lambda j, i: (i, j)
    return _mm("pl_forward", (g, s // tm),
               [(n4, (tm, d), rmap, wplg_nat, (g * kb, nb), lambda j, i: (0, j), "nn", 0, 0),
                (p, (tm, kp), rmap, wpl, (None, kp, nb), lambda j, i: (j, 0, 0), "nn", 1, 0)],
               [(h3, (tm, nb), cmap)],
               [((s, d), F32, (tm, nb), cmap), ((s, d), F32, (tm, nb), cmap), ((s, d), BF, (tm, nb), cmap)],
               epi, nacc=2, sub=4)


def _row_dx(name, dout, w, tm=1024, comm=()):
    s, n = dout.shape
    g, kb, _ = w.shape
    tm = _tile(s, tm)
    return _mm(name, (g, s // tm),
               [(dout, (tm, n), lambda j, i: (i, 0), w, (None, kb, n), lambda j, i: (j, 0, 0), "nt", 0, 0)], [],
               [((s, g * kb), F32, (tm, kb), lambda j, i: (i, j))], _store, comm=comm)[0]


def _in_proj_bwd_x(pieces, weights, tm=512, tn=512, comm=()):
    s = pieces[0].shape[0]
    d = weights[0].shape[1]
    tm, tn = _tile(s, tm), _tile(d, tn)
    prods = [(pc, (tm, pc.shape[1]), lambda j, i: (i, 0), w, (pc.shape[1], tn), lambda j, i: (0, j), "nn", 0, 0)
             for pc, w in zip(pieces, weights)]
    return _mm("in_proj_dx", (d // tn, s // tm), prods, [],
               [((s, d), F32, (tm, tn), lambda j, i: (i, j))], _store, comm=comm)[0]


def _split_w_in(w_in_t):
    g, nb, d = w_in_t.shape
    nat = w_in_t.reshape(g * nb, d)
    na, lat = 3 * NA_HEADS * NA_DIM, 2 * ML_RANK + ML_ROPE
    return nat, nat[na:na + lat], nat[na + lat:na + lat + d], nat[na + lat + d:]


def _pair_sum(name, part, landed, core):
    _, _, r, c = part.shape
    tr, tc = _ew_tile(r, c)

    def body(core_ref, a_ref, b_ref, o_ref):
        o_ref[...] = (a_ref[...].astype(F32) + b_ref[...].astype(F32)).astype(o_ref.dtype)

    return pl.pallas_call(
        body, name=name,
        grid_spec=pltpu.PrefetchScalarGridSpec(
            num_scalar_prefetch=1, grid=(NCHIP, r // tr, c // tc),
            in_specs=[pl.BlockSpec((None, None, tr, tc), lambda j, i, k, core_ref: (j, core_ref[0], i, k)),
                      pl.BlockSpec((None, tr, tc), lambda j, i, k, core_ref: (j, i, k))],
            out_specs=pl.BlockSpec((None, tr, tc), lambda j, i, k, core_ref: (j, i, k))),
        out_shape=jax.ShapeDtypeStruct(landed.shape, landed.dtype), compiler_params=_params(3),
    )(core, part, landed)


def _device_step(x, p, target, sp, own, core):
    s, d = x.shape
    rows = s // GRID_W
    cos, sin, rot, rot_t = _rope_consts(s)
    w, dw4, sums, dsp, pending = {}, {}, {}, {}, []

    def gather(*names):
        return _GatherPart(names, [own[n] for n in names])

    def got(part):
        w.update(zip(part.names, part.results))

    def grad(name, g):
        dw4[name] = g.reshape((NCHIP, 2) + g.shape[1:])

    def to_sibling(*names):
        return _SiblingPart(names, [dw4[n] for n in names])

    def add_pairs(part):
        for n, landed in zip(part.names, part.results):
            sums[n] = _pair_sum("pair_sum_" + n, dw4[n], landed, core)

    def start_chips(tag, *names):
        send, recv, thru, lands, token = _chips_start("rs_start_" + tag, [sums[n] for n in names])
        pending.append((tag, names, send, recv, thru, lands))
        return token

    c0 = gather("ffn1_w_gate", "ffn1_w_up")
    c1 = gather("ffn1_w_down")
    c2 = gather("w_in")

    def ffn1_wgu():
        got(c0)
        return w["ffn1_w_gate"], w["ffn1_w_up"]

    def ffn1_wd():
        got(c1)
        return w["ffn1_w_down"]

    h1, ffn1_saved = _ffn_forward("ffn1", x, sp["ffn1_norm"], ffn1_wgu, ffn1_wd,
                                  norm_comm=[c0], up_comm=[c1], down_comm=[c2])
    got(c2)
    wqkv, wlat, wga, wgb = _split_w_in(w["w_in"])
    u = _rms_fwd("mix_norm", h1, sp["mix_norm"])
    c3 = gather("w_uq", "w_ukv")
    qkv = _mm_nt("in_qkv", u, wqkv, BF, tn=1024, comm=[c3], rows=3 * NA_HEADS * NA_DIM)
    got(c3)
    lat = _mm_nt("in_lat", u, wlat, F32, tm=1024)
    c3a = gather("w_branch_a")
    ga = _mm_nt("in_ga", u, wga, F32, tm=1024, tn=1024, comm=[c3a])
    got(c3a)
    c3b = gather("w_branch_b")
    gb = _mm_nt("in_gb", u, wgb, F32, tm=1024, tn=1024, comm=[c3b])
    got(c3b)
    tb = _na_table(sp["na_rpb"], rows)
    c4 = gather("ffn2_w_gate")
    oa = _na_fwd(qkv, tb, comm=[c4])
    got(c4)
    cq, ckv, kr = _mla_prep(lat, sp["q_a_norm"], sp["kv_a_norm"], cos, sin, rot)
    c4a = gather("w_out")
    qf = _mla_q_proj(cq, w["w_uq"], cos, sin, rot, comm=[c4a])
    got(c4a)
    c4b = gather("w_pl_gate")
    kf, vf = _mla_kv_proj(ckv, w["w_ukv"], kr, comm=[c4b])
    got(c4b)
    c5 = gather("ffn2_w_up")
    ob = _mla_fwd(qf, kf, vf, comm=[c5])
    got(c5)
    c5a = gather("w_pl")
    ya, yb, merged = _mix_merge(oa, ob, w["w_branch_a"], w["w_branch_b"], ga, gb, comm=[c5a])
    got(c5a)
    h2 = _mix_out(merged, w["w_out"], h1)
    c6 = gather("ffn2_w_down")

    def ffn2_wd():
        got(c6)
        return w["ffn2_w_down"]

    h3, ffn2_saved = _ffn_forward("ffn2", h2, sp["ffn2_norm"], lambda: (w["ffn2_w_gate"], w["ffn2_w_up"]), ffn2_wd,
                                  up_comm=[c6])
    n4 = _rms_fwd("pl_norm", h3, sp["pl_norm"])
    pb = p.astype(BF)
    h4, t, pe = _pl_forward(n4, w["w_pl_gate"], pb, w["w_pl"], h3)

    dh4, dsp["final_norm"], loss = _loss_head(h4, target, sp["final_norm"])
    dt, dpe = _pl_bwd_elem(dh4, pe, t)
    grad("w_pl", _grp_dw("pl_dw", pb, dpe))
    grad("w_pl_gate", _row_dw("plg_dw", n4, dt))
    s1 = to_sibling("w_pl", "w_pl_gate")
    dn4 = _row_dx("plg_dx", dt, w["w_pl_gate"], comm=[s1])
    add_pairs(s1)
    tok = start_chips("pl", "w_pl", "w_pl_gate")
    dh3, dhb, dsp["pl_norm"] = _rms_bwd("pl_dnorm", dn4, h3, sp["pl_norm"], dh4, comm=[_After(tok)])

    xn, hg, hu, a = ffn2_saved
    grad("ffn2_w_down", _ffn_bwd_wd("ffn2_dwd", a, dhb))
    s2 = to_sibling("ffn2_w_down")
    dhg, dhu = _ffn_bwd_act("ffn2_dact", dhb, w["ffn2_w_down"], hg, hu, comm=[s2])
    add_pairs(s2)
    tok = start_chips("ffn2_down", "ffn2_w_down")
    dwg, dwu = _ffn_bwd_wup("ffn2_dwup", xn, dhg, dhu, comm=[_After(tok)])
    grad("ffn2_w_gate", dwg)
    grad("ffn2_w_up", dwu)
    s3 = to_sibling("ffn2_w_gate", "ffn2_w_up")
    dxn = _ffn_bwd_x("ffn2_dx", dhg, dhu, w["ffn2_w_gate"], w["ffn2_w_up"], comm=[s3])
    add_pairs(s3)
    tok = start_chips("ffn2_up", "ffn2_w_gate", "ffn2_w_up")
    dh2, dh2b, dsp["ffn2_norm"] = _rms_bwd("ffn2_dnorm", dxn, h2, sp["ffn2_norm"], dh3, comm=[_After(tok)])

    grad("w_out", _row_dw("out_dw", merged, dh2b))
    s4 = to_sibling("w_out")
    dya, dyb, dga, dgb = _mix_out_bwd(dh2b, w["w_out"], ga, gb, ya, yb, comm=[s4])
    add_pairs(s4)
    grad("w_branch_a", _grp_dw("bra_dw", oa, dya))
    grad("w_branch_b", _grp_dw("brb_dw", ob, dyb))
    doa = _grp_dx("bra_dx", dya, w["w_branch_a"], out_dtype=BF)
    s5 = to_sibling("w_branch_a", "w_branch_b")
    dob = _grp_dx("brb_dx", dyb, w["w_branch_b"], out_dtype=BF, comm=[s5])
    add_pairs(s5)
    tok = start_chips("mix", "w_out", "w_branch_a", "w_branch_b")

    dqf, dkf, dvf = _mla_bwd(qf, kf, vf, dob, comm=[_After(tok)])
    dqp, dkv, dkr = _mla_post(dqf, dkf, dvf, cos, sin, rot_t)
    grad("w_uq", _grp_dw_t("uq_dw", dqp, cq))
    grad("w_ukv", _grp_dw("ukv_dw", ckv, dkv))
    dcq = _grp_dx_t("uq_dx", dqp, w["w_uq"])
    s6 = to_sibling("w_uq", "w_ukv")
    dckv = _grp_dx("ukv_dx", dkv, w["w_ukv"], comm=[s6])
    add_pairs(s6)
    tok = start_chips("mla", "w_uq", "w_ukv")
    dlat, dsp["q_a_norm"], dsp["kv_a_norm"] = _mla_lat_bwd(dcq, dckv, dkr, lat, sp["q_a_norm"], sp["kv_a_norm"],
                                                         cos, sin, rot_t)
    dq_na, dk_na, dv_na, dtab = _na_bwd(qkv, tb, doa, comm=[_After(tok)])
    dsp["na_rpb"] = _na_rpb_grad(dtab, rows)
    dqkv = jnp.concatenate([dq_na, dk_na.astype(BF), dv_na.astype(BF)], axis=1)

    pieces = [dqkv, dlat, dga, dgb]
    dwin = jnp.zeros((sum(pc.shape[1] for pc in pieces), d), BF)
    row0 = 0
    for i, pc in enumerate(pieces):
        dwin = _mm_tn_into("in_dw%d" % i, pc, u, dwin, row0)
        row0 += pc.shape[1]
    grad("w_in", dwin.reshape(NDEV, -1, d))
    s7 = to_sibling("w_in")
    du = _in_proj_bwd_x(pieces, [wqkv, wlat, wga, wgb], comm=[s7])
    add_pairs(s7)
    tok = start_chips("w_in", "w_in")
    dh1, dhb, dsp["mix_norm"] = _rms_bwd("mix_dnorm", du, h1, sp["mix_norm"], dh2, comm=[_After(tok)])

    xn, hg, hu, a = ffn1_saved
    grad("ffn1_w_down", _ffn_bwd_wd("ffn1_dwd", a, dhb))
    s8 = to_sibling("ffn1_w_down")
    dhg, dhu = _ffn_bwd_act("ffn1_dact", dhb, w["ffn1_w_down"], hg, hu, comm=[s8])
    add_pairs(s8)
    tok = start_chips("ffn1_down", "ffn1_w_down")
    dwg, dwu = _ffn_bwd_wup("ffn1_dwup", xn, dhg, dhu, comm=[_After(tok)])
    grad("ffn1_w_gate", dwg)
    grad("ffn1_w_up", dwu)
    s9 = to_sibling("ffn1_w_gate", "ffn1_w_up")
    _comm_only("rs_sibling_ffn1", [s9])
    add_pairs(s9)
    tok = start_chips("ffn1_up", "ffn1_w_gate", "ffn1_w_up")
    dxn = _ffn_bwd_x("ffn1_dx", dhg, dhu, w["ffn1_w_gate"], w["ffn1_w_up"], comm=[_After(tok)])
    dx, _, dsp["ffn1_norm"] = _rms_bwd("ffn1_dnorm", dxn, x, sp["ffn1_norm"], dh1)
    return loss, dx, pending, dsp


def _gather_small(buf):
    def body(in_ref, out_ref, send_sems, recv_sems, local_sem):
        x, y, c = _coords()
        mine = pltpu.make_async_copy(in_ref, out_ref.at[4 * x + 2 * y + c], local_sem)
        mine.start()
        cps = []
        for k in range(1, NDEV):
            fx, fy, fc = (k >> 2) & 1, (k >> 1) & 1, k & 1
            peer = (x ^ fx, y ^ fy, c ^ fc)
            cps.append(pltpu.make_async_remote_copy(
                src_ref=in_ref, dst_ref=out_ref.at[4 * x + 2 * y + c], send_sem=send_sems.at[k - 1],
                recv_sem=recv_sems.at[k - 1], device_id=peer, device_id_type=MESH))
        for cp in cps:
            cp.start()
        for k in range(1, NDEV):
            fx, fy, fc = (k >> 2) & 1, (k >> 1) & 1, k & 1
            px, py, pc = x ^ fx, y ^ fy, c ^ fc
            pltpu.make_async_remote_copy(
                src_ref=in_ref, dst_ref=out_ref.at[4 * px + 2 * py + pc], send_sem=send_sems.at[k - 1],
                recv_sem=recv_sems.at[k - 1], device_id=(px, py, pc), device_id_type=MESH).wait_recv()
        for cp in cps:
            cp.wait_send()
        mine.wait()

    return pl.pallas_call(
        body, name="gather_small", in_specs=[ANY], out_specs=ANY,
        out_shape=jax.ShapeDtypeStruct((NDEV,) + buf.shape, buf.dtype),
        scratch_shapes=[pltpu.SemaphoreType.DMA((NDEV - 1,)), pltpu.SemaphoreType.DMA((NDEV - 1,)),
                        pltpu.SemaphoreType.DMA],
    )(buf)


def _adam_math(wv, g, m, v):
    m_new = B1 * m + (1.0 - B1) * g
    v_new = B2 * v + (1.0 - B2) * (g * g)
    m_hat = m_new / (1.0 - B1 ** STEP)
    v_hat = v_new / (1.0 - B2 ** STEP)
    return -LR * (m_hat / (jnp.sqrt(v_hat) + ADAM_EPS) + WD * wv), m_new, v_new


def _adam(name, parts, wv, m, v, after=None):
    npart, r, c = parts.shape
    tr, tc = _ew_tile(r, c)

    def body(p_ref, w_ref, m_ref, v_ref, *rest):
        g_ref, d_ref, mo_ref, vo_ref = rest[-4:]
        g = p_ref[0].astype(F32)
        for j in range(1, npart):
            g = g + p_ref[j].astype(F32)
        g_ref[...] = g
        d_ref[...], mo_ref[...], vo_ref[...] = _adam_math(w_ref[...], g, m_ref[...], v_ref[...])

    blk = pl.BlockSpec((tr, tc), lambda i, k: (i, k))
    extra = [] if after is None else [after]
    return pl.pallas_call(
        body, name=name, grid=(r // tr, c // tc),
        in_specs=[pl.BlockSpec((npart, tr, tc), lambda i, k: (0, i, k)), blk, blk, blk] + [ANY] * len(extra),
        out_specs=[blk] * 4, out_shape=[jax.ShapeDtypeStruct((r, c), F32)] * 4, compiler_params=_params(2),
    )(parts, wv, m, v, *extra)


def _adam_exchanged(name, sums, land, wv, m, v, my_chip):
    _, r, c = sums.shape
    tr, tc = _ew_tile(r, c)

    def body(chip_ref, s_ref, l_ref, w_ref, m_ref, v_ref, g_ref, d_ref, mo_ref, vo_ref):
        g = s_ref[...].astype(F32)
        for j in range(3):
            g = g + l_ref[j].astype(F32)
        g_ref[...] = g
        d_ref[...], mo_ref[...], vo_ref[...] = _adam_math(w_ref[...], g, m_ref[...], v_ref[...])

    blk = pl.BlockSpec((tr, tc), lambda i, k, chip_ref: (i, k))
    return pl.pallas_call(
        body, name=name,
        grid_spec=pltpu.PrefetchScalarGridSpec(
            num_scalar_prefetch=1, grid=(r // tr, c // tc),
            in_specs=[pl.BlockSpec((None, tr, tc), lambda i, k, chip_ref: (chip_ref[0], i, k)),
                      pl.BlockSpec((3, tr, tc), lambda i, k, chip_ref: (0, i, k)), blk, blk, blk],
            out_specs=[blk] * 4),
        out_shape=[jax.ShapeDtypeStruct((r, c), F32)] * 4, compiler_params=_params(2),
    )(my_chip, sums, land, wv, m, v)


SHARDED = ("ffn1_w_gate", "ffn1_w_up", "ffn1_w_down", "w_in", "w_uq", "w_ukv", "w_branch_a", "w_branch_b", "w_out",
           "ffn2_w_gate", "ffn2_w_up", "ffn2_w_down", "w_pl", "w_pl_gate")
TRANSPOSED = ("ffn1_w_gate", "ffn1_w_up", "ffn2_w_gate", "ffn2_w_up", "w_in", "w_uq")
REPLICATED = ("ffn1_norm", "mix_norm", "q_a_norm", "kv_a_norm", "na_rpb", "ffn2_norm", "pl_norm", "final_norm")
WEIGHTS = ("ffn1_norm", "ffn1_w_gate", "ffn1_w_up", "ffn1_w_down", "mix_norm", "w_in", "q_a_norm", "w_uq",
           "kv_a_norm", "w_ukv", "na_rpb", "w_branch_a", "w_branch_b", "w_out", "ffn2_norm", "ffn2_w_gate",
           "ffn2_w_up", "ffn2_w_down", "pl_norm", "w_pl", "w_pl_gate", "final_norm")
SMALL_W = 2048


def _pack_small(vals):
    rows = []
    for name in REPLICATED:
        flat = vals[name].reshape(-1).astype(F32)
        n = -(-flat.shape[0] // SMALL_W) * SMALL_W
        rows.append(jnp.pad(flat, (0, n - flat.shape[0])).reshape(-1, SMALL_W))
    return jnp.concatenate(rows, axis=0)


def _unpack_small(buf, shapes):
    out, r = {}, 0
    for name in REPLICATED:
        size = int(np.prod(shapes[name]))
        nrow = -(-size // SMALL_W)
        out[name] = buf[r:r + nrow].reshape(-1)[:size].reshape(shapes[name])
        r += nrow
    return out


def kernel(x, p, ffn1_norm, ffn1_w_gate, ffn1_w_up, ffn1_w_down, mix_norm, w_in, q_a_norm, w_uq, kv_a_norm, w_ukv, na_rpb, w_branch_a, w_branch_b, w_out, ffn2_norm, ffn2_w_gate, ffn2_w_up, ffn2_w_down, pl_norm, w_pl, w_pl_gate, final_norm, loss_target, m_ffn1_norm, m_ffn1_w_gate, m_ffn1_w_up, m_ffn1_w_down, m_mix_norm, m_w_in, m_q_a_norm, m_w_uq, m_kv_a_norm, m_w_ukv, m_na_rpb, m_w_branch_a, m_w_branch_b, m_w_out, m_ffn2_norm, m_ffn2_w_gate, m_ffn2_w_up, m_ffn2_w_down, m_pl_norm, m_w_pl, m_w_pl_gate, m_final_norm, v_ffn1_norm, v_ffn1_w_gate, v_ffn1_w_up, v_ffn1_w_down, v_mix_norm, v_w_in, v_q_a_norm, v_w_uq, v_kv_a_norm, v_w_ukv, v_na_rpb, v_w_branch_a, v_w_branch_b, v_w_out, v_ffn2_norm, v_ffn2_w_gate, v_ffn2_w_up, v_ffn2_w_down, v_pl_norm, v_w_pl, v_w_pl_gate, v_final_norm):
    args = dict(locals())
    wts = {n: args[n] for n in WEIGHTS}
    mom = {n: args["m_" + n] for n in WEIGHTS}
    var = {n: args["v_" + n] for n in WEIGHTS}
    shapes = {n: wts[n].shape for n in WEIGHTS}
    core = lax.axis_index("c").astype(jnp.int32).reshape(1)

    local = lambda n, a: a[0].T if n in TRANSPOSED else a[0]
    own = {n: local(n, wts[n]).astype(BF) for n in SHARDED}
    sp = {n: wts[n].reshape(1, -1) for n in REPLICATED if n != "na_rpb"}
    sp["na_rpb"] = wts["na_rpb"][0]
    loss_part, grad_x, pending, dsp = _device_step(x[0], p[0, 0], loss_target[0], sp, own, core)

    out = {}
    last = grad_x
    my_chip = (2 * lax.axis_index("x") + lax.axis_index("y")).astype(jnp.int32).reshape(1)
    for tag, names, send, recv, thru, lands in pending:
        thru, lands = _chips_wait("rs_wait_" + tag, send, recv, thru, lands, last)
        for n, s4, l3 in zip(names, thru, lands):
            res4 = _adam_exchanged("adam_" + n, s4, l3, local(n, wts[n]), local(n, mom[n]), local(n, var[n]), my_chip)
            out[n] = tuple((a.T if n in TRANSPOSED else a)[None] for a in res4)
            last = res4[1]

    small = jnp.concatenate([_pack_small(dsp), jnp.pad(loss_part, ((0, 0), (0, SMALL_W - loss_part.shape[1])))], 0)
    pad_rows = -small.shape[0] % 8
    small = jnp.pad(small, ((0, pad_rows), (0, 0)))
    every = _gather_small(small)
    zeros = jnp.zeros((1 + pad_rows, SMALL_W), F32)
    pack = lambda d: jnp.concatenate([_pack_small(d), zeros], 0)
    g_s, d_s, m_s, v_s = _adam("adam_small", every, pack(wts), pack(mom), pack(var))
    n_rows = small.shape[0] - 1 - pad_rows
    loss = g_s[n_rows, 0]
    small_out = [_unpack_small(b, shapes) for b in (g_s, d_s, m_s, v_s)]
    for n in REPLICATED:
        out[n] = tuple(b[n] for b in small_out)

    res = [loss, grad_x[None]]
    for k in range(4):
        res += [out[n][k] for n in WEIGHTS]
    return tuple(res)
```

```python
import functools

import numpy as np
import jax
import jax.numpy as jnp
from jax import lax
from jax.experimental import pallas as pl
from jax.experimental.pallas import tpu as pltpu

F32 = jnp.float32
BF = jnp.bfloat16
MESH = pl.DeviceIdType.MESH

NDEV = 8
NCHIP = 4
VMEM_LIMIT = 56 * 1024 * 1024
EPS = 1e-6
NEG = -1e30
GRID_W = 64
NA_HEADS, NA_DIM = 8, 128
NA_ROWS_WIN, NA_COLS_WIN = 8, 16
NA_HG = 4
NA_QROWS = 4
ML_HEADS, ML_NOPE, ML_ROPE, ML_V = 8, 128, 64, 128
ML_QK = ML_NOPE + ML_ROPE
ML_RANK = 512
ROPE_THETA = 10000.0
LR, B1, B2, ADAM_EPS, WD, STEP = 0.001, 0.9, 0.999, 1e-08, 0.01, 10
HI = lax.Precision.HIGHEST

_DN = {"nn": (((1,), (0,)), ((), ())), "nt": (((1,), (1,)), ((), ())), "tn": (((0,), (0,)), ((), ()))}


def _params(n):
    return pltpu.CompilerParams(dimension_semantics=("arbitrary",) * n, vmem_limit_bytes=VMEM_LIMIT)


def _sig(v):
    return jax.nn.sigmoid(v)


ANY = pl.BlockSpec(memory_space=pl.ANY)


def _coords():
    return lax.axis_index("x"), lax.axis_index("y"), lax.axis_index("c")


class _Part:
    inputs, out_shapes, sem_shapes, results = (), (), (), None

    def mid(self, ins, outs, sems):
        pass

    def late(self, ins, outs, sems):
        pass


class _After(_Part):
    def __init__(self, token):
        self.inputs = [token]

    def start(self, ins, outs, sems):
        pass

    finish = start


class _GatherPart(_Part):
    def __init__(self, names, shards):
        n = len(shards)
        self.names, self.inputs = list(names), list(shards)
        self.out_shapes = [jax.ShapeDtypeStruct((NDEV,) + a.shape, a.dtype) for a in shards]
        self.sem_shapes = [pltpu.SemaphoreType.DMA((n, 7)), pltpu.SemaphoreType.DMA((n, 7)),
                           pltpu.SemaphoreType.DMA((n,))]

    def _plan(self, ins, outs, sems):
        send_sems, recv_sems, local_sems = sems
        x, y, c = _coords()
        me, sib, diag = (x, y, c), (x, y, 1 - c), (1 - x, 1 - y, c)
        n1, n2 = (x ^ (1 - c), y ^ c, c), (x ^ c, y ^ (1 - c), c)

        def copy(i, k, block, to, src=None):
            px, py, pc = block
            dst = outs[i].at[4 * px + 2 * py + pc]
            return pltpu.make_async_remote_copy(
                src_ref=dst if src is None else src, dst_ref=dst, send_sem=send_sems.at[i, k],
                recv_sem=recv_sems.at[i, k], device_id=to, device_id_type=MESH)

        mine = [pltpu.make_async_copy(ins[i], outs[i].at[4 * x + 2 * y + c], local_sems.at[i])
                for i in range(len(ins))]
        return copy, mine, me, sib, n1, n2, diag

    def _own_sends(self, ins, copy, me, sib, n1, n2):
        return [copy(i, k, me, to, src=ins[i]) for i in range(len(ins)) for k, to in enumerate((sib, n1, n2))]

    def start(self, ins, outs, sems):
        copy, mine, me, sib, n1, n2, _ = self._plan(ins, outs, sems)
        for cp in mine + self._own_sends(ins, copy, me, sib, n1, n2):
            cp.start()

    def mid(self, ins, outs, sems):
        copy, _, me, sib, n1, n2, _ = self._plan(ins, outs, sems)
        for i in range(len(ins)):
            copy(i, 1, n1, me).wait_recv()
            copy(i, 3, n1, n2).start()
            copy(i, 4, n1, sib).start()

    def late(self, ins, outs, sems):
        copy, _, me, sib, _, n2, diag = self._plan(ins, outs, sems)
        for i in range(len(ins)):
            copy(i, 2, n2, me).wait_recv()
            copy(i, 5, n2, sib).start()
        for i in range(len(ins)):
            copy(i, 3, diag, me).wait_recv()
            copy(i, 6, diag, sib).start()

    def finish(self, ins, outs, sems):
        copy, mine, me, sib, n1, n2, diag = self._plan(ins, outs, sems)
        other = lambda dev: (dev[0], dev[1], sib[2])
        n = len(ins)
        for i in range(n):
            copy(i, 0, sib, me).wait_recv()
            for k, block in ((4, other(n2)), (5, other(n1)), (6, other(diag))):
                copy(i, k, block, me).wait_recv()
        for cp in self._own_sends(ins, copy, me, sib, n1, n2):
            cp.wait_send()
        for i in range(n):
            for k, block in ((3, n1), (4, n1), (5, n2), (6, diag)):
                copy(i, k, block, sib).wait_send()
        for cp in mine:
            cp.wait()


class _SiblingPart(_Part):
    def __init__(self, names, parts):
        n = len(parts)
        self.names, self.inputs = list(names), list(parts)
        self.out_shapes = [jax.ShapeDtypeStruct((NCHIP,) + a.shape[2:], a.dtype) for a in parts]
        self.sem_shapes = [pltpu.SemaphoreType.DMA((n,)), pltpu.SemaphoreType.DMA((n,))]

    def _copies(self, ins, outs, sems):
        x, y, c = _coords()
        return [pltpu.make_async_remote_copy(
            src_ref=ins[i].at[:, 1 - c], dst_ref=outs[i], send_sem=sems[0].at[i], recv_sem=sems[1].at[i],
            device_id=(x, y, 1 - c), device_id_type=MESH) for i in range(len(ins))]

    def start(self, ins, outs, sems):
        for cp in self._copies(ins, outs, sems):
            cp.start()

    def finish(self, ins, outs, sems):
        cps = self._copies(ins, outs, sems)
        for cp in cps:
            cp.wait_recv()
        for cp in cps:
            cp.wait_send()


HBM = pl.BlockSpec(memory_space=pltpu.HBM)
SEM = pl.BlockSpec(memory_space=pltpu.SEMAPHORE)


def _chip_peers():
    x, y, c = _coords()
    return [(1 - x, y, c), (x, 1 - y, c), (1 - x, 1 - y, c)]


def _chips_start(name, sums):
    n = len(sums)

    def body(*refs):
        ins, lands, send_sems, recv_sems = refs[:n], refs[n:2 * n], refs[2 * n], refs[2 * n + 1]
        for i in range(n):
            for k, (px, py, pc) in enumerate(_chip_peers()):
                pltpu.make_async_remote_copy(
                    src_ref=ins[i].at[2 * px + py], dst_ref=lands[i].at[k], send_sem=send_sems.at[3 * i + k],
                    recv_sem=recv_sems.at[3 * i + k], device_id=(px, py, pc), device_id_type=MESH).start()
        refs[-1][...] = jnp.zeros_like(refs[-1])

    lands = [lax.empty((3,) + a.shape[1:], a.dtype) for a in sums]
    bufs = list(sums) + lands
    res = pl.pallas_call(
        body, name=name, in_specs=[HBM] * (2 * n),
        out_specs=(SEM, SEM, *[HBM] * (2 * n), pl.BlockSpec(memory_space=pltpu.VMEM)),
        out_shape=(pltpu.SemaphoreType.DMA((3 * n,)), pltpu.SemaphoreType.DMA((3 * n,)),
                   *[pltpu.HBM(a.shape, a.dtype) for a in bufs], jax.ShapeDtypeStruct((8, 128), F32)),
        input_output_aliases={i: 2 + i for i in range(2 * n)},
        compiler_params=pltpu.CompilerParams(has_side_effects=pltpu.SideEffectType.DATAFLOW_SIDE_EFFECTING),
    )(*[pltpu.with_memory_space_constraint(a, pltpu.HBM) for a in bufs])
    return res[0], res[1], list(res[2:2 + n]), list(res[2 + n:2 + 2 * n]), res[-1]


def _chips_wait(name, send_sems, recv_sems, sums, lands, after):
    n = len(sums)

    def body(*refs):
        ins, zones, send, recv = refs[:n], refs[n:2 * n], refs[2 * n], refs[2 * n + 1]
        for i in range(n):
            for k, peer in enumerate(_chip_peers()):
                cp = pltpu.make_async_remote_copy(
                    src_ref=ins[i].at[0], dst_ref=zones[i].at[k], send_sem=send.at[3 * i + k],
                    recv_sem=recv.at[3 * i + k],
                    device_id=peer, device_id_type=MESH)
                cp.wait_send()
                cp.wait_recv()

    bufs = list(sums) + list(lands)
    res = pl.pallas_call(
        body, name=name, in_specs=[HBM] * (2 * n) + [SEM, SEM, ANY], out_specs=[HBM] * (2 * n),
        out_shape=[pltpu.HBM(a.shape, a.dtype) for a in bufs], input_output_aliases={i: i for i in range(2 * n)},
        compiler_params=pltpu.CompilerParams(has_side_effects=pltpu.SideEffectType.DATAFLOW_SIDE_EFFECTING),
    )(*bufs, send_sems, recv_sems, after)
    return list(res[:n]), list(res[n:])


def _call(name, body, grid, in_specs, out_specs, out_shape, args, comm=(), scratch=()):
    comm = [p for p in comm if p is not None]
    single = not isinstance(out_shape, (list, tuple))
    o_specs = [out_specs] if single else list(out_specs)
    o_shape = [out_shape] if single else list(out_shape)
    n_in, n_out = len(in_specs), len(o_specs)
    c_in = [a for p in comm for a in p.inputs]
    c_out = [s for p in comm for s in p.out_shapes]
    c_sem = [s for p in comm for s in p.sem_shapes]

    def wrapped(*refs):
        ins, outs = refs[:n_in], refs[n_in + len(c_in):n_in + len(c_in) + n_out]
        pos = [n_in, n_in + len(c_in) + n_out, n_in + len(c_in) + n_out + len(c_out)]
        own = refs[pos[2]:pos[2] + len(scratch)]
        pos[2] += len(scratch)
        split = []
        for p in comm:
            sizes = [len(p.inputs), len(p.out_shapes), len(p.sem_shapes)]
            split.append([refs[o:o + n] for o, n in zip(pos, sizes)])
            pos = [o + n for o, n in zip(pos, sizes)]
        step, steps = 0, 1
        for a, g in enumerate(grid):
            step, steps = step * g + pl.program_id(a), steps * g

        def run(which, at):
            def go():
                for p, cut in zip(comm, split):
                    getattr(p, which)(*cut)
            if not comm:
                return
            if grid:
                pl.when(step == at)(go)
            else:
                go()

        run("start", 0)
        body(*ins, *outs, *own)
        run("mid", steps // 2)
        run("late", max(steps // 2, steps - 1 - max(1, steps // 8)))
        run("finish", steps - 1)

    res = pl.pallas_call(
        wrapped, name=name, grid=grid, in_specs=list(in_specs) + [ANY] * len(c_in),
        out_specs=o_specs + [ANY] * len(c_out), out_shape=o_shape + c_out, scratch_shapes=list(scratch) + c_sem,
        compiler_params=_params(len(grid)),
    )(*args, *c_in)
    pos = n_out
    for p in comm:
        p.results = list(res[pos:pos + len(p.out_shapes)])
        pos += len(p.out_shapes)
    return res[0] if single else list(res[:n_out])


def _comm_only(name, comm):
    def body(o_ref):
        o_ref[...] = jnp.zeros_like(o_ref)

    _call(name, body, (), [], pl.BlockSpec(memory_space=pltpu.VMEM), jax.ShapeDtypeStruct((8, 128), F32), [], comm)


def _mm(name, grid, prods, extras, outs, epi, nacc=1, comm=()):
    n_p, n_e = len(prods), len(extras)

    def body(*refs):
        ab, ex, out = refs[:2 * n_p], refs[2 * n_p:2 * n_p + n_e], refs[2 * n_p + n_e:]
        accs = [None] * nacc
        for i, prod in enumerate(prods):
            dn, acc, loop = prod[6], prod[7], prod[8]
            a_ref, b_ref = ab[2 * i], ab[2 * i + 1]
            if loop:
                for g in range(loop):
                    t = lax.dot_general(a_ref[g], b_ref[g], _DN[dn], preferred_element_type=F32)
                    accs[acc] = t if accs[acc] is None else accs[acc] + t
            else:
                t = lax.dot_general(a_ref[...], b_ref[...], _DN[dn], preferred_element_type=F32)
                accs[acc] = t if accs[acc] is None else accs[acc] + t
        epi(accs, ex, out)

    in_specs, args = [], []
    for prod in prods:
        in_specs += [pl.BlockSpec(prod[1], prod[2]), pl.BlockSpec(prod[4], prod[5])]
        args += [prod[0], prod[3]]
    for e, e_blk, e_map in extras:
        in_specs.append(pl.BlockSpec(e_blk, e_map))
        args.append(e)
    return _call(name, body, grid, in_specs, [pl.BlockSpec(blk, mp) for _, _, blk, mp in outs],
                 [jax.ShapeDtypeStruct(s, d) for s, d, _, _ in outs], args, comm)


def _store(accs, ex, out):
    out[0][...] = accs[0].astype(out[0].dtype)


def _ew_tile(r, c, budget=3 << 19):
    for t in range(r - r % 16, 0, -16):
        if r % t == 0 and t * c * 4 <= budget:
            return t, c
    for t in range(c - c % 128, 0, -128):
        if c % t == 0 and r * t * 4 <= budget:
            return r, t
    return r, c


def _tile(n, want):
    t = min(n, want)
    assert n % t == 0, (n, want)
    return t


def _mm_nt(name, a, bt, out_dtype, tm=512, tn=512, comm=(), rows=None):
    m, k = a.shape
    n = rows or bt.shape[0]
    tm, tn = _tile(m, tm), (tn if n % tn == 0 else n)
    return _mm(name, (n // tn, m // tm),
               [(a, (tm, k), lambda j, i: (i, 0), bt, (tn, k), lambda j, i: (j, 0), "nt", 0, 0)], [],
               [((m, n), out_dtype, (tm, tn), lambda j, i: (i, j))], _store, comm=comm)[0]


def _mm_tn_into(name, a, b, buf, row0, ta=1024, tb=512):
    t, ka = a.shape
    nb = b.shape[1]
    ta, tb = (ta if ka % ta == 0 else ka), (tb if nb % tb == 0 else nb)

    def body(a_ref, b_ref, buf_in, buf_out, tile, sem):
        i, j = pl.program_id(0), pl.program_id(1)
        tile[...] = lax.dot_general(a_ref[...], b_ref[...], _DN["tn"], preferred_element_type=F32).astype(tile.dtype)
        rows = pl.ds(pl.multiple_of(row0 + i * ta, 16), ta)
        cp = pltpu.make_async_copy(tile, buf_out.at[rows, pl.ds(pl.multiple_of(j * tb, 128), tb)], sem)
        cp.start()
        cp.wait()

    return pl.pallas_call(
        body, name=name, grid=(ka // ta, nb // tb),
        in_specs=[pl.BlockSpec((t, ta), lambda i, j: (0, i)), pl.BlockSpec((t, tb), lambda i, j: (0, j)), ANY],
        out_specs=ANY, out_shape=jax.ShapeDtypeStruct(buf.shape, buf.dtype), input_output_aliases={2: 0},
        scratch_shapes=[pltpu.VMEM((ta, tb), buf.dtype), pltpu.SemaphoreType.DMA],
        compiler_params=_params(2))(a, b, buf)


def _rms_fwd(name, x, g, tm=256, comm=()):
    s, d = x.shape
    tm = _tile(s, tm)

    def body(x_ref, g_ref, o_ref):
        v = x_ref[...]
        o_ref[...] = (v * lax.rsqrt(jnp.mean(v * v, axis=-1, keepdims=True) + EPS) * g_ref[...]).astype(o_ref.dtype)

    return _call(name, body, (s // tm,),
                 [pl.BlockSpec((tm, d), lambda i: (i, 0)), pl.BlockSpec((1, d), lambda i: (0, 0))],
                 pl.BlockSpec((tm, d), lambda i: (i, 0)), jax.ShapeDtypeStruct((s, d), BF), [x, g], comm)


def _acc_rows(ref, part, i):
    @pl.when(i == 0)
    def _():
        ref[...] = part

    @pl.when(i > 0)
    def _():
        ref[...] += part


def _rms_bwd_math(dn, v, g):
    rstd = lax.rsqrt(jnp.mean(v * v, axis=-1, keepdims=True) + EPS)
    xh = v * rstd
    dxh = dn * g
    dx = rstd * (dxh - xh * jnp.mean(dxh * xh, axis=-1, keepdims=True))
    return dx, jnp.sum(dn * xh, axis=0, keepdims=True)


def _rms_bwd(name, dn, x, g, resid, tm=256, comm=()):
    s, d = x.shape
    tm = _tile(s, tm)

    def body(dn_ref, x_ref, g_ref, r_ref, dx_ref, dxb_ref, dg_ref):
        dx, part = _rms_bwd_math(dn_ref[...].astype(F32), x_ref[...], g_ref[...])
        tot = r_ref[...] + dx
        dx_ref[...] = tot
        dxb_ref[...] = tot.astype(BF)
        _acc_rows(dg_ref, part, pl.program_id(0))

    row = pl.BlockSpec((tm, d), lambda i: (i, 0))
    one = pl.BlockSpec((1, d), lambda i: (0, 0))
    return _call(name, body, (s // tm,), [row, row, one, row], [row, row, one],
                 [jax.ShapeDtypeStruct((s, d), F32), jax.ShapeDtypeStruct((s, d), BF),
                  jax.ShapeDtypeStruct((1, d), F32)], [dn, x, g, resid], comm)


def _loss_head(h, target, g, tm=256):
    s, d = h.shape
    tm = _tile(s, tm)

    def body(h_ref, t_ref, g_ref, dh_ref, dg_ref, loss_ref):
        v, gv = h_ref[...], g_ref[...]
        rstd = lax.rsqrt(jnp.mean(v * v, axis=-1, keepdims=True) + EPS)
        xh = v * rstd
        err = xh * gv - t_ref[...]
        part_loss = 0.5 * jnp.sum(jnp.mean(err * err, axis=-1, keepdims=True), axis=0, keepdims=True)
        dy = err * (1.0 / d)
        dxh = dy * gv
        dh_ref[...] = rstd * (dxh - xh * jnp.mean(dxh * xh, axis=-1, keepdims=True))
        i = pl.program_id(0)
        _acc_rows(dg_ref, jnp.sum(dy * xh, axis=0, keepdims=True), i)
        _acc_rows(loss_ref, jnp.broadcast_to(part_loss, loss_ref.shape), i)

    row = pl.BlockSpec((tm, d), lambda i: (i, 0))
    one = pl.BlockSpec((1, d), lambda i: (0, 0))
    return pl.pallas_call(
        body, name="loss_head", grid=(s // tm,), in_specs=[row, row, one],
        out_specs=[row, one, pl.BlockSpec((1, 128), lambda i: (0, 0))],
        out_shape=[jax.ShapeDtypeStruct((s, d), F32), jax.ShapeDtypeStruct((1, d), F32),
                   jax.ShapeDtypeStruct((1, 128), F32)],
        compiler_params=_params(1))(h, target, g)


def _pl_bwd_elem(dh, pe, t, tm=256):
    s, d = dh.shape
    tm = _tile(s, tm)

    def body(dh_ref, pe_ref, t_ref, dt_ref, dpe_ref):
        dh_v, sg = dh_ref[...], _sig(t_ref[...])
        dt_ref[...] = (dh_v * pe_ref[...].astype(F32) * sg * (1.0 - sg)).astype(BF)
        dpe_ref[...] = (dh_v * sg).astype(BF)

    row = pl.BlockSpec((tm, d), lambda i: (i, 0))
    return pl.pallas_call(
        body, name="pl_bwd_elem", grid=(s // tm,), in_specs=[row, row, row], out_specs=[row, row],
        out_shape=[jax.ShapeDtypeStruct((s, d), BF)] * 2, compiler_params=_params(1))(dh, pe, t)


def _ffn_up(name, xn, wg, wu, tm=1024, comm=()):
    s, d = xn.shape
    g, fb, _ = wg.shape
    tm = _tile(s, tm)

    def epi(accs, ex, out):
        hg, hu = accs
        out[0][...] = hg.astype(BF)
        out[1][...] = hu.astype(BF)
        out[2][...] = (hg * _sig(hg) * hu).astype(BF)

    a_map = lambda j, i: (i, 0)
    w_map = lambda j, i: (j, 0, 0)
    o = ((g, s, fb), BF, (None, tm, fb), lambda j, i: (j, i, 0))
    return _mm(name, (g, s // tm),
               [(xn, (tm, d), a_map, wg, (None, fb, d), w_map, "nt", 0, 0),
                (xn, (tm, d), a_map, wu, (None, fb, d), w_map, "nt", 1, 0)], [], [o, o, o], epi, nacc=2, comm=comm)


def _ffn_down(name, a, wd, resid, tm=1024, tn=512, comm=()):
    g, s, fb = a.shape
    d = wd.shape[2]
    tm, tn = _tile(s, tm), _tile(d, tn)

    def epi(accs, ex, out):
        out[0][...] = ex[0][...] + 0.5 * accs[0]

    return _mm(name, (d // tn, s // tm),
               [(a, (g, tm, fb), lambda j, i: (0, i, 0), wd, (g, fb, tn), lambda j, i: (0, 0, j), "nn", 0, g)],
               [(resid, (tm, tn), lambda j, i: (i, j))],
               [((s, d), F32, (tm, tn), lambda j, i: (i, j))], epi, comm=comm)[0]


def _ffn_bwd_act(name, dh, wd, hg, hu, tm=1024, comm=()):
    s, d = dh.shape
    g, fb, _ = wd.shape
    tm = _tile(s, tm)

    def epi(accs, ex, out):
        da = 0.5 * accs[0]
        hg_v, hu_v = ex[0][...].astype(F32), ex[1][...].astype(F32)
        sg = _sig(hg_v)
        out[0][...] = (da * hu_v * (sg * (1.0 + hg_v * (1.0 - sg)))).astype(BF)
        out[1][...] = (da * (hg_v * sg)).astype(BF)

    blk = (None, tm, fb)
    gmap = lambda j, i: (j, i, 0)
    return _mm(name, (g, s // tm),
               [(dh, (tm, d), lambda j, i: (i, 0), wd, (None, fb, d), lambda j, i: (j, 0, 0), "nt", 0, 0)],
               [(hg, blk, gmap), (hu, blk, gmap)],
               [((g, s, fb), BF, blk, gmap), ((g, s, fb), BF, blk, gmap)], epi, comm=comm)


def _ffn_bwd_wd(name, a, dh, tn=1024, comm=()):
    g, s, fb = a.shape
    d = dh.shape[1]
    tn = _tile(d, tn)

    def epi(accs, ex, out):
        out[0][...] = (0.5 * accs[0]).astype(BF)

    return _mm(name, (g, d // tn),
               [(a, (None, s, fb), lambda j, i: (j, 0, 0), dh, (s, tn), lambda j, i: (0, i), "tn", 0, 0)], [],
               [((g, fb, d), BF, (None, fb, tn), lambda j, i: (j, 0, i))], epi, comm=comm)[0]


def _ffn_bwd_wup(name, xn, dhg, dhu, tk=1024, comm=()):
    s, d = xn.shape
    g, _, fb = dhg.shape
    tk = _tile(d, tk)

    def epi(accs, ex, out):
        out[0][...] = accs[0].astype(BF)
        out[1][...] = accs[1].astype(BF)

    a_map = lambda j, i: (j, 0, 0)
    b_map = lambda j, i: (0, i)
    o = ((g, fb, d), BF, (None, fb, tk), lambda j, i: (j, 0, i))
    return _mm(name, (g, d // tk),
               [(dhg, (None, s, fb), a_map, xn, (s, tk), b_map, "tn", 0, 0),
                (dhu, (None, s, fb), a_map, xn, (s, tk), b_map, "tn", 1, 0)], [], [o, o], epi, nacc=2, comm=comm)


def _ffn_bwd_x(name, dhg, dhu, wg, wu, tm=512, tn=512, comm=()):
    g, s, fb = dhg.shape
    d = wg.shape[2]
    tm, tn = _tile(s, tm), _tile(d, tn)
    a_blk, a_map = (g, tm, fb), lambda j, i: (0, i, 0)
    b_blk, b_map = (g, fb, tn), lambda j, i: (0, 0, j)
    return _mm(name, (d // tn, s // tm),
               [(dhg, a_blk, a_map, wg, b_blk, b_map, "nn", 0, g), (dhu, a_blk, a_map, wu, b_blk, b_map, "nn", 0, g)],
               [], [((s, d), F32, (tm, tn), lambda j, i: (i, j))], _store, comm=comm)[0]


def _ffn_forward(tag, h, gain, get_wgu, get_wd, norm_comm=(), up_comm=(), down_comm=()):
    xn = _rms_fwd(tag + "_norm", h, gain, comm=norm_comm)
    wg, wu = get_wgu()
    hg, hu, a = _ffn_up(tag + "_up", xn, wg, wu, comm=up_comm)
    return _ffn_down(tag + "_down", a, get_wd(), h, comm=down_comm), (xn, hg, hu, a)


def _na_geometry(rows):
    kh = min(NA_ROWS_WIN, rows)
    cols = np.arange(GRID_W)
    col_start = np.clip(cols - NA_COLS_WIN // 2, 0, GRID_W - NA_COLS_WIN)
    mask = (cols[None, :] >= col_start[:, None]) & (cols[None, :] < col_start[:, None] + NA_COLS_WIN)
    dc = np.clip(cols[None, :] - cols[:, None], -(NA_COLS_WIN - 1), NA_COLS_WIN - 1) + (NA_COLS_WIN - 1)
    return kh, mask, dc


def _na_table(rpb, rows):
    _, mask, dc = _na_geometry(rows)
    nd, nc, cells = 2 * NA_ROWS_WIN - 1, 2 * NA_COLS_WIN - 1, GRID_W * GRID_W
    onehot = np.zeros((128, cells), np.float32)
    onehot[dc.reshape(-1), np.arange(cells)] = mask.reshape(-1).astype(np.float32)
    off = np.where(mask.reshape(1, -1), 0.0, NEG).astype(np.float32)

    def body(r_ref, e_ref, off_ref, o_ref):
        o_ref[...] = jnp.dot(r_ref[...], e_ref[...], precision=HI, preferred_element_type=F32) + off_ref[...]

    flat = pl.pallas_call(body, name="na_table", out_shape=jax.ShapeDtypeStruct((NA_HEADS * nd, cells), F32),
                          compiler_params=_params(0))(
        jnp.pad(rpb.reshape(NA_HEADS * nd, nc), ((0, 0), (0, 128 - nc))), jnp.asarray(onehot), jnp.asarray(off))
    return flat.reshape(NA_HEADS, nd, GRID_W, GRID_W)


class _NaPlan:
    def __init__(self, s):
        self.s, self.rows = s, s // GRID_W
        self.kh = min(NA_ROWS_WIN, self.rows)
        self.qr = min(NA_QROWS, self.rows)
        self.kr = min(self.rows, self.kh + self.qr - 1)
        self.groups = self.rows // self.qr
        self.nd = 2 * NA_ROWS_WIN - 1
        self.hw, self.nq = NA_HG * NA_DIM, NA_HEADS // NA_HG
        clip = lambda v, hi: min(max(v, 0), hi)
        pats = [(clip(g * self.qr - self.kh // 2, self.rows - self.kr) - g * self.qr,)
                + tuple(clip(g * self.qr + a - self.kh // 2, self.rows - self.kh) - g * self.qr for a in range(self.qr))
                for g in range(self.groups)]
        self.rebuild = [g for g in range(self.groups) if g == 0 or pats[g] != pats[g - 1]]

    def first_key_row(self, g):
        return jnp.clip(g * self.qr - self.kh // 2, 0, self.rows - self.kr)

    def specs(self):
        blk = pl.BlockSpec((self.qr * GRID_W, self.hw), lambda j, g: (g, j))
        k_spec = pl.BlockSpec((self.s, self.hw), lambda j, g: (0, self.nq + j))
        v_spec = pl.BlockSpec((self.s, self.hw), lambda j, g: (0, 2 * self.nq + j))
        t_spec = pl.BlockSpec((NA_HG, self.nd, GRID_W, GRID_W), lambda j, g: (j, 0, 0, 0))
        return blk, k_spec, v_spec, t_spec

    def bias_scratch(self):
        return pltpu.VMEM((NA_HG, self.qr * GRID_W, self.kr * GRID_W), F32)

    def fill_bias(self, t_ref, bias_ref, g):
        def build():
            r0, ks = g * self.qr, self.first_key_row(g)
            for a in range(self.qr):
                rs = jnp.clip(r0 + a - self.kh // 2, 0, self.rows - self.kh)
                for i in range(self.kr):
                    valid = jnp.logical_and(ks + i >= rs, ks + i < rs + self.kh)
                    idx = jnp.clip(ks + i - r0 - a + NA_ROWS_WIN - 1, 0, self.nd - 1)
                    for h in range(NA_HG):
                        bias_ref[h, a * GRID_W:(a + 1) * GRID_W, i * GRID_W:(i + 1) * GRID_W] = jnp.where(
                            valid, t_ref[h, idx], NEG)

        pl.when(functools.reduce(jnp.logical_or, [g == r for r in self.rebuild]))(build)

    def window(self, g):
        return pl.ds(pl.multiple_of(self.first_key_row(g) * GRID_W, GRID_W), self.kr * GRID_W)


def _na_probs(q, k, bias):
    sc = lax.dot_general(q, k, _DN["nt"], preferred_element_type=F32) * (NA_DIM ** -0.5) + bias
    e = jnp.exp(sc - jnp.max(sc, axis=-1, keepdims=True))
    return e / jnp.sum(e, axis=-1, keepdims=True)


def _na_fwd(qkv, table, comm=()):
    plan = _NaPlan(qkv.shape[0])
    blk, k_spec, v_spec, t_spec = plan.specs()

    def body(q_ref, k_ref, v_ref, t_ref, o_ref, bias_ref):
        g = pl.program_id(1)
        plan.fill_bias(t_ref, bias_ref, g)
        win = plan.window(g)
        for h in range(NA_HG):
            cs = slice(h * NA_DIM, (h + 1) * NA_DIM)
            p = _na_probs(q_ref[:, cs], k_ref[win, cs], bias_ref[h])
            o_ref[:, cs] = jnp.dot(p.astype(BF), v_ref[win, cs], preferred_element_type=F32).astype(BF)

    return _call("na_fwd", body, (plan.nq, plan.groups), [blk, k_spec, v_spec, t_spec], blk,
                 jax.ShapeDtypeStruct((plan.s, NA_HEADS * NA_DIM), BF), [qkv, qkv, qkv, table], comm,
                 scratch=[plan.bias_scratch()])


def _na_bwd(qkv, table, do, comm=()):
    plan = _NaPlan(qkv.shape[0])
    blk, k_spec, v_spec, t_spec = plan.specs()
    qr, kr = plan.qr, plan.kr

    def body(q_ref, k_ref, v_ref, t_ref, do_ref, dq_ref, dk_ref, dv_ref, dt_ref, bias_ref):
        g = pl.program_id(1)

        @pl.when(g == 0)
        def _():
            dk_ref[...] = jnp.zeros_like(dk_ref)
            dv_ref[...] = jnp.zeros_like(dv_ref)
            dt_ref[...] = jnp.zeros_like(dt_ref)

        plan.fill_bias(t_ref, bias_ref, g)
        win = plan.window(g)
        base = plan.first_key_row(g) - g * qr + NA_ROWS_WIN - 1
        for h in range(NA_HG):
            cs = slice(h * NA_DIM, (h + 1) * NA_DIM)
            q, k, v, do_h = q_ref[:, cs], k_ref[win, cs], v_ref[win, cs], do_ref[:, cs]
            p = _na_probs(q, k, bias_ref[h])
            dp = lax.dot_general(do_h, v, _DN["nt"], preferred_element_type=F32)
            ds = p * (dp - jnp.sum(p * dp, axis=-1, keepdims=True))
            for dlt in range(1 - qr, kr):
                tiles = [ds[a * GRID_W:(a + 1) * GRID_W, (a + dlt) * GRID_W:(a + dlt + 1) * GRID_W]
                         for a in range(qr) if 0 <= a + dlt < kr]
                dt_ref[h, jnp.clip(base + dlt, 0, plan.nd - 1)] += functools.reduce(jnp.add, tiles)
            dsb = (ds * (NA_DIM ** -0.5)).astype(BF)
            dq_ref[:, cs] = jnp.dot(dsb, k, preferred_element_type=F32).astype(BF)
            dk_ref[win, cs] += lax.dot_general(dsb, q, _DN["tn"], preferred_element_type=F32)
            dv_ref[win, cs] += lax.dot_general(p.astype(BF), do_h, _DN["tn"], preferred_element_type=F32)

    width = NA_HEADS * NA_DIM
    whole = pl.BlockSpec((plan.s, plan.hw), lambda j, g: (0, j))
    return _call(
        "na_bwd", body, (plan.nq, plan.groups), [blk, k_spec, v_spec, t_spec, blk], [blk, whole, whole, t_spec],
        [jax.ShapeDtypeStruct((plan.s, width), BF), jax.ShapeDtypeStruct((plan.s, width), F32),
         jax.ShapeDtypeStruct((plan.s, width), F32),
         jax.ShapeDtypeStruct((NA_HEADS, plan.nd, GRID_W, GRID_W), F32)],
        [qkv, qkv, qkv, table, do], comm, scratch=[plan.bias_scratch()])


def _na_rpb_grad(dt, rows):
    _, mask, dc = _na_geometry(rows)
    nd, nc = 2 * NA_ROWS_WIN - 1, 2 * NA_COLS_WIN - 1
    onehot = np.zeros((GRID_W * GRID_W, 128), np.float32)
    onehot[np.arange(GRID_W * GRID_W), dc.reshape(-1)] = mask.reshape(-1).astype(np.float32)
    flat = dt.reshape(NA_HEADS * nd, GRID_W * GRID_W)

    def body(a_ref, e_ref, o_ref):
        o_ref[...] = jnp.dot(a_ref[...], e_ref[...], precision=HI, preferred_element_type=F32)

    out = pl.pallas_call(body, name="na_rpb_grad", out_shape=jax.ShapeDtypeStruct((NA_HEADS * nd, 128), F32),
                         compiler_params=_params(0))(flat, jnp.asarray(onehot))
    return out[:, :nc].reshape(NA_HEADS, nd, nc)


def _rope_consts(s):
    pos = np.arange(s, dtype=np.float32)
    inv = (1.0 / (ROPE_THETA ** (np.arange(0, ML_ROPE, 2, dtype=np.float32) / ML_ROPE))).astype(np.float32)
    ang = pos[:, None] * inv[None, :]
    cos, sin = np.cos(ang).astype(np.float32), np.sin(ang).astype(np.float32)
    half = ML_ROPE // 2
    rot = np.zeros((ML_ROPE, ML_ROPE), np.float32)
    rot[np.arange(half) + half, np.arange(half)] = -1.0
    rot[np.arange(half), np.arange(half) + half] = 1.0
    return (jnp.asarray(np.concatenate([cos, cos], 1)), jnp.asarray(np.concatenate([sin, sin], 1)),
            jnp.asarray(rot), jnp.asarray(rot.T.copy()))


def _rope(v, cos, sin, rot):
    return v * cos + jnp.dot(v, rot, precision=HI, preferred_element_type=F32) * sin


def _unrope(dv, cos, sin, rot_t):
    return dv * cos + jnp.dot(dv * sin, rot_t, precision=HI, preferred_element_type=F32)


def _rms(v, g):
    return v * lax.rsqrt(jnp.mean(v * v, axis=-1, keepdims=True) + EPS) * g


def _mla_prep(lat, gq, gkv, cos, sin, rot, tm=256):
    s, w = lat.shape
    tm = _tile(s, tm)

    def body(l_ref, gq_ref, gkv_ref, c_ref, s_ref, r_ref, cq_ref, ckv_ref, kr_ref):
        cq_ref[...] = _rms(l_ref[:, :ML_RANK], gq_ref[...]).astype(BF)
        ckv_ref[...] = _rms(l_ref[:, ML_RANK:2 * ML_RANK], gkv_ref[...]).astype(BF)
        kr_ref[...] = _rope(l_ref[:, 2 * ML_RANK:], c_ref[...], s_ref[...], r_ref[...]).astype(BF)

    row = lambda c: pl.BlockSpec((tm, c), lambda i: (i, 0))
    full = lambda a: pl.BlockSpec(a.shape, lambda i: (0, 0))
    return pl.pallas_call(
        body, name="mla_prep", grid=(s // tm,),
        in_specs=[row(w), full(gq), full(gkv), row(ML_ROPE), row(ML_ROPE), full(rot)],
        out_specs=[row(ML_RANK), row(ML_RANK), row(ML_ROPE)],
        out_shape=[jax.ShapeDtypeStruct((s, ML_RANK), BF), jax.ShapeDtypeStruct((s, ML_RANK), BF),
                   jax.ShapeDtypeStruct((s, ML_ROPE), BF)],
        compiler_params=_params(1))(lat, gq, gkv, cos, sin, rot)


def _mla_q_proj(cq, wuq, cos, sin, rot, tm=512, comm=()):
    s, k = cq.shape
    tm = _tile(s, tm)

    def epi(accs, ex, out):
        acc = accs[0]
        out[0][:, :ML_NOPE] = acc[:, :ML_NOPE].astype(BF)
        out[0][:, ML_NOPE:] = _rope(acc[:, ML_NOPE:], ex[0][...], ex[1][...], ex[2][...]).astype(BF)

    rmap = lambda j, i: (i, 0)
    return _mm("mla_q_proj", (ML_HEADS, s // tm),
               [(cq, (tm, k), rmap, wuq, (None, ML_QK, k), lambda j, i: (j, 0, 0), "nt", 0, 0)],
               [(cos, (tm, ML_ROPE), rmap), (sin, (tm, ML_ROPE), rmap), (rot, rot.shape, lambda j, i: (0, 0))],
               [((ML_HEADS, s, ML_QK), BF, (None, tm, ML_QK), lambda j, i: (j, i, 0))], epi, comm=comm)[0]


def _mla_kv_proj(ckv, wukv, kr, tm=512, comm=()):
    s, k = ckv.shape
    tm = _tile(s, tm)

    def epi(accs, ex, out):
        acc = accs[0]
        out[0][:, :ML_NOPE] = acc[:, :ML_NOPE].astype(BF)
        out[0][:, ML_NOPE:] = ex[0][...]
        out[1][...] = acc[:, ML_NOPE:].astype(BF)

    rmap = lambda j, i: (i, 0)
    gmap = lambda j, i: (j, i, 0)
    return _mm("mla_kv_proj", (ML_HEADS, s // tm),
               [(ckv, (tm, k), rmap, wukv, (None, k, ML_NOPE + ML_V), lambda j, i: (j, 0, 0), "nn", 0, 0)],
               [(kr, (tm, ML_ROPE), rmap)],
               [((ML_HEADS, s, ML_QK), BF, (None, tm, ML_QK), gmap), ((ML_HEADS, s, ML_V), BF, (None, tm, ML_V), gmap)],
               epi, comm=comm)


def _mla_probs(q, k):
    sc = lax.dot_general(q, k, _DN["nt"], preferred_element_type=F32) * (ML_QK ** -0.5)
    e = jnp.exp(sc - jnp.max(sc, axis=-1, keepdims=True))
    return e / jnp.sum(e, axis=-1, keepdims=True)


def _mla_fwd(q, k, v, tq=1024, comm=()):
    _, s, _ = q.shape
    tq = _tile(s, tq)

    def body(q_ref, k_ref, v_ref, o_ref):
        p = _mla_probs(q_ref[...], k_ref[...])
        o_ref[...] = jnp.dot(p.astype(BF), v_ref[...], preferred_element_type=F32).astype(BF)

    return _call("mla_fwd", body, (ML_HEADS, s // tq),
                 [pl.BlockSpec((None, tq, ML_QK), lambda h, i: (h, i, 0)),
                  pl.BlockSpec((None, s, ML_QK), lambda h, i: (h, 0, 0)),
                  pl.BlockSpec((None, s, ML_V), lambda h, i: (h, 0, 0))],
                 pl.BlockSpec((tq, ML_V), lambda h, i: (i, h)),
                 jax.ShapeDtypeStruct((s, ML_HEADS * ML_V), BF), [q, k, v], comm)


def _mla_bwd(q, k, v, do, tq=1024, comm=()):
    _, s, _ = q.shape
    tq = _tile(s, tq)

    def body(q_ref, k_ref, v_ref, do_ref, dq_ref, dk_ref, dv_ref):
        i = pl.program_id(1)
        qv, kv, vv, dov = q_ref[...], k_ref[...], v_ref[...], do_ref[...]
        p = _mla_probs(qv, kv)
        dp = lax.dot_general(dov, vv, _DN["nt"], preferred_element_type=F32)
        ds = (p * (dp - jnp.sum(p * dp, axis=-1, keepdims=True)) * (ML_QK ** -0.5)).astype(BF)
        dq_ref[...] = jnp.dot(ds, kv, preferred_element_type=F32)
        _acc_rows(dk_ref, lax.dot_general(ds, qv, _DN["tn"], preferred_element_type=F32), i)
        _acc_rows(dv_ref, lax.dot_general(p.astype(BF), dov, _DN["tn"], preferred_element_type=F32), i)

    return _call(
        "mla_bwd", body, (ML_HEADS, s // tq),
        [pl.BlockSpec((None, tq, ML_QK), lambda h, i: (h, i, 0)),
         pl.BlockSpec((None, s, ML_QK), lambda h, i: (h, 0, 0)),
         pl.BlockSpec((None, s, ML_V), lambda h, i: (h, 0, 0)),
         pl.BlockSpec((tq, ML_V), lambda h, i: (i, h))],
        [pl.BlockSpec((None, tq, ML_QK), lambda h, i: (h, i, 0)),
         pl.BlockSpec((None, s, ML_QK), lambda h, i: (h, 0, 0)),
         pl.BlockSpec((None, s, ML_V), lambda h, i: (h, 0, 0))],
        [jax.ShapeDtypeStruct((ML_HEADS, s, ML_QK), F32), jax.ShapeDtypeStruct((ML_HEADS, s, ML_QK), F32),
         jax.ShapeDtypeStruct((ML_HEADS, s, ML_V), F32)],
        [q, k, v, do], comm)


def _mla_post(dq, dk, dv, cos, sin, rot_t, tm=1024):
    _, s, _ = dq.shape
    tm = _tile(s, tm)

    def body(dq_ref, dk_ref, dv_ref, c_ref, s_ref, r_ref, dqp_ref, dkv_ref, dkr_ref):
        h = pl.program_id(1)
        dqv, dkk = dq_ref[...], dk_ref[...]
        dqp_ref[:, :ML_NOPE] = dqv[:, :ML_NOPE].astype(BF)
        dqp_ref[:, ML_NOPE:] = _unrope(dqv[:, ML_NOPE:], c_ref[...], s_ref[...], r_ref[...]).astype(BF)
        dkv_ref[:, :ML_NOPE] = dkk[:, :ML_NOPE].astype(BF)
        dkv_ref[:, ML_NOPE:] = dv_ref[...].astype(BF)
        _acc_rows(dkr_ref, dkk[:, ML_NOPE:], h)

    gspec = lambda c: pl.BlockSpec((None, tm, c), lambda i, h: (h, i, 0))
    rspec = pl.BlockSpec((tm, ML_ROPE), lambda i, h: (i, 0))
    return pl.pallas_call(
        body, name="mla_post", grid=(s // tm, ML_HEADS),
        in_specs=[gspec(ML_QK), gspec(ML_QK), gspec(ML_V), rspec, rspec,
                  pl.BlockSpec(rot_t.shape, lambda i, h: (0, 0))],
        out_specs=[gspec(ML_QK), gspec(ML_NOPE + ML_V), rspec],
        out_shape=[jax.ShapeDtypeStruct((ML_HEADS, s, ML_QK), BF),
                   jax.ShapeDtypeStruct((ML_HEADS, s, ML_NOPE + ML_V), BF),
                   jax.ShapeDtypeStruct((s, ML_ROPE), F32)],
        compiler_params=_params(2))(dq, dk, dv, cos, sin, rot_t)


def _mla_lat_bwd(dcq, dckv, dkr, lat, gq, gkv, cos, sin, rot_t, tm=256):
    s, w = lat.shape
    tm = _tile(s, tm)

    def body(dcq_ref, dckv_ref, dkr_ref, l_ref, gq_ref, gkv_ref, c_ref, s_ref, r_ref, dl_ref, dgq_ref, dgkv_ref):
        i = pl.program_id(0)
        dql, pq = _rms_bwd_math(dcq_ref[...], l_ref[:, :ML_RANK], gq_ref[...])
        dkl, pkv = _rms_bwd_math(dckv_ref[...], l_ref[:, ML_RANK:2 * ML_RANK], gkv_ref[...])
        dl_ref[:, :ML_RANK] = dql.astype(BF)
        dl_ref[:, ML_RANK:2 * ML_RANK] = dkl.astype(BF)
        dl_ref[:, 2 * ML_RANK:] = _unrope(dkr_ref[...], c_ref[...], s_ref[...], r_ref[...]).astype(BF)
        _acc_rows(dgq_ref, pq, i)
        _acc_rows(dgkv_ref, pkv, i)

    row = lambda c: pl.BlockSpec((tm, c), lambda i: (i, 0))
    full = lambda a: pl.BlockSpec(a.shape, lambda i: (0, 0))
    return pl.pallas_call(
        body, name="mla_lat_bwd", grid=(s // tm,),
        in_specs=[row(ML_RANK), row(ML_RANK), row(ML_ROPE), row(w), full(gq), full(gkv), row(ML_ROPE), row(ML_ROPE),
                  full(rot_t)],
        out_specs=[row(w), full(gq), full(gkv)],
        out_shape=[jax.ShapeDtypeStruct((s, w), BF), jax.ShapeDtypeStruct(gq.shape, F32),
                   jax.ShapeDtypeStruct(gkv.shape, F32)],
        compiler_params=_params(1))(dcq, dckv, dkr, lat, gq, gkv, cos, sin, rot_t)


def _grp_dw(name, a, dout, ta=1024):
    s, k = a.shape
    ta = _tile(k, ta)
    if dout.ndim == 3:
        g, _, nb = dout.shape
        b_blk, b_map = (None, s, nb), lambda j, i: (j, 0, 0)
    else:
        g, nb = NDEV, dout.shape[1] // NDEV
        b_blk, b_map = (s, nb), lambda j, i: (0, j)
    return _mm(name, (g, k // ta),
               [(a, (s, ta), lambda j, i: (0, i), dout, b_blk, b_map, "tn", 0, 0)], [],
               [((g, k, nb), BF, (None, ta, nb), lambda j, i: (j, i, 0))], _store)[0]


def _grp_dw_t(name, dout, a, ta=512):
    g, s, nb = dout.shape
    k = a.shape[1]
    ta = _tile(k, ta)
    return _mm(name, (g, k // ta),
               [(dout, (None, s, nb), lambda j, i: (j, 0, 0), a, (s, ta), lambda j, i: (0, i), "tn", 0, 0)], [],
               [((g, nb, k), BF, (None, nb, ta), lambda j, i: (j, 0, i))], _store)[0]


def _grp_dx_t(name, dout, wt, tm=512, tn=512, comm=()):
    g, s, nb = dout.shape
    k = wt.shape[2]
    tm, tn = _tile(s, tm), _tile(k, tn)
    return _mm(name, (k // tn, s // tm),
               [(dout, (g, tm, nb), lambda j, i: (0, i, 0), wt, (g, nb, tn), lambda j, i: (0, 0, j), "nn", 0, g)], [],
               [((s, k), F32, (tm, tn), lambda j, i: (i, j))], _store, comm=comm)[0]


def _grp_dx(name, dout, w, tm=512, tn=512, out_dtype=F32, comm=()):
    g, s, nb = dout.shape
    k = w.shape[1]
    tm, tn = _tile(s, tm), _tile(k, tn)
    return _mm(name, (k // tn, s // tm),
               [(dout, (g, tm, nb), lambda j, i: (0, i, 0), w, (g, tn, nb), lambda j, i: (0, j, 0), "nt", 0, g)], [],
               [((s, k), out_dtype, (tm, tn), lambda j, i: (i, j))], _store, comm=comm)[0]


def _row_dw(name, a, dout, tn=2048):
    s, n = dout.shape
    tn = _tile(n, tn)
    if a.ndim == 3:
        kb = a.shape[2]
        a_blk, a_map = (None, s, kb), lambda j, i: (j, 0, 0)
    else:
        kb = a.shape[1] // NDEV
        a_blk, a_map = (s, kb), lambda j, i: (0, j)
    return _mm(name, (NDEV, n // tn),
               [(a, a_blk, a_map, dout, (s, tn), lambda j, i: (0, i), "tn", 0, 0)], [],
               [((NDEV, kb, n), BF, (None, kb, tn), lambda j, i: (j, 0, i))], _store)[0]


def _mix_merge(oa, ob, wa, wb, ga, gb, tm=1024, comm=()):
    s, k = oa.shape
    g, _, nb = wa.shape
    tm = _tile(s, tm)

    def epi(accs, ex, out):
        ya, yb = accs
        out[0][...] = ya.astype(BF)
        out[1][...] = yb.astype(BF)
        out[2][...] = (_sig(ex[0][...]) * ya + _sig(ex[1][...]) * yb).astype(BF)

    rmap = lambda j, i: (i, 0)
    wmap = lambda j, i: (j, 0, 0)
    o = ((g, s, nb), BF, (None, tm, nb), lambda j, i: (j, i, 0))
    cmap = lambda j, i: (i, j)
    return _mm("mix_merge", (g, s // tm),
               [(oa, (tm, k), rmap, wa, (None, k, nb), wmap, "nn", 0, 0),
                (ob, (tm, k), rmap, wb, (None, k, nb), wmap, "nn", 1, 0)],
               [(ga, (tm, nb), cmap), (gb, (tm, nb), cmap)], [o, o, o], epi, nacc=2, comm=comm)


def _mix_out(merged, wout, resid, tm=1024, tn=512):
    g, s, kb = merged.shape
    d = wout.shape[2]
    tm, tn = _tile(s, tm), _tile(d, tn)

    def epi(accs, ex, out):
        out[0][...] = ex[0][...] + accs[0]

    return _mm("mix_out", (d // tn, s // tm),
               [(merged, (g, tm, kb), lambda j, i: (0, i, 0), wout, (g, kb, tn), lambda j, i: (0, 0, j), "nn", 0, g)],
               [(resid, (tm, tn), lambda j, i: (i, j))],
               [((s, d), F32, (tm, tn), lambda j, i: (i, j))], epi)[0]


def _mix_out_bwd(dh, wout, ga, gb, ya, yb, tm=1024, comm=()):
    s, d = dh.shape
    g, kb, _ = wout.shape
    tm = _tile(s, tm)

    def epi(accs, ex, out):
        dm = accs[0]
        sa, sb = _sig(ex[0][...]), _sig(ex[1][...])
        out[0][...] = (dm * sa).astype(BF)
        out[1][...] = (dm * sb).astype(BF)
        out[2][...] = (dm * ex[2][...].astype(F32) * sa * (1.0 - sa)).astype(BF)
        out[3][...] = (dm * ex[3][...].astype(F32) * sb * (1.0 - sb)).astype(BF)

    cmap = lambda j, i: (i, j)
    gmap = lambda j, i: (j, i, 0)
    og = ((g, s, kb), BF, (None, tm, kb), gmap)
    oc = ((s, g * kb), BF, (tm, kb), cmap)
    return _mm("mix_out_bwd", (g, s // tm),
               [(dh, (tm, d), lambda j, i: (i, 0), wout, (None, kb, d), lambda j, i: (j, 0, 0), "nt", 0, 0)],
               [(ga, (tm, kb), cmap), (gb, (tm, kb), cmap), (ya, (None, tm, kb), gmap), (yb, (None, tm, kb), gmap)],
               [og, og, oc, oc], epi, comm=comm)


def _pl_forward(n4, wplg, p, wpl, h3, tm=1024):
    s, d = n4.shape
    g, kb, _ = wplg.shape
    kp, nb = wpl.shape[1], wpl.shape[2]
    tm = _tile(s, tm)
    wplg_nat = wplg.reshape(g * kb, d)

    def epi(accs, ex, out):
        t, pe = accs
        out[0][...] = ex[0][...] + _sig(t) * pe
        out[1][...] = t
        out[2][...] = pe.astype(BF)

    rmap = lambda j, i: (i, 0)
    cmap = lambda j, i: (i, j)
    return _mm("pl_forward", (g, s // tm),
               [(n4, (tm, d), rmap, wplg_nat, (g * kb, nb), lambda j, i: (0, j), "nn", 0, 0),
                (p, (tm, kp), rmap, wpl, (None, kp, nb), lambda j, i: (j, 0, 0), "nn", 1, 0)],
               [(h3, (tm, nb), cmap)],
               [((s, d), F32, (tm, nb), cmap), ((s, d), F32, (tm, nb), cmap), ((s, d), BF, (tm, nb), cmap)],
               epi, nacc=2)


def _row_dx(name, dout, w, tm=1024, comm=()):
    s, n = dout.shape
    g, kb, _ = w.shape
    tm = _tile(s, tm)
    return _mm(name, (g, s // tm),
               [(dout, (tm, n), lambda j, i: (i, 0), w, (None, kb, n), lambda j, i: (j, 0, 0), "nt", 0, 0)], [],
               [((s, g * kb), F32, (tm, kb), lambda j, i: (i, j))], _store, comm=comm)[0]


def _in_proj_bwd_x(pieces, weights, tm=512, tn=512, comm=()):
    s = pieces[0].shape[0]
    d = weights[0].shape[1]
    tm, tn = _tile(s, tm), _tile(d, tn)
    prods = [(pc, (tm, pc.shape[1]), lambda j, i: (i, 0), w, (pc.shape[1], tn), lambda j, i: (0, j), "nn", 0, 0)
             for pc, w in zip(pieces, weights)]
    return _mm("in_proj_dx", (d // tn, s // tm), prods, [],
               [((s, d), F32, (tm, tn), lambda j, i: (i, j))], _store, comm=comm)[0]


def _split_w_in(w_in_t):
    g, nb, d = w_in_t.shape
    nat = w_in_t.reshape(g * nb, d)
    na, lat = 3 * NA_HEADS * NA_DIM, 2 * ML_RANK + ML_ROPE
    return nat, nat[na:na + lat], nat[na + lat:na + lat + d], nat[na + lat + d:]


def _pair_sum(name, part, landed, core):
    _, _, r, c = part.shape
    tr, tc = _ew_tile(r, c)

    def body(core_ref, a_ref, b_ref, o_ref):
        o_ref[...] = (a_ref[...].astype(F32) + b_ref[...].astype(F32)).astype(o_ref.dtype)

    return pl.pallas_call(
        body, name=name,
        grid_spec=pltpu.PrefetchScalarGridSpec(
            num_scalar_prefetch=1, grid=(NCHIP, r // tr, c // tc),
            in_specs=[pl.BlockSpec((None, None, tr, tc), lambda j, i, k, core_ref: (j, core_ref[0], i, k)),
                      pl.BlockSpec((None, tr, tc), lambda j, i, k, core_ref: (j, i, k))],
            out_specs=pl.BlockSpec((None, tr, tc), lambda j, i, k, core_ref: (j, i, k))),
        out_shape=jax.ShapeDtypeStruct(landed.shape, landed.dtype), compiler_params=_params(3),
    )(core, part, landed)


def _device_step(x, p, target, sp, own, core):
    s, d = x.shape
    rows = s // GRID_W
    cos, sin, rot, rot_t = _rope_consts(s)
    w, dw4, sums, dsp, pending = {}, {}, {}, {}, []

    def gather(*names):
        return _GatherPart(names, [own[n] for n in names])

    def got(part):
        w.update(zip(part.names, part.results))

    def grad(name, g):
        dw4[name] = g.reshape((NCHIP, 2) + g.shape[1:])

    def to_sibling(*names):
        return _SiblingPart(names, [dw4[n] for n in names])

    def add_pairs(part):
        for n, landed in zip(part.names, part.results):
            sums[n] = _pair_sum("pair_sum_" + n, dw4[n], landed, core)

    def start_chips(tag, *names):
        send, recv, thru, lands, token = _chips_start("rs_start_" + tag, [sums[n] for n in names])
        pending.append((tag, names, send, recv, thru, lands))
        return token

    c0 = gather("ffn1_w_gate", "ffn1_w_up")
    c1 = gather("ffn1_w_down")
    c2 = gather("w_in")

    def ffn1_wgu():
        got(c0)
        return w["ffn1_w_gate"], w["ffn1_w_up"]

    def ffn1_wd():
        got(c1)
        return w["ffn1_w_down"]

    h1, ffn1_saved = _ffn_forward("ffn1", x, sp["ffn1_norm"], ffn1_wgu, ffn1_wd,
                                  norm_comm=[c0], up_comm=[c1], down_comm=[c2])
    got(c2)
    wqkv, wlat, wga, wgb = _split_w_in(w["w_in"])
    u = _rms_fwd("mix_norm", h1, sp["mix_norm"])
    c3 = gather("w_uq", "w_ukv")
    qkv = _mm_nt("in_qkv", u, wqkv, BF, tn=1024, comm=[c3], rows=3 * NA_HEADS * NA_DIM)
    got(c3)
    lat = _mm_nt("in_lat", u, wlat, F32, tm=1024)
    c3a = gather("w_branch_a")
    ga = _mm_nt("in_ga", u, wga, F32, tn=1024, comm=[c3a])
    got(c3a)
    c3b = gather("w_branch_b")
    gb = _mm_nt("in_gb", u, wgb, F32, tn=1024, comm=[c3b])
    got(c3b)
    tb = _na_table(sp["na_rpb"], rows)
    c4 = gather("ffn2_w_gate")
    oa = _na_fwd(qkv, tb, comm=[c4])
    got(c4)
    cq, ckv, kr = _mla_prep(lat, sp["q_a_norm"], sp["kv_a_norm"], cos, sin, rot)
    c4a = gather("w_out")
    qf = _mla_q_proj(cq, w["w_uq"], cos, sin, rot, comm=[c4a])
    got(c4a)
    c4b = gather("w_pl_gate")
    kf, vf = _mla_kv_proj(ckv, w["w_ukv"], kr, comm=[c4b])
    got(c4b)
    c5 = gather("ffn2_w_up")
    ob = _mla_fwd(qf, kf, vf, comm=[c5])
    got(c5)
    c5a = gather("w_pl")
    ya, yb, merged = _mix_merge(oa, ob, w["w_branch_a"], w["w_branch_b"], ga, gb, comm=[c5a])
    got(c5a)
    h2 = _mix_out(merged, w["w_out"], h1)
    c6 = gather("ffn2_w_down")

    def ffn2_wd():
        got(c6)
        return w["ffn2_w_down"]

    h3, ffn2_saved = _ffn_forward("ffn2", h2, sp["ffn2_norm"], lambda: (w["ffn2_w_gate"], w["ffn2_w_up"]), ffn2_wd,
                                  up_comm=[c6])
    n4 = _rms_fwd("pl_norm", h3, sp["pl_norm"])
    pb = p.astype(BF)
    h4, t, pe = _pl_forward(n4, w["w_pl_gate"], pb, w["w_pl"], h3)

    dh4, dsp["final_norm"], loss = _loss_head(h4, target, sp["final_norm"])
    dt, dpe = _pl_bwd_elem(dh4, pe, t)
    grad("w_pl", _grp_dw("pl_dw", pb, dpe))
    grad("w_pl_gate", _row_dw("plg_dw", n4, dt))
    s1 = to_sibling("w_pl", "w_pl_gate")
    dn4 = _row_dx("plg_dx", dt, w["w_pl_gate"], comm=[s1])
    add_pairs(s1)
    dh3, dhb, dsp["pl_norm"] = _rms_bwd("pl_dnorm", dn4, h3, sp["pl_norm"], dh4)

    xn, hg, hu, a = ffn2_saved
    grad("ffn2_w_down", _ffn_bwd_wd("ffn2_dwd", a, dhb))
    s2 = to_sibling("ffn2_w_down")
    dhg, dhu = _ffn_bwd_act("ffn2_dact", dhb, w["ffn2_w_down"], hg, hu, comm=[s2])
    add_pairs(s2)
    tok = start_chips("ffn2_down", "w_pl", "w_pl_gate", "ffn2_w_down")
    dwg, dwu = _ffn_bwd_wup("ffn2_dwup", xn, dhg, dhu, comm=[_After(tok)])
    grad("ffn2_w_gate", dwg)
    grad("ffn2_w_up", dwu)
    s3 = to_sibling("ffn2_w_gate", "ffn2_w_up")
    dxn = _ffn_bwd_x("ffn2_dx", dhg, dhu, w["ffn2_w_gate"], w["ffn2_w_up"], comm=[s3])
    add_pairs(s3)
    tok = start_chips("ffn2_up", "ffn2_w_gate", "ffn2_w_up")
    dh2, dh2b, dsp["ffn2_norm"] = _rms_bwd("ffn2_dnorm", dxn, h2, sp["ffn2_norm"], dh3, comm=[_After(tok)])

    grad("w_out", _row_dw("out_dw", merged, dh2b))
    s4 = to_sibling("w_out")
    dya, dyb, dga, dgb = _mix_out_bwd(dh2b, w["w_out"], ga, gb, ya, yb, comm=[s4])
    add_pairs(s4)
    grad("w_branch_a", _grp_dw("bra_dw", oa, dya))
    grad("w_branch_b", _grp_dw("brb_dw", ob, dyb))
    doa = _grp_dx("bra_dx", dya, w["w_branch_a"], out_dtype=BF)
    s5 = to_sibling("w_branch_a", "w_branch_b")
    dob = _grp_dx("brb_dx", dyb, w["w_branch_b"], out_dtype=BF, comm=[s5])
    add_pairs(s5)
    tok = start_chips("mix", "w_out", "w_branch_a", "w_branch_b")

    dqf, dkf, dvf = _mla_bwd(qf, kf, vf, dob, comm=[_After(tok)])
    dqp, dkv, dkr = _mla_post(dqf, dkf, dvf, cos, sin, rot_t)
    grad("w_uq", _grp_dw_t("uq_dw", dqp, cq))
    grad("w_ukv", _grp_dw("ukv_dw", ckv, dkv))
    dcq = _grp_dx_t("uq_dx", dqp, w["w_uq"])
    s6 = to_sibling("w_uq", "w_ukv")
    dckv = _grp_dx("ukv_dx", dkv, w["w_ukv"], comm=[s6])
    add_pairs(s6)
    dlat, dsp["q_a_norm"], dsp["kv_a_norm"] = _mla_lat_bwd(dcq, dckv, dkr, lat, sp["q_a_norm"], sp["kv_a_norm"],
                                                         cos, sin, rot_t)
    dq_na, dk_na, dv_na, dtab = _na_bwd(qkv, tb, doa)
    dsp["na_rpb"] = _na_rpb_grad(dtab, rows)
    dqkv = jnp.concatenate([dq_na, dk_na.astype(BF), dv_na.astype(BF)], axis=1)

    pieces = [dqkv, dlat, dga, dgb]
    dwin = jnp.zeros((sum(pc.shape[1] for pc in pieces), d), BF)
    row0 = 0
    for i, pc in enumerate(pieces):
        dwin = _mm_tn_into("in_dw%d" % i, pc, u, dwin, row0)
        row0 += pc.shape[1]
    grad("w_in", dwin.reshape(NDEV, -1, d))
    s7 = to_sibling("w_in")
    du = _in_proj_bwd_x(pieces, [wqkv, wlat, wga, wgb], comm=[s7])
    add_pairs(s7)
    tok = start_chips("w_in", "w_uq", "w_ukv", "w_in")
    dh1, dhb, dsp["mix_norm"] = _rms_bwd("mix_dnorm", du, h1, sp["mix_norm"], dh2, comm=[_After(tok)])

    xn, hg, hu, a = ffn1_saved
    grad("ffn1_w_down", _ffn_bwd_wd("ffn1_dwd", a, dhb))
    s8 = to_sibling("ffn1_w_down")
    dhg, dhu = _ffn_bwd_act("ffn1_dact", dhb, w["ffn1_w_down"], hg, hu, comm=[s8])
    add_pairs(s8)
    tok = start_chips("ffn1_down", "ffn1_w_down")
    dwg, dwu = _ffn_bwd_wup("ffn1_dwup", xn, dhg, dhu, comm=[_After(tok)])
    grad("ffn1_w_gate", dwg)
    grad("ffn1_w_up", dwu)
    s9 = to_sibling("ffn1_w_gate", "ffn1_w_up")
    _comm_only("rs_sibling_ffn1", [s9])
    add_pairs(s9)
    tok = start_chips("ffn1_up", "ffn1_w_gate", "ffn1_w_up")
    dxn = _ffn_bwd_x("ffn1_dx", dhg, dhu, w["ffn1_w_gate"], w["ffn1_w_up"], comm=[_After(tok)])
    dx, _, dsp["ffn1_norm"] = _rms_bwd("ffn1_dnorm", dxn, x, sp["ffn1_norm"], dh1)
    return loss, dx, pending, dsp


def _gather_small(buf):
    def body(in_ref, out_ref, send_sems, recv_sems, local_sem):
        x, y, c = _coords()
        mine = pltpu.make_async_copy(in_ref, out_ref.at[4 * x + 2 * y + c], local_sem)
        mine.start()
        cps = []
        for k in range(1, NDEV):
            fx, fy, fc = (k >> 2) & 1, (k >> 1) & 1, k & 1
            peer = (x ^ fx, y ^ fy, c ^ fc)
            cps.append(pltpu.make_async_remote_copy(
                src_ref=in_ref, dst_ref=out_ref.at[4 * x + 2 * y + c], send_sem=send_sems.at[k - 1],
                recv_sem=recv_sems.at[k - 1], device_id=peer, device_id_type=MESH))
        for cp in cps:
            cp.start()
        for k in range(1, NDEV):
            fx, fy, fc = (k >> 2) & 1, (k >> 1) & 1, k & 1
            px, py, pc = x ^ fx, y ^ fy, c ^ fc
            pltpu.make_async_remote_copy(
                src_ref=in_ref, dst_ref=out_ref.at[4 * px + 2 * py + pc], send_sem=send_sems.at[k - 1],
                recv_sem=recv_sems.at[k - 1], device_id=(px, py, pc), device_id_type=MESH).wait_recv()
        for cp in cps:
            cp.wait_send()
        mine.wait()

    return pl.pallas_call(
        body, name="gather_small", in_specs=[ANY], out_specs=ANY,
        out_shape=jax.ShapeDtypeStruct((NDEV,) + buf.shape, buf.dtype),
        scratch_shapes=[pltpu.SemaphoreType.DMA((NDEV - 1,)), pltpu.SemaphoreType.DMA((NDEV - 1,)),
                        pltpu.SemaphoreType.DMA],
    )(buf)


def _adam_math(wv, g, m, v):
    m_new = B1 * m + (1.0 - B1) * g
    v_new = B2 * v + (1.0 - B2) * (g * g)
    m_hat = m_new / (1.0 - B1 ** STEP)
    v_hat = v_new / (1.0 - B2 ** STEP)
    return -LR * (m_hat / (jnp.sqrt(v_hat) + ADAM_EPS) + WD * wv), m_new, v_new


def _adam(name, parts, wv, m, v):
    npart, r, c = parts.shape
    tr, tc = _ew_tile(r, c)

    def body(p_ref, w_ref, m_ref, v_ref, g_ref, d_ref, mo_ref, vo_ref):
        g = p_ref[0].astype(F32)
        for j in range(1, npart):
            g = g + p_ref[j].astype(F32)
        g_ref[...] = g
        d_ref[...], mo_ref[...], vo_ref[...] = _adam_math(w_ref[...], g, m_ref[...], v_ref[...])

    blk = pl.BlockSpec((tr, tc), lambda i, k: (i, k))
    return pl.pallas_call(
        body, name=name, grid=(r // tr, c // tc),
        in_specs=[pl.BlockSpec((npart, tr, tc), lambda i, k: (0, i, k)), blk, blk, blk],
        out_specs=[blk] * 4, out_shape=[jax.ShapeDtypeStruct((r, c), F32)] * 4, compiler_params=_params(2),
    )(parts, wv, m, v)


def _adam_exchanged(name, sums, land, wv, m, v, my_chip):
    _, r, c = sums.shape
    tr, tc = _ew_tile(r, c)

    def body(chip_ref, s_ref, l_ref, w_ref, m_ref, v_ref, g_ref, d_ref, mo_ref, vo_ref):
        g = s_ref[...].astype(F32)
        for j in range(3):
            g = g + l_ref[j].astype(F32)
        g_ref[...] = g
        d_ref[...], mo_ref[...], vo_ref[...] = _adam_math(w_ref[...], g, m_ref[...], v_ref[...])

    blk = pl.BlockSpec((tr, tc), lambda i, k, chip_ref: (i, k))
    return pl.pallas_call(
        body, name=name,
        grid_spec=pltpu.PrefetchScalarGridSpec(
            num_scalar_prefetch=1, grid=(r // tr, c // tc),
            in_specs=[pl.BlockSpec((None, tr, tc), lambda i, k, chip_ref: (chip_ref[0], i, k)),
                      pl.BlockSpec((3, tr, tc), lambda i, k, chip_ref: (0, i, k)), blk, blk, blk],
            out_specs=[blk] * 4),
        out_shape=[jax.ShapeDtypeStruct((r, c), F32)] * 4, compiler_params=_params(2),
    )(my_chip, sums, land, wv, m, v)


SHARDED = ("ffn1_w_gate", "ffn1_w_up", "ffn1_w_down", "w_in", "w_uq", "w_ukv", "w_branch_a", "w_branch_b", "w_out",
           "ffn2_w_gate", "ffn2_w_up", "ffn2_w_down", "w_pl", "w_pl_gate")
TRANSPOSED = ("ffn1_w_gate", "ffn1_w_up", "ffn2_w_gate", "ffn2_w_up", "w_in", "w_uq")
REPLICATED = ("ffn1_norm", "mix_norm", "q_a_norm", "kv_a_norm", "na_rpb", "ffn2_norm", "pl_norm", "final_norm")
WEIGHTS = ("ffn1_norm", "ffn1_w_gate", "ffn1_w_up", "ffn1_w_down", "mix_norm", "w_in", "q_a_norm", "w_uq",
           "kv_a_norm", "w_ukv", "na_rpb", "w_branch_a", "w_branch_b", "w_out", "ffn2_norm", "ffn2_w_gate",
           "ffn2_w_up", "ffn2_w_down", "pl_norm", "w_pl", "w_pl_gate", "final_norm")
SMALL_W = 2048


def _pack_small(vals):
    rows = []
    for name in REPLICATED:
        flat = vals[name].reshape(-1).astype(F32)
        n = -(-flat.shape[0] // SMALL_W) * SMALL_W
        rows.append(jnp.pad(flat, (0, n - flat.shape[0])).reshape(-1, SMALL_W))
    return jnp.concatenate(rows, axis=0)


def _unpack_small(buf, shapes):
    out, r = {}, 0
    for name in REPLICATED:
        size = int(np.prod(shapes[name]))
        nrow = -(-size // SMALL_W)
        out[name] = buf[r:r + nrow].reshape(-1)[:size].reshape(shapes[name])
        r += nrow
    return out


def kernel(x, p, ffn1_norm, ffn1_w_gate, ffn1_w_up, ffn1_w_down, mix_norm, w_in, q_a_norm, w_uq, kv_a_norm, w_ukv, na_rpb, w_branch_a, w_branch_b, w_out, ffn2_norm, ffn2_w_gate, ffn2_w_up, ffn2_w_down, pl_norm, w_pl, w_pl_gate, final_norm, loss_target, m_ffn1_norm, m_ffn1_w_gate, m_ffn1_w_up, m_ffn1_w_down, m_mix_norm, m_w_in, m_q_a_norm, m_w_uq, m_kv_a_norm, m_w_ukv, m_na_rpb, m_w_branch_a, m_w_branch_b, m_w_out, m_ffn2_norm, m_ffn2_w_gate, m_ffn2_w_up, m_ffn2_w_down, m_pl_norm, m_w_pl, m_w_pl_gate, m_final_norm, v_ffn1_norm, v_ffn1_w_gate, v_ffn1_w_up, v_ffn1_w_down, v_mix_norm, v_w_in, v_q_a_norm, v_w_uq, v_kv_a_norm, v_w_ukv, v_na_rpb, v_w_branch_a, v_w_branch_b, v_w_out, v_ffn2_norm, v_ffn2_w_gate, v_ffn2_w_up, v_ffn2_w_down, v_pl_norm, v_w_pl, v_w_pl_gate, v_final_norm):
    args = dict(locals())
    wts = {n: args[n] for n in WEIGHTS}
    mom = {n: args["m_" + n] for n in WEIGHTS}
    var = {n: args["v_" + n] for n in WEIGHTS}
    shapes = {n: wts[n].shape for n in WEIGHTS}
    core = lax.axis_index("c").astype(jnp.int32).reshape(1)

    local = lambda n, a: a[0].T if n in TRANSPOSED else a[0]
    own = {n: local(n, wts[n]).astype(BF) for n in SHARDED}
    sp = {n: wts[n].reshape(1, -1) for n in REPLICATED if n != "na_rpb"}
    sp["na_rpb"] = wts["na_rpb"][0]
    loss_part, grad_x, pending, dsp = _device_step(x[0], p[0, 0], loss_target[0], sp, own, core)

    out = {}
    last = grad_x
    my_chip = (2 * lax.axis_index("x") + lax.axis_index("y")).astype(jnp.int32).reshape(1)
    for tag, names, send, recv, thru, lands in pending:
        thru, lands = _chips_wait("rs_wait_" + tag, send, recv, thru, lands, last)
        for n, s4, l3 in zip(names, thru, lands):
            res4 = _adam_exchanged("adam_" + n, s4, l3, local(n, wts[n]), local(n, mom[n]), local(n, var[n]), my_chip)
            out[n] = tuple((a.T if n in TRANSPOSED else a)[None] for a in res4)
            last = res4[1]

    small = jnp.concatenate([_pack_small(dsp), jnp.pad(loss_part, ((0, 0), (0, SMALL_W - loss_part.shape[1])))], 0)
    pad_rows = -small.shape[0] % 8
    small = jnp.pad(small, ((0, pad_rows), (0, 0)))
    every = _gather_small(small)
    zeros = jnp.zeros((1 + pad_rows, SMALL_W), F32)
    pack = lambda d: jnp.concatenate([_pack_small(d), zeros], 0)
    g_s, d_s, m_s, v_s = _adam("adam_small", every, pack(wts), pack(mom), pack(var))
    n_rows = small.shape[0] - 1 - pad_rows
    loss = g_s[n_rows, 0]
    small_out = [_unpack_small(b, shapes) for b in (g_s, d_s, m_s, v_s)]
    for n in REPLICATED:
        out[n] = tuple(b[n] for b in small_out)

    res = [loss, grad_x[None]]
    for k in range(4):
        res += [out[n][k] for n in WEIGHTS]
    return tuple(res)
```

```python
import functools

import numpy as np
import jax
import jax.numpy as jnp
from jax import lax
from jax.experimental import pallas as pl
from jax.experimental.pallas import tpu as pltpu

F32 = jnp.float32
BF = jnp.bfloat16
MESH = pl.DeviceIdType.MESH

NDEV = 8
NCHIP = 4
VMEM_LIMIT = 56 * 1024 * 1024
EPS = 1e-6
NEG = -1e30
GRID_W = 64
NA_HEADS, NA_DIM = 8, 128
NA_ROWS_WIN, NA_COLS_WIN = 8, 16
NA_HG = 4
NA_QROWS = 4
ML_HEADS, ML_NOPE, ML_ROPE, ML_V = 8, 128, 64, 128
ML_QK = ML_NOPE + ML_ROPE
ML_RANK = 512
ROPE_THETA = 10000.0
LR, B1, B2, ADAM_EPS, WD, STEP = 0.001, 0.9, 0.999, 1e-08, 0.01, 10
HI = lax.Precision.HIGHEST

_DN = {"nn": (((1,), (0,)), ((), ())), "nt": (((1,), (1,)), ((), ())), "tn": (((0,), (0,)), ((), ()))}


def _params(n):
    return pltpu.CompilerParams(dimension_semantics=("arbitrary",) * n, vmem_limit_bytes=VMEM_LIMIT)


def _sig(v):
    return jax.nn.sigmoid(v)


ANY = pl.BlockSpec(memory_space=pl.ANY)


def _coords():
    return lax.axis_index("x"), lax.axis_index("y"), lax.axis_index("c")


class _Part:
    inputs, out_shapes, sem_shapes, results = (), (), (), None

    def mid(self, ins, outs, sems):
        pass

    def late(self, ins, outs, sems):
        pass


class _After(_Part):
    def __init__(self, token):
        self.inputs = [token]

    def start(self, ins, outs, sems):
        pass

    finish = start


class _GatherPart(_Part):
    def __init__(self, names, shards):
        n = len(shards)
        self.names, self.inputs = list(names), list(shards)
        self.out_shapes = [jax.ShapeDtypeStruct((NDEV,) + a.shape, a.dtype) for a in shards]
        self.sem_shapes = [pltpu.SemaphoreType.DMA((n, 7)), pltpu.SemaphoreType.DMA((n, 7)),
                           pltpu.SemaphoreType.DMA((n,))]

    def _plan(self, ins, outs, sems):
        send_sems, recv_sems, local_sems = sems
        x, y, c = _coords()
        me, sib, diag = (x, y, c), (x, y, 1 - c), (1 - x, 1 - y, c)
        n1, n2 = (x ^ (1 - c), y ^ c, c), (x ^ c, y ^ (1 - c), c)

        def copy(i, k, block, to, src=None):
            px, py, pc = block
            dst = outs[i].at[4 * px + 2 * py + pc]
            return pltpu.make_async_remote_copy(
                src_ref=dst if src is None else src, dst_ref=dst, send_sem=send_sems.at[i, k],
                recv_sem=recv_sems.at[i, k], device_id=to, device_id_type=MESH)

        mine = [pltpu.make_async_copy(ins[i], outs[i].at[4 * x + 2 * y + c], local_sems.at[i])
                for i in range(len(ins))]
        return copy, mine, me, sib, n1, n2, diag

    def _own_sends(self, ins, copy, me, sib, n1, n2):
        return [copy(i, k, me, to, src=ins[i]) for i in range(len(ins)) for k, to in enumerate((sib, n1, n2))]

    def start(self, ins, outs, sems):
        copy, mine, me, sib, n1, n2, _ = self._plan(ins, outs, sems)
        for cp in mine + self._own_sends(ins, copy, me, sib, n1, n2):
            cp.start()

    def mid(self, ins, outs, sems):
        copy, _, me, sib, n1, n2, _ = self._plan(ins, outs, sems)
        for i in range(len(ins)):
            copy(i, 1, n1, me).wait_recv()
            copy(i, 3, n1, n2).start()
            copy(i, 4, n1, sib).start()

    def late(self, ins, outs, sems):
        copy, _, me, sib, _, n2, diag = self._plan(ins, outs, sems)
        for i in range(len(ins)):
            copy(i, 2, n2, me).wait_recv()
            copy(i, 5, n2, sib).start()
        for i in range(len(ins)):
            copy(i, 3, diag, me).wait_recv()
            copy(i, 6, diag, sib).start()

    def finish(self, ins, outs, sems):
        copy, mine, me, sib, n1, n2, diag = self._plan(ins, outs, sems)
        other = lambda dev: (dev[0], dev[1], sib[2])
        n = len(ins)
        for i in range(n):
            copy(i, 0, sib, me).wait_recv()
            for k, block in ((4, other(n2)), (5, other(n1)), (6, other(diag))):
                copy(i, k, block, me).wait_recv()
        for cp in self._own_sends(ins, copy, me, sib, n1, n2):
            cp.wait_send()
        for i in range(n):
            for k, block in ((3, n1), (4, n1), (5, n2), (6, diag)):
                copy(i, k, block, sib).wait_send()
        for cp in mine:
            cp.wait()


class _SiblingPart(_Part):
    def __init__(self, names, parts):
        n = len(parts)
        self.names, self.inputs = list(names), list(parts)
        self.out_shapes = [jax.ShapeDtypeStruct((NCHIP,) + a.shape[2:], a.dtype) for a in parts]
        self.sem_shapes = [pltpu.SemaphoreType.DMA((n,)), pltpu.SemaphoreType.DMA((n,))]

    def _copies(self, ins, outs, sems):
        x, y, c = _coords()
        return [pltpu.make_async_remote_copy(
            src_ref=ins[i].at[:, 1 - c], dst_ref=outs[i], send_sem=sems[0].at[i], recv_sem=sems[1].at[i],
            device_id=(x, y, 1 - c), device_id_type=MESH) for i in range(len(ins))]

    def start(self, ins, outs, sems):
        for cp in self._copies(ins, outs, sems):
            cp.start()

    def finish(self, ins, outs, sems):
        cps = self._copies(ins, outs, sems)
        for cp in cps:
            cp.wait_recv()
        for cp in cps:
            cp.wait_send()


HBM = pl.BlockSpec(memory_space=pltpu.HBM)
SEM = pl.BlockSpec(memory_space=pltpu.SEMAPHORE)


def _chip_peers():
    x, y, c = _coords()
    return [(1 - x, y, c), (x, 1 - y, c), (1 - x, 1 - y, c)]


def _chips_start(name, sums):
    n = len(sums)

    def body(*refs):
        ins, lands, send_sems, recv_sems = refs[:n], refs[n:2 * n], refs[2 * n], refs[2 * n + 1]
        for i in range(n):
            for k, (px, py, pc) in enumerate(_chip_peers()):
                pltpu.make_async_remote_copy(
                    src_ref=ins[i].at[2 * px + py], dst_ref=lands[i].at[k], send_sem=send_sems.at[3 * i + k],
                    recv_sem=recv_sems.at[3 * i + k], device_id=(px, py, pc), device_id_type=MESH).start()
        refs[-1][...] = jnp.zeros_like(refs[-1])

    lands = [lax.empty((3,) + a.shape[1:], a.dtype) for a in sums]
    bufs = list(sums) + lands
    res = pl.pallas_call(
        body, name=name, in_specs=[HBM] * (2 * n),
        out_specs=(SEM, SEM, *[HBM] * (2 * n), pl.BlockSpec(memory_space=pltpu.VMEM)),
        out_shape=(pltpu.SemaphoreType.DMA((3 * n,)), pltpu.SemaphoreType.DMA((3 * n,)),
                   *[pltpu.HBM(a.shape, a.dtype) for a in bufs], jax.ShapeDtypeStruct((8, 128), F32)),
        input_output_aliases={i: 2 + i for i in range(2 * n)},
        compiler_params=pltpu.CompilerParams(has_side_effects=pltpu.SideEffectType.DATAFLOW_SIDE_EFFECTING),
    )(*[pltpu.with_memory_space_constraint(a, pltpu.HBM) for a in bufs])
    return res[0], res[1], list(res[2:2 + n]), list(res[2 + n:2 + 2 * n]), res[-1]


def _chips_wait(name, send_sems, recv_sems, sums, lands, after):
    n = len(sums)

    def body(*refs):
        ins, zones, send, recv = refs[:n], refs[n:2 * n], refs[2 * n], refs[2 * n + 1]
        for i in range(n):
            for k, peer in enumerate(_chip_peers()):
                cp = pltpu.make_async_remote_copy(
                    src_ref=ins[i].at[0], dst_ref=zones[i].at[k], send_sem=send.at[3 * i + k],
                    recv_sem=recv.at[3 * i + k],
                    device_id=peer, device_id_type=MESH)
                cp.wait_send()
                cp.wait_recv()

    bufs = list(sums) + list(lands)
    res = pl.pallas_call(
        body, name=name, in_specs=[HBM] * (2 * n) + [SEM, SEM, ANY], out_specs=[HBM] * (2 * n),
        out_shape=[pltpu.HBM(a.shape, a.dtype) for a in bufs], input_output_aliases={i: i for i in range(2 * n)},
        compiler_params=pltpu.CompilerParams(has_side_effects=pltpu.SideEffectType.DATAFLOW_SIDE_EFFECTING),
    )(*bufs, send_sems, recv_sems, after)
    return list(res[:n]), list(res[n:])


def _call(name, body, grid, in_specs, out_specs, out_shape, args, comm=(), scratch=()):
    comm = [p for p in comm if p is not None]
    single = not isinstance(out_shape, (list, tuple))
    o_specs = [out_specs] if single else list(out_specs)
    o_shape = [out_shape] if single else list(out_shape)
    n_in, n_out = len(in_specs), len(o_specs)
    c_in = [a for p in comm for a in p.inputs]
    c_out = [s for p in comm for s in p.out_shapes]
    c_sem = [s for p in comm for s in p.sem_shapes]

    def wrapped(*refs):
        ins, outs = refs[:n_in], refs[n_in + len(c_in):n_in + len(c_in) + n_out]
        pos = [n_in, n_in + len(c_in) + n_out, n_in + len(c_in) + n_out + len(c_out)]
        own = refs[pos[2]:pos[2] + len(scratch)]
        pos[2] += len(scratch)
        split = []
        for p in comm:
            sizes = [len(p.inputs), len(p.out_shapes), len(p.sem_shapes)]
            split.append([refs[o:o + n] for o, n in zip(pos, sizes)])
            pos = [o + n for o, n in zip(pos, sizes)]
        step, steps = 0, 1
        for a, g in enumerate(grid):
            step, steps = step * g + pl.program_id(a), steps * g

        def run(which, at):
            def go():
                for p, cut in zip(comm, split):
                    getattr(p, which)(*cut)
            if not comm:
                return
            if grid:
                pl.when(step == at)(go)
            else:
                go()

        run("start", 0)
        body(*ins, *outs, *own)
        run("mid", steps // 2)
        run("late", max(steps // 2, steps - 1 - max(1, steps // 8)))
        run("finish", steps - 1)

    res = pl.pallas_call(
        wrapped, name=name, grid=grid, in_specs=list(in_specs) + [ANY] * len(c_in),
        out_specs=o_specs + [ANY] * len(c_out), out_shape=o_shape + c_out, scratch_shapes=list(scratch) + c_sem,
        compiler_params=_params(len(grid)),
    )(*args, *c_in)
    pos = n_out
    for p in comm:
        p.results = list(res[pos:pos + len(p.out_shapes)])
        pos += len(p.out_shapes)
    return res[0] if single else list(res[:n_out])


def _comm_only(name, comm):
    def body(o_ref):
        o_ref[...] = jnp.zeros_like(o_ref)

    _call(name, body, (), [], pl.BlockSpec(memory_space=pltpu.VMEM), jax.ShapeDtypeStruct((8, 128), F32), [], comm)


def _mm(name, grid, prods, extras, outs, epi, nacc=1, comm=()):
    n_p, n_e = len(prods), len(extras)

    def body(*refs):
        ab, ex, out = refs[:2 * n_p], refs[2 * n_p:2 * n_p + n_e], refs[2 * n_p + n_e:]
        accs = [None] * nacc
        for i, prod in enumerate(prods):
            dn, acc, loop = prod[6], prod[7], prod[8]
            a_ref, b_ref = ab[2 * i], ab[2 * i + 1]
            if loop:
                for g in range(loop):
                    t = lax.dot_general(a_ref[g], b_ref[g], _DN[dn], preferred_element_type=F32)
                    accs[acc] = t if accs[acc] is None else accs[acc] + t
            else:
                t = lax.dot_general(a_ref[...], b_ref[...], _DN[dn], preferred_element_type=F32)
                accs[acc] = t if accs[acc] is None else accs[acc] + t
        epi(accs, ex, out)

    in_specs, args = [], []
    for prod in prods:
        in_specs += [pl.BlockSpec(prod[1], prod[2]), pl.BlockSpec(prod[4], prod[5])]
        args += [prod[0], prod[3]]
    for e, e_blk, e_map in extras:
        in_specs.append(pl.BlockSpec(e_blk, e_map))
        args.append(e)
    return _call(name, body, grid, in_specs, [pl.BlockSpec(blk, mp) for _, _, blk, mp in outs],
                 [jax.ShapeDtypeStruct(s, d) for s, d, _, _ in outs], args, comm)


def _store(accs, ex, out):
    out[0][...] = accs[0].astype(out[0].dtype)


def _ew_tile(r, c, budget=3 << 19):
    for t in range(r - r % 16, 0, -16):
        if r % t == 0 and t * c * 4 <= budget:
            return t, c
    for t in range(c - c % 128, 0, -128):
        if c % t == 0 and r * t * 4 <= budget:
            return r, t
    return r, c


def _tile(n, want):
    t = min(n, want)
    assert n % t == 0, (n, want)
    return t


def _mm_nt(name, a, bt, out_dtype, tm=512, tn=512, comm=(), rows=None):
    m, k = a.shape
    n = rows or bt.shape[0]
    tm, tn = _tile(m, tm), (tn if n % tn == 0 else n)
    return _mm(name, (n // tn, m // tm),
               [(a, (tm, k), lambda j, i: (i, 0), bt, (tn, k), lambda j, i: (j, 0), "nt", 0, 0)], [],
               [((m, n), out_dtype, (tm, tn), lambda j, i: (i, j))], _store, comm=comm)[0]


def _mm_tn_into(name, a, b, buf, row0, ta=1024, tb=512):
    t, ka = a.shape
    nb = b.shape[1]
    ta, tb = (ta if ka % ta == 0 else ka), (tb if nb % tb == 0 else nb)

    def body(a_ref, b_ref, buf_in, buf_out, tile, sem):
        i, j = pl.program_id(0), pl.program_id(1)
        tile[...] = lax.dot_general(a_ref[...], b_ref[...], _DN["tn"], preferred_element_type=F32).astype(tile.dtype)
        rows = pl.ds(pl.multiple_of(row0 + i * ta, 16), ta)
        cp = pltpu.make_async_copy(tile, buf_out.at[rows, pl.ds(pl.multiple_of(j * tb, 128), tb)], sem)
        cp.start()
        cp.wait()

    return pl.pallas_call(
        body, name=name, grid=(ka // ta, nb // tb),
        in_specs=[pl.BlockSpec((t, ta), lambda i, j: (0, i)), pl.BlockSpec((t, tb), lambda i, j: (0, j)), ANY],
        out_specs=ANY, out_shape=jax.ShapeDtypeStruct(buf.shape, buf.dtype), input_output_aliases={2: 0},
        scratch_shapes=[pltpu.VMEM((ta, tb), buf.dtype), pltpu.SemaphoreType.DMA],
        compiler_params=_params(2))(a, b, buf)


def _rms_fwd(name, x, g, tm=256, comm=()):
    s, d = x.shape
    tm = _tile(s, tm)

    def body(x_ref, g_ref, o_ref):
        v = x_ref[...]
        o_ref[...] = (v * lax.rsqrt(jnp.mean(v * v, axis=-1, keepdims=True) + EPS) * g_ref[...]).astype(o_ref.dtype)

    return _call(name, body, (s // tm,),
                 [pl.BlockSpec((tm, d), lambda i: (i, 0)), pl.BlockSpec((1, d), lambda i: (0, 0))],
                 pl.BlockSpec((tm, d), lambda i: (i, 0)), jax.ShapeDtypeStruct((s, d), BF), [x, g], comm)


def _acc_rows(ref, part, i):
    @pl.when(i == 0)
    def _():
        ref[...] = part

    @pl.when(i > 0)
    def _():
        ref[...] += part


def _rms_bwd_math(dn, v, g):
    rstd = lax.rsqrt(jnp.mean(v * v, axis=-1, keepdims=True) + EPS)
    xh = v * rstd
    dxh = dn * g
    dx = rstd * (dxh - xh * jnp.mean(dxh * xh, axis=-1, keepdims=True))
    return dx, jnp.sum(dn * xh, axis=0, keepdims=True)


def _rms_bwd(name, dn, x, g, resid, tm=256, comm=()):
    s, d = x.shape
    tm = _tile(s, tm)

    def body(dn_ref, x_ref, g_ref, r_ref, dx_ref, dxb_ref, dg_ref):
        dx, part = _rms_bwd_math(dn_ref[...].astype(F32), x_ref[...], g_ref[...])
        tot = r_ref[...] + dx
        dx_ref[...] = tot
        dxb_ref[...] = tot.astype(BF)
        _acc_rows(dg_ref, part, pl.program_id(0))

    row = pl.BlockSpec((tm, d), lambda i: (i, 0))
    one = pl.BlockSpec((1, d), lambda i: (0, 0))
    return _call(name, body, (s // tm,), [row, row, one, row], [row, row, one],
                 [jax.ShapeDtypeStruct((s, d), F32), jax.ShapeDtypeStruct((s, d), BF),
                  jax.ShapeDtypeStruct((1, d), F32)], [dn, x, g, resid], comm)


def _loss_head(h, target, g, tm=256):
    s, d = h.shape
    tm = _tile(s, tm)

    def body(h_ref, t_ref, g_ref, dh_ref, dg_ref, loss_ref):
        v, gv = h_ref[...], g_ref[...]
        rstd = lax.rsqrt(jnp.mean(v * v, axis=-1, keepdims=True) + EPS)
        xh = v * rstd
        err = xh * gv - t_ref[...]
        part_loss = 0.5 * jnp.sum(jnp.mean(err * err, axis=-1, keepdims=True), axis=0, keepdims=True)
        dy = err * (1.0 / d)
        dxh = dy * gv
        dh_ref[...] = rstd * (dxh - xh * jnp.mean(dxh * xh, axis=-1, keepdims=True))
        i = pl.program_id(0)
        _acc_rows(dg_ref, jnp.sum(dy * xh, axis=0, keepdims=True), i)
        _acc_rows(loss_ref, jnp.broadcast_to(part_loss, loss_ref.shape), i)

    row = pl.BlockSpec((tm, d), lambda i: (i, 0))
    one = pl.BlockSpec((1, d), lambda i: (0, 0))
    return pl.pallas_call(
        body, name="loss_head", grid=(s // tm,), in_specs=[row, row, one],
        out_specs=[row, one, pl.BlockSpec((1, 128), lambda i: (0, 0))],
        out_shape=[jax.ShapeDtypeStruct((s, d), F32), jax.ShapeDtypeStruct((1, d), F32),
                   jax.ShapeDtypeStruct((1, 128), F32)],
        compiler_params=_params(1))(h, target, g)


def _pl_bwd_elem(dh, pe, t, tm=256):
    s, d = dh.shape
    tm = _tile(s, tm)

    def body(dh_ref, pe_ref, t_ref, dt_ref, dpe_ref):
        dh_v, sg = dh_ref[...], _sig(t_ref[...])
        dt_ref[...] = (dh_v * pe_ref[...].astype(F32) * sg * (1.0 - sg)).astype(BF)
        dpe_ref[...] = (dh_v * sg).astype(BF)

    row = pl.BlockSpec((tm, d), lambda i: (i, 0))
    return pl.pallas_call(
        body, name="pl_bwd_elem", grid=(s // tm,), in_specs=[row, row, row], out_specs=[row, row],
        out_shape=[jax.ShapeDtypeStruct((s, d), BF)] * 2, compiler_params=_params(1))(dh, pe, t)


def _ffn_up(name, xn, wg, wu, tm=1024, comm=()):
    s, d = xn.shape
    g, fb, _ = wg.shape
    tm = _tile(s, tm)

    def epi(accs, ex, out):
        hg, hu = accs
        out[0][...] = hg.astype(BF)
        out[1][...] = hu.astype(BF)
        out[2][...] = (hg * _sig(hg) * hu).astype(BF)

    a_map = lambda j, i: (i, 0)
    w_map = lambda j, i: (j, 0, 0)
    o = ((g, s, fb), BF, (None, tm, fb), lambda j, i: (j, i, 0))
    return _mm(name, (g, s // tm),
               [(xn, (tm, d), a_map, wg, (None, fb, d), w_map, "nt", 0, 0),
                (xn, (tm, d), a_map, wu, (None, fb, d), w_map, "nt", 1, 0)], [], [o, o, o], epi, nacc=2, comm=comm)


def _ffn_down(name, a, wd, resid, tm=1024, tn=512, comm=()):
    g, s, fb = a.shape
    d = wd.shape[2]
    tm, tn = _tile(s, tm), _tile(d, tn)

    def epi(accs, ex, out):
        out[0][...] = ex[0][...] + 0.5 * accs[0]

    return _mm(name, (d // tn, s // tm),
               [(a, (g, tm, fb), lambda j, i: (0, i, 0), wd, (g, fb, tn), lambda j, i: (0, 0, j), "nn", 0, g)],
               [(resid, (tm, tn), lambda j, i: (i, j))],
               [((s, d), F32, (tm, tn), lambda j, i: (i, j))], epi, comm=comm)[0]


def _ffn_bwd_act(name, dh, wd, hg, hu, tm=1024, comm=()):
    s, d = dh.shape
    g, fb, _ = wd.shape
    tm = _tile(s, tm)

    def epi(accs, ex, out):
        da = 0.5 * accs[0]
        hg_v, hu_v = ex[0][...].astype(F32), ex[1][...].astype(F32)
        sg = _sig(hg_v)
        out[0][...] = (da * hu_v * (sg * (1.0 + hg_v * (1.0 - sg)))).astype(BF)
        out[1][...] = (da * (hg_v * sg)).astype(BF)

    blk = (None, tm, fb)
    gmap = lambda j, i: (j, i, 0)
    return _mm(name, (g, s // tm),
               [(dh, (tm, d), lambda j, i: (i, 0), wd, (None, fb, d), lambda j, i: (j, 0, 0), "nt", 0, 0)],
               [(hg, blk, gmap), (hu, blk, gmap)],
               [((g, s, fb), BF, blk, gmap), ((g, s, fb), BF, blk, gmap)], epi, comm=comm)


def _ffn_bwd_wd(name, a, dh, tn=1024, comm=()):
    g, s, fb = a.shape
    d = dh.shape[1]
    tn = _tile(d, tn)

    def epi(accs, ex, out):
        out[0][...] = (0.5 * accs[0]).astype(BF)

    return _mm(name, (g, d // tn),
               [(a, (None, s, fb), lambda j, i: (j, 0, 0), dh, (s, tn), lambda j, i: (0, i), "tn", 0, 0)], [],
               [((g, fb, d), BF, (None, fb, tn), lambda j, i: (j, 0, i))], epi, comm=comm)[0]


def _ffn_bwd_wup(name, xn, dhg, dhu, tk=1024, comm=()):
    s, d = xn.shape
    g, _, fb = dhg.shape
    tk = _tile(d, tk)

    def epi(accs, ex, out):
        out[0][...] = accs[0].astype(BF)
        out[1][...] = accs[1].astype(BF)

    a_map = lambda j, i: (j, 0, 0)
    b_map = lambda j, i: (0, i)
    o = ((g, fb, d), BF, (None, fb, tk), lambda j, i: (j, 0, i))
    return _mm(name, (g, d // tk),
               [(dhg, (None, s, fb), a_map, xn, (s, tk), b_map, "tn", 0, 0),
                (dhu, (None, s, fb), a_map, xn, (s, tk), b_map, "tn", 1, 0)], [], [o, o], epi, nacc=2, comm=comm)


def _ffn_bwd_x(name, dhg, dhu, wg, wu, tm=512, tn=512, comm=()):
    g, s, fb = dhg.shape
    d = wg.shape[2]
    tm, tn = _tile(s, tm), _tile(d, tn)
    a_blk, a_map = (g, tm, fb), lambda j, i: (0, i, 0)
    b_blk, b_map = (g, fb, tn), lambda j, i: (0, 0, j)
    return _mm(name, (d // tn, s // tm),
               [(dhg, a_blk, a_map, wg, b_blk, b_map, "nn", 0, g), (dhu, a_blk, a_map, wu, b_blk, b_map, "nn", 0, g)],
               [], [((s, d), F32, (tm, tn), lambda j, i: (i, j))], _store, comm=comm)[0]


def _ffn_forward(tag, h, gain, get_wgu, get_wd, norm_comm=(), up_comm=(), down_comm=()):
    xn = _rms_fwd(tag + "_norm", h, gain, comm=norm_comm)
    wg, wu = get_wgu()
    hg, hu, a = _ffn_up(tag + "_up", xn, wg, wu, comm=up_comm)
    return _ffn_down(tag + "_down", a, get_wd(), h, comm=down_comm), (xn, hg, hu, a)


def _na_geometry(rows):
    kh = min(NA_ROWS_WIN, rows)
    cols = np.arange(GRID_W)
    col_start = np.clip(cols - NA_COLS_WIN // 2, 0, GRID_W - NA_COLS_WIN)
    mask = (cols[None, :] >= col_start[:, None]) & (cols[None, :] < col_start[:, None] + NA_COLS_WIN)
    dc = np.clip(cols[None, :] - cols[:, None], -(NA_COLS_WIN - 1), NA_COLS_WIN - 1) + (NA_COLS_WIN - 1)
    return kh, mask, dc


def _na_table(rpb, rows):
    _, mask, dc = _na_geometry(rows)
    nd, nc, cells = 2 * NA_ROWS_WIN - 1, 2 * NA_COLS_WIN - 1, GRID_W * GRID_W
    onehot = np.zeros((128, cells), np.float32)
    onehot[dc.reshape(-1), np.arange(cells)] = mask.reshape(-1).astype(np.float32)
    off = np.where(mask.reshape(1, -1), 0.0, NEG).astype(np.float32)

    def body(r_ref, e_ref, off_ref, o_ref):
        o_ref[...] = jnp.dot(r_ref[...], e_ref[...], precision=HI, preferred_element_type=F32) + off_ref[...]

    flat = pl.pallas_call(body, name="na_table", out_shape=jax.ShapeDtypeStruct((NA_HEADS * nd, cells), F32),
                          compiler_params=_params(0))(
        jnp.pad(rpb.reshape(NA_HEADS * nd, nc), ((0, 0), (0, 128 - nc))), jnp.asarray(onehot), jnp.asarray(off))
    return flat.reshape(NA_HEADS, nd, GRID_W, GRID_W)


class _NaPlan:
    def __init__(self, s):
        self.s, self.rows = s, s // GRID_W
        self.kh = min(NA_ROWS_WIN, self.rows)
        self.qr = min(NA_QROWS, self.rows)
        self.kr = min(self.rows, self.kh + self.qr - 1)
        self.groups = self.rows // self.qr
        self.nd = 2 * NA_ROWS_WIN - 1
        self.hw, self.nq = NA_HG * NA_DIM, NA_HEADS // NA_HG
        clip = lambda v, hi: min(max(v, 0), hi)
        pats = [(clip(g * self.qr - self.kh // 2, self.rows - self.kr) - g * self.qr,)
                + tuple(clip(g * self.qr + a - self.kh // 2, self.rows - self.kh) - g * self.qr for a in range(self.qr))
                for g in range(self.groups)]
        self.rebuild = [g for g in range(self.groups) if g == 0 or pats[g] != pats[g - 1]]

    def first_key_row(self, g):
        return jnp.clip(g * self.qr - self.kh // 2, 0, self.rows - self.kr)

    def specs(self):
        blk = pl.BlockSpec((self.qr * GRID_W, self.hw), lambda j, g: (g, j))
        k_spec = pl.BlockSpec((self.s, self.hw), lambda j, g: (0, self.nq + j))
        v_spec = pl.BlockSpec((self.s, self.hw), lambda j, g: (0, 2 * self.nq + j))
        t_spec = pl.BlockSpec((NA_HG, self.nd, GRID_W, GRID_W), lambda j, g: (j, 0, 0, 0))
        return blk, k_spec, v_spec, t_spec

    def bias_scratch(self):
        return pltpu.VMEM((NA_HG, self.qr * GRID_W, self.kr * GRID_W), F32)

    def fill_bias(self, t_ref, bias_ref, g):
        def build():
            r0, ks = g * self.qr, self.first_key_row(g)
            for a in range(self.qr):
                rs = jnp.clip(r0 + a - self.kh // 2, 0, self.rows - self.kh)
                for i in range(self.kr):
                    valid = jnp.logical_and(ks + i >= rs, ks + i < rs + self.kh)
                    idx = jnp.clip(ks + i - r0 - a + NA_ROWS_WIN - 1, 0, self.nd - 1)
                    for h in range(NA_HG):
                        bias_ref[h, a * GRID_W:(a + 1) * GRID_W, i * GRID_W:(i + 1) * GRID_W] = jnp.where(
                            valid, t_ref[h, idx], NEG)

        pl.when(functools.reduce(jnp.logical_or, [g == r for r in self.rebuild]))(build)

    def window(self, g):
        return pl.ds(pl.multiple_of(self.first_key_row(g) * GRID_W, GRID_W), self.kr * GRID_W)


def _na_probs(q, k, bias):
    sc = lax.dot_general(q, k, _DN["nt"], preferred_element_type=F32) * (NA_DIM ** -0.5) + bias
    e = jnp.exp(sc - jnp.max(sc, axis=-1, keepdims=True))
    return e / jnp.sum(e, axis=-1, keepdims=True)


def _na_fwd(qkv, table, comm=()):
    plan = _NaPlan(qkv.shape[0])
    blk, k_spec, v_spec, t_spec = plan.specs()

    def body(q_ref, k_ref, v_ref, t_ref, o_ref, bias_ref):
        g = pl.program_id(1)
        plan.fill_bias(t_ref, bias_ref, g)
        win = plan.window(g)
        for h in range(NA_HG):
            cs = slice(h * NA_DIM, (h + 1) * NA_DIM)
            p = _na_probs(q_ref[:, cs], k_ref[win, cs], bias_ref[h])
            o_ref[:, cs] = jnp.dot(p.astype(BF), v_ref[win, cs], preferred_element_type=F32).astype(BF)

    return _call("na_fwd", body, (plan.nq, plan.groups), [blk, k_spec, v_spec, t_spec], blk,
                 jax.ShapeDtypeStruct((plan.s, NA_HEADS * NA_DIM), BF), [qkv, qkv, qkv, table], comm,
                 scratch=[plan.bias_scratch()])


def _na_bwd(qkv, table, do, comm=()):
    plan = _NaPlan(qkv.shape[0])
    blk, k_spec, v_spec, t_spec = plan.specs()
    qr, kr = plan.qr, plan.kr

    def body(q_ref, k_ref, v_ref, t_ref, do_ref, dq_ref, dk_ref, dv_ref, dt_ref, bias_ref):
        g = pl.program_id(1)

        @pl.when(g == 0)
        def _():
            dk_ref[...] = jnp.zeros_like(dk_ref)
            dv_ref[...] = jnp.zeros_like(dv_ref)
            dt_ref[...] = jnp.zeros_like(dt_ref)

        plan.fill_bias(t_ref, bias_ref, g)
        win = plan.window(g)
        base = plan.first_key_row(g) - g * qr + NA_ROWS_WIN - 1
        for h in range(NA_HG):
            cs = slice(h * NA_DIM, (h + 1) * NA_DIM)
            q, k, v, do_h = q_ref[:, cs], k_ref[win, cs], v_ref[win, cs], do_ref[:, cs]
            p = _na_probs(q, k, bias_ref[h])
            dp = lax.dot_general(do_h, v, _DN["nt"], preferred_element_type=F32)
            ds = p * (dp - jnp.sum(p * dp, axis=-1, keepdims=True))
            for dlt in range(1 - qr, kr):
                tiles = [ds[a * GRID_W:(a + 1) * GRID_W, (a + dlt) * GRID_W:(a + dlt + 1) * GRID_W]
                         for a in range(qr) if 0 <= a + dlt < kr]
                dt_ref[h, jnp.clip(base + dlt, 0, plan.nd - 1)] += functools.reduce(jnp.add, tiles)
            dsb = (ds * (NA_DIM ** -0.5)).astype(BF)
            dq_ref[:, cs] = jnp.dot(dsb, k, preferred_element_type=F32).astype(BF)
            dk_ref[win, cs] += lax.dot_general(dsb, q, _DN["tn"], preferred_element_type=F32)
            dv_ref[win, cs] += lax.dot_general(p.astype(BF), do_h, _DN["tn"], preferred_element_type=F32)

    width = NA_HEADS * NA_DIM
    whole = pl.BlockSpec((plan.s, plan.hw), lambda j, g: (0, j))
    return _call(
        "na_bwd", body, (plan.nq, plan.groups), [blk, k_spec, v_spec, t_spec, blk], [blk, whole, whole, t_spec],
        [jax.ShapeDtypeStruct((plan.s, width), BF), jax.ShapeDtypeStruct((plan.s, width), F32),
         jax.ShapeDtypeStruct((plan.s, width), F32),
         jax.ShapeDtypeStruct((NA_HEADS, plan.nd, GRID_W, GRID_W), F32)],
        [qkv, qkv, qkv, table, do], comm, scratch=[plan.bias_scratch()])


def _na_rpb_grad(dt, rows):
    _, mask, dc = _na_geometry(rows)
    nd, nc = 2 * NA_ROWS_WIN - 1, 2 * NA_COLS_WIN - 1
    onehot = np.zeros((GRID_W * GRID_W, 128), np.float32)
    onehot[np.arange(GRID_W * GRID_W), dc.reshape(-1)] = mask.reshape(-1).astype(np.float32)
    flat = dt.reshape(NA_HEADS * nd, GRID_W * GRID_W)

    def body(a_ref, e_ref, o_ref):
        o_ref[...] = jnp.dot(a_ref[...], e_ref[...], precision=HI, preferred_element_type=F32)

    out = pl.pallas_call(body, name="na_rpb_grad", out_shape=jax.ShapeDtypeStruct((NA_HEADS * nd, 128), F32),
                         compiler_params=_params(0))(flat, jnp.asarray(onehot))
    return out[:, :nc].reshape(NA_HEADS, nd, nc)


def _rope_consts(s):
    pos = np.arange(s, dtype=np.float32)
    inv = (1.0 / (ROPE_THETA ** (np.arange(0, ML_ROPE, 2, dtype=np.float32) / ML_ROPE))).astype(np.float32)
    ang = pos[:, None] * inv[None, :]
    cos, sin = np.cos(ang).astype(np.float32), np.sin(ang).astype(np.float32)
    half = ML_ROPE // 2
    rot = np.zeros((ML_ROPE, ML_ROPE), np.float32)
    rot[np.arange(half) + half, np.arange(half)] = -1.0
    rot[np.arange(half), np.arange(half) + half] = 1.0
    return (jnp.asarray(np.concatenate([cos, cos], 1)), jnp.asarray(np.concatenate([sin, sin], 1)),
            jnp.asarray(rot), jnp.asarray(rot.T.copy()))


def _rope(v, cos, sin, rot):
    return v * cos + jnp.dot(v, rot, precision=HI, preferred_element_type=F32) * sin


def _unrope(dv, cos, sin, rot_t):
    return dv * cos + jnp.dot(dv * sin, rot_t, precision=HI, preferred_element_type=F32)


def _rms(v, g):
    return v * lax.rsqrt(jnp.mean(v * v, axis=-1, keepdims=True) + EPS) * g


def _mla_prep(lat, gq, gkv, cos, sin, rot, tm=256):
    s, w = lat.shape
    tm = _tile(s, tm)

    def body(l_ref, gq_ref, gkv_ref, c_ref, s_ref, r_ref, cq_ref, ckv_ref, kr_ref):
        cq_ref[...] = _rms(l_ref[:, :ML_RANK], gq_ref[...]).astype(BF)
        ckv_ref[...] = _rms(l_ref[:, ML_RANK:2 * ML_RANK], gkv_ref[...]).astype(BF)
        kr_ref[...] = _rope(l_ref[:, 2 * ML_RANK:], c_ref[...], s_ref[...], r_ref[...]).astype(BF)

    row = lambda c: pl.BlockSpec((tm, c), lambda i: (i, 0))
    full = lambda a: pl.BlockSpec(a.shape, lambda i: (0, 0))
    return pl.pallas_call(
        body, name="mla_prep", grid=(s // tm,),
        in_specs=[row(w), full(gq), full(gkv), row(ML_ROPE), row(ML_ROPE), full(rot)],
        out_specs=[row(ML_RANK), row(ML_RANK), row(ML_ROPE)],
        out_shape=[jax.ShapeDtypeStruct((s, ML_RANK), BF), jax.ShapeDtypeStruct((s, ML_RANK), BF),
                   jax.ShapeDtypeStruct((s, ML_ROPE), BF)],
        compiler_params=_params(1))(lat, gq, gkv, cos, sin, rot)


def _mla_q_proj(cq, wuq, cos, sin, rot, tm=512, comm=()):
    s, k = cq.shape
    tm = _tile(s, tm)

    def epi(accs, ex, out):
        acc = accs[0]
        out[0][:, :ML_NOPE] = acc[:, :ML_NOPE].astype(BF)
        out[0][:, ML_NOPE:] = _rope(acc[:, ML_NOPE:], ex[0][...], ex[1][...], ex[2][...]).astype(BF)

    rmap = lambda j, i: (i, 0)
    return _mm("mla_q_proj", (ML_HEADS, s // tm),
               [(cq, (tm, k), rmap, wuq, (None, ML_QK, k), lambda j, i: (j, 0, 0), "nt", 0, 0)],
               [(cos, (tm, ML_ROPE), rmap), (sin, (tm, ML_ROPE), rmap), (rot, rot.shape, lambda j, i: (0, 0))],
               [((ML_HEADS, s, ML_QK), BF, (None, tm, ML_QK), lambda j, i: (j, i, 0))], epi, comm=comm)[0]


def _mla_kv_proj(ckv, wukv, kr, tm=512, comm=()):
    s, k = ckv.shape
    tm = _tile(s, tm)

    def epi(accs, ex, out):
        acc = accs[0]
        out[0][:, :ML_NOPE] = acc[:, :ML_NOPE].astype(BF)
        out[0][:, ML_NOPE:] = ex[0][...]
        out[1][...] = acc[:, ML_NOPE:].astype(BF)

    rmap = lambda j, i: (i, 0)
    gmap = lambda j, i: (j, i, 0)
    return _mm("mla_kv_proj", (ML_HEADS, s // tm),
               [(ckv, (tm, k), rmap, wukv, (None, k, ML_NOPE + ML_V), lambda j, i: (j, 0, 0), "nn", 0, 0)],
               [(kr, (tm, ML_ROPE), rmap)],
               [((ML_HEADS, s, ML_QK), BF, (None, tm, ML_QK), gmap), ((ML_HEADS, s, ML_V), BF, (None, tm, ML_V), gmap)],
               epi, comm=comm)


def _mla_probs(q, k):
    sc = lax.dot_general(q, k, _DN["nt"], preferred_element_type=F32) * (ML_QK ** -0.5)
    e = jnp.exp(sc - jnp.max(sc, axis=-1, keepdims=True))
    return e / jnp.sum(e, axis=-1, keepdims=True)


def _mla_fwd(q, k, v, tq=1024, comm=()):
    _, s, _ = q.shape
    tq = _tile(s, tq)

    def body(q_ref, k_ref, v_ref, o_ref):
        p = _mla_probs(q_ref[...], k_ref[...])
        o_ref[...] = jnp.dot(p.astype(BF), v_ref[...], preferred_element_type=F32).astype(BF)

    return _call("mla_fwd", body, (ML_HEADS, s // tq),
                 [pl.BlockSpec((None, tq, ML_QK), lambda h, i: (h, i, 0)),
                  pl.BlockSpec((None, s, ML_QK), lambda h, i: (h, 0, 0)),
                  pl.BlockSpec((None, s, ML_V), lambda h, i: (h, 0, 0))],
                 pl.BlockSpec((tq, ML_V), lambda h, i: (i, h)),
                 jax.ShapeDtypeStruct((s, ML_HEADS * ML_V), BF), [q, k, v], comm)


def _mla_bwd(q, k, v, do, tq=1024, comm=()):
    _, s, _ = q.shape
    tq = _tile(s, tq)

    def body(q_ref, k_ref, v_ref, do_ref, dq_ref, dk_ref, dv_ref):
        i = pl.program_id(1)
        qv, kv, vv, dov = q_ref[...], k_ref[...], v_ref[...], do_ref[...]
        p = _mla_probs(qv, kv)
        dp = lax.dot_general(dov, vv, _DN["nt"], preferred_element_type=F32)
        ds = (p * (dp - jnp.sum(p * dp, axis=-1, keepdims=True)) * (ML_QK ** -0.5)).astype(BF)
        dq_ref[...] = jnp.dot(ds, kv, preferred_element_type=F32)
        _acc_rows(dk_ref, lax.dot_general(ds, qv, _DN["tn"], preferred_element_type=F32), i)
        _acc_rows(dv_ref, lax.dot_general(p.astype(BF), dov, _DN["tn"], preferred_element_type=F32), i)

    return _call(
        "mla_bwd", body, (ML_HEADS, s // tq),
        [pl.BlockSpec((None, tq, ML_QK), lambda h, i: (h, i, 0)),
         pl.BlockSpec((None, s, ML_QK), lambda h, i: (h, 0, 0)),
         pl.BlockSpec((None, s, ML_V), lambda h, i: (h, 0, 0)),
         pl.BlockSpec((tq, ML_V), lambda h, i: (i, h))],
        [pl.BlockSpec((None, tq, ML_QK), lambda h, i: (h, i, 0)),
         pl.BlockSpec((None, s, ML_QK), lambda h, i: (h, 0, 0)),
         pl.BlockSpec((None, s, ML_V), lambda h, i: (h, 0, 0))],
        [jax.ShapeDtypeStruct((ML_HEADS, s, ML_QK), F32), jax.ShapeDtypeStruct((ML_HEADS, s, ML_QK), F32),
         jax.ShapeDtypeStruct((ML_HEADS, s, ML_V), F32)],
        [q, k, v, do], comm)


def _mla_post(dq, dk, dv, cos, sin, rot_t, tm=1024):
    _, s, _ = dq.shape
    tm = _tile(s, tm)

    def body(dq_ref, dk_ref, dv_ref, c_ref, s_ref, r_ref, dqp_ref, dkv_ref, dkr_ref):
        h = pl.program_id(1)
        dqv, dkk = dq_ref[...], dk_ref[...]
        dqp_ref[:, :ML_NOPE] = dqv[:, :ML_NOPE].astype(BF)
        dqp_ref[:, ML_NOPE:] = _unrope(dqv[:, ML_NOPE:], c_ref[...], s_ref[...], r_ref[...]).astype(BF)
        dkv_ref[:, :ML_NOPE] = dkk[:, :ML_NOPE].astype(BF)
        dkv_ref[:, ML_NOPE:] = dv_ref[...].astype(BF)
        _acc_rows(dkr_ref, dkk[:, ML_NOPE:], h)

    gspec = lambda c: pl.BlockSpec((None, tm, c), lambda i, h: (h, i, 0))
    rspec = pl.BlockSpec((tm, ML_ROPE), lambda i, h: (i, 0))
    return pl.pallas_call(
        body, name="mla_post", grid=(s // tm, ML_HEADS),
        in_specs=[gspec(ML_QK), gspec(ML_QK), gspec(ML_V), rspec, rspec,
                  pl.BlockSpec(rot_t.shape, lambda i, h: (0, 0))],
        out_specs=[gspec(ML_QK), gspec(ML_NOPE + ML_V), rspec],
        out_shape=[jax.ShapeDtypeStruct((ML_HEADS, s, ML_QK), BF),
                   jax.ShapeDtypeStruct((ML_HEADS, s, ML_NOPE + ML_V), BF),
                   jax.ShapeDtypeStruct((s, ML_ROPE), F32)],
        compiler_params=_params(2))(dq, dk, dv, cos, sin, rot_t)


def _mla_lat_bwd(dcq, dckv, dkr, lat, gq, gkv, cos, sin, rot_t, tm=256):
    s, w = lat.shape
    tm = _tile(s, tm)

    def body(dcq_ref, dckv_ref, dkr_ref, l_ref, gq_ref, gkv_ref, c_ref, s_ref, r_ref, dl_ref, dgq_ref, dgkv_ref):
        i = pl.program_id(0)
        dql, pq = _rms_bwd_math(dcq_ref[...], l_ref[:, :ML_RANK], gq_ref[...])
        dkl, pkv = _rms_bwd_math(dckv_ref[...], l_ref[:, ML_RANK:2 * ML_RANK], gkv_ref[...])
        dl_ref[:, :ML_RANK] = dql.astype(BF)
        dl_ref[:, ML_RANK:2 * ML_RANK] = dkl.astype(BF)
        dl_ref[:, 2 * ML_RANK:] = _unrope(dkr_ref[...], c_ref[...], s_ref[...], r_ref[...]).astype(BF)
        _acc_rows(dgq_ref, pq, i)
        _acc_rows(dgkv_ref, pkv, i)

    row = lambda c: pl.BlockSpec((tm, c), lambda i: (i, 0))
    full = lambda a: pl.BlockSpec(a.shape, lambda i: (0, 0))
    return pl.pallas_call(
        body, name="mla_lat_bwd", grid=(s // tm,),
        in_specs=[row(ML_RANK), row(ML_RANK), row(ML_ROPE), row(w), full(gq), full(gkv), row(ML_ROPE), row(ML_ROPE),
                  full(rot_t)],
        out_specs=[row(w), full(gq), full(gkv)],
        out_shape=[jax.ShapeDtypeStruct((s, w), BF), jax.ShapeDtypeStruct(gq.shape, F32),
                   jax.ShapeDtypeStruct(gkv.shape, F32)],
        compiler_params=_params(1))(dcq, dckv, dkr, lat, gq, gkv, cos, sin, rot_t)


def _grp_dw(name, a, dout, ta=1024):
    s, k = a.shape
    ta = _tile(k, ta)
    if dout.ndim == 3:
        g, _, nb = dout.shape
        b_blk, b_map = (None, s, nb), lambda j, i: (j, 0, 0)
    else:
        g, nb = NDEV, dout.shape[1] // NDEV
        b_blk, b_map = (s, nb), lambda j, i: (0, j)
    return _mm(name, (g, k // ta),
               [(a, (s, ta), lambda j, i: (0, i), dout, b_blk, b_map, "tn", 0, 0)], [],
               [((g, k, nb), BF, (None, ta, nb), lambda j, i: (j, i, 0))], _store)[0]


def _grp_dw_t(name, dout, a, ta=512):
    g, s, nb = dout.shape
    k = a.shape[1]
    ta = _tile(k, ta)
    return _mm(name, (g, k // ta),
               [(dout, (None, s, nb), lambda j, i: (j, 0, 0), a, (s, ta), lambda j, i: (0, i), "tn", 0, 0)], [],
               [((g, nb, k), BF, (None, nb, ta), lambda j, i: (j, 0, i))], _store)[0]


def _grp_dx_t(name, dout, wt, tm=512, tn=512, comm=()):
    g, s, nb = dout.shape
    k = wt.shape[2]
    tm, tn = _tile(s, tm), _tile(k, tn)
    return _mm(name, (k // tn, s // tm),
               [(dout, (g, tm, nb), lambda j, i: (0, i, 0), wt, (g, nb, tn), lambda j, i: (0, 0, j), "nn", 0, g)], [],
               [((s, k), F32, (tm, tn), lambda j, i: (i, j))], _store, comm=comm)[0]


def _grp_dx(name, dout, w, tm=512, tn=512, out_dtype=F32, comm=()):
    g, s, nb = dout.shape
    k = w.shape[1]
    tm, tn = _tile(s, tm), _tile(k, tn)
    return _mm(name, (k // tn, s // tm),
               [(dout, (g, tm, nb), lambda j, i: (0, i, 0), w, (g, tn, nb), lambda j, i: (0, j, 0), "nt", 0, g)], [],
               [((s, k), out_dtype, (tm, tn), lambda j, i: (i, j))], _store, comm=comm)[0]


def _row_dw(name, a, dout, tn=2048):
    s, n = dout.shape
    tn = _tile(n, tn)
    if a.ndim == 3:
        kb = a.shape[2]
        a_blk, a_map = (None, s, kb), lambda j, i: (j, 0, 0)
    else:
        kb = a.shape[1] // NDEV
        a_blk, a_map = (s, kb), lambda j, i: (0, j)
    return _mm(name, (NDEV, n // tn),
               [(a, a_blk, a_map, dout, (s, tn), lambda j, i: (0, i), "tn", 0, 0)], [],
               [((NDEV, kb, n), BF, (None, kb, tn), lambda j, i: (j, 0, i))], _store)[0]


def _mix_merge(oa, ob, wa, wb, ga, gb, tm=1024, comm=()):
    s, k = oa.shape
    g, _, nb = wa.shape
    tm = _tile(s, tm)

    def epi(accs, ex, out):
        ya, yb = accs
        out[0][...] = ya.astype(BF)
        out[1][...] = yb.astype(BF)
        out[2][...] = (_sig(ex[0][...]) * ya + _sig(ex[1][...]) * yb).astype(BF)

    rmap = lambda j, i: (i, 0)
    wmap = lambda j, i: (j, 0, 0)
    o = ((g, s, nb), BF, (None, tm, nb), lambda j, i: (j, i, 0))
    cmap = lambda j, i: (i, j)
    return _mm("mix_merge", (g, s // tm),
               [(oa, (tm, k), rmap, wa, (None, k, nb), wmap, "nn", 0, 0),
                (ob, (tm, k), rmap, wb, (None, k, nb), wmap, "nn", 1, 0)],
               [(ga, (tm, nb), cmap), (gb, (tm, nb), cmap)], [o, o, o], epi, nacc=2, comm=comm)


def _mix_out(merged, wout, resid, tm=1024, tn=512):
    g, s, kb = merged.shape
    d = wout.shape[2]
    tm, tn = _tile(s, tm), _tile(d, tn)

    def epi(accs, ex, out):
        out[0][...] = ex[0][...] + accs[0]

    return _mm("mix_out", (d // tn, s // tm),
               [(merged, (g, tm, kb), lambda j, i: (0, i, 0), wout, (g, kb, tn), lambda j, i: (0, 0, j), "nn", 0, g)],
               [(resid, (tm, tn), lambda j, i: (i, j))],
               [((s, d), F32, (tm, tn), lambda j, i: (i, j))], epi)[0]


def _mix_out_bwd(dh, wout, ga, gb, ya, yb, tm=1024, comm=()):
    s, d = dh.shape
    g, kb, _ = wout.shape
    tm = _tile(s, tm)

    def epi(accs, ex, out):
        dm = accs[0]
        sa, sb = _sig(ex[0][...]), _sig(ex[1][...])
        out[0][...] = (dm * sa).astype(BF)
        out[1][...] = (dm * sb).astype(BF)
        out[2][...] = (dm * ex[2][...].astype(F32) * sa * (1.0 - sa)).astype(BF)
        out[3][...] = (dm * ex[3][...].astype(F32) * sb * (1.0 - sb)).astype(BF)

    cmap = lambda j, i: (i, j)
    gmap = lambda j, i: (j, i, 0)
    og = ((g, s, kb), BF, (None, tm, kb), gmap)
    oc = ((s, g * kb), BF, (tm, kb), cmap)
    return _mm("mix_out_bwd", (g, s // tm),
               [(dh, (tm, d), lambda j, i: (i, 0), wout, (None, kb, d), lambda j, i: (j, 0, 0), "nt", 0, 0)],
               [(ga, (tm, kb), cmap), (gb, (tm, kb), cmap), (ya, (None, tm, kb), gmap), (yb, (None, tm, kb), gmap)],
               [og, og, oc, oc], epi, comm=comm)


def _pl_forward(n4, wplg, p, wpl, h3, tm=1024):
    s, d = n4.shape
    g, kb, _ = wplg.shape
    kp, nb = wpl.shape[1], wpl.shape[2]
    tm = _tile(s, tm)
    wplg_nat = wplg.reshape(g * kb, d)

    def epi(accs, ex, out):
        t, pe = accs
        out[0][...] = ex[0][...] + _sig(t) * pe
        out[1][...] = t
        out[2][...] = pe.astype(BF)

    rmap = lambda j, i: (i, 0)
    cmap = lambda j, i: (i, j)
    return _mm("pl_forward", (g, s // tm),
               [(n4, (tm, d), rmap, wplg_nat, (g * kb, nb), lambda j, i: (0, j), "nn", 0, 0),
                (p, (tm, kp), rmap, wpl, (None, kp, nb), lambda j, i: (j, 0, 0), "nn", 1, 0)],
               [(h3, (tm, nb), cmap)],
               [((s, d), F32, (tm, nb), cmap), ((s, d), F32, (tm, nb), cmap), ((s, d), BF, (tm, nb), cmap)],
               epi, nacc=2)


def _row_dx(name, dout, w, tm=1024, comm=()):
    s, n = dout.shape
    g, kb, _ = w.shape
    tm = _tile(s, tm)
    return _mm(name, (g, s // tm),
               [(dout, (tm, n), lambda j, i: (i, 0), w, (None, kb, n), lambda j, i: (j, 0, 0), "nt", 0, 0)], [],
               [((s, g * kb), F32, (tm, kb), lambda j, i: (i, j))], _store, comm=comm)[0]


def _in_proj_bwd_x(pieces, weights, tm=512, tn=512, comm=()):
    s = pieces[0].shape[0]
    d = weights[0].shape[1]
    tm, tn = _tile(s, tm), _tile(d, tn)
    prods = [(pc, (tm, pc.shape[1]), lambda j, i: (i, 0), w, (pc.shape[1], tn), lambda j, i: (0, j), "nn", 0, 0)
             for pc, w in zip(pieces, weights)]
    return _mm("in_proj_dx", (d // tn, s // tm), prods, [],
               [((s, d), F32, (tm, tn), lambda j, i: (i, j))], _store, comm=comm)[0]


def _split_w_in(w_in_t):
    g, nb, d = w_in_t.shape
    nat = w_in_t.reshape(g * nb, d)
    na, lat = 3 * NA_HEADS * NA_DIM, 2 * ML_RANK + ML_ROPE
    return nat, nat[na:na + lat], nat[na + lat:na + lat + d], nat[na + lat + d:]


def _pair_sum(name, part, landed, core):
    _, _, r, c = part.shape
    tr, tc = _ew_tile(r, c)

    def body(core_ref, a_ref, b_ref, o_ref):
        o_ref[...] = (a_ref[...].astype(F32) + b_ref[...].astype(F32)).astype(o_ref.dtype)

    return pl.pallas_call(
        body, name=name,
        grid_spec=pltpu.PrefetchScalarGridSpec(
            num_scalar_prefetch=1, grid=(NCHIP, r // tr, c // tc),
            in_specs=[pl.BlockSpec((None, None, tr, tc), lambda j, i, k, core_ref: (j, core_ref[0], i, k)),
                      pl.BlockSpec((None, tr, tc), lambda j, i, k, core_ref: (j, i, k))],
            out_specs=pl.BlockSpec((None, tr, tc), lambda j, i, k, core_ref: (j, i, k))),
        out_shape=jax.ShapeDtypeStruct(landed.shape, landed.dtype), compiler_params=_params(3),
    )(core, part, landed)


def _device_step(x, p, target, sp, own, core):
    s, d = x.shape
    rows = s // GRID_W
    cos, sin, rot, rot_t = _rope_consts(s)
    w, dw4, sums, dsp, pending = {}, {}, {}, {}, []

    def gather(*names):
        return _GatherPart(names, [own[n] for n in names])

    def got(part):
        w.update(zip(part.names, part.results))

    def grad(name, g):
        dw4[name] = g.reshape((NCHIP, 2) + g.shape[1:])

    def to_sibling(*names):
        return _SiblingPart(names, [dw4[n] for n in names])

    def add_pairs(part):
        for n, landed in zip(part.names, part.results):
            sums[n] = _pair_sum("pair_sum_" + n, dw4[n], landed, core)

    def start_chips(tag, *names):
        send, recv, thru, lands, token = _chips_start("rs_start_" + tag, [sums[n] for n in names])
        pending.append((tag, names, send, recv, thru, lands))
        return token

    c0 = gather("ffn1_w_gate", "ffn1_w_up")
    c1 = gather("ffn1_w_down")
    c2 = gather("w_in")

    def ffn1_wgu():
        got(c0)
        return w["ffn1_w_gate"], w["ffn1_w_up"]

    def ffn1_wd():
        got(c1)
        return w["ffn1_w_down"]

    h1, ffn1_saved = _ffn_forward("ffn1", x, sp["ffn1_norm"], ffn1_wgu, ffn1_wd,
                                  norm_comm=[c0], up_comm=[c1], down_comm=[c2])
    got(c2)
    wqkv, wlat, wga, wgb = _split_w_in(w["w_in"])
    u = _rms_fwd("mix_norm", h1, sp["mix_norm"])
    c3 = gather("w_uq", "w_ukv")
    qkv = _mm_nt("in_qkv", u, wqkv, BF, tn=1024, comm=[c3], rows=3 * NA_HEADS * NA_DIM)
    got(c3)
    lat = _mm_nt("in_lat", u, wlat, F32, tm=1024)
    c3a = gather("w_branch_a")
    ga = _mm_nt("in_ga", u, wga, F32, tn=1024, comm=[c3a])
    got(c3a)
    c3b = gather("w_branch_b")
    gb = _mm_nt("in_gb", u, wgb, F32, tn=1024, comm=[c3b])
    got(c3b)
    tb = _na_table(sp["na_rpb"], rows)
    c4 = gather("ffn2_w_gate")
    oa = _na_fwd(qkv, tb, comm=[c4])
    got(c4)
    cq, ckv, kr = _mla_prep(lat, sp["q_a_norm"], sp["kv_a_norm"], cos, sin, rot)
    c4a = gather("w_out")
    qf = _mla_q_proj(cq, w["w_uq"], cos, sin, rot, comm=[c4a])
    got(c4a)
    kf, vf = _mla_kv_proj(ckv, w["w_ukv"], kr)
    c5 = gather("ffn2_w_up")
    ob = _mla_fwd(qf, kf, vf, comm=[c5])
    got(c5)
    c5a = gather("w_pl", "w_pl_gate")
    ya, yb, merged = _mix_merge(oa, ob, w["w_branch_a"], w["w_branch_b"], ga, gb, comm=[c5a])
    got(c5a)
    h2 = _mix_out(merged, w["w_out"], h1)
    c6 = gather("ffn2_w_down")

    def ffn2_wd():
        got(c6)
        return w["ffn2_w_down"]

    h3, ffn2_saved = _ffn_forward("ffn2", h2, sp["ffn2_norm"], lambda: (w["ffn2_w_gate"], w["ffn2_w_up"]), ffn2_wd,
                                  up_comm=[c6])
    n4 = _rms_fwd("pl_norm", h3, sp["pl_norm"])
    pb = p.astype(BF)
    h4, t, pe = _pl_forward(n4, w["w_pl_gate"], pb, w["w_pl"], h3)

    dh4, dsp["final_norm"], loss = _loss_head(h4, target, sp["final_norm"])
    dt, dpe = _pl_bwd_elem(dh4, pe, t)
    grad("w_pl", _grp_dw("pl_dw", pb, dpe))
    grad("w_pl_gate", _row_dw("plg_dw", n4, dt))
    s1 = to_sibling("w_pl", "w_pl_gate")
    dn4 = _row_dx("plg_dx", dt, w["w_pl_gate"], comm=[s1])
    add_pairs(s1)
    dh3, dhb, dsp["pl_norm"] = _rms_bwd("pl_dnorm", dn4, h3, sp["pl_norm"], dh4)

    xn, hg, hu, a = ffn2_saved
    grad("ffn2_w_down", _ffn_bwd_wd("ffn2_dwd", a, dhb))
    s2 = to_sibling("ffn2_w_down")
    dhg, dhu = _ffn_bwd_act("ffn2_dact", dhb, w["ffn2_w_down"], hg, hu, comm=[s2])
    add_pairs(s2)
    tok = start_chips("ffn2_down", "w_pl", "w_pl_gate", "ffn2_w_down")
    dwg, dwu = _ffn_bwd_wup("ffn2_dwup", xn, dhg, dhu, comm=[_After(tok)])
    grad("ffn2_w_gate", dwg)
    grad("ffn2_w_up", dwu)
    s3 = to_sibling("ffn2_w_gate", "ffn2_w_up")
    dxn = _ffn_bwd_x("ffn2_dx", dhg, dhu, w["ffn2_w_gate"], w["ffn2_w_up"], comm=[s3])
    add_pairs(s3)
    tok = start_chips("ffn2_up", "ffn2_w_gate", "ffn2_w_up")
    dh2, dh2b, dsp["ffn2_norm"] = _rms_bwd("ffn2_dnorm", dxn, h2, sp["ffn2_norm"], dh3, comm=[_After(tok)])

    grad("w_out", _row_dw("out_dw", merged, dh2b))
    s4 = to_sibling("w_out")
    dya, dyb, dga, dgb = _mix_out_bwd(dh2b, w["w_out"], ga, gb, ya, yb, comm=[s4])
    add_pairs(s4)
    grad("w_branch_a", _grp_dw("bra_dw", oa, dya))
    grad("w_branch_b", _grp_dw("brb_dw", ob, dyb))
    doa = _grp_dx("bra_dx", dya, w["w_branch_a"], out_dtype=BF)
    s5 = to_sibling("w_branch_a", "w_branch_b")
    dob = _grp_dx("brb_dx", dyb, w["w_branch_b"], out_dtype=BF, comm=[s5])
    add_pairs(s5)

    dqf, dkf, dvf = _mla_bwd(qf, kf, vf, dob)
    dqp, dkv, dkr = _mla_post(dqf, dkf, dvf, cos, sin, rot_t)
    grad("w_uq", _grp_dw_t("uq_dw", dqp, cq))
    grad("w_ukv", _grp_dw("ukv_dw", ckv, dkv))
    dcq = _grp_dx_t("uq_dx", dqp, w["w_uq"])
    s6 = to_sibling("w_uq", "w_ukv")
    dckv = _grp_dx("ukv_dx", dkv, w["w_ukv"], comm=[s6])
    add_pairs(s6)
    dlat, dsp["q_a_norm"], dsp["kv_a_norm"] = _mla_lat_bwd(dcq, dckv, dkr, lat, sp["q_a_norm"], sp["kv_a_norm"],
                                                         cos, sin, rot_t)
    dq_na, dk_na, dv_na, dtab = _na_bwd(qkv, tb, doa)
    dsp["na_rpb"] = _na_rpb_grad(dtab, rows)
    dqkv = jnp.concatenate([dq_na, dk_na.astype(BF), dv_na.astype(BF)], axis=1)

    pieces = [dqkv, dlat, dga, dgb]
    dwin = jnp.zeros((sum(pc.shape[1] for pc in pieces), d), BF)
    row0 = 0
    for i, pc in enumerate(pieces):
        dwin = _mm_tn_into("in_dw%d" % i, pc, u, dwin, row0)
        row0 += pc.shape[1]
    grad("w_in", dwin.reshape(NDEV, -1, d))
    s7 = to_sibling("w_in")
    du = _in_proj_bwd_x(pieces, [wqkv, wlat, wga, wgb], comm=[s7])
    add_pairs(s7)
    tok = start_chips("w_in", "w_out", "w_branch_a", "w_branch_b", "w_uq", "w_ukv", "w_in")
    dh1, dhb, dsp["mix_norm"] = _rms_bwd("mix_dnorm", du, h1, sp["mix_norm"], dh2, comm=[_After(tok)])

    xn, hg, hu, a = ffn1_saved
    grad("ffn1_w_down", _ffn_bwd_wd("ffn1_dwd", a, dhb))
    s8 = to_sibling("ffn1_w_down")
    dhg, dhu = _ffn_bwd_act("ffn1_dact", dhb, w["ffn1_w_down"], hg, hu, comm=[s8])
    add_pairs(s8)
    tok = start_chips("ffn1_down", "ffn1_w_down")
    dwg, dwu = _ffn_bwd_wup("ffn1_dwup", xn, dhg, dhu, comm=[_After(tok)])
    grad("ffn1_w_gate", dwg)
    grad("ffn1_w_up", dwu)
    s9 = to_sibling("ffn1_w_gate", "ffn1_w_up")
    _comm_only("rs_sibling_ffn1", [s9])
    add_pairs(s9)
    tok = start_chips("ffn1_up", "ffn1_w_gate", "ffn1_w_up")
    dxn = _ffn_bwd_x("ffn1_dx", dhg, dhu, w["ffn1_w_gate"], w["ffn1_w_up"], comm=[_After(tok)])
    dx, _, dsp["ffn1_norm"] = _rms_bwd("ffn1_dnorm", dxn, x, sp["ffn1_norm"], dh1)
    return loss, dx, pending, dsp


def _gather_small(buf):
    def body(in_ref, out_ref, send_sems, recv_sems, local_sem):
        x, y, c = _coords()
        mine = pltpu.make_async_copy(in_ref, out_ref.at[4 * x + 2 * y + c], local_sem)
        mine.start()
        cps = []
        for k in range(1, NDEV):
            fx, fy, fc = (k >> 2) & 1, (k >> 1) & 1, k & 1
            peer = (x ^ fx, y ^ fy, c ^ fc)
            cps.append(pltpu.make_async_remote_copy(
                src_ref=in_ref, dst_ref=out_ref.at[4 * x + 2 * y + c], send_sem=send_sems.at[k - 1],
                recv_sem=recv_sems.at[k - 1], device_id=peer, device_id_type=MESH))
        for cp in cps:
            cp.start()
        for k in range(1, NDEV):
            fx, fy, fc = (k >> 2) & 1, (k >> 1) & 1, k & 1
            px, py, pc = x ^ fx, y ^ fy, c ^ fc
            pltpu.make_async_remote_copy(
                src_ref=in_ref, dst_ref=out_ref.at[4 * px + 2 * py + pc], send_sem=send_sems.at[k - 1],
                recv_sem=recv_sems.at[k - 1], device_id=(px, py, pc), device_id_type=MESH).wait_recv()
        for cp in cps:
            cp.wait_send()
        mine.wait()

    return pl.pallas_call(
        body, name="gather_small", in_specs=[ANY], out_specs=ANY,
        out_shape=jax.ShapeDtypeStruct((NDEV,) + buf.shape, buf.dtype),
        scratch_shapes=[pltpu.SemaphoreType.DMA((NDEV - 1,)), pltpu.SemaphoreType.DMA((NDEV - 1,)),
                        pltpu.SemaphoreType.DMA],
    )(buf)


def _adam_math(wv, g, m, v):
    m_new = B1 * m + (1.0 - B1) * g
    v_new = B2 * v + (1.0 - B2) * (g * g)
    m_hat = m_new / (1.0 - B1 ** STEP)
    v_hat = v_new / (1.0 - B2 ** STEP)
    return -LR * (m_hat / (jnp.sqrt(v_hat) + ADAM_EPS) + WD * wv), m_new, v_new


def _adam(name, parts, wv, m, v):
    npart, r, c = parts.shape
    tr, tc = _ew_tile(r, c)

    def body(p_ref, w_ref, m_ref, v_ref, g_ref, d_ref, mo_ref, vo_ref):
        g = p_ref[0].astype(F32)
        for j in range(1, npart):
            g = g + p_ref[j].astype(F32)
        g_ref[...] = g
        d_ref[...], mo_ref[...], vo_ref[...] = _adam_math(w_ref[...], g, m_ref[...], v_ref[...])

    blk = pl.BlockSpec((tr, tc), lambda i, k: (i, k))
    return pl.pallas_call(
        body, name=name, grid=(r // tr, c // tc),
        in_specs=[pl.BlockSpec((npart, tr, tc), lambda i, k: (0, i, k)), blk, blk, blk],
        out_specs=[blk] * 4, out_shape=[jax.ShapeDtypeStruct((r, c), F32)] * 4, compiler_params=_params(2),
    )(parts, wv, m, v)


def _adam_exchanged(name, sums, land, wv, m, v, my_chip):
    _, r, c = sums.shape
    tr, tc = _ew_tile(r, c)

    def body(chip_ref, s_ref, l_ref, w_ref, m_ref, v_ref, g_ref, d_ref, mo_ref, vo_ref):
        g = s_ref[...].astype(F32)
        for j in range(3):
            g = g + l_ref[j].astype(F32)
        g_ref[...] = g
        d_ref[...], mo_ref[...], vo_ref[...] = _adam_math(w_ref[...], g, m_ref[...], v_ref[...])

    blk = pl.BlockSpec((tr, tc), lambda i, k, chip_ref: (i, k))
    return pl.pallas_call(
        body, name=name,
        grid_spec=pltpu.PrefetchScalarGridSpec(
            num_scalar_prefetch=1, grid=(r // tr, c // tc),
            in_specs=[pl.BlockSpec((None, tr, tc), lambda i, k, chip_ref: (chip_ref[0], i, k)),
                      pl.BlockSpec((3, tr, tc), lambda i, k, chip_ref: (0, i, k)), blk, blk, blk],
            out_specs=[blk] * 4),
        out_shape=[jax.ShapeDtypeStruct((r, c), F32)] * 4, compiler_params=_params(2),
    )(my_chip, sums, land, wv, m, v)


SHARDED = ("ffn1_w_gate", "ffn1_w_up", "ffn1_w_down", "w_in", "w_uq", "w_ukv", "w_branch_a", "w_branch_b", "w_out",
           "ffn2_w_gate", "ffn2_w_up", "ffn2_w_down", "w_pl", "w_pl_gate")
TRANSPOSED = ("ffn1_w_gate", "ffn1_w_up", "ffn2_w_gate", "ffn2_w_up", "w_in", "w_uq")
REPLICATED = ("ffn1_norm", "mix_norm", "q_a_norm", "kv_a_norm", "na_rpb", "ffn2_norm", "pl_norm", "final_norm")
WEIGHTS = ("ffn1_norm", "ffn1_w_gate", "ffn1_w_up", "ffn1_w_down", "mix_norm", "w_in", "q_a_norm", "w_uq",
           "kv_a_norm", "w_ukv", "na_rpb", "w_branch_a", "w_branch_b", "w_out", "ffn2_norm", "ffn2_w_gate",
           "ffn2_w_up", "ffn2_w_down", "pl_norm", "w_pl", "w_pl_gate", "final_norm")
SMALL_W = 2048


def _pack_small(vals):
    rows = []
    for name in REPLICATED:
        flat = vals[name].reshape(-1).astype(F32)
        n = -(-flat.shape[0] // SMALL_W) * SMALL_W
        rows.append(jnp.pad(flat, (0, n - flat.shape[0])).reshape(-1, SMALL_W))
    return jnp.concatenate(rows, axis=0)


def _unpack_small(buf, shapes):
    out, r = {}, 0
    for name in REPLICATED:
        size = int(np.prod(shapes[name]))
        nrow = -(-size // SMALL_W)
        out[name] = buf[r:r + nrow].reshape(-1)[:size].reshape(shapes[name])
        r += nrow
    return out


def kernel(x, p, ffn1_norm, ffn1_w_gate, ffn1_w_up, ffn1_w_down, mix_norm, w_in, q_a_norm, w_uq, kv_a_norm, w_ukv, na_rpb, w_branch_a, w_branch_b, w_out, ffn2_norm, ffn2_w_gate, ffn2_w_up, ffn2_w_down, pl_norm, w_pl, w_pl_gate, final_norm, loss_target, m_ffn1_norm, m_ffn1_w_gate, m_ffn1_w_up, m_ffn1_w_down, m_mix_norm, m_w_in, m_q_a_norm, m_w_uq, m_kv_a_norm, m_w_ukv, m_na_rpb, m_w_branch_a, m_w_branch_b, m_w_out, m_ffn2_norm, m_ffn2_w_gate, m_ffn2_w_up, m_ffn2_w_down, m_pl_norm, m_w_pl, m_w_pl_gate, m_final_norm, v_ffn1_norm, v_ffn1_w_gate, v_ffn1_w_up, v_ffn1_w_down, v_mix_norm, v_w_in, v_q_a_norm, v_w_uq, v_kv_a_norm, v_w_ukv, v_na_rpb, v_w_branch_a, v_w_branch_b, v_w_out, v_ffn2_norm, v_ffn2_w_gate, v_ffn2_w_up, v_ffn2_w_down, v_pl_norm, v_w_pl, v_w_pl_gate, v_final_norm):
    args = dict(locals())
    wts = {n: args[n] for n in WEIGHTS}
    mom = {n: args["m_" + n] for n in WEIGHTS}
    var = {n: args["v_" + n] for n in WEIGHTS}
    shapes = {n: wts[n].shape for n in WEIGHTS}
    core = lax.axis_index("c").astype(jnp.int32).reshape(1)

    local = lambda n, a: a[0].T if n in TRANSPOSED else a[0]
    own = {n: local(n, wts[n]).astype(BF) for n in SHARDED}
    sp = {n: wts[n].reshape(1, -1) for n in REPLICATED if n != "na_rpb"}
    sp["na_rpb"] = wts["na_rpb"][0]
    loss_part, grad_x, pending, dsp = _device_step(x[0], p[0, 0], loss_target[0], sp, own, core)

    out = {}
    last = grad_x
    my_chip = (2 * lax.axis_index("x") + lax.axis_index("y")).astype(jnp.int32).reshape(1)
    for tag, names, send, recv, thru, lands in pending:
        thru, lands = _chips_wait("rs_wait_" + tag, send, recv, thru, lands, last)
        for n, s4, l3 in zip(names, thru, lands):
            res4 = _adam_exchanged("adam_" + n, s4, l3, local(n, wts[n]), local(n, mom[n]), local(n, var[n]), my_chip)
            out[n] = tuple((a.T if n in TRANSPOSED else a)[None] for a in res4)
            last = res4[1]

    small = jnp.concatenate([_pack_small(dsp), jnp.pad(loss_part, ((0, 0), (0, SMALL_W - loss_part.shape[1])))], 0)
    pad_rows = -small.shape[0] % 8
    small = jnp.pad(small, ((0, pad_rows), (0, 0)))
    every = _gather_small(small)
    zeros = jnp.zeros((1 + pad_rows, SMALL_W), F32)
    pack = lambda d: jnp.concatenate([_pack_small(d), zeros], 0)
    g_s, d_s, m_s, v_s = _adam("adam_small", every, pack(wts), pack(mom), pack(var))
    n_rows = small.shape[0] - 1 - pad_rows
    loss = g_s[n_rows, 0]
    small_out = [_unpack_small(b, shapes) for b in (g_s, d_s, m_s, v_s)]
    for n in REPLICATED:
        out[n] = tuple(b[n] for b in small_out)

    res = [loss, grad_x[None]]
    for k in range(4):
        res += [out[n][k] for n in WEIGHTS]
    return tuple(res)
```

```python
import functools

import numpy as np
import jax
import jax.numpy as jnp
from jax import lax
from jax.experimental import pallas as pl
from jax.experimental.pallas import tpu as pltpu

F32 = jnp.float32
BF = jnp.bfloat16
MESH = pl.DeviceIdType.MESH

NDEV = 8
NCHIP = 4
VMEM_LIMIT = 56 * 1024 * 1024
EPS = 1e-6
NEG = -1e30
GRID_W = 64
NA_HEADS, NA_DIM = 8, 128
NA_ROWS_WIN, NA_COLS_WIN = 8, 16
NA_HG = 4
NA_QROWS = 4
ML_HEADS, ML_NOPE, ML_ROPE, ML_V = 8, 128, 64, 128
ML_QK = ML_NOPE + ML_ROPE
ML_RANK = 512
ROPE_THETA = 10000.0
LR, B1, B2, ADAM_EPS, WD, STEP = 0.001, 0.9, 0.999, 1e-08, 0.01, 10
HI = lax.Precision.HIGHEST

_DN = {"nn": (((1,), (0,)), ((), ())), "nt": (((1,), (1,)), ((), ())), "tn": (((0,), (0,)), ((), ()))}


def _params(n):
    return pltpu.CompilerParams(dimension_semantics=("arbitrary",) * n, vmem_limit_bytes=VMEM_LIMIT)


def _sig(v):
    return jax.nn.sigmoid(v)


ANY = pl.BlockSpec(memory_space=pl.ANY)


def _coords():
    return lax.axis_index("x"), lax.axis_index("y"), lax.axis_index("c")


class _Part:
    inputs, out_shapes, sem_shapes, results = (), (), (), None

    def mid(self, ins, outs, sems):
        pass

    def late(self, ins, outs, sems):
        pass


class _After(_Part):
    def __init__(self, token):
        self.inputs = [token]

    def start(self, ins, outs, sems):
        pass

    finish = start


class _GatherPart(_Part):
    def __init__(self, names, shards):
        n = len(shards)
        self.names, self.inputs = list(names), list(shards)
        self.out_shapes = [jax.ShapeDtypeStruct((NDEV,) + a.shape, a.dtype) for a in shards]
        self.sem_shapes = [pltpu.SemaphoreType.DMA((n, 7)), pltpu.SemaphoreType.DMA((n, 7)),
                           pltpu.SemaphoreType.DMA((n,))]

    def _plan(self, ins, outs, sems):
        send_sems, recv_sems, local_sems = sems
        x, y, c = _coords()
        me, sib, diag = (x, y, c), (x, y, 1 - c), (1 - x, 1 - y, c)
        n1, n2 = (x ^ (1 - c), y ^ c, c), (x ^ c, y ^ (1 - c), c)

        def copy(i, k, block, to, src=None):
            px, py, pc = block
            dst = outs[i].at[4 * px + 2 * py + pc]
            return pltpu.make_async_remote_copy(
                src_ref=dst if src is None else src, dst_ref=dst, send_sem=send_sems.at[i, k],
                recv_sem=recv_sems.at[i, k], device_id=to, device_id_type=MESH)

        mine = [pltpu.make_async_copy(ins[i], outs[i].at[4 * x + 2 * y + c], local_sems.at[i])
                for i in range(len(ins))]
        return copy, mine, me, sib, n1, n2, diag

    def _own_sends(self, ins, copy, me, sib, n1, n2):
        return [copy(i, k, me, to, src=ins[i]) for i in range(len(ins)) for k, to in enumerate((sib, n1, n2))]

    def start(self, ins, outs, sems):
        copy, mine, me, sib, n1, n2, _ = self._plan(ins, outs, sems)
        for cp in mine + self._own_sends(ins, copy, me, sib, n1, n2):
            cp.start()

    def mid(self, ins, outs, sems):
        copy, _, me, sib, n1, n2, _ = self._plan(ins, outs, sems)
        for i in range(len(ins)):
            copy(i, 1, n1, me).wait_recv()
            copy(i, 3, n1, n2).start()
            copy(i, 4, n1, sib).start()

    def late(self, ins, outs, sems):
        copy, _, me, sib, _, n2, diag = self._plan(ins, outs, sems)
        for i in range(len(ins)):
            copy(i, 2, n2, me).wait_recv()
            copy(i, 5, n2, sib).start()
        for i in range(len(ins)):
            copy(i, 3, diag, me).wait_recv()
            copy(i, 6, diag, sib).start()

    def finish(self, ins, outs, sems):
        copy, mine, me, sib, n1, n2, diag = self._plan(ins, outs, sems)
        other = lambda dev: (dev[0], dev[1], sib[2])
        n = len(ins)
        for i in range(n):
            copy(i, 0, sib, me).wait_recv()
            for k, block in ((4, other(n2)), (5, other(n1)), (6, other(diag))):
                copy(i, k, block, me).wait_recv()
        for cp in self._own_sends(ins, copy, me, sib, n1, n2):
            cp.wait_send()
        for i in range(n):
            for k, block in ((3, n1), (4, n1), (5, n2), (6, diag)):
                copy(i, k, block, sib).wait_send()
        for cp in mine:
            cp.wait()


class _SiblingPart(_Part):
    def __init__(self, names, parts):
        n = len(parts)
        self.names, self.inputs = list(names), list(parts)
        self.out_shapes = [jax.ShapeDtypeStruct((NCHIP,) + a.shape[2:], a.dtype) for a in parts]
        self.sem_shapes = [pltpu.SemaphoreType.DMA((n,)), pltpu.SemaphoreType.DMA((n,))]

    def _copies(self, ins, outs, sems):
        x, y, c = _coords()
        return [pltpu.make_async_remote_copy(
            src_ref=ins[i].at[:, 1 - c], dst_ref=outs[i], send_sem=sems[0].at[i], recv_sem=sems[1].at[i],
            device_id=(x, y, 1 - c), device_id_type=MESH) for i in range(len(ins))]

    def start(self, ins, outs, sems):
        for cp in self._copies(ins, outs, sems):
            cp.start()

    def finish(self, ins, outs, sems):
        cps = self._copies(ins, outs, sems)
        for cp in cps:
            cp.wait_recv()
        for cp in cps:
            cp.wait_send()


HBM = pl.BlockSpec(memory_space=pltpu.HBM)
SEM = pl.BlockSpec(memory_space=pltpu.SEMAPHORE)


def _chip_peers():
    x, y, c = _coords()
    return [(1 - x, y, c), (x, 1 - y, c), (1 - x, 1 - y, c)]


def _chips_start(name, sums):
    n = len(sums)

    def body(*refs):
        ins, lands, send_sems, recv_sems = refs[:n], refs[n:2 * n], refs[2 * n], refs[2 * n + 1]
        for i in range(n):
            for k, (px, py, pc) in enumerate(_chip_peers()):
                pltpu.make_async_remote_copy(
                    src_ref=ins[i].at[2 * px + py], dst_ref=lands[i].at[k], send_sem=send_sems.at[3 * i + k],
                    recv_sem=recv_sems.at[3 * i + k], device_id=(px, py, pc), device_id_type=MESH).start()
        refs[-1][...] = jnp.zeros_like(refs[-1])

    lands = [lax.empty((3,) + a.shape[1:], a.dtype) for a in sums]
    bufs = list(sums) + lands
    res = pl.pallas_call(
        body, name=name, in_specs=[HBM] * (2 * n),
        out_specs=(SEM, SEM, *[HBM] * (2 * n), pl.BlockSpec(memory_space=pltpu.VMEM)),
        out_shape=(pltpu.SemaphoreType.DMA((3 * n,)), pltpu.SemaphoreType.DMA((3 * n,)),
                   *[pltpu.HBM(a.shape, a.dtype) for a in bufs], jax.ShapeDtypeStruct((8, 128), F32)),
        input_output_aliases={i: 2 + i for i in range(2 * n)},
        compiler_params=pltpu.CompilerParams(has_side_effects=pltpu.SideEffectType.DATAFLOW_SIDE_EFFECTING),
    )(*[pltpu.with_memory_space_constraint(a, pltpu.HBM) for a in bufs])
    return res[0], res[1], list(res[2:2 + n]), list(res[2 + n:2 + 2 * n]), res[-1]


def _chips_wait(name, send_sems, recv_sems, sums, lands, after):
    n = len(sums)

    def body(*refs):
        ins, zones, send, recv = refs[:n], refs[n:2 * n], refs[2 * n], refs[2 * n + 1]
        for i in range(n):
            for k, peer in enumerate(_chip_peers()):
                cp = pltpu.make_async_remote_copy(
                    src_ref=ins[i].at[0], dst_ref=zones[i].at[k], send_sem=send.at[3 * i + k],
                    recv_sem=recv.at[3 * i + k],
                    device_id=peer, device_id_type=MESH)
                cp.wait_send()
                cp.wait_recv()

    bufs = list(sums) + list(lands)
    res = pl.pallas_call(
        body, name=name, in_specs=[HBM] * (2 * n) + [SEM, SEM, ANY], out_specs=[HBM] * (2 * n),
        out_shape=[pltpu.HBM(a.shape, a.dtype) for a in bufs], input_output_aliases={i: i for i in range(2 * n)},
        compiler_params=pltpu.CompilerParams(has_side_effects=pltpu.SideEffectType.DATAFLOW_SIDE_EFFECTING),
    )(*bufs, send_sems, recv_sems, after)
    return list(res[:n]), list(res[n:])


def _call(name, body, grid, in_specs, out_specs, out_shape, args, comm=(), scratch=()):
    comm = [p for p in comm if p is not None]
    single = not isinstance(out_shape, (list, tuple))
    o_specs = [out_specs] if single else list(out_specs)
    o_shape = [out_shape] if single else list(out_shape)
    n_in, n_out = len(in_specs), len(o_specs)
    c_in = [a for p in comm for a in p.inputs]
    c_out = [s for p in comm for s in p.out_shapes]
    c_sem = [s for p in comm for s in p.sem_shapes]

    def wrapped(*refs):
        ins, outs = refs[:n_in], refs[n_in + len(c_in):n_in + len(c_in) + n_out]
        pos = [n_in, n_in + len(c_in) + n_out, n_in + len(c_in) + n_out + len(c_out)]
        own = refs[pos[2]:pos[2] + len(scratch)]
        pos[2] += len(scratch)
        split = []
        for p in comm:
            sizes = [len(p.inputs), len(p.out_shapes), len(p.sem_shapes)]
            split.append([refs[o:o + n] for o, n in zip(pos, sizes)])
            pos = [o + n for o, n in zip(pos, sizes)]
        step, steps = 0, 1
        for a, g in enumerate(grid):
            step, steps = step * g + pl.program_id(a), steps * g

        def run(which, at):
            def go():
                for p, cut in zip(comm, split):
                    getattr(p, which)(*cut)
            if not comm:
                return
            if grid:
                pl.when(step == at)(go)
            else:
                go()

        run("start", 0)
        body(*ins, *outs, *own)
        run("mid", steps // 2)
        run("late", max(steps // 2, steps - 1 - max(1, steps // 8)))
        run("finish", steps - 1)

    res = pl.pallas_call(
        wrapped, name=name, grid=grid, in_specs=list(in_specs) + [ANY] * len(c_in),
        out_specs=o_specs + [ANY] * len(c_out), out_shape=o_shape + c_out, scratch_shapes=list(scratch) + c_sem,
        compiler_params=_params(len(grid)),
    )(*args, *c_in)
    pos = n_out
    for p in comm:
        p.results = list(res[pos:pos + len(p.out_shapes)])
        pos += len(p.out_shapes)
    return res[0] if single else list(res[:n_out])


def _comm_only(name, comm):
    def body(o_ref):
        o_ref[...] = jnp.zeros_like(o_ref)

    _call(name, body, (), [], pl.BlockSpec(memory_space=pltpu.VMEM), jax.ShapeDtypeStruct((8, 128), F32), [], comm)


def _mm(name, grid, prods, extras, outs, epi, nacc=1, comm=()):
    n_p, n_e = len(prods), len(extras)

    def body(*refs):
        ab, ex, out = refs[:2 * n_p], refs[2 * n_p:2 * n_p + n_e], refs[2 * n_p + n_e:]
        accs = [None] * nacc
        for i, prod in enumerate(prods):
            dn, acc, loop = prod[6], prod[7], prod[8]
            a_ref, b_ref = ab[2 * i], ab[2 * i + 1]
            if loop:
                for g in range(loop):
                    t = lax.dot_general(a_ref[g], b_ref[g], _DN[dn], preferred_element_type=F32)
                    accs[acc] = t if accs[acc] is None else accs[acc] + t
            else:
                t = lax.dot_general(a_ref[...], b_ref[...], _DN[dn], preferred_element_type=F32)
                accs[acc] = t if accs[acc] is None else accs[acc] + t
        epi(accs, ex, out)

    in_specs, args = [], []
    for prod in prods:
        in_specs += [pl.BlockSpec(prod[1], prod[2]), pl.BlockSpec(prod[4], prod[5])]
        args += [prod[0], prod[3]]
    for e, e_blk, e_map in extras:
        in_specs.append(pl.BlockSpec(e_blk, e_map))
        args.append(e)
    return _call(name, body, grid, in_specs, [pl.BlockSpec(blk, mp) for _, _, blk, mp in outs],
                 [jax.ShapeDtypeStruct(s, d) for s, d, _, _ in outs], args, comm)


def _store(accs, ex, out):
    out[0][...] = accs[0].astype(out[0].dtype)


def _ew_tile(r, c, budget=3 << 19):
    for t in range(r - r % 16, 0, -16):
        if r % t == 0 and t * c * 4 <= budget:
            return t, c
    for t in range(c - c % 128, 0, -128):
        if c % t == 0 and r * t * 4 <= budget:
            return r, t
    return r, c


def _tile(n, want):
    t = min(n, want)
    assert n % t == 0, (n, want)
    return t


def _mm_nt(name, a, bt, out_dtype, tm=512, tn=512, comm=(), rows=None):
    m, k = a.shape
    n = rows or bt.shape[0]
    tm, tn = _tile(m, tm), (tn if n % tn == 0 else n)
    return _mm(name, (n // tn, m // tm),
               [(a, (tm, k), lambda j, i: (i, 0), bt, (tn, k), lambda j, i: (j, 0), "nt", 0, 0)], [],
               [((m, n), out_dtype, (tm, tn), lambda j, i: (i, j))], _store, comm=comm)[0]


def _mm_tn_into(name, a, b, buf, row0, ta=1024, tb=512):
    t, ka = a.shape
    nb = b.shape[1]
    ta, tb = (ta if ka % ta == 0 else ka), (tb if nb % tb == 0 else nb)

    def body(a_ref, b_ref, buf_in, buf_out, tile, sem):
        i, j = pl.program_id(0), pl.program_id(1)
        tile[...] = lax.dot_general(a_ref[...], b_ref[...], _DN["tn"], preferred_element_type=F32).astype(tile.dtype)
        rows = pl.ds(pl.multiple_of(row0 + i * ta, 16), ta)
        cp = pltpu.make_async_copy(tile, buf_out.at[rows, pl.ds(pl.multiple_of(j * tb, 128), tb)], sem)
        cp.start()
        cp.wait()

    return pl.pallas_call(
        body, name=name, grid=(ka // ta, nb // tb),
        in_specs=[pl.BlockSpec((t, ta), lambda i, j: (0, i)), pl.BlockSpec((t, tb), lambda i, j: (0, j)), ANY],
        out_specs=ANY, out_shape=jax.ShapeDtypeStruct(buf.shape, buf.dtype), input_output_aliases={2: 0},
        scratch_shapes=[pltpu.VMEM((ta, tb), buf.dtype), pltpu.SemaphoreType.DMA],
        compiler_params=_params(2))(a, b, buf)


def _rms_fwd(name, x, g, tm=256, comm=()):
    s, d = x.shape
    tm = _tile(s, tm)

    def body(x_ref, g_ref, o_ref):
        v = x_ref[...]
        o_ref[...] = (v * lax.rsqrt(jnp.mean(v * v, axis=-1, keepdims=True) + EPS) * g_ref[...]).astype(o_ref.dtype)

    return _call(name, body, (s // tm,),
                 [pl.BlockSpec((tm, d), lambda i: (i, 0)), pl.BlockSpec((1, d), lambda i: (0, 0))],
                 pl.BlockSpec((tm, d), lambda i: (i, 0)), jax.ShapeDtypeStruct((s, d), BF), [x, g], comm)


def _acc_rows(ref, part, i):
    @pl.when(i == 0)
    def _():
        ref[...] = part

    @pl.when(i > 0)
    def _():
        ref[...] += part


def _rms_bwd_math(dn, v, g):
    rstd = lax.rsqrt(jnp.mean(v * v, axis=-1, keepdims=True) + EPS)
    xh = v * rstd
    dxh = dn * g
    dx = rstd * (dxh - xh * jnp.mean(dxh * xh, axis=-1, keepdims=True))
    return dx, jnp.sum(dn * xh, axis=0, keepdims=True)


def _rms_bwd(name, dn, x, g, resid, tm=256, comm=()):
    s, d = x.shape
    tm = _tile(s, tm)

    def body(dn_ref, x_ref, g_ref, r_ref, dx_ref, dxb_ref, dg_ref):
        dx, part = _rms_bwd_math(dn_ref[...].astype(F32), x_ref[...], g_ref[...])
        tot = r_ref[...] + dx
        dx_ref[...] = tot
        dxb_ref[...] = tot.astype(BF)
        _acc_rows(dg_ref, part, pl.program_id(0))

    row = pl.BlockSpec((tm, d), lambda i: (i, 0))
    one = pl.BlockSpec((1, d), lambda i: (0, 0))
    return _call(name, body, (s // tm,), [row, row, one, row], [row, row, one],
                 [jax.ShapeDtypeStruct((s, d), F32), jax.ShapeDtypeStruct((s, d), BF),
                  jax.ShapeDtypeStruct((1, d), F32)], [dn, x, g, resid], comm)


def _loss_head(h, target, g, tm=256):
    s, d = h.shape
    tm = _tile(s, tm)

    def body(h_ref, t_ref, g_ref, dh_ref, dg_ref, loss_ref):
        v, gv = h_ref[...], g_ref[...]
        rstd = lax.rsqrt(jnp.mean(v * v, axis=-1, keepdims=True) + EPS)
        xh = v * rstd
        err = xh * gv - t_ref[...]
        part_loss = 0.5 * jnp.sum(jnp.mean(err * err, axis=-1, keepdims=True), axis=0, keepdims=True)
        dy = err * (1.0 / d)
        dxh = dy * gv
        dh_ref[...] = rstd * (dxh - xh * jnp.mean(dxh * xh, axis=-1, keepdims=True))
        i = pl.program_id(0)
        _acc_rows(dg_ref, jnp.sum(dy * xh, axis=0, keepdims=True), i)
        _acc_rows(loss_ref, jnp.broadcast_to(part_loss, loss_ref.shape), i)

    row = pl.BlockSpec((tm, d), lambda i: (i, 0))
    one = pl.BlockSpec((1, d), lambda i: (0, 0))
    return pl.pallas_call(
        body, name="loss_head", grid=(s // tm,), in_specs=[row, row, one],
        out_specs=[row, one, pl.BlockSpec((1, 128), lambda i: (0, 0))],
        out_shape=[jax.ShapeDtypeStruct((s, d), F32), jax.ShapeDtypeStruct((1, d), F32),
                   jax.ShapeDtypeStruct((1, 128), F32)],
        compiler_params=_params(1))(h, target, g)


def _pl_bwd_elem(dh, pe, t, tm=256):
    s, d = dh.shape
    tm = _tile(s, tm)

    def body(dh_ref, pe_ref, t_ref, dt_ref, dpe_ref):
        dh_v, sg = dh_ref[...], _sig(t_ref[...])
        dt_ref[...] = (dh_v * pe_ref[...].astype(F32) * sg * (1.0 - sg)).astype(BF)
        dpe_ref[...] = (dh_v * sg).astype(BF)

    row = pl.BlockSpec((tm, d), lambda i: (i, 0))
    return pl.pallas_call(
        body, name="pl_bwd_elem", grid=(s // tm,), in_specs=[row, row, row], out_specs=[row, row],
        out_shape=[jax.ShapeDtypeStruct((s, d), BF)] * 2, compiler_params=_params(1))(dh, pe, t)


def _ffn_up(name, xn, wg, wu, tm=1024, comm=()):
    s, d = xn.shape
    g, fb, _ = wg.shape
    tm = _tile(s, tm)

    def epi(accs, ex, out):
        hg, hu = accs
        out[0][...] = hg.astype(BF)
        out[1][...] = hu.astype(BF)
        out[2][...] = (hg * _sig(hg) * hu).astype(BF)

    a_map = lambda j, i: (i, 0)
    w_map = lambda j, i: (j, 0, 0)
    o = ((g, s, fb), BF, (None, tm, fb), lambda j, i: (j, i, 0))
    return _mm(name, (g, s // tm),
               [(xn, (tm, d), a_map, wg, (None, fb, d), w_map, "nt", 0, 0),
                (xn, (tm, d), a_map, wu, (None, fb, d), w_map, "nt", 1, 0)], [], [o, o, o], epi, nacc=2, comm=comm)


def _ffn_down(name, a, wd, resid, tm=1024, tn=512, comm=()):
    g, s, fb = a.shape
    d = wd.shape[2]
    tm, tn = _tile(s, tm), _tile(d, tn)

    def epi(accs, ex, out):
        out[0][...] = ex[0][...] + 0.5 * accs[0]

    return _mm(name, (d // tn, s // tm),
               [(a, (g, tm, fb), lambda j, i: (0, i, 0), wd, (g, fb, tn), lambda j, i: (0, 0, j), "nn", 0, g)],
               [(resid, (tm, tn), lambda j, i: (i, j))],
               [((s, d), F32, (tm, tn), lambda j, i: (i, j))], epi, comm=comm)[0]


def _ffn_bwd_act(name, dh, wd, hg, hu, tm=1024, comm=()):
    s, d = dh.shape
    g, fb, _ = wd.shape
    tm = _tile(s, tm)

    def epi(accs, ex, out):
        da = 0.5 * accs[0]
        hg_v, hu_v = ex[0][...].astype(F32), ex[1][...].astype(F32)
        sg = _sig(hg_v)
        out[0][...] = (da * hu_v * (sg * (1.0 + hg_v * (1.0 - sg)))).astype(BF)
        out[1][...] = (da * (hg_v * sg)).astype(BF)

    blk = (None, tm, fb)
    gmap = lambda j, i: (j, i, 0)
    return _mm(name, (g, s // tm),
               [(dh, (tm, d), lambda j, i: (i, 0), wd, (None, fb, d), lambda j, i: (j, 0, 0), "nt", 0, 0)],
               [(hg, blk, gmap), (hu, blk, gmap)],
               [((g, s, fb), BF, blk, gmap), ((g, s, fb), BF, blk, gmap)], epi, comm=comm)


def _ffn_bwd_wd(name, a, dh, tn=1024, comm=()):
    g, s, fb = a.shape
    d = dh.shape[1]
    tn = _tile(d, tn)

    def epi(accs, ex, out):
        out[0][...] = (0.5 * accs[0]).astype(BF)

    return _mm(name, (g, d // tn),
               [(a, (None, s, fb), lambda j, i: (j, 0, 0), dh, (s, tn), lambda j, i: (0, i), "tn", 0, 0)], [],
               [((g, fb, d), BF, (None, fb, tn), lambda j, i: (j, 0, i))], epi, comm=comm)[0]


def _ffn_bwd_wup(name, xn, dhg, dhu, tk=1024, comm=()):
    s, d = xn.shape
    g, _, fb = dhg.shape
    tk = _tile(d, tk)

    def epi(accs, ex, out):
        out[0][...] = accs[0].astype(BF)
        out[1][...] = accs[1].astype(BF)

    a_map = lambda j, i: (j, 0, 0)
    b_map = lambda j, i: (0, i)
    o = ((g, fb, d), BF, (None, fb, tk), lambda j, i: (j, 0, i))
    return _mm(name, (g, d // tk),
               [(dhg, (None, s, fb), a_map, xn, (s, tk), b_map, "tn", 0, 0),
                (dhu, (None, s, fb), a_map, xn, (s, tk), b_map, "tn", 1, 0)], [], [o, o], epi, nacc=2, comm=comm)


def _ffn_bwd_x(name, dhg, dhu, wg, wu, tm=512, tn=512, comm=()):
    g, s, fb = dhg.shape
    d = wg.shape[2]
    tm, tn = _tile(s, tm), _tile(d, tn)
    a_blk, a_map = (g, tm, fb), lambda j, i: (0, i, 0)
    b_blk, b_map = (g, fb, tn), lambda j, i: (0, 0, j)
    return _mm(name, (d // tn, s // tm),
               [(dhg, a_blk, a_map, wg, b_blk, b_map, "nn", 0, g), (dhu, a_blk, a_map, wu, b_blk, b_map, "nn", 0, g)],
               [], [((s, d), F32, (tm, tn), lambda j, i: (i, j))], _store, comm=comm)[0]


def _ffn_forward(tag, h, gain, get_wgu, get_wd, norm_comm=(), up_comm=(), down_comm=()):
    xn = _rms_fwd(tag + "_norm", h, gain, comm=norm_comm)
    wg, wu = get_wgu()
    hg, hu, a = _ffn_up(tag + "_up", xn, wg, wu, comm=up_comm)
    return _ffn_down(tag + "_down", a, get_wd(), h, comm=down_comm), (xn, hg, hu, a)


def _na_geometry(rows):
    kh = min(NA_ROWS_WIN, rows)
    cols = np.arange(GRID_W)
    col_start = np.clip(cols - NA_COLS_WIN // 2, 0, GRID_W - NA_COLS_WIN)
    mask = (cols[None, :] >= col_start[:, None]) & (cols[None, :] < col_start[:, None] + NA_COLS_WIN)
    dc = np.clip(cols[None, :] - cols[:, None], -(NA_COLS_WIN - 1), NA_COLS_WIN - 1) + (NA_COLS_WIN - 1)
    return kh, mask, dc


def _na_table(rpb, rows):
    _, mask, dc = _na_geometry(rows)
    nd, nc, cells = 2 * NA_ROWS_WIN - 1, 2 * NA_COLS_WIN - 1, GRID_W * GRID_W
    onehot = np.zeros((128, cells), np.float32)
    onehot[dc.reshape(-1), np.arange(cells)] = mask.reshape(-1).astype(np.float32)
    off = np.where(mask.reshape(1, -1), 0.0, NEG).astype(np.float32)

    def body(r_ref, e_ref, off_ref, o_ref):
        o_ref[...] = jnp.dot(r_ref[...], e_ref[...], precision=HI, preferred_element_type=F32) + off_ref[...]

    flat = pl.pallas_call(body, name="na_table", out_shape=jax.ShapeDtypeStruct((NA_HEADS * nd, cells), F32),
                          compiler_params=_params(0))(
        jnp.pad(rpb.reshape(NA_HEADS * nd, nc), ((0, 0), (0, 128 - nc))), jnp.asarray(onehot), jnp.asarray(off))
    return flat.reshape(NA_HEADS, nd, GRID_W, GRID_W)


class _NaPlan:
    def __init__(self, s):
        self.s, self.rows = s, s // GRID_W
        self.kh = min(NA_ROWS_WIN, self.rows)
        self.qr = min(NA_QROWS, self.rows)
        self.kr = min(self.rows, self.kh + self.qr - 1)
        self.groups = self.rows // self.qr
        self.nd = 2 * NA_ROWS_WIN - 1
        self.hw, self.nq = NA_HG * NA_DIM, NA_HEADS // NA_HG
        clip = lambda v, hi: min(max(v, 0), hi)
        pats = [(clip(g * self.qr - self.kh // 2, self.rows - self.kr) - g * self.qr,)
                + tuple(clip(g * self.qr + a - self.kh // 2, self.rows - self.kh) - g * self.qr for a in range(self.qr))
                for g in range(self.groups)]
        self.rebuild = [g for g in range(self.groups) if g == 0 or pats[g] != pats[g - 1]]

    def first_key_row(self, g):
        return jnp.clip(g * self.qr - self.kh // 2, 0, self.rows - self.kr)

    def specs(self):
        blk = pl.BlockSpec((self.qr * GRID_W, self.hw), lambda j, g: (g, j))
        k_spec = pl.BlockSpec((self.s, self.hw), lambda j, g: (0, self.nq + j))
        v_spec = pl.BlockSpec((self.s, self.hw), lambda j, g: (0, 2 * self.nq + j))
        t_spec = pl.BlockSpec((NA_HG, self.nd, GRID_W, GRID_W), lambda j, g: (j, 0, 0, 0))
        return blk, k_spec, v_spec, t_spec

    def bias_scratch(self):
        return pltpu.VMEM((NA_HG, self.qr * GRID_W, self.kr * GRID_W), F32)

    def fill_bias(self, t_ref, bias_ref, g):
        def build():
            r0, ks = g * self.qr, self.first_key_row(g)
            for a in range(self.qr):
                rs = jnp.clip(r0 + a - self.kh // 2, 0, self.rows - self.kh)
                for i in range(self.kr):
                    valid = jnp.logical_and(ks + i >= rs, ks + i < rs + self.kh)
                    idx = jnp.clip(ks + i - r0 - a + NA_ROWS_WIN - 1, 0, self.nd - 1)
                    for h in range(NA_HG):
                        bias_ref[h, a * GRID_W:(a + 1) * GRID_W, i * GRID_W:(i + 1) * GRID_W] = jnp.where(
                            valid, t_ref[h, idx], NEG)

        pl.when(functools.reduce(jnp.logical_or, [g == r for r in self.rebuild]))(build)

    def window(self, g):
        return pl.ds(pl.multiple_of(self.first_key_row(g) * GRID_W, GRID_W), self.kr * GRID_W)


def _na_probs(q, k, bias):
    sc = lax.dot_general(q, k, _DN["nt"], preferred_element_type=F32) * (NA_DIM ** -0.5) + bias
    e = jnp.exp(sc - jnp.max(sc, axis=-1, keepdims=True))
    return e / jnp.sum(e, axis=-1, keepdims=True)


def _na_fwd(qkv, table, comm=()):
    plan = _NaPlan(qkv.shape[0])
    blk, k_spec, v_spec, t_spec = plan.specs()

    def body(q_ref, k_ref, v_ref, t_ref, o_ref, bias_ref):
        g = pl.program_id(1)
        plan.fill_bias(t_ref, bias_ref, g)
        win = plan.window(g)
        for h in range(NA_HG):
            cs = slice(h * NA_DIM, (h + 1) * NA_DIM)
            p = _na_probs(q_ref[:, cs], k_ref[win, cs], bias_ref[h])
            o_ref[:, cs] = jnp.dot(p.astype(BF), v_ref[win, cs], preferred_element_type=F32).astype(BF)

    return _call("na_fwd", body, (plan.nq, plan.groups), [blk, k_spec, v_spec, t_spec], blk,
                 jax.ShapeDtypeStruct((plan.s, NA_HEADS * NA_DIM), BF), [qkv, qkv, qkv, table], comm,
                 scratch=[plan.bias_scratch()])


def _na_bwd(qkv, table, do, comm=()):
    plan = _NaPlan(qkv.shape[0])
    blk, k_spec, v_spec, t_spec = plan.specs()
    qr, kr = plan.qr, plan.kr

    def body(q_ref, k_ref, v_ref, t_ref, do_ref, dq_ref, dk_ref, dv_ref, dt_ref, bias_ref):
        g = pl.program_id(1)

        @pl.when(g == 0)
        def _():
            dk_ref[...] = jnp.zeros_like(dk_ref)
            dv_ref[...] = jnp.zeros_like(dv_ref)
            dt_ref[...] = jnp.zeros_like(dt_ref)

        plan.fill_bias(t_ref, bias_ref, g)
        win = plan.window(g)
        base = plan.first_key_row(g) - g * qr + NA_ROWS_WIN - 1
        for h in range(NA_HG):
            cs = slice(h * NA_DIM, (h + 1) * NA_DIM)
            q, k, v, do_h = q_ref[:, cs], k_ref[win, cs], v_ref[win, cs], do_ref[:, cs]
            p = _na_probs(q, k, bias_ref[h])
            dp = lax.dot_general(do_h, v, _DN["nt"], preferred_element_type=F32)
            ds = p * (dp - jnp.sum(p * dp, axis=-1, keepdims=True))
            for dlt in range(1 - qr, kr):
                tiles = [ds[a * GRID_W:(a + 1) * GRID_W, (a + dlt) * GRID_W:(a + dlt + 1) * GRID_W]
                         for a in range(qr) if 0 <= a + dlt < kr]
                dt_ref[h, jnp.clip(base + dlt, 0, plan.nd - 1)] += functools.reduce(jnp.add, tiles)
            dsb = (ds * (NA_DIM ** -0.5)).astype(BF)
            dq_ref[:, cs] = jnp.dot(dsb, k, preferred_element_type=F32).astype(BF)
            dk_ref[win, cs] += lax.dot_general(dsb, q, _DN["tn"], preferred_element_type=F32)
            dv_ref[win, cs] += lax.dot_general(p.astype(BF), do_h, _DN["tn"], preferred_element_type=F32)

    width = NA_HEADS * NA_DIM
    whole = pl.BlockSpec((plan.s, plan.hw), lambda j, g: (0, j))
    return _call(
        "na_bwd", body, (plan.nq, plan.groups), [blk, k_spec, v_spec, t_spec, blk], [blk, whole, whole, t_spec],
        [jax.ShapeDtypeStruct((plan.s, width), BF), jax.ShapeDtypeStruct((plan.s, width), F32),
         jax.ShapeDtypeStruct((plan.s, width), F32),
         jax.ShapeDtypeStruct((NA_HEADS, plan.nd, GRID_W, GRID_W), F32)],
        [qkv, qkv, qkv, table, do], comm, scratch=[plan.bias_scratch()])


def _na_rpb_grad(dt, rows):
    _, mask, dc = _na_geometry(rows)
    nd, nc = 2 * NA_ROWS_WIN - 1, 2 * NA_COLS_WIN - 1
    onehot = np.zeros((GRID_W * GRID_W, 128), np.float32)
    onehot[np.arange(GRID_W * GRID_W), dc.reshape(-1)] = mask.reshape(-1).astype(np.float32)
    flat = dt.reshape(NA_HEADS * nd, GRID_W * GRID_W)

    def body(a_ref, e_ref, o_ref):
        o_ref[...] = jnp.dot(a_ref[...], e_ref[...], precision=HI, preferred_element_type=F32)

    out = pl.pallas_call(body, name="na_rpb_grad", out_shape=jax.ShapeDtypeStruct((NA_HEADS * nd, 128), F32),
                         compiler_params=_params(0))(flat, jnp.asarray(onehot))
    return out[:, :nc].reshape(NA_HEADS, nd, nc)


def _rope_consts(s):
    pos = np.arange(s, dtype=np.float32)
    inv = (1.0 / (ROPE_THETA ** (np.arange(0, ML_ROPE, 2, dtype=np.float32) / ML_ROPE))).astype(np.float32)
    ang = pos[:, None] * inv[None, :]
    cos, sin = np.cos(ang).astype(np.float32), np.sin(ang).astype(np.float32)
    half = ML_ROPE // 2
    rot = np.zeros((ML_ROPE, ML_ROPE), np.float32)
    rot[np.arange(half) + half, np.arange(half)] = -1.0
    rot[np.arange(half), np.arange(half) + half] = 1.0
    return (jnp.asarray(np.concatenate([cos, cos], 1)), jnp.asarray(np.concatenate([sin, sin], 1)),
            jnp.asarray(rot), jnp.asarray(rot.T.copy()))


def _rope(v, cos, sin, rot):
    return v * cos + jnp.dot(v, rot, precision=HI, preferred_element_type=F32) * sin


def _unrope(dv, cos, sin, rot_t):
    return dv * cos + jnp.dot(dv * sin, rot_t, precision=HI, preferred_element_type=F32)


def _rms(v, g):
    return v * lax.rsqrt(jnp.mean(v * v, axis=-1, keepdims=True) + EPS) * g


def _mla_prep(lat, gq, gkv, cos, sin, rot, tm=256):
    s, w = lat.shape
    tm = _tile(s, tm)

    def body(l_ref, gq_ref, gkv_ref, c_ref, s_ref, r_ref, cq_ref, ckv_ref, kr_ref):
        cq_ref[...] = _rms(l_ref[:, :ML_RANK], gq_ref[...]).astype(BF)
        ckv_ref[...] = _rms(l_ref[:, ML_RANK:2 * ML_RANK], gkv_ref[...]).astype(BF)
        kr_ref[...] = _rope(l_ref[:, 2 * ML_RANK:], c_ref[...], s_ref[...], r_ref[...]).astype(BF)

    row = lambda c: pl.BlockSpec((tm, c), lambda i: (i, 0))
    full = lambda a: pl.BlockSpec(a.shape, lambda i: (0, 0))
    return pl.pallas_call(
        body, name="mla_prep", grid=(s // tm,),
        in_specs=[row(w), full(gq), full(gkv), row(ML_ROPE), row(ML_ROPE), full(rot)],
        out_specs=[row(ML_RANK), row(ML_RANK), row(ML_ROPE)],
        out_shape=[jax.ShapeDtypeStruct((s, ML_RANK), BF), jax.ShapeDtypeStruct((s, ML_RANK), BF),
                   jax.ShapeDtypeStruct((s, ML_ROPE), BF)],
        compiler_params=_params(1))(lat, gq, gkv, cos, sin, rot)


def _mla_q_proj(cq, wuq, cos, sin, rot, tm=512, comm=()):
    s, k = cq.shape
    tm = _tile(s, tm)

    def epi(accs, ex, out):
        acc = accs[0]
        out[0][:, :ML_NOPE] = acc[:, :ML_NOPE].astype(BF)
        out[0][:, ML_NOPE:] = _rope(acc[:, ML_NOPE:], ex[0][...], ex[1][...], ex[2][...]).astype(BF)

    rmap = lambda j, i: (i, 0)
    return _mm("mla_q_proj", (ML_HEADS, s // tm),
               [(cq, (tm, k), rmap, wuq, (None, ML_QK, k), lambda j, i: (j, 0, 0), "nt", 0, 0)],
               [(cos, (tm, ML_ROPE), rmap), (sin, (tm, ML_ROPE), rmap), (rot, rot.shape, lambda j, i: (0, 0))],
               [((ML_HEADS, s, ML_QK), BF, (None, tm, ML_QK), lambda j, i: (j, i, 0))], epi, comm=comm)[0]


def _mla_kv_proj(ckv, wukv, kr, tm=512, comm=()):
    s, k = ckv.shape
    tm = _tile(s, tm)

    def epi(accs, ex, out):
        acc = accs[0]
        out[0][:, :ML_NOPE] = acc[:, :ML_NOPE].astype(BF)
        out[0][:, ML_NOPE:] = ex[0][...]
        out[1][...] = acc[:, ML_NOPE:].astype(BF)

    rmap = lambda j, i: (i, 0)
    gmap = lambda j, i: (j, i, 0)
    return _mm("mla_kv_proj", (ML_HEADS, s // tm),
               [(ckv, (tm, k), rmap, wukv, (None, k, ML_NOPE + ML_V), lambda j, i: (j, 0, 0), "nn", 0, 0)],
               [(kr, (tm, ML_ROPE), rmap)],
               [((ML_HEADS, s, ML_QK), BF, (None, tm, ML_QK), gmap), ((ML_HEADS, s, ML_V), BF, (None, tm, ML_V), gmap)],
               epi, comm=comm)


def _mla_probs(q, k):
    sc = lax.dot_general(q, k, _DN["nt"], preferred_element_type=F32) * (ML_QK ** -0.5)
    e = jnp.exp(sc - jnp.max(sc, axis=-1, keepdims=True))
    return e / jnp.sum(e, axis=-1, keepdims=True)


def _mla_fwd(q, k, v, tq=1024, comm=()):
    _, s, _ = q.shape
    tq = _tile(s, tq)

    def body(q_ref, k_ref, v_ref, o_ref):
        p = _mla_probs(q_ref[...], k_ref[...])
        o_ref[...] = jnp.dot(p.astype(BF), v_ref[...], preferred_element_type=F32).astype(BF)

    return _call("mla_fwd", body, (ML_HEADS, s // tq),
                 [pl.BlockSpec((None, tq, ML_QK), lambda h, i: (h, i, 0)),
                  pl.BlockSpec((None, s, ML_QK), lambda h, i: (h, 0, 0)),
                  pl.BlockSpec((None, s, ML_V), lambda h, i: (h, 0, 0))],
                 pl.BlockSpec((tq, ML_V), lambda h, i: (i, h)),
                 jax.ShapeDtypeStruct((s, ML_HEADS * ML_V), BF), [q, k, v], comm)


def _mla_bwd(q, k, v, do, tq=1024, comm=()):
    _, s, _ = q.shape
    tq = _tile(s, tq)

    def body(q_ref, k_ref, v_ref, do_ref, dq_ref, dk_ref, dv_ref):
        i = pl.program_id(1)
        qv, kv, vv, dov = q_ref[...], k_ref[...], v_ref[...], do_ref[...]
        p = _mla_probs(qv, kv)
        dp = lax.dot_general(dov, vv, _DN["nt"], preferred_element_type=F32)
        ds = (p * (dp - jnp.sum(p * dp, axis=-1, keepdims=True)) * (ML_QK ** -0.5)).astype(BF)
        dq_ref[...] = jnp.dot(ds, kv, preferred_element_type=F32)
        _acc_rows(dk_ref, lax.dot_general(ds, qv, _DN["tn"], preferred_element_type=F32), i)
        _acc_rows(dv_ref, lax.dot_general(p.astype(BF), dov, _DN["tn"], preferred_element_type=F32), i)

    return _call(
        "mla_bwd", body, (ML_HEADS, s // tq),
        [pl.BlockSpec((None, tq, ML_QK), lambda h, i: (h, i, 0)),
         pl.BlockSpec((None, s, ML_QK), lambda h, i: (h, 0, 0)),
         pl.BlockSpec((None, s, ML_V), lambda h, i: (h, 0, 0)),
         pl.BlockSpec((tq, ML_V), lambda h, i: (i, h))],
        [pl.BlockSpec((None, tq, ML_QK), lambda h, i: (h, i, 0)),
         pl.BlockSpec((None, s, ML_QK), lambda h, i: (h, 0, 0)),
         pl.BlockSpec((None, s, ML_V), lambda h, i: (h, 0, 0))],
        [jax.ShapeDtypeStruct((ML_HEADS, s, ML_QK), F32), jax.ShapeDtypeStruct((ML_HEADS, s, ML_QK), F32),
         jax.ShapeDtypeStruct((ML_HEADS, s, ML_V), F32)],
        [q, k, v, do], comm)


def _mla_post(dq, dk, dv, cos, sin, rot_t, tm=1024):
    _, s, _ = dq.shape
    tm = _tile(s, tm)

    def body(dq_ref, dk_ref, dv_ref, c_ref, s_ref, r_ref, dqp_ref, dkv_ref, dkr_ref):
        h = pl.program_id(1)
        dqv, dkk = dq_ref[...], dk_ref[...]
        dqp_ref[:, :ML_NOPE] = dqv[:, :ML_NOPE].astype(BF)
        dqp_ref[:, ML_NOPE:] = _unrope(dqv[:, ML_NOPE:], c_ref[...], s_ref[...], r_ref[...]).astype(BF)
        dkv_ref[:, :ML_NOPE] = dkk[:, :ML_NOPE].astype(BF)
        dkv_ref[:, ML_NOPE:] = dv_ref[...].astype(BF)
        _acc_rows(dkr_ref, dkk[:, ML_NOPE:], h)

    gspec = lambda c: pl.BlockSpec((None, tm, c), lambda i, h: (h, i, 0))
    rspec = pl.BlockSpec((tm, ML_ROPE), lambda i, h: (i, 0))
    return pl.pallas_call(
        body, name="mla_post", grid=(s // tm, ML_HEADS),
        in_specs=[gspec(ML_QK), gspec(ML_QK), gspec(ML_V), rspec, rspec,
                  pl.BlockSpec(rot_t.shape, lambda i, h: (0, 0))],
        out_specs=[gspec(ML_QK), gspec(ML_NOPE + ML_V), rspec],
        out_shape=[jax.ShapeDtypeStruct((ML_HEADS, s, ML_QK), BF),
                   jax.ShapeDtypeStruct((ML_HEADS, s, ML_NOPE + ML_V), BF),
                   jax.ShapeDtypeStruct((s, ML_ROPE), F32)],
        compiler_params=_params(2))(dq, dk, dv, cos, sin, rot_t)


def _mla_lat_bwd(dcq, dckv, dkr, lat, gq, gkv, cos, sin, rot_t, tm=256):
    s, w = lat.shape
    tm = _tile(s, tm)

    def body(dcq_ref, dckv_ref, dkr_ref, l_ref, gq_ref, gkv_ref, c_ref, s_ref, r_ref, dl_ref, dgq_ref, dgkv_ref):
        i = pl.program_id(0)
        dql, pq = _rms_bwd_math(dcq_ref[...], l_ref[:, :ML_RANK], gq_ref[...])
        dkl, pkv = _rms_bwd_math(dckv_ref[...], l_ref[:, ML_RANK:2 * ML_RANK], gkv_ref[...])
        dl_ref[:, :ML_RANK] = dql.astype(BF)
        dl_ref[:, ML_RANK:2 * ML_RANK] = dkl.astype(BF)
        dl_ref[:, 2 * ML_RANK:] = _unrope(dkr_ref[...], c_ref[...], s_ref[...], r_ref[...]).astype(BF)
        _acc_rows(dgq_ref, pq, i)
        _acc_rows(dgkv_ref, pkv, i)

    row = lambda c: pl.BlockSpec((tm, c), lambda i: (i, 0))
    full = lambda a: pl.BlockSpec(a.shape, lambda i: (0, 0))
    return pl.pallas_call(
        body, name="mla_lat_bwd", grid=(s // tm,),
        in_specs=[row(ML_RANK), row(ML_RANK), row(ML_ROPE), row(w), full(gq), full(gkv), row(ML_ROPE), row(ML_ROPE),
                  full(rot_t)],
        out_specs=[row(w), full(gq), full(gkv)],
        out_shape=[jax.ShapeDtypeStruct((s, w), BF), jax.ShapeDtypeStruct(gq.shape, F32),
                   jax.ShapeDtypeStruct(gkv.shape, F32)],
        compiler_params=_params(1))(dcq, dckv, dkr, lat, gq, gkv, cos, sin, rot_t)


def _grp_dw(name, a, dout, ta=1024):
    s, k = a.shape
    ta = _tile(k, ta)
    if dout.ndim == 3:
        g, _, nb = dout.shape
        b_blk, b_map = (None, s, nb), lambda j, i: (j, 0, 0)
    else:
        g, nb = NDEV, dout.shape[1] // NDEV
        b_blk, b_map = (s, nb), lambda j, i: (0, j)
    return _mm(name, (g, k // ta),
               [(a, (s, ta), lambda j, i: (0, i), dout, b_blk, b_map, "tn", 0, 0)], [],
               [((g, k, nb), BF, (None, ta, nb), lambda j, i: (j, i, 0))], _store)[0]


def _grp_dw_t(name, dout, a, ta=512):
    g, s, nb = dout.shape
    k = a.shape[1]
    ta = _tile(k, ta)
    return _mm(name, (g, k // ta),
               [(dout, (None, s, nb), lambda j, i: (j, 0, 0), a, (s, ta), lambda j, i: (0, i), "tn", 0, 0)], [],
               [((g, nb, k), BF, (None, nb, ta), lambda j, i: (j, 0, i))], _store)[0]


def _grp_dx_t(name, dout, wt, tm=512, tn=512, comm=()):
    g, s, nb = dout.shape
    k = wt.shape[2]
    tm, tn = _tile(s, tm), _tile(k, tn)
    return _mm(name, (k // tn, s // tm),
               [(dout, (g, tm, nb), lambda j, i: (0, i, 0), wt, (g, nb, tn), lambda j, i: (0, 0, j), "nn", 0, g)], [],
               [((s, k), F32, (tm, tn), lambda j, i: (i, j))], _store, comm=comm)[0]


def _grp_dx(name, dout, w, tm=512, tn=512, out_dtype=F32, comm=()):
    g, s, nb = dout.shape
    k = w.shape[1]
    tm, tn = _tile(s, tm), _tile(k, tn)
    return _mm(name, (k // tn, s // tm),
               [(dout, (g, tm, nb), lambda j, i: (0, i, 0), w, (g, tn, nb), lambda j, i: (0, j, 0), "nt", 0, g)], [],
               [((s, k), out_dtype, (tm, tn), lambda j, i: (i, j))], _store, comm=comm)[0]


def _row_dw(name, a, dout, tn=2048):
    s, n = dout.shape
    tn = _tile(n, tn)
    if a.ndim == 3:
        kb = a.shape[2]
        a_blk, a_map = (None, s, kb), lambda j, i: (j, 0, 0)
    else:
        kb = a.shape[1] // NDEV
        a_blk, a_map = (s, kb), lambda j, i: (0, j)
    return _mm(name, (NDEV, n // tn),
               [(a, a_blk, a_map, dout, (s, tn), lambda j, i: (0, i), "tn", 0, 0)], [],
               [((NDEV, kb, n), BF, (None, kb, tn), lambda j, i: (j, 0, i))], _store)[0]


def _mix_merge(oa, ob, wa, wb, ga, gb, tm=1024, comm=()):
    s, k = oa.shape
    g, _, nb = wa.shape
    tm = _tile(s, tm)

    def epi(accs, ex, out):
        ya, yb = accs
        out[0][...] = ya.astype(BF)
        out[1][...] = yb.astype(BF)
        out[2][...] = (_sig(ex[0][...]) * ya + _sig(ex[1][...]) * yb).astype(BF)

    rmap = lambda j, i: (i, 0)
    wmap = lambda j, i: (j, 0, 0)
    o = ((g, s, nb), BF, (None, tm, nb), lambda j, i: (j, i, 0))
    cmap = lambda j, i: (i, j)
    return _mm("mix_merge", (g, s // tm),
               [(oa, (tm, k), rmap, wa, (None, k, nb), wmap, "nn", 0, 0),
                (ob, (tm, k), rmap, wb, (None, k, nb), wmap, "nn", 1, 0)],
               [(ga, (tm, nb), cmap), (gb, (tm, nb), cmap)], [o, o, o], epi, nacc=2, comm=comm)


def _mix_out(merged, wout, resid, tm=1024, tn=512):
    g, s, kb = merged.shape
    d = wout.shape[2]
    tm, tn = _tile(s, tm), _tile(d, tn)

    def epi(accs, ex, out):
        out[0][...] = ex[0][...] + accs[0]

    return _mm("mix_out", (d // tn, s // tm),
               [(merged, (g, tm, kb), lambda j, i: (0, i, 0), wout, (g, kb, tn), lambda j, i: (0, 0, j), "nn", 0, g)],
               [(resid, (tm, tn), lambda j, i: (i, j))],
               [((s, d), F32, (tm, tn), lambda j, i: (i, j))], epi)[0]


def _mix_out_bwd(dh, wout, ga, gb, ya, yb, tm=1024, comm=()):
    s, d = dh.shape
    g, kb, _ = wout.shape
    tm = _tile(s, tm)

    def epi(accs, ex, out):
        dm = accs[0]
        sa, sb = _sig(ex[0][...]), _sig(ex[1][...])
        out[0][...] = (dm * sa).astype(BF)
        out[1][...] = (dm * sb).astype(BF)
        out[2][...] = (dm * ex[2][...].astype(F32) * sa * (1.0 - sa)).astype(BF)
        out[3][...] = (dm * ex[3][...].astype(F32) * sb * (1.0 - sb)).astype(BF)

    cmap = lambda j, i: (i, j)
    gmap = lambda j, i: (j, i, 0)
    og = ((g, s, kb), BF, (None, tm, kb), gmap)
    oc = ((s, g * kb), BF, (tm, kb), cmap)
    return _mm("mix_out_bwd", (g, s // tm),
               [(dh, (tm, d), lambda j, i: (i, 0), wout, (None, kb, d), lambda j, i: (j, 0, 0), "nt", 0, 0)],
               [(ga, (tm, kb), cmap), (gb, (tm, kb), cmap), (ya, (None, tm, kb), gmap), (yb, (None, tm, kb), gmap)],
               [og, og, oc, oc], epi, comm=comm)


def _pl_forward(n4, wplg, p, wpl, h3, tm=1024):
    s, d = n4.shape
    g, kb, _ = wplg.shape
    kp, nb = wpl.shape[1], wpl.shape[2]
    tm = _tile(s, tm)
    wplg_nat = wplg.reshape(g * kb, d)

    def epi(accs, ex, out):
        t, pe = accs
        out[0][...] = ex[0][...] + _sig(t) * pe
        out[1][...] = t
        out[2][...] = pe.astype(BF)

    rmap = lambda j, i: (i, 0)
    cmap = lambda j, i: (i, j)
    return _mm("pl_forward", (g, s // tm),
               [(n4, (tm, d), rmap, wplg_nat, (g * kb, nb), lambda j, i: (0, j), "nn", 0, 0),
                (p, (tm, kp), rmap, wpl, (None, kp, nb), lambda j, i: (j, 0, 0), "nn", 1, 0)],
               [(h3, (tm, nb), cmap)],
               [((s, d), F32, (tm, nb), cmap), ((s, d), F32, (tm, nb), cmap), ((s, d), BF, (tm, nb), cmap)],
               epi, nacc=2)


def _row_dx(name, dout, w, tm=1024, comm=()):
    s, n = dout.shape
    g, kb, _ = w.shape
    tm = _tile(s, tm)
    return _mm(name, (g, s // tm),
               [(dout, (tm, n), lambda j, i: (i, 0), w, (None, kb, n), lambda j, i: (j, 0, 0), "nt", 0, 0)], [],
               [((s, g * kb), F32, (tm, kb), lambda j, i: (i, j))], _store, comm=comm)[0]


def _in_proj_bwd_x(pieces, weights, tm=512, tn=512, comm=()):
    s = pieces[0].shape[0]
    d = weights[0].shape[1]
    tm, tn = _tile(s, tm), _tile(d, tn)
    prods = [(pc, (tm, pc.shape[1]), lambda j, i: (i, 0), w, (pc.shape[1], tn), lambda j, i: (0, j), "nn", 0, 0)
             for pc, w in zip(pieces, weights)]
    return _mm("in_proj_dx", (d // tn, s // tm), prods, [],
               [((s, d), F32, (tm, tn), lambda j, i: (i, j))], _store, comm=comm)[0]


def _split_w_in(w_in_t):
    g, nb, d = w_in_t.shape
    nat = w_in_t.reshape(g * nb, d)
    na, lat = 3 * NA_HEADS * NA_DIM, 2 * ML_RANK + ML_ROPE
    return nat, nat[na:na + lat], nat[na + lat:na + lat + d], nat[na + lat + d:]


def _pair_sum(name, part, landed, core):
    _, _, r, c = part.shape
    tr, tc = _ew_tile(r, c)

    def body(core_ref, a_ref, b_ref, o_ref):
        o_ref[...] = (a_ref[...].astype(F32) + b_ref[...].astype(F32)).astype(o_ref.dtype)

    return pl.pallas_call(
        body, name=name,
        grid_spec=pltpu.PrefetchScalarGridSpec(
            num_scalar_prefetch=1, grid=(NCHIP, r // tr, c // tc),
            in_specs=[pl.BlockSpec((None, None, tr, tc), lambda j, i, k, core_ref: (j, core_ref[0], i, k)),
                      pl.BlockSpec((None, tr, tc), lambda j, i, k, core_ref: (j, i, k))],
            out_specs=pl.BlockSpec((None, tr, tc), lambda j, i, k, core_ref: (j, i, k))),
        out_shape=jax.ShapeDtypeStruct(landed.shape, landed.dtype), compiler_params=_params(3),
    )(core, part, landed)


def _device_step(x, p, target, sp, own, core):
    s, d = x.shape
    rows = s // GRID_W
    cos, sin, rot, rot_t = _rope_consts(s)
    w, dw4, sums, dsp, pending = {}, {}, {}, {}, []

    def gather(*names):
        return _GatherPart(names, [own[n] for n in names])

    def got(part):
        w.update(zip(part.names, part.results))

    def grad(name, g):
        dw4[name] = g.reshape((NCHIP, 2) + g.shape[1:])

    def to_sibling(*names):
        return _SiblingPart(names, [dw4[n] for n in names])

    def add_pairs(part):
        for n, landed in zip(part.names, part.results):
            sums[n] = _pair_sum("pair_sum_" + n, dw4[n], landed, core)

    def start_chips(tag, *names):
        send, recv, thru, lands, token = _chips_start("rs_start_" + tag, [sums[n] for n in names])
        pending.append((tag, names, send, recv, thru, lands))
        return token

    c0 = gather("ffn1_w_gate", "ffn1_w_up")
    c1 = gather("ffn1_w_down")
    c2 = gather("w_in")

    def ffn1_wgu():
        got(c0)
        return w["ffn1_w_gate"], w["ffn1_w_up"]

    def ffn1_wd():
        got(c1)
        return w["ffn1_w_down"]

    h1, ffn1_saved = _ffn_forward("ffn1", x, sp["ffn1_norm"], ffn1_wgu, ffn1_wd,
                                  norm_comm=[c0], up_comm=[c1], down_comm=[c2])
    got(c2)
    wqkv, wlat, wga, wgb = _split_w_in(w["w_in"])
    u = _rms_fwd("mix_norm", h1, sp["mix_norm"])
    c3 = gather("w_uq", "w_ukv")
    qkv = _mm_nt("in_qkv", u, wqkv, BF, tn=1024, comm=[c3], rows=3 * NA_HEADS * NA_DIM)
    got(c3)
    lat = _mm_nt("in_lat", u, wlat, F32, tm=1024)
    c3a = gather("w_branch_a")
    ga = _mm_nt("in_ga", u, wga, F32, tn=1024, comm=[c3a])
    got(c3a)
    c3b = gather("w_branch_b")
    gb = _mm_nt("in_gb", u, wgb, F32, tn=1024, comm=[c3b])
    got(c3b)
    tb = _na_table(sp["na_rpb"], rows)
    c4 = gather("ffn2_w_gate")
    oa = _na_fwd(qkv, tb, comm=[c4])
    got(c4)
    cq, ckv, kr = _mla_prep(lat, sp["q_a_norm"], sp["kv_a_norm"], cos, sin, rot)
    c4a = gather("w_out")
    qf = _mla_q_proj(cq, w["w_uq"], cos, sin, rot, comm=[c4a])
    got(c4a)
    kf, vf = _mla_kv_proj(ckv, w["w_ukv"], kr)
    c5 = gather("ffn2_w_up")
    ob = _mla_fwd(qf, kf, vf, comm=[c5])
    got(c5)
    c5a = gather("w_pl", "w_pl_gate")
    ya, yb, merged = _mix_merge(oa, ob, w["w_branch_a"], w["w_branch_b"], ga, gb, comm=[c5a])
    got(c5a)
    h2 = _mix_out(merged, w["w_out"], h1)
    c6 = gather("ffn2_w_down")

    def ffn2_wd():
        got(c6)
        return w["ffn2_w_down"]

    h3, ffn2_saved = _ffn_forward("ffn2", h2, sp["ffn2_norm"], lambda: (w["ffn2_w_gate"], w["ffn2_w_up"]), ffn2_wd,
                                  up_comm=[c6])
    n4 = _rms_fwd("pl_norm", h3, sp["pl_norm"])
    pb = p.astype(BF)
    h4, t, pe = _pl_forward(n4, w["w_pl_gate"], pb, w["w_pl"], h3)

    dh4, dsp["final_norm"], loss = _loss_head(h4, target, sp["final_norm"])
    dt, dpe = _pl_bwd_elem(dh4, pe, t)
    grad("w_pl", _grp_dw("pl_dw", pb, dpe))
    grad("w_pl_gate", _row_dw("plg_dw", n4, dt))
    s1 = to_sibling("w_pl", "w_pl_gate")
    dn4 = _row_dx("plg_dx", dt, w["w_pl_gate"], comm=[s1])
    add_pairs(s1)
    dh3, dhb, dsp["pl_norm"] = _rms_bwd("pl_dnorm", dn4, h3, sp["pl_norm"], dh4)

    xn, hg, hu, a = ffn2_saved
    grad("ffn2_w_down", _ffn_bwd_wd("ffn2_dwd", a, dhb))
    s2 = to_sibling("ffn2_w_down")
    dhg, dhu = _ffn_bwd_act("ffn2_dact", dhb, w["ffn2_w_down"], hg, hu, comm=[s2])
    add_pairs(s2)
    tok = start_chips("ffn2_down", "w_pl", "w_pl_gate", "ffn2_w_down")
    dwg, dwu = _ffn_bwd_wup("ffn2_dwup", xn, dhg, dhu, comm=[_After(tok)])
    grad("ffn2_w_gate", dwg)
    grad("ffn2_w_up", dwu)
    s3 = to_sibling("ffn2_w_gate", "ffn2_w_up")
    dxn = _ffn_bwd_x("ffn2_dx", dhg, dhu, w["ffn2_w_gate"], w["ffn2_w_up"], comm=[s3])
    add_pairs(s3)
    tok = start_chips("ffn2_up", "ffn2_w_gate", "ffn2_w_up")
    dh2, dh2b, dsp["ffn2_norm"] = _rms_bwd("ffn2_dnorm", dxn, h2, sp["ffn2_norm"], dh3, comm=[_After(tok)])

    grad("w_out", _row_dw("out_dw", merged, dh2b))
    s4 = to_sibling("w_out")
    dya, dyb, dga, dgb = _mix_out_bwd(dh2b, w["w_out"], ga, gb, ya, yb, comm=[s4])
    add_pairs(s4)
    grad("w_branch_a", _grp_dw("bra_dw", oa, dya))
    grad("w_branch_b", _grp_dw("brb_dw", ob, dyb))
    doa = _grp_dx("bra_dx", dya, w["w_branch_a"], out_dtype=BF)
    s5 = to_sibling("w_branch_a", "w_branch_b")
    dob = _grp_dx("brb_dx", dyb, w["w_branch_b"], out_dtype=BF, comm=[s5])
    add_pairs(s5)

    dqf, dkf, dvf = _mla_bwd(qf, kf, vf, dob)
    dqp, dkv, dkr = _mla_post(dqf, dkf, dvf, cos, sin, rot_t)
    grad("w_uq", _grp_dw_t("uq_dw", dqp, cq))
    grad("w_ukv", _grp_dw("ukv_dw", ckv, dkv))
    dcq = _grp_dx_t("uq_dx", dqp, w["w_uq"])
    s6 = to_sibling("w_uq", "w_ukv")
    dckv = _grp_dx("ukv_dx", dkv, w["w_ukv"], comm=[s6])
    add_pairs(s6)
    dlat, dsp["q_a_norm"], dsp["kv_a_norm"] = _mla_lat_bwd(dcq, dckv, dkr, lat, sp["q_a_norm"], sp["kv_a_norm"],
                                                         cos, sin, rot_t)
    dq_na, dk_na, dv_na, dtab = _na_bwd(qkv, tb, doa)
    dsp["na_rpb"] = _na_rpb_grad(dtab, rows)
    dqkv = jnp.concatenate([dq_na, dk_na.astype(BF), dv_na.astype(BF)], axis=1)

    pieces = [dqkv, dlat, dga, dgb]
    dwin = jnp.zeros((sum(pc.shape[1] for pc in pieces), d), BF)
    row0 = 0
    for i, pc in enumerate(pieces):
        dwin = _mm_tn_into("in_dw%d" % i, pc, u, dwin, row0)
        row0 += pc.shape[1]
    grad("w_in", dwin.reshape(NDEV, -1, d))
    s7 = to_sibling("w_in")
    du = _in_proj_bwd_x(pieces, [wqkv, wlat, wga, wgb], comm=[s7])
    add_pairs(s7)
    tok = start_chips("w_in", "w_out", "w_branch_a", "w_branch_b", "w_uq", "w_ukv", "w_in")
    dh1, dhb, dsp["mix_norm"] = _rms_bwd("mix_dnorm", du, h1, sp["mix_norm"], dh2, comm=[_After(tok)])

    xn, hg, hu, a = ffn1_saved
    grad("ffn1_w_down", _ffn_bwd_wd("ffn1_dwd", a, dhb))
    s8 = to_sibling("ffn1_w_down")
    dhg, dhu = _ffn_bwd_act("ffn1_dact", dhb, w["ffn1_w_down"], hg, hu, comm=[s8])
    add_pairs(s8)
    tok = start_chips("ffn1_down", "ffn1_w_down")
    dwg, dwu = _ffn_bwd_wup("ffn1_dwup", xn, dhg, dhu, comm=[_After(tok)])
    grad("ffn1_w_gate", dwg)
    grad("ffn1_w_up", dwu)
    s9 = to_sibling("ffn1_w_gate", "ffn1_w_up")
    _comm_only("rs_sibling_ffn1", [s9])
    add_pairs(s9)
    tok = start_chips("ffn1_up", "ffn1_w_gate", "ffn1_w_up")
    dxn = _ffn_bwd_x("ffn1_dx", dhg, dhu, w["ffn1_w_gate"], w["ffn1_w_up"], comm=[_After(tok)])
    dx, _, dsp["ffn1_norm"] = _rms_bwd("ffn1_dnorm", dxn, x, sp["ffn1_norm"], dh1)
    return loss, dx, pending, dsp


def _small_peers():
    x, y, c = _coords()
    return [(x ^ ((k >> 2) & 1), y ^ ((k >> 1) & 1), c ^ (k & 1)) for k in range(1, NDEV)]


def _small_start(buf):
    def body(b_ref, z_ref, send_sems, recv_sems, b_thru, z_thru, token):
        x, y, c = _coords()
        for k, peer in enumerate(_small_peers()):
            pltpu.make_async_remote_copy(
                src_ref=b_ref, dst_ref=z_ref.at[4 * x + 2 * y + c], send_sem=send_sems.at[k],
                recv_sem=recv_sems.at[k], device_id=peer, device_id_type=MESH).start()
        token[...] = jnp.zeros_like(token)

    zone = lax.empty((NDEV,) + buf.shape, buf.dtype)
    res = pl.pallas_call(
        body, name="small_start", in_specs=[HBM, HBM],
        out_specs=(SEM, SEM, HBM, HBM, pl.BlockSpec(memory_space=pltpu.VMEM)),
        out_shape=(pltpu.SemaphoreType.DMA((NDEV - 1,)), pltpu.SemaphoreType.DMA((NDEV - 1,)),
                   pltpu.HBM(buf.shape, buf.dtype), pltpu.HBM(zone.shape, zone.dtype),
                   jax.ShapeDtypeStruct((8, 128), F32)),
        input_output_aliases={0: 2, 1: 3},
        compiler_params=pltpu.CompilerParams(has_side_effects=pltpu.SideEffectType.DATAFLOW_SIDE_EFFECTING),
    )(pltpu.with_memory_space_constraint(buf, pltpu.HBM), pltpu.with_memory_space_constraint(zone, pltpu.HBM))
    return res


def _small_wait(send_sems, recv_sems, buf, zone, after):
    def body(b_ref, z_ref, send, recv, after_ref, b_out, z_out):
        for k, (px, py, pc) in enumerate(_small_peers()):
            cp = pltpu.make_async_remote_copy(
                src_ref=b_ref, dst_ref=z_ref.at[4 * px + 2 * py + pc], send_sem=send.at[k], recv_sem=recv.at[k],
                device_id=(px, py, pc), device_id_type=MESH)
            cp.wait_send()
            cp.wait_recv()

    return pl.pallas_call(
        body, name="small_wait", in_specs=[HBM, HBM, SEM, SEM, ANY], out_specs=[HBM, HBM],
        out_shape=[pltpu.HBM(buf.shape, buf.dtype), pltpu.HBM(zone.shape, zone.dtype)],
        input_output_aliases={0: 0, 1: 1},
        compiler_params=pltpu.CompilerParams(has_side_effects=pltpu.SideEffectType.DATAFLOW_SIDE_EFFECTING),
    )(buf, zone, send_sems, recv_sems, after)


def _adam_math(wv, g, m, v):
    m_new = B1 * m + (1.0 - B1) * g
    v_new = B2 * v + (1.0 - B2) * (g * g)
    m_hat = m_new / (1.0 - B1 ** STEP)
    v_hat = v_new / (1.0 - B2 ** STEP)
    return -LR * (m_hat / (jnp.sqrt(v_hat) + ADAM_EPS) + WD * wv), m_new, v_new


def _adam_replicated(name, zone, own, wv, m, v, me):
    ndev, r, c = zone.shape

    def body(me_ref, z_ref, o_ref, w_ref, m_ref, v_ref, g_ref, d_ref, mo_ref, vo_ref):
        g = jnp.zeros((r, c), F32)
        for d in range(ndev):
            g = g + jnp.where(me_ref[0] == d, o_ref[...], z_ref[d])
        g_ref[...] = g
        d_ref[...], mo_ref[...], vo_ref[...] = _adam_math(w_ref[...], g, m_ref[...], v_ref[...])

    blk = pl.BlockSpec((r, c), lambda i, me_ref: (0, 0))
    return pl.pallas_call(
        body, name=name,
        grid_spec=pltpu.PrefetchScalarGridSpec(
            num_scalar_prefetch=1, grid=(1,),
            in_specs=[pl.BlockSpec((ndev, r, c), lambda i, me_ref: (0, 0, 0)), blk, blk, blk, blk],
            out_specs=[blk] * 4),
        out_shape=[jax.ShapeDtypeStruct((r, c), F32)] * 4, compiler_params=_params(1),
    )(me, zone, own, wv, m, v)


def _adam_exchanged(name, sums, land, wv, m, v, my_chip):
    _, r, c = sums.shape
    tr, tc = _ew_tile(r, c)

    def body(chip_ref, s_ref, l_ref, w_ref, m_ref, v_ref, g_ref, d_ref, mo_ref, vo_ref):
        g = s_ref[...].astype(F32)
        for j in range(3):
            g = g + l_ref[j].astype(F32)
        g_ref[...] = g
        d_ref[...], mo_ref[...], vo_ref[...] = _adam_math(w_ref[...], g, m_ref[...], v_ref[...])

    blk = pl.BlockSpec((tr, tc), lambda i, k, chip_ref: (i, k))
    return pl.pallas_call(
        body, name=name,
        grid_spec=pltpu.PrefetchScalarGridSpec(
            num_scalar_prefetch=1, grid=(r // tr, c // tc),
            in_specs=[pl.BlockSpec((None, tr, tc), lambda i, k, chip_ref: (chip_ref[0], i, k)),
                      pl.BlockSpec((3, tr, tc), lambda i, k, chip_ref: (0, i, k)), blk, blk, blk],
            out_specs=[blk] * 4),
        out_shape=[jax.ShapeDtypeStruct((r, c), F32)] * 4, compiler_params=_params(2),
    )(my_chip, sums, land, wv, m, v)


SHARDED = ("ffn1_w_gate", "ffn1_w_up", "ffn1_w_down", "w_in", "w_uq", "w_ukv", "w_branch_a", "w_branch_b", "w_out",
           "ffn2_w_gate", "ffn2_w_up", "ffn2_w_down", "w_pl", "w_pl_gate")
TRANSPOSED = ("ffn1_w_gate", "ffn1_w_up", "ffn2_w_gate", "ffn2_w_up", "w_in", "w_uq")
REPLICATED = ("ffn1_norm", "mix_norm", "q_a_norm", "kv_a_norm", "na_rpb", "ffn2_norm", "pl_norm", "final_norm")
WEIGHTS = ("ffn1_norm", "ffn1_w_gate", "ffn1_w_up", "ffn1_w_down", "mix_norm", "w_in", "q_a_norm", "w_uq",
           "kv_a_norm", "w_ukv", "na_rpb", "w_branch_a", "w_branch_b", "w_out", "ffn2_norm", "ffn2_w_gate",
           "ffn2_w_up", "ffn2_w_down", "pl_norm", "w_pl", "w_pl_gate", "final_norm")
SMALL_W = 2048


def _pack_small(vals):
    rows = []
    for name in REPLICATED:
        flat = vals[name].reshape(-1).astype(F32)
        n = -(-flat.shape[0] // SMALL_W) * SMALL_W
        rows.append(jnp.pad(flat, (0, n - flat.shape[0])).reshape(-1, SMALL_W))
    return jnp.concatenate(rows, axis=0)


def _unpack_small(buf, shapes):
    out, r = {}, 0
    for name in REPLICATED:
        size = int(np.prod(shapes[name]))
        nrow = -(-size // SMALL_W)
        out[name] = buf[r:r + nrow].reshape(-1)[:size].reshape(shapes[name])
        r += nrow
    return out


def kernel(x, p, ffn1_norm, ffn1_w_gate, ffn1_w_up, ffn1_w_down, mix_norm, w_in, q_a_norm, w_uq, kv_a_norm, w_ukv, na_rpb, w_branch_a, w_branch_b, w_out, ffn2_norm, ffn2_w_gate, ffn2_w_up, ffn2_w_down, pl_norm, w_pl, w_pl_gate, final_norm, loss_target, m_ffn1_norm, m_ffn1_w_gate, m_ffn1_w_up, m_ffn1_w_down, m_mix_norm, m_w_in, m_q_a_norm, m_w_uq, m_kv_a_norm, m_w_ukv, m_na_rpb, m_w_branch_a, m_w_branch_b, m_w_out, m_ffn2_norm, m_ffn2_w_gate, m_ffn2_w_up, m_ffn2_w_down, m_pl_norm, m_w_pl, m_w_pl_gate, m_final_norm, v_ffn1_norm, v_ffn1_w_gate, v_ffn1_w_up, v_ffn1_w_down, v_mix_norm, v_w_in, v_q_a_norm, v_w_uq, v_kv_a_norm, v_w_ukv, v_na_rpb, v_w_branch_a, v_w_branch_b, v_w_out, v_ffn2_norm, v_ffn2_w_gate, v_ffn2_w_up, v_ffn2_w_down, v_pl_norm, v_w_pl, v_w_pl_gate, v_final_norm):
    args = dict(locals())
    wts = {n: args[n] for n in WEIGHTS}
    mom = {n: args["m_" + n] for n in WEIGHTS}
    var = {n: args["v_" + n] for n in WEIGHTS}
    shapes = {n: wts[n].shape for n in WEIGHTS}
    core = lax.axis_index("c").astype(jnp.int32).reshape(1)

    local = lambda n, a: a[0].T if n in TRANSPOSED else a[0]
    own = {n: local(n, wts[n]).astype(BF) for n in SHARDED}
    sp = {n: wts[n].reshape(1, -1) for n in REPLICATED if n != "na_rpb"}
    sp["na_rpb"] = wts["na_rpb"][0]
    loss_part, grad_x, pending, dsp = _device_step(x[0], p[0, 0], loss_target[0], sp, own, core)

    small = jnp.concatenate([_pack_small(dsp), jnp.pad(loss_part, ((0, 0), (0, SMALL_W - loss_part.shape[1])))], 0)
    pad_rows = -small.shape[0] % 8
    small = jnp.pad(small, ((0, pad_rows), (0, 0)))
    s_send, s_recv, s_buf, s_zone, last = _small_start(small)

    out = {}
    my_chip = (2 * lax.axis_index("x") + lax.axis_index("y")).astype(jnp.int32).reshape(1)
    for tag, names, send, recv, thru, lands in pending:
        thru, lands = _chips_wait("rs_wait_" + tag, send, recv, thru, lands, last)
        for n, s4, l3 in zip(names, thru, lands):
            res4 = _adam_exchanged("adam_" + n, s4, l3, local(n, wts[n]), local(n, mom[n]), local(n, var[n]), my_chip)
            out[n] = tuple((a.T if n in TRANSPOSED else a)[None] for a in res4)
            last = res4[1]

    s_buf, s_zone = _small_wait(s_send, s_recv, s_buf, s_zone, last)
    zeros = jnp.zeros((1 + pad_rows, SMALL_W), F32)
    pack = lambda d: jnp.concatenate([_pack_small(d), zeros], 0)
    me = 2 * my_chip + core
    g_s, d_s, m_s, v_s = _adam_replicated("adam_small", s_zone, s_buf, pack(wts), pack(mom), pack(var), me)
    n_rows = small.shape[0] - 1 - pad_rows
    loss = g_s[n_rows, 0]
    small_out = [_unpack_small(b, shapes) for b in (g_s, d_s, m_s, v_s)]
    for n in REPLICATED:
        out[n] = tuple(b[n] for b in small_out)

    res = [loss, grad_x[None]]
    for k in range(4):
        res += [out[n][k] for n in WEIGHTS]
    return tuple(res)
```

```python
import functools

import numpy as np
import jax
import jax.numpy as jnp
from jax import lax
from jax.experimental import pallas as pl
from jax.experimental.pallas import tpu as pltpu

F32 = jnp.float32
BF = jnp.bfloat16
MESH = pl.DeviceIdType.MESH

NDEV = 8
NCHIP = 4
VMEM_LIMIT = 56 * 1024 * 1024
EPS = 1e-6
NEG = -1e30
GRID_W = 64
NA_HEADS, NA_DIM = 8, 128
NA_ROWS_WIN, NA_COLS_WIN = 8, 16
NA_HG = 4
NA_QROWS = 4
ML_HEADS, ML_NOPE, ML_ROPE, ML_V = 8, 128, 64, 128
ML_QK = ML_NOPE + ML_ROPE
ML_RANK = 512
ROPE_THETA = 10000.0
LR, B1, B2, ADAM_EPS, WD, STEP = 0.001, 0.9, 0.999, 1e-08, 0.01, 10
HI = lax.Precision.HIGHEST

_DN = {"nn": (((1,), (0,)), ((), ())), "nt": (((1,), (1,)), ((), ())), "tn": (((0,), (0,)), ((), ()))}


def _params(n):
    return pltpu.CompilerParams(dimension_semantics=("arbitrary",) * n, vmem_limit_bytes=VMEM_LIMIT)


def _sig(v):
    return jax.nn.sigmoid(v)


ANY = pl.BlockSpec(memory_space=pl.ANY)


def _coords():
    return lax.axis_index("x"), lax.axis_index("y"), lax.axis_index("c")


class _Part:
    inputs, out_shapes, sem_shapes, results = (), (), (), None

    def mid(self, ins, outs, sems):
        pass

    def late(self, ins, outs, sems):
        pass


class _After(_Part):
    def __init__(self, token):
        self.inputs = [token]

    def start(self, ins, outs, sems):
        pass

    finish = start


class _GatherPart(_Part):
    def __init__(self, names, shards):
        n = len(shards)
        self.names, self.inputs = list(names), list(shards)
        self.out_shapes = [jax.ShapeDtypeStruct((NDEV,) + a.shape, a.dtype) for a in shards]
        self.sem_shapes = [pltpu.SemaphoreType.DMA((n, 7)), pltpu.SemaphoreType.DMA((n, 7)),
                           pltpu.SemaphoreType.DMA((n,))]

    def _plan(self, ins, outs, sems):
        send_sems, recv_sems, local_sems = sems
        x, y, c = _coords()
        me, sib, diag = (x, y, c), (x, y, 1 - c), (1 - x, 1 - y, c)
        n1, n2 = (x ^ (1 - c), y ^ c, c), (x ^ c, y ^ (1 - c), c)

        def copy(i, k, block, to, src=None):
            px, py, pc = block
            dst = outs[i].at[4 * px + 2 * py + pc]
            return pltpu.make_async_remote_copy(
                src_ref=dst if src is None else src, dst_ref=dst, send_sem=send_sems.at[i, k],
                recv_sem=recv_sems.at[i, k], device_id=to, device_id_type=MESH)

        mine = [pltpu.make_async_copy(ins[i], outs[i].at[4 * x + 2 * y + c], local_sems.at[i])
                for i in range(len(ins))]
        return copy, mine, me, sib, n1, n2, diag

    def _own_sends(self, ins, copy, me, sib, n1, n2):
        return [copy(i, k, me, to, src=ins[i]) for i in range(len(ins)) for k, to in enumerate((sib, n1, n2))]

    def start(self, ins, outs, sems):
        copy, mine, me, sib, n1, n2, _ = self._plan(ins, outs, sems)
        for cp in mine + self._own_sends(ins, copy, me, sib, n1, n2):
            cp.start()

    def mid(self, ins, outs, sems):
        copy, _, me, sib, n1, n2, _ = self._plan(ins, outs, sems)
        for i in range(len(ins)):
            copy(i, 1, n1, me).wait_recv()
            copy(i, 3, n1, n2).start()
            copy(i, 4, n1, sib).start()

    def late(self, ins, outs, sems):
        copy, _, me, sib, _, n2, diag = self._plan(ins, outs, sems)
        for i in range(len(ins)):
            copy(i, 2, n2, me).wait_recv()
            copy(i, 5, n2, sib).start()
        for i in range(len(ins)):
            copy(i, 3, diag, me).wait_recv()
            copy(i, 6, diag, sib).start()

    def finish(self, ins, outs, sems):
        copy, mine, me, sib, n1, n2, diag = self._plan(ins, outs, sems)
        other = lambda dev: (dev[0], dev[1], sib[2])
        n = len(ins)
        for i in range(n):
            copy(i, 0, sib, me).wait_recv()
            for k, block in ((4, other(n2)), (5, other(n1)), (6, other(diag))):
                copy(i, k, block, me).wait_recv()
        for cp in self._own_sends(ins, copy, me, sib, n1, n2):
            cp.wait_send()
        for i in range(n):
            for k, block in ((3, n1), (4, n1), (5, n2), (6, diag)):
                copy(i, k, block, sib).wait_send()
        for cp in mine:
            cp.wait()


class _SiblingPart(_Part):
    def __init__(self, names, parts):
        n = len(parts)
        self.names, self.inputs = list(names), list(parts)
        self.out_shapes = [jax.ShapeDtypeStruct((NCHIP,) + a.shape[2:], a.dtype) for a in parts]
        self.sem_shapes = [pltpu.SemaphoreType.DMA((n,)), pltpu.SemaphoreType.DMA((n,))]

    def _copies(self, ins, outs, sems):
        x, y, c = _coords()
        return [pltpu.make_async_remote_copy(
            src_ref=ins[i].at[:, 1 - c], dst_ref=outs[i], send_sem=sems[0].at[i], recv_sem=sems[1].at[i],
            device_id=(x, y, 1 - c), device_id_type=MESH) for i in range(len(ins))]

    def start(self, ins, outs, sems):
        for cp in self._copies(ins, outs, sems):
            cp.start()

    def finish(self, ins, outs, sems):
        cps = self._copies(ins, outs, sems)
        for cp in cps:
            cp.wait_recv()
        for cp in cps:
            cp.wait_send()


HBM = pl.BlockSpec(memory_space=pltpu.HBM)
SEM = pl.BlockSpec(memory_space=pltpu.SEMAPHORE)


def _chip_peers():
    x, y, c = _coords()
    return [(1 - x, y, c), (x, 1 - y, c), (1 - x, 1 - y, c)]


def _chips_start(name, sums):
    n = len(sums)

    def body(*refs):
        ins, lands, send_sems, recv_sems = refs[:n], refs[n:2 * n], refs[2 * n], refs[2 * n + 1]
        for i in range(n):
            for k, (px, py, pc) in enumerate(_chip_peers()):
                pltpu.make_async_remote_copy(
                    src_ref=ins[i].at[2 * px + py], dst_ref=lands[i].at[k], send_sem=send_sems.at[3 * i + k],
                    recv_sem=recv_sems.at[3 * i + k], device_id=(px, py, pc), device_id_type=MESH).start()
        refs[-1][...] = jnp.zeros_like(refs[-1])

    lands = [lax.empty((3,) + a.shape[1:], a.dtype) for a in sums]
    bufs = list(sums) + lands
    res = pl.pallas_call(
        body, name=name, in_specs=[HBM] * (2 * n),
        out_specs=(SEM, SEM, *[HBM] * (2 * n), pl.BlockSpec(memory_space=pltpu.VMEM)),
        out_shape=(pltpu.SemaphoreType.DMA((3 * n,)), pltpu.SemaphoreType.DMA((3 * n,)),
                   *[pltpu.HBM(a.shape, a.dtype) for a in bufs], jax.ShapeDtypeStruct((8, 128), F32)),
        input_output_aliases={i: 2 + i for i in range(2 * n)},
        compiler_params=pltpu.CompilerParams(has_side_effects=pltpu.SideEffectType.DATAFLOW_SIDE_EFFECTING),
    )(*[pltpu.with_memory_space_constraint(a, pltpu.HBM) for a in bufs])
    return res[0], res[1], list(res[2:2 + n]), list(res[2 + n:2 + 2 * n]), res[-1]


def _chips_wait(name, send_sems, recv_sems, sums, lands, after):
    n = len(sums)

    def body(*refs):
        ins, zones, send, recv = refs[:n], refs[n:2 * n], refs[2 * n], refs[2 * n + 1]
        for i in range(n):
            for k, peer in enumerate(_chip_peers()):
                cp = pltpu.make_async_remote_copy(
                    src_ref=ins[i].at[0], dst_ref=zones[i].at[k], send_sem=send.at[3 * i + k],
                    recv_sem=recv.at[3 * i + k],
                    device_id=peer, device_id_type=MESH)
                cp.wait_send()
                cp.wait_recv()

    bufs = list(sums) + list(lands)
    res = pl.pallas_call(
        body, name=name, in_specs=[HBM] * (2 * n) + [SEM, SEM, ANY], out_specs=[HBM] * (2 * n),
        out_shape=[pltpu.HBM(a.shape, a.dtype) for a in bufs], input_output_aliases={i: i for i in range(2 * n)},
        compiler_params=pltpu.CompilerParams(has_side_effects=pltpu.SideEffectType.DATAFLOW_SIDE_EFFECTING),
    )(*bufs, send_sems, recv_sems, after)
    return list(res[:n]), list(res[n:])


def _call(name, body, grid, in_specs, out_specs, out_shape, args, comm=(), scratch=()):
    comm = [p for p in comm if p is not None]
    single = not isinstance(out_shape, (list, tuple))
    o_specs = [out_specs] if single else list(out_specs)
    o_shape = [out_shape] if single else list(out_shape)
    n_in, n_out = len(in_specs), len(o_specs)
    c_in = [a for p in comm for a in p.inputs]
    c_out = [s for p in comm for s in p.out_shapes]
    c_sem = [s for p in comm for s in p.sem_shapes]

    def wrapped(*refs):
        ins, outs = refs[:n_in], refs[n_in + len(c_in):n_in + len(c_in) + n_out]
        pos = [n_in, n_in + len(c_in) + n_out, n_in + len(c_in) + n_out + len(c_out)]
        own = refs[pos[2]:pos[2] + len(scratch)]
        pos[2] += len(scratch)
        split = []
        for p in comm:
            sizes = [len(p.inputs), len(p.out_shapes), len(p.sem_shapes)]
            split.append([refs[o:o + n] for o, n in zip(pos, sizes)])
            pos = [o + n for o, n in zip(pos, sizes)]
        step, steps = 0, 1
        for a, g in enumerate(grid):
            step, steps = step * g + pl.program_id(a), steps * g

        def run(which, at):
            def go():
                for p, cut in zip(comm, split):
                    getattr(p, which)(*cut)
            if not comm:
                return
            if grid:
                pl.when(step == at)(go)
            else:
                go()

        run("start", 0)
        body(*ins, *outs, *own)
        run("mid", steps // 2)
        run("late", max(steps // 2, steps - 1 - max(1, steps // 8)))
        run("finish", steps - 1)

    res = pl.pallas_call(
        wrapped, name=name, grid=grid, in_specs=list(in_specs) + [ANY] * len(c_in),
        out_specs=o_specs + [ANY] * len(c_out), out_shape=o_shape + c_out, scratch_shapes=list(scratch) + c_sem,
        compiler_params=_params(len(grid)),
    )(*args, *c_in)
    pos = n_out
    for p in comm:
        p.results = list(res[pos:pos + len(p.out_shapes)])
        pos += len(p.out_shapes)
    return res[0] if single else list(res[:n_out])


def _comm_only(name, comm):
    def body(o_ref):
        o_ref[...] = jnp.zeros_like(o_ref)

    _call(name, body, (), [], pl.BlockSpec(memory_space=pltpu.VMEM), jax.ShapeDtypeStruct((8, 128), F32), [], comm)


def _mm(name, grid, prods, extras, outs, epi, nacc=1, comm=()):
    n_p, n_e = len(prods), len(extras)

    def body(*refs):
        ab, ex, out = refs[:2 * n_p], refs[2 * n_p:2 * n_p + n_e], refs[2 * n_p + n_e:]
        accs = [None] * nacc
        for i, prod in enumerate(prods):
            dn, acc, loop = prod[6], prod[7], prod[8]
            a_ref, b_ref = ab[2 * i], ab[2 * i + 1]
            if loop:
                for g in range(loop):
                    t = lax.dot_general(a_ref[g], b_ref[g], _DN[dn], preferred_element_type=F32)
                    accs[acc] = t if accs[acc] is None else accs[acc] + t
            else:
                t = lax.dot_general(a_ref[...], b_ref[...], _DN[dn], preferred_element_type=F32)
                accs[acc] = t if accs[acc] is None else accs[acc] + t
        epi(accs, ex, out)

    in_specs, args = [], []
    for prod in prods:
        in_specs += [pl.BlockSpec(prod[1], prod[2]), pl.BlockSpec(prod[4], prod[5])]
        args += [prod[0], prod[3]]
    for e, e_blk, e_map in extras:
        in_specs.append(pl.BlockSpec(e_blk, e_map))
        args.append(e)
    return _call(name, body, grid, in_specs, [pl.BlockSpec(blk, mp) for _, _, blk, mp in outs],
                 [jax.ShapeDtypeStruct(s, d) for s, d, _, _ in outs], args, comm)


def _store(accs, ex, out):
    out[0][...] = accs[0].astype(out[0].dtype)


def _ew_tile(r, c, budget=3 << 19):
    for t in range(r - r % 16, 0, -16):
        if r % t == 0 and t * c * 4 <= budget:
            return t, c
    for t in range(c - c % 128, 0, -128):
        if c % t == 0 and r * t * 4 <= budget:
            return r, t
    return r, c


def _tile(n, want):
    t = min(n, want)
    assert n % t == 0, (n, want)
    return t


def _mm_nt(name, a, bt, out_dtype, tm=512, tn=512, comm=(), rows=None):
    m, k = a.shape
    n = rows or bt.shape[0]
    tm, tn = _tile(m, tm), (tn if n % tn == 0 else n)
    return _mm(name, (n // tn, m // tm),
               [(a, (tm, k), lambda j, i: (i, 0), bt, (tn, k), lambda j, i: (j, 0), "nt", 0, 0)], [],
               [((m, n), out_dtype, (tm, tn), lambda j, i: (i, j))], _store, comm=comm)[0]


def _mm_tn_into(name, a, b, buf, row0, ta=1024, tb=512):
    t, ka = a.shape
    nb = b.shape[1]
    ta, tb = (ta if ka % ta == 0 else ka), (tb if nb % tb == 0 else nb)

    def body(a_ref, b_ref, buf_in, buf_out, tile, sem):
        i, j = pl.program_id(0), pl.program_id(1)
        tile[...] = lax.dot_general(a_ref[...], b_ref[...], _DN["tn"], preferred_element_type=F32).astype(tile.dtype)
        rows = pl.ds(pl.multiple_of(row0 + i * ta, 16), ta)
        cp = pltpu.make_async_copy(tile, buf_out.at[rows, pl.ds(pl.multiple_of(j * tb, 128), tb)], sem)
        cp.start()
        cp.wait()

    return pl.pallas_call(
        body, name=name, grid=(ka // ta, nb // tb),
        in_specs=[pl.BlockSpec((t, ta), lambda i, j: (0, i)), pl.BlockSpec((t, tb), lambda i, j: (0, j)), ANY],
        out_specs=ANY, out_shape=jax.ShapeDtypeStruct(buf.shape, buf.dtype), input_output_aliases={2: 0},
        scratch_shapes=[pltpu.VMEM((ta, tb), buf.dtype), pltpu.SemaphoreType.DMA],
        compiler_params=_params(2))(a, b, buf)


def _rms_fwd(name, x, g, tm=256, comm=()):
    s, d = x.shape
    tm = _tile(s, tm)

    def body(x_ref, g_ref, o_ref):
        v = x_ref[...]
        o_ref[...] = (v * lax.rsqrt(jnp.mean(v * v, axis=-1, keepdims=True) + EPS) * g_ref[...]).astype(o_ref.dtype)

    return _call(name, body, (s // tm,),
                 [pl.BlockSpec((tm, d), lambda i: (i, 0)), pl.BlockSpec((1, d), lambda i: (0, 0))],
                 pl.BlockSpec((tm, d), lambda i: (i, 0)), jax.ShapeDtypeStruct((s, d), BF), [x, g], comm)


def _acc_rows(ref, part, i):
    @pl.when(i == 0)
    def _():
        ref[...] = part

    @pl.when(i > 0)
    def _():
        ref[...] += part


def _rms_bwd_math(dn, v, g):
    rstd = lax.rsqrt(jnp.mean(v * v, axis=-1, keepdims=True) + EPS)
    xh = v * rstd
    dxh = dn * g
    dx = rstd * (dxh - xh * jnp.mean(dxh * xh, axis=-1, keepdims=True))
    return dx, jnp.sum(dn * xh, axis=0, keepdims=True)


def _rms_bwd(name, dn, x, g, resid, tm=256, comm=()):
    s, d = x.shape
    tm = _tile(s, tm)

    def body(dn_ref, x_ref, g_ref, r_ref, dx_ref, dxb_ref, dg_ref):
        dx, part = _rms_bwd_math(dn_ref[...].astype(F32), x_ref[...], g_ref[...])
        tot = r_ref[...] + dx
        dx_ref[...] = tot
        dxb_ref[...] = tot.astype(BF)
        _acc_rows(dg_ref, part, pl.program_id(0))

    row = pl.BlockSpec((tm, d), lambda i: (i, 0))
    one = pl.BlockSpec((1, d), lambda i: (0, 0))
    return _call(name, body, (s // tm,), [row, row, one, row], [row, row, one],
                 [jax.ShapeDtypeStruct((s, d), F32), jax.ShapeDtypeStruct((s, d), BF),
                  jax.ShapeDtypeStruct((1, d), F32)], [dn, x, g, resid], comm)


def _loss_head(h, target, g, tm=256):
    s, d = h.shape
    tm = _tile(s, tm)

    def body(h_ref, t_ref, g_ref, dh_ref, dg_ref, loss_ref):
        v, gv = h_ref[...], g_ref[...]
        rstd = lax.rsqrt(jnp.mean(v * v, axis=-1, keepdims=True) + EPS)
        xh = v * rstd
        err = xh * gv - t_ref[...]
        part_loss = 0.5 * jnp.sum(jnp.mean(err * err, axis=-1, keepdims=True), axis=0, keepdims=True)
        dy = err * (1.0 / d)
        dxh = dy * gv
        dh_ref[...] = rstd * (dxh - xh * jnp.mean(dxh * xh, axis=-1, keepdims=True))
        i = pl.program_id(0)
        _acc_rows(dg_ref, jnp.sum(dy * xh, axis=0, keepdims=True), i)
        _acc_rows(loss_ref, jnp.broadcast_to(part_loss, loss_ref.shape), i)

    row = pl.BlockSpec((tm, d), lambda i: (i, 0))
    one = pl.BlockSpec((1, d), lambda i: (0, 0))
    return pl.pallas_call(
        body, name="loss_head", grid=(s // tm,), in_specs=[row, row, one],
        out_specs=[row, one, pl.BlockSpec((1, 128), lambda i: (0, 0))],
        out_shape=[jax.ShapeDtypeStruct((s, d), F32), jax.ShapeDtypeStruct((1, d), F32),
                   jax.ShapeDtypeStruct((1, 128), F32)],
        compiler_params=_params(1))(h, target, g)


def _pl_bwd_elem(dh, pe, t, tm=256):
    s, d = dh.shape
    tm = _tile(s, tm)

    def body(dh_ref, pe_ref, t_ref, dt_ref, dpe_ref):
        dh_v, sg = dh_ref[...], _sig(t_ref[...])
        dt_ref[...] = (dh_v * pe_ref[...].astype(F32) * sg * (1.0 - sg)).astype(BF)
        dpe_ref[...] = (dh_v * sg).astype(BF)

    row = pl.BlockSpec((tm, d), lambda i: (i, 0))
    return pl.pallas_call(
        body, name="pl_bwd_elem", grid=(s // tm,), in_specs=[row, row, row], out_specs=[row, row],
        out_shape=[jax.ShapeDtypeStruct((s, d), BF)] * 2, compiler_params=_params(1))(dh, pe, t)


def _ffn_up(name, xn, wg, wu, tm=1024, comm=()):
    s, d = xn.shape
    g, fb, _ = wg.shape
    tm = _tile(s, tm)

    def epi(accs, ex, out):
        hg, hu = accs
        out[0][...] = hg.astype(BF)
        out[1][...] = hu.astype(BF)
        out[2][...] = (hg * _sig(hg) * hu).astype(BF)

    a_map = lambda j, i: (i, 0)
    w_map = lambda j, i: (j, 0, 0)
    o = ((g, s, fb), BF, (None, tm, fb), lambda j, i: (j, i, 0))
    return _mm(name, (g, s // tm),
               [(xn, (tm, d), a_map, wg, (None, fb, d), w_map, "nt", 0, 0),
                (xn, (tm, d), a_map, wu, (None, fb, d), w_map, "nt", 1, 0)], [], [o, o, o], epi, nacc=2, comm=comm)


def _ffn_down(name, a, wd, resid, tm=1024, tn=512, comm=()):
    g, s, fb = a.shape
    d = wd.shape[2]
    tm, tn = _tile(s, tm), _tile(d, tn)

    def epi(accs, ex, out):
        out[0][...] = ex[0][...] + 0.5 * accs[0]

    return _mm(name, (d // tn, s // tm),
               [(a, (g, tm, fb), lambda j, i: (0, i, 0), wd, (g, fb, tn), lambda j, i: (0, 0, j), "nn", 0, g)],
               [(resid, (tm, tn), lambda j, i: (i, j))],
               [((s, d), F32, (tm, tn), lambda j, i: (i, j))], epi, comm=comm)[0]


def _ffn_bwd_act(name, dh, wd, hg, hu, tm=1024, comm=()):
    s, d = dh.shape
    g, fb, _ = wd.shape
    tm = _tile(s, tm)

    def epi(accs, ex, out):
        da = 0.5 * accs[0]
        hg_v, hu_v = ex[0][...].astype(F32), ex[1][...].astype(F32)
        sg = _sig(hg_v)
        out[0][...] = (da * hu_v * (sg * (1.0 + hg_v * (1.0 - sg)))).astype(BF)
        out[1][...] = (da * (hg_v * sg)).astype(BF)

    blk = (None, tm, fb)
    gmap = lambda j, i: (j, i, 0)
    return _mm(name, (g, s // tm),
               [(dh, (tm, d), lambda j, i: (i, 0), wd, (None, fb, d), lambda j, i: (j, 0, 0), "nt", 0, 0)],
               [(hg, blk, gmap), (hu, blk, gmap)],
               [((g, s, fb), BF, blk, gmap), ((g, s, fb), BF, blk, gmap)], epi, comm=comm)


def _ffn_bwd_wd(name, a, dh, tn=1024, comm=()):
    g, s, fb = a.shape
    d = dh.shape[1]
    tn = _tile(d, tn)

    def epi(accs, ex, out):
        out[0][...] = (0.5 * accs[0]).astype(BF)

    return _mm(name, (g, d // tn),
               [(a, (None, s, fb), lambda j, i: (j, 0, 0), dh, (s, tn), lambda j, i: (0, i), "tn", 0, 0)], [],
               [((g, fb, d), BF, (None, fb, tn), lambda j, i: (j, 0, i))], epi, comm=comm)[0]


def _ffn_bwd_wup(name, xn, dhg, dhu, tk=1024, comm=()):
    s, d = xn.shape
    g, _, fb = dhg.shape
    tk = _tile(d, tk)

    def epi(accs, ex, out):
        out[0][...] = accs[0].astype(BF)
        out[1][...] = accs[1].astype(BF)

    a_map = lambda j, i: (j, 0, 0)
    b_map = lambda j, i: (0, i)
    o = ((g, fb, d), BF, (None, fb, tk), lambda j, i: (j, 0, i))
    return _mm(name, (g, d // tk),
               [(dhg, (None, s, fb), a_map, xn, (s, tk), b_map, "tn", 0, 0),
                (dhu, (None, s, fb), a_map, xn, (s, tk), b_map, "tn", 1, 0)], [], [o, o], epi, nacc=2, comm=comm)


def _ffn_bwd_x(name, dhg, dhu, wg, wu, tm=512, tn=512, comm=()):
    g, s, fb = dhg.shape
    d = wg.shape[2]
    tm, tn = _tile(s, tm), _tile(d, tn)
    a_blk, a_map = (g, tm, fb), lambda j, i: (0, i, 0)
    b_blk, b_map = (g, fb, tn), lambda j, i: (0, 0, j)
    return _mm(name, (d // tn, s // tm),
               [(dhg, a_blk, a_map, wg, b_blk, b_map, "nn", 0, g), (dhu, a_blk, a_map, wu, b_blk, b_map, "nn", 0, g)],
               [], [((s, d), F32, (tm, tn), lambda j, i: (i, j))], _store, comm=comm)[0]


def _ffn_forward(tag, h, gain, get_wgu, get_wd, norm_comm=(), up_comm=(), down_comm=()):
    xn = _rms_fwd(tag + "_norm", h, gain, comm=norm_comm)
    wg, wu = get_wgu()
    hg, hu, a = _ffn_up(tag + "_up", xn, wg, wu, comm=up_comm)
    return _ffn_down(tag + "_down", a, get_wd(), h, comm=down_comm), (xn, hg, hu, a)


def _na_geometry(rows):
    kh = min(NA_ROWS_WIN, rows)
    cols = np.arange(GRID_W)
    col_start = np.clip(cols - NA_COLS_WIN // 2, 0, GRID_W - NA_COLS_WIN)
    mask = (cols[None, :] >= col_start[:, None]) & (cols[None, :] < col_start[:, None] + NA_COLS_WIN)
    dc = np.clip(cols[None, :] - cols[:, None], -(NA_COLS_WIN - 1), NA_COLS_WIN - 1) + (NA_COLS_WIN - 1)
    return kh, mask, dc


def _na_table(rpb, rows):
    _, mask, dc = _na_geometry(rows)
    nd, nc, cells = 2 * NA_ROWS_WIN - 1, 2 * NA_COLS_WIN - 1, GRID_W * GRID_W
    onehot = np.zeros((128, cells), np.float32)
    onehot[dc.reshape(-1), np.arange(cells)] = mask.reshape(-1).astype(np.float32)
    off = np.where(mask.reshape(1, -1), 0.0, NEG).astype(np.float32)

    def body(r_ref, e_ref, off_ref, o_ref):
        o_ref[...] = jnp.dot(r_ref[...], e_ref[...], precision=HI, preferred_element_type=F32) + off_ref[...]

    flat = pl.pallas_call(body, name="na_table", out_shape=jax.ShapeDtypeStruct((NA_HEADS * nd, cells), F32),
                          compiler_params=_params(0))(
        jnp.pad(rpb.reshape(NA_HEADS * nd, nc), ((0, 0), (0, 128 - nc))), jnp.asarray(onehot), jnp.asarray(off))
    return flat.reshape(NA_HEADS, nd, GRID_W, GRID_W)


class _NaPlan:
    def __init__(self, s):
        self.s, self.rows = s, s // GRID_W
        self.kh = min(NA_ROWS_WIN, self.rows)
        self.qr = min(NA_QROWS, self.rows)
        self.kr = min(self.rows, self.kh + self.qr - 1)
        self.groups = self.rows // self.qr
        self.nd = 2 * NA_ROWS_WIN - 1
        self.hw, self.nq = NA_HG * NA_DIM, NA_HEADS // NA_HG
        clip = lambda v, hi: min(max(v, 0), hi)
        pats = [(clip(g * self.qr - self.kh // 2, self.rows - self.kr) - g * self.qr,)
                + tuple(clip(g * self.qr + a - self.kh // 2, self.rows - self.kh) - g * self.qr for a in range(self.qr))
                for g in range(self.groups)]
        self.rebuild = [g for g in range(self.groups) if g == 0 or pats[g] != pats[g - 1]]

    def first_key_row(self, g):
        return jnp.clip(g * self.qr - self.kh // 2, 0, self.rows - self.kr)

    def specs(self):
        blk = pl.BlockSpec((self.qr * GRID_W, self.hw), lambda j, g: (g, j))
        k_spec = pl.BlockSpec((self.s, self.hw), lambda j, g: (0, self.nq + j))
        v_spec = pl.BlockSpec((self.s, self.hw), lambda j, g: (0, 2 * self.nq + j))
        t_spec = pl.BlockSpec((NA_HG, self.nd, GRID_W, GRID_W), lambda j, g: (j, 0, 0, 0))
        return blk, k_spec, v_spec, t_spec

    def bias_scratch(self):
        return pltpu.VMEM((NA_HG, self.qr * GRID_W, self.kr * GRID_W), F32)

    def fill_bias(self, t_ref, bias_ref, g):
        def build():
            r0, ks = g * self.qr, self.first_key_row(g)
            for a in range(self.qr):
                rs = jnp.clip(r0 + a - self.kh // 2, 0, self.rows - self.kh)
                for i in range(self.kr):
                    valid = jnp.logical_and(ks + i >= rs, ks + i < rs + self.kh)
                    idx = jnp.clip(ks + i - r0 - a + NA_ROWS_WIN - 1, 0, self.nd - 1)
                    for h in range(NA_HG):
                        bias_ref[h, a * GRID_W:(a + 1) * GRID_W, i * GRID_W:(i + 1) * GRID_W] = jnp.where(
                            valid, t_ref[h, idx], NEG)

        pl.when(functools.reduce(jnp.logical_or, [g == r for r in self.rebuild]))(build)

    def window(self, g):
        return pl.ds(pl.multiple_of(self.first_key_row(g) * GRID_W, GRID_W), self.kr * GRID_W)


def _na_probs(q, k, bias):
    sc = lax.dot_general(q, k, _DN["nt"], preferred_element_type=F32) * (NA_DIM ** -0.5) + bias
    e = jnp.exp(sc - jnp.max(sc, axis=-1, keepdims=True))
    return e / jnp.sum(e, axis=-1, keepdims=True)


def _na_fwd(qkv, table, comm=()):
    plan = _NaPlan(qkv.shape[0])
    blk, k_spec, v_spec, t_spec = plan.specs()

    def body(q_ref, k_ref, v_ref, t_ref, o_ref, bias_ref):
        g = pl.program_id(1)
        plan.fill_bias(t_ref, bias_ref, g)
        win = plan.window(g)
        for h in range(NA_HG):
            cs = slice(h * NA_DIM, (h + 1) * NA_DIM)
            p = _na_probs(q_ref[:, cs], k_ref[win, cs], bias_ref[h])
            o_ref[:, cs] = jnp.dot(p.astype(BF), v_ref[win, cs], preferred_element_type=F32).astype(BF)

    return _call("na_fwd", body, (plan.nq, plan.groups), [blk, k_spec, v_spec, t_spec], blk,
                 jax.ShapeDtypeStruct((plan.s, NA_HEADS * NA_DIM), BF), [qkv, qkv, qkv, table], comm,
                 scratch=[plan.bias_scratch()])


def _na_bwd(qkv, table, do, comm=()):
    plan = _NaPlan(qkv.shape[0])
    blk, k_spec, v_spec, t_spec = plan.specs()
    qr, kr = plan.qr, plan.kr

    def body(q_ref, k_ref, v_ref, t_ref, do_ref, dq_ref, dk_ref, dv_ref, dt_ref, bias_ref):
        g = pl.program_id(1)

        @pl.when(g == 0)
        def _():
            dk_ref[...] = jnp.zeros_like(dk_ref)
            dv_ref[...] = jnp.zeros_like(dv_ref)
            dt_ref[...] = jnp.zeros_like(dt_ref)

        plan.fill_bias(t_ref, bias_ref, g)
        win = plan.window(g)
        base = plan.first_key_row(g) - g * qr + NA_ROWS_WIN - 1
        for h in range(NA_HG):
            cs = slice(h * NA_DIM, (h + 1) * NA_DIM)
            q, k, v, do_h = q_ref[:, cs], k_ref[win, cs], v_ref[win, cs], do_ref[:, cs]
            p = _na_probs(q, k, bias_ref[h])
            dp = lax.dot_general(do_h, v, _DN["nt"], preferred_element_type=F32)
            ds = p * (dp - jnp.sum(p * dp, axis=-1, keepdims=True))
            for dlt in range(1 - qr, kr):
                tiles = [ds[a * GRID_W:(a + 1) * GRID_W, (a + dlt) * GRID_W:(a + dlt + 1) * GRID_W]
                         for a in range(qr) if 0 <= a + dlt < kr]
                dt_ref[h, jnp.clip(base + dlt, 0, plan.nd - 1)] += functools.reduce(jnp.add, tiles)
            dsb = (ds * (NA_DIM ** -0.5)).astype(BF)
            dq_ref[:, cs] = jnp.dot(dsb, k, preferred_element_type=F32).astype(BF)
            dk_ref[win, cs] += lax.dot_general(dsb, q, _DN["tn"], preferred_element_type=F32)
            dv_ref[win, cs] += lax.dot_general(p.astype(BF), do_h, _DN["tn"], preferred_element_type=F32)

    width = NA_HEADS * NA_DIM
    whole = pl.BlockSpec((plan.s, plan.hw), lambda j, g: (0, j))
    return _call(
        "na_bwd", body, (plan.nq, plan.groups), [blk, k_spec, v_spec, t_spec, blk], [blk, whole, whole, t_spec],
        [jax.ShapeDtypeStruct((plan.s, width), BF), jax.ShapeDtypeStruct((plan.s, width), F32),
         jax.ShapeDtypeStruct((plan.s, width), F32),
         jax.ShapeDtypeStruct((NA_HEADS, plan.nd, GRID_W, GRID_W), F32)],
        [qkv, qkv, qkv, table, do], comm, scratch=[plan.bias_scratch()])


def _na_rpb_grad(dt, rows):
    _, mask, dc = _na_geometry(rows)
    nd, nc = 2 * NA_ROWS_WIN - 1, 2 * NA_COLS_WIN - 1
    onehot = np.zeros((GRID_W * GRID_W, 128), np.float32)
    onehot[np.arange(GRID_W * GRID_W), dc.reshape(-1)] = mask.reshape(-1).astype(np.float32)
    flat = dt.reshape(NA_HEADS * nd, GRID_W * GRID_W)

    def body(a_ref, e_ref, o_ref):
        o_ref[...] = jnp.dot(a_ref[...], e_ref[...], precision=HI, preferred_element_type=F32)

    out = pl.pallas_call(body, name="na_rpb_grad", out_shape=jax.ShapeDtypeStruct((NA_HEADS * nd, 128), F32),
                         compiler_params=_params(0))(flat, jnp.asarray(onehot))
    return out[:, :nc].reshape(NA_HEADS, nd, nc)


def _rope_consts(s):
    pos = np.arange(s, dtype=np.float32)
    inv = (1.0 / (ROPE_THETA ** (np.arange(0, ML_ROPE, 2, dtype=np.float32) / ML_ROPE))).astype(np.float32)
    ang = pos[:, None] * inv[None, :]
    cos, sin = np.cos(ang).astype(np.float32), np.sin(ang).astype(np.float32)
    half = ML_ROPE // 2
    rot = np.zeros((ML_ROPE, ML_ROPE), np.float32)
    rot[np.arange(half) + half, np.arange(half)] = -1.0
    rot[np.arange(half), np.arange(half) + half] = 1.0
    return (jnp.asarray(np.concatenate([cos, cos], 1)), jnp.asarray(np.concatenate([sin, sin], 1)),
            jnp.asarray(rot), jnp.asarray(rot.T.copy()))


def _rope(v, cos, sin, rot):
    return v * cos + jnp.dot(v, rot, precision=HI, preferred_element_type=F32) * sin


def _unrope(dv, cos, sin, rot_t):
    return dv * cos + jnp.dot(dv * sin, rot_t, precision=HI, preferred_element_type=F32)


def _rms(v, g):
    return v * lax.rsqrt(jnp.mean(v * v, axis=-1, keepdims=True) + EPS) * g


def _mla_prep(lat, gq, gkv, cos, sin, rot, tm=256):
    s, w = lat.shape
    tm = _tile(s, tm)

    def body(l_ref, gq_ref, gkv_ref, c_ref, s_ref, r_ref, cq_ref, ckv_ref, kr_ref):
        cq_ref[...] = _rms(l_ref[:, :ML_RANK], gq_ref[...]).astype(BF)
        ckv_ref[...] = _rms(l_ref[:, ML_RANK:2 * ML_RANK], gkv_ref[...]).astype(BF)
        kr_ref[...] = _rope(l_ref[:, 2 * ML_RANK:], c_ref[...], s_ref[...], r_ref[...]).astype(BF)

    row = lambda c: pl.BlockSpec((tm, c), lambda i: (i, 0))
    full = lambda a: pl.BlockSpec(a.shape, lambda i: (0, 0))
    return pl.pallas_call(
        body, name="mla_prep", grid=(s // tm,),
        in_specs=[row(w), full(gq), full(gkv), row(ML_ROPE), row(ML_ROPE), full(rot)],
        out_specs=[row(ML_RANK), row(ML_RANK), row(ML_ROPE)],
        out_shape=[jax.ShapeDtypeStruct((s, ML_RANK), BF), jax.ShapeDtypeStruct((s, ML_RANK), BF),
                   jax.ShapeDtypeStruct((s, ML_ROPE), BF)],
        compiler_params=_params(1))(lat, gq, gkv, cos, sin, rot)


def _mla_q_proj(cq, wuq, cos, sin, rot, tm=512, comm=()):
    s, k = cq.shape
    tm = _tile(s, tm)

    def epi(accs, ex, out):
        acc = accs[0]
        out[0][:, :ML_NOPE] = acc[:, :ML_NOPE].astype(BF)
        out[0][:, ML_NOPE:] = _rope(acc[:, ML_NOPE:], ex[0][...], ex[1][...], ex[2][...]).astype(BF)

    rmap = lambda j, i: (i, 0)
    return _mm("mla_q_proj", (ML_HEADS, s // tm),
               [(cq, (tm, k), rmap, wuq, (None, ML_QK, k), lambda j, i: (j, 0, 0), "nt", 0, 0)],
               [(cos, (tm, ML_ROPE), rmap), (sin, (tm, ML_ROPE), rmap), (rot, rot.shape, lambda j, i: (0, 0))],
               [((ML_HEADS, s, ML_QK), BF, (None, tm, ML_QK), lambda j, i: (j, i, 0))], epi, comm=comm)[0]


def _mla_kv_proj(ckv, wukv, kr, tm=512, comm=()):
    s, k = ckv.shape
    tm = _tile(s, tm)

    def epi(accs, ex, out):
        acc = accs[0]
        out[0][:, :ML_NOPE] = acc[:, :ML_NOPE].astype(BF)
        out[0][:, ML_NOPE:] = ex[0][...]
        out[1][...] = acc[:, ML_NOPE:].astype(BF)

    rmap = lambda j, i: (i, 0)
    gmap = lambda j, i: (j, i, 0)
    return _mm("mla_kv_proj", (ML_HEADS, s // tm),
               [(ckv, (tm, k), rmap, wukv, (None, k, ML_NOPE + ML_V), lambda j, i: (j, 0, 0), "nn", 0, 0)],
               [(kr, (tm, ML_ROPE), rmap)],
               [((ML_HEADS, s, ML_QK), BF, (None, tm, ML_QK), gmap), ((ML_HEADS, s, ML_V), BF, (None, tm, ML_V), gmap)],
               epi, comm=comm)


def _mla_probs(q, k):
    sc = lax.dot_general(q, k, _DN["nt"], preferred_element_type=F32) * (ML_QK ** -0.5)
    e = jnp.exp(sc - jnp.max(sc, axis=-1, keepdims=True))
    return e / jnp.sum(e, axis=-1, keepdims=True)


def _mla_fwd(q, k, v, tq=1024, comm=()):
    _, s, _ = q.shape
    tq = _tile(s, tq)

    def body(q_ref, k_ref, v_ref, o_ref):
        p = _mla_probs(q_ref[...], k_ref[...])
        o_ref[...] = jnp.dot(p.astype(BF), v_ref[...], preferred_element_type=F32).astype(BF)

    return _call("mla_fwd", body, (ML_HEADS, s // tq),
                 [pl.BlockSpec((None, tq, ML_QK), lambda h, i: (h, i, 0)),
                  pl.BlockSpec((None, s, ML_QK), lambda h, i: (h, 0, 0)),
                  pl.BlockSpec((None, s, ML_V), lambda h, i: (h, 0, 0))],
                 pl.BlockSpec((tq, ML_V), lambda h, i: (i, h)),
                 jax.ShapeDtypeStruct((s, ML_HEADS * ML_V), BF), [q, k, v], comm)


def _mla_bwd(q, k, v, do, tq=1024, comm=()):
    _, s, _ = q.shape
    tq = _tile(s, tq)

    def body(q_ref, k_ref, v_ref, do_ref, dq_ref, dk_ref, dv_ref):
        i = pl.program_id(1)
        qv, kv, vv, dov = q_ref[...], k_ref[...], v_ref[...], do_ref[...]
        p = _mla_probs(qv, kv)
        dp = lax.dot_general(dov, vv, _DN["nt"], preferred_element_type=F32)
        ds = (p * (dp - jnp.sum(p * dp, axis=-1, keepdims=True)) * (ML_QK ** -0.5)).astype(BF)
        dq_ref[...] = jnp.dot(ds, kv, preferred_element_type=F32)
        _acc_rows(dk_ref, lax.dot_general(ds, qv, _DN["tn"], preferred_element_type=F32), i)
        _acc_rows(dv_ref, lax.dot_general(p.astype(BF), dov, _DN["tn"], preferred_element_type=F32), i)

    return _call(
        "mla_bwd", body, (ML_HEADS, s // tq),
        [pl.BlockSpec((None, tq, ML_QK), lambda h, i: (h, i, 0)),
         pl.BlockSpec((None, s, ML_QK), lambda h, i: (h, 0, 0)),
         pl.BlockSpec((None, s, ML_V), lambda h, i: (h, 0, 0)),
         pl.BlockSpec((tq, ML_V), lambda h, i: (i, h))],
        [pl.BlockSpec((None, tq, ML_QK), lambda h, i: (h, i, 0)),
         pl.BlockSpec((None, s, ML_QK), lambda h, i: (h, 0, 0)),
         pl.BlockSpec((None, s, ML_V), lambda h, i: (h, 0, 0))],
        [jax.ShapeDtypeStruct((ML_HEADS, s, ML_QK), F32), jax.ShapeDtypeStruct((ML_HEADS, s, ML_QK), F32),
         jax.ShapeDtypeStruct((ML_HEADS, s, ML_V), F32)],
        [q, k, v, do], comm)


def _mla_post(dq, dk, dv, cos, sin, rot_t, tm=1024):
    _, s, _ = dq.shape
    tm = _tile(s, tm)

    def body(dq_ref, dk_ref, dv_ref, c_ref, s_ref, r_ref, dqp_ref, dkv_ref, dkr_ref):
        h = pl.program_id(1)
        dqv, dkk = dq_ref[...], dk_ref[...]
        dqp_ref[:, :ML_NOPE] = dqv[:, :ML_NOPE].astype(BF)
        dqp_ref[:, ML_NOPE:] = _unrope(dqv[:, ML_NOPE:], c_ref[...], s_ref[...], r_ref[...]).astype(BF)
        dkv_ref[:, :ML_NOPE] = dkk[:, :ML_NOPE].astype(BF)
        dkv_ref[:, ML_NOPE:] = dv_ref[...].astype(BF)
        _acc_rows(dkr_ref, dkk[:, ML_NOPE:], h)

    gspec = lambda c: pl.BlockSpec((None, tm, c), lambda i, h: (h, i, 0))
    rspec = pl.BlockSpec((tm, ML_ROPE), lambda i, h: (i, 0))
    return pl.pallas_call(
        body, name="mla_post", grid=(s // tm, ML_HEADS),
        in_specs=[gspec(ML_QK), gspec(ML_QK), gspec(ML_V), rspec, rspec,
                  pl.BlockSpec(rot_t.shape, lambda i, h: (0, 0))],
        out_specs=[gspec(ML_QK), gspec(ML_NOPE + ML_V), rspec],
        out_shape=[jax.ShapeDtypeStruct((ML_HEADS, s, ML_QK), BF),
                   jax.ShapeDtypeStruct((ML_HEADS, s, ML_NOPE + ML_V), BF),
                   jax.ShapeDtypeStruct((s, ML_ROPE), F32)],
        compiler_params=_params(2))(dq, dk, dv, cos, sin, rot_t)


def _mla_lat_bwd(dcq, dckv, dkr, lat, gq, gkv, cos, sin, rot_t, tm=256):
    s, w = lat.shape
    tm = _tile(s, tm)

    def body(dcq_ref, dckv_ref, dkr_ref, l_ref, gq_ref, gkv_ref, c_ref, s_ref, r_ref, dl_ref, dgq_ref, dgkv_ref):
        i = pl.program_id(0)
        dql, pq = _rms_bwd_math(dcq_ref[...], l_ref[:, :ML_RANK], gq_ref[...])
        dkl, pkv = _rms_bwd_math(dckv_ref[...], l_ref[:, ML_RANK:2 * ML_RANK], gkv_ref[...])
        dl_ref[:, :ML_RANK] = dql.astype(BF)
        dl_ref[:, ML_RANK:2 * ML_RANK] = dkl.astype(BF)
        dl_ref[:, 2 * ML_RANK:] = _unrope(dkr_ref[...], c_ref[...], s_ref[...], r_ref[...]).astype(BF)
        _acc_rows(dgq_ref, pq, i)
        _acc_rows(dgkv_ref, pkv, i)

    row = lambda c: pl.BlockSpec((tm, c), lambda i: (i, 0))
    full = lambda a: pl.BlockSpec(a.shape, lambda i: (0, 0))
    return pl.pallas_call(
        body, name="mla_lat_bwd", grid=(s // tm,),
        in_specs=[row(ML_RANK), row(ML_RANK), row(ML_ROPE), row(w), full(gq), full(gkv), row(ML_ROPE), row(ML_ROPE),
                  full(rot_t)],
        out_specs=[row(w), full(gq), full(gkv)],
        out_shape=[jax.ShapeDtypeStruct((s, w), BF), jax.ShapeDtypeStruct(gq.shape, F32),
                   jax.ShapeDtypeStruct(gkv.shape, F32)],
        compiler_params=_params(1))(dcq, dckv, dkr, lat, gq, gkv, cos, sin, rot_t)


def _grp_dw(name, a, dout, ta=1024):
    s, k = a.shape
    ta = _tile(k, ta)
    if dout.ndim == 3:
        g, _, nb = dout.shape
        b_blk, b_map = (None, s, nb), lambda j, i: (j, 0, 0)
    else:
        g, nb = NDEV, dout.shape[1] // NDEV
        b_blk, b_map = (s, nb), lambda j, i: (0, j)
    return _mm(name, (g, k // ta),
               [(a, (s, ta), lambda j, i: (0, i), dout, b_blk, b_map, "tn", 0, 0)], [],
               [((g, k, nb), BF, (None, ta, nb), lambda j, i: (j, i, 0))], _store)[0]


def _grp_dw_t(name, dout, a, ta=512):
    g, s, nb = dout.shape
    k = a.shape[1]
    ta = _tile(k, ta)
    return _mm(name, (g, k // ta),
               [(dout, (None, s, nb), lambda j, i: (j, 0, 0), a, (s, ta), lambda j, i: (0, i), "tn", 0, 0)], [],
               [((g, nb, k), BF, (None, nb, ta), lambda j, i: (j, 0, i))], _store)[0]


def _grp_dx_t(name, dout, wt, tm=512, tn=512, comm=()):
    g, s, nb = dout.shape
    k = wt.shape[2]
    tm, tn = _tile(s, tm), _tile(k, tn)
    return _mm(name, (k // tn, s // tm),
               [(dout, (g, tm, nb), lambda j, i: (0, i, 0), wt, (g, nb, tn), lambda j, i: (0, 0, j), "nn", 0, g)], [],
               [((s, k), F32, (tm, tn), lambda j, i: (i, j))], _store, comm=comm)[0]


def _grp_dx(name, dout, w, tm=512, tn=512, out_dtype=F32, comm=()):
    g, s, nb = dout.shape
    k = w.shape[1]
    tm, tn = _tile(s, tm), _tile(k, tn)
    return _mm(name, (k // tn, s // tm),
               [(dout, (g, tm, nb), lambda j, i: (0, i, 0), w, (g, tn, nb), lambda j, i: (0, j, 0), "nt", 0, g)], [],
               [((s, k), out_dtype, (tm, tn), lambda j, i: (i, j))], _store, comm=comm)[0]


def _row_dw(name, a, dout, tn=2048):
    s, n = dout.shape
    tn = _tile(n, tn)
    if a.ndim == 3:
        kb = a.shape[2]
        a_blk, a_map = (None, s, kb), lambda j, i: (j, 0, 0)
    else:
        kb = a.shape[1] // NDEV
        a_blk, a_map = (s, kb), lambda j, i: (0, j)
    return _mm(name, (NDEV, n // tn),
               [(a, a_blk, a_map, dout, (s, tn), lambda j, i: (0, i), "tn", 0, 0)], [],
               [((NDEV, kb, n), BF, (None, kb, tn), lambda j, i: (j, 0, i))], _store)[0]


def _mix_merge(oa, ob, wa, wb, ga, gb, tm=1024, comm=()):
    s, k = oa.shape
    g, _, nb = wa.shape
    tm = _tile(s, tm)

    def epi(accs, ex, out):
        ya, yb = accs
        out[0][...] = ya.astype(BF)
        out[1][...] = yb.astype(BF)
        out[2][...] = (_sig(ex[0][...]) * ya + _sig(ex[1][...]) * yb).astype(BF)

    rmap = lambda j, i: (i, 0)
    wmap = lambda j, i: (j, 0, 0)
    o = ((g, s, nb), BF, (None, tm, nb), lambda j, i: (j, i, 0))
    cmap = lambda j, i: (i, j)
    return _mm("mix_merge", (g, s // tm),
               [(oa, (tm, k), rmap, wa, (None, k, nb), wmap, "nn", 0, 0),
                (ob, (tm, k), rmap, wb, (None, k, nb), wmap, "nn", 1, 0)],
               [(ga, (tm, nb), cmap), (gb, (tm, nb), cmap)], [o, o, o], epi, nacc=2, comm=comm)


def _mix_out(merged, wout, resid, tm=1024, tn=512):
    g, s, kb = merged.shape
    d = wout.shape[2]
    tm, tn = _tile(s, tm), _tile(d, tn)

    def epi(accs, ex, out):
        out[0][...] = ex[0][...] + accs[0]

    return _mm("mix_out", (d // tn, s // tm),
               [(merged, (g, tm, kb), lambda j, i: (0, i, 0), wout, (g, kb, tn), lambda j, i: (0, 0, j), "nn", 0, g)],
               [(resid, (tm, tn), lambda j, i: (i, j))],
               [((s, d), F32, (tm, tn), lambda j, i: (i, j))], epi)[0]


def _mix_out_bwd(dh, wout, ga, gb, ya, yb, tm=1024, comm=()):
    s, d = dh.shape
    g, kb, _ = wout.shape
    tm = _tile(s, tm)

    def epi(accs, ex, out):
        dm = accs[0]
        sa, sb = _sig(ex[0][...]), _sig(ex[1][...])
        out[0][...] = (dm * sa).astype(BF)
        out[1][...] = (dm * sb).astype(BF)
        out[2][...] = (dm * ex[2][...].astype(F32) * sa * (1.0 - sa)).astype(BF)
        out[3][...] = (dm * ex[3][...].astype(F32) * sb * (1.0 - sb)).astype(BF)

    cmap = lambda j, i: (i, j)
    gmap = lambda j, i: (j, i, 0)
    og = ((g, s, kb), BF, (None, tm, kb), gmap)
    oc = ((s, g * kb), BF, (tm, kb), cmap)
    return _mm("mix_out_bwd", (g, s // tm),
               [(dh, (tm, d), lambda j, i: (i, 0), wout, (None, kb, d), lambda j, i: (j, 0, 0), "nt", 0, 0)],
               [(ga, (tm, kb), cmap), (gb, (tm, kb), cmap), (ya, (None, tm, kb), gmap), (yb, (None, tm, kb), gmap)],
               [og, og, oc, oc], epi, comm=comm)


def _pl_forward(n4, wplg, p, wpl, h3, tm=1024):
    s, d = n4.shape
    g, kb, _ = wplg.shape
    kp, nb = wpl.shape[1], wpl.shape[2]
    tm = _tile(s, tm)
    wplg_nat = wplg.reshape(g * kb, d)

    def epi(accs, ex, out):
        t, pe = accs
        out[0][...] = ex[0][...] + _sig(t) * pe
        out[1][...] = t
        out[2][...] = pe.astype(BF)

    rmap = lambda j, i: (i, 0)
    cmap = lambda j, i: (i, j)
    return _mm("pl_forward", (g, s // tm),
               [(n4, (tm, d), rmap, wplg_nat, (g * kb, nb), lambda j, i: (0, j), "nn", 0, 0),
                (p, (tm, kp), rmap, wpl, (None, kp, nb), lambda j, i: (j, 0, 0), "nn", 1, 0)],
               [(h3, (tm, nb), cmap)],
               [((s, d), F32, (tm, nb), cmap), ((s, d), F32, (tm, nb), cmap), ((s, d), BF, (tm, nb), cmap)],
               epi, nacc=2)


def _row_dx(name, dout, w, tm=1024, comm=()):
    s, n = dout.shape
    g, kb, _ = w.shape
    tm = _tile(s, tm)
    return _mm(name, (g, s // tm),
               [(dout, (tm, n), lambda j, i: (i, 0), w, (None, kb, n), lambda j, i: (j, 0, 0), "nt", 0, 0)], [],
               [((s, g * kb), F32, (tm, kb), lambda j, i: (i, j))], _store, comm=comm)[0]


def _in_proj_bwd_x(pieces, weights, tm=512, tn=512, comm=()):
    s = pieces[0].shape[0]
    d = weights[0].shape[1]
    tm, tn = _tile(s, tm), _tile(d, tn)
    prods = [(pc, (tm, pc.shape[1]), lambda j, i: (i, 0), w, (pc.shape[1], tn), lambda j, i: (0, j), "nn", 0, 0)
             for pc, w in zip(pieces, weights)]
    return _mm("in_proj_dx", (d // tn, s // tm), prods, [],
               [((s, d), F32, (tm, tn), lambda j, i: (i, j))], _store, comm=comm)[0]


def _split_w_in(w_in_t):
    g, nb, d = w_in_t.shape
    nat = w_in_t.reshape(g * nb, d)
    na, lat = 3 * NA_HEADS * NA_DIM, 2 * ML_RANK + ML_ROPE
    return nat, nat[na:na + lat], nat[na + lat:na + lat + d], nat[na + lat + d:]


def _pair_sum(name, part, landed, core):
    _, _, r, c = part.shape
    tr, tc = _ew_tile(r, c)

    def body(core_ref, a_ref, b_ref, o_ref):
        o_ref[...] = (a_ref[...].astype(F32) + b_ref[...].astype(F32)).astype(o_ref.dtype)

    return pl.pallas_call(
        body, name=name,
        grid_spec=pltpu.PrefetchScalarGridSpec(
            num_scalar_prefetch=1, grid=(NCHIP, r // tr, c // tc),
            in_specs=[pl.BlockSpec((None, None, tr, tc), lambda j, i, k, core_ref: (j, core_ref[0], i, k)),
                      pl.BlockSpec((None, tr, tc), lambda j, i, k, core_ref: (j, i, k))],
            out_specs=pl.BlockSpec((None, tr, tc), lambda j, i, k, core_ref: (j, i, k))),
        out_shape=jax.ShapeDtypeStruct(landed.shape, landed.dtype), compiler_params=_params(3),
    )(core, part, landed)


def _device_step(x, p, target, sp, own, core):
    s, d = x.shape
    rows = s // GRID_W
    cos, sin, rot, rot_t = _rope_consts(s)
    w, dw4, sums, dsp, pending = {}, {}, {}, {}, []

    def gather(*names):
        return _GatherPart(names, [own[n] for n in names])

    def got(part):
        w.update(zip(part.names, part.results))

    def grad(name, g):
        dw4[name] = g.reshape((NCHIP, 2) + g.shape[1:])

    def to_sibling(*names):
        return _SiblingPart(names, [dw4[n] for n in names])

    def add_pairs(part):
        for n, landed in zip(part.names, part.results):
            sums[n] = _pair_sum("pair_sum_" + n, dw4[n], landed, core)

    def start_chips(tag, *names):
        send, recv, thru, lands, token = _chips_start("rs_start_" + tag, [sums[n] for n in names])
        pending.append((tag, names, send, recv, thru, lands))
        return token

    c0 = gather("ffn1_w_gate", "ffn1_w_up")
    c1 = gather("ffn1_w_down")
    c2 = gather("w_in")

    def ffn1_wgu():
        got(c0)
        return w["ffn1_w_gate"], w["ffn1_w_up"]

    def ffn1_wd():
        got(c1)
        return w["ffn1_w_down"]

    h1, ffn1_saved = _ffn_forward("ffn1", x, sp["ffn1_norm"], ffn1_wgu, ffn1_wd,
                                  norm_comm=[c0], up_comm=[c1], down_comm=[c2])
    got(c2)
    wqkv, wlat, wga, wgb = _split_w_in(w["w_in"])
    u = _rms_fwd("mix_norm", h1, sp["mix_norm"])
    c3 = gather("w_uq", "w_ukv", "w_branch_a")
    qkv = _mm_nt("in_qkv", u, wqkv, BF, tn=1024, comm=[c3], rows=3 * NA_HEADS * NA_DIM)
    got(c3)
    lat = _mm_nt("in_lat", u, wlat, F32, tm=1024)
    c3a = gather("w_out")
    ga = _mm_nt("in_ga", u, wga, F32, tn=1024, comm=[c3a])
    got(c3a)
    c3b = gather("w_branch_b")
    gb = _mm_nt("in_gb", u, wgb, F32, tn=1024, comm=[c3b])
    got(c3b)
    tb = _na_table(sp["na_rpb"], rows)
    c4 = gather("ffn2_w_gate")
    oa = _na_fwd(qkv, tb, comm=[c4])
    got(c4)
    cq, ckv, kr = _mla_prep(lat, sp["q_a_norm"], sp["kv_a_norm"], cos, sin, rot)
    qf = _mla_q_proj(cq, w["w_uq"], cos, sin, rot)
    kf, vf = _mla_kv_proj(ckv, w["w_ukv"], kr)
    c5 = gather("ffn2_w_up")
    ob = _mla_fwd(qf, kf, vf, comm=[c5])
    got(c5)
    c5a = gather("w_pl", "w_pl_gate")
    ya, yb, merged = _mix_merge(oa, ob, w["w_branch_a"], w["w_branch_b"], ga, gb, comm=[c5a])
    got(c5a)
    h2 = _mix_out(merged, w["w_out"], h1)
    c6 = gather("ffn2_w_down")

    def ffn2_wd():
        got(c6)
        return w["ffn2_w_down"]

    h3, ffn2_saved = _ffn_forward("ffn2", h2, sp["ffn2_norm"], lambda: (w["ffn2_w_gate"], w["ffn2_w_up"]), ffn2_wd,
                                  up_comm=[c6])
    n4 = _rms_fwd("pl_norm", h3, sp["pl_norm"])
    pb = p.astype(BF)
    h4, t, pe = _pl_forward(n4, w["w_pl_gate"], pb, w["w_pl"], h3)

    dh4, dsp["final_norm"], loss = _loss_head(h4, target, sp["final_norm"])
    dt, dpe = _pl_bwd_elem(dh4, pe, t)
    grad("w_pl", _grp_dw("pl_dw", pb, dpe))
    grad("w_pl_gate", _row_dw("plg_dw", n4, dt))
    s1 = to_sibling("w_pl", "w_pl_gate")
    dn4 = _row_dx("plg_dx", dt, w["w_pl_gate"], comm=[s1])
    add_pairs(s1)
    dh3, dhb, dsp["pl_norm"] = _rms_bwd("pl_dnorm", dn4, h3, sp["pl_norm"], dh4)

    xn, hg, hu, a = ffn2_saved
    grad("ffn2_w_down", _ffn_bwd_wd("ffn2_dwd", a, dhb))
    s2 = to_sibling("ffn2_w_down")
    dhg, dhu = _ffn_bwd_act("ffn2_dact", dhb, w["ffn2_w_down"], hg, hu, comm=[s2])
    add_pairs(s2)
    tok = start_chips("ffn2_down", "w_pl", "w_pl_gate", "ffn2_w_down")
    dwg, dwu = _ffn_bwd_wup("ffn2_dwup", xn, dhg, dhu, comm=[_After(tok)])
    grad("ffn2_w_gate", dwg)
    grad("ffn2_w_up", dwu)
    s3 = to_sibling("ffn2_w_gate", "ffn2_w_up")
    dxn = _ffn_bwd_x("ffn2_dx", dhg, dhu, w["ffn2_w_gate"], w["ffn2_w_up"], comm=[s3])
    add_pairs(s3)
    tok = start_chips("ffn2_up", "ffn2_w_gate", "ffn2_w_up")
    dh2, dh2b, dsp["ffn2_norm"] = _rms_bwd("ffn2_dnorm", dxn, h2, sp["ffn2_norm"], dh3, comm=[_After(tok)])

    grad("w_out", _row_dw("out_dw", merged, dh2b))
    s4 = to_sibling("w_out")
    dya, dyb, dga, dgb = _mix_out_bwd(dh2b, w["w_out"], ga, gb, ya, yb, comm=[s4])
    add_pairs(s4)
    grad("w_branch_a", _grp_dw("bra_dw", oa, dya))
    grad("w_branch_b", _grp_dw("brb_dw", ob, dyb))
    doa = _grp_dx("bra_dx", dya, w["w_branch_a"], out_dtype=BF)
    s5 = to_sibling("w_branch_a", "w_branch_b")
    dob = _grp_dx("brb_dx", dyb, w["w_branch_b"], out_dtype=BF, comm=[s5])
    add_pairs(s5)

    dqf, dkf, dvf = _mla_bwd(qf, kf, vf, dob)
    dqp, dkv, dkr = _mla_post(dqf, dkf, dvf, cos, sin, rot_t)
    grad("w_uq", _grp_dw_t("uq_dw", dqp, cq))
    grad("w_ukv", _grp_dw("ukv_dw", ckv, dkv))
    dcq = _grp_dx_t("uq_dx", dqp, w["w_uq"])
    s6 = to_sibling("w_uq", "w_ukv")
    dckv = _grp_dx("ukv_dx", dkv, w["w_ukv"], comm=[s6])
    add_pairs(s6)
    dlat, dsp["q_a_norm"], dsp["kv_a_norm"] = _mla_lat_bwd(dcq, dckv, dkr, lat, sp["q_a_norm"], sp["kv_a_norm"],
                                                         cos, sin, rot_t)
    dq_na, dk_na, dv_na, dtab = _na_bwd(qkv, tb, doa)
    dsp["na_rpb"] = _na_rpb_grad(dtab, rows)
    dqkv = jnp.concatenate([dq_na, dk_na.astype(BF), dv_na.astype(BF)], axis=1)

    pieces = [dqkv, dlat, dga, dgb]
    dwin = jnp.zeros((sum(pc.shape[1] for pc in pieces), d), BF)
    row0 = 0
    for i, pc in enumerate(pieces):
        dwin = _mm_tn_into("in_dw%d" % i, pc, u, dwin, row0)
        row0 += pc.shape[1]
    grad("w_in", dwin.reshape(NDEV, -1, d))
    s7 = to_sibling("w_in")
    du = _in_proj_bwd_x(pieces, [wqkv, wlat, wga, wgb], comm=[s7])
    add_pairs(s7)
    tok = start_chips("w_in", "w_out", "w_branch_a", "w_branch_b", "w_uq", "w_ukv", "w_in")
    dh1, dhb, dsp["mix_norm"] = _rms_bwd("mix_dnorm", du, h1, sp["mix_norm"], dh2, comm=[_After(tok)])

    xn, hg, hu, a = ffn1_saved
    grad("ffn1_w_down", _ffn_bwd_wd("ffn1_dwd", a, dhb))
    s8 = to_sibling("ffn1_w_down")
    dhg, dhu = _ffn_bwd_act("ffn1_dact", dhb, w["ffn1_w_down"], hg, hu, comm=[s8])
    add_pairs(s8)
    tok = start_chips("ffn1_down", "ffn1_w_down")
    dwg, dwu = _ffn_bwd_wup("ffn1_dwup", xn, dhg, dhu, comm=[_After(tok)])
    grad("ffn1_w_gate", dwg)
    grad("ffn1_w_up", dwu)
    s9 = to_sibling("ffn1_w_gate", "ffn1_w_up")
    _comm_only("rs_sibling_ffn1", [s9])
    add_pairs(s9)
    tok = start_chips("ffn1_up", "ffn1_w_gate", "ffn1_w_up")
    dxn = _ffn_bwd_x("ffn1_dx", dhg, dhu, w["ffn1_w_gate"], w["ffn1_w_up"], comm=[_After(tok)])
    dx, _, dsp["ffn1_norm"] = _rms_bwd("ffn1_dnorm", dxn, x, sp["ffn1_norm"], dh1)
    return loss, dx, pending, dsp


def _small_peers():
    x, y, c = _coords()
    return [(x ^ ((k >> 2) & 1), y ^ ((k >> 1) & 1), c ^ (k & 1)) for k in range(1, NDEV)]


def _small_start(buf):
    def body(b_ref, z_ref, send_sems, recv_sems, b_thru, z_thru, token):
        x, y, c = _coords()
        for k, peer in enumerate(_small_peers()):
            pltpu.make_async_remote_copy(
                src_ref=b_ref, dst_ref=z_ref.at[4 * x + 2 * y + c], send_sem=send_sems.at[k],
                recv_sem=recv_sems.at[k], device_id=peer, device_id_type=MESH).start()
        token[...] = jnp.zeros_like(token)

    zone = lax.empty((NDEV,) + buf.shape, buf.dtype)
    res = pl.pallas_call(
        body, name="small_start", in_specs=[HBM, HBM],
        out_specs=(SEM, SEM, HBM, HBM, pl.BlockSpec(memory_space=pltpu.VMEM)),
        out_shape=(pltpu.SemaphoreType.DMA((NDEV - 1,)), pltpu.SemaphoreType.DMA((NDEV - 1,)),
                   pltpu.HBM(buf.shape, buf.dtype), pltpu.HBM(zone.shape, zone.dtype),
                   jax.ShapeDtypeStruct((8, 128), F32)),
        input_output_aliases={0: 2, 1: 3},
        compiler_params=pltpu.CompilerParams(has_side_effects=pltpu.SideEffectType.DATAFLOW_SIDE_EFFECTING),
    )(pltpu.with_memory_space_constraint(buf, pltpu.HBM), pltpu.with_memory_space_constraint(zone, pltpu.HBM))
    return res


def _small_wait(send_sems, recv_sems, buf, zone, after):
    def body(b_ref, z_ref, send, recv, after_ref, b_out, z_out):
        for k, (px, py, pc) in enumerate(_small_peers()):
            cp = pltpu.make_async_remote_copy(
                src_ref=b_ref, dst_ref=z_ref.at[4 * px + 2 * py + pc], send_sem=send.at[k], recv_sem=recv.at[k],
                device_id=(px, py, pc), device_id_type=MESH)
            cp.wait_send()
            cp.wait_recv()

    return pl.pallas_call(
        body, name="small_wait", in_specs=[HBM, HBM, SEM, SEM, ANY], out_specs=[HBM, HBM],
        out_shape=[pltpu.HBM(buf.shape, buf.dtype), pltpu.HBM(zone.shape, zone.dtype)],
        input_output_aliases={0: 0, 1: 1},
        compiler_params=pltpu.CompilerParams(has_side_effects=pltpu.SideEffectType.DATAFLOW_SIDE_EFFECTING),
    )(buf, zone, send_sems, recv_sems, after)


def _adam_math(wv, g, m, v):
    m_new = B1 * m + (1.0 - B1) * g
    v_new = B2 * v + (1.0 - B2) * (g * g)
    m_hat = m_new / (1.0 - B1 ** STEP)
    v_hat = v_new / (1.0 - B2 ** STEP)
    return -LR * (m_hat / (jnp.sqrt(v_hat) + ADAM_EPS) + WD * wv), m_new, v_new


def _adam_replicated(name, zone, own, wv, m, v, me):
    ndev, r, c = zone.shape

    def body(me_ref, z_ref, o_ref, w_ref, m_ref, v_ref, g_ref, d_ref, mo_ref, vo_ref):
        g = jnp.zeros((r, c), F32)
        for d in range(ndev):
            g = g + jnp.where(me_ref[0] == d, o_ref[...], z_ref[d])
        g_ref[...] = g
        d_ref[...], mo_ref[...], vo_ref[...] = _adam_math(w_ref[...], g, m_ref[...], v_ref[...])

    blk = pl.BlockSpec((r, c), lambda i, me_ref: (0, 0))
    return pl.pallas_call(
        body, name=name,
        grid_spec=pltpu.PrefetchScalarGridSpec(
            num_scalar_prefetch=1, grid=(1,),
            in_specs=[pl.BlockSpec((ndev, r, c), lambda i, me_ref: (0, 0, 0)), blk, blk, blk, blk],
            out_specs=[blk] * 4),
        out_shape=[jax.ShapeDtypeStruct((r, c), F32)] * 4, compiler_params=_params(1),
    )(me, zone, own, wv, m, v)


def _adam_exchanged(name, sums, land, wv, m, v, my_chip):
    _, r, c = sums.shape
    tr, tc = _ew_tile(r, c)

    def body(chip_ref, s_ref, l_ref, w_ref, m_ref, v_ref, g_ref, d_ref, mo_ref, vo_ref):
        g = s_ref[...].astype(F32)
        for j in range(3):
            g = g + l_ref[j].astype(F32)
        g_ref[...] = g
        d_ref[...], mo_ref[...], vo_ref[...] = _adam_math(w_ref[...], g, m_ref[...], v_ref[...])

    blk = pl.BlockSpec((tr, tc), lambda i, k, chip_ref: (i, k))
    return pl.pallas_call(
        body, name=name,
        grid_spec=pltpu.PrefetchScalarGridSpec(
            num_scalar_prefetch=1, grid=(r // tr, c // tc),
            in_specs=[pl.BlockSpec((None, tr, tc), lambda i, k, chip_ref: (chip_ref[0], i, k)),
                      pl.BlockSpec((3, tr, tc), lambda i, k, chip_ref: (0, i, k)), blk, blk, blk],
            out_specs=[blk] * 4),
        out_shape=[jax.ShapeDtypeStruct((r, c), F32)] * 4, compiler_params=_params(2),
    )(my_chip, sums, land, wv, m, v)


SHARDED = ("ffn1_w_gate", "ffn1_w_up", "ffn1_w_down", "w_in", "w_uq", "w_ukv", "w_branch_a", "w_branch_b", "w_out",
           "ffn2_w_gate", "ffn2_w_up", "ffn2_w_down", "w_pl", "w_pl_gate")
TRANSPOSED = ("ffn1_w_gate", "ffn1_w_up", "ffn2_w_gate", "ffn2_w_up", "w_in", "w_uq")
REPLICATED = ("ffn1_norm", "mix_norm", "q_a_norm", "kv_a_norm", "na_rpb", "ffn2_norm", "pl_norm", "final_norm")
WEIGHTS = ("ffn1_norm", "ffn1_w_gate", "ffn1_w_up", "ffn1_w_down", "mix_norm", "w_in", "q_a_norm", "w_uq",
           "kv_a_norm", "w_ukv", "na_rpb", "w_branch_a", "w_branch_b", "w_out", "ffn2_norm", "ffn2_w_gate",
           "ffn2_w_up", "ffn2_w_down", "pl_norm", "w_pl", "w_pl_gate", "final_norm")
SMALL_W = 2048


def _pack_small(vals):
    rows = []
    for name in REPLICATED:
        flat = vals[name].reshape(-1).astype(F32)
        n = -(-flat.shape[0] // SMALL_W) * SMALL_W
        rows.append(jnp.pad(flat, (0, n - flat.shape[0])).reshape(-1, SMALL_W))
    return jnp.concatenate(rows, axis=0)


def _unpack_small(buf, shapes):
    out, r = {}, 0
    for name in REPLICATED:
        size = int(np.prod(shapes[name]))
        nrow = -(-size // SMALL_W)
        out[name] = buf[r:r + nrow].reshape(-1)[:size].reshape(shapes[name])
        r += nrow
    return out


def kernel(x, p, ffn1_norm, ffn1_w_gate, ffn1_w_up, ffn1_w_down, mix_norm, w_in, q_a_norm, w_uq, kv_a_norm, w_ukv, na_rpb, w_branch_a, w_branch_b, w_out, ffn2_norm, ffn2_w_gate, ffn2_w_up, ffn2_w_down, pl_norm, w_pl, w_pl_gate, final_norm, loss_target, m_ffn1_norm, m_ffn1_w_gate, m_ffn1_w_up, m_ffn1_w_down, m_mix_norm, m_w_in, m_q_a_norm, m_w_uq, m_kv_a_norm, m_w_ukv, m_na_rpb, m_w_branch_a, m_w_branch_b, m_w_out, m_ffn2_norm, m_ffn2_w_gate, m_ffn2_w_up, m_ffn2_w_down, m_pl_norm, m_w_pl, m_w_pl_gate, m_final_norm, v_ffn1_norm, v_ffn1_w_gate, v_ffn1_w_up, v_ffn1_w_down, v_mix_norm, v_w_in, v_q_a_norm, v_w_uq, v_kv_a_norm, v_w_ukv, v_na_rpb, v_w_branch_a, v_w_branch_b, v_w_out, v_ffn2_norm, v_ffn2_w_gate, v_ffn2_w_up, v_ffn2_w_down, v_pl_norm, v_w_pl, v_w_pl_gate, v_final_norm):
    args = dict(locals())
    wts = {n: args[n] for n in WEIGHTS}
    mom = {n: args["m_" + n] for n in WEIGHTS}
    var = {n: args["v_" + n] for n in WEIGHTS}
    shapes = {n: wts[n].shape for n in WEIGHTS}
    core = lax.axis_index("c").astype(jnp.int32).reshape(1)

    local = lambda n, a: a[0].T if n in TRANSPOSED else a[0]
    own = {n: local(n, wts[n]).astype(BF) for n in SHARDED}
    sp = {n: wts[n].reshape(1, -1) for n in REPLICATED if n != "na_rpb"}
    sp["na_rpb"] = wts["na_rpb"][0]
    loss_part, grad_x, pending, dsp = _device_step(x[0], p[0, 0], loss_target[0], sp, own, core)

    small = jnp.concatenate([_pack_small(dsp), jnp.pad(loss_part, ((0, 0), (0, SMALL_W - loss_part.shape[1])))], 0)
    pad_rows = -small.shape[0] % 8
    small = jnp.pad(small, ((0, pad_rows), (0, 0)))
    s_send, s_recv, s_buf, s_zone, last = _small_start(small)

    out = {}
    my_chip = (2 * lax.axis_index("x") + lax.axis_index("y")).astype(jnp.int32).reshape(1)
    for tag, names, send, recv, thru, lands in pending:
        thru, lands = _chips_wait("rs_wait_" + tag, send, recv, thru, lands, last)
        for n, s4, l3 in zip(names, thru, lands):
            res4 = _adam_exchanged("adam_" + n, s4, l3, local(n, wts[n]), local(n, mom[n]), local(n, var[n]), my_chip)
            out[n] = tuple((a.T if n in TRANSPOSED else a)[None] for a in res4)
            last = res4[1]

    s_buf, s_zone = _small_wait(s_send, s_recv, s_buf, s_zone, last)
    zeros = jnp.zeros((1 + pad_rows, SMALL_W), F32)
    pack = lambda d: jnp.concatenate([_pack_small(d), zeros], 0)
    me = 2 * my_chip + core
    g_s, d_s, m_s, v_s = _adam_replicated("adam_small", s_zone, s_buf, pack(wts), pack(mom), pack(var), me)
    n_rows = small.shape[0] - 1 - pad_rows
    loss = g_s[n_rows, 0]
    small_out = [_unpack_small(b, shapes) for b in (g_s, d_s, m_s, v_s)]
    for n in REPLICATED:
        out[n] = tuple(b[n] for b in small_out)

    res = [loss, grad_x[None]]
    for k in range(4):
        res += [out[n][k] for n in WEIGHTS]
    return tuple(res)
```

```python
import functools

import numpy as np
import jax
import jax.numpy as jnp
from jax import lax
from jax.experimental import pallas as pl
from jax.experimental.pallas import tpu as pltpu

F32 = jnp.float32
BF = jnp.bfloat16
MESH = pl.DeviceIdType.MESH

NDEV = 8
NCHIP = 4
VMEM_LIMIT = 56 * 1024 * 1024
EPS = 1e-6
NEG = -1e30
GRID_W = 64
NA_HEADS, NA_DIM = 8, 128
NA_ROWS_WIN, NA_COLS_WIN = 8, 16
NA_HG = 4
NA_QROWS = 4
ML_HEADS, ML_NOPE, ML_ROPE, ML_V = 8, 128, 64, 128
ML_QK = ML_NOPE + ML_ROPE
ML_RANK = 512
ROPE_THETA = 10000.0
LR, B1, B2, ADAM_EPS, WD, STEP = 0.001, 0.9, 0.999, 1e-08, 0.01, 10
HI = lax.Precision.HIGHEST

_DN = {"nn": (((1,), (0,)), ((), ())), "nt": (((1,), (1,)), ((), ())), "tn": (((0,), (0,)), ((), ()))}


def _params(n):
    return pltpu.CompilerParams(dimension_semantics=("arbitrary",) * n, vmem_limit_bytes=VMEM_LIMIT)


def _sig(v):
    return jax.nn.sigmoid(v)


ANY = pl.BlockSpec(memory_space=pl.ANY)


def _coords():
    return lax.axis_index("x"), lax.axis_index("y"), lax.axis_index("c")


class _Part:
    inputs, out_shapes, sem_shapes, results = (), (), (), None

    def mid(self, ins, outs, sems):
        pass

    def late(self, ins, outs, sems):
        pass


class _After(_Part):
    def __init__(self, token):
        self.inputs = [token]

    def start(self, ins, outs, sems):
        pass

    finish = start


class _GatherPart(_Part):
    def __init__(self, names, shards):
        n = len(shards)
        self.names, self.inputs = list(names), list(shards)
        self.out_shapes = [jax.ShapeDtypeStruct((NDEV,) + a.shape, a.dtype) for a in shards]
        self.sem_shapes = [pltpu.SemaphoreType.DMA((n, 7)), pltpu.SemaphoreType.DMA((n, 7)),
                           pltpu.SemaphoreType.DMA((n,))]

    def _plan(self, ins, outs, sems):
        send_sems, recv_sems, local_sems = sems
        x, y, c = _coords()
        me, sib, diag = (x, y, c), (x, y, 1 - c), (1 - x, 1 - y, c)
        n1, n2 = (x ^ (1 - c), y ^ c, c), (x ^ c, y ^ (1 - c), c)

        def copy(i, k, block, to, src=None):
            px, py, pc = block
            dst = outs[i].at[4 * px + 2 * py + pc]
            return pltpu.make_async_remote_copy(
                src_ref=dst if src is None else src, dst_ref=dst, send_sem=send_sems.at[i, k],
                recv_sem=recv_sems.at[i, k], device_id=to, device_id_type=MESH)

        mine = [pltpu.make_async_copy(ins[i], outs[i].at[4 * x + 2 * y + c], local_sems.at[i])
                for i in range(len(ins))]
        return copy, mine, me, sib, n1, n2, diag

    def _own_sends(self, ins, copy, me, sib, n1, n2):
        return [copy(i, k, me, to, src=ins[i]) for i in range(len(ins)) for k, to in enumerate((sib, n1, n2))]

    def start(self, ins, outs, sems):
        copy, mine, me, sib, n1, n2, _ = self._plan(ins, outs, sems)
        for cp in mine + self._own_sends(ins, copy, me, sib, n1, n2):
            cp.start()

    def mid(self, ins, outs, sems):
        copy, _, me, sib, n1, n2, _ = self._plan(ins, outs, sems)
        for i in range(len(ins)):
            copy(i, 1, n1, me).wait_recv()
            copy(i, 3, n1, n2).start()
            copy(i, 4, n1, sib).start()

    def late(self, ins, outs, sems):
        copy, _, me, sib, _, n2, diag = self._plan(ins, outs, sems)
        for i in range(len(ins)):
            copy(i, 2, n2, me).wait_recv()
            copy(i, 5, n2, sib).start()
        for i in range(len(ins)):
            copy(i, 3, diag, me).wait_recv()
            copy(i, 6, diag, sib).start()

    def finish(self, ins, outs, sems):
        copy, mine, me, sib, n1, n2, diag = self._plan(ins, outs, sems)
        other = lambda dev: (dev[0], dev[1], sib[2])
        n = len(ins)
        for i in range(n):
            copy(i, 0, sib, me).wait_recv()
            for k, block in ((4, other(n2)), (5, other(n1)), (6, other(diag))):
                copy(i, k, block, me).wait_recv()
        for cp in self._own_sends(ins, copy, me, sib, n1, n2):
            cp.wait_send()
        for i in range(n):
            for k, block in ((3, n1), (4, n1), (5, n2), (6, diag)):
                copy(i, k, block, sib).wait_send()
        for cp in mine:
            cp.wait()


class _SiblingPart(_Part):
    def __init__(self, names, parts):
        n = len(parts)
        self.names, self.inputs = list(names), list(parts)
        self.out_shapes = [jax.ShapeDtypeStruct((NCHIP,) + a.shape[2:], a.dtype) for a in parts]
        self.sem_shapes = [pltpu.SemaphoreType.DMA((n,)), pltpu.SemaphoreType.DMA((n,))]

    def _copies(self, ins, outs, sems):
        x, y, c = _coords()
        return [pltpu.make_async_remote_copy(
            src_ref=ins[i].at[:, 1 - c], dst_ref=outs[i], send_sem=sems[0].at[i], recv_sem=sems[1].at[i],
            device_id=(x, y, 1 - c), device_id_type=MESH) for i in range(len(ins))]

    def start(self, ins, outs, sems):
        for cp in self._copies(ins, outs, sems):
            cp.start()

    def finish(self, ins, outs, sems):
        cps = self._copies(ins, outs, sems)
        for cp in cps:
            cp.wait_recv()
        for cp in cps:
            cp.wait_send()


HBM = pl.BlockSpec(memory_space=pltpu.HBM)
SEM = pl.BlockSpec(memory_space=pltpu.SEMAPHORE)


def _chip_peers():
    x, y, c = _coords()
    return [(1 - x, y, c), (x, 1 - y, c), (1 - x, 1 - y, c)]


def _chips_start(name, sums):
    n = len(sums)

    def body(*refs):
        ins, lands, send_sems, recv_sems = refs[:n], refs[n:2 * n], refs[2 * n], refs[2 * n + 1]
        for i in range(n):
            for k, (px, py, pc) in enumerate(_chip_peers()):
                pltpu.make_async_remote_copy(
                    src_ref=ins[i].at[2 * px + py], dst_ref=lands[i].at[k], send_sem=send_sems.at[3 * i + k],
                    recv_sem=recv_sems.at[3 * i + k], device_id=(px, py, pc), device_id_type=MESH).start()
        refs[-1][...] = jnp.zeros_like(refs[-1])

    lands = [lax.empty((3,) + a.shape[1:], a.dtype) for a in sums]
    bufs = list(sums) + lands
    res = pl.pallas_call(
        body, name=name, in_specs=[HBM] * (2 * n),
        out_specs=(SEM, SEM, *[HBM] * (2 * n), pl.BlockSpec(memory_space=pltpu.VMEM)),
        out_shape=(pltpu.SemaphoreType.DMA((3 * n,)), pltpu.SemaphoreType.DMA((3 * n,)),
                   *[pltpu.HBM(a.shape, a.dtype) for a in bufs], jax.ShapeDtypeStruct((8, 128), F32)),
        input_output_aliases={i: 2 + i for i in range(2 * n)},
        compiler_params=pltpu.CompilerParams(has_side_effects=pltpu.SideEffectType.DATAFLOW_SIDE_EFFECTING),
    )(*[pltpu.with_memory_space_constraint(a, pltpu.HBM) for a in bufs])
    return res[0], res[1], list(res[2:2 + n]), list(res[2 + n:2 + 2 * n]), res[-1]


def _chips_wait(name, send_sems, recv_sems, sums, lands, after):
    n = len(sums)

    def body(*refs):
        ins, zones, send, recv = refs[:n], refs[n:2 * n], refs[2 * n], refs[2 * n + 1]
        for i in range(n):
            for k, peer in enumerate(_chip_peers()):
                cp = pltpu.make_async_remote_copy(
                    src_ref=ins[i].at[0], dst_ref=zones[i].at[k], send_sem=send.at[3 * i + k],
                    recv_sem=recv.at[3 * i + k],
                    device_id=peer, device_id_type=MESH)
                cp.wait_send()
                cp.wait_recv()

    bufs = list(sums) + list(lands)
    res = pl.pallas_call(
        body, name=name, in_specs=[HBM] * (2 * n) + [SEM, SEM, ANY], out_specs=[HBM] * (2 * n),
        out_shape=[pltpu.HBM(a.shape, a.dtype) for a in bufs], input_output_aliases={i: i for i in range(2 * n)},
        compiler_params=pltpu.CompilerParams(has_side_effects=pltpu.SideEffectType.DATAFLOW_SIDE_EFFECTING),
    )(*bufs, send_sems, recv_sems, after)
    return list(res[:n]), list(res[n:])


def _call(name, body, grid, in_specs, out_specs, out_shape, args, comm=(), scratch=()):
    comm = [p for p in comm if p is not None]
    single = not isinstance(out_shape, (list, tuple))
    o_specs = [out_specs] if single else list(out_specs)
    o_shape = [out_shape] if single else list(out_shape)
    n_in, n_out = len(in_specs), len(o_specs)
    c_in = [a for p in comm for a in p.inputs]
    c_out = [s for p in comm for s in p.out_shapes]
    c_sem = [s for p in comm for s in p.sem_shapes]

    def wrapped(*refs):
        ins, outs = refs[:n_in], refs[n_in + len(c_in):n_in + len(c_in) + n_out]
        pos = [n_in, n_in + len(c_in) + n_out, n_in + len(c_in) + n_out + len(c_out)]
        own = refs[pos[2]:pos[2] + len(scratch)]
        pos[2] += len(scratch)
        split = []
        for p in comm:
            sizes = [len(p.inputs), len(p.out_shapes), len(p.sem_shapes)]
            split.append([refs[o:o + n] for o, n in zip(pos, sizes)])
            pos = [o + n for o, n in zip(pos, sizes)]
        step, steps = 0, 1
        for a, g in enumerate(grid):
            step, steps = step * g + pl.program_id(a), steps * g

        def run(which, at):
            def go():
                for p, cut in zip(comm, split):
                    getattr(p, which)(*cut)
            if not comm:
                return
            if grid:
                pl.when(step == at)(go)
            else:
                go()

        run("start", 0)
        body(*ins, *outs, *own)
        run("mid", steps // 2)
        run("late", max(steps // 2, steps - 1 - max(1, steps // 8)))
        run("finish", steps - 1)

    res = pl.pallas_call(
        wrapped, name=name, grid=grid, in_specs=list(in_specs) + [ANY] * len(c_in),
        out_specs=o_specs + [ANY] * len(c_out), out_shape=o_shape + c_out, scratch_shapes=list(scratch) + c_sem,
        compiler_params=_params(len(grid)),
    )(*args, *c_in)
    pos = n_out
    for p in comm:
        p.results = list(res[pos:pos + len(p.out_shapes)])
        pos += len(p.out_shapes)
    return res[0] if single else list(res[:n_out])


def _comm_only(name, comm):
    def body(o_ref):
        o_ref[...] = jnp.zeros_like(o_ref)

    _call(name, body, (), [], pl.BlockSpec(memory_space=pltpu.VMEM), jax.ShapeDtypeStruct((8, 128), F32), [], comm)


def _mm(name, grid, prods, extras, outs, epi, nacc=1, comm=()):
    n_p, n_e = len(prods), len(extras)

    def body(*refs):
        ab, ex, out = refs[:2 * n_p], refs[2 * n_p:2 * n_p + n_e], refs[2 * n_p + n_e:]
        accs = [None] * nacc
        for i, prod in enumerate(prods):
            dn, acc, loop = prod[6], prod[7], prod[8]
            a_ref, b_ref = ab[2 * i], ab[2 * i + 1]
            if loop:
                for g in range(loop):
                    t = lax.dot_general(a_ref[g], b_ref[g], _DN[dn], preferred_element_type=F32)
                    accs[acc] = t if accs[acc] is None else accs[acc] + t
            else:
                t = lax.dot_general(a_ref[...], b_ref[...], _DN[dn], preferred_element_type=F32)
                accs[acc] = t if accs[acc] is None else accs[acc] + t
        epi(accs, ex, out)

    in_specs, args = [], []
    for prod in prods:
        in_specs += [pl.BlockSpec(prod[1], prod[2]), pl.BlockSpec(prod[4], prod[5])]
        args += [prod[0], prod[3]]
    for e, e_blk, e_map in extras:
        in_specs.append(pl.BlockSpec(e_blk, e_map))
        args.append(e)
    return _call(name, body, grid, in_specs, [pl.BlockSpec(blk, mp) for _, _, blk, mp in outs],
                 [jax.ShapeDtypeStruct(s, d) for s, d, _, _ in outs], args, comm)


def _store(accs, ex, out):
    out[0][...] = accs[0].astype(out[0].dtype)


def _ew_tile(r, c, budget=3 << 19):
    for t in range(r - r % 16, 0, -16):
        if r % t == 0 and t * c * 4 <= budget:
            return t, c
    for t in range(c - c % 128, 0, -128):
        if c % t == 0 and r * t * 4 <= budget:
            return r, t
    return r, c


def _tile(n, want):
    t = min(n, want)
    assert n % t == 0, (n, want)
    return t


def _mm_nt(name, a, bt, out_dtype, tm=512, tn=512, comm=(), rows=None):
    m, k = a.shape
    n = rows or bt.shape[0]
    tm, tn = _tile(m, tm), (tn if n % tn == 0 else n)
    return _mm(name, (n // tn, m // tm),
               [(a, (tm, k), lambda j, i: (i, 0), bt, (tn, k), lambda j, i: (j, 0), "nt", 0, 0)], [],
               [((m, n), out_dtype, (tm, tn), lambda j, i: (i, j))], _store, comm=comm)[0]


def _mm_tn_into(name, a, b, buf, row0, ta=1024, tb=512):
    t, ka = a.shape
    nb = b.shape[1]
    ta, tb = (ta if ka % ta == 0 else ka), (tb if nb % tb == 0 else nb)

    def body(a_ref, b_ref, buf_in, buf_out, tile, sem):
        i, j = pl.program_id(0), pl.program_id(1)
        tile[...] = lax.dot_general(a_ref[...], b_ref[...], _DN["tn"], preferred_element_type=F32).astype(tile.dtype)
        rows = pl.ds(pl.multiple_of(row0 + i * ta, 16), ta)
        cp = pltpu.make_async_copy(tile, buf_out.at[rows, pl.ds(pl.multiple_of(j * tb, 128), tb)], sem)
        cp.start()
        cp.wait()

    return pl.pallas_call(
        body, name=name, grid=(ka // ta, nb // tb),
        in_specs=[pl.BlockSpec((t, ta), lambda i, j: (0, i)), pl.BlockSpec((t, tb), lambda i, j: (0, j)), ANY],
        out_specs=ANY, out_shape=jax.ShapeDtypeStruct(buf.shape, buf.dtype), input_output_aliases={2: 0},
        scratch_shapes=[pltpu.VMEM((ta, tb), buf.dtype), pltpu.SemaphoreType.DMA],
        compiler_params=_params(2))(a, b, buf)


def _rms_fwd(name, x, g, tm=256, comm=()):
    s, d = x.shape
    tm = _tile(s, tm)

    def body(x_ref, g_ref, o_ref):
        v = x_ref[...]
        o_ref[...] = (v * lax.rsqrt(jnp.mean(v * v, axis=-1, keepdims=True) + EPS) * g_ref[...]).astype(o_ref.dtype)

    return _call(name, body, (s // tm,),
                 [pl.BlockSpec((tm, d), lambda i: (i, 0)), pl.BlockSpec((1, d), lambda i: (0, 0))],
                 pl.BlockSpec((tm, d), lambda i: (i, 0)), jax.ShapeDtypeStruct((s, d), BF), [x, g], comm)


def _acc_rows(ref, part, i):
    @pl.when(i == 0)
    def _():
        ref[...] = part

    @pl.when(i > 0)
    def _():
        ref[...] += part


def _rms_bwd_math(dn, v, g):
    rstd = lax.rsqrt(jnp.mean(v * v, axis=-1, keepdims=True) + EPS)
    xh = v * rstd
    dxh = dn * g
    dx = rstd * (dxh - xh * jnp.mean(dxh * xh, axis=-1, keepdims=True))
    return dx, jnp.sum(dn * xh, axis=0, keepdims=True)


def _rms_bwd(name, dn, x, g, resid, tm=256, comm=()):
    s, d = x.shape
    tm = _tile(s, tm)

    def body(dn_ref, x_ref, g_ref, r_ref, dx_ref, dxb_ref, dg_ref):
        dx, part = _rms_bwd_math(dn_ref[...].astype(F32), x_ref[...], g_ref[...])
        tot = r_ref[...] + dx
        dx_ref[...] = tot
        dxb_ref[...] = tot.astype(BF)
        _acc_rows(dg_ref, part, pl.program_id(0))

    row = pl.BlockSpec((tm, d), lambda i: (i, 0))
    one = pl.BlockSpec((1, d), lambda i: (0, 0))
    return _call(name, body, (s // tm,), [row, row, one, row], [row, row, one],
                 [jax.ShapeDtypeStruct((s, d), F32), jax.ShapeDtypeStruct((s, d), BF),
                  jax.ShapeDtypeStruct((1, d), F32)], [dn, x, g, resid], comm)


def _loss_head(h, target, g, tm=256):
    s, d = h.shape
    tm = _tile(s, tm)

    def body(h_ref, t_ref, g_ref, dh_ref, dg_ref, loss_ref):
        v, gv = h_ref[...], g_ref[...]
        rstd = lax.rsqrt(jnp.mean(v * v, axis=-1, keepdims=True) + EPS)
        xh = v * rstd
        err = xh * gv - t_ref[...]
        part_loss = 0.5 * jnp.sum(jnp.mean(err * err, axis=-1, keepdims=True), axis=0, keepdims=True)
        dy = err * (1.0 / d)
        dxh = dy * gv
        dh_ref[...] = rstd * (dxh - xh * jnp.mean(dxh * xh, axis=-1, keepdims=True))
        i = pl.program_id(0)
        _acc_rows(dg_ref, jnp.sum(dy * xh, axis=0, keepdims=True), i)
        _acc_rows(loss_ref, jnp.broadcast_to(part_loss, loss_ref.shape), i)

    row = pl.BlockSpec((tm, d), lambda i: (i, 0))
    one = pl.BlockSpec((1, d), lambda i: (0, 0))
    return pl.pallas_call(
        body, name="loss_head", grid=(s // tm,), in_specs=[row, row, one],
        out_specs=[row, one, pl.BlockSpec((1, 128), lambda i: (0, 0))],
        out_shape=[jax.ShapeDtypeStruct((s, d), F32), jax.ShapeDtypeStruct((1, d), F32),
                   jax.ShapeDtypeStruct((1, 128), F32)],
        compiler_params=_params(1))(h, target, g)


def _pl_bwd_elem(dh, pe, t, tm=256):
    s, d = dh.shape
    tm = _tile(s, tm)

    def body(dh_ref, pe_ref, t_ref, dt_ref, dpe_ref):
        dh_v, sg = dh_ref[...], _sig(t_ref[...])
        dt_ref[...] = (dh_v * pe_ref[...].astype(F32) * sg * (1.0 - sg)).astype(BF)
        dpe_ref[...] = (dh_v * sg).astype(BF)

    row = pl.BlockSpec((tm, d), lambda i: (i, 0))
    return pl.pallas_call(
        body, name="pl_bwd_elem", grid=(s // tm,), in_specs=[row, row, row], out_specs=[row, row],
        out_shape=[jax.ShapeDtypeStruct((s, d), BF)] * 2, compiler_params=_params(1))(dh, pe, t)


def _ffn_up(name, xn, wg, wu, tm=1024, comm=()):
    s, d = xn.shape
    g, fb, _ = wg.shape
    tm = _tile(s, tm)

    def epi(accs, ex, out):
        hg, hu = accs
        out[0][...] = hg.astype(BF)
        out[1][...] = hu.astype(BF)
        out[2][...] = (hg * _sig(hg) * hu).astype(BF)

    a_map = lambda j, i: (i, 0)
    w_map = lambda j, i: (j, 0, 0)
    o = ((g, s, fb), BF, (None, tm, fb), lambda j, i: (j, i, 0))
    return _mm(name, (g, s // tm),
               [(xn, (tm, d), a_map, wg, (None, fb, d), w_map, "nt", 0, 0),
                (xn, (tm, d), a_map, wu, (None, fb, d), w_map, "nt", 1, 0)], [], [o, o, o], epi, nacc=2, comm=comm)


def _ffn_down(name, a, wd, resid, tm=1024, tn=512, comm=()):
    g, s, fb = a.shape
    d = wd.shape[2]
    tm, tn = _tile(s, tm), _tile(d, tn)

    def epi(accs, ex, out):
        out[0][...] = ex[0][...] + 0.5 * accs[0]

    return _mm(name, (d // tn, s // tm),
               [(a, (g, tm, fb), lambda j, i: (0, i, 0), wd, (g, fb, tn), lambda j, i: (0, 0, j), "nn", 0, g)],
               [(resid, (tm, tn), lambda j, i: (i, j))],
               [((s, d), F32, (tm, tn), lambda j, i: (i, j))], epi, comm=comm)[0]


def _ffn_bwd_act(name, dh, wd, hg, hu, tm=1024, comm=()):
    s, d = dh.shape
    g, fb, _ = wd.shape
    tm = _tile(s, tm)

    def epi(accs, ex, out):
        da = 0.5 * accs[0]
        hg_v, hu_v = ex[0][...].astype(F32), ex[1][...].astype(F32)
        sg = _sig(hg_v)
        out[0][...] = (da * hu_v * (sg * (1.0 + hg_v * (1.0 - sg)))).astype(BF)
        out[1][...] = (da * (hg_v * sg)).astype(BF)

    blk = (None, tm, fb)
    gmap = lambda j, i: (j, i, 0)
    return _mm(name, (g, s // tm),
               [(dh, (tm, d), lambda j, i: (i, 0), wd, (None, fb, d), lambda j, i: (j, 0, 0), "nt", 0, 0)],
               [(hg, blk, gmap), (hu, blk, gmap)],
               [((g, s, fb), BF, blk, gmap), ((g, s, fb), BF, blk, gmap)], epi, comm=comm)


def _ffn_bwd_wd(name, a, dh, tn=1024, comm=()):
    g, s, fb = a.shape
    d = dh.shape[1]
    tn = _tile(d, tn)

    def epi(accs, ex, out):
        out[0][...] = (0.5 * accs[0]).astype(BF)

    return _mm(name, (g, d // tn),
               [(a, (None, s, fb), lambda j, i: (j, 0, 0), dh, (s, tn), lambda j, i: (0, i), "tn", 0, 0)], [],
               [((g, fb, d), BF, (None, fb, tn), lambda j, i: (j, 0, i))], epi, comm=comm)[0]


def _ffn_bwd_wup(name, xn, dhg, dhu, tk=1024, comm=()):
    s, d = xn.shape
    g, _, fb = dhg.shape
    tk = _tile(d, tk)

    def epi(accs, ex, out):
        out[0][...] = accs[0].astype(BF)
        out[1][...] = accs[1].astype(BF)

    a_map = lambda j, i: (j, 0, 0)
    b_map = lambda j, i: (0, i)
    o = ((g, fb, d), BF, (None, fb, tk), lambda j, i: (j, 0, i))
    return _mm(name, (g, d // tk),
               [(dhg, (None, s, fb), a_map, xn, (s, tk), b_map, "tn", 0, 0),
                (dhu, (None, s, fb), a_map, xn, (s, tk), b_map, "tn", 1, 0)], [], [o, o], epi, nacc=2, comm=comm)


def _ffn_bwd_x(name, dhg, dhu, wg, wu, tm=512, tn=512, comm=()):
    g, s, fb = dhg.shape
    d = wg.shape[2]
    tm, tn = _tile(s, tm), _tile(d, tn)
    a_blk, a_map = (g, tm, fb), lambda j, i: (0, i, 0)
    b_blk, b_map = (g, fb, tn), lambda j, i: (0, 0, j)
    return _mm(name, (d // tn, s // tm),
               [(dhg, a_blk, a_map, wg, b_blk, b_map, "nn", 0, g), (dhu, a_blk, a_map, wu, b_blk, b_map, "nn", 0, g)],
               [], [((s, d), F32, (tm, tn), lambda j, i: (i, j))], _store, comm=comm)[0]


def _ffn_forward(tag, h, gain, get_wgu, get_wd, norm_comm=(), up_comm=(), down_comm=()):
    xn = _rms_fwd(tag + "_norm", h, gain, comm=norm_comm)
    wg, wu = get_wgu()
    hg, hu, a = _ffn_up(tag + "_up", xn, wg, wu, comm=up_comm)
    return _ffn_down(tag + "_down", a, get_wd(), h, comm=down_comm), (xn, hg, hu, a)


def _na_geometry(rows):
    kh = min(NA_ROWS_WIN, rows)
    cols = np.arange(GRID_W)
    col_start = np.clip(cols - NA_COLS_WIN // 2, 0, GRID_W - NA_COLS_WIN)
    mask = (cols[None, :] >= col_start[:, None]) & (cols[None, :] < col_start[:, None] + NA_COLS_WIN)
    dc = np.clip(cols[None, :] - cols[:, None], -(NA_COLS_WIN - 1), NA_COLS_WIN - 1) + (NA_COLS_WIN - 1)
    return kh, mask, dc


def _na_table(rpb, rows):
    _, mask, dc = _na_geometry(rows)
    nd, nc, cells = 2 * NA_ROWS_WIN - 1, 2 * NA_COLS_WIN - 1, GRID_W * GRID_W
    onehot = np.zeros((128, cells), np.float32)
    onehot[dc.reshape(-1), np.arange(cells)] = mask.reshape(-1).astype(np.float32)
    off = np.where(mask.reshape(1, -1), 0.0, NEG).astype(np.float32)

    def body(r_ref, e_ref, off_ref, o_ref):
        o_ref[...] = jnp.dot(r_ref[...], e_ref[...], precision=HI, preferred_element_type=F32) + off_ref[...]

    flat = pl.pallas_call(body, name="na_table", out_shape=jax.ShapeDtypeStruct((NA_HEADS * nd, cells), F32),
                          compiler_params=_params(0))(
        jnp.pad(rpb.reshape(NA_HEADS * nd, nc), ((0, 0), (0, 128 - nc))), jnp.asarray(onehot), jnp.asarray(off))
    return flat.reshape(NA_HEADS, nd, GRID_W, GRID_W)


class _NaPlan:
    def __init__(self, s):
        self.s, self.rows = s, s // GRID_W
        self.kh = min(NA_ROWS_WIN, self.rows)
        self.qr = min(NA_QROWS, self.rows)
        self.kr = min(self.rows, self.kh + self.qr - 1)
        self.groups = self.rows // self.qr
        self.nd = 2 * NA_ROWS_WIN - 1
        self.hw, self.nq = NA_HG * NA_DIM, NA_HEADS // NA_HG
        clip = lambda v, hi: min(max(v, 0), hi)
        pats = [(clip(g * self.qr - self.kh // 2, self.rows - self.kr) - g * self.qr,)
                + tuple(clip(g * self.qr + a - self.kh // 2, self.rows - self.kh) - g * self.qr for a in range(self.qr))
                for g in range(self.groups)]
        self.rebuild = [g for g in range(self.groups) if g == 0 or pats[g] != pats[g - 1]]

    def first_key_row(self, g):
        return jnp.clip(g * self.qr - self.kh // 2, 0, self.rows - self.kr)

    def specs(self):
        blk = pl.BlockSpec((self.qr * GRID_W, self.hw), lambda j, g: (g, j))
        k_spec = pl.BlockSpec((self.s, self.hw), lambda j, g: (0, self.nq + j))
        v_spec = pl.BlockSpec((self.s, self.hw), lambda j, g: (0, 2 * self.nq + j))
        t_spec = pl.BlockSpec((NA_HG, self.nd, GRID_W, GRID_W), lambda j, g: (j, 0, 0, 0))
        return blk, k_spec, v_spec, t_spec

    def bias_scratch(self):
        return pltpu.VMEM((NA_HG, self.qr * GRID_W, self.kr * GRID_W), F32)

    def fill_bias(self, t_ref, bias_ref, g):
        def build():
            r0, ks = g * self.qr, self.first_key_row(g)
            for a in range(self.qr):
                rs = jnp.clip(r0 + a - self.kh // 2, 0, self.rows - self.kh)
                for i in range(self.kr):
                    valid = jnp.logical_and(ks + i >= rs, ks + i < rs + self.kh)
                    idx = jnp.clip(ks + i - r0 - a + NA_ROWS_WIN - 1, 0, self.nd - 1)
                    for h in range(NA_HG):
                        bias_ref[h, a * GRID_W:(a + 1) * GRID_W, i * GRID_W:(i + 1) * GRID_W] = jnp.where(
                            valid, t_ref[h, idx], NEG)

        pl.when(functools.reduce(jnp.logical_or, [g == r for r in self.rebuild]))(build)

    def window(self, g):
        return pl.ds(pl.multiple_of(self.first_key_row(g) * GRID_W, GRID_W), self.kr * GRID_W)


def _na_probs(q, k, bias):
    sc = lax.dot_general(q, k, _DN["nt"], preferred_element_type=F32) * (NA_DIM ** -0.5) + bias
    e = jnp.exp(sc - jnp.max(sc, axis=-1, keepdims=True))
    return e / jnp.sum(e, axis=-1, keepdims=True)


def _na_fwd(qkv, table, comm=()):
    plan = _NaPlan(qkv.shape[0])
    blk, k_spec, v_spec, t_spec = plan.specs()

    def body(q_ref, k_ref, v_ref, t_ref, o_ref, bias_ref):
        g = pl.program_id(1)
        plan.fill_bias(t_ref, bias_ref, g)
        win = plan.window(g)
        for h in range(NA_HG):
            cs = slice(h * NA_DIM, (h + 1) * NA_DIM)
            p = _na_probs(q_ref[:, cs], k_ref[win, cs], bias_ref[h])
            o_ref[:, cs] = jnp.dot(p.astype(BF), v_ref[win, cs], preferred_element_type=F32).astype(BF)

    return _call("na_fwd", body, (plan.nq, plan.groups), [blk, k_spec, v_spec, t_spec], blk,
                 jax.ShapeDtypeStruct((plan.s, NA_HEADS * NA_DIM), BF), [qkv, qkv, qkv, table], comm,
                 scratch=[plan.bias_scratch()])


def _na_bwd(qkv, table, do, comm=()):
    plan = _NaPlan(qkv.shape[0])
    blk, k_spec, v_spec, t_spec = plan.specs()
    qr, kr = plan.qr, plan.kr

    def body(q_ref, k_ref, v_ref, t_ref, do_ref, dq_ref, dk_ref, dv_ref, dt_ref, bias_ref):
        g = pl.program_id(1)

        @pl.when(g == 0)
        def _():
            dk_ref[...] = jnp.zeros_like(dk_ref)
            dv_ref[...] = jnp.zeros_like(dv_ref)
            dt_ref[...] = jnp.zeros_like(dt_ref)

        plan.fill_bias(t_ref, bias_ref, g)
        win = plan.window(g)
        base = plan.first_key_row(g) - g * qr + NA_ROWS_WIN - 1
        for h in range(NA_HG):
            cs = slice(h * NA_DIM, (h + 1) * NA_DIM)
            q, k, v, do_h = q_ref[:, cs], k_ref[win, cs], v_ref[win, cs], do_ref[:, cs]
            p = _na_probs(q, k, bias_ref[h])
            dp = lax.dot_general(do_h, v, _DN["nt"], preferred_element_type=F32)
            ds = p * (dp - jnp.sum(p * dp, axis=-1, keepdims=True))
            for dlt in range(1 - qr, kr):
                tiles = [ds[a * GRID_W:(a + 1) * GRID_W, (a + dlt) * GRID_W:(a + dlt + 1) * GRID_W]
                         for a in range(qr) if 0 <= a + dlt < kr]
                dt_ref[h, jnp.clip(base + dlt, 0, plan.nd - 1)] += functools.reduce(jnp.add, tiles)
            dsb = (ds * (NA_DIM ** -0.5)).astype(BF)
            dq_ref[:, cs] = jnp.dot(dsb, k, preferred_element_type=F32).astype(BF)
            dk_ref[win, cs] += lax.dot_general(dsb, q, _DN["tn"], preferred_element_type=F32)
            dv_ref[win, cs] += lax.dot_general(p.astype(BF), do_h, _DN["tn"], preferred_element_type=F32)

    width = NA_HEADS * NA_DIM
    whole = pl.BlockSpec((plan.s, plan.hw), lambda j, g: (0, j))
    return _call(
        "na_bwd", body, (plan.nq, plan.groups), [blk, k_spec, v_spec, t_spec, blk], [blk, whole, whole, t_spec],
        [jax.ShapeDtypeStruct((plan.s, width), BF), jax.ShapeDtypeStruct((plan.s, width), F32),
         jax.ShapeDtypeStruct((plan.s, width), F32),
         jax.ShapeDtypeStruct((NA_HEADS, plan.nd, GRID_W, GRID_W), F32)],
        [qkv, qkv, qkv, table, do], comm, scratch=[plan.bias_scratch()])


def _na_rpb_grad(dt, rows):
    _, mask, dc = _na_geometry(rows)
    nd, nc = 2 * NA_ROWS_WIN - 1, 2 * NA_COLS_WIN - 1
    onehot = np.zeros((GRID_W * GRID_W, 128), np.float32)
    onehot[np.arange(GRID_W * GRID_W), dc.reshape(-1)] = mask.reshape(-1).astype(np.float32)
    flat = dt.reshape(NA_HEADS * nd, GRID_W * GRID_W)

    def body(a_ref, e_ref, o_ref):
        o_ref[...] = jnp.dot(a_ref[...], e_ref[...], precision=HI, preferred_element_type=F32)

    out = pl.pallas_call(body, name="na_rpb_grad", out_shape=jax.ShapeDtypeStruct((NA_HEADS * nd, 128), F32),
                         compiler_params=_params(0))(flat, jnp.asarray(onehot))
    return out[:, :nc].reshape(NA_HEADS, nd, nc)


def _rope_consts(s):
    pos = np.arange(s, dtype=np.float32)
    inv = (1.0 / (ROPE_THETA ** (np.arange(0, ML_ROPE, 2, dtype=np.float32) / ML_ROPE))).astype(np.float32)
    ang = pos[:, None] * inv[None, :]
    cos, sin = np.cos(ang).astype(np.float32), np.sin(ang).astype(np.float32)
    half = ML_ROPE // 2
    rot = np.zeros((ML_ROPE, ML_ROPE), np.float32)
    rot[np.arange(half) + half, np.arange(half)] = -1.0
    rot[np.arange(half), np.arange(half) + half] = 1.0
    return (jnp.asarray(np.concatenate([cos, cos], 1)), jnp.asarray(np.concatenate([sin, sin], 1)),
            jnp.asarray(rot), jnp.asarray(rot.T.copy()))


def _rope(v, cos, sin, rot):
    return v * cos + jnp.dot(v, rot, precision=HI, preferred_element_type=F32) * sin


def _unrope(dv, cos, sin, rot_t):
    return dv * cos + jnp.dot(dv * sin, rot_t, precision=HI, preferred_element_type=F32)


def _rms(v, g):
    return v * lax.rsqrt(jnp.mean(v * v, axis=-1, keepdims=True) + EPS) * g


def _mla_prep(lat, gq, gkv, cos, sin, rot, tm=256):
    s, w = lat.shape
    tm = _tile(s, tm)

    def body(l_ref, gq_ref, gkv_ref, c_ref, s_ref, r_ref, cq_ref, ckv_ref, kr_ref):
        cq_ref[...] = _rms(l_ref[:, :ML_RANK], gq_ref[...]).astype(BF)
        ckv_ref[...] = _rms(l_ref[:, ML_RANK:2 * ML_RANK], gkv_ref[...]).astype(BF)
        kr_ref[...] = _rope(l_ref[:, 2 * ML_RANK:], c_ref[...], s_ref[...], r_ref[...]).astype(BF)

    row = lambda c: pl.BlockSpec((tm, c), lambda i: (i, 0))
    full = lambda a: pl.BlockSpec(a.shape, lambda i: (0, 0))
    return pl.pallas_call(
        body, name="mla_prep", grid=(s // tm,),
        in_specs=[row(w), full(gq), full(gkv), row(ML_ROPE), row(ML_ROPE), full(rot)],
        out_specs=[row(ML_RANK), row(ML_RANK), row(ML_ROPE)],
        out_shape=[jax.ShapeDtypeStruct((s, ML_RANK), BF), jax.ShapeDtypeStruct((s, ML_RANK), BF),
                   jax.ShapeDtypeStruct((s, ML_ROPE), BF)],
        compiler_params=_params(1))(lat, gq, gkv, cos, sin, rot)


def _mla_q_proj(cq, wuq, cos, sin, rot, tm=512, comm=()):
    s, k = cq.shape
    tm = _tile(s, tm)

    def epi(accs, ex, out):
        acc = accs[0]
        out[0][:, :ML_NOPE] = acc[:, :ML_NOPE].astype(BF)
        out[0][:, ML_NOPE:] = _rope(acc[:, ML_NOPE:], ex[0][...], ex[1][...], ex[2][...]).astype(BF)

    rmap = lambda j, i: (i, 0)
    return _mm("mla_q_proj", (ML_HEADS, s // tm),
               [(cq, (tm, k), rmap, wuq, (None, ML_QK, k), lambda j, i: (j, 0, 0), "nt", 0, 0)],
               [(cos, (tm, ML_ROPE), rmap), (sin, (tm, ML_ROPE), rmap), (rot, rot.shape, lambda j, i: (0, 0))],
               [((ML_HEADS, s, ML_QK), BF, (None, tm, ML_QK), lambda j, i: (j, i, 0))], epi, comm=comm)[0]


def _mla_kv_proj(ckv, wukv, kr, tm=512, comm=()):
    s, k = ckv.shape
    tm = _tile(s, tm)

    def epi(accs, ex, out):
        acc = accs[0]
        out[0][:, :ML_NOPE] = acc[:, :ML_NOPE].astype(BF)
        out[0][:, ML_NOPE:] = ex[0][...]
        out[1][...] = acc[:, ML_NOPE:].astype(BF)

    rmap = lambda j, i: (i, 0)
    gmap = lambda j, i: (j, i, 0)
    return _mm("mla_kv_proj", (ML_HEADS, s // tm),
               [(ckv, (tm, k), rmap, wukv, (None, k, ML_NOPE + ML_V), lambda j, i: (j, 0, 0), "nn", 0, 0)],
               [(kr, (tm, ML_ROPE), rmap)],
               [((ML_HEADS, s, ML_QK), BF, (None, tm, ML_QK), gmap), ((ML_HEADS, s, ML_V), BF, (None, tm, ML_V), gmap)],
               epi, comm=comm)


def _mla_probs(q, k):
    sc = lax.dot_general(q, k, _DN["nt"], preferred_element_type=F32) * (ML_QK ** -0.5)
    e = jnp.exp(sc - jnp.max(sc, axis=-1, keepdims=True))
    return e / jnp.sum(e, axis=-1, keepdims=True)


def _mla_fwd(q, k, v, tq=1024, comm=()):
    _, s, _ = q.shape
    tq = _tile(s, tq)

    def body(q_ref, k_ref, v_ref, o_ref):
        p = _mla_probs(q_ref[...], k_ref[...])
        o_ref[...] = jnp.dot(p.astype(BF), v_ref[...], preferred_element_type=F32).astype(BF)

    return _call("mla_fwd", body, (ML_HEADS, s // tq),
                 [pl.BlockSpec((None, tq, ML_QK), lambda h, i: (h, i, 0)),
                  pl.BlockSpec((None, s, ML_QK), lambda h, i: (h, 0, 0)),
                  pl.BlockSpec((None, s, ML_V), lambda h, i: (h, 0, 0))],
                 pl.BlockSpec((tq, ML_V), lambda h, i: (i, h)),
                 jax.ShapeDtypeStruct((s, ML_HEADS * ML_V), BF), [q, k, v], comm)


def _mla_bwd(q, k, v, do, tq=1024, comm=()):
    _, s, _ = q.shape
    tq = _tile(s, tq)

    def body(q_ref, k_ref, v_ref, do_ref, dq_ref, dk_ref, dv_ref):
        i = pl.program_id(1)
        qv, kv, vv, dov = q_ref[...], k_ref[...], v_ref[...], do_ref[...]
        p = _mla_probs(qv, kv)
        dp = lax.dot_general(dov, vv, _DN["nt"], preferred_element_type=F32)
        ds = (p * (dp - jnp.sum(p * dp, axis=-1, keepdims=True)) * (ML_QK ** -0.5)).astype(BF)
        dq_ref[...] = jnp.dot(ds, kv, preferred_element_type=F32)
        _acc_rows(dk_ref, lax.dot_general(ds, qv, _DN["tn"], preferred_element_type=F32), i)
        _acc_rows(dv_ref, lax.dot_general(p.astype(BF), dov, _DN["tn"], preferred_element_type=F32), i)

    return _call(
        "mla_bwd", body, (ML_HEADS, s // tq),
        [pl.BlockSpec((None, tq, ML_QK), lambda h, i: (h, i, 0)),
         pl.BlockSpec((None, s, ML_QK), lambda h, i: (h, 0, 0)),
         pl.BlockSpec((None, s, ML_V), lambda h, i: (h, 0, 0)),
         pl.BlockSpec((tq, ML_V), lambda h, i: (i, h))],
        [pl.BlockSpec((None, tq, ML_QK), lambda h, i: (h, i, 0)),
         pl.BlockSpec((None, s, ML_QK), lambda h, i: (h, 0, 0)),
         pl.BlockSpec((None, s, ML_V), lambda h, i: (h, 0, 0))],
        [jax.ShapeDtypeStruct((ML_HEADS, s, ML_QK), F32), jax.ShapeDtypeStruct((ML_HEADS, s, ML_QK), F32),
         jax.ShapeDtypeStruct((ML_HEADS, s, ML_V), F32)],
        [q, k, v, do], comm)


def _mla_post(dq, dk, dv, cos, sin, rot_t, tm=1024):
    _, s, _ = dq.shape
    tm = _tile(s, tm)

    def body(dq_ref, dk_ref, dv_ref, c_ref, s_ref, r_ref, dqp_ref, dkv_ref, dkr_ref):
        h = pl.program_id(1)
        dqv, dkk = dq_ref[...], dk_ref[...]
        dqp_ref[:, :ML_NOPE] = dqv[:, :ML_NOPE].astype(BF)
        dqp_ref[:, ML_NOPE:] = _unrope(dqv[:, ML_NOPE:], c_ref[...], s_ref[...], r_ref[...]).astype(BF)
        dkv_ref[:, :ML_NOPE] = dkk[:, :ML_NOPE].astype(BF)
        dkv_ref[:, ML_NOPE:] = dv_ref[...].astype(BF)
        _acc_rows(dkr_ref, dkk[:, ML_NOPE:], h)

    gspec = lambda c: pl.BlockSpec((None, tm, c), lambda i, h: (h, i, 0))
    rspec = pl.BlockSpec((tm, ML_ROPE), lambda i, h: (i, 0))
    return pl.pallas_call(
        body, name="mla_post", grid=(s // tm, ML_HEADS),
        in_specs=[gspec(ML_QK), gspec(ML_QK), gspec(ML_V), rspec, rspec,
                  pl.BlockSpec(rot_t.shape, lambda i, h: (0, 0))],
        out_specs=[gspec(ML_QK), gspec(ML_NOPE + ML_V), rspec],
        out_shape=[jax.ShapeDtypeStruct((ML_HEADS, s, ML_QK), BF),
                   jax.ShapeDtypeStruct((ML_HEADS, s, ML_NOPE + ML_V), BF),
                   jax.ShapeDtypeStruct((s, ML_ROPE), F32)],
        compiler_params=_params(2))(dq, dk, dv, cos, sin, rot_t)


def _mla_lat_bwd(dcq, dckv, dkr, lat, gq, gkv, cos, sin, rot_t, tm=256):
    s, w = lat.shape
    tm = _tile(s, tm)

    def body(dcq_ref, dckv_ref, dkr_ref, l_ref, gq_ref, gkv_ref, c_ref, s_ref, r_ref, dl_ref, dgq_ref, dgkv_ref):
        i = pl.program_id(0)
        dql, pq = _rms_bwd_math(dcq_ref[...], l_ref[:, :ML_RANK], gq_ref[...])
        dkl, pkv = _rms_bwd_math(dckv_ref[...], l_ref[:, ML_RANK:2 * ML_RANK], gkv_ref[...])
        dl_ref[:, :ML_RANK] = dql.astype(BF)
        dl_ref[:, ML_RANK:2 * ML_RANK] = dkl.astype(BF)
        dl_ref[:, 2 * ML_RANK:] = _unrope(dkr_ref[...], c_ref[...], s_ref[...], r_ref[...]).astype(BF)
        _acc_rows(dgq_ref, pq, i)
        _acc_rows(dgkv_ref, pkv, i)

    row = lambda c: pl.BlockSpec((tm, c), lambda i: (i, 0))
    full = lambda a: pl.BlockSpec(a.shape, lambda i: (0, 0))
    return pl.pallas_call(
        body, name="mla_lat_bwd", grid=(s // tm,),
        in_specs=[row(ML_RANK), row(ML_RANK), row(ML_ROPE), row(w), full(gq), full(gkv), row(ML_ROPE), row(ML_ROPE),
                  full(rot_t)],
        out_specs=[row(w), full(gq), full(gkv)],
        out_shape=[jax.ShapeDtypeStruct((s, w), BF), jax.ShapeDtypeStruct(gq.shape, F32),
                   jax.ShapeDtypeStruct(gkv.shape, F32)],
        compiler_params=_params(1))(dcq, dckv, dkr, lat, gq, gkv, cos, sin, rot_t)


def _grp_dw(name, a, dout, ta=1024):
    s, k = a.shape
    ta = _tile(k, ta)
    if dout.ndim == 3:
        g, _, nb = dout.shape
        b_blk, b_map = (None, s, nb), lambda j, i: (j, 0, 0)
    else:
        g, nb = NDEV, dout.shape[1] // NDEV
        b_blk, b_map = (s, nb), lambda j, i: (0, j)
    return _mm(name, (g, k // ta),
               [(a, (s, ta), lambda j, i: (0, i), dout, b_blk, b_map, "tn", 0, 0)], [],
               [((g, k, nb), BF, (None, ta, nb), lambda j, i: (j, i, 0))], _store)[0]


def _grp_dw_t(name, dout, a, ta=512):
    g, s, nb = dout.shape
    k = a.shape[1]
    ta = _tile(k, ta)
    return _mm(name, (g, k // ta),
               [(dout, (None, s, nb), lambda j, i: (j, 0, 0), a, (s, ta), lambda j, i: (0, i), "tn", 0, 0)], [],
               [((g, nb, k), BF, (None, nb, ta), lambda j, i: (j, 0, i))], _store)[0]


def _grp_dx_t(name, dout, wt, tm=512, tn=512, comm=()):
    g, s, nb = dout.shape
    k = wt.shape[2]
    tm, tn = _tile(s, tm), _tile(k, tn)
    return _mm(name, (k // tn, s // tm),
               [(dout, (g, tm, nb), lambda j, i: (0, i, 0), wt, (g, nb, tn), lambda j, i: (0, 0, j), "nn", 0, g)], [],
               [((s, k), F32, (tm, tn), lambda j, i: (i, j))], _store, comm=comm)[0]


def _grp_dx(name, dout, w, tm=512, tn=512, out_dtype=F32, comm=()):
    g, s, nb = dout.shape
    k = w.shape[1]
    tm, tn = _tile(s, tm), _tile(k, tn)
    return _mm(name, (k // tn, s // tm),
               [(dout, (g, tm, nb), lambda j, i: (0, i, 0), w, (g, tn, nb), lambda j, i: (0, j, 0), "nt", 0, g)], [],
               [((s, k), out_dtype, (tm, tn), lambda j, i: (i, j))], _store, comm=comm)[0]


def _row_dw(name, a, dout, tn=2048):
    s, n = dout.shape
    tn = _tile(n, tn)
    if a.ndim == 3:
        kb = a.shape[2]
        a_blk, a_map = (None, s, kb), lambda j, i: (j, 0, 0)
    else:
        kb = a.shape[1] // NDEV
        a_blk, a_map = (s, kb), lambda j, i: (0, j)
    return _mm(name, (NDEV, n // tn),
               [(a, a_blk, a_map, dout, (s, tn), lambda j, i: (0, i), "tn", 0, 0)], [],
               [((NDEV, kb, n), BF, (None, kb, tn), lambda j, i: (j, 0, i))], _store)[0]


def _mix_merge(oa, ob, wa, wb, ga, gb, tm=1024, comm=()):
    s, k = oa.shape
    g, _, nb = wa.shape
    tm = _tile(s, tm)

    def epi(accs, ex, out):
        ya, yb = accs
        out[0][...] = ya.astype(BF)
        out[1][...] = yb.astype(BF)
        out[2][...] = (_sig(ex[0][...]) * ya + _sig(ex[1][...]) * yb).astype(BF)

    rmap = lambda j, i: (i, 0)
    wmap = lambda j, i: (j, 0, 0)
    o = ((g, s, nb), BF, (None, tm, nb), lambda j, i: (j, i, 0))
    cmap = lambda j, i: (i, j)
    return _mm("mix_merge", (g, s // tm),
               [(oa, (tm, k), rmap, wa, (None, k, nb), wmap, "nn", 0, 0),
                (ob, (tm, k), rmap, wb, (None, k, nb), wmap, "nn", 1, 0)],
               [(ga, (tm, nb), cmap), (gb, (tm, nb), cmap)], [o, o, o], epi, nacc=2, comm=comm)


def _mix_out(merged, wout, resid, tm=1024, tn=512):
    g, s, kb = merged.shape
    d = wout.shape[2]
    tm, tn = _tile(s, tm), _tile(d, tn)

    def epi(accs, ex, out):
        out[0][...] = ex[0][...] + accs[0]

    return _mm("mix_out", (d // tn, s // tm),
               [(merged, (g, tm, kb), lambda j, i: (0, i, 0), wout, (g, kb, tn), lambda j, i: (0, 0, j), "nn", 0, g)],
               [(resid, (tm, tn), lambda j, i: (i, j))],
               [((s, d), F32, (tm, tn), lambda j, i: (i, j))], epi)[0]


def _mix_out_bwd(dh, wout, ga, gb, ya, yb, tm=1024, comm=()):
    s, d = dh.shape
    g, kb, _ = wout.shape
    tm = _tile(s, tm)

    def epi(accs, ex, out):
        dm = accs[0]
        sa, sb = _sig(ex[0][...]), _sig(ex[1][...])
        out[0][...] = (dm * sa).astype(BF)
        out[1][...] = (dm * sb).astype(BF)
        out[2][...] = (dm * ex[2][...].astype(F32) * sa * (1.0 - sa)).astype(BF)
        out[3][...] = (dm * ex[3][...].astype(F32) * sb * (1.0 - sb)).astype(BF)

    cmap = lambda j, i: (i, j)
    gmap = lambda j, i: (j, i, 0)
    og = ((g, s, kb), BF, (None, tm, kb), gmap)
    oc = ((s, g * kb), BF, (tm, kb), cmap)
    return _mm("mix_out_bwd", (g, s // tm),
               [(dh, (tm, d), lambda j, i: (i, 0), wout, (None, kb, d), lambda j, i: (j, 0, 0), "nt", 0, 0)],
               [(ga, (tm, kb), cmap), (gb, (tm, kb), cmap), (ya, (None, tm, kb), gmap), (yb, (None, tm, kb), gmap)],
               [og, og, oc, oc], epi, comm=comm)


def _pl_forward(n4, wplg, p, wpl, h3, tm=1024):
    s, d = n4.shape
    g, kb, _ = wplg.shape
    kp, nb = wpl.shape[1], wpl.shape[2]
    tm = _tile(s, tm)
    wplg_nat = wplg.reshape(g * kb, d)

    def epi(accs, ex, out):
        t, pe = accs
        out[0][...] = ex[0][...] + _sig(t) * pe
        out[1][...] = t
        out[2][...] = pe.astype(BF)

    rmap = lambda j, i: (i, 0)
    cmap = lambda j, i: (i, j)
    return _mm("pl_forward", (g, s // tm),
               [(n4, (tm, d), rmap, wplg_nat, (g * kb, nb), lambda j, i: (0, j), "nn", 0, 0),
                (p, (tm, kp), rmap, wpl, (None, kp, nb), lambda j, i: (j, 0, 0), "nn", 1, 0)],
               [(h3, (tm, nb), cmap)],
               [((s, d), F32, (tm, nb), cmap), ((s, d), F32, (tm, nb), cmap), ((s, d), BF, (tm, nb), cmap)],
               epi, nacc=2)


def _row_dx(name, dout, w, tm=1024, comm=()):
    s, n = dout.shape
    g, kb, _ = w.shape
    tm = _tile(s, tm)
    return _mm(name, (g, s // tm),
               [(dout, (tm, n), lambda j, i: (i, 0), w, (None, kb, n), lambda j, i: (j, 0, 0), "nt", 0, 0)], [],
               [((s, g * kb), F32, (tm, kb), lambda j, i: (i, j))], _store, comm=comm)[0]


def _in_proj_bwd_x(pieces, weights, tm=512, tn=512, comm=()):
    s = pieces[0].shape[0]
    d = weights[0].shape[1]
    tm, tn = _tile(s, tm), _tile(d, tn)
    prods = [(pc, (tm, pc.shape[1]), lambda j, i: (i, 0), w, (pc.shape[1], tn), lambda j, i: (0, j), "nn", 0, 0)
             for pc, w in zip(pieces, weights)]
    return _mm("in_proj_dx", (d // tn, s // tm), prods, [],
               [((s, d), F32, (tm, tn), lambda j, i: (i, j))], _store, comm=comm)[0]


def _split_w_in(w_in_t):
    g, nb, d = w_in_t.shape
    nat = w_in_t.reshape(g * nb, d)
    na, lat = 3 * NA_HEADS * NA_DIM, 2 * ML_RANK + ML_ROPE
    return nat, nat[na:na + lat], nat[na + lat:na + lat + d], nat[na + lat + d:]


def _pair_sum(name, part, landed, core):
    _, _, r, c = part.shape
    tr, tc = _ew_tile(r, c)

    def body(core_ref, a_ref, b_ref, o_ref):
        o_ref[...] = (a_ref[...].astype(F32) + b_ref[...].astype(F32)).astype(o_ref.dtype)

    return pl.pallas_call(
        body, name=name,
        grid_spec=pltpu.PrefetchScalarGridSpec(
            num_scalar_prefetch=1, grid=(NCHIP, r // tr, c // tc),
            in_specs=[pl.BlockSpec((None, None, tr, tc), lambda j, i, k, core_ref: (j, core_ref[0], i, k)),
                      pl.BlockSpec((None, tr, tc), lambda j, i, k, core_ref: (j, i, k))],
            out_specs=pl.BlockSpec((None, tr, tc), lambda j, i, k, core_ref: (j, i, k))),
        out_shape=jax.ShapeDtypeStruct(landed.shape, landed.dtype), compiler_params=_params(3),
    )(core, part, landed)


def _device_step(x, p, target, sp, own, core):
    s, d = x.shape
    rows = s // GRID_W
    cos, sin, rot, rot_t = _rope_consts(s)
    w, dw4, sums, dsp, pending = {}, {}, {}, {}, []

    def gather(*names):
        return _GatherPart(names, [own[n] for n in names])

    def got(part):
        w.update(zip(part.names, part.results))

    def grad(name, g):
        dw4[name] = g.reshape((NCHIP, 2) + g.shape[1:])

    def to_sibling(*names):
        return _SiblingPart(names, [dw4[n] for n in names])

    def add_pairs(part):
        for n, landed in zip(part.names, part.results):
            sums[n] = _pair_sum("pair_sum_" + n, dw4[n], landed, core)

    def start_chips(tag, *names):
        send, recv, thru, lands, token = _chips_start("rs_start_" + tag, [sums[n] for n in names])
        pending.append((tag, names, send, recv, thru, lands))
        return token

    c0 = gather("ffn1_w_gate", "ffn1_w_up")
    c1 = gather("ffn1_w_down")
    c2 = gather("w_in")

    def ffn1_wgu():
        got(c0)
        return w["ffn1_w_gate"], w["ffn1_w_up"]

    def ffn1_wd():
        got(c1)
        return w["ffn1_w_down"]

    h1, ffn1_saved = _ffn_forward("ffn1", x, sp["ffn1_norm"], ffn1_wgu, ffn1_wd,
                                  norm_comm=[c0], up_comm=[c1], down_comm=[c2])
    got(c2)
    wqkv, wlat, wga, wgb = _split_w_in(w["w_in"])
    u = _rms_fwd("mix_norm", h1, sp["mix_norm"])
    c3 = gather("w_uq", "w_ukv")
    qkv = _mm_nt("in_qkv", u, wqkv, BF, tn=1024, comm=[c3], rows=3 * NA_HEADS * NA_DIM)
    got(c3)
    lat = _mm_nt("in_lat", u, wlat, F32, tm=1024)
    c3a = gather("w_branch_a")
    ga = _mm_nt("in_ga", u, wga, F32, tn=1024, comm=[c3a])
    got(c3a)
    c3b = gather("w_branch_b")
    gb = _mm_nt("in_gb", u, wgb, F32, tn=1024, comm=[c3b])
    got(c3b)
    tb = _na_table(sp["na_rpb"], rows)
    c4 = gather("ffn2_w_gate")
    oa = _na_fwd(qkv, tb, comm=[c4])
    got(c4)
    cq, ckv, kr = _mla_prep(lat, sp["q_a_norm"], sp["kv_a_norm"], cos, sin, rot)
    c4a = gather("w_out")
    qf = _mla_q_proj(cq, w["w_uq"], cos, sin, rot, comm=[c4a])
    got(c4a)
    kf, vf = _mla_kv_proj(ckv, w["w_ukv"], kr)
    c5 = gather("ffn2_w_up")
    ob = _mla_fwd(qf, kf, vf, comm=[c5])
    got(c5)
    c5a = gather("w_pl", "w_pl_gate")
    ya, yb, merged = _mix_merge(oa, ob, w["w_branch_a"], w["w_branch_b"], ga, gb, comm=[c5a])
    got(c5a)
    h2 = _mix_out(merged, w["w_out"], h1)
    c6 = gather("ffn2_w_down")

    def ffn2_wd():
        got(c6)
        return w["ffn2_w_down"]

    h3, ffn2_saved = _ffn_forward("ffn2", h2, sp["ffn2_norm"], lambda: (w["ffn2_w_gate"], w["ffn2_w_up"]), ffn2_wd,
                                  up_comm=[c6])
    n4 = _rms_fwd("pl_norm", h3, sp["pl_norm"])
    pb = p.astype(BF)
    h4, t, pe = _pl_forward(n4, w["w_pl_gate"], pb, w["w_pl"], h3)

    dh4, dsp["final_norm"], loss = _loss_head(h4, target, sp["final_norm"])
    dt, dpe = _pl_bwd_elem(dh4, pe, t)
    grad("w_pl", _grp_dw("pl_dw", pb, dpe))
    grad("w_pl_gate", _row_dw("plg_dw", n4, dt))
    s1 = to_sibling("w_pl", "w_pl_gate")
    dn4 = _row_dx("plg_dx", dt, w["w_pl_gate"], comm=[s1])
    add_pairs(s1)
    dh3, dhb, dsp["pl_norm"] = _rms_bwd("pl_dnorm", dn4, h3, sp["pl_norm"], dh4)

    xn, hg, hu, a = ffn2_saved
    grad("ffn2_w_down", _ffn_bwd_wd("ffn2_dwd", a, dhb))
    s2 = to_sibling("ffn2_w_down")
    dhg, dhu = _ffn_bwd_act("ffn2_dact", dhb, w["ffn2_w_down"], hg, hu, comm=[s2])
    add_pairs(s2)
    tok = start_chips("ffn2_down", "w_pl", "w_pl_gate", "ffn2_w_down")
    dwg, dwu = _ffn_bwd_wup("ffn2_dwup", xn, dhg, dhu, comm=[_After(tok)])
    grad("ffn2_w_gate", dwg)
    grad("ffn2_w_up", dwu)
    s3 = to_sibling("ffn2_w_gate", "ffn2_w_up")
    dxn = _ffn_bwd_x("ffn2_dx", dhg, dhu, w["ffn2_w_gate"], w["ffn2_w_up"], comm=[s3])
    add_pairs(s3)
    tok = start_chips("ffn2_up", "ffn2_w_gate", "ffn2_w_up")
    dh2, dh2b, dsp["ffn2_norm"] = _rms_bwd("ffn2_dnorm", dxn, h2, sp["ffn2_norm"], dh3, comm=[_After(tok)])

    grad("w_out", _row_dw("out_dw", merged, dh2b))
    s4 = to_sibling("w_out")
    dya, dyb, dga, dgb = _mix_out_bwd(dh2b, w["w_out"], ga, gb, ya, yb, comm=[s4])
    add_pairs(s4)
    grad("w_branch_a", _grp_dw("bra_dw", oa, dya))
    grad("w_branch_b", _grp_dw("brb_dw", ob, dyb))
    doa = _grp_dx("bra_dx", dya, w["w_branch_a"], out_dtype=BF)
    s5 = to_sibling("w_branch_a", "w_branch_b")
    dob = _grp_dx("brb_dx", dyb, w["w_branch_b"], out_dtype=BF, comm=[s5])
    add_pairs(s5)

    dqf, dkf, dvf = _mla_bwd(qf, kf, vf, dob)
    dqp, dkv, dkr = _mla_post(dqf, dkf, dvf, cos, sin, rot_t)
    grad("w_uq", _grp_dw_t("uq_dw", dqp, cq))
    grad("w_ukv", _grp_dw("ukv_dw", ckv, dkv))
    dcq = _grp_dx_t("uq_dx", dqp, w["w_uq"])
    s6 = to_sibling("w_uq", "w_ukv")
    dckv = _grp_dx("ukv_dx", dkv, w["w_ukv"], comm=[s6])
    add_pairs(s6)
    dlat, dsp["q_a_norm"], dsp["kv_a_norm"] = _mla_lat_bwd(dcq, dckv, dkr, lat, sp["q_a_norm"], sp["kv_a_norm"],
                                                         cos, sin, rot_t)
    dq_na, dk_na, dv_na, dtab = _na_bwd(qkv, tb, doa)
    dsp["na_rpb"] = _na_rpb_grad(dtab, rows)
    dqkv = jnp.concatenate([dq_na, dk_na.astype(BF), dv_na.astype(BF)], axis=1)

    pieces = [dqkv, dlat, dga, dgb]
    dwin = jnp.zeros((sum(pc.shape[1] for pc in pieces), d), BF)
    row0 = 0
    for i, pc in enumerate(pieces):
        dwin = _mm_tn_into("in_dw%d" % i, pc, u, dwin, row0)
        row0 += pc.shape[1]
    grad("w_in", dwin.reshape(NDEV, -1, d))
    s7 = to_sibling("w_in")
    du = _in_proj_bwd_x(pieces, [wqkv, wlat, wga, wgb], comm=[s7])
    add_pairs(s7)
    tok = start_chips("w_in", "w_out", "w_branch_a", "w_branch_b", "w_uq", "w_ukv", "w_in")
    dh1, dhb, dsp["mix_norm"] = _rms_bwd("mix_dnorm", du, h1, sp["mix_norm"], dh2, comm=[_After(tok)])

    xn, hg, hu, a = ffn1_saved
    dhg, dhu = _ffn_bwd_act("ffn1_dact", dhb, w["ffn1_w_down"], hg, hu)
    dwg, dwu = _ffn_bwd_wup("ffn1_dwup", xn, dhg, dhu)
    grad("ffn1_w_gate", dwg)
    grad("ffn1_w_up", dwu)
    s8 = to_sibling("ffn1_w_gate", "ffn1_w_up")
    grad("ffn1_w_down", _ffn_bwd_wd("ffn1_dwd", a, dhb, comm=[s8]))
    add_pairs(s8)
    tok = start_chips("ffn1_up", "ffn1_w_gate", "ffn1_w_up")
    s9 = to_sibling("ffn1_w_down")
    _comm_only("rs_sibling_ffn1", [s9, _After(tok)])
    add_pairs(s9)
    tok = start_chips("ffn1_down", "ffn1_w_down")
    dxn = _ffn_bwd_x("ffn1_dx", dhg, dhu, w["ffn1_w_gate"], w["ffn1_w_up"], comm=[_After(tok)])
    dx, _, dsp["ffn1_norm"] = _rms_bwd("ffn1_dnorm", dxn, x, sp["ffn1_norm"], dh1)
    return loss, dx, pending, dsp


def _small_peers():
    x, y, c = _coords()
    return [(x ^ ((k >> 2) & 1), y ^ ((k >> 1) & 1), c ^ (k & 1)) for k in range(1, NDEV)]


def _small_start(buf):
    def body(b_ref, z_ref, send_sems, recv_sems, b_thru, z_thru, token):
        x, y, c = _coords()
        for k, peer in enumerate(_small_peers()):
            pltpu.make_async_remote_copy(
                src_ref=b_ref, dst_ref=z_ref.at[4 * x + 2 * y + c], send_sem=send_sems.at[k],
                recv_sem=recv_sems.at[k], device_id=peer, device_id_type=MESH).start()
        token[...] = jnp.zeros_like(token)

    zone = lax.empty((NDEV,) + buf.shape, buf.dtype)
    res = pl.pallas_call(
        body, name="small_start", in_specs=[HBM, HBM],
        out_specs=(SEM, SEM, HBM, HBM, pl.BlockSpec(memory_space=pltpu.VMEM)),
        out_shape=(pltpu.SemaphoreType.DMA((NDEV - 1,)), pltpu.SemaphoreType.DMA((NDEV - 1,)),
                   pltpu.HBM(buf.shape, buf.dtype), pltpu.HBM(zone.shape, zone.dtype),
                   jax.ShapeDtypeStruct((8, 128), F32)),
        input_output_aliases={0: 2, 1: 3},
        compiler_params=pltpu.CompilerParams(has_side_effects=pltpu.SideEffectType.DATAFLOW_SIDE_EFFECTING),
    )(pltpu.with_memory_space_constraint(buf, pltpu.HBM), pltpu.with_memory_space_constraint(zone, pltpu.HBM))
    return res


def _small_wait(send_sems, recv_sems, buf, zone, after):
    def body(b_ref, z_ref, send, recv, after_ref, b_out, z_out):
        for k, (px, py, pc) in enumerate(_small_peers()):
            cp = pltpu.make_async_remote_copy(
                src_ref=b_ref, dst_ref=z_ref.at[4 * px + 2 * py + pc], send_sem=send.at[k], recv_sem=recv.at[k],
                device_id=(px, py, pc), device_id_type=MESH)
            cp.wait_send()
            cp.wait_recv()

    return pl.pallas_call(
        body, name="small_wait", in_specs=[HBM, HBM, SEM, SEM, ANY], out_specs=[HBM, HBM],
        out_shape=[pltpu.HBM(buf.shape, buf.dtype), pltpu.HBM(zone.shape, zone.dtype)],
        input_output_aliases={0: 0, 1: 1},
        compiler_params=pltpu.CompilerParams(has_side_effects=pltpu.SideEffectType.DATAFLOW_SIDE_EFFECTING),
    )(buf, zone, send_sems, recv_sems, after)


def _adam_math(wv, g, m, v):
    m_new = B1 * m + (1.0 - B1) * g
    v_new = B2 * v + (1.0 - B2) * (g * g)
    m_hat = m_new / (1.0 - B1 ** STEP)
    v_hat = v_new / (1.0 - B2 ** STEP)
    return -LR * (m_hat / (jnp.sqrt(v_hat) + ADAM_EPS) + WD * wv), m_new, v_new


def _adam_replicated(name, zone, own, wv, m, v, me):
    ndev, r, c = zone.shape

    def body(me_ref, z_ref, o_ref, w_ref, m_ref, v_ref, g_ref, d_ref, mo_ref, vo_ref):
        g = jnp.zeros((r, c), F32)
        for d in range(ndev):
            g = g + jnp.where(me_ref[0] == d, o_ref[...], z_ref[d])
        g_ref[...] = g
        d_ref[...], mo_ref[...], vo_ref[...] = _adam_math(w_ref[...], g, m_ref[...], v_ref[...])

    blk = pl.BlockSpec((r, c), lambda i, me_ref: (0, 0))
    return pl.pallas_call(
        body, name=name,
        grid_spec=pltpu.PrefetchScalarGridSpec(
            num_scalar_prefetch=1, grid=(1,),
            in_specs=[pl.BlockSpec((ndev, r, c), lambda i, me_ref: (0, 0, 0)), blk, blk, blk, blk],
            out_specs=[blk] * 4),
        out_shape=[jax.ShapeDtypeStruct((r, c), F32)] * 4, compiler_params=_params(1),
    )(me, zone, own, wv, m, v)


def _adam_exchanged(name, sums, land, wv, m, v, my_chip):
    _, r, c = sums.shape
    tr, tc = _ew_tile(r, c)

    def body(chip_ref, s_ref, l_ref, w_ref, m_ref, v_ref, g_ref, d_ref, mo_ref, vo_ref):
        g = s_ref[...].astype(F32)
        for j in range(3):
            g = g + l_ref[j].astype(F32)
        g_ref[...] = g
        d_ref[...], mo_ref[...], vo_ref[...] = _adam_math(w_ref[...], g, m_ref[...], v_ref[...])

    blk = pl.BlockSpec((tr, tc), lambda i, k, chip_ref: (i, k))
    return pl.pallas_call(
        body, name=name,
        grid_spec=pltpu.PrefetchScalarGridSpec(
            num_scalar_prefetch=1, grid=(r // tr, c // tc),
            in_specs=[pl.BlockSpec((None, tr, tc), lambda i, k, chip_ref: (chip_ref[0], i, k)),
                      pl.BlockSpec((3, tr, tc), lambda i, k, chip_ref: (0, i, k)), blk, blk, blk],
            out_specs=[blk] * 4),
        out_shape=[jax.ShapeDtypeStruct((r, c), F32)] * 4, compiler_params=_params(2),
    )(my_chip, sums, land, wv, m, v)


SHARDED = ("ffn1_w_gate", "ffn1_w_up", "ffn1_w_down", "w_in", "w_uq", "w_ukv", "w_branch_a", "w_branch_b", "w_out",
           "ffn2_w_gate", "ffn2_w_up", "ffn2_w_down", "w_pl", "w_pl_gate")
TRANSPOSED = ("ffn1_w_gate", "ffn1_w_up", "ffn2_w_gate", "ffn2_w_up", "w_in", "w_uq")
REPLICATED = ("ffn1_norm", "mix_norm", "q_a_norm", "kv_a_norm", "na_rpb", "ffn2_norm", "pl_norm", "final_norm")
WEIGHTS = ("ffn1_norm", "ffn1_w_gate", "ffn1_w_up", "ffn1_w_down", "mix_norm", "w_in", "q_a_norm", "w_uq",
           "kv_a_norm", "w_ukv", "na_rpb", "w_branch_a", "w_branch_b", "w_out", "ffn2_norm", "ffn2_w_gate",
           "ffn2_w_up", "ffn2_w_down", "pl_norm", "w_pl", "w_pl_gate", "final_norm")
SMALL_W = 2048


def _pack_small(vals):
    rows = []
    for name in REPLICATED:
        flat = vals[name].reshape(-1).astype(F32)
        n = -(-flat.shape[0] // SMALL_W) * SMALL_W
        rows.append(jnp.pad(flat, (0, n - flat.shape[0])).reshape(-1, SMALL_W))
    return jnp.concatenate(rows, axis=0)


def _unpack_small(buf, shapes):
    out, r = {}, 0
    for name in REPLICATED:
        size = int(np.prod(shapes[name]))
        nrow = -(-size // SMALL_W)
        out[name] = buf[r:r + nrow].reshape(-1)[:size].reshape(shapes[name])
        r += nrow
    return out


def kernel(x, p, ffn1_norm, ffn1_w_gate, ffn1_w_up, ffn1_w_down, mix_norm, w_in, q_a_norm, w_uq, kv_a_norm, w_ukv, na_rpb, w_branch_a, w_branch_b, w_out, ffn2_norm, ffn2_w_gate, ffn2_w_up, ffn2_w_down, pl_norm, w_pl, w_pl_gate, final_norm, loss_target, m_ffn1_norm, m_ffn1_w_gate, m_ffn1_w_up, m_ffn1_w_down, m_mix_norm, m_w_in, m_q_a_norm, m_w_uq, m_kv_a_norm, m_w_ukv, m_na_rpb, m_w_branch_a, m_w_branch_b, m_w_out, m_ffn2_norm, m_ffn2_w_gate, m_ffn2_w_up, m_ffn2_w_down, m_pl_norm, m_w_pl, m_w_pl_gate, m_final_norm, v_ffn1_norm, v_ffn1_w_gate, v_ffn1_w_up, v_ffn1_w_down, v_mix_norm, v_w_in, v_q_a_norm, v_w_uq, v_kv_a_norm, v_w_ukv, v_na_rpb, v_w_branch_a, v_w_branch_b, v_w_out, v_ffn2_norm, v_ffn2_w_gate, v_ffn2_w_up, v_ffn2_w_down, v_pl_norm, v_w_pl, v_w_pl_gate, v_final_norm):
    args = dict(locals())
    wts = {n: args[n] for n in WEIGHTS}
    mom = {n: args["m_" + n] for n in WEIGHTS}
    var = {n: args["v_" + n] for n in WEIGHTS}
    shapes = {n: wts[n].shape for n in WEIGHTS}
    core = lax.axis_index("c").astype(jnp.int32).reshape(1)

    local = lambda n, a: a[0].T if n in TRANSPOSED else a[0]
    own = {n: local(n, wts[n]).astype(BF) for n in SHARDED}
    sp = {n: wts[n].reshape(1, -1) for n in REPLICATED if n != "na_rpb"}
    sp["na_rpb"] = wts["na_rpb"][0]
    loss_part, grad_x, pending, dsp = _device_step(x[0], p[0, 0], loss_target[0], sp, own, core)

    small = jnp.concatenate([_pack_small(dsp), jnp.pad(loss_part, ((0, 0), (0, SMALL_W - loss_part.shape[1])))], 0)
    pad_rows = -small.shape[0] % 8
    small = jnp.pad(small, ((0, pad_rows), (0, 0)))
    s_send, s_recv, s_buf, s_zone, last = _small_start(small)

    out = {}
    my_chip = (2 * lax.axis_index("x") + lax.axis_index("y")).astype(jnp.int32).reshape(1)
    for tag, names, send, recv, thru, lands in pending:
        thru, lands = _chips_wait("rs_wait_" + tag, send, recv, thru, lands, last)
        for n, s4, l3 in zip(names, thru, lands):
            res4 = _adam_exchanged("adam_" + n, s4, l3, local(n, wts[n]), local(n, mom[n]), local(n, var[n]), my_chip)
            out[n] = tuple((a.T if n in TRANSPOSED else a)[None] for a in res4)
            last = res4[1]

    s_buf, s_zone = _small_wait(s_send, s_recv, s_buf, s_zone, last)
    zeros = jnp.zeros((1 + pad_rows, SMALL_W), F32)
    pack = lambda d: jnp.concatenate([_pack_small(d), zeros], 0)
    me = 2 * my_chip + core
    g_s, d_s, m_s, v_s = _adam_replicated("adam_small", s_zone, s_buf, pack(wts), pack(mom), pack(var), me)
    n_rows = small.shape[0] - 1 - pad_rows
    loss = g_s[n_rows, 0]
    small_out = [_unpack_small(b, shapes) for b in (g_s, d_s, m_s, v_s)]
    for n in REPLICATED:
        out[n] = tuple(b[n] for b in small_out)

    res = [loss, grad_x[None]]
    for k in range(4):
        res += [out[n][k] for n in WEIGHTS]
    return tuple(res)
```

```python
import functools

import numpy as np
import jax
import jax.numpy as jnp
from jax import lax
from jax.experimental import pallas as pl
from jax.experimental.pallas import tpu as pltpu

F32 = jnp.float32
BF = jnp.bfloat16
MESH = pl.DeviceIdType.MESH

NDEV = 8
NCHIP = 4
VMEM_LIMIT = 56 * 1024 * 1024
EPS = 1e-6
NEG = -1e30
GRID_W = 64
NA_HEADS, NA_DIM = 8, 128
NA_ROWS_WIN, NA_COLS_WIN = 8, 16
NA_HG = 4
NA_QROWS = 4
ML_HEADS, ML_NOPE, ML_ROPE, ML_V = 8, 128, 64, 128
ML_QK = ML_NOPE + ML_ROPE
ML_RANK = 512
ROPE_THETA = 10000.0
LR, B1, B2, ADAM_EPS, WD, STEP = 0.001, 0.9, 0.999, 1e-08, 0.01, 10
HI = lax.Precision.HIGHEST

_DN = {"nn": (((1,), (0,)), ((), ())), "nt": (((1,), (1,)), ((), ())), "tn": (((0,), (0,)), ((), ()))}


def _params(n):
    return pltpu.CompilerParams(dimension_semantics=("arbitrary",) * n, vmem_limit_bytes=VMEM_LIMIT)


def _sig(v):
    return jax.nn.sigmoid(v)


ANY = pl.BlockSpec(memory_space=pl.ANY)


def _coords():
    return lax.axis_index("x"), lax.axis_index("y"), lax.axis_index("c")


class _Part:
    inputs, out_shapes, sem_shapes, results = (), (), (), None

    def mid(self, ins, outs, sems):
        pass

    def late(self, ins, outs, sems):
        pass


class _After(_Part):
    def __init__(self, token):
        self.inputs = [token]

    def start(self, ins, outs, sems):
        pass

    finish = start


class _GatherPart(_Part):
    def __init__(self, names, shards):
        n = len(shards)
        self.names, self.inputs = list(names), list(shards)
        self.out_shapes = [jax.ShapeDtypeStruct((NDEV,) + a.shape, a.dtype) for a in shards]
        self.sem_shapes = [pltpu.SemaphoreType.DMA((n, 7)), pltpu.SemaphoreType.DMA((n, 7)),
                           pltpu.SemaphoreType.DMA((n,))]

    def _plan(self, ins, outs, sems):
        send_sems, recv_sems, local_sems = sems
        x, y, c = _coords()
        me, sib, diag = (x, y, c), (x, y, 1 - c), (1 - x, 1 - y, c)
        n1, n2 = (x ^ (1 - c), y ^ c, c), (x ^ c, y ^ (1 - c), c)

        def copy(i, k, block, to, src=None):
            px, py, pc = block
            dst = outs[i].at[4 * px + 2 * py + pc]
            return pltpu.make_async_remote_copy(
                src_ref=dst if src is None else src, dst_ref=dst, send_sem=send_sems.at[i, k],
                recv_sem=recv_sems.at[i, k], device_id=to, device_id_type=MESH)

        mine = [pltpu.make_async_copy(ins[i], outs[i].at[4 * x + 2 * y + c], local_sems.at[i])
                for i in range(len(ins))]
        return copy, mine, me, sib, n1, n2, diag

    def _own_sends(self, ins, copy, me, sib, n1, n2):
        return [copy(i, k, me, to, src=ins[i]) for i in range(len(ins)) for k, to in enumerate((sib, n1, n2))]

    def start(self, ins, outs, sems):
        copy, mine, me, sib, n1, n2, _ = self._plan(ins, outs, sems)
        for cp in mine + self._own_sends(ins, copy, me, sib, n1, n2):
            cp.start()

    def mid(self, ins, outs, sems):
        copy, _, me, sib, n1, n2, _ = self._plan(ins, outs, sems)
        for i in range(len(ins)):
            copy(i, 1, n1, me).wait_recv()
            copy(i, 3, n1, n2).start()
            copy(i, 4, n1, sib).start()

    def late(self, ins, outs, sems):
        copy, _, me, sib, _, n2, diag = self._plan(ins, outs, sems)
        for i in range(len(ins)):
            copy(i, 2, n2, me).wait_recv()
            copy(i, 5, n2, sib).start()
        for i in range(len(ins)):
            copy(i, 3, diag, me).wait_recv()
            copy(i, 6, diag, sib).start()

    def finish(self, ins, outs, sems):
        copy, mine, me, sib, n1, n2, diag = self._plan(ins, outs, sems)
        other = lambda dev: (dev[0], dev[1], sib[2])
        n = len(ins)
        for i in range(n):
            copy(i, 0, sib, me).wait_recv()
            for k, block in ((4, other(n2)), (5, other(n1)), (6, other(diag))):
                copy(i, k, block, me).wait_recv()
        for cp in self._own_sends(ins, copy, me, sib, n1, n2):
            cp.wait_send()
        for i in range(n):
            for k, block in ((3, n1), (4, n1), (5, n2), (6, diag)):
                copy(i, k, block, sib).wait_send()
        for cp in mine:
            cp.wait()


class _SiblingPart(_Part):
    def __init__(self, names, parts):
        n = len(parts)
        self.names, self.inputs = list(names), list(parts)
        self.out_shapes = [jax.ShapeDtypeStruct((NCHIP,) + a.shape[2:], a.dtype) for a in parts]
        self.sem_shapes = [pltpu.SemaphoreType.DMA((n,)), pltpu.SemaphoreType.DMA((n,))]

    def _copies(self, ins, outs, sems):
        x, y, c = _coords()
        return [pltpu.make_async_remote_copy(
            src_ref=ins[i].at[:, 1 - c], dst_ref=outs[i], send_sem=sems[0].at[i], recv_sem=sems[1].at[i],
            device_id=(x, y, 1 - c), device_id_type=MESH) for i in range(len(ins))]

    def start(self, ins, outs, sems):
        for cp in self._copies(ins, outs, sems):
            cp.start()

    def finish(self, ins, outs, sems):
        cps = self._copies(ins, outs, sems)
        for cp in cps:
            cp.wait_recv()
        for cp in cps:
            cp.wait_send()


HBM = pl.BlockSpec(memory_space=pltpu.HBM)
SEM = pl.BlockSpec(memory_space=pltpu.SEMAPHORE)


def _chip_peers():
    x, y, c = _coords()
    return [(1 - x, y, c), (x, 1 - y, c), (1 - x, 1 - y, c)]


def _chips_start(name, sums):
    n = len(sums)

    def body(*refs):
        ins, lands, send_sems, recv_sems = refs[:n], refs[n:2 * n], refs[2 * n], refs[2 * n + 1]
        for i in range(n):
            for k, (px, py, pc) in enumerate(_chip_peers()):
                pltpu.make_async_remote_copy(
                    src_ref=ins[i].at[2 * px + py], dst_ref=lands[i].at[k], send_sem=send_sems.at[3 * i + k],
                    recv_sem=recv_sems.at[3 * i + k], device_id=(px, py, pc), device_id_type=MESH).start()
        refs[-1][...] = jnp.zeros_like(refs[-1])

    lands = [lax.empty((3,) + a.shape[1:], a.dtype) for a in sums]
    bufs = list(sums) + lands
    res = pl.pallas_call(
        body, name=name, in_specs=[HBM] * (2 * n),
        out_specs=(SEM, SEM, *[HBM] * (2 * n), pl.BlockSpec(memory_space=pltpu.VMEM)),
        out_shape=(pltpu.SemaphoreType.DMA((3 * n,)), pltpu.SemaphoreType.DMA((3 * n,)),
                   *[pltpu.HBM(a.shape, a.dtype) for a in bufs], jax.ShapeDtypeStruct((8, 128), F32)),
        input_output_aliases={i: 2 + i for i in range(2 * n)},
        compiler_params=pltpu.CompilerParams(has_side_effects=pltpu.SideEffectType.DATAFLOW_SIDE_EFFECTING),
    )(*[pltpu.with_memory_space_constraint(a, pltpu.HBM) for a in bufs])
    return res[0], res[1], list(res[2:2 + n]), list(res[2 + n:2 + 2 * n]), res[-1]


def _chips_wait(name, send_sems, recv_sems, sums, lands, after):
    n = len(sums)

    def body(*refs):
        ins, zones, send, recv = refs[:n], refs[n:2 * n], refs[2 * n], refs[2 * n + 1]
        for i in range(n):
            for k, peer in enumerate(_chip_peers()):
                cp = pltpu.make_async_remote_copy(
                    src_ref=ins[i].at[0], dst_ref=zones[i].at[k], send_sem=send.at[3 * i + k],
                    recv_sem=recv.at[3 * i + k],
                    device_id=peer, device_id_type=MESH)
                cp.wait_send()
                cp.wait_recv()

    bufs = list(sums) + list(lands)
    res = pl.pallas_call(
        body, name=name, in_specs=[HBM] * (2 * n) + [SEM, SEM, ANY], out_specs=[HBM] * (2 * n),
        out_shape=[pltpu.HBM(a.shape, a.dtype) for a in bufs], input_output_aliases={i: i for i in range(2 * n)},
        compiler_params=pltpu.CompilerParams(has_side_effects=pltpu.SideEffectType.DATAFLOW_SIDE_EFFECTING),
    )(*bufs, send_sems, recv_sems, after)
    return list(res[:n]), list(res[n:])


def _call(name, body, grid, in_specs, out_specs, out_shape, args, comm=(), scratch=()):
    comm = [p for p in comm if p is not None]
    single = not isinstance(out_shape, (list, tuple))
    o_specs = [out_specs] if single else list(out_specs)
    o_shape = [out_shape] if single else list(out_shape)
    n_in, n_out = len(in_specs), len(o_specs)
    c_in = [a for p in comm for a in p.inputs]
    c_out = [s for p in comm for s in p.out_shapes]
    c_sem = [s for p in comm for s in p.sem_shapes]

    def wrapped(*refs):
        ins, outs = refs[:n_in], refs[n_in + len(c_in):n_in + len(c_in) + n_out]
        pos = [n_in, n_in + len(c_in) + n_out, n_in + len(c_in) + n_out + len(c_out)]
        own = refs[pos[2]:pos[2] + len(scratch)]
        pos[2] += len(scratch)
        split = []
        for p in comm:
            sizes = [len(p.inputs), len(p.out_shapes), len(p.sem_shapes)]
            split.append([refs[o:o + n] for o, n in zip(pos, sizes)])
            pos = [o + n for o, n in zip(pos, sizes)]
        step, steps = 0, 1
        for a, g in enumerate(grid):
            step, steps = step * g + pl.program_id(a), steps * g

        def run(which, at):
            def go():
                for p, cut in zip(comm, split):
                    getattr(p, which)(*cut)
            if not comm:
                return
            if grid:
                pl.when(step == at)(go)
            else:
                go()

        run("start", 0)
        body(*ins, *outs, *own)
        run("mid", steps // 2)
        run("late", max(steps // 2, steps - 1 - max(1, steps // 8)))
        run("finish", steps - 1)

    res = pl.pallas_call(
        wrapped, name=name, grid=grid, in_specs=list(in_specs) + [ANY] * len(c_in),
        out_specs=o_specs + [ANY] * len(c_out), out_shape=o_shape + c_out, scratch_shapes=list(scratch) + c_sem,
        compiler_params=_params(len(grid)),
    )(*args, *c_in)
    pos = n_out
    for p in comm:
        p.results = list(res[pos:pos + len(p.out_shapes)])
        pos += len(p.out_shapes)
    return res[0] if single else list(res[:n_out])


def _comm_only(name, comm):
    def body(o_ref):
        o_ref[...] = jnp.zeros_like(o_ref)

    _call(name, body, (), [], pl.BlockSpec(memory_space=pltpu.VMEM), jax.ShapeDtypeStruct((8, 128), F32), [], comm)


def _mm(name, grid, prods, extras, outs, epi, nacc=1, comm=()):
    n_p, n_e = len(prods), len(extras)

    def body(*refs):
        ab, ex, out = refs[:2 * n_p], refs[2 * n_p:2 * n_p + n_e], refs[2 * n_p + n_e:]
        accs = [None] * nacc
        for i, prod in enumerate(prods):
            dn, acc, loop = prod[6], prod[7], prod[8]
            a_ref, b_ref = ab[2 * i], ab[2 * i + 1]
            if loop:
                for g in range(loop):
                    t = lax.dot_general(a_ref[g], b_ref[g], _DN[dn], preferred_element_type=F32)
                    accs[acc] = t if accs[acc] is None else accs[acc] + t
            else:
                t = lax.dot_general(a_ref[...], b_ref[...], _DN[dn], preferred_element_type=F32)
                accs[acc] = t if accs[acc] is None else accs[acc] + t
        epi(accs, ex, out)

    in_specs, args = [], []
    for prod in prods:
        in_specs += [pl.BlockSpec(prod[1], prod[2]), pl.BlockSpec(prod[4], prod[5])]
        args += [prod[0], prod[3]]
    for e, e_blk, e_map in extras:
        in_specs.append(pl.BlockSpec(e_blk, e_map))
        args.append(e)
    return _call(name, body, grid, in_specs, [pl.BlockSpec(blk, mp) for _, _, blk, mp in outs],
                 [jax.ShapeDtypeStruct(s, d) for s, d, _, _ in outs], args, comm)


def _store(accs, ex, out):
    out[0][...] = accs[0].astype(out[0].dtype)


def _ew_tile(r, c, budget=3 << 19):
    for t in range(r - r % 16, 0, -16):
        if r % t == 0 and t * c * 4 <= budget:
            return t, c
    for t in range(c - c % 128, 0, -128):
        if c % t == 0 and r * t * 4 <= budget:
            return r, t
    return r, c


def _tile(n, want):
    t = min(n, want)
    assert n % t == 0, (n, want)
    return t


def _mm_nt(name, a, bt, out_dtype, tm=512, tn=512, comm=(), rows=None):
    m, k = a.shape
    n = rows or bt.shape[0]
    tm, tn = _tile(m, tm), (tn if n % tn == 0 else n)
    return _mm(name, (n // tn, m // tm),
               [(a, (tm, k), lambda j, i: (i, 0), bt, (tn, k), lambda j, i: (j, 0), "nt", 0, 0)], [],
               [((m, n), out_dtype, (tm, tn), lambda j, i: (i, j))], _store, comm=comm)[0]


def _mm_tn_into(name, a, b, buf, row0, ta=1024, tb=512):
    t, ka = a.shape
    nb = b.shape[1]
    ta, tb = (ta if ka % ta == 0 else ka), (tb if nb % tb == 0 else nb)

    def body(a_ref, b_ref, buf_in, buf_out, tile, sem):
        i, j = pl.program_id(0), pl.program_id(1)
        tile[...] = lax.dot_general(a_ref[...], b_ref[...], _DN["tn"], preferred_element_type=F32).astype(tile.dtype)
        rows = pl.ds(pl.multiple_of(row0 + i * ta, 16), ta)
        cp = pltpu.make_async_copy(tile, buf_out.at[rows, pl.ds(pl.multiple_of(j * tb, 128), tb)], sem)
        cp.start()
        cp.wait()

    return pl.pallas_call(
        body, name=name, grid=(ka // ta, nb // tb),
        in_specs=[pl.BlockSpec((t, ta), lambda i, j: (0, i)), pl.BlockSpec((t, tb), lambda i, j: (0, j)), ANY],
        out_specs=ANY, out_shape=jax.ShapeDtypeStruct(buf.shape, buf.dtype), input_output_aliases={2: 0},
        scratch_shapes=[pltpu.VMEM((ta, tb), buf.dtype), pltpu.SemaphoreType.DMA],
        compiler_params=_params(2))(a, b, buf)


def _rms_fwd(name, x, g, tm=256, comm=()):
    s, d = x.shape
    tm = _tile(s, tm)

    def body(x_ref, g_ref, o_ref):
        v = x_ref[...]
        o_ref[...] = (v * lax.rsqrt(jnp.mean(v * v, axis=-1, keepdims=True) + EPS) * g_ref[...]).astype(o_ref.dtype)

    return _call(name, body, (s // tm,),
                 [pl.BlockSpec((tm, d), lambda i: (i, 0)), pl.BlockSpec((1, d), lambda i: (0, 0))],
                 pl.BlockSpec((tm, d), lambda i: (i, 0)), jax.ShapeDtypeStruct((s, d), BF), [x, g], comm)


def _acc_rows(ref, part, i):
    @pl.when(i == 0)
    def _():
        ref[...] = part

    @pl.when(i > 0)
    def _():
        ref[...] += part


def _rms_bwd_math(dn, v, g):
    rstd = lax.rsqrt(jnp.mean(v * v, axis=-1, keepdims=True) + EPS)
    xh = v * rstd
    dxh = dn * g
    dx = rstd * (dxh - xh * jnp.mean(dxh * xh, axis=-1, keepdims=True))
    return dx, jnp.sum(dn * xh, axis=0, keepdims=True)


def _rms_bwd(name, dn, x, g, resid, tm=256, comm=()):
    s, d = x.shape
    tm = _tile(s, tm)

    def body(dn_ref, x_ref, g_ref, r_ref, dx_ref, dxb_ref, dg_ref):
        dx, part = _rms_bwd_math(dn_ref[...].astype(F32), x_ref[...], g_ref[...])
        tot = r_ref[...] + dx
        dx_ref[...] = tot
        dxb_ref[...] = tot.astype(BF)
        _acc_rows(dg_ref, part, pl.program_id(0))

    row = pl.BlockSpec((tm, d), lambda i: (i, 0))
    one = pl.BlockSpec((1, d), lambda i: (0, 0))
    return _call(name, body, (s // tm,), [row, row, one, row], [row, row, one],
                 [jax.ShapeDtypeStruct((s, d), F32), jax.ShapeDtypeStruct((s, d), BF),
                  jax.ShapeDtypeStruct((1, d), F32)], [dn, x, g, resid], comm)


def _loss_head(h, target, g, tm=256):
    s, d = h.shape
    tm = _tile(s, tm)

    def body(h_ref, t_ref, g_ref, dh_ref, dg_ref, loss_ref):
        v, gv = h_ref[...], g_ref[...]
        rstd = lax.rsqrt(jnp.mean(v * v, axis=-1, keepdims=True) + EPS)
        xh = v * rstd
        err = xh * gv - t_ref[...]
        part_loss = 0.5 * jnp.sum(jnp.mean(err * err, axis=-1, keepdims=True), axis=0, keepdims=True)
        dy = err * (1.0 / d)
        dxh = dy * gv
        dh_ref[...] = rstd * (dxh - xh * jnp.mean(dxh * xh, axis=-1, keepdims=True))
        i = pl.program_id(0)
        _acc_rows(dg_ref, jnp.sum(dy * xh, axis=0, keepdims=True), i)
        _acc_rows(loss_ref, jnp.broadcast_to(part_loss, loss_ref.shape), i)

    row = pl.BlockSpec((tm, d), lambda i: (i, 0))
    one = pl.BlockSpec((1, d), lambda i: (0, 0))
    return pl.pallas_call(
        body, name="loss_head", grid=(s // tm,), in_specs=[row, row, one],
        out_specs=[row, one, pl.BlockSpec((1, 128), lambda i: (0, 0))],
        out_shape=[jax.ShapeDtypeStruct((s, d), F32), jax.ShapeDtypeStruct((1, d), F32),
                   jax.ShapeDtypeStruct((1, 128), F32)],
        compiler_params=_params(1))(h, target, g)


def _pl_bwd_elem(dh, pe, t, tm=256):
    s, d = dh.shape
    tm = _tile(s, tm)

    def body(dh_ref, pe_ref, t_ref, dt_ref, dpe_ref):
        dh_v, sg = dh_ref[...], _sig(t_ref[...])
        dt_ref[...] = (dh_v * pe_ref[...].astype(F32) * sg * (1.0 - sg)).astype(BF)
        dpe_ref[...] = (dh_v * sg).astype(BF)

    row = pl.BlockSpec((tm, d), lambda i: (i, 0))
    return pl.pallas_call(
        body, name="pl_bwd_elem", grid=(s // tm,), in_specs=[row, row, row], out_specs=[row, row],
        out_shape=[jax.ShapeDtypeStruct((s, d), BF)] * 2, compiler_params=_params(1))(dh, pe, t)


def _ffn_up(name, xn, wg, wu, tm=1024, comm=()):
    s, d = xn.shape
    g, fb, _ = wg.shape
    tm = _tile(s, tm)

    def epi(accs, ex, out):
        hg, hu = accs
        out[0][...] = hg.astype(BF)
        out[1][...] = hu.astype(BF)
        out[2][...] = (hg * _sig(hg) * hu).astype(BF)

    a_map = lambda j, i: (i, 0)
    w_map = lambda j, i: (j, 0, 0)
    o = ((g, s, fb), BF, (None, tm, fb), lambda j, i: (j, i, 0))
    return _mm(name, (g, s // tm),
               [(xn, (tm, d), a_map, wg, (None, fb, d), w_map, "nt", 0, 0),
                (xn, (tm, d), a_map, wu, (None, fb, d), w_map, "nt", 1, 0)], [], [o, o, o], epi, nacc=2, comm=comm)


def _ffn_down(name, a, wd, resid, tm=1024, tn=512, comm=()):
    g, s, fb = a.shape
    d = wd.shape[2]
    tm, tn = _tile(s, tm), _tile(d, tn)

    def epi(accs, ex, out):
        out[0][...] = ex[0][...] + 0.5 * accs[0]

    return _mm(name, (d // tn, s // tm),
               [(a, (g, tm, fb), lambda j, i: (0, i, 0), wd, (g, fb, tn), lambda j, i: (0, 0, j), "nn", 0, g)],
               [(resid, (tm, tn), lambda j, i: (i, j))],
               [((s, d), F32, (tm, tn), lambda j, i: (i, j))], epi, comm=comm)[0]


def _ffn_bwd_act(name, dh, wd, hg, hu, tm=1024, comm=()):
    s, d = dh.shape
    g, fb, _ = wd.shape
    tm = _tile(s, tm)

    def epi(accs, ex, out):
        da = 0.5 * accs[0]
        hg_v, hu_v = ex[0][...].astype(F32), ex[1][...].astype(F32)
        sg = _sig(hg_v)
        out[0][...] = (da * hu_v * (sg * (1.0 + hg_v * (1.0 - sg)))).astype(BF)
        out[1][...] = (da * (hg_v * sg)).astype(BF)

    blk = (None, tm, fb)
    gmap = lambda j, i: (j, i, 0)
    return _mm(name, (g, s // tm),
               [(dh, (tm, d), lambda j, i: (i, 0), wd, (None, fb, d), lambda j, i: (j, 0, 0), "nt", 0, 0)],
               [(hg, blk, gmap), (hu, blk, gmap)],
               [((g, s, fb), BF, blk, gmap), ((g, s, fb), BF, blk, gmap)], epi, comm=comm)


def _ffn_bwd_wd(name, a, dh, tn=1024, comm=()):
    g, s, fb = a.shape
    d = dh.shape[1]
    tn = _tile(d, tn)

    def epi(accs, ex, out):
        out[0][...] = (0.5 * accs[0]).astype(BF)

    return _mm(name, (g, d // tn),
               [(a, (None, s, fb), lambda j, i: (j, 0, 0), dh, (s, tn), lambda j, i: (0, i), "tn", 0, 0)], [],
               [((g, fb, d), BF, (None, fb, tn), lambda j, i: (j, 0, i))], epi, comm=comm)[0]


def _ffn_bwd_wup(name, xn, dhg, dhu, tk=1024, comm=()):
    s, d = xn.shape
    g, _, fb = dhg.shape
    tk = _tile(d, tk)

    def epi(accs, ex, out):
        out[0][...] = accs[0].astype(BF)
        out[1][...] = accs[1].astype(BF)

    a_map = lambda j, i: (j, 0, 0)
    b_map = lambda j, i: (0, i)
    o = ((g, fb, d), BF, (None, fb, tk), lambda j, i: (j, 0, i))
    return _mm(name, (g, d // tk),
               [(dhg, (None, s, fb), a_map, xn, (s, tk), b_map, "tn", 0, 0),
                (dhu, (None, s, fb), a_map, xn, (s, tk), b_map, "tn", 1, 0)], [], [o, o], epi, nacc=2, comm=comm)


def _ffn_bwd_x(name, dhg, dhu, wg, wu, tm=512, tn=512, comm=()):
    g, s, fb = dhg.shape
    d = wg.shape[2]
    tm, tn = _tile(s, tm), _tile(d, tn)
    a_blk, a_map = (g, tm, fb), lambda j, i: (0, i, 0)
    b_blk, b_map = (g, fb, tn), lambda j, i: (0, 0, j)
    return _mm(name, (d // tn, s // tm),
               [(dhg, a_blk, a_map, wg, b_blk, b_map, "nn", 0, g), (dhu, a_blk, a_map, wu, b_blk, b_map, "nn", 0, g)],
               [], [((s, d), F32, (tm, tn), lambda j, i: (i, j))], _store, comm=comm)[0]


def _ffn_forward(tag, h, gain, get_wgu, get_wd, norm_comm=(), up_comm=(), down_comm=()):
    xn = _rms_fwd(tag + "_norm", h, gain, comm=norm_comm)
    wg, wu = get_wgu()
    hg, hu, a = _ffn_up(tag + "_up", xn, wg, wu, comm=up_comm)
    return _ffn_down(tag + "_down", a, get_wd(), h, comm=down_comm), (xn, hg, hu, a)


def _na_geometry(rows):
    kh = min(NA_ROWS_WIN, rows)
    cols = np.arange(GRID_W)
    col_start = np.clip(cols - NA_COLS_WIN // 2, 0, GRID_W - NA_COLS_WIN)
    mask = (cols[None, :] >= col_start[:, None]) & (cols[None, :] < col_start[:, None] + NA_COLS_WIN)
    dc = np.clip(cols[None, :] - cols[:, None], -(NA_COLS_WIN - 1), NA_COLS_WIN - 1) + (NA_COLS_WIN - 1)
    return kh, mask, dc


def _na_table(rpb, rows):
    _, mask, dc = _na_geometry(rows)
    nd, nc, cells = 2 * NA_ROWS_WIN - 1, 2 * NA_COLS_WIN - 1, GRID_W * GRID_W
    onehot = np.zeros((128, cells), np.float32)
    onehot[dc.reshape(-1), np.arange(cells)] = mask.reshape(-1).astype(np.float32)
    off = np.where(mask.reshape(1, -1), 0.0, NEG).astype(np.float32)

    def body(r_ref, e_ref, off_ref, o_ref):
        o_ref[...] = jnp.dot(r_ref[...], e_ref[...], precision=HI, preferred_element_type=F32) + off_ref[...]

    flat = pl.pallas_call(body, name="na_table", out_shape=jax.ShapeDtypeStruct((NA_HEADS * nd, cells), F32),
                          compiler_params=_params(0))(
        jnp.pad(rpb.reshape(NA_HEADS * nd, nc), ((0, 0), (0, 128 - nc))), jnp.asarray(onehot), jnp.asarray(off))
    return flat.reshape(NA_HEADS, nd, GRID_W, GRID_W)


class _NaPlan:
    def __init__(self, s):
        self.s, self.rows = s, s // GRID_W
        self.kh = min(NA_ROWS_WIN, self.rows)
        self.qr = min(NA_QROWS, self.rows)
        self.kr = min(self.rows, self.kh + self.qr - 1)
        self.groups = self.rows // self.qr
        self.nd = 2 * NA_ROWS_WIN - 1
        self.hw, self.nq = NA_HG * NA_DIM, NA_HEADS // NA_HG
        clip = lambda v, hi: min(max(v, 0), hi)
        pats = [(clip(g * self.qr - self.kh // 2, self.rows - self.kr) - g * self.qr,)
                + tuple(clip(g * self.qr + a - self.kh // 2, self.rows - self.kh) - g * self.qr for a in range(self.qr))
                for g in range(self.groups)]
        self.rebuild = [g for g in range(self.groups) if g == 0 or pats[g] != pats[g - 1]]

    def first_key_row(self, g):
        return jnp.clip(g * self.qr - self.kh // 2, 0, self.rows - self.kr)

    def specs(self):
        blk = pl.BlockSpec((self.qr * GRID_W, self.hw), lambda j, g: (g, j))
        k_spec = pl.BlockSpec((self.s, self.hw), lambda j, g: (0, self.nq + j))
        v_spec = pl.BlockSpec((self.s, self.hw), lambda j, g: (0, 2 * self.nq + j))
        t_spec = pl.BlockSpec((NA_HG, self.nd, GRID_W, GRID_W), lambda j, g: (j, 0, 0, 0))
        return blk, k_spec, v_spec, t_spec

    def bias_scratch(self):
        return pltpu.VMEM((NA_HG, self.qr * GRID_W, self.kr * GRID_W), F32)

    def fill_bias(self, t_ref, bias_ref, g):
        def build():
            r0, ks = g * self.qr, self.first_key_row(g)
            for a in range(self.qr):
                rs = jnp.clip(r0 + a - self.kh // 2, 0, self.rows - self.kh)
                for i in range(self.kr):
                    valid = jnp.logical_and(ks + i >= rs, ks + i < rs + self.kh)
                    idx = jnp.clip(ks + i - r0 - a + NA_ROWS_WIN - 1, 0, self.nd - 1)
                    for h in range(NA_HG):
                        bias_ref[h, a * GRID_W:(a + 1) * GRID_W, i * GRID_W:(i + 1) * GRID_W] = jnp.where(
                            valid, t_ref[h, idx], NEG)

        pl.when(functools.reduce(jnp.logical_or, [g == r for r in self.rebuild]))(build)

    def window(self, g):
        return pl.ds(pl.multiple_of(self.first_key_row(g) * GRID_W, GRID_W), self.kr * GRID_W)


def _na_probs(q, k, bias):
    sc = lax.dot_general(q, k, _DN["nt"], preferred_element_type=F32) * (NA_DIM ** -0.5) + bias
    e = jnp.exp(sc - jnp.max(sc, axis=-1, keepdims=True))
    return e / jnp.sum(e, axis=-1, keepdims=True)


def _na_fwd(qkv, table, comm=()):
    plan = _NaPlan(qkv.shape[0])
    blk, k_spec, v_spec, t_spec = plan.specs()

    def body(q_ref, k_ref, v_ref, t_ref, o_ref, bias_ref):
        g = pl.program_id(1)
        plan.fill_bias(t_ref, bias_ref, g)
        win = plan.window(g)
        for h in range(NA_HG):
            cs = slice(h * NA_DIM, (h + 1) * NA_DIM)
            p = _na_probs(q_ref[:, cs], k_ref[win, cs], bias_ref[h])
            o_ref[:, cs] = jnp.dot(p.astype(BF), v_ref[win, cs], preferred_element_type=F32).astype(BF)

    return _call("na_fwd", body, (plan.nq, plan.groups), [blk, k_spec, v_spec, t_spec], blk,
                 jax.ShapeDtypeStruct((plan.s, NA_HEADS * NA_DIM), BF), [qkv, qkv, qkv, table], comm,
                 scratch=[plan.bias_scratch()])


def _na_bwd(qkv, table, do, comm=()):
    plan = _NaPlan(qkv.shape[0])
    blk, k_spec, v_spec, t_spec = plan.specs()
    qr, kr = plan.qr, plan.kr

    def body(q_ref, k_ref, v_ref, t_ref, do_ref, dq_ref, dk_ref, dv_ref, dt_ref, bias_ref):
        g = pl.program_id(1)

        @pl.when(g == 0)
        def _():
            dk_ref[...] = jnp.zeros_like(dk_ref)
            dv_ref[...] = jnp.zeros_like(dv_ref)
            dt_ref[...] = jnp.zeros_like(dt_ref)

        plan.fill_bias(t_ref, bias_ref, g)
        win = plan.window(g)
        base = plan.first_key_row(g) - g * qr + NA_ROWS_WIN - 1
        for h in range(NA_HG):
            cs = slice(h * NA_DIM, (h + 1) * NA_DIM)
            q, k, v, do_h = q_ref[:, cs], k_ref[win, cs], v_ref[win, cs], do_ref[:, cs]
            p = _na_probs(q, k, bias_ref[h])
            dp = lax.dot_general(do_h, v, _DN["nt"], preferred_element_type=F32)
            ds = p * (dp - jnp.sum(p * dp, axis=-1, keepdims=True))
            for dlt in range(1 - qr, kr):
                tiles = [ds[a * GRID_W:(a + 1) * GRID_W, (a + dlt) * GRID_W:(a + dlt + 1) * GRID_W]
                         for a in range(qr) if 0 <= a + dlt < kr]
                dt_ref[h, jnp.clip(base + dlt, 0, plan.nd - 1)] += functools.reduce(jnp.add, tiles)
            dsb = (ds * (NA_DIM ** -0.5)).astype(BF)
            dq_ref[:, cs] = jnp.dot(dsb, k, preferred_element_type=F32).astype(BF)
            dk_ref[win, cs] += lax.dot_general(dsb, q, _DN["tn"], preferred_element_type=F32)
            dv_ref[win, cs] += lax.dot_general(p.astype(BF), do_h, _DN["tn"], preferred_element_type=F32)

    width = NA_HEADS * NA_DIM
    whole = pl.BlockSpec((plan.s, plan.hw), lambda j, g: (0, j))
    return _call(
        "na_bwd", body, (plan.nq, plan.groups), [blk, k_spec, v_spec, t_spec, blk], [blk, whole, whole, t_spec],
        [jax.ShapeDtypeStruct((plan.s, width), BF), jax.ShapeDtypeStruct((plan.s, width), F32),
         jax.ShapeDtypeStruct((plan.s, width), F32),
         jax.ShapeDtypeStruct((NA_HEADS, plan.nd, GRID_W, GRID_W), F32)],
        [qkv, qkv, qkv, table, do], comm, scratch=[plan.bias_scratch()])


def _na_rpb_grad(dt, rows):
    _, mask, dc = _na_geometry(rows)
    nd, nc = 2 * NA_ROWS_WIN - 1, 2 * NA_COLS_WIN - 1
    onehot = np.zeros((GRID_W * GRID_W, 128), np.float32)
    onehot[np.arange(GRID_W * GRID_W), dc.reshape(-1)] = mask.reshape(-1).astype(np.float32)
    flat = dt.reshape(NA_HEADS * nd, GRID_W * GRID_W)

    def body(a_ref, e_ref, o_ref):
        o_ref[...] = jnp.dot(a_ref[...], e_ref[...], precision=HI, preferred_element_type=F32)

    out = pl.pallas_call(body, name="na_rpb_grad", out_shape=jax.ShapeDtypeStruct((NA_HEADS * nd, 128), F32),
                         compiler_params=_params(0))(flat, jnp.asarray(onehot))
    return out[:, :nc].reshape(NA_HEADS, nd, nc)


def _rope_consts(s):
    pos = np.arange(s, dtype=np.float32)
    inv = (1.0 / (ROPE_THETA ** (np.arange(0, ML_ROPE, 2, dtype=np.float32) / ML_ROPE))).astype(np.float32)
    ang = pos[:, None] * inv[None, :]
    cos, sin = np.cos(ang).astype(np.float32), np.sin(ang).astype(np.float32)
    half = ML_ROPE // 2
    rot = np.zeros((ML_ROPE, ML_ROPE), np.float32)
    rot[np.arange(half) + half, np.arange(half)] = -1.0
    rot[np.arange(half), np.arange(half) + half] = 1.0
    return (jnp.asarray(np.concatenate([cos, cos], 1)), jnp.asarray(np.concatenate([sin, sin], 1)),
            jnp.asarray(rot), jnp.asarray(rot.T.copy()))


def _rope(v, cos, sin, rot):
    return v * cos + jnp.dot(v, rot, precision=HI, preferred_element_type=F32) * sin


def _unrope(dv, cos, sin, rot_t):
    return dv * cos + jnp.dot(dv * sin, rot_t, precision=HI, preferred_element_type=F32)


def _rms(v, g):
    return v * lax.rsqrt(jnp.mean(v * v, axis=-1, keepdims=True) + EPS) * g


def _mla_prep(lat, gq, gkv, cos, sin, rot, tm=256):
    s, w = lat.shape
    tm = _tile(s, tm)

    def body(l_ref, gq_ref, gkv_ref, c_ref, s_ref, r_ref, cq_ref, ckv_ref, kr_ref):
        cq_ref[...] = _rms(l_ref[:, :ML_RANK], gq_ref[...]).astype(BF)
        ckv_ref[...] = _rms(l_ref[:, ML_RANK:2 * ML_RANK], gkv_ref[...]).astype(BF)
        kr_ref[...] = _rope(l_ref[:, 2 * ML_RANK:], c_ref[...], s_ref[...], r_ref[...]).astype(BF)

    row = lambda c: pl.BlockSpec((tm, c), lambda i: (i, 0))
    full = lambda a: pl.BlockSpec(a.shape, lambda i: (0, 0))
    return pl.pallas_call(
        body, name="mla_prep", grid=(s // tm,),
        in_specs=[row(w), full(gq), full(gkv), row(ML_ROPE), row(ML_ROPE), full(rot)],
        out_specs=[row(ML_RANK), row(ML_RANK), row(ML_ROPE)],
        out_shape=[jax.ShapeDtypeStruct((s, ML_RANK), BF), jax.ShapeDtypeStruct((s, ML_RANK), BF),
                   jax.ShapeDtypeStruct((s, ML_ROPE), BF)],
        compiler_params=_params(1))(lat, gq, gkv, cos, sin, rot)


def _mla_q_proj(cq, wuq, cos, sin, rot, tm=512, comm=()):
    s, k = cq.shape
    tm = _tile(s, tm)

    def epi(accs, ex, out):
        acc = accs[0]
        out[0][:, :ML_NOPE] = acc[:, :ML_NOPE].astype(BF)
        out[0][:, ML_NOPE:] = _rope(acc[:, ML_NOPE:], ex[0][...], ex[1][...], ex[2][...]).astype(BF)

    rmap = lambda j, i: (i, 0)
    return _mm("mla_q_proj", (ML_HEADS, s // tm),
               [(cq, (tm, k), rmap, wuq, (None, ML_QK, k), lambda j, i: (j, 0, 0), "nt", 0, 0)],
               [(cos, (tm, ML_ROPE), rmap), (sin, (tm, ML_ROPE), rmap), (rot, rot.shape, lambda j, i: (0, 0))],
               [((ML_HEADS, s, ML_QK), BF, (None, tm, ML_QK), lambda j, i: (j, i, 0))], epi, comm=comm)[0]


def _mla_kv_proj(ckv, wukv, kr, tm=512, comm=()):
    s, k = ckv.shape
    tm = _tile(s, tm)

    def epi(accs, ex, out):
        acc = accs[0]
        out[0][:, :ML_NOPE] = acc[:, :ML_NOPE].astype(BF)
        out[0][:, ML_NOPE:] = ex[0][...]
        out[1][...] = acc[:, ML_NOPE:].astype(BF)

    rmap = lambda j, i: (i, 0)
    gmap = lambda j, i: (j, i, 0)
    return _mm("mla_kv_proj", (ML_HEADS, s // tm),
               [(ckv, (tm, k), rmap, wukv, (None, k, ML_NOPE + ML_V), lambda j, i: (j, 0, 0), "nn", 0, 0)],
               [(kr, (tm, ML_ROPE), rmap)],
               [((ML_HEADS, s, ML_QK), BF, (None, tm, ML_QK), gmap), ((ML_HEADS, s, ML_V), BF, (None, tm, ML_V), gmap)],
               epi, comm=comm)


def _mla_probs(q, k):
    sc = lax.dot_general(q, k, _DN["nt"], preferred_element_type=F32) * (ML_QK ** -0.5)
    e = jnp.exp(sc - jnp.max(sc, axis=-1, keepdims=True))
    return e / jnp.sum(e, axis=-1, keepdims=True)


def _mla_fwd(q, k, v, tq=1024, comm=()):
    _, s, _ = q.shape
    tq = _tile(s, tq)

    def body(q_ref, k_ref, v_ref, o_ref):
        p = _mla_probs(q_ref[...], k_ref[...])
        o_ref[...] = jnp.dot(p.astype(BF), v_ref[...], preferred_element_type=F32).astype(BF)

    return _call("mla_fwd", body, (ML_HEADS, s // tq),
                 [pl.BlockSpec((None, tq, ML_QK), lambda h, i: (h, i, 0)),
                  pl.BlockSpec((None, s, ML_QK), lambda h, i: (h, 0, 0)),
                  pl.BlockSpec((None, s, ML_V), lambda h, i: (h, 0, 0))],
                 pl.BlockSpec((tq, ML_V), lambda h, i: (i, h)),
                 jax.ShapeDtypeStruct((s, ML_HEADS * ML_V), BF), [q, k, v], comm)


def _mla_bwd(q, k, v, do, tq=1024, comm=()):
    _, s, _ = q.shape
    tq = _tile(s, tq)

    def body(q_ref, k_ref, v_ref, do_ref, dq_ref, dk_ref, dv_ref):
        i = pl.program_id(1)
        qv, kv, vv, dov = q_ref[...], k_ref[...], v_ref[...], do_ref[...]
        p = _mla_probs(qv, kv)
        dp = lax.dot_general(dov, vv, _DN["nt"], preferred_element_type=F32)
        ds = (p * (dp - jnp.sum(p * dp, axis=-1, keepdims=True)) * (ML_QK ** -0.5)).astype(BF)
        dq_ref[...] = jnp.dot(ds, kv, preferred_element_type=F32)
        _acc_rows(dk_ref, lax.dot_general(ds, qv, _DN["tn"], preferred_element_type=F32), i)
        _acc_rows(dv_ref, lax.dot_general(p.astype(BF), dov, _DN["tn"], preferred_element_type=F32), i)

    return _call(
        "mla_bwd", body, (ML_HEADS, s // tq),
        [pl.BlockSpec((None, tq, ML_QK), lambda h, i: (h, i, 0)),
         pl.BlockSpec((None, s, ML_QK), lambda h, i: (h, 0, 0)),
         pl.BlockSpec((None, s, ML_V), lambda h, i: (h, 0, 0)),
         pl.BlockSpec((tq, ML_V), lambda h, i: (i, h))],
        [pl.BlockSpec((None, tq, ML_QK), lambda h, i: (h, i, 0)),
         pl.BlockSpec((None, s, ML_QK), lambda h, i: (h, 0, 0)),
         pl.BlockSpec((None, s, ML_V), lambda h, i: (h, 0, 0))],
        [jax.ShapeDtypeStruct((ML_HEADS, s, ML_QK), F32), jax.ShapeDtypeStruct((ML_HEADS, s, ML_QK), F32),
         jax.ShapeDtypeStruct((ML_HEADS, s, ML_V), F32)],
        [q, k, v, do], comm)


def _mla_post(dq, dk, dv, cos, sin, rot_t, tm=1024):
    _, s, _ = dq.shape
    tm = _tile(s, tm)

    def body(dq_ref, dk_ref, dv_ref, c_ref, s_ref, r_ref, dqp_ref, dkv_ref, dkr_ref):
        h = pl.program_id(1)
        dqv, dkk = dq_ref[...], dk_ref[...]
        dqp_ref[:, :ML_NOPE] = dqv[:, :ML_NOPE].astype(BF)
        dqp_ref[:, ML_NOPE:] = _unrope(dqv[:, ML_NOPE:], c_ref[...], s_ref[...], r_ref[...]).astype(BF)
        dkv_ref[:, :ML_NOPE] = dkk[:, :ML_NOPE].astype(BF)
        dkv_ref[:, ML_NOPE:] = dv_ref[...].astype(BF)
        _acc_rows(dkr_ref, dkk[:, ML_NOPE:], h)

    gspec = lambda c: pl.BlockSpec((None, tm, c), lambda i, h: (h, i, 0))
    rspec = pl.BlockSpec((tm, ML_ROPE), lambda i, h: (i, 0))
    return pl.pallas_call(
        body, name="mla_post", grid=(s // tm, ML_HEADS),
        in_specs=[gspec(ML_QK), gspec(ML_QK), gspec(ML_V), rspec, rspec,
                  pl.BlockSpec(rot_t.shape, lambda i, h: (0, 0))],
        out_specs=[gspec(ML_QK), gspec(ML_NOPE + ML_V), rspec],
        out_shape=[jax.ShapeDtypeStruct((ML_HEADS, s, ML_QK), BF),
                   jax.ShapeDtypeStruct((ML_HEADS, s, ML_NOPE + ML_V), BF),
                   jax.ShapeDtypeStruct((s, ML_ROPE), F32)],
        compiler_params=_params(2))(dq, dk, dv, cos, sin, rot_t)


def _mla_lat_bwd(dcq, dckv, dkr, lat, gq, gkv, cos, sin, rot_t, tm=256):
    s, w = lat.shape
    tm = _tile(s, tm)

    def body(dcq_ref, dckv_ref, dkr_ref, l_ref, gq_ref, gkv_ref, c_ref, s_ref, r_ref, dl_ref, dgq_ref, dgkv_ref):
        i = pl.program_id(0)
        dql, pq = _rms_bwd_math(dcq_ref[...], l_ref[:, :ML_RANK], gq_ref[...])
        dkl, pkv = _rms_bwd_math(dckv_ref[...], l_ref[:, ML_RANK:2 * ML_RANK], gkv_ref[...])
        dl_ref[:, :ML_RANK] = dql.astype(BF)
        dl_ref[:, ML_RANK:2 * ML_RANK] = dkl.astype(BF)
        dl_ref[:, 2 * ML_RANK:] = _unrope(dkr_ref[...], c_ref[...], s_ref[...], r_ref[...]).astype(BF)
        _acc_rows(dgq_ref, pq, i)
        _acc_rows(dgkv_ref, pkv, i)

    row = lambda c: pl.BlockSpec((tm, c), lambda i: (i, 0))
    full = lambda a: pl.BlockSpec(a.shape, lambda i: (0, 0))
    return pl.pallas_call(
        body, name="mla_lat_bwd", grid=(s // tm,),
        in_specs=[row(ML_RANK), row(ML_RANK), row(ML_ROPE), row(w), full(gq), full(gkv), row(ML_ROPE), row(ML_ROPE),
                  full(rot_t)],
        out_specs=[row(w), full(gq), full(gkv)],
        out_shape=[jax.ShapeDtypeStruct((s, w), BF), jax.ShapeDtypeStruct(gq.shape, F32),
                   jax.ShapeDtypeStruct(gkv.shape, F32)],
        compiler_params=_params(1))(dcq, dckv, dkr, lat, gq, gkv, cos, sin, rot_t)


def _grp_dw(name, a, dout, ta=1024):
    s, k = a.shape
    ta = _tile(k, ta)
    if dout.ndim == 3:
        g, _, nb = dout.shape
        b_blk, b_map = (None, s, nb), lambda j, i: (j, 0, 0)
    else:
        g, nb = NDEV, dout.shape[1] // NDEV
        b_blk, b_map = (s, nb), lambda j, i: (0, j)
    return _mm(name, (g, k // ta),
               [(a, (s, ta), lambda j, i: (0, i), dout, b_blk, b_map, "tn", 0, 0)], [],
               [((g, k, nb), BF, (None, ta, nb), lambda j, i: (j, i, 0))], _store)[0]


def _grp_dw_t(name, dout, a, ta=512):
    g, s, nb = dout.shape
    k = a.shape[1]
    ta = _tile(k, ta)
    return _mm(name, (g, k // ta),
               [(dout, (None, s, nb), lambda j, i: (j, 0, 0), a, (s, ta), lambda j, i: (0, i), "tn", 0, 0)], [],
               [((g, nb, k), BF, (None, nb, ta), lambda j, i: (j, 0, i))], _store)[0]


def _grp_dx_t(name, dout, wt, tm=512, tn=512, comm=()):
    g, s, nb = dout.shape
    k = wt.shape[2]
    tm, tn = _tile(s, tm), _tile(k, tn)
    return _mm(name, (k // tn, s // tm),
               [(dout, (g, tm, nb), lambda j, i: (0, i, 0), wt, (g, nb, tn), lambda j, i: (0, 0, j), "nn", 0, g)], [],
               [((s, k), F32, (tm, tn), lambda j, i: (i, j))], _store, comm=comm)[0]


def _grp_dx(name, dout, w, tm=512, tn=512, out_dtype=F32, comm=()):
    g, s, nb = dout.shape
    k = w.shape[1]
    tm, tn = _tile(s, tm), _tile(k, tn)
    return _mm(name, (k // tn, s // tm),
               [(dout, (g, tm, nb), lambda j, i: (0, i, 0), w, (g, tn, nb), lambda j, i: (0, j, 0), "nt", 0, g)], [],
               [((s, k), out_dtype, (tm, tn), lambda j, i: (i, j))], _store, comm=comm)[0]


def _row_dw(name, a, dout, tn=2048):
    s, n = dout.shape
    tn = _tile(n, tn)
    if a.ndim == 3:
        kb = a.shape[2]
        a_blk, a_map = (None, s, kb), lambda j, i: (j, 0, 0)
    else:
        kb = a.shape[1] // NDEV
        a_blk, a_map = (s, kb), lambda j, i: (0, j)
    return _mm(name, (NDEV, n // tn),
               [(a, a_blk, a_map, dout, (s, tn), lambda j, i: (0, i), "tn", 0, 0)], [],
               [((NDEV, kb, n), BF, (None, kb, tn), lambda j, i: (j, 0, i))], _store)[0]


def _mix_merge(oa, ob, wa, wb, ga, gb, tm=1024, comm=()):
    s, k = oa.shape
    g, _, nb = wa.shape
    tm = _tile(s, tm)

    def epi(accs, ex, out):
        ya, yb = accs
        out[0][...] = ya.astype(BF)
        out[1][...] = yb.astype(BF)
        out[2][...] = (_sig(ex[0][...]) * ya + _sig(ex[1][...]) * yb).astype(BF)

    rmap = lambda j, i: (i, 0)
    wmap = lambda j, i: (j, 0, 0)
    o = ((g, s, nb), BF, (None, tm, nb), lambda j, i: (j, i, 0))
    cmap = lambda j, i: (i, j)
    return _mm("mix_merge", (g, s // tm),
               [(oa, (tm, k), rmap, wa, (None, k, nb), wmap, "nn", 0, 0),
                (ob, (tm, k), rmap, wb, (None, k, nb), wmap, "nn", 1, 0)],
               [(ga, (tm, nb), cmap), (gb, (tm, nb), cmap)], [o, o, o], epi, nacc=2, comm=comm)


def _mix_out(merged, wout, resid, tm=1024, tn=512):
    g, s, kb = merged.shape
    d = wout.shape[2]
    tm, tn = _tile(s, tm), _tile(d, tn)

    def epi(accs, ex, out):
        out[0][...] = ex[0][...] + accs[0]

    return _mm("mix_out", (d // tn, s // tm),
               [(merged, (g, tm, kb), lambda j, i: (0, i, 0), wout, (g, kb, tn), lambda j, i: (0, 0, j), "nn", 0, g)],
               [(resid, (tm, tn), lambda j, i: (i, j))],
               [((s, d), F32, (tm, tn), lambda j, i: (i, j))], epi)[0]


def _mix_out_bwd(dh, wout, ga, gb, ya, yb, tm=1024, comm=()):
    s, d = dh.shape
    g, kb, _ = wout.shape
    tm = _tile(s, tm)

    def epi(accs, ex, out):
        dm = accs[0]
        sa, sb = _sig(ex[0][...]), _sig(ex[1][...])
        out[0][...] = (dm * sa).astype(BF)
        out[1][...] = (dm * sb).astype(BF)
        out[2][...] = (dm * ex[2][...].astype(F32) * sa * (1.0 - sa)).astype(BF)
        out[3][...] = (dm * ex[3][...].astype(F32) * sb * (1.0 - sb)).astype(BF)

    cmap = lambda j, i: (i, j)
    gmap = lambda j, i: (j, i, 0)
    og = ((g, s, kb), BF, (None, tm, kb), gmap)
    oc = ((s, g * kb), BF, (tm, kb), cmap)
    return _mm("mix_out_bwd", (g, s // tm),
               [(dh, (tm, d), lambda j, i: (i, 0), wout, (None, kb, d), lambda j, i: (j, 0, 0), "nt", 0, 0)],
               [(ga, (tm, kb), cmap), (gb, (tm, kb), cmap), (ya, (None, tm, kb), gmap), (yb, (None, tm, kb), gmap)],
               [og, og, oc, oc], epi, comm=comm)


def _pl_forward(n4, wplg, p, wpl, h3, tm=1024):
    s, d = n4.shape
    g, kb, _ = wplg.shape
    kp, nb = wpl.shape[1], wpl.shape[2]
    tm = _tile(s, tm)
    wplg_nat = wplg.reshape(g * kb, d)

    def epi(accs, ex, out):
        t, pe = accs
        out[0][...] = ex[0][...] + _sig(t) * pe
        out[1][...] = t
        out[2][...] = pe.astype(BF)

    rmap = lambda j, i: (i, 0)
    cmap = lambda j, i: (i, j)
    return _mm("pl_forward", (g, s // tm),
               [(n4, (tm, d), rmap, wplg_nat, (g * kb, nb), lambda j, i: (0, j), "nn", 0, 0),
                (p, (tm, kp), rmap, wpl, (None, kp, nb), lambda j, i: (j, 0, 0), "nn", 1, 0)],
               [(h3, (tm, nb), cmap)],
               [((s, d), F32, (tm, nb), cmap), ((s, d), F32, (tm, nb), cmap), ((s, d), BF, (tm, nb), cmap)],
               epi, nacc=2)


def _row_dx(name, dout, w, tm=1024, comm=()):
    s, n = dout.shape
    g, kb, _ = w.shape
    tm = _tile(s, tm)
    return _mm(name, (g, s // tm),
               [(dout, (tm, n), lambda j, i: (i, 0), w, (None, kb, n), lambda j, i: (j, 0, 0), "nt", 0, 0)], [],
               [((s, g * kb), F32, (tm, kb), lambda j, i: (i, j))], _store, comm=comm)[0]


def _in_proj_bwd_x(pieces, weights, tm=512, tn=512, comm=()):
    s = pieces[0].shape[0]
    d = weights[0].shape[1]
    tm, tn = _tile(s, tm), _tile(d, tn)
    prods = [(pc, (tm, pc.shape[1]), lambda j, i: (i, 0), w, (pc.shape[1], tn), lambda j, i: (0, j), "nn", 0, 0)
             for pc, w in zip(pieces, weights)]
    return _mm("in_proj_dx", (d // tn, s // tm), prods, [],
               [((s, d), F32, (tm, tn), lambda j, i: (i, j))], _store, comm=comm)[0]


def _split_w_in(w_in_t):
    g, nb, d = w_in_t.shape
    nat = w_in_t.reshape(g * nb, d)
    na, lat = 3 * NA_HEADS * NA_DIM, 2 * ML_RANK + ML_ROPE
    return nat, nat[na:na + lat], nat[na + lat:na + lat + d], nat[na + lat + d:]


def _pair_sum(name, part, landed, core):
    _, _, r, c = part.shape
    tr, tc = _ew_tile(r, c)

    def body(core_ref, a_ref, b_ref, o_ref):
        o_ref[...] = (a_ref[...].astype(F32) + b_ref[...].astype(F32)).astype(o_ref.dtype)

    return pl.pallas_call(
        body, name=name,
        grid_spec=pltpu.PrefetchScalarGridSpec(
            num_scalar_prefetch=1, grid=(NCHIP, r // tr, c // tc),
            in_specs=[pl.BlockSpec((None, None, tr, tc), lambda j, i, k, core_ref: (j, core_ref[0], i, k)),
                      pl.BlockSpec((None, tr, tc), lambda j, i, k, core_ref: (j, i, k))],
            out_specs=pl.BlockSpec((None, tr, tc), lambda j, i, k, core_ref: (j, i, k))),
        out_shape=jax.ShapeDtypeStruct(landed.shape, landed.dtype), compiler_params=_params(3),
    )(core, part, landed)


def _device_step(x, p, target, sp, own, core):
    s, d = x.shape
    rows = s // GRID_W
    cos, sin, rot, rot_t = _rope_consts(s)
    w, dw4, sums, dsp, pending = {}, {}, {}, {}, []

    def gather(*names):
        return _GatherPart(names, [own[n] for n in names])

    def got(part):
        w.update(zip(part.names, part.results))

    def grad(name, g):
        dw4[name] = g.reshape((NCHIP, 2) + g.shape[1:])

    def to_sibling(*names):
        return _SiblingPart(names, [dw4[n] for n in names])

    def add_pairs(part):
        for n, landed in zip(part.names, part.results):
            sums[n] = _pair_sum("pair_sum_" + n, dw4[n], landed, core)

    def start_chips(tag, *names):
        send, recv, thru, lands, token = _chips_start("rs_start_" + tag, [sums[n] for n in names])
        pending.append((tag, names, send, recv, thru, lands))
        return token

    c0 = gather("ffn1_w_gate", "ffn1_w_up")
    c1 = gather("ffn1_w_down")
    c2 = gather("w_in")

    def ffn1_wgu():
        got(c0)
        return w["ffn1_w_gate"], w["ffn1_w_up"]

    def ffn1_wd():
        got(c1)
        return w["ffn1_w_down"]

    h1, ffn1_saved = _ffn_forward("ffn1", x, sp["ffn1_norm"], ffn1_wgu, ffn1_wd,
                                  norm_comm=[c0], up_comm=[c1], down_comm=[c2])
    got(c2)
    wqkv, wlat, wga, wgb = _split_w_in(w["w_in"])
    u = _rms_fwd("mix_norm", h1, sp["mix_norm"])
    c3 = gather("w_uq", "w_ukv")
    qkv = _mm_nt("in_qkv", u, wqkv, BF, tn=1024, comm=[c3], rows=3 * NA_HEADS * NA_DIM)
    got(c3)
    lat = _mm_nt("in_lat", u, wlat, F32, tm=1024)
    c3a = gather("w_branch_a")
    ga = _mm_nt("in_ga", u, wga, F32, tn=1024, comm=[c3a])
    got(c3a)
    c3b = gather("w_branch_b")
    gb = _mm_nt("in_gb", u, wgb, F32, tn=1024, comm=[c3b])
    got(c3b)
    tb = _na_table(sp["na_rpb"], rows)
    c4 = gather("ffn2_w_gate")
    oa = _na_fwd(qkv, tb, comm=[c4])
    got(c4)
    cq, ckv, kr = _mla_prep(lat, sp["q_a_norm"], sp["kv_a_norm"], cos, sin, rot)
    c4a = gather("w_out")
    qf = _mla_q_proj(cq, w["w_uq"], cos, sin, rot, comm=[c4a])
    got(c4a)
    kf, vf = _mla_kv_proj(ckv, w["w_ukv"], kr)
    c5 = gather("ffn2_w_up")
    ob = _mla_fwd(qf, kf, vf, comm=[c5])
    got(c5)
    c5a = gather("w_pl", "w_pl_gate")
    ya, yb, merged = _mix_merge(oa, ob, w["w_branch_a"], w["w_branch_b"], ga, gb, comm=[c5a])
    got(c5a)
    h2 = _mix_out(merged, w["w_out"], h1)
    c6 = gather("ffn2_w_down")

    def ffn2_wd():
        got(c6)
        return w["ffn2_w_down"]

    h3, ffn2_saved = _ffn_forward("ffn2", h2, sp["ffn2_norm"], lambda: (w["ffn2_w_gate"], w["ffn2_w_up"]), ffn2_wd,
                                  up_comm=[c6])
    n4 = _rms_fwd("pl_norm", h3, sp["pl_norm"])
    pb = p.astype(BF)
    h4, t, pe = _pl_forward(n4, w["w_pl_gate"], pb, w["w_pl"], h3)

    dh4, dsp["final_norm"], loss = _loss_head(h4, target, sp["final_norm"])
    dt, dpe = _pl_bwd_elem(dh4, pe, t)
    grad("w_pl", _grp_dw("pl_dw", pb, dpe))
    grad("w_pl_gate", _row_dw("plg_dw", n4, dt))
    s1 = to_sibling("w_pl", "w_pl_gate")
    dn4 = _row_dx("plg_dx", dt, w["w_pl_gate"], comm=[s1])
    add_pairs(s1)
    dh3, dhb, dsp["pl_norm"] = _rms_bwd("pl_dnorm", dn4, h3, sp["pl_norm"], dh4)

    xn, hg, hu, a = ffn2_saved
    grad("ffn2_w_down", _ffn_bwd_wd("ffn2_dwd", a, dhb))
    s2 = to_sibling("ffn2_w_down")
    dhg, dhu = _ffn_bwd_act("ffn2_dact", dhb, w["ffn2_w_down"], hg, hu, comm=[s2])
    add_pairs(s2)
    tok = start_chips("ffn2_down", "w_pl", "w_pl_gate", "ffn2_w_down")
    dwg, dwu = _ffn_bwd_wup("ffn2_dwup", xn, dhg, dhu, comm=[_After(tok)])
    grad("ffn2_w_gate", dwg)
    grad("ffn2_w_up", dwu)
    s3 = to_sibling("ffn2_w_gate", "ffn2_w_up")
    dxn = _ffn_bwd_x("ffn2_dx", dhg, dhu, w["ffn2_w_gate"], w["ffn2_w_up"], comm=[s3])
    add_pairs(s3)
    tok = start_chips("ffn2_up", "ffn2_w_gate", "ffn2_w_up")
    dh2, dh2b, dsp["ffn2_norm"] = _rms_bwd("ffn2_dnorm", dxn, h2, sp["ffn2_norm"], dh3, comm=[_After(tok)])

    grad("w_out", _row_dw("out_dw", merged, dh2b))
    s4 = to_sibling("w_out")
    dya, dyb, dga, dgb = _mix_out_bwd(dh2b, w["w_out"], ga, gb, ya, yb, comm=[s4])
    add_pairs(s4)
    grad("w_branch_a", _grp_dw("bra_dw", oa, dya))
    grad("w_branch_b", _grp_dw("brb_dw", ob, dyb))
    doa = _grp_dx("bra_dx", dya, w["w_branch_a"], out_dtype=BF)
    s5 = to_sibling("w_branch_a", "w_branch_b")
    dob = _grp_dx("brb_dx", dyb, w["w_branch_b"], out_dtype=BF, comm=[s5])
    add_pairs(s5)

    dqf, dkf, dvf = _mla_bwd(qf, kf, vf, dob)
    dqp, dkv, dkr = _mla_post(dqf, dkf, dvf, cos, sin, rot_t)
    grad("w_uq", _grp_dw_t("uq_dw", dqp, cq))
    grad("w_ukv", _grp_dw("ukv_dw", ckv, dkv))
    dcq = _grp_dx_t("uq_dx", dqp, w["w_uq"])
    s6 = to_sibling("w_uq", "w_ukv")
    dckv = _grp_dx("ukv_dx", dkv, w["w_ukv"], comm=[s6])
    add_pairs(s6)
    dlat, dsp["q_a_norm"], dsp["kv_a_norm"] = _mla_lat_bwd(dcq, dckv, dkr, lat, sp["q_a_norm"], sp["kv_a_norm"],
                                                         cos, sin, rot_t)
    dq_na, dk_na, dv_na, dtab = _na_bwd(qkv, tb, doa)
    dsp["na_rpb"] = _na_rpb_grad(dtab, rows)
    dqkv = jnp.concatenate([dq_na, dk_na.astype(BF), dv_na.astype(BF)], axis=1)

    pieces = [dqkv, dlat, dga, dgb]
    dwin = jnp.zeros((sum(pc.shape[1] for pc in pieces), d), BF)
    row0 = 0
    for i, pc in enumerate(pieces):
        dwin = _mm_tn_into("in_dw%d" % i, pc, u, dwin, row0)
        row0 += pc.shape[1]
    grad("w_in", dwin.reshape(NDEV, -1, d))
    s7 = to_sibling("w_in")
    du = _in_proj_bwd_x(pieces, [wqkv, wlat, wga, wgb], comm=[s7])
    add_pairs(s7)
    tok = start_chips("w_in", "w_out", "w_branch_a", "w_branch_b", "w_uq", "w_ukv", "w_in")
    dh1, dhb, dsp["mix_norm"] = _rms_bwd("mix_dnorm", du, h1, sp["mix_norm"], dh2, comm=[_After(tok)])

    xn, hg, hu, a = ffn1_saved
    dhg, dhu = _ffn_bwd_act("ffn1_dact", dhb, w["ffn1_w_down"], hg, hu)
    dwg, dwu = _ffn_bwd_wup("ffn1_dwup", xn, dhg, dhu)
    grad("ffn1_w_gate", dwg)
    grad("ffn1_w_up", dwu)
    s8 = to_sibling("ffn1_w_gate", "ffn1_w_up")
    grad("ffn1_w_down", _ffn_bwd_wd("ffn1_dwd", a, dhb, comm=[s8]))
    add_pairs(s8)
    tok = start_chips("ffn1_up", "ffn1_w_gate", "ffn1_w_up")
    s9 = to_sibling("ffn1_w_down")
    dxn = _ffn_bwd_x("ffn1_dx", dhg, dhu, w["ffn1_w_gate"], w["ffn1_w_up"], comm=[s9, _After(tok)])
    add_pairs(s9)
    tok = start_chips("ffn1_down", "ffn1_w_down")
    dx, _, dsp["ffn1_norm"] = _rms_bwd("ffn1_dnorm", dxn, x, sp["ffn1_norm"], dh1, comm=[_After(tok)])
    return loss, dx, pending, dsp


def _small_peers():
    x, y, c = _coords()
    return [(x ^ ((k >> 2) & 1), y ^ ((k >> 1) & 1), c ^ (k & 1)) for k in range(1, NDEV)]


def _small_start(buf):
    def body(b_ref, z_ref, send_sems, recv_sems, b_thru, z_thru, token):
        x, y, c = _coords()
        for k, peer in enumerate(_small_peers()):
            pltpu.make_async_remote_copy(
                src_ref=b_ref, dst_ref=z_ref.at[4 * x + 2 * y + c], send_sem=send_sems.at[k],
                recv_sem=recv_sems.at[k], device_id=peer, device_id_type=MESH).start()
        token[...] = jnp.zeros_like(token)

    zone = lax.empty((NDEV,) + buf.shape, buf.dtype)
    res = pl.pallas_call(
        body, name="small_start", in_specs=[HBM, HBM],
        out_specs=(SEM, SEM, HBM, HBM, pl.BlockSpec(memory_space=pltpu.VMEM)),
        out_shape=(pltpu.SemaphoreType.DMA((NDEV - 1,)), pltpu.SemaphoreType.DMA((NDEV - 1,)),
                   pltpu.HBM(buf.shape, buf.dtype), pltpu.HBM(zone.shape, zone.dtype),
                   jax.ShapeDtypeStruct((8, 128), F32)),
        input_output_aliases={0: 2, 1: 3},
        compiler_params=pltpu.CompilerParams(has_side_effects=pltpu.SideEffectType.DATAFLOW_SIDE_EFFECTING),
    )(pltpu.with_memory_space_constraint(buf, pltpu.HBM), pltpu.with_memory_space_constraint(zone, pltpu.HBM))
    return res


def _small_wait(send_sems, recv_sems, buf, zone, after):
    def body(b_ref, z_ref, send, recv, after_ref, b_out, z_out):
        for k, (px, py, pc) in enumerate(_small_peers()):
            cp = pltpu.make_async_remote_copy(
                src_ref=b_ref, dst_ref=z_ref.at[4 * px + 2 * py + pc], send_sem=send.at[k], recv_sem=recv.at[k],
                device_id=(px, py, pc), device_id_type=MESH)
            cp.wait_send()
            cp.wait_recv()

    return pl.pallas_call(
        body, name="small_wait", in_specs=[HBM, HBM, SEM, SEM, ANY], out_specs=[HBM, HBM],
        out_shape=[pltpu.HBM(buf.shape, buf.dtype), pltpu.HBM(zone.shape, zone.dtype)],
        input_output_aliases={0: 0, 1: 1},
        compiler_params=pltpu.CompilerParams(has_side_effects=pltpu.SideEffectType.DATAFLOW_SIDE_EFFECTING),
    )(buf, zone, send_sems, recv_sems, after)


def _adam_math(wv, g, m, v):
    m_new = B1 * m + (1.0 - B1) * g
    v_new = B2 * v + (1.0 - B2) * (g * g)
    m_hat = m_new / (1.0 - B1 ** STEP)
    v_hat = v_new / (1.0 - B2 ** STEP)
    return -LR * (m_hat / (jnp.sqrt(v_hat) + ADAM_EPS) + WD * wv), m_new, v_new


def _adam_replicated(name, zone, own, wv, m, v, me):
    ndev, r, c = zone.shape

    def body(me_ref, z_ref, o_ref, w_ref, m_ref, v_ref, g_ref, d_ref, mo_ref, vo_ref):
        g = jnp.zeros((r, c), F32)
        for d in range(ndev):
            g = g + jnp.where(me_ref[0] == d, o_ref[...], z_ref[d])
        g_ref[...] = g
        d_ref[...], mo_ref[...], vo_ref[...] = _adam_math(w_ref[...], g, m_ref[...], v_ref[...])

    blk = pl.BlockSpec((r, c), lambda i, me_ref: (0, 0))
    return pl.pallas_call(
        body, name=name,
        grid_spec=pltpu.PrefetchScalarGridSpec(
            num_scalar_prefetch=1, grid=(1,),
            in_specs=[pl.BlockSpec((ndev, r, c), lambda i, me_ref: (0, 0, 0)), blk, blk, blk, blk],
            out_specs=[blk] * 4),
        out_shape=[jax.ShapeDtypeStruct((r, c), F32)] * 4, compiler_params=_params(1),
    )(me, zone, own, wv, m, v)


def _adam_exchanged(name, sums, land, wv, m, v, my_chip):
    _, r, c = sums.shape
    tr, tc = _ew_tile(r, c)

    def body(chip_ref, s_ref, l_ref, w_ref, m_ref, v_ref, g_ref, d_ref, mo_ref, vo_ref):
        g = s_ref[...].astype(F32)
        for j in range(3):
            g = g + l_ref[j].astype(F32)
        g_ref[...] = g
        d_ref[...], mo_ref[...], vo_ref[...] = _adam_math(w_ref[...], g, m_ref[...], v_ref[...])

    blk = pl.BlockSpec((tr, tc), lambda i, k, chip_ref: (i, k))
    return pl.pallas_call(
        body, name=name,
        grid_spec=pltpu.PrefetchScalarGridSpec(
            num_scalar_prefetch=1, grid=(r // tr, c // tc),
            in_specs=[pl.BlockSpec((None, tr, tc), lambda i, k, chip_ref: (chip_ref[0], i, k)),
                      pl.BlockSpec((3, tr, tc), lambda i, k, chip_ref: (0, i, k)), blk, blk, blk],
            out_specs=[blk] * 4),
        out_shape=[jax.ShapeDtypeStruct((r, c), F32)] * 4, compiler_params=_params(2),
    )(my_chip, sums, land, wv, m, v)


SHARDED = ("ffn1_w_gate", "ffn1_w_up", "ffn1_w_down", "w_in", "w_uq", "w_ukv", "w_branch_a", "w_branch_b", "w_out",
           "ffn2_w_gate", "ffn2_w_up", "ffn2_w_down", "w_pl", "w_pl_gate")
TRANSPOSED = ("ffn1_w_gate", "ffn1_w_up", "ffn2_w_gate", "ffn2_w_up", "w_in", "w_uq")
REPLICATED = ("ffn1_norm", "mix_norm", "q_a_norm", "kv_a_norm", "na_rpb", "ffn2_norm", "pl_norm", "final_norm")
WEIGHTS = ("ffn1_norm", "ffn1_w_gate", "ffn1_w_up", "ffn1_w_down", "mix_norm", "w_in", "q_a_norm", "w_uq",
           "kv_a_norm", "w_ukv", "na_rpb", "w_branch_a", "w_branch_b", "w_out", "ffn2_norm", "ffn2_w_gate",
           "ffn2_w_up", "ffn2_w_down", "pl_norm", "w_pl", "w_pl_gate", "final_norm")
SMALL_W = 2048


def _pack_small(vals):
    rows = []
    for name in REPLICATED:
        flat = vals[name].reshape(-1).astype(F32)
        n = -(-flat.shape[0] // SMALL_W) * SMALL_W
        rows.append(jnp.pad(flat, (0, n - flat.shape[0])).reshape(-1, SMALL_W))
    return jnp.concatenate(rows, axis=0)


def _unpack_small(buf, shapes):
    out, r = {}, 0
    for name in REPLICATED:
        size = int(np.prod(shapes[name]))
        nrow = -(-size // SMALL_W)
        out[name] = buf[r:r + nrow].reshape(-1)[:size].reshape(shapes[name])
        r += nrow
    return out


def kernel(x, p, ffn1_norm, ffn1_w_gate, ffn1_w_up, ffn1_w_down, mix_norm, w_in, q_a_norm, w_uq, kv_a_norm, w_ukv, na_rpb, w_branch_a, w_branch_b, w_out, ffn2_norm, ffn2_w_gate, ffn2_w_up, ffn2_w_down, pl_norm, w_pl, w_pl_gate, final_norm, loss_target, m_ffn1_norm, m_ffn1_w_gate, m_ffn1_w_up, m_ffn1_w_down, m_mix_norm, m_w_in, m_q_a_norm, m_w_uq, m_kv_a_norm, m_w_ukv, m_na_rpb, m_w_branch_a, m_w_branch_b, m_w_out, m_ffn2_norm, m_ffn2_w_gate, m_ffn2_w_up, m_ffn2_w_down, m_pl_norm, m_w_pl, m_w_pl_gate, m_final_norm, v_ffn1_norm, v_ffn1_w_gate, v_ffn1_w_up, v_ffn1_w_down, v_mix_norm, v_w_in, v_q_a_norm, v_w_uq, v_kv_a_norm, v_w_ukv, v_na_rpb, v_w_branch_a, v_w_branch_b, v_w_out, v_ffn2_norm, v_ffn2_w_gate, v_ffn2_w_up, v_ffn2_w_down, v_pl_norm, v_w_pl, v_w_pl_gate, v_final_norm):
    args = dict(locals())
    wts = {n: args[n] for n in WEIGHTS}
    mom = {n: args["m_" + n] for n in WEIGHTS}
    var = {n: args["v_" + n] for n in WEIGHTS}
    shapes = {n: wts[n].shape for n in WEIGHTS}
    core = lax.axis_index("c").astype(jnp.int32).reshape(1)

    local = lambda n, a: a[0].T if n in TRANSPOSED else a[0]
    own = {n: local(n, wts[n]).astype(BF) for n in SHARDED}
    sp = {n: wts[n].reshape(1, -1) for n in REPLICATED if n != "na_rpb"}
    sp["na_rpb"] = wts["na_rpb"][0]
    loss_part, grad_x, pending, dsp = _device_step(x[0], p[0, 0], loss_target[0], sp, own, core)

    small = jnp.concatenate([_pack_small(dsp), jnp.pad(loss_part, ((0, 0), (0, SMALL_W - loss_part.shape[1])))], 0)
    pad_rows = -small.shape[0] % 8
    small = jnp.pad(small, ((0, pad_rows), (0, 0)))
    s_send, s_recv, s_buf, s_zone, last = _small_start(small)

    out = {}
    my_chip = (2 * lax.axis_index("x") + lax.axis_index("y")).astype(jnp.int32).reshape(1)
    for tag, names, send, recv, thru, lands in pending:
        thru, lands = _chips_wait("rs_wait_" + tag, send, recv, thru, lands, last)
        for n, s4, l3 in zip(names, thru, lands):
            res4 = _adam_exchanged("adam_" + n, s4, l3, local(n, wts[n]), local(n, mom[n]), local(n, var[n]), my_chip)
            out[n] = tuple((a.T if n in TRANSPOSED else a)[None] for a in res4)
            last = res4[1]

    s_buf, s_zone = _small_wait(s_send, s_recv, s_buf, s_zone, last)
    zeros = jnp.zeros((1 + pad_rows, SMALL_W), F32)
    pack = lambda d: jnp.concatenate([_pack_small(d), zeros], 0)
    me = 2 * my_chip + core
    g_s, d_s, m_s, v_s = _adam_replicated("adam_small", s_zone, s_buf, pack(wts), pack(mom), pack(var), me)
    n_rows = small.shape[0] - 1 - pad_rows
    loss = g_s[n_rows, 0]
    small_out = [_unpack_small(b, shapes) for b in (g_s, d_s, m_s, v_s)]
    for n in REPLICATED:
        out[n] = tuple(b[n] for b in small_out)

    res = [loss, grad_x[None]]
    for k in range(4):
        res += [out[n][k] for n in WEIGHTS]
    return tuple(res)
```
